```python
import math, functools
import jax, jax.numpy as jnp
from jax import lax
import numpy as np

D_MODEL = 1024
BATCH = 4
SEQ = 4096
DEPTH = 1
DEC_BATCH = 32
DEC_SEQ = 64
PAST_LEN = 1024

CHUNK = 64
D_MIX = D_MODEL
ATTN_WIDTH = D_MIX // 2
HEAD_DIM = 64
N_HEADS = ATTN_WIDTH // HEAD_DIM
N_KV_HEADS = 2
Q_PER_KV = N_HEADS // N_KV_HEADS
WINDOW = 128
WIN_CHUNKS = WINDOW // CHUNK
SSM_WIDTH = D_MIX - ATTN_WIDTH
SSM_GROUP = 16
SSM_GROUPS = SSM_WIDTH // SSM_GROUP
SSM_STATE = 64
D_FF = -(-8 * D_MODEL // (3 * 256)) * 256
PROJ_Q = N_HEADS * HEAD_DIM
PROJ_KV = N_KV_HEADS * HEAD_DIM
D_IN_PROJ = PROJ_Q + 2 * PROJ_KV + SSM_WIDTH
DN_ALPHA = (2.0 * DEPTH) ** 0.25
DN_BETA = (8.0 * DEPTH) ** -0.25
LN_EPS = 1e-5
NEG_INF = -1e30

kernel_name = "hymba_swa_sink_s5_deepnorm_stream_step"


def layer_norm(x, g, b):
    xf = x.astype(jnp.float32)
    mu = jnp.mean(xf, -1, keepdims=True)
    var = jnp.mean(jnp.square(xf - mu), -1, keepdims=True)
    return ((xf - mu) * lax.rsqrt(var + LN_EPS) * g.astype(jnp.float32) + b.astype(jnp.float32)).astype(x.dtype)


def sink_attention(q, k, v, sinks, mask):
    s = jnp.einsum('...qkgd,...skd->...kgqs', q, k).astype(jnp.float32) * (HEAD_DIM ** -0.5)
    if mask is not None:
        s = jnp.where(mask, s, NEG_INF)
    sink = sinks.astype(jnp.float32).reshape(N_KV_HEADS, Q_PER_KV, 1, 1)
    m = jnp.maximum(jnp.max(s, -1, keepdims=True), sink)
    p = jnp.exp(s - m)
    w = p / (jnp.sum(p, -1, keepdims=True) + jnp.exp(sink - m))
    return jnp.einsum('...kgqs,...skd->...qkgd', w.astype(v.dtype), v)


def prompt_window_attention(q, k, v, sinks, win_rows):
    B, L = q.shape[:2]
    nc = L // CHUNK
    qb = q.reshape(B, nc, CHUNK, N_KV_HEADS, Q_PER_KV, HEAD_DIM)
    pad = ((0, 0), (WIN_CHUNKS, 0), (0, 0), (0, 0), (0, 0))
    kp = jnp.pad(k.reshape(B, nc, CHUNK, N_KV_HEADS, HEAD_DIM), pad)
    vp = jnp.pad(v.reshape(B, nc, CHUNK, N_KV_HEADS, HEAD_DIM), pad)
    kband = jnp.concatenate([kp[:, j:j + nc] for j in range(WIN_CHUNKS + 1)], axis=2)
    vband = jnp.concatenate([vp[:, j:j + nc] for j in range(WIN_CHUNKS + 1)], axis=2)
    key_chunk = (jnp.arange(nc)[:, None] - WIN_CHUNKS
                 + jnp.repeat(jnp.arange(WIN_CHUNKS + 1), CHUNK)[None, :])
    mask = (key_chunk >= 0)[:, None, None, None, :]
    out = sink_attention(qb, kband, vband, sinks, mask)
    return out.reshape(B, L, PROJ_Q), k[:, -win_rows:], v[:, -win_rows:]


def sample_window_attention(q, k, v, sinks, cache_k, cache_v):
    B, S = q.shape[:2]
    k_all = jnp.concatenate([cache_k.astype(k.dtype), k], axis=1)
    v_all = jnp.concatenate([cache_v.astype(v.dtype), v], axis=1)
    out = sink_attention(q.reshape(B, S, N_KV_HEADS, Q_PER_KV, HEAD_DIM), k_all, v_all, sinks, None)
    win_rows = cache_k.shape[1]
    return out.reshape(B, S, PROJ_Q), k_all[:, -win_rows:], v_all[:, -win_rows:]


def s5_mixer(u, h0_re, h0_im, lam_re, lam_im, log_step, b_re, b_im, c_re, c_im, d_skip, w_glu, b_glu):
    Bn, L = u.shape[:2]
    f32 = jnp.float32
    dt = jnp.exp(log_step.astype(f32))[:, None]
    lr, li = lam_re.astype(f32), lam_im.astype(f32)
    mag = jnp.exp(lr * dt)
    ab_re, ab_im = mag * jnp.cos(li * dt), mag * jnp.sin(li * dt)
    nr, ni = ab_re - 1.0, ab_im
    den = lr * lr + li * li
    f_re, f_im = (nr * lr + ni * li) / den, (ni * lr - nr * li) / den
    br, bi = b_re.astype(f32), b_im.astype(f32)
    bb_re = f_re[..., None] * br - f_im[..., None] * bi
    bb_im = f_re[..., None] * bi + f_im[..., None] * br
    ug = u.astype(f32).reshape(Bn, L, SSM_GROUPS, SSM_GROUP)
    bu_re = jnp.einsum('blgc,gpc->blgp', ug, bb_re)
    bu_im = jnp.einsum('blgc,gpc->blgp', ug, bb_im)
    h0r, h0i = h0_re.astype(f32), h0_im.astype(f32)
    bu_re = bu_re.at[:, 0].add(ab_re * h0r - ab_im * h0i)
    bu_im = bu_im.at[:, 0].add(ab_re * h0i + ab_im * h0r)
    a_re = jnp.broadcast_to(ab_re, bu_re.shape)
    a_im = jnp.broadcast_to(ab_im, bu_im.shape)

    def combine(e1, e2):
        a1r, a1i, b1r, b1i = e1
        a2r, a2i, b2r, b2i = e2
        return (a2r * a1r - a2i * a1i, a2r * a1i + a2i * a1r,
                a2r * b1r - a2i * b1i + b2r, a2r * b1i + a2i * b1r + b2i)

    _, _, hr, hi = lax.associative_scan(combine, (a_re, a_im, bu_re, bu_im), axis=1)
    y = (jnp.einsum('blgp,gcp->blgc', hr, c_re.astype(f32))
         - jnp.einsum('blgp,gcp->blgc', hi, c_im.astype(f32)))
    y = y.reshape(Bn, L, SSM_WIDTH) + d_skip.astype(f32) * u.astype(f32)
    y = jax.nn.gelu(y, approximate=False)
    y = y * jax.nn.sigmoid(y @ w_glu.astype(f32) + b_glu.astype(f32))
    return y.astype(u.dtype), hr[:, -1], hi[:, -1]


def trunk_layer(x, attn, h0_re, h0_im, w_in, lam_re, lam_im, log_step, b_re, b_im, c_re, c_im,
                d_skip, w_glu, b_glu, w_out, ln1_g, ln1_b, w_gate_up, w_down, ln2_g, ln2_b):
    B, L, _ = x.shape
    proj = x @ w_in
    q, k, v, u = jnp.split(proj, [PROJ_Q, PROJ_Q + PROJ_KV, PROJ_Q + 2 * PROJ_KV], axis=-1)
    q = q.reshape(B, L, N_HEADS, HEAD_DIM)
    k = k.reshape(B, L, N_KV_HEADS, HEAD_DIM)
    v = v.reshape(B, L, N_KV_HEADS, HEAD_DIM)
    a, new_k, new_v = attn(q, k, v)
    s, new_re, new_im = s5_mixer(u, h0_re, h0_im, lam_re, lam_im, log_step, b_re, b_im,
                                 c_re, c_im, d_skip, w_glu, b_glu)
    mix = jnp.concatenate([a, s.astype(a.dtype)], axis=-1) @ w_out
    h = layer_norm(DN_ALPHA * x + mix, ln1_g, ln1_b)
    g, up = jnp.split(h @ w_gate_up, 2, axis=-1)
    f = (jax.nn.silu(g) * up) @ w_down
    y = layer_norm(DN_ALPHA * h + f, ln2_g, ln2_b)
    return y, new_k, new_v, new_re, new_im


def setup_inputs(seed: int = 0) -> dict:
    key = jax.random.key(seed)
    ks = jax.random.split(key, 32)
    nrm = lambda k, shape, scale: scale * jax.random.normal(k, shape, jnp.float32)
    win_rows = min(WINDOW, PAST_LEN)
    n_idx = jnp.arange(SSM_STATE, dtype=jnp.float32)
    return {
        'x_prompt': nrm(ks[0], (BATCH, SEQ, D_MODEL), 1.0),
        'x_sample': nrm(ks[1], (DEC_BATCH, DEC_SEQ, D_MODEL), 1.0),
        'cache_win_k': nrm(ks[2], (DEPTH, DEC_BATCH, win_rows, N_KV_HEADS, HEAD_DIM), 1.0),
        'cache_win_v': nrm(ks[3], (DEPTH, DEC_BATCH, win_rows, N_KV_HEADS, HEAD_DIM), 1.0),
        'state_ssm_re': nrm(ks[4], (DEPTH, DEC_BATCH, SSM_GROUPS, SSM_STATE), 0.3),
        'state_ssm_im': nrm(ks[5], (DEPTH, DEC_BATCH, SSM_GROUPS, SSM_STATE), 0.3),
        'ln_in_g': 1.0 + nrm(ks[6], (D_MODEL,), 0.02),
        'ln_in_b': nrm(ks[7], (D_MODEL,), 0.02),
        'w_in': nrm(ks[8], (DEPTH, D_MODEL, D_IN_PROJ), D_MODEL ** -0.5),
        'attn_sinks': nrm(ks[9], (DEPTH, N_HEADS), 0.5),
        'ssm_lambda_re': -0.5 + nrm(ks[10], (DEPTH, SSM_GROUPS, SSM_STATE), 0.01),
        'ssm_lambda_im': math.pi * n_idx + nrm(ks[11], (DEPTH, SSM_GROUPS, SSM_STATE), 0.01),
        'ssm_log_step': jax.random.uniform(ks[12], (DEPTH, SSM_GROUPS), jnp.float32,
                                           math.log(1e-3), math.log(1e-1)),
        'ssm_b_re': nrm(ks[13], (DEPTH, SSM_GROUPS, SSM_STATE, SSM_GROUP), (2 * SSM_GROUP) ** -0.5),
        'ssm_b_im': nrm(ks[14], (DEPTH, SSM_GROUPS, SSM_STATE, SSM_GROUP), (2 * SSM_GROUP) ** -0.5),
        'ssm_c_re': nrm(ks[15], (DEPTH, SSM_GROUPS, SSM_GROUP, SSM_STATE), (2 * SSM_STATE) ** -0.5),
        'ssm_c_im': nrm(ks[16], (DEPTH, SSM_GROUPS, SSM_GROUP, SSM_STATE), (2 * SSM_STATE) ** -0.5),
        'ssm_d': nrm(ks[17], (DEPTH, SSM_WIDTH), 1.0),
        'w_glu': nrm(ks[18], (DEPTH, SSM_WIDTH, SSM_WIDTH), SSM_WIDTH ** -0.5),
        'b_glu': nrm(ks[19], (DEPTH, SSM_WIDTH), 0.02),
        'w_out': nrm(ks[20], (DEPTH, D_MIX, D_MODEL), DN_BETA * D_MIX ** -0.5),
        'ln1_g': 1.0 + nrm(ks[21], (DEPTH, D_MODEL), 0.02),
        'ln1_b': nrm(ks[22], (DEPTH, D_MODEL), 0.02),
        'w_gate_up': nrm(ks[23], (DEPTH, D_MODEL, 2 * D_FF), D_MODEL ** -0.5),
        'w_down': nrm(ks[24], (DEPTH, D_FF, D_MODEL), DN_BETA * D_FF ** -0.5),
        'ln2_g': 1.0 + nrm(ks[25], (DEPTH, D_MODEL), 0.02),
        'ln2_b': nrm(ks[26], (DEPTH, D_MODEL), 0.02),
    }


def reference(x_prompt, x_sample, cache_win_k, cache_win_v, state_ssm_re, state_ssm_im,
              ln_in_g, ln_in_b, w_in, attn_sinks, ssm_lambda_re, ssm_lambda_im, ssm_log_step,
              ssm_b_re, ssm_b_im, ssm_c_re, ssm_c_im, ssm_d, w_glu, b_glu, w_out,
              ln1_g, ln1_b, w_gate_up, w_down, ln2_g, ln2_b):
    win_rows = cache_win_k.shape[2]
    xp = layer_norm(x_prompt, ln_in_g, ln_in_b)
    xs = layer_norm(x_sample, ln_in_g, ln_in_b)
    kp_l, vp_l, rp_l, ip_l, ks_l, vs_l, rs_l, is_l = [], [], [], [], [], [], [], []
    for l in range(DEPTH):
        shared = (w_in[l], ssm_lambda_re[l], ssm_lambda_im[l], ssm_log_step[l], ssm_b_re[l], ssm_b_im[l],
                  ssm_c_re[l], ssm_c_im[l], ssm_d[l], w_glu[l], b_glu[l], w_out[l],
                  ln1_g[l], ln1_b[l], w_gate_up[l], w_down[l], ln2_g[l], ln2_b[l])
        attn_p = functools.partial(prompt_window_attention, sinks=attn_sinks[l], win_rows=win_rows)
        h0 = jnp.zeros((xp.shape[0], SSM_GROUPS, SSM_STATE), jnp.float32)
        xp, k1, v1, r1, i1 = trunk_layer(xp, attn_p, h0, h0, *shared)
        attn_s = functools.partial(sample_window_attention, sinks=attn_sinks[l],
                                   cache_k=cache_win_k[l], cache_v=cache_win_v[l])
        xs, k2, v2, r2, i2 = trunk_layer(xs, attn_s, state_ssm_re[l], state_ssm_im[l], *shared)
        kp_l.append(k1); vp_l.append(v1); rp_l.append(r1); ip_l.append(i1)
        ks_l.append(k2); vs_l.append(v2); rs_l.append(r2); is_l.append(i2)
    return (xp, xs,
            jnp.stack(kp_l), jnp.stack(vp_l), jnp.stack(rp_l), jnp.stack(ip_l),
            jnp.stack(ks_l), jnp.stack(vs_l), jnp.stack(rs_l), jnp.stack(is_l))
```

```python
import functools
import math

import jax
import jax.numpy as jnp
from jax import lax
from jax.experimental import pallas as pl
from jax.experimental.pallas import tpu as pltpu

F32 = jnp.float32
BF16 = jnp.bfloat16

D_MODEL = 1024
HEAD_DIM = 64
N_HEADS = 8
N_KV_HEADS = 2
Q_PER_KV = N_HEADS // N_KV_HEADS
CHUNK = 64
WINDOW = 128
WIN_CHUNKS = WINDOW // CHUNK
BAND = (WIN_CHUNKS + 1) * CHUNK
PROJ_Q = N_HEADS * HEAD_DIM
PROJ_KV = N_KV_HEADS * HEAD_DIM
SSM_WIDTH = 512
SSM_GROUP = 16
SSM_GROUPS = SSM_WIDTH // SSM_GROUP
SSM_STATE = 64
STATE_LANES = 2 * SSM_STATE
D_FF = 2816
D_IN_PROJ = PROJ_Q + 2 * PROJ_KV + SSM_WIDTH
LN_EPS = 1e-5
NEG_INF = -1e30

SSM_T = CHUNK
SSM_CW = SSM_T * SSM_GROUP
PROMPT_ROWS = 8
FF_BLOCK = 256
V7X_VMEM_LIMIT_BYTES = 56 * 1024 * 1024


def _layer_norm(x, g, b):
    mu = jnp.mean(x, axis=-1, keepdims=True)
    xc = x - mu
    var = jnp.mean(xc * xc, axis=-1, keepdims=True)
    return xc * lax.rsqrt(var + LN_EPS) * g + b


def _proj_kernel(x_ref, g_ref, b_ref, w_ref, q_ref, k_ref, v_ref, u_ref):
    xn = _layer_norm(x_ref[...], g_ref[...], b_ref[...])
    p = jnp.dot(xn.astype(BF16), w_ref[...], preferred_element_type=F32)
    q_ref[...] = (p[:, :PROJ_Q] * (HEAD_DIM ** -0.5)).astype(BF16)
    k_ref[...] = p[:, PROJ_Q:PROJ_Q + PROJ_KV]
    v_ref[...] = p[:, PROJ_Q + PROJ_KV:PROJ_Q + 2 * PROJ_KV]
    u_ref[...] = p[:, PROJ_Q + 2 * PROJ_KV:]


def _proj(x2d, ln_g, ln_b, w_in_bf16, tm):
    n = x2d.shape[0]
    const = lambda i: (0, 0)
    row = lambda i: (i, 0)
    return pl.pallas_call(
        _proj_kernel,
        grid=(n // tm,),
        in_specs=[
            pl.BlockSpec((tm, D_MODEL), row),
            pl.BlockSpec((1, D_MODEL), const),
            pl.BlockSpec((1, D_MODEL), const),
            pl.BlockSpec((D_MODEL, D_IN_PROJ), const),
        ],
        out_specs=[
            pl.BlockSpec((tm, PROJ_Q), row),
            pl.BlockSpec((tm, PROJ_KV), row),
            pl.BlockSpec((tm, PROJ_KV), row),
            pl.BlockSpec((tm, SSM_WIDTH), row),
        ],
        out_shape=[
            jax.ShapeDtypeStruct((n, PROJ_Q), BF16),
            jax.ShapeDtypeStruct((n, PROJ_KV), F32),
            jax.ShapeDtypeStruct((n, PROJ_KV), F32),
            jax.ShapeDtypeStruct((n, SSM_WIDTH), F32),
        ],
        compiler_params=pltpu.CompilerParams(dimension_semantics=("arbitrary",)),
        name="proj",
    )(x2d, ln_g, ln_b, w_in_bf16)


def _attn_kernel(sink_ref, q_ref, kp_ref, kc_ref, vp_ref, vc_ref, o_ref, *, n_chunks, masked):
    tq = n_chunks * CHUNK
    kk = jnp.concatenate([kp_ref[...], kc_ref[...]], axis=0).astype(BF16)
    vv = jnp.concatenate([vp_ref[...], vc_ref[...]], axis=0).astype(BF16)
    first_key = pl.program_id(1) * tq - WINDOW
    for j in range(n_chunks):
        qc = q_ref[j * CHUNK:(j + 1) * CHUNK, :]
        kj = kk[j * CHUNK:j * CHUNK + BAND]
        vj = vv[j * CHUNK:j * CHUNK + BAND]
        if masked:
            key_pos = lax.broadcasted_iota(jnp.int32, (1, BAND), 1) + (first_key + j * CHUNK)
            valid = key_pos >= 0
        heads = []
        for h in range(N_KV_HEADS):
            kh = kj[:, h * HEAD_DIM:(h + 1) * HEAD_DIM]
            vh = vj[:, h * HEAD_DIM:(h + 1) * HEAD_DIM]
            qs = jnp.concatenate(
                [qc[:, (h * Q_PER_KV + g) * HEAD_DIM:(h * Q_PER_KV + g + 1) * HEAD_DIM]
                 for g in range(Q_PER_KV)], axis=0)
            s = lax.dot_general(qs, kh, (((1,), (1,)), ((), ())), preferred_element_type=F32)
            if masked:
                s = jnp.where(valid, s, NEG_INF)
            sink = sink_ref[h]
            m = jnp.maximum(jnp.max(s, axis=-1, keepdims=True), sink)
            p = jnp.exp(s - m)
            den = jnp.sum(p, axis=-1, keepdims=True) + jnp.exp(sink - m)
            w = (p / den).astype(BF16)
            o = jnp.dot(w, vh, preferred_element_type=F32)
            heads.extend(o[g * CHUNK:(g + 1) * CHUNK] for g in range(Q_PER_KV))
        o_ref[j * CHUNK:(j + 1) * CHUNK, :] = jnp.concatenate(heads, axis=-1).astype(BF16)


def _attention(sink_rows, q, k_prev, k_cur, v_prev, v_cur, *, n_chunks, masked, prev_map):
    bsz, seq, _ = q.shape
    tq = n_chunks * CHUNK
    cur = lambda b, i: (b, i, 0)
    kernel = functools.partial(_attn_kernel, n_chunks=n_chunks, masked=masked)
    return pl.pallas_call(
        kernel,
        grid=(bsz, seq // tq),
        in_specs=[
            pl.BlockSpec((N_KV_HEADS, Q_PER_KV * CHUNK, 1), lambda b, i: (0, 0, 0)),
            pl.BlockSpec((None, tq, PROJ_Q), cur),
            pl.BlockSpec((None, WINDOW, PROJ_KV), prev_map),
            pl.BlockSpec((None, tq, PROJ_KV), cur),
            pl.BlockSpec((None, WINDOW, PROJ_KV), prev_map),
            pl.BlockSpec((None, tq, PROJ_KV), cur),
        ],
        out_specs=pl.BlockSpec((None, tq, PROJ_Q), cur),
        out_shape=jax.ShapeDtypeStruct((bsz, seq, PROJ_Q), BF16),
        compiler_params=pltpu.CompilerParams(dimension_semantics=("arbitrary", "arbitrary")),
        name="attn",
    )(sink_rows, q, k_prev, k_cur, v_prev, v_cur)


def _power_table(tau, nbits, a_re, a_im):
    rows = tau.shape[0]
    w_re = jnp.ones((rows, STATE_LANES), F32)
    w_im = jnp.zeros((rows, STATE_LANES), F32)
    p_re, p_im = a_re, a_im
    for k in range(nbits):
        bit = ((tau >> k) & 1) == 1
        f_re = jnp.where(bit, p_re, 1.0)
        f_im = jnp.where(bit, p_im, 0.0)
        w_re, w_im = w_re * f_re - w_im * f_im, w_re * f_im + w_im * f_re
        p_re, p_im = p_re * p_re - p_im * p_im, 2.0 * p_re * p_im
    return w_re, w_im


def _tables_kernel(lr_ref, li_ref, ls_ref, bre_ref, bim_ref, cre_ref, cim_ref,
                   m_ref, p_ref, qt_ref, coef_ref):
    lo = lax.broadcasted_iota(jnp.int32, (1, STATE_LANES), 1) < SSM_STATE
    lr, li = lr_ref[...], li_ref[...]
    dt = jnp.exp(ls_ref[...])
    mag = jnp.exp(lr * dt)
    a_re, a_im = mag * jnp.cos(li * dt), mag * jnp.sin(li * dt)
    nr, ni = a_re - 1.0, a_im
    den = lr * lr + li * li
    f_re, f_im = (nr * lr + ni * li) / den, (ni * lr - nr * li) / den
    b_re, b_im = bre_ref[...], bim_ref[...]
    bb_re = f_re * b_re - f_im * b_im
    bb_im = f_re * b_im + f_im * b_re
    c_re, c_im = cre_ref[...], cim_ref[...]

    tau = lax.broadcasted_iota(jnp.int32, (SSM_T, 1), 0)
    w_re, w_im = _power_table(tau, 6, a_re, a_im)
    w1_re, w1_im = _power_table(tau + 1, 7, a_re, a_im)
    wr_re, wr_im = _power_table(SSM_T - 1 - tau, 6, a_re, a_im)

    def outer(w, c):
        return (w[:, None, :] * c[None, :, :]).reshape(SSM_CW, STATE_LANES)

    cw_mix = (outer(jnp.where(lo, w_re, w_im), c_re) + outer(jnp.where(lo, -w_im, w_re), c_im))
    bb_mix = jnp.where(lo, bb_re, -bb_im)
    strip = lax.dot_general(bb_mix, cw_mix, (((1,), (1,)), ((), ())),
                            precision=lax.Precision.HIGHEST,
                            preferred_element_type=F32)
    col = lax.broadcasted_iota(jnp.int32, (1, SSM_CW), 1)
    for s in range(SSM_T):
        shifted = strip if s == 0 else pltpu.roll(strip, s * SSM_GROUP, 1)
        m_ref[s * SSM_GROUP:(s + 1) * SSM_GROUP, :] = jnp.where(
            col >= s * SSM_GROUP, shifted, 0.0).astype(BF16)

    x_a, y_a = jnp.where(lo, bb_re, bb_im), jnp.where(lo, -bb_im, bb_re)
    x_b, y_b = jnp.where(lo, bb_im, bb_re), jnp.where(lo, bb_re, -bb_im)
    p_ref[:, :STATE_LANES] = (outer(wr_re, x_a) + outer(wr_im, y_a)).astype(BF16)
    p_ref[:, STATE_LANES:] = (outer(wr_re, x_b) + outer(wr_im, y_b)).astype(BF16)

    qt_ref[...] = (outer(jnp.where(lo, w1_re, -w1_im), c_re)
                   + outer(jnp.where(lo, -w1_im, -w1_re), c_im)).astype(BF16)

    t_re, t_im = a_re, a_im
    for _ in range(6):
        t_re, t_im = t_re * t_re - t_im * t_im, 2.0 * t_re * t_im
    coef_ref[0:1, :] = t_re
    coef_ref[1:2, :] = jnp.where(lo, -t_im, t_im)
    coef_ref[2:8, :] = jnp.zeros((6, STATE_LANES), F32)


def _ssm_tables(lam_re, lam_im, log_step, b_re, b_im, c_re, c_im):
    dup = lambda a: jnp.concatenate([a, a], axis=-1)
    lr = dup(lam_re)[:, None, :]
    li = dup(lam_im)[:, None, :]
    ls = jnp.broadcast_to(log_step[:, None, None], (SSM_GROUPS, 1, STATE_LANES))
    bt_re = dup(jnp.swapaxes(b_re, 1, 2))
    bt_im = dup(jnp.swapaxes(b_im, 1, 2))
    c2_re, c2_im = dup(c_re), dup(c_im)
    g3 = lambda g: (g, 0, 0)
    row_spec = pl.BlockSpec((None, 1, STATE_LANES), g3)
    mat_spec = pl.BlockSpec((None, SSM_GROUP, STATE_LANES), g3)
    return pl.pallas_call(
        _tables_kernel,
        grid=(SSM_GROUPS,),
        in_specs=[row_spec, row_spec, row_spec, mat_spec, mat_spec, mat_spec, mat_spec],
        out_specs=[
            pl.BlockSpec((None, SSM_CW, SSM_CW), g3),
            pl.BlockSpec((None, SSM_CW, 2 * STATE_LANES), g3),
            pl.BlockSpec((None, SSM_CW, STATE_LANES), g3),
            pl.BlockSpec((None, 8, STATE_LANES), g3),
        ],
        out_shape=[
            jax.ShapeDtypeStruct((SSM_GROUPS, SSM_CW, SSM_CW), BF16),
            jax.ShapeDtypeStruct((SSM_GROUPS, SSM_CW, 2 * STATE_LANES), BF16),
            jax.ShapeDtypeStruct((SSM_GROUPS, SSM_CW, STATE_LANES), BF16),
            jax.ShapeDtypeStruct((SSM_GROUPS, 8, STATE_LANES), F32),
        ],
        compiler_params=pltpu.CompilerParams(dimension_semantics=("arbitrary",)),
        name="ssm_tables",
    )(lr, li, ls, bt_re, bt_im, c2_re, c2_im)


def _ssm_kernel(u_ref, m_ref, p_ref, qt_ref, coef_ref, h0_ref, h0s_ref,
                y_ref, hp_ref, hs_ref, s12_ref, hprev_ref, *, n_prompt_chunks, n_sample):
    u = u_ref[...]
    s12_ref[...] = jnp.dot(u, p_ref[...], preferred_element_type=F32)
    a1 = coef_ref[0:1, :]
    a2 = coef_ref[1:2, :]
    h = jnp.zeros((PROMPT_ROWS, STATE_LANES), F32)
    hsw = jnp.zeros((PROMPT_ROWS, STATE_LANES), F32)
    for c in range(n_prompt_chunks):
        rows = slice(c * PROMPT_ROWS, (c + 1) * PROMPT_ROWS)
        hprev_ref[rows, :] = h
        s1 = s12_ref[rows, :STATE_LANES]
        s2 = s12_ref[rows, STATE_LANES:]
        h, hsw = a1 * h + a2 * hsw + s1, a1 * hsw - a2 * h + s2
    hp_ref[...] = h
    base = n_prompt_chunks * PROMPT_ROWS
    h0, h0s = h0_ref[...], h0s_ref[...]
    hprev_ref[base:base + n_sample, :] = h0
    hs_ref[...] = a1 * h0 + a2 * h0s + s12_ref[base:base + n_sample, :STATE_LANES]
    y = jnp.dot(u, m_ref[...], preferred_element_type=F32)
    y += lax.dot_general(hprev_ref[...].astype(BF16), qt_ref[...], (((1,), (1,)), ((), ())),
                         preferred_element_type=F32)
    y_ref[...] = y


def _ssm(u_rows, m_tab, p_tab, qt_tab, coef, h0, h0s, *, n_prompt_chunks, n_sample):
    rows = u_rows.shape[1]
    g3 = lambda g: (g, 0, 0)
    kernel = functools.partial(_ssm_kernel, n_prompt_chunks=n_prompt_chunks, n_sample=n_sample)
    return pl.pallas_call(
        kernel,
        grid=(SSM_GROUPS,),
        in_specs=[
            pl.BlockSpec((None, rows, SSM_CW), g3),
            pl.BlockSpec((None, SSM_CW, SSM_CW), g3),
            pl.BlockSpec((None, SSM_CW, 2 * STATE_LANES), g3),
            pl.BlockSpec((None, SSM_CW, STATE_LANES), g3),
            pl.BlockSpec((None, 8, STATE_LANES), g3),
            pl.BlockSpec((None, n_sample, STATE_LANES), g3),
            pl.BlockSpec((None, n_sample, STATE_LANES), g3),
        ],
        out_specs=[
            pl.BlockSpec((None, rows, SSM_CW), g3),
            pl.BlockSpec((None, PROMPT_ROWS, STATE_LANES), g3),
            pl.BlockSpec((None, n_sample, STATE_LANES), g3),
        ],
        out_shape=[
            jax.ShapeDtypeStruct((SSM_GROUPS, rows, SSM_CW), F32),
            jax.ShapeDtypeStruct((SSM_GROUPS, PROMPT_ROWS, STATE_LANES), F32),
            jax.ShapeDtypeStruct((SSM_GROUPS, n_sample, STATE_LANES), F32),
        ],
        scratch_shapes=[
            pltpu.VMEM((rows, 2 * STATE_LANES), F32),
            pltpu.VMEM((rows, STATE_LANES), F32),
        ],
        compiler_params=pltpu.CompilerParams(dimension_semantics=("arbitrary",)),
        name="ssm",
    )(u_rows, m_tab, p_tab, qt_tab, coef, h0, h0s)


def _post_kernel(x_ref, a_ref, u_ref, y_ref, lng_ref, lnb_ref, d_ref, wglu_ref, bglu_ref,
                 wout_ref, ln1g_ref, ln1b_ref, wgu_ref, wdown_ref, ln2g_ref, ln2b_ref, o_ref,
                 *, alpha):
    xn = _layer_norm(x_ref[...], lng_ref[...], lnb_ref[...])
    ys = y_ref[...] + d_ref[...] * u_ref[...]
    gl = 0.5 * ys * (1.0 + lax.erf(ys * math.sqrt(0.5)))
    z = jnp.dot(gl.astype(BF16), wglu_ref[...], preferred_element_type=F32) + bglu_ref[...]
    s = gl * jax.nn.sigmoid(z)
    mix = jnp.dot(a_ref[...], wout_ref[:PROJ_Q, :], preferred_element_type=F32)
    mix += jnp.dot(s.astype(BF16), wout_ref[PROJ_Q:, :], preferred_element_type=F32)
    h = _layer_norm(alpha * xn + mix, ln1g_ref[...], ln1b_ref[...])
    hb = h.astype(BF16)
    f = jnp.zeros_like(h)
    for j in range(D_FF // FF_BLOCK):
        cols = slice(j * FF_BLOCK, (j + 1) * FF_BLOCK)
        up_cols = slice(D_FF + j * FF_BLOCK, D_FF + (j + 1) * FF_BLOCK)
        g = jnp.dot(hb, wgu_ref[:, cols], preferred_element_type=F32)
        up = jnp.dot(hb, wgu_ref[:, up_cols], preferred_element_type=F32)
        act = (g * jax.nn.sigmoid(g)) * up
        f += jnp.dot(act.astype(BF16), wdown_ref[cols, :], preferred_element_type=F32)
    o_ref[...] = _layer_norm(alpha * h + f, ln2g_ref[...], ln2b_ref[...])


def _post(x2d, a2d, u2d, y2d, ln_g, ln_b, d_skip, w_glu, b_glu, w_out, ln1_g, ln1_b,
          w_gate_up, w_down, ln2_g, ln2_b, *, alpha, tm):
    n = x2d.shape[0]
    row = lambda i: (i, 0)
    const = lambda i: (0, 0)
    resident = lambda shape: pl.BlockSpec(shape, const, pipeline_mode=pl.Buffered(1))
    vec = lambda width: pl.BlockSpec((1, width), const)
    return pl.pallas_call(
        functools.partial(_post_kernel, alpha=alpha),
        grid=(n // tm,),
        in_specs=[
            pl.BlockSpec((tm, D_MODEL), row),
            pl.BlockSpec((tm, PROJ_Q), row),
            pl.BlockSpec((tm, SSM_WIDTH), row),
            pl.BlockSpec((tm, SSM_WIDTH), row),
            vec(D_MODEL), vec(D_MODEL), vec(SSM_WIDTH),
            resident((SSM_WIDTH, SSM_WIDTH)), vec(SSM_WIDTH),
            resident((D_MODEL, D_MODEL)), vec(D_MODEL), vec(D_MODEL),
            resident((D_MODEL, 2 * D_FF)), resident((D_FF, D_MODEL)),
            vec(D_MODEL), vec(D_MODEL),
        ],
        out_specs=pl.BlockSpec((tm, D_MODEL), row),
        out_shape=jax.ShapeDtypeStruct((n, D_MODEL), F32),
        compiler_params=pltpu.CompilerParams(
            dimension_semantics=("arbitrary",), vmem_limit_bytes=V7X_VMEM_LIMIT_BYTES),
        name="post",
    )(x2d, a2d, u2d, y2d, ln_g, ln_b, d_skip, w_glu, b_glu, w_out, ln1_g, ln1_b,
      w_gate_up, w_down, ln2_g, ln2_b)


def _chunk_major(u2d, bsz, seq):
    u5 = u2d.reshape(bsz, seq // SSM_T, SSM_T, SSM_GROUPS, SSM_GROUP)
    return jnp.transpose(u5, (3, 1, 0, 2, 4)).reshape(SSM_GROUPS, seq // SSM_T, bsz, SSM_CW)


def _token_major(y4, bsz, seq):
    y5 = y4.reshape(SSM_GROUPS, seq // SSM_T, bsz, SSM_T, SSM_GROUP)
    return jnp.transpose(y5, (2, 1, 3, 0, 4)).reshape(bsz * seq, SSM_WIDTH)


def kernel(x_prompt, x_sample, cache_win_k, cache_win_v, state_ssm_re, state_ssm_im,
           ln_in_g, ln_in_b, w_in, attn_sinks, ssm_lambda_re, ssm_lambda_im, ssm_log_step,
           ssm_b_re, ssm_b_im, ssm_c_re, ssm_c_im, ssm_d, w_glu, b_glu, w_out,
           ln1_g, ln1_b, w_gate_up, w_down, ln2_g, ln2_b):
    depth = w_in.shape[0]
    assert depth == 1, "single-layer step"
    bp, lp, _ = x_prompt.shape
    bs, ls, _ = x_sample.shape
    assert ls == SSM_T and lp % (4 * CHUNK) == 0 and bp <= PROMPT_ROWS
    win_rows = cache_win_k.shape[2]
    assert win_rows == WINDOW
    alpha = (2.0 * depth) ** 0.25
    l = 0
    row = lambda a: a.reshape(1, -1)

    w_in_b = w_in[l].astype(BF16)
    lng, lnb = row(ln_in_g), row(ln_in_b)
    xp2 = x_prompt.reshape(bp * lp, D_MODEL)
    xs2 = x_sample.reshape(bs * ls, D_MODEL)
    qp, kp, vp, up = _proj(xp2, lng, lnb, w_in_b, 512)
    qs, ks, vs, us = _proj(xs2, lng, lnb, w_in_b, 512)

    sink_rows = jnp.repeat(attn_sinks[l].reshape(N_KV_HEADS, Q_PER_KV), CHUNK, axis=1)[..., None]
    kp3, vp3 = kp.reshape(bp, lp, PROJ_KV), vp.reshape(bp, lp, PROJ_KV)
    n_chunks_p = 4
    blocks_per_tile = n_chunks_p * CHUNK // WINDOW
    prev_prompt = lambda b, i: (b, jnp.maximum(i * blocks_per_tile - 1, 0), 0)
    ap = _attention(sink_rows, qp.reshape(bp, lp, PROJ_Q), kp3, kp3, vp3, vp3,
                    n_chunks=n_chunks_p, masked=True, prev_map=prev_prompt)
    ck = cache_win_k[l].reshape(bs, win_rows, PROJ_KV)
    cv = cache_win_v[l].reshape(bs, win_rows, PROJ_KV)
    ks3, vs3 = ks.reshape(bs, ls, PROJ_KV), vs.reshape(bs, ls, PROJ_KV)
    a_s = _attention(sink_rows, qs.reshape(bs, ls, PROJ_Q), ck, ks3, cv, vs3,
                     n_chunks=1, masked=False, prev_map=lambda b, i: (b, 0, 0))

    m_tab, p_tab, qt_tab, coef = _ssm_tables(
        ssm_lambda_re[l], ssm_lambda_im[l], ssm_log_step[l],
        ssm_b_re[l], ssm_b_im[l], ssm_c_re[l], ssm_c_im[l])
    ncp = lp // SSM_T
    ucp = _chunk_major(up.astype(BF16), bp, lp)
    ucp = jnp.pad(ucp, ((0, 0), (0, 0), (0, PROMPT_ROWS - bp), (0, 0)))
    ucs = _chunk_major(us.astype(BF16), bs, ls)
    u_rows = jnp.concatenate(
        [ucp.reshape(SSM_GROUPS, ncp * PROMPT_ROWS, SSM_CW), ucs.reshape(SSM_GROUPS, bs, SSM_CW)],
        axis=1)
    sre = jnp.swapaxes(state_ssm_re[l], 0, 1)
    sim = jnp.swapaxes(state_ssm_im[l], 0, 1)
    h0 = jnp.concatenate([sre, sim], axis=-1)
    h0s = jnp.concatenate([sim, sre], axis=-1)
    y_rows, hfin_p, hfin_s = _ssm(u_rows, m_tab, p_tab, qt_tab, coef, h0, h0s,
                                  n_prompt_chunks=ncp, n_sample=bs)
    yp = _token_major(
        y_rows[:, :ncp * PROMPT_ROWS].reshape(SSM_GROUPS, ncp, PROMPT_ROWS, SSM_CW)[:, :, :bp],
        bp, lp)
    ys = _token_major(y_rows[:, ncp * PROMPT_ROWS:].reshape(SSM_GROUPS, 1, bs, SSM_CW), bs, ls)

    post_args = (lng, lnb, row(ssm_d[l]), w_glu[l].astype(BF16), row(b_glu[l]),
                 w_out[l].astype(BF16), row(ln1_g[l]), row(ln1_b[l]),
                 w_gate_up[l].astype(BF16), w_down[l].astype(BF16), row(ln2_g[l]), row(ln2_b[l]))
    out_p = _post(xp2, ap.reshape(bp * lp, PROJ_Q), up, yp, *post_args, alpha=alpha, tm=512)
    out_s = _post(xs2, a_s.reshape(bs * ls, PROJ_Q), us, ys, *post_args, alpha=alpha, tm=512)

    kv_shape = (N_KV_HEADS, HEAD_DIM)
    win_k_p = kp3[:, -win_rows:].reshape(1, bp, win_rows, *kv_shape)
    win_v_p = vp3[:, -win_rows:].reshape(1, bp, win_rows, *kv_shape)
    win_k_s = jnp.concatenate([ck, ks3], axis=1)[:, -win_rows:].reshape(1, bs, win_rows, *kv_shape)
    win_v_s = jnp.concatenate([cv, vs3], axis=1)[:, -win_rows:].reshape(1, bs, win_rows, *kv_shape)
    state = lambda hf, n: jnp.swapaxes(hf[:, :n], 0, 1)
    sp, ss = state(hfin_p, bp), state(hfin_s, bs)
    return (out_p.reshape(bp, lp, D_MODEL), out_s.reshape(bs, ls, D_MODEL),
            win_k_p, win_v_p, sp[None, ..., :SSM_STATE], sp[None, ..., SSM_STATE:],
            win_k_s, win_v_s, ss[None, ..., :SSM_STATE], ss[None, ..., SSM_STATE:])
```

```python
import functools
import math

import jax
import jax.numpy as jnp
from jax import lax
from jax.experimental import pallas as pl
from jax.experimental.pallas import tpu as pltpu

F32 = jnp.float32
BF16 = jnp.bfloat16

D_MODEL = 1024
HEAD_DIM = 64
N_HEADS = 8
N_KV_HEADS = 2
Q_PER_KV = N_HEADS // N_KV_HEADS
CHUNK = 64
WINDOW = 128
WIN_CHUNKS = WINDOW // CHUNK
BAND = (WIN_CHUNKS + 1) * CHUNK
PROJ_Q = N_HEADS * HEAD_DIM
PROJ_KV = N_KV_HEADS * HEAD_DIM
SSM_WIDTH = 512
SSM_GROUP = 16
SSM_GROUPS = SSM_WIDTH // SSM_GROUP
SSM_STATE = 64
STATE_LANES = 2 * SSM_STATE
D_FF = 2816
D_IN_PROJ = PROJ_Q + 2 * PROJ_KV + SSM_WIDTH
LN_EPS = 1e-5
NEG_INF = -1e30

SSM_T = CHUNK
SSM_CW = SSM_T * SSM_GROUP
PROMPT_ROWS = 8
FF_BLOCK = 256
V7X_VMEM_LIMIT_BYTES = 56 * 1024 * 1024


def _layer_norm(x, g, b):
    mu = jnp.mean(x, axis=-1, keepdims=True)
    xc = x - mu
    var = jnp.mean(xc * xc, axis=-1, keepdims=True)
    return xc * lax.rsqrt(var + LN_EPS) * g + b


KVU = 2 * PROJ_KV + SSM_WIDTH
QV = PROJ_Q + PROJ_KV


def _proj_kernel(x_ref, g_ref, b_ref, w_ref, wt_ref, qt_ref, vt_ref, k_ref, v_ref, u_ref):
    xb = _layer_norm(x_ref[...], g_ref[...], b_ref[...]).astype(BF16)
    p = jnp.dot(xb, w_ref[...], preferred_element_type=F32)
    k_ref[...] = p[:, :PROJ_KV]
    v_ref[...] = p[:, PROJ_KV:2 * PROJ_KV]
    u_ref[...] = p[:, 2 * PROJ_KV:]
    pt = lax.dot_general(wt_ref[...], xb, (((1,), (1,)), ((), ())), preferred_element_type=F32)
    qt_ref[...] = (pt[:PROJ_Q] * (HEAD_DIM ** -0.5)).astype(BF16)
    vt_ref[...] = pt[PROJ_Q:].astype(BF16)


def _proj(x2d, ln_g, ln_b, w_kvu, w_qv_t, tm):
    n = x2d.shape[0]
    const = lambda i: (0, 0)
    row = lambda i: (i, 0)
    col = lambda i: (0, i)
    return pl.pallas_call(
        _proj_kernel,
        grid=(n // tm,),
        in_specs=[
            pl.BlockSpec((tm, D_MODEL), row),
            pl.BlockSpec((1, D_MODEL), const),
            pl.BlockSpec((1, D_MODEL), const),
            pl.BlockSpec((D_MODEL, KVU), const),
            pl.BlockSpec((QV, D_MODEL), const),
        ],
        out_specs=[
            pl.BlockSpec((PROJ_Q, tm), col),
            pl.BlockSpec((PROJ_KV, tm), col),
            pl.BlockSpec((tm, PROJ_KV), row),
            pl.BlockSpec((tm, PROJ_KV), row),
            pl.BlockSpec((tm, SSM_WIDTH), row),
        ],
        out_shape=[
            jax.ShapeDtypeStruct((PROJ_Q, n), BF16),
            jax.ShapeDtypeStruct((PROJ_KV, n), BF16),
            jax.ShapeDtypeStruct((n, PROJ_KV), F32),
            jax.ShapeDtypeStruct((n, PROJ_KV), F32),
            jax.ShapeDtypeStruct((n, SSM_WIDTH), F32),
        ],
        compiler_params=pltpu.CompilerParams(dimension_semantics=("arbitrary",)),
        name="proj",
    )(x2d, ln_g, ln_b, w_kvu, w_qv_t)


PAIR = 2 * CHUNK
HEAD_LANES = Q_PER_KV * PAIR


def _attend_pair(kwin, vtwin, qt_ref, lanes, valid, sink_ref, o_ref):
    for h in range(N_KV_HEADS):
        base = h * Q_PER_KV * HEAD_DIM
        qrow = jnp.concatenate(
            [qt_ref[base + g * HEAD_DIM:base + (g + 1) * HEAD_DIM, lanes] for g in range(Q_PER_KV)],
            axis=1)
        zero = jnp.zeros_like(qrow)
        qstack = jnp.concatenate([qrow, zero] if h == 0 else [zero, qrow], axis=0)
        s = jnp.dot(kwin, qstack, preferred_element_type=F32)
        s = jnp.where(valid, s, NEG_INF)
        sink = sink_ref[h]
        m = jnp.maximum(jnp.max(s, axis=0, keepdims=True), sink)
        p = jnp.exp(s - m)
        den = jnp.sum(p, axis=0, keepdims=True) + jnp.exp(sink - m)
        w = (p * (1.0 / den)).astype(BF16)
        o = jnp.dot(vtwin[h * HEAD_DIM:(h + 1) * HEAD_DIM, :], w, preferred_element_type=F32)
        for g in range(Q_PER_KV):
            o_ref[base + g * HEAD_DIM:base + (g + 1) * HEAD_DIM, lanes] = (
                o[:, g * PAIR:(g + 1) * PAIR].astype(BF16))


def _attn_prompt_kernel(sink_ref, qt_ref, kp_ref, kc_ref, vtp_ref, vtc_ref, o_ref, *, n_pairs):
    tq = n_pairs * PAIR
    nk = WINDOW + PAIR
    kk = jnp.concatenate([kp_ref[...], kc_ref[...]], axis=0).astype(BF16)
    vt = jnp.concatenate([vtp_ref[...], vtc_ref[...]], axis=1)
    r = lax.broadcasted_iota(jnp.int32, (nk, HEAD_LANES), 0)
    first_chunk = (lax.broadcasted_iota(jnp.int32, (nk, HEAD_LANES), 1) & (PAIR - 1)) < CHUNK
    lo = jnp.where(first_chunk, 0, CHUNK)
    hi = jnp.where(first_chunk, BAND, nk)
    first_pos = pl.program_id(1) * tq - WINDOW
    for pp in range(n_pairs):
        lo_pp = jnp.maximum(lo, -first_pos) if pp == 0 else lo
        valid = (r >= lo_pp) & (r < hi)
        _attend_pair(kk[pp * PAIR:pp * PAIR + nk], vt[:, pp * PAIR:pp * PAIR + nk], qt_ref,
                     slice(pp * PAIR, (pp + 1) * PAIR), valid, sink_ref, o_ref)


def _attn_sample_kernel(sink_ref, qt_ref, ck_ref, kn_ref, cvt_ref, vtn_ref, o_ref, *, n_pairs):
    nk = 2 * WINDOW + PAIR
    r = lax.broadcasted_iota(jnp.int32, (nk, HEAD_LANES), 0)
    query_seq = (lax.broadcasted_iota(jnp.int32, (nk, HEAD_LANES), 1) & (PAIR - 1)) >> 6
    key_seq = jnp.where(r < 2 * WINDOW, r >> 7, (r - 2 * WINDOW) >> 6)
    valid = query_seq == key_seq
    for pp in range(n_pairs):
        lanes = slice(pp * PAIR, (pp + 1) * PAIR)
        kwin = jnp.concatenate([ck_ref[2 * pp], ck_ref[2 * pp + 1], kn_ref[lanes, :]],
                               axis=0).astype(BF16)
        vtwin = jnp.concatenate([cvt_ref[:, 2 * pp * WINDOW:2 * (pp + 1) * WINDOW], vtn_ref[:, lanes]],
                                axis=1)
        _attend_pair(kwin, vtwin, qt_ref, lanes, valid, sink_ref, o_ref)


_SINK_SPEC = pl.BlockSpec((N_KV_HEADS, 1, HEAD_LANES), lambda *_: (0, 0, 0))


def _attention_prompt(sink_rows, qt, k, vt, bsz, seq, n_pairs):
    tq = n_pairs * PAIR
    nt = seq // tq
    wpt = tq // WINDOW
    cur_c = lambda b, i: (0, b * nt + i)
    cur_r = lambda b, i: (b * nt + i, 0)
    prev = lambda b, i: jnp.maximum((b * nt + i) * wpt - 1, 0)
    return pl.pallas_call(
        functools.partial(_attn_prompt_kernel, n_pairs=n_pairs),
        grid=(bsz, nt),
        in_specs=[
            _SINK_SPEC,
            pl.BlockSpec((PROJ_Q, tq), cur_c),
            pl.BlockSpec((WINDOW, PROJ_KV), lambda b, i: (prev(b, i), 0)),
            pl.BlockSpec((tq, PROJ_KV), cur_r),
            pl.BlockSpec((PROJ_KV, WINDOW), lambda b, i: (0, prev(b, i))),
            pl.BlockSpec((PROJ_KV, tq), cur_c),
        ],
        out_specs=pl.BlockSpec((PROJ_Q, tq), cur_c),
        out_shape=jax.ShapeDtypeStruct((PROJ_Q, bsz * seq), BF16),
        compiler_params=pltpu.CompilerParams(dimension_semantics=("arbitrary", "arbitrary")),
        name="attn_prompt",
    )(sink_rows, qt, k, k, vt, vt)


def _attention_sample(sink_rows, qt, cache_k, k, cache_vt, vt, n_pairs):
    n = qt.shape[1]
    tq = n_pairs * PAIR
    return pl.pallas_call(
        functools.partial(_attn_sample_kernel, n_pairs=n_pairs),
        grid=(n // tq,),
        in_specs=[
            _SINK_SPEC,
            pl.BlockSpec((PROJ_Q, tq), lambda i: (0, i)),
            pl.BlockSpec((2 * n_pairs, WINDOW, PROJ_KV), lambda i: (i, 0, 0)),
            pl.BlockSpec((tq, PROJ_KV), lambda i: (i, 0)),
            pl.BlockSpec((PROJ_KV, 2 * n_pairs * WINDOW), lambda i: (0, i)),
            pl.BlockSpec((PROJ_KV, tq), lambda i: (0, i)),
        ],
        out_specs=pl.BlockSpec((PROJ_Q, tq), lambda i: (0, i)),
        out_shape=jax.ShapeDtypeStruct((PROJ_Q, n), BF16),
        compiler_params=pltpu.CompilerParams(dimension_semantics=("arbitrary",)),
        name="attn_sample",
    )(sink_rows, qt, cache_k, k, cache_vt, vt)


def _power_table(tau, nbits, a_re, a_im):
    rows = tau.shape[0]
    w_re = jnp.ones((rows, STATE_LANES), F32)
    w_im = jnp.zeros((rows, STATE_LANES), F32)
    p_re, p_im = a_re, a_im
    for k in range(nbits):
        bit = ((tau >> k) & 1) == 1
        f_re = jnp.where(bit, p_re, 1.0)
        f_im = jnp.where(bit, p_im, 0.0)
        w_re, w_im = w_re * f_re - w_im * f_im, w_re * f_im + w_im * f_re
        p_re, p_im = p_re * p_re - p_im * p_im, 2.0 * p_re * p_im
    return w_re, w_im


def _tables_kernel(lr_ref, li_ref, ls_ref, bre_ref, bim_ref, cre_ref, cim_ref,
                   m_ref, p_ref, qt_ref, coef_ref):
    lo = lax.broadcasted_iota(jnp.int32, (1, STATE_LANES), 1) < SSM_STATE
    lr, li = lr_ref[...], li_ref[...]
    dt = jnp.exp(ls_ref[...])
    mag = jnp.exp(lr * dt)
    a_re, a_im = mag * jnp.cos(li * dt), mag * jnp.sin(li * dt)
    nr, ni = a_re - 1.0, a_im
    den = lr * lr + li * li
    f_re, f_im = (nr * lr + ni * li) / den, (ni * lr - nr * li) / den
    b_re, b_im = bre_ref[...], bim_ref[...]
    bb_re = f_re * b_re - f_im * b_im
    bb_im = f_re * b_im + f_im * b_re
    c_re, c_im = cre_ref[...], cim_ref[...]

    tau = lax.broadcasted_iota(jnp.int32, (SSM_T, 1), 0)
    w_re, w_im = _power_table(tau, 6, a_re, a_im)
    w1_re, w1_im = _power_table(tau + 1, 7, a_re, a_im)
    wr_re, wr_im = _power_table(SSM_T - 1 - tau, 6, a_re, a_im)

    def outer(w, c):
        return (w[:, None, :] * c[None, :, :]).reshape(SSM_CW, STATE_LANES)

    cw_mix = (outer(jnp.where(lo, w_re, w_im), c_re) + outer(jnp.where(lo, -w_im, w_re), c_im))
    bb_mix = jnp.where(lo, bb_re, -bb_im)
    strip = lax.dot_general(bb_mix, cw_mix, (((1,), (1,)), ((), ())),
                            precision=lax.Precision.HIGHEST,
                            preferred_element_type=F32)
    col = lax.broadcasted_iota(jnp.int32, (1, SSM_CW), 1)
    for s in range(SSM_T):
        shifted = strip if s == 0 else pltpu.roll(strip, s * SSM_GROUP, 1)
        m_ref[s * SSM_GROUP:(s + 1) * SSM_GROUP, :] = jnp.where(
            col >= s * SSM_GROUP, shifted, 0.0).astype(BF16)

    x_a, y_a = jnp.where(lo, bb_re, bb_im), jnp.where(lo, -bb_im, bb_re)
    x_b, y_b = jnp.where(lo, bb_im, bb_re), jnp.where(lo, bb_re, -bb_im)
    p_ref[:, :STATE_LANES] = (outer(wr_re, x_a) + outer(wr_im, y_a)).astype(BF16)
    p_ref[:, STATE_LANES:] = (outer(wr_re, x_b) + outer(wr_im, y_b)).astype(BF16)

    qt_ref[...] = (outer(jnp.where(lo, w1_re, -w1_im), c_re)
                   + outer(jnp.where(lo, -w1_im, -w1_re), c_im)).astype(BF16)

    t_re, t_im = a_re, a_im
    for _ in range(6):
        t_re, t_im = t_re * t_re - t_im * t_im, 2.0 * t_re * t_im
    coef_ref[0:1, :] = t_re
    coef_ref[1:2, :] = jnp.where(lo, -t_im, t_im)
    coef_ref[2:8, :] = jnp.zeros((6, STATE_LANES), F32)


def _ssm_tables(lam_re, lam_im, log_step, b_re, b_im, c_re, c_im):
    dup = lambda a: jnp.concatenate([a, a], axis=-1)
    lr = dup(lam_re)[:, None, :]
    li = dup(lam_im)[:, None, :]
    ls = jnp.broadcast_to(log_step[:, None, None], (SSM_GROUPS, 1, STATE_LANES))
    bt_re = dup(jnp.swapaxes(b_re, 1, 2))
    bt_im = dup(jnp.swapaxes(b_im, 1, 2))
    c2_re, c2_im = dup(c_re), dup(c_im)
    g3 = lambda g: (g, 0, 0)
    row_spec = pl.BlockSpec((None, 1, STATE_LANES), g3)
    mat_spec = pl.BlockSpec((None, SSM_GROUP, STATE_LANES), g3)
    return pl.pallas_call(
        _tables_kernel,
        grid=(SSM_GROUPS,),
        in_specs=[row_spec, row_spec, row_spec, mat_spec, mat_spec, mat_spec, mat_spec],
        out_specs=[
            pl.BlockSpec((None, SSM_CW, SSM_CW), g3),
            pl.BlockSpec((None, SSM_CW, 2 * STATE_LANES), g3),
            pl.BlockSpec((None, SSM_CW, STATE_LANES), g3),
            pl.BlockSpec((None, 8, STATE_LANES), g3),
        ],
        out_shape=[
            jax.ShapeDtypeStruct((SSM_GROUPS, SSM_CW, SSM_CW), BF16),
            jax.ShapeDtypeStruct((SSM_GROUPS, SSM_CW, 2 * STATE_LANES), BF16),
            jax.ShapeDtypeStruct((SSM_GROUPS, SSM_CW, STATE_LANES), BF16),
            jax.ShapeDtypeStruct((SSM_GROUPS, 8, STATE_LANES), F32),
        ],
        compiler_params=pltpu.CompilerParams(dimension_semantics=("arbitrary",)),
        name="ssm_tables",
    )(lr, li, ls, bt_re, bt_im, c2_re, c2_im)


def _ssm_kernel(u_ref, m_ref, p_ref, qt_ref, coef_ref, h0_ref, h0s_ref,
                y_ref, hp_ref, hs_ref, s12_ref, hprev_ref, *, n_prompt_chunks, n_sample):
    u = u_ref[...]
    s12_ref[...] = jnp.dot(u, p_ref[...], preferred_element_type=F32)
    a1 = coef_ref[0:1, :]
    a2 = coef_ref[1:2, :]
    h = jnp.zeros((PROMPT_ROWS, STATE_LANES), F32)
    hsw = jnp.zeros((PROMPT_ROWS, STATE_LANES), F32)
    for c in range(n_prompt_chunks):
        rows = slice(c * PROMPT_ROWS, (c + 1) * PROMPT_ROWS)
        hprev_ref[rows, :] = h
        s1 = s12_ref[rows, :STATE_LANES]
        s2 = s12_ref[rows, STATE_LANES:]
        h, hsw = a1 * h + a2 * hsw + s1, a1 * hsw - a2 * h + s2
    hp_ref[...] = h
    base = n_prompt_chunks * PROMPT_ROWS
    h0, h0s = h0_ref[...], h0s_ref[...]
    hprev_ref[base:base + n_sample, :] = h0
    hs_ref[...] = a1 * h0 + a2 * h0s + s12_ref[base:base + n_sample, :STATE_LANES]
    y = jnp.dot(u, m_ref[...], preferred_element_type=F32)
    y += lax.dot_general(hprev_ref[...].astype(BF16), qt_ref[...], (((1,), (1,)), ((), ())),
                         preferred_element_type=F32)
    y_ref[...] = y


def _ssm(u_rows, m_tab, p_tab, qt_tab, coef, h0, h0s, *, n_prompt_chunks, n_sample):
    rows = u_rows.shape[1]
    g3 = lambda g: (g, 0, 0)
    kernel = functools.partial(_ssm_kernel, n_prompt_chunks=n_prompt_chunks, n_sample=n_sample)
    return pl.pallas_call(
        kernel,
        grid=(SSM_GROUPS,),
        in_specs=[
            pl.BlockSpec((None, rows, SSM_CW), g3),
            pl.BlockSpec((None, SSM_CW, SSM_CW), g3),
            pl.BlockSpec((None, SSM_CW, 2 * STATE_LANES), g3),
            pl.BlockSpec((None, SSM_CW, STATE_LANES), g3),
            pl.BlockSpec((None, 8, STATE_LANES), g3),
            pl.BlockSpec((None, n_sample, STATE_LANES), g3),
            pl.BlockSpec((None, n_sample, STATE_LANES), g3),
        ],
        out_specs=[
            pl.BlockSpec((None, rows, SSM_CW), g3),
            pl.BlockSpec((None, PROMPT_ROWS, STATE_LANES), g3),
            pl.BlockSpec((None, n_sample, STATE_LANES), g3),
        ],
        out_shape=[
            jax.ShapeDtypeStruct((SSM_GROUPS, rows, SSM_CW), F32),
            jax.ShapeDtypeStruct((SSM_GROUPS, PROMPT_ROWS, STATE_LANES), F32),
            jax.ShapeDtypeStruct((SSM_GROUPS, n_sample, STATE_LANES), F32),
        ],
        scratch_shapes=[
            pltpu.VMEM((rows, 2 * STATE_LANES), F32),
            pltpu.VMEM((rows, STATE_LANES), F32),
        ],
        compiler_params=pltpu.CompilerParams(dimension_semantics=("arbitrary",)),
        name="ssm",
    )(u_rows, m_tab, p_tab, qt_tab, coef, h0, h0s)


def _post_kernel(x_ref, at_ref, u_ref, y_ref, lng_ref, lnb_ref, d_ref, wglu_ref, bglu_ref,
                 wout_ref, ln1g_ref, ln1b_ref, wgu_ref, wdown_ref, ln2g_ref, ln2b_ref, o_ref,
                 *, alpha):
    xn = _layer_norm(x_ref[...], lng_ref[...], lnb_ref[...])
    ys = y_ref[...] + d_ref[...] * u_ref[...]
    gl = 0.5 * ys * (1.0 + lax.erf(ys * math.sqrt(0.5)))
    z = jnp.dot(gl.astype(BF16), wglu_ref[...], preferred_element_type=F32) + bglu_ref[...]
    s = gl * jax.nn.sigmoid(z)
    mix = lax.dot_general(at_ref[...], wout_ref[:PROJ_Q, :], (((0,), (0,)), ((), ())),
                          preferred_element_type=F32)
    mix += jnp.dot(s.astype(BF16), wout_ref[PROJ_Q:, :], preferred_element_type=F32)
    h = _layer_norm(alpha * xn + mix, ln1g_ref[...], ln1b_ref[...])
    hb = h.astype(BF16)
    f = jnp.zeros_like(h)
    for j in range(D_FF // FF_BLOCK):
        cols = slice(j * FF_BLOCK, (j + 1) * FF_BLOCK)
        up_cols = slice(D_FF + j * FF_BLOCK, D_FF + (j + 1) * FF_BLOCK)
        g = jnp.dot(hb, wgu_ref[:, cols], preferred_element_type=F32)
        up = jnp.dot(hb, wgu_ref[:, up_cols], preferred_element_type=F32)
        act = (g * jax.nn.sigmoid(g)) * up
        f += jnp.dot(act.astype(BF16), wdown_ref[cols, :], preferred_element_type=F32)
    o_ref[...] = _layer_norm(alpha * h + f, ln2g_ref[...], ln2b_ref[...])


def _post(x2d, at2d, u2d, y2d, ln_g, ln_b, d_skip, w_glu, b_glu, w_out, ln1_g, ln1_b,
          w_gate_up, w_down, ln2_g, ln2_b, *, alpha, tm):
    n = x2d.shape[0]
    row = lambda i: (i, 0)
    const = lambda i: (0, 0)
    resident = lambda shape: pl.BlockSpec(shape, const, pipeline_mode=pl.Buffered(1))
    vec = lambda width: pl.BlockSpec((1, width), const)
    return pl.pallas_call(
        functools.partial(_post_kernel, alpha=alpha),
        grid=(n // tm,),
        in_specs=[
            pl.BlockSpec((tm, D_MODEL), row),
            pl.BlockSpec((PROJ_Q, tm), lambda i: (0, i)),
            pl.BlockSpec((tm, SSM_WIDTH), row),
            pl.BlockSpec((tm, SSM_WIDTH), row),
            vec(D_MODEL), vec(D_MODEL), vec(SSM_WIDTH),
            resident((SSM_WIDTH, SSM_WIDTH)), vec(SSM_WIDTH),
            resident((D_MODEL, D_MODEL)), vec(D_MODEL), vec(D_MODEL),
            resident((D_MODEL, 2 * D_FF)), resident((D_FF, D_MODEL)),
            vec(D_MODEL), vec(D_MODEL),
        ],
        out_specs=pl.BlockSpec((tm, D_MODEL), row),
        out_shape=jax.ShapeDtypeStruct((n, D_MODEL), F32),
        compiler_params=pltpu.CompilerParams(
            dimension_semantics=("arbitrary",), vmem_limit_bytes=V7X_VMEM_LIMIT_BYTES),
        name="post",
    )(x2d, at2d, u2d, y2d, ln_g, ln_b, d_skip, w_glu, b_glu, w_out, ln1_g, ln1_b,
      w_gate_up, w_down, ln2_g, ln2_b)


def _chunk_major(u2d, bsz, seq):
    u5 = u2d.reshape(bsz, seq // SSM_T, SSM_T, SSM_GROUPS, SSM_GROUP)
    return jnp.transpose(u5, (3, 1, 0, 2, 4)).reshape(SSM_GROUPS, seq // SSM_T, bsz, SSM_CW)


def _token_major(y4, bsz, seq):
    y5 = y4.reshape(SSM_GROUPS, seq // SSM_T, bsz, SSM_T, SSM_GROUP)
    return jnp.transpose(y5, (2, 1, 3, 0, 4)).reshape(bsz * seq, SSM_WIDTH)


def kernel(x_prompt, x_sample, cache_win_k, cache_win_v, state_ssm_re, state_ssm_im,
           ln_in_g, ln_in_b, w_in, attn_sinks, ssm_lambda_re, ssm_lambda_im, ssm_log_step,
           ssm_b_re, ssm_b_im, ssm_c_re, ssm_c_im, ssm_d, w_glu, b_glu, w_out,
           ln1_g, ln1_b, w_gate_up, w_down, ln2_g, ln2_b):
    depth = w_in.shape[0]
    assert depth == 1, "single-layer step"
    bp, lp, _ = x_prompt.shape
    bs, ls, _ = x_sample.shape
    assert ls == SSM_T and lp % (4 * CHUNK) == 0 and bp <= PROMPT_ROWS
    win_rows = cache_win_k.shape[2]
    assert win_rows == WINDOW
    alpha = (2.0 * depth) ** 0.25
    l = 0
    row = lambda a: a.reshape(1, -1)

    w_in_b = w_in[l].astype(BF16)
    w_kvu = w_in_b[:, PROJ_Q:]
    w_qv_t = jnp.concatenate([w_in_b[:, :PROJ_Q], w_in_b[:, PROJ_Q + PROJ_KV:PROJ_Q + 2 * PROJ_KV]],
                             axis=1).T
    lng, lnb = row(ln_in_g), row(ln_in_b)
    xp2 = x_prompt.reshape(bp * lp, D_MODEL)
    xs2 = x_sample.reshape(bs * ls, D_MODEL)
    qtp, vtp, kp, vp, up = _proj(xp2, lng, lnb, w_kvu, w_qv_t, 512)
    qts, vts, ks, vs, us = _proj(xs2, lng, lnb, w_kvu, w_qv_t, 512)

    sink_rows = jnp.repeat(attn_sinks[l].reshape(N_KV_HEADS, Q_PER_KV), PAIR, axis=1)[:, None, :]
    atp = _attention_prompt(sink_rows, qtp, kp, vtp, bp, lp, n_pairs=4)
    ck = cache_win_k[l].reshape(bs, win_rows, PROJ_KV)
    cv = cache_win_v[l].reshape(bs, win_rows, PROJ_KV)
    cvt = jnp.transpose(cv, (2, 0, 1)).reshape(PROJ_KV, bs * win_rows).astype(BF16)
    ats = _attention_sample(sink_rows, qts, ck, ks, cvt, vts, n_pairs=4)
    kp3, vp3 = kp.reshape(bp, lp, PROJ_KV), vp.reshape(bp, lp, PROJ_KV)
    ks3, vs3 = ks.reshape(bs, ls, PROJ_KV), vs.reshape(bs, ls, PROJ_KV)

    m_tab, p_tab, qt_tab, coef = _ssm_tables(
        ssm_lambda_re[l], ssm_lambda_im[l], ssm_log_step[l],
        ssm_b_re[l], ssm_b_im[l], ssm_c_re[l], ssm_c_im[l])
    ncp = lp // SSM_T
    ucp = _chunk_major(up.astype(BF16), bp, lp)
    ucp = jnp.pad(ucp, ((0, 0), (0, 0), (0, PROMPT_ROWS - bp), (0, 0)))
    ucs = _chunk_major(us.astype(BF16), bs, ls)
    u_rows = jnp.concatenate(
        [ucp.reshape(SSM_GROUPS, ncp * PROMPT_ROWS, SSM_CW), ucs.reshape(SSM_GROUPS, bs, SSM_CW)],
        axis=1)
    sre = jnp.swapaxes(state_ssm_re[l], 0, 1)
    sim = jnp.swapaxes(state_ssm_im[l], 0, 1)
    h0 = jnp.concatenate([sre, sim], axis=-1)
    h0s = jnp.concatenate([sim, sre], axis=-1)
    y_rows, hfin_p, hfin_s = _ssm(u_rows, m_tab, p_tab, qt_tab, coef, h0, h0s,
                                  n_prompt_chunks=ncp, n_sample=bs)
    yp = _token_major(
        y_rows[:, :ncp * PROMPT_ROWS].reshape(SSM_GROUPS, ncp, PROMPT_ROWS, SSM_CW)[:, :, :bp],
        bp, lp)
    ys = _token_major(y_rows[:, ncp * PROMPT_ROWS:].reshape(SSM_GROUPS, 1, bs, SSM_CW), bs, ls)

    post_args = (lng, lnb, row(ssm_d[l]), w_glu[l].astype(BF16), row(b_glu[l]),
                 w_out[l].astype(BF16), row(ln1_g[l]), row(ln1_b[l]),
                 w_gate_up[l].astype(BF16), w_down[l].astype(BF16), row(ln2_g[l]), row(ln2_b[l]))
    out_p = _post(xp2, atp, up, yp, *post_args, alpha=alpha, tm=512)
    out_s = _post(xs2, ats, us, ys, *post_args, alpha=alpha, tm=512)

    kv_shape = (N_KV_HEADS, HEAD_DIM)
    win_k_p = kp3[:, -win_rows:].reshape(1, bp, win_rows, *kv_shape)
    win_v_p = vp3[:, -win_rows:].reshape(1, bp, win_rows, *kv_shape)
    win_k_s = jnp.concatenate([ck, ks3], axis=1)[:, -win_rows:].reshape(1, bs, win_rows, *kv_shape)
    win_v_s = jnp.concatenate([cv, vs3], axis=1)[:, -win_rows:].reshape(1, bs, win_rows, *kv_shape)
    state = lambda hf, n: jnp.swapaxes(hf[:, :n], 0, 1)
    sp, ss = state(hfin_p, bp), state(hfin_s, bs)
    return (out_p.reshape(bp, lp, D_MODEL), out_s.reshape(bs, ls, D_MODEL),
            win_k_p, win_v_p, sp[None, ..., :SSM_STATE], sp[None, ..., SSM_STATE:],
            win_k_s, win_v_s, ss[None, ..., :SSM_STATE], ss[None, ..., SSM_STATE:])
```

```python
import functools
import math

import jax
import jax.numpy as jnp
from jax import lax
from jax.experimental import pallas as pl
from jax.experimental.pallas import tpu as pltpu

F32 = jnp.float32
BF16 = jnp.bfloat16

D_MODEL = 1024
HEAD_DIM = 64
N_HEADS = 8
N_KV_HEADS = 2
Q_PER_KV = N_HEADS // N_KV_HEADS
CHUNK = 64
WINDOW = 128
WIN_CHUNKS = WINDOW // CHUNK
BAND = (WIN_CHUNKS + 1) * CHUNK
PROJ_Q = N_HEADS * HEAD_DIM
PROJ_KV = N_KV_HEADS * HEAD_DIM
SSM_WIDTH = 512
SSM_GROUP = 16
SSM_GROUPS = SSM_WIDTH // SSM_GROUP
SSM_STATE = 64
STATE_LANES = 2 * SSM_STATE
D_FF = 2816
D_IN_PROJ = PROJ_Q + 2 * PROJ_KV + SSM_WIDTH
LN_EPS = 1e-5
NEG_INF = -1e30

SSM_T = CHUNK
SSM_CW = SSM_T * SSM_GROUP
ROW_TOKENS = 2 * SSM_T
COEF_ROWS = 16
STATE_OUT_ROWS = 8
FF_BLOCK = 256
V7X_VMEM_LIMIT_BYTES = 56 * 1024 * 1024


def _layer_norm(x, g, b):
    mu = jnp.mean(x, axis=-1, keepdims=True)
    xc = x - mu
    var = jnp.mean(xc * xc, axis=-1, keepdims=True)
    return xc * lax.rsqrt(var + LN_EPS) * g + b


KV = 2 * PROJ_KV
QVU = PROJ_Q + PROJ_KV + SSM_WIDTH


def _proj_kernel(x_ref, g_ref, b_ref, w_ref, wt_ref, qt_ref, vt_ref, ut_ref, k_ref, v_ref):
    xb = _layer_norm(x_ref[...], g_ref[...], b_ref[...]).astype(BF16)
    p = jnp.dot(xb, w_ref[...], preferred_element_type=F32)
    k_ref[...] = p[:, :PROJ_KV]
    v_ref[...] = p[:, PROJ_KV:]
    pt = lax.dot_general(wt_ref[...], xb, (((1,), (1,)), ((), ())), preferred_element_type=F32)
    qt_ref[...] = (pt[:PROJ_Q] * (HEAD_DIM ** -0.5)).astype(BF16)
    vt_ref[...] = pt[PROJ_Q:PROJ_Q + PROJ_KV].astype(BF16)
    ut_ref[...] = pt[PROJ_Q + PROJ_KV:]


def _proj(x2d, ln_g, ln_b, w_kv, w_qvu_t, tm):
    n = x2d.shape[0]
    const = lambda i: (0, 0)
    row = lambda i: (i, 0)
    col = lambda i: (0, i)
    return pl.pallas_call(
        _proj_kernel,
        grid=(n // tm,),
        in_specs=[
            pl.BlockSpec((tm, D_MODEL), row),
            pl.BlockSpec((1, D_MODEL), const),
            pl.BlockSpec((1, D_MODEL), const),
            pl.BlockSpec((D_MODEL, KV), const),
            pl.BlockSpec((QVU, D_MODEL), const),
        ],
        out_specs=[
            pl.BlockSpec((PROJ_Q, tm), col),
            pl.BlockSpec((PROJ_KV, tm), col),
            pl.BlockSpec((SSM_WIDTH, tm), col),
            pl.BlockSpec((tm, PROJ_KV), row),
            pl.BlockSpec((tm, PROJ_KV), row),
        ],
        out_shape=[
            jax.ShapeDtypeStruct((PROJ_Q, n), BF16),
            jax.ShapeDtypeStruct((PROJ_KV, n), BF16),
            jax.ShapeDtypeStruct((SSM_WIDTH, n), F32),
            jax.ShapeDtypeStruct((n, PROJ_KV), F32),
            jax.ShapeDtypeStruct((n, PROJ_KV), F32),
        ],
        compiler_params=pltpu.CompilerParams(dimension_semantics=("arbitrary",)),
        name="proj",
    )(x2d, ln_g, ln_b, w_kv, w_qvu_t)


PAIR = 2 * CHUNK
HEAD_LANES = Q_PER_KV * PAIR


def _attend_pair(kwin, vtwin, qt_ref, lanes, valid, sink_ref, o_ref):
    for h in range(N_KV_HEADS):
        base = h * Q_PER_KV * HEAD_DIM
        qrow = jnp.concatenate(
            [qt_ref[base + g * HEAD_DIM:base + (g + 1) * HEAD_DIM, lanes] for g in range(Q_PER_KV)],
            axis=1)
        zero = jnp.zeros_like(qrow)
        qstack = jnp.concatenate([qrow, zero] if h == 0 else [zero, qrow], axis=0)
        s = jnp.dot(kwin, qstack, preferred_element_type=F32)
        s = jnp.where(valid, s, NEG_INF)
        sink = sink_ref[h]
        m = jnp.maximum(jnp.max(s, axis=0, keepdims=True), sink)
        p = jnp.exp(s - m)
        den = jnp.sum(p, axis=0, keepdims=True) + jnp.exp(sink - m)
        w = (p * (1.0 / den)).astype(BF16)
        o = jnp.dot(vtwin[h * HEAD_DIM:(h + 1) * HEAD_DIM, :], w, preferred_element_type=F32)
        for g in range(Q_PER_KV):
            o_ref[base + g * HEAD_DIM:base + (g + 1) * HEAD_DIM, lanes] = (
                o[:, g * PAIR:(g + 1) * PAIR].astype(BF16))


def _attn_prompt_kernel(sink_ref, qt_ref, kp_ref, kc_ref, vtp_ref, vtc_ref, o_ref, *, n_pairs):
    tq = n_pairs * PAIR
    nk = WINDOW + PAIR
    kk = jnp.concatenate([kp_ref[...], kc_ref[...]], axis=0).astype(BF16)
    vt = jnp.concatenate([vtp_ref[...], vtc_ref[...]], axis=1)
    r = lax.broadcasted_iota(jnp.int32, (nk, HEAD_LANES), 0)
    first_chunk = (lax.broadcasted_iota(jnp.int32, (nk, HEAD_LANES), 1) & (PAIR - 1)) < CHUNK
    lo = jnp.where(first_chunk, 0, CHUNK)
    hi = jnp.where(first_chunk, BAND, nk)
    first_pos = pl.program_id(1) * tq - WINDOW
    for pp in range(n_pairs):
        lo_pp = jnp.maximum(lo, -first_pos) if pp == 0 else lo
        valid = (r >= lo_pp) & (r < hi)
        _attend_pair(kk[pp * PAIR:pp * PAIR + nk], vt[:, pp * PAIR:pp * PAIR + nk], qt_ref,
                     slice(pp * PAIR, (pp + 1) * PAIR), valid, sink_ref, o_ref)


def _attn_sample_kernel(sink_ref, qt_ref, ck_ref, kn_ref, cvt_ref, vtn_ref, o_ref, *, n_pairs):
    nk = 2 * WINDOW + PAIR
    r = lax.broadcasted_iota(jnp.int32, (nk, HEAD_LANES), 0)
    query_seq = (lax.broadcasted_iota(jnp.int32, (nk, HEAD_LANES), 1) & (PAIR - 1)) >> 6
    key_seq = jnp.where(r < 2 * WINDOW, r >> 7, (r - 2 * WINDOW) >> 6)
    valid = query_seq == key_seq
    for pp in range(n_pairs):
        lanes = slice(pp * PAIR, (pp + 1) * PAIR)
        kwin = jnp.concatenate([ck_ref[2 * pp], ck_ref[2 * pp + 1], kn_ref[lanes, :]],
                               axis=0).astype(BF16)
        vtwin = jnp.concatenate([cvt_ref[:, 2 * pp * WINDOW:2 * (pp + 1) * WINDOW], vtn_ref[:, lanes]],
                                axis=1)
        _attend_pair(kwin, vtwin, qt_ref, lanes, valid, sink_ref, o_ref)


_SINK_SPEC = pl.BlockSpec((N_KV_HEADS, 1, HEAD_LANES), lambda *_: (0, 0, 0))


def _attention_prompt(sink_rows, qt, k, vt, bsz, seq, n_pairs):
    tq = n_pairs * PAIR
    nt = seq // tq
    wpt = tq // WINDOW
    cur_c = lambda b, i: (0, b * nt + i)
    cur_r = lambda b, i: (b * nt + i, 0)
    prev = lambda b, i: jnp.maximum((b * nt + i) * wpt - 1, 0)
    return pl.pallas_call(
        functools.partial(_attn_prompt_kernel, n_pairs=n_pairs),
        grid=(bsz, nt),
        in_specs=[
            _SINK_SPEC,
            pl.BlockSpec((PROJ_Q, tq), cur_c),
            pl.BlockSpec((WINDOW, PROJ_KV), lambda b, i: (prev(b, i), 0)),
            pl.BlockSpec((tq, PROJ_KV), cur_r),
            pl.BlockSpec((PROJ_KV, WINDOW), lambda b, i: (0, prev(b, i))),
            pl.BlockSpec((PROJ_KV, tq), cur_c),
        ],
        out_specs=pl.BlockSpec((PROJ_Q, tq), cur_c),
        out_shape=jax.ShapeDtypeStruct((PROJ_Q, bsz * seq), BF16),
        compiler_params=pltpu.CompilerParams(dimension_semantics=("arbitrary", "arbitrary")),
        name="attn_prompt",
    )(sink_rows, qt, k, k, vt, vt)


def _attention_sample(sink_rows, qt, cache_k, k, cache_vt, vt, n_pairs):
    n = qt.shape[1]
    tq = n_pairs * PAIR
    return pl.pallas_call(
        functools.partial(_attn_sample_kernel, n_pairs=n_pairs),
        grid=(n // tq,),
        in_specs=[
            _SINK_SPEC,
            pl.BlockSpec((PROJ_Q, tq), lambda i: (0, i)),
            pl.BlockSpec((2 * n_pairs, WINDOW, PROJ_KV), lambda i: (i, 0, 0)),
            pl.BlockSpec((tq, PROJ_KV), lambda i: (i, 0)),
            pl.BlockSpec((PROJ_KV, 2 * n_pairs * WINDOW), lambda i: (0, i)),
            pl.BlockSpec((PROJ_KV, tq), lambda i: (0, i)),
        ],
        out_specs=pl.BlockSpec((PROJ_Q, tq), lambda i: (0, i)),
        out_shape=jax.ShapeDtypeStruct((PROJ_Q, n), BF16),
        compiler_params=pltpu.CompilerParams(dimension_semantics=("arbitrary",)),
        name="attn_sample",
    )(sink_rows, qt, cache_k, k, cache_vt, vt)


def _power_table(tau, nbits, a_re, a_im):
    rows = tau.shape[0]
    w_re = jnp.ones((rows, STATE_LANES), F32)
    w_im = jnp.zeros((rows, STATE_LANES), F32)
    p_re, p_im = a_re, a_im
    for k in range(nbits):
        bit = ((tau >> k) & 1) == 1
        f_re = jnp.where(bit, p_re, 1.0)
        f_im = jnp.where(bit, p_im, 0.0)
        w_re, w_im = w_re * f_re - w_im * f_im, w_re * f_im + w_im * f_re
        p_re, p_im = p_re * p_re - p_im * p_im, 2.0 * p_re * p_im
    return w_re, w_im


def _tables_kernel(lr_ref, li_ref, ls_ref, bre_ref, bim_ref, cre_ref, cim_ref,
                   m_ref, p_ref, qt_ref, coef_ref):
    lo = lax.broadcasted_iota(jnp.int32, (1, STATE_LANES), 1) < SSM_STATE
    lr, li = lr_ref[...], li_ref[...]
    dt = jnp.exp(ls_ref[...])
    mag = jnp.exp(lr * dt)
    a_re, a_im = mag * jnp.cos(li * dt), mag * jnp.sin(li * dt)
    nr, ni = a_re - 1.0, a_im
    den = lr * lr + li * li
    f_re, f_im = (nr * lr + ni * li) / den, (ni * lr - nr * li) / den
    b_re, b_im = bre_ref[...], bim_ref[...]
    bb_re = f_re * b_re - f_im * b_im
    bb_im = f_re * b_im + f_im * b_re
    c_re, c_im = cre_ref[...], cim_ref[...]

    tau = lax.broadcasted_iota(jnp.int32, (SSM_T, 1), 0)
    w_re, w_im = _power_table(tau, 6, a_re, a_im)
    w1_re, w1_im = _power_table(tau + 1, 7, a_re, a_im)
    wr_re, wr_im = _power_table(SSM_T - 1 - tau, 6, a_re, a_im)

    def outer(c, w):
        return (c[:, None, :] * w[None, :, :]).reshape(SSM_CW, STATE_LANES)

    cw_mix = (outer(c_re, jnp.where(lo, w_re, w_im)) + outer(c_im, jnp.where(lo, -w_im, w_re)))
    bb_mix = jnp.where(lo, bb_re, -bb_im)
    strip = lax.dot_general(bb_mix, cw_mix, (((1,), (1,)), ((), ())),
                            precision=lax.Precision.HIGHEST,
                            preferred_element_type=F32)
    s_idx = lax.broadcasted_iota(jnp.int32, (SSM_T, SSM_CW), 0)
    t_idx = lax.broadcasted_iota(jnp.int32, (SSM_T, SSM_CW), 1) & (SSM_T - 1)
    causal = t_idx >= s_idx
    for c in range(SSM_GROUP):
        rows = jnp.broadcast_to(strip[c:c + 1, :], (SSM_T, SSM_CW))
        shifted = pltpu.roll(rows, 0, 1, stride=1, stride_axis=0)
        m_ref[c * SSM_T:(c + 1) * SSM_T, :] = jnp.where(causal, shifted, 0.0).astype(BF16)

    x_a, y_a = jnp.where(lo, bb_re, bb_im), jnp.where(lo, -bb_im, bb_re)
    x_b, y_b = jnp.where(lo, bb_im, bb_re), jnp.where(lo, bb_re, -bb_im)
    p_ref[:, :STATE_LANES] = (outer(x_a, wr_re) + outer(y_a, wr_im)).astype(BF16)
    p_ref[:, STATE_LANES:] = (outer(x_b, wr_re) + outer(y_b, wr_im)).astype(BF16)

    qt_ref[...] = (outer(c_re, jnp.where(lo, w1_re, -w1_im))
                   + outer(c_im, jnp.where(lo, -w1_im, -w1_re))).astype(BF16)

    t_re, t_im = a_re, a_im
    for _ in range(6):
        t_re, t_im = t_re * t_re - t_im * t_im, 2.0 * t_re * t_im
    for k in range(COEF_ROWS // 2):
        coef_ref[2 * k:2 * k + 1, :] = t_re
        coef_ref[2 * k + 1:2 * k + 2, :] = jnp.where(lo, -t_im, t_im)
        t_re, t_im = t_re * t_re - t_im * t_im, 2.0 * t_re * t_im


def _ssm_tables(lam_re, lam_im, log_step, b_re, b_im, c_re, c_im):
    dup = lambda a: jnp.concatenate([a, a], axis=-1)
    lr = dup(lam_re)[:, None, :]
    li = dup(lam_im)[:, None, :]
    ls = jnp.broadcast_to(log_step[:, None, None], (SSM_GROUPS, 1, STATE_LANES))
    bt_re = dup(jnp.swapaxes(b_re, 1, 2))
    bt_im = dup(jnp.swapaxes(b_im, 1, 2))
    c2_re, c2_im = dup(c_re), dup(c_im)
    g3 = lambda g: (g, 0, 0)
    row_spec = pl.BlockSpec((None, 1, STATE_LANES), g3)
    mat_spec = pl.BlockSpec((None, SSM_GROUP, STATE_LANES), g3)
    return pl.pallas_call(
        _tables_kernel,
        grid=(SSM_GROUPS,),
        in_specs=[row_spec, row_spec, row_spec, mat_spec, mat_spec, mat_spec, mat_spec],
        out_specs=[
            pl.BlockSpec((None, SSM_CW, SSM_CW), g3),
            pl.BlockSpec((None, SSM_CW, 2 * STATE_LANES), g3),
            pl.BlockSpec((None, SSM_CW, STATE_LANES), g3),
            pl.BlockSpec((None, COEF_ROWS, STATE_LANES), g3),
        ],
        out_shape=[
            jax.ShapeDtypeStruct((SSM_GROUPS, SSM_CW, SSM_CW), BF16),
            jax.ShapeDtypeStruct((SSM_GROUPS, SSM_CW, 2 * STATE_LANES), BF16),
            jax.ShapeDtypeStruct((SSM_GROUPS, SSM_CW, STATE_LANES), BF16),
            jax.ShapeDtypeStruct((SSM_GROUPS, COEF_ROWS, STATE_LANES), F32),
        ],
        compiler_params=pltpu.CompilerParams(dimension_semantics=("arbitrary",)),
        name="ssm_tables",
    )(lr, li, ls, bt_re, bt_im, c2_re, c2_im)


def _cmul(a1, a2, h, hs):
    return a1 * h + a2 * hs, a1 * hs - a2 * h


def _ssm_kernel(up_ref, us_ref, m_ref, p_ref, qt_ref, coef_ref, h0e_ref, h0o_ref, h0es_ref, h0os_ref,
                yp_ref, ys_ref, hp_ref, hse_ref, hso_ref, *, rows_per_seq):
    lo = lax.broadcasted_iota(jnp.int32, (1, ROW_TOKENS), 1) < SSM_T

    def chunk_rows(ref):
        even, odd = [], []
        for k in range(SSM_GROUP // 2):
            a, b = ref[2 * k], ref[2 * k + 1]
            even.append(jnp.where(lo, a, pltpu.roll(b, SSM_T, 1)))
            odd.append(jnp.where(lo, pltpu.roll(a, SSM_T, 1), b))
        return jnp.concatenate(even, axis=1), jnp.concatenate(odd, axis=1)

    def store_rows(y_even, y_odd, ref):
        for k in range(SSM_GROUP // 2):
            te = y_even[:, k * ROW_TOKENS:(k + 1) * ROW_TOKENS]
            to = y_odd[:, k * ROW_TOKENS:(k + 1) * ROW_TOKENS]
            ref[2 * k] = jnp.where(lo, te, pltpu.roll(to, SSM_T, 1))
            ref[2 * k + 1] = jnp.where(lo, pltpu.roll(te, SSM_T, 1), to)

    pe, po = chunk_rows(up_ref)
    se, so = chunk_rows(us_ref)
    rp, rs = pe.shape[0], se.shape[0]
    u = jnp.concatenate([pe, po, se, so], axis=0).astype(BF16)
    s12 = jnp.dot(u, p_ref[...], preferred_element_type=F32)
    s1, s2 = s12[:, :STATE_LANES], s12[:, STATE_LANES:]
    a1, a2 = coef_ref[0:1, :], coef_ref[1:2, :]

    e1, e2, o1, o2 = s1[:rp], s2[:rp], s1[rp:2 * rp], s2[rp:2 * rp]
    x1, x2 = _cmul(a1, a2, e1, e2)
    x1, x2 = x1 + o1, x2 + o2
    pos = lax.broadcasted_iota(jnp.int32, (rp, STATE_LANES), 0) & (rows_per_seq - 1)
    for k in range(rows_per_seq.bit_length() - 1):
        d = 1 << k
        b1, b2 = coef_ref[2 + 2 * k:3 + 2 * k, :], coef_ref[3 + 2 * k:4 + 2 * k, :]
        sh1 = jnp.where(pos >= d, pltpu.roll(x1, d, 0), 0.0)
        sh2 = jnp.where(pos >= d, pltpu.roll(x2, d, 0), 0.0)
        y1, y2 = _cmul(b1, b2, sh1, sh2)
        x1, x2 = x1 + y1, x2 + y2
    g1 = jnp.where(pos >= 1, pltpu.roll(x1, 1, 0), 0.0)
    g2 = jnp.where(pos >= 1, pltpu.roll(x2, 1, 0), 0.0)
    ho1 = _cmul(a1, a2, g1, g2)[0] + e1
    hp_ref[...] = jnp.zeros(hp_ref.shape, F32)
    for b in range(rp // rows_per_seq):
        last = (b + 1) * rows_per_seq - 1
        hp_ref[b:b + 1, :] = x1[last:last + 1, :]

    h0e, h0o = h0e_ref[...], h0o_ref[...]
    hse_ref[...] = a1 * h0e + a2 * h0es_ref[...] + s1[2 * rp:2 * rp + rs]
    hso_ref[...] = a1 * h0o + a2 * h0os_ref[...] + s1[2 * rp + rs:]

    hprev = jnp.concatenate([g1, ho1, h0e, h0o], axis=0).astype(BF16)
    y = jnp.dot(u, m_ref[...], preferred_element_type=F32)
    y += lax.dot_general(hprev, qt_ref[...], (((1,), (1,)), ((), ())), preferred_element_type=F32)
    store_rows(y[:rp], y[rp:2 * rp], yp_ref)
    store_rows(y[2 * rp:2 * rp + rs], y[2 * rp + rs:], ys_ref)


def _ssm(u3p, u3s, m_tab, p_tab, qt_tab, coef, h0e, h0o, h0es, h0os, *, rows_per_seq):
    rp, rs = u3p.shape[1], u3s.shape[1]
    assert rows_per_seq & (rows_per_seq - 1) == 0 and 2 * rows_per_seq.bit_length() <= COEF_ROWS
    g3 = lambda g: (g, 0, 0)
    u_spec = lambda rows: pl.BlockSpec((SSM_GROUP, rows, ROW_TOKENS), g3)
    h_spec = pl.BlockSpec((None, rs, STATE_LANES), g3)
    return pl.pallas_call(
        functools.partial(_ssm_kernel, rows_per_seq=rows_per_seq),
        grid=(SSM_GROUPS,),
        in_specs=[
            u_spec(rp), u_spec(rs),
            pl.BlockSpec((None, SSM_CW, SSM_CW), g3),
            pl.BlockSpec((None, SSM_CW, 2 * STATE_LANES), g3),
            pl.BlockSpec((None, SSM_CW, STATE_LANES), g3),
            pl.BlockSpec((None, COEF_ROWS, STATE_LANES), g3),
            h_spec, h_spec, h_spec, h_spec,
        ],
        out_specs=[
            u_spec(rp), u_spec(rs),
            pl.BlockSpec((None, STATE_OUT_ROWS, STATE_LANES), g3),
            h_spec, h_spec,
        ],
        out_shape=[
            jax.ShapeDtypeStruct((SSM_WIDTH, rp, ROW_TOKENS), F32),
            jax.ShapeDtypeStruct((SSM_WIDTH, rs, ROW_TOKENS), F32),
            jax.ShapeDtypeStruct((SSM_GROUPS, STATE_OUT_ROWS, STATE_LANES), F32),
            jax.ShapeDtypeStruct((SSM_GROUPS, rs, STATE_LANES), F32),
            jax.ShapeDtypeStruct((SSM_GROUPS, rs, STATE_LANES), F32),
        ],
        compiler_params=pltpu.CompilerParams(dimension_semantics=("arbitrary",)),
        name="ssm",
    )(u3p, u3s, m_tab, p_tab, qt_tab, coef, h0e, h0o, h0es, h0os)


def _post_kernel(x_ref, at_ref, ut_ref, yt_ref, lng_ref, lnb_ref, d_ref, wglut_ref, bglu_ref,
                 wout_ref, ln1g_ref, ln1b_ref, wgu_ref, wdown_ref, ln2g_ref, ln2b_ref, o_ref,
                 *, alpha):
    xn = _layer_norm(x_ref[...], lng_ref[...], lnb_ref[...])
    ys = yt_ref[...] + d_ref[...] * ut_ref[...]
    gl = 0.5 * ys * (1.0 + lax.erf(ys * math.sqrt(0.5)))
    z = jnp.dot(wglut_ref[...], gl.astype(BF16), preferred_element_type=F32) + bglu_ref[...]
    s = gl * jax.nn.sigmoid(z)
    tn = (((0,), (0,)), ((), ()))
    mix = lax.dot_general(at_ref[...], wout_ref[:PROJ_Q, :], tn, preferred_element_type=F32)
    mix += lax.dot_general(s.astype(BF16), wout_ref[PROJ_Q:, :], tn, preferred_element_type=F32)
    h = _layer_norm(alpha * xn + mix, ln1g_ref[...], ln1b_ref[...])
    hb = h.astype(BF16)
    f = jnp.zeros_like(h)
    for j in range(D_FF // FF_BLOCK):
        cols = slice(j * FF_BLOCK, (j + 1) * FF_BLOCK)
        up_cols = slice(D_FF + j * FF_BLOCK, D_FF + (j + 1) * FF_BLOCK)
        g = jnp.dot(hb, wgu_ref[:, cols], preferred_element_type=F32)
        up = jnp.dot(hb, wgu_ref[:, up_cols], preferred_element_type=F32)
        act = (g * jax.nn.sigmoid(g)) * up
        f += jnp.dot(act.astype(BF16), wdown_ref[cols, :], preferred_element_type=F32)
    o_ref[...] = _layer_norm(alpha * h + f, ln2g_ref[...], ln2b_ref[...])


def _post(x2d, at2d, ut2d, yt2d, ln_g, ln_b, d_col, w_glu_t, b_glu_col, w_out, ln1_g, ln1_b,
          w_gate_up, w_down, ln2_g, ln2_b, *, alpha, tm):
    n = x2d.shape[0]
    row = lambda i: (i, 0)
    col = lambda i: (0, i)
    const = lambda i: (0, 0)
    resident = lambda shape: pl.BlockSpec(shape, const, pipeline_mode=pl.Buffered(1))
    vec = lambda width: pl.BlockSpec((1, width), const)
    colvec = pl.BlockSpec((SSM_WIDTH, 1), const)
    return pl.pallas_call(
        functools.partial(_post_kernel, alpha=alpha),
        grid=(n // tm,),
        in_specs=[
            pl.BlockSpec((tm, D_MODEL), row),
            pl.BlockSpec((PROJ_Q, tm), col),
            pl.BlockSpec((SSM_WIDTH, tm), col),
            pl.BlockSpec((SSM_WIDTH, tm), col),
            vec(D_MODEL), vec(D_MODEL), colvec,
            resident((SSM_WIDTH, SSM_WIDTH)), colvec,
            resident((D_MODEL, D_MODEL)), vec(D_MODEL), vec(D_MODEL),
            resident((D_MODEL, 2 * D_FF)), resident((D_FF, D_MODEL)),
            vec(D_MODEL), vec(D_MODEL),
        ],
        out_specs=pl.BlockSpec((tm, D_MODEL), row),
        out_shape=jax.ShapeDtypeStruct((n, D_MODEL), F32),
        compiler_params=pltpu.CompilerParams(
            dimension_semantics=("arbitrary",), vmem_limit_bytes=V7X_VMEM_LIMIT_BYTES),
        name="post",
    )(x2d, at2d, ut2d, yt2d, ln_g, ln_b, d_col, w_glu_t, b_glu_col, w_out, ln1_g, ln1_b,
      w_gate_up, w_down, ln2_g, ln2_b)


def kernel(x_prompt, x_sample, cache_win_k, cache_win_v, state_ssm_re, state_ssm_im,
           ln_in_g, ln_in_b, w_in, attn_sinks, ssm_lambda_re, ssm_lambda_im, ssm_log_step,
           ssm_b_re, ssm_b_im, ssm_c_re, ssm_c_im, ssm_d, w_glu, b_glu, w_out,
           ln1_g, ln1_b, w_gate_up, w_down, ln2_g, ln2_b):
    depth = w_in.shape[0]
    assert depth == 1, "single-layer step"
    bp, lp, _ = x_prompt.shape
    bs, ls, _ = x_sample.shape
    assert ls == SSM_T and bs % 2 == 0 and lp % 512 == 0 and bp <= STATE_OUT_ROWS
    win_rows = cache_win_k.shape[2]
    assert win_rows == WINDOW
    alpha = (2.0 * depth) ** 0.25
    l = 0
    row = lambda a: a.reshape(1, -1)
    column = lambda a: a.reshape(-1, 1)

    w_in_b = w_in[l].astype(BF16)
    w_kv = w_in_b[:, PROJ_Q:PROJ_Q + 2 * PROJ_KV]
    w_qvu_t = jnp.concatenate([w_in_b[:, :PROJ_Q], w_in_b[:, PROJ_Q + PROJ_KV:]], axis=1).T
    lng, lnb = row(ln_in_g), row(ln_in_b)
    xp2 = x_prompt.reshape(bp * lp, D_MODEL)
    xs2 = x_sample.reshape(bs * ls, D_MODEL)
    qtp, vtp, utp, kp, vp = _proj(xp2, lng, lnb, w_kv, w_qvu_t, 512)
    qts, vts, uts, ks, vs = _proj(xs2, lng, lnb, w_kv, w_qvu_t, 512)

    sink_rows = jnp.repeat(attn_sinks[l].reshape(N_KV_HEADS, Q_PER_KV), PAIR, axis=1)[:, None, :]
    atp = _attention_prompt(sink_rows, qtp, kp, vtp, bp, lp, n_pairs=4)
    ck = cache_win_k[l].reshape(bs, win_rows, PROJ_KV)
    cv = cache_win_v[l].reshape(bs, win_rows, PROJ_KV)
    cvt = jnp.transpose(cv, (2, 0, 1)).reshape(PROJ_KV, bs * win_rows).astype(BF16)
    ats = _attention_sample(sink_rows, qts, ck, ks, cvt, vts, n_pairs=4)
    kp3, vp3 = kp.reshape(bp, lp, PROJ_KV), vp.reshape(bp, lp, PROJ_KV)
    ks3, vs3 = ks.reshape(bs, ls, PROJ_KV), vs.reshape(bs, ls, PROJ_KV)

    m_tab, p_tab, qt_tab, coef = _ssm_tables(
        ssm_lambda_re[l], ssm_lambda_im[l], ssm_log_step[l],
        ssm_b_re[l], ssm_b_im[l], ssm_c_re[l], ssm_c_im[l])
    u3p = utp.reshape(SSM_WIDTH, bp * lp // ROW_TOKENS, ROW_TOKENS)
    u3s = uts.reshape(SSM_WIDTH, bs * ls // ROW_TOKENS, ROW_TOKENS)
    sre = jnp.swapaxes(state_ssm_re[l], 0, 1)
    sim = jnp.swapaxes(state_ssm_im[l], 0, 1)
    h0 = jnp.concatenate([sre, sim], axis=-1)
    h0s = jnp.concatenate([sim, sre], axis=-1)
    y3p, y3s, hfin_p, hfin_e, hfin_o = _ssm(
        u3p, u3s, m_tab, p_tab, qt_tab, coef, h0[:, 0::2], h0[:, 1::2], h0s[:, 0::2], h0s[:, 1::2],
        rows_per_seq=lp // ROW_TOKENS)
    ytp = y3p.reshape(SSM_WIDTH, bp * lp)
    yts = y3s.reshape(SSM_WIDTH, bs * ls)
    hfin_s = jnp.stack([hfin_e, hfin_o], axis=2).reshape(SSM_GROUPS, bs, STATE_LANES)

    post_args = (lng, lnb, column(ssm_d[l]), w_glu[l].T.astype(BF16), column(b_glu[l]),
                 w_out[l].astype(BF16), row(ln1_g[l]), row(ln1_b[l]),
                 w_gate_up[l].astype(BF16), w_down[l].astype(BF16), row(ln2_g[l]), row(ln2_b[l]))
    out_p = _post(xp2, atp, utp, ytp, *post_args, alpha=alpha, tm=512)
    out_s = _post(xs2, ats, uts, yts, *post_args, alpha=alpha, tm=512)

    kv_shape = (N_KV_HEADS, HEAD_DIM)
    win_k_p = kp3[:, -win_rows:].reshape(1, bp, win_rows, *kv_shape)
    win_v_p = vp3[:, -win_rows:].reshape(1, bp, win_rows, *kv_shape)
    win_k_s = jnp.concatenate([ck, ks3], axis=1)[:, -win_rows:].reshape(1, bs, win_rows, *kv_shape)
    win_v_s = jnp.concatenate([cv, vs3], axis=1)[:, -win_rows:].reshape(1, bs, win_rows, *kv_shape)
    state = lambda hf, n: jnp.swapaxes(hf[:, :n], 0, 1)
    sp, ss = state(hfin_p, bp), state(hfin_s, bs)
    return (out_p.reshape(bp, lp, D_MODEL), out_s.reshape(bs, ls, D_MODEL),
            win_k_p, win_v_p, sp[None, ..., :SSM_STATE], sp[None, ..., SSM_STATE:],
            win_k_s, win_v_s, ss[None, ..., :SSM_STATE], ss[None, ..., SSM_STATE:])
```

```python
import functools
import math

import jax
import jax.numpy as jnp
from jax import lax
from jax.experimental import pallas as pl
from jax.experimental.pallas import tpu as pltpu

F32 = jnp.float32
BF16 = jnp.bfloat16

D_MODEL = 1024
HEAD_DIM = 64
N_HEADS = 8
N_KV_HEADS = 2
Q_PER_KV = N_HEADS // N_KV_HEADS
CHUNK = 64
WINDOW = 128
WIN_CHUNKS = WINDOW // CHUNK
BAND = (WIN_CHUNKS + 1) * CHUNK
PROJ_Q = N_HEADS * HEAD_DIM
PROJ_KV = N_KV_HEADS * HEAD_DIM
SSM_WIDTH = 512
SSM_GROUP = 16
SSM_GROUPS = SSM_WIDTH // SSM_GROUP
SSM_STATE = 64
STATE_LANES = 2 * SSM_STATE
D_FF = 2816
D_IN_PROJ = PROJ_Q + 2 * PROJ_KV + SSM_WIDTH
LN_EPS = 1e-5
NEG_INF = -1e30

SSM_T = CHUNK
SSM_CW = SSM_T * SSM_GROUP
ROW_TOKENS = 2 * SSM_T
COEF_ROWS = 16
STATE_OUT_ROWS = 8
FF_BLOCK = 256
V7X_VMEM_LIMIT_BYTES = 56 * 1024 * 1024


def _layer_norm(x, g, b):
    mu = jnp.mean(x, axis=-1, keepdims=True)
    xc = x - mu
    var = jnp.mean(xc * xc, axis=-1, keepdims=True)
    return xc * lax.rsqrt(var + LN_EPS) * g + b


KV = 2 * PROJ_KV
QVU = PROJ_Q + PROJ_KV + SSM_WIDTH


def _proj_kernel(x_ref, g_ref, b_ref, w_ref, wt_ref, qt_ref, vt_ref, ut_ref, k_ref, v_ref):
    xb = _layer_norm(x_ref[...], g_ref[...], b_ref[...]).astype(BF16)
    p = jnp.dot(xb, w_ref[...], preferred_element_type=F32)
    k_ref[...] = p[:, :PROJ_KV]
    v_ref[...] = p[:, PROJ_KV:]
    pt = lax.dot_general(wt_ref[...], xb, (((1,), (1,)), ((), ())), preferred_element_type=F32)
    qt_ref[...] = (pt[:PROJ_Q] * (HEAD_DIM ** -0.5)).astype(BF16)
    vt_ref[...] = pt[PROJ_Q:PROJ_Q + PROJ_KV].astype(BF16)
    ut_ref[...] = pt[PROJ_Q + PROJ_KV:]


def _proj(x2d, ln_g, ln_b, w_kv, w_qvu_t, tm):
    n = x2d.shape[0]
    const = lambda i: (0, 0)
    row = lambda i: (i, 0)
    col = lambda i: (0, i)
    return pl.pallas_call(
        _proj_kernel,
        grid=(n // tm,),
        in_specs=[
            pl.BlockSpec((tm, D_MODEL), row),
            pl.BlockSpec((1, D_MODEL), const),
            pl.BlockSpec((1, D_MODEL), const),
            pl.BlockSpec((D_MODEL, KV), const),
            pl.BlockSpec((QVU, D_MODEL), const),
        ],
        out_specs=[
            pl.BlockSpec((PROJ_Q, tm), col),
            pl.BlockSpec((PROJ_KV, tm), col),
            pl.BlockSpec((SSM_WIDTH, tm), col),
            pl.BlockSpec((tm, PROJ_KV), row),
            pl.BlockSpec((tm, PROJ_KV), row),
        ],
        out_shape=[
            jax.ShapeDtypeStruct((PROJ_Q, n), BF16),
            jax.ShapeDtypeStruct((PROJ_KV, n), BF16),
            jax.ShapeDtypeStruct((SSM_WIDTH, n), F32),
            jax.ShapeDtypeStruct((n, PROJ_KV), F32),
            jax.ShapeDtypeStruct((n, PROJ_KV), F32),
        ],
        compiler_params=pltpu.CompilerParams(dimension_semantics=("arbitrary",)),
        name="proj",
    )(x2d, ln_g, ln_b, w_kv, w_qvu_t)


PAIR = 2 * CHUNK
HEAD_LANES = Q_PER_KV * PAIR


def _attend_pair(kwin, vtwin, qt_ref, lanes, valid, sink_ref, o_ref):
    for h in range(N_KV_HEADS):
        base = h * Q_PER_KV * HEAD_DIM
        qrow = jnp.concatenate(
            [qt_ref[base + g * HEAD_DIM:base + (g + 1) * HEAD_DIM, lanes] for g in range(Q_PER_KV)],
            axis=1)
        zero = jnp.zeros_like(qrow)
        qstack = jnp.concatenate([qrow, zero] if h == 0 else [zero, qrow], axis=0)
        s = jnp.dot(kwin, qstack, preferred_element_type=F32)
        s = jnp.where(valid, s, NEG_INF)
        sink = sink_ref[h]
        m = jnp.maximum(jnp.max(s, axis=0, keepdims=True), sink)
        p = jnp.exp(s - m)
        den = jnp.sum(p, axis=0, keepdims=True) + jnp.exp(sink - m)
        w = (p * (1.0 / den)).astype(BF16)
        o = jnp.dot(vtwin[h * HEAD_DIM:(h + 1) * HEAD_DIM, :], w, preferred_element_type=F32)
        for g in range(Q_PER_KV):
            o_ref[base + g * HEAD_DIM:base + (g + 1) * HEAD_DIM, lanes] = (
                o[:, g * PAIR:(g + 1) * PAIR].astype(BF16))


def _attn_prompt_kernel(sink_ref, qt_ref, kp_ref, kc_ref, vtp_ref, vtc_ref, o_ref, *, n_pairs):
    tq = n_pairs * PAIR
    nk = WINDOW + PAIR
    kk = jnp.concatenate([kp_ref[...], kc_ref[...]], axis=0).astype(BF16)
    vt = jnp.concatenate([vtp_ref[...], vtc_ref[...]], axis=1)
    r = lax.broadcasted_iota(jnp.int32, (nk, HEAD_LANES), 0)
    first_chunk = (lax.broadcasted_iota(jnp.int32, (nk, HEAD_LANES), 1) & (PAIR - 1)) < CHUNK
    lo = jnp.where(first_chunk, 0, CHUNK)
    hi = jnp.where(first_chunk, BAND, nk)
    first_pos = pl.program_id(1) * tq - WINDOW
    for pp in range(n_pairs):
        lo_pp = jnp.maximum(lo, -first_pos) if pp == 0 else lo
        valid = (r >= lo_pp) & (r < hi)
        _attend_pair(kk[pp * PAIR:pp * PAIR + nk], vt[:, pp * PAIR:pp * PAIR + nk], qt_ref,
                     slice(pp * PAIR, (pp + 1) * PAIR), valid, sink_ref, o_ref)


def _attn_sample_kernel(sink_ref, qt_ref, ck_ref, kn_ref, cvt_ref, vtn_ref, o_ref, *, n_pairs):
    nk = 2 * WINDOW + PAIR
    r = lax.broadcasted_iota(jnp.int32, (nk, HEAD_LANES), 0)
    query_seq = (lax.broadcasted_iota(jnp.int32, (nk, HEAD_LANES), 1) & (PAIR - 1)) >> 6
    key_seq = jnp.where(r < 2 * WINDOW, r >> 7, (r - 2 * WINDOW) >> 6)
    valid = query_seq == key_seq
    for pp in range(n_pairs):
        lanes = slice(pp * PAIR, (pp + 1) * PAIR)
        kwin = jnp.concatenate([ck_ref[2 * pp], ck_ref[2 * pp + 1], kn_ref[lanes, :]],
                               axis=0).astype(BF16)
        vtwin = jnp.concatenate([cvt_ref[:, 2 * pp * WINDOW:2 * (pp + 1) * WINDOW], vtn_ref[:, lanes]],
                                axis=1)
        _attend_pair(kwin, vtwin, qt_ref, lanes, valid, sink_ref, o_ref)


_SINK_SPEC = pl.BlockSpec((N_KV_HEADS, 1, HEAD_LANES), lambda *_: (0, 0, 0))


def _attention_prompt(sink_rows, qt, k, vt, bsz, seq, n_pairs):
    tq = n_pairs * PAIR
    nt = seq // tq
    wpt = tq // WINDOW
    cur_c = lambda b, i: (0, b * nt + i)
    cur_r = lambda b, i: (b * nt + i, 0)
    prev = lambda b, i: jnp.maximum((b * nt + i) * wpt - 1, 0)
    return pl.pallas_call(
        functools.partial(_attn_prompt_kernel, n_pairs=n_pairs),
        grid=(bsz, nt),
        in_specs=[
            _SINK_SPEC,
            pl.BlockSpec((PROJ_Q, tq), cur_c),
            pl.BlockSpec((WINDOW, PROJ_KV), lambda b, i: (prev(b, i), 0)),
            pl.BlockSpec((tq, PROJ_KV), cur_r),
            pl.BlockSpec((PROJ_KV, WINDOW), lambda b, i: (0, prev(b, i))),
            pl.BlockSpec((PROJ_KV, tq), cur_c),
        ],
        out_specs=pl.BlockSpec((PROJ_Q, tq), cur_c),
        out_shape=jax.ShapeDtypeStruct((PROJ_Q, bsz * seq), BF16),
        compiler_params=pltpu.CompilerParams(dimension_semantics=("arbitrary", "arbitrary")),
        name="attn_prompt",
    )(sink_rows, qt, k, k, vt, vt)


def _attention_sample(sink_rows, qt, cache_k, k, cache_vt, vt, n_pairs):
    n = qt.shape[1]
    tq = n_pairs * PAIR
    return pl.pallas_call(
        functools.partial(_attn_sample_kernel, n_pairs=n_pairs),
        grid=(n // tq,),
        in_specs=[
            _SINK_SPEC,
            pl.BlockSpec((PROJ_Q, tq), lambda i: (0, i)),
            pl.BlockSpec((2 * n_pairs, WINDOW, PROJ_KV), lambda i: (i, 0, 0)),
            pl.BlockSpec((tq, PROJ_KV), lambda i: (i, 0)),
            pl.BlockSpec((PROJ_KV, 2 * n_pairs * WINDOW), lambda i: (0, i)),
            pl.BlockSpec((PROJ_KV, tq), lambda i: (0, i)),
        ],
        out_specs=pl.BlockSpec((PROJ_Q, tq), lambda i: (0, i)),
        out_shape=jax.ShapeDtypeStruct((PROJ_Q, n), BF16),
        compiler_params=pltpu.CompilerParams(dimension_semantics=("arbitrary",)),
        name="attn_sample",
    )(sink_rows, qt, cache_k, k, cache_vt, vt)


def _power_table(tau, nbits, a_re, a_im):
    rows = tau.shape[0]
    w_re = jnp.ones((rows, STATE_LANES), F32)
    w_im = jnp.zeros((rows, STATE_LANES), F32)
    p_re, p_im = a_re, a_im
    for k in range(nbits):
        bit = ((tau >> k) & 1) == 1
        f_re = jnp.where(bit, p_re, 1.0)
        f_im = jnp.where(bit, p_im, 0.0)
        w_re, w_im = w_re * f_re - w_im * f_im, w_re * f_im + w_im * f_re
        p_re, p_im = p_re * p_re - p_im * p_im, 2.0 * p_re * p_im
    return w_re, w_im


def _build_tables(lr_ref, li_ref, ls_ref, bre_ref, bim_ref, cre_ref, cim_ref,
                  m_ref, p_ref, qt_ref, coef_ref):
    lo = lax.broadcasted_iota(jnp.int32, (1, STATE_LANES), 1) < SSM_STATE
    lr, li = lr_ref[...], li_ref[...]
    dt = jnp.exp(ls_ref[...])
    mag = jnp.exp(lr * dt)
    a_re, a_im = mag * jnp.cos(li * dt), mag * jnp.sin(li * dt)
    nr, ni = a_re - 1.0, a_im
    den = lr * lr + li * li
    f_re, f_im = (nr * lr + ni * li) / den, (ni * lr - nr * li) / den
    b_re, b_im = bre_ref[...], bim_ref[...]
    bb_re = f_re * b_re - f_im * b_im
    bb_im = f_re * b_im + f_im * b_re
    c_re, c_im = cre_ref[...], cim_ref[...]

    tau = lax.broadcasted_iota(jnp.int32, (SSM_T, 1), 0)
    w_re, w_im = _power_table(tau, 6, a_re, a_im)
    w1_re, w1_im = _power_table(tau + 1, 7, a_re, a_im)
    wr_re, wr_im = _power_table(SSM_T - 1 - tau, 6, a_re, a_im)

    def outer(c, w):
        return (c[:, None, :] * w[None, :, :]).reshape(SSM_CW, STATE_LANES)

    cw_mix = (outer(c_re, jnp.where(lo, w_re, w_im)) + outer(c_im, jnp.where(lo, -w_im, w_re)))
    bb_mix = jnp.where(lo, bb_re, -bb_im)
    strip = lax.dot_general(bb_mix, cw_mix, (((1,), (1,)), ((), ())),
                            precision=lax.Precision.HIGHEST,
                            preferred_element_type=F32)
    s_idx = lax.broadcasted_iota(jnp.int32, (SSM_T, SSM_CW), 0)
    t_idx = lax.broadcasted_iota(jnp.int32, (SSM_T, SSM_CW), 1) & (SSM_T - 1)
    causal = t_idx >= s_idx
    for c in range(SSM_GROUP):
        rows = jnp.broadcast_to(strip[c:c + 1, :], (SSM_T, SSM_CW))
        shifted = pltpu.roll(rows, 0, 1, stride=1, stride_axis=0)
        m_ref[c * SSM_T:(c + 1) * SSM_T, :] = jnp.where(causal, shifted, 0.0).astype(BF16)

    x_a, y_a = jnp.where(lo, bb_re, bb_im), jnp.where(lo, -bb_im, bb_re)
    x_b, y_b = jnp.where(lo, bb_im, bb_re), jnp.where(lo, bb_re, -bb_im)
    p_ref[:, :STATE_LANES] = (outer(x_a, wr_re) + outer(y_a, wr_im)).astype(BF16)
    p_ref[:, STATE_LANES:] = (outer(x_b, wr_re) + outer(y_b, wr_im)).astype(BF16)

    qt_ref[...] = (outer(c_re, jnp.where(lo, w1_re, -w1_im))
                   + outer(c_im, jnp.where(lo, -w1_im, -w1_re))).astype(BF16)

    t_re, t_im = a_re, a_im
    for _ in range(6):
        t_re, t_im = t_re * t_re - t_im * t_im, 2.0 * t_re * t_im
    for k in range(COEF_ROWS // 2):
        coef_ref[2 * k:2 * k + 1, :] = t_re
        coef_ref[2 * k + 1:2 * k + 2, :] = jnp.where(lo, -t_im, t_im)
        t_re, t_im = t_re * t_re - t_im * t_im, 2.0 * t_re * t_im


def _table_params(lam_re, lam_im, log_step, b_re, b_im, c_re, c_im):
    dup = lambda a: jnp.concatenate([a, a], axis=-1)
    lr = dup(lam_re)[:, None, :]
    li = dup(lam_im)[:, None, :]
    ls = jnp.broadcast_to(log_step[:, None, None], (SSM_GROUPS, 1, STATE_LANES))
    bt_re = dup(jnp.swapaxes(b_re, 1, 2))
    bt_im = dup(jnp.swapaxes(b_im, 1, 2))
    return lr, li, ls, bt_re, bt_im, dup(c_re), dup(c_im)


def _cmul(a1, a2, h, hs):
    return a1 * h + a2 * hs, a1 * hs - a2 * h


def _ssm_kernel(lr_ref, li_ref, ls_ref, bre_ref, bim_ref, cre_ref, cim_ref,
                up_ref, us_ref, h0e_ref, h0o_ref, h0es_ref, h0os_ref,
                yp_ref, ys_ref, hp_ref, hse_ref, hso_ref,
                m_ref, p_ref, qt_ref, coef_ref, *, rows_per_seq):
    _build_tables(lr_ref, li_ref, ls_ref, bre_ref, bim_ref, cre_ref, cim_ref,
                  m_ref, p_ref, qt_ref, coef_ref)
    lo = lax.broadcasted_iota(jnp.int32, (1, ROW_TOKENS), 1) < SSM_T

    def chunk_rows(ref):
        even, odd = [], []
        for k in range(SSM_GROUP // 2):
            a, b = ref[2 * k], ref[2 * k + 1]
            even.append(jnp.where(lo, a, pltpu.roll(b, SSM_T, 1)))
            odd.append(jnp.where(lo, pltpu.roll(a, SSM_T, 1), b))
        return jnp.concatenate(even, axis=1), jnp.concatenate(odd, axis=1)

    def store_rows(y_even, y_odd, ref):
        for k in range(SSM_GROUP // 2):
            te = y_even[:, k * ROW_TOKENS:(k + 1) * ROW_TOKENS]
            to = y_odd[:, k * ROW_TOKENS:(k + 1) * ROW_TOKENS]
            ref[2 * k] = jnp.where(lo, te, pltpu.roll(to, SSM_T, 1))
            ref[2 * k + 1] = jnp.where(lo, pltpu.roll(te, SSM_T, 1), to)

    pe, po = chunk_rows(up_ref)
    se, so = chunk_rows(us_ref)
    rp, rs = pe.shape[0], se.shape[0]
    u = jnp.concatenate([pe, po, se, so], axis=0).astype(BF16)
    s12 = jnp.dot(u, p_ref[...], preferred_element_type=F32)
    s1, s2 = s12[:, :STATE_LANES], s12[:, STATE_LANES:]
    a1, a2 = coef_ref[0:1, :], coef_ref[1:2, :]

    e1, e2, o1, o2 = s1[:rp], s2[:rp], s1[rp:2 * rp], s2[rp:2 * rp]
    x1, x2 = _cmul(a1, a2, e1, e2)
    x1, x2 = x1 + o1, x2 + o2
    pos = lax.broadcasted_iota(jnp.int32, (rp, STATE_LANES), 0) & (rows_per_seq - 1)
    for k in range(rows_per_seq.bit_length() - 1):
        d = 1 << k
        b1, b2 = coef_ref[2 + 2 * k:3 + 2 * k, :], coef_ref[3 + 2 * k:4 + 2 * k, :]
        sh1 = jnp.where(pos >= d, pltpu.roll(x1, d, 0), 0.0)
        sh2 = jnp.where(pos >= d, pltpu.roll(x2, d, 0), 0.0)
        y1, y2 = _cmul(b1, b2, sh1, sh2)
        x1, x2 = x1 + y1, x2 + y2
    g1 = jnp.where(pos >= 1, pltpu.roll(x1, 1, 0), 0.0)
    g2 = jnp.where(pos >= 1, pltpu.roll(x2, 1, 0), 0.0)
    ho1 = _cmul(a1, a2, g1, g2)[0] + e1
    hp_ref[...] = jnp.zeros(hp_ref.shape, F32)
    for b in range(rp // rows_per_seq):
        last = (b + 1) * rows_per_seq - 1
        hp_ref[b:b + 1, :] = x1[last:last + 1, :]

    h0e, h0o = h0e_ref[...], h0o_ref[...]
    hse_ref[...] = a1 * h0e + a2 * h0es_ref[...] + s1[2 * rp:2 * rp + rs]
    hso_ref[...] = a1 * h0o + a2 * h0os_ref[...] + s1[2 * rp + rs:]

    hprev = jnp.concatenate([g1, ho1, h0e, h0o], axis=0).astype(BF16)
    y = jnp.dot(u, m_ref[...], preferred_element_type=F32)
    y += lax.dot_general(hprev, qt_ref[...], (((1,), (1,)), ((), ())), preferred_element_type=F32)
    store_rows(y[:rp], y[rp:2 * rp], yp_ref)
    store_rows(y[2 * rp:2 * rp + rs], y[2 * rp + rs:], ys_ref)


def _ssm(table_params, u3p, u3s, h0e, h0o, h0es, h0os, *, rows_per_seq):
    rp, rs = u3p.shape[1], u3s.shape[1]
    assert rows_per_seq & (rows_per_seq - 1) == 0 and 2 * rows_per_seq.bit_length() <= COEF_ROWS
    g3 = lambda g: (g, 0, 0)
    u_spec = lambda rows: pl.BlockSpec((SSM_GROUP, rows, ROW_TOKENS), g3)
    h_spec = pl.BlockSpec((None, rs, STATE_LANES), g3)
    row_spec = pl.BlockSpec((None, 1, STATE_LANES), g3)
    mat_spec = pl.BlockSpec((None, SSM_GROUP, STATE_LANES), g3)
    return pl.pallas_call(
        functools.partial(_ssm_kernel, rows_per_seq=rows_per_seq),
        grid=(SSM_GROUPS,),
        in_specs=[
            row_spec, row_spec, row_spec, mat_spec, mat_spec, mat_spec, mat_spec,
            u_spec(rp), u_spec(rs),
            h_spec, h_spec, h_spec, h_spec,
        ],
        out_specs=[
            u_spec(rp), u_spec(rs),
            pl.BlockSpec((None, STATE_OUT_ROWS, STATE_LANES), g3),
            h_spec, h_spec,
        ],
        out_shape=[
            jax.ShapeDtypeStruct((SSM_WIDTH, rp, ROW_TOKENS), F32),
            jax.ShapeDtypeStruct((SSM_WIDTH, rs, ROW_TOKENS), F32),
            jax.ShapeDtypeStruct((SSM_GROUPS, STATE_OUT_ROWS, STATE_LANES), F32),
            jax.ShapeDtypeStruct((SSM_GROUPS, rs, STATE_LANES), F32),
            jax.ShapeDtypeStruct((SSM_GROUPS, rs, STATE_LANES), F32),
        ],
        scratch_shapes=[
            pltpu.VMEM((SSM_CW, SSM_CW), BF16),
            pltpu.VMEM((SSM_CW, 2 * STATE_LANES), BF16),
            pltpu.VMEM((SSM_CW, STATE_LANES), BF16),
            pltpu.VMEM((COEF_ROWS, STATE_LANES), F32),
        ],
        compiler_params=pltpu.CompilerParams(dimension_semantics=("arbitrary",)),
        name="ssm",
    )(*table_params, u3p, u3s, h0e, h0o, h0es, h0os)


def _post_kernel(x_ref, at_ref, ut_ref, yt_ref, lng_ref, lnb_ref, d_ref, wglut_ref, bglu_ref,
                 wout_ref, ln1g_ref, ln1b_ref, wgu_ref, wdown_ref, ln2g_ref, ln2b_ref, o_ref,
                 *, alpha):
    xn = _layer_norm(x_ref[...], lng_ref[...], lnb_ref[...])
    ys = yt_ref[...] + d_ref[...] * ut_ref[...]
    gl = 0.5 * ys * (1.0 + lax.erf(ys * math.sqrt(0.5)))
    z = jnp.dot(wglut_ref[...], gl.astype(BF16), preferred_element_type=F32) + bglu_ref[...]
    s = gl * jax.nn.sigmoid(z)
    tn = (((0,), (0,)), ((), ()))
    mix = lax.dot_general(at_ref[...], wout_ref[:PROJ_Q, :], tn, preferred_element_type=F32)
    mix += lax.dot_general(s.astype(BF16), wout_ref[PROJ_Q:, :], tn, preferred_element_type=F32)
    h = _layer_norm(alpha * xn + mix, ln1g_ref[...], ln1b_ref[...])
    hb = h.astype(BF16)
    f = jnp.zeros_like(h)
    for j in range(D_FF // FF_BLOCK):
        cols = slice(j * FF_BLOCK, (j + 1) * FF_BLOCK)
        up_cols = slice(D_FF + j * FF_BLOCK, D_FF + (j + 1) * FF_BLOCK)
        g = jnp.dot(hb, wgu_ref[:, cols], preferred_element_type=F32)
        up = jnp.dot(hb, wgu_ref[:, up_cols], preferred_element_type=F32)
        act = (g * jax.nn.sigmoid(g)) * up
        f += jnp.dot(act.astype(BF16), wdown_ref[cols, :], preferred_element_type=F32)
    o_ref[...] = _layer_norm(alpha * h + f, ln2g_ref[...], ln2b_ref[...])


def _post(x2d, at2d, ut2d, yt2d, ln_g, ln_b, d_col, w_glu_t, b_glu_col, w_out, ln1_g, ln1_b,
          w_gate_up, w_down, ln2_g, ln2_b, *, alpha, tm):
    n = x2d.shape[0]
    row = lambda i: (i, 0)
    col = lambda i: (0, i)
    const = lambda i: (0, 0)
    resident = lambda shape: pl.BlockSpec(shape, const, pipeline_mode=pl.Buffered(1))
    vec = lambda width: pl.BlockSpec((1, width), const)
    colvec = pl.BlockSpec((SSM_WIDTH, 1), const)
    return pl.pallas_call(
        functools.partial(_post_kernel, alpha=alpha),
        grid=(n // tm,),
        in_specs=[
            pl.BlockSpec((tm, D_MODEL), row),
            pl.BlockSpec((PROJ_Q, tm), col),
            pl.BlockSpec((SSM_WIDTH, tm), col),
            pl.BlockSpec((SSM_WIDTH, tm), col),
            vec(D_MODEL), vec(D_MODEL), colvec,
            resident((SSM_WIDTH, SSM_WIDTH)), colvec,
            resident((D_MODEL, D_MODEL)), vec(D_MODEL), vec(D_MODEL),
            resident((D_MODEL, 2 * D_FF)), resident((D_FF, D_MODEL)),
            vec(D_MODEL), vec(D_MODEL),
        ],
        out_specs=pl.BlockSpec((tm, D_MODEL), row),
        out_shape=jax.ShapeDtypeStruct((n, D_MODEL), F32),
        compiler_params=pltpu.CompilerParams(
            dimension_semantics=("arbitrary",), vmem_limit_bytes=V7X_VMEM_LIMIT_BYTES),
        name="post",
    )(x2d, at2d, ut2d, yt2d, ln_g, ln_b, d_col, w_glu_t, b_glu_col, w_out, ln1_g, ln1_b,
      w_gate_up, w_down, ln2_g, ln2_b)


def kernel(x_prompt, x_sample, cache_win_k, cache_win_v, state_ssm_re, state_ssm_im,
           ln_in_g, ln_in_b, w_in, attn_sinks, ssm_lambda_re, ssm_lambda_im, ssm_log_step,
           ssm_b_re, ssm_b_im, ssm_c_re, ssm_c_im, ssm_d, w_glu, b_glu, w_out,
           ln1_g, ln1_b, w_gate_up, w_down, ln2_g, ln2_b):
    depth = w_in.shape[0]
    assert depth == 1, "single-layer step"
    bp, lp, _ = x_prompt.shape
    bs, ls, _ = x_sample.shape
    assert ls == SSM_T and bs % 2 == 0 and lp % 512 == 0 and bp <= STATE_OUT_ROWS
    win_rows = cache_win_k.shape[2]
    assert win_rows == WINDOW
    alpha = (2.0 * depth) ** 0.25
    l = 0
    row = lambda a: a.reshape(1, -1)
    column = lambda a: a.reshape(-1, 1)

    w_in_b = w_in[l].astype(BF16)
    w_kv = w_in_b[:, PROJ_Q:PROJ_Q + 2 * PROJ_KV]
    w_qvu_t = jnp.concatenate([w_in_b[:, :PROJ_Q], w_in_b[:, PROJ_Q + PROJ_KV:]], axis=1).T
    lng, lnb = row(ln_in_g), row(ln_in_b)
    xp2 = x_prompt.reshape(bp * lp, D_MODEL)
    xs2 = x_sample.reshape(bs * ls, D_MODEL)
    qtp, vtp, utp, kp, vp = _proj(xp2, lng, lnb, w_kv, w_qvu_t, 512)
    qts, vts, uts, ks, vs = _proj(xs2, lng, lnb, w_kv, w_qvu_t, 512)

    sink_rows = jnp.repeat(attn_sinks[l].reshape(N_KV_HEADS, Q_PER_KV), PAIR, axis=1)[:, None, :]
    atp = _attention_prompt(sink_rows, qtp, kp, vtp, bp, lp, n_pairs=4)
    ck = cache_win_k[l].reshape(bs, win_rows, PROJ_KV)
    cv = cache_win_v[l].reshape(bs, win_rows, PROJ_KV)
    cvt = jnp.transpose(cv, (2, 0, 1)).reshape(PROJ_KV, bs * win_rows).astype(BF16)
    ats = _attention_sample(sink_rows, qts, ck, ks, cvt, vts, n_pairs=4)
    kp3, vp3 = kp.reshape(bp, lp, PROJ_KV), vp.reshape(bp, lp, PROJ_KV)
    ks3, vs3 = ks.reshape(bs, ls, PROJ_KV), vs.reshape(bs, ls, PROJ_KV)

    table_params = _table_params(
        ssm_lambda_re[l], ssm_lambda_im[l], ssm_log_step[l],
        ssm_b_re[l], ssm_b_im[l], ssm_c_re[l], ssm_c_im[l])
    u3p = utp.reshape(SSM_WIDTH, bp * lp // ROW_TOKENS, ROW_TOKENS)
    u3s = uts.reshape(SSM_WIDTH, bs * ls // ROW_TOKENS, ROW_TOKENS)
    sre = jnp.swapaxes(state_ssm_re[l], 0, 1)
    sim = jnp.swapaxes(state_ssm_im[l], 0, 1)
    h0 = jnp.concatenate([sre, sim], axis=-1)
    h0s = jnp.concatenate([sim, sre], axis=-1)
    y3p, y3s, hfin_p, hfin_e, hfin_o = _ssm(
        table_params, u3p, u3s, h0[:, 0::2], h0[:, 1::2], h0s[:, 0::2], h0s[:, 1::2],
        rows_per_seq=lp // ROW_TOKENS)
    ytp = y3p.reshape(SSM_WIDTH, bp * lp)
    yts = y3s.reshape(SSM_WIDTH, bs * ls)
    hfin_s = jnp.stack([hfin_e, hfin_o], axis=2).reshape(SSM_GROUPS, bs, STATE_LANES)

    post_args = (lng, lnb, column(ssm_d[l]), w_glu[l].T.astype(BF16), column(b_glu[l]),
                 w_out[l].astype(BF16), row(ln1_g[l]), row(ln1_b[l]),
                 w_gate_up[l].astype(BF16), w_down[l].astype(BF16), row(ln2_g[l]), row(ln2_b[l]))
    out_p = _post(xp2, atp, utp, ytp, *post_args, alpha=alpha, tm=512)
    out_s = _post(xs2, ats, uts, yts, *post_args, alpha=alpha, tm=512)

    kv_shape = (N_KV_HEADS, HEAD_DIM)
    win_k_p = kp3[:, -win_rows:].reshape(1, bp, win_rows, *kv_shape)
    win_v_p = vp3[:, -win_rows:].reshape(1, bp, win_rows, *kv_shape)
    win_k_s = jnp.concatenate([ck, ks3], axis=1)[:, -win_rows:].reshape(1, bs, win_rows, *kv_shape)
    win_v_s = jnp.concatenate([cv, vs3], axis=1)[:, -win_rows:].reshape(1, bs, win_rows, *kv_shape)
    state = lambda hf, n: jnp.swapaxes(hf[:, :n], 0, 1)
    sp, ss = state(hfin_p, bp), state(hfin_s, bs)
    return (out_p.reshape(bp, lp, D_MODEL), out_s.reshape(bs, ls, D_MODEL),
            win_k_p, win_v_p, sp[None, ..., :SSM_STATE], sp[None, ..., SSM_STATE:],
            win_k_s, win_v_s, ss[None, ..., :SSM_STATE], ss[None, ..., SSM_STATE:])
```

```python
import functools
import math

import jax
import jax.numpy as jnp
from jax import lax
from jax.experimental import pallas as pl
from jax.experimental.pallas import tpu as pltpu

F32 = jnp.float32
BF16 = jnp.bfloat16

D_MODEL = 1024
HEAD_DIM = 64
N_HEADS = 8
N_KV_HEADS = 2
Q_PER_KV = N_HEADS // N_KV_HEADS
CHUNK = 64
WINDOW = 128
WIN_CHUNKS = WINDOW // CHUNK
BAND = (WIN_CHUNKS + 1) * CHUNK
PROJ_Q = N_HEADS * HEAD_DIM
PROJ_KV = N_KV_HEADS * HEAD_DIM
SSM_WIDTH = 512
SSM_GROUP = 16
SSM_GROUPS = SSM_WIDTH // SSM_GROUP
SSM_STATE = 64
STATE_LANES = 2 * SSM_STATE
D_FF = 2816
D_IN_PROJ = PROJ_Q + 2 * PROJ_KV + SSM_WIDTH
LN_EPS = 1e-5
NEG_INF = -1e30

SSM_T = CHUNK
SSM_CW = SSM_T * SSM_GROUP
ROW_TOKENS = 2 * SSM_T
COEF_ROWS = 16
STATE_OUT_ROWS = 8
FF_BLOCK = 256
V7X_VMEM_LIMIT_BYTES = 56 * 1024 * 1024


def _layer_norm(x, g, b):
    mu = jnp.mean(x, axis=-1, keepdims=True)
    xc = x - mu
    var = jnp.mean(xc * xc, axis=-1, keepdims=True)
    return xc * lax.rsqrt(var + LN_EPS) * g + b


KV = 2 * PROJ_KV
QVU = PROJ_Q + PROJ_KV + SSM_WIDTH


def _proj_kernel(x_ref, g_ref, b_ref, w_ref, wt_ref, qt_ref, vt_ref, ut_ref, k_ref, v_ref):
    xb = _layer_norm(x_ref[...], g_ref[...], b_ref[...]).astype(BF16)
    p = jnp.dot(xb, w_ref[...], preferred_element_type=F32)
    k_ref[...] = p[:, :PROJ_KV]
    v_ref[...] = p[:, PROJ_KV:]
    pt = lax.dot_general(wt_ref[...], xb, (((1,), (1,)), ((), ())), preferred_element_type=F32)
    qt_ref[...] = (pt[:PROJ_Q] * (HEAD_DIM ** -0.5)).astype(BF16)
    vt_ref[...] = pt[PROJ_Q:PROJ_Q + PROJ_KV].astype(BF16)
    for j in range(ut_ref.shape[0]):
        ut_ref[j] = pt[PROJ_Q + PROJ_KV:, j * ROW_TOKENS:(j + 1) * ROW_TOKENS]


def _proj(x2d, ln_g, ln_b, w_kv, w_qvu_t, tm):
    n = x2d.shape[0]
    const = lambda i: (0, 0)
    row = lambda i: (i, 0)
    col = lambda i: (0, i)
    return pl.pallas_call(
        _proj_kernel,
        grid=(n // tm,),
        in_specs=[
            pl.BlockSpec((tm, D_MODEL), row),
            pl.BlockSpec((1, D_MODEL), const),
            pl.BlockSpec((1, D_MODEL), const),
            pl.BlockSpec((D_MODEL, KV), const),
            pl.BlockSpec((QVU, D_MODEL), const),
        ],
        out_specs=[
            pl.BlockSpec((PROJ_Q, tm), col),
            pl.BlockSpec((PROJ_KV, tm), col),
            pl.BlockSpec((tm // ROW_TOKENS, SSM_WIDTH, ROW_TOKENS), lambda i: (i, 0, 0)),
            pl.BlockSpec((tm, PROJ_KV), row),
            pl.BlockSpec((tm, PROJ_KV), row),
        ],
        out_shape=[
            jax.ShapeDtypeStruct((PROJ_Q, n), BF16),
            jax.ShapeDtypeStruct((PROJ_KV, n), BF16),
            jax.ShapeDtypeStruct((n // ROW_TOKENS, SSM_WIDTH, ROW_TOKENS), F32),
            jax.ShapeDtypeStruct((n, PROJ_KV), F32),
            jax.ShapeDtypeStruct((n, PROJ_KV), F32),
        ],
        compiler_params=pltpu.CompilerParams(dimension_semantics=("arbitrary",)),
        name="proj",
    )(x2d, ln_g, ln_b, w_kv, w_qvu_t)


PAIR = 2 * CHUNK
HEAD_LANES = Q_PER_KV * PAIR


def _attend_pair(kwin, vtwin, qt_ref, lanes, valid, sink_ref, o_ref):
    for h in range(N_KV_HEADS):
        base = h * Q_PER_KV * HEAD_DIM
        qrow = jnp.concatenate(
            [qt_ref[base + g * HEAD_DIM:base + (g + 1) * HEAD_DIM, lanes] for g in range(Q_PER_KV)],
            axis=1)
        zero = jnp.zeros_like(qrow)
        qstack = jnp.concatenate([qrow, zero] if h == 0 else [zero, qrow], axis=0)
        s = jnp.dot(kwin, qstack, preferred_element_type=F32)
        s = jnp.where(valid, s, NEG_INF)
        sink = sink_ref[h]
        m = jnp.maximum(jnp.max(s, axis=0, keepdims=True), sink)
        p = jnp.exp(s - m)
        den = jnp.sum(p, axis=0, keepdims=True) + jnp.exp(sink - m)
        w = (p * (1.0 / den)).astype(BF16)
        o = jnp.dot(vtwin[h * HEAD_DIM:(h + 1) * HEAD_DIM, :], w, preferred_element_type=F32)
        for g in range(Q_PER_KV):
            o_ref[base + g * HEAD_DIM:base + (g + 1) * HEAD_DIM, lanes] = (
                o[:, g * PAIR:(g + 1) * PAIR].astype(BF16))


def _attn_prompt_kernel(sink_ref, qt_ref, kp_ref, kc_ref, vtp_ref, vtc_ref, o_ref, *, n_pairs):
    tq = n_pairs * PAIR
    nk = WINDOW + PAIR
    kk = jnp.concatenate([kp_ref[...], kc_ref[...]], axis=0).astype(BF16)
    vt = jnp.concatenate([vtp_ref[...], vtc_ref[...]], axis=1)
    r = lax.broadcasted_iota(jnp.int32, (nk, HEAD_LANES), 0)
    first_chunk = (lax.broadcasted_iota(jnp.int32, (nk, HEAD_LANES), 1) & (PAIR - 1)) < CHUNK
    lo = jnp.where(first_chunk, 0, CHUNK)
    hi = jnp.where(first_chunk, BAND, nk)
    first_pos = pl.program_id(1) * tq - WINDOW
    for pp in range(n_pairs):
        lo_pp = jnp.maximum(lo, -first_pos) if pp == 0 else lo
        valid = (r >= lo_pp) & (r < hi)
        _attend_pair(kk[pp * PAIR:pp * PAIR + nk], vt[:, pp * PAIR:pp * PAIR + nk], qt_ref,
                     slice(pp * PAIR, (pp + 1) * PAIR), valid, sink_ref, o_ref)


def _attn_sample_kernel(sink_ref, qt_ref, ck_ref, kn_ref, cvt_ref, vtn_ref, o_ref, *, n_pairs):
    nk = 2 * WINDOW + PAIR
    r = lax.broadcasted_iota(jnp.int32, (nk, HEAD_LANES), 0)
    query_seq = (lax.broadcasted_iota(jnp.int32, (nk, HEAD_LANES), 1) & (PAIR - 1)) >> 6
    key_seq = jnp.where(r < 2 * WINDOW, r >> 7, (r - 2 * WINDOW) >> 6)
    valid = query_seq == key_seq
    for pp in range(n_pairs):
        lanes = slice(pp * PAIR, (pp + 1) * PAIR)
        kwin = jnp.concatenate([ck_ref[2 * pp], ck_ref[2 * pp + 1], kn_ref[lanes, :]],
                               axis=0).astype(BF16)
        vtwin = jnp.concatenate([cvt_ref[:, 2 * pp * WINDOW:2 * (pp + 1) * WINDOW], vtn_ref[:, lanes]],
                                axis=1)
        _attend_pair(kwin, vtwin, qt_ref, lanes, valid, sink_ref, o_ref)


_SINK_SPEC = pl.BlockSpec((N_KV_HEADS, 1, HEAD_LANES), lambda *_: (0, 0, 0))


def _attention_prompt(sink_rows, qt, k, vt, bsz, seq, n_pairs):
    tq = n_pairs * PAIR
    nt = seq // tq
    wpt = tq // WINDOW
    cur_c = lambda b, i: (0, b * nt + i)
    cur_r = lambda b, i: (b * nt + i, 0)
    prev = lambda b, i: jnp.maximum((b * nt + i) * wpt - 1, 0)
    return pl.pallas_call(
        functools.partial(_attn_prompt_kernel, n_pairs=n_pairs),
        grid=(bsz, nt),
        in_specs=[
            _SINK_SPEC,
            pl.BlockSpec((PROJ_Q, tq), cur_c),
            pl.BlockSpec((WINDOW, PROJ_KV), lambda b, i: (prev(b, i), 0)),
            pl.BlockSpec((tq, PROJ_KV), cur_r),
            pl.BlockSpec((PROJ_KV, WINDOW), lambda b, i: (0, prev(b, i))),
            pl.BlockSpec((PROJ_KV, tq), cur_c),
        ],
        out_specs=pl.BlockSpec((PROJ_Q, tq), cur_c),
        out_shape=jax.ShapeDtypeStruct((PROJ_Q, bsz * seq), BF16),
        compiler_params=pltpu.CompilerParams(dimension_semantics=("arbitrary", "arbitrary")),
        name="attn_prompt",
    )(sink_rows, qt, k, k, vt, vt)


def _attention_sample(sink_rows, qt, cache_k, k, cache_vt, vt, n_pairs):
    n = qt.shape[1]
    tq = n_pairs * PAIR
    return pl.pallas_call(
        functools.partial(_attn_sample_kernel, n_pairs=n_pairs),
        grid=(n // tq,),
        in_specs=[
            _SINK_SPEC,
            pl.BlockSpec((PROJ_Q, tq), lambda i: (0, i)),
            pl.BlockSpec((2 * n_pairs, WINDOW, PROJ_KV), lambda i: (i, 0, 0)),
            pl.BlockSpec((tq, PROJ_KV), lambda i: (i, 0)),
            pl.BlockSpec((PROJ_KV, 2 * n_pairs * WINDOW), lambda i: (0, i)),
            pl.BlockSpec((PROJ_KV, tq), lambda i: (0, i)),
        ],
        out_specs=pl.BlockSpec((PROJ_Q, tq), lambda i: (0, i)),
        out_shape=jax.ShapeDtypeStruct((PROJ_Q, n), BF16),
        compiler_params=pltpu.CompilerParams(dimension_semantics=("arbitrary",)),
        name="attn_sample",
    )(sink_rows, qt, cache_k, k, cache_vt, vt)


def _power_table(tau, nbits, a_re, a_im):
    rows = tau.shape[0]
    w_re = jnp.ones((rows, STATE_LANES), F32)
    w_im = jnp.zeros((rows, STATE_LANES), F32)
    p_re, p_im = a_re, a_im
    for k in range(nbits):
        bit = ((tau >> k) & 1) == 1
        f_re = jnp.where(bit, p_re, 1.0)
        f_im = jnp.where(bit, p_im, 0.0)
        w_re, w_im = w_re * f_re - w_im * f_im, w_re * f_im + w_im * f_re
        p_re, p_im = p_re * p_re - p_im * p_im, 2.0 * p_re * p_im
    return w_re, w_im


def _build_tables(lr_ref, li_ref, ls_ref, bre_ref, bim_ref, cre_ref, cim_ref,
                  m_ref, p_ref, qt_ref, coef_ref):
    lo = lax.broadcasted_iota(jnp.int32, (1, STATE_LANES), 1) < SSM_STATE
    lr, li = lr_ref[...], li_ref[...]
    dt = jnp.exp(ls_ref[...])
    mag = jnp.exp(lr * dt)
    a_re, a_im = mag * jnp.cos(li * dt), mag * jnp.sin(li * dt)
    nr, ni = a_re - 1.0, a_im
    den = lr * lr + li * li
    f_re, f_im = (nr * lr + ni * li) / den, (ni * lr - nr * li) / den
    b_re, b_im = bre_ref[...], bim_ref[...]
    bb_re = f_re * b_re - f_im * b_im
    bb_im = f_re * b_im + f_im * b_re
    c_re, c_im = cre_ref[...], cim_ref[...]

    tau = lax.broadcasted_iota(jnp.int32, (SSM_T, 1), 0)
    w_re, w_im = _power_table(tau, 6, a_re, a_im)
    w1_re, w1_im = _power_table(tau + 1, 7, a_re, a_im)
    wr_re, wr_im = _power_table(SSM_T - 1 - tau, 6, a_re, a_im)

    def outer(c, w):
        return (c[:, None, :] * w[None, :, :]).reshape(SSM_CW, STATE_LANES)

    cw_mix = (outer(c_re, jnp.where(lo, w_re, w_im)) + outer(c_im, jnp.where(lo, -w_im, w_re)))
    bb_mix = jnp.where(lo, bb_re, -bb_im)
    strip = lax.dot_general(bb_mix, cw_mix, (((1,), (1,)), ((), ())),
                            precision=lax.Precision.HIGHEST,
                            preferred_element_type=F32)
    s_idx = lax.broadcasted_iota(jnp.int32, (SSM_T, SSM_CW), 0)
    t_idx = lax.broadcasted_iota(jnp.int32, (SSM_T, SSM_CW), 1) & (SSM_T - 1)
    causal = t_idx >= s_idx
    for c in range(SSM_GROUP):
        rows = jnp.broadcast_to(strip[c:c + 1, :], (SSM_T, SSM_CW))
        shifted = pltpu.roll(rows, 0, 1, stride=1, stride_axis=0)
        m_ref[c * SSM_T:(c + 1) * SSM_T, :] = jnp.where(causal, shifted, 0.0).astype(BF16)

    x_a, y_a = jnp.where(lo, bb_re, bb_im), jnp.where(lo, -bb_im, bb_re)
    x_b, y_b = jnp.where(lo, bb_im, bb_re), jnp.where(lo, bb_re, -bb_im)
    p_ref[:, :STATE_LANES] = (outer(x_a, wr_re) + outer(y_a, wr_im)).astype(BF16)
    p_ref[:, STATE_LANES:] = (outer(x_b, wr_re) + outer(y_b, wr_im)).astype(BF16)

    qt_ref[...] = (outer(c_re, jnp.where(lo, w1_re, -w1_im))
                   + outer(c_im, jnp.where(lo, -w1_im, -w1_re))).astype(BF16)

    t_re, t_im = a_re, a_im
    for _ in range(6):
        t_re, t_im = t_re * t_re - t_im * t_im, 2.0 * t_re * t_im
    for k in range(COEF_ROWS // 2):
        coef_ref[2 * k:2 * k + 1, :] = t_re
        coef_ref[2 * k + 1:2 * k + 2, :] = jnp.where(lo, -t_im, t_im)
        t_re, t_im = t_re * t_re - t_im * t_im, 2.0 * t_re * t_im


def _table_params(lam_re, lam_im, log_step, b_re, b_im, c_re, c_im):
    dup = lambda a: jnp.concatenate([a, a], axis=-1)
    lr = dup(lam_re)[:, None, :]
    li = dup(lam_im)[:, None, :]
    ls = jnp.broadcast_to(log_step[:, None, None], (SSM_GROUPS, 1, STATE_LANES))
    bt_re = dup(jnp.swapaxes(b_re, 1, 2))
    bt_im = dup(jnp.swapaxes(b_im, 1, 2))
    return lr, li, ls, bt_re, bt_im, dup(c_re), dup(c_im)


def _cmul(a1, a2, h, hs):
    return a1 * h + a2 * hs, a1 * hs - a2 * h


def _ssm_kernel(lr_ref, li_ref, ls_ref, bre_ref, bim_ref, cre_ref, cim_ref,
                up_ref, us_ref, h0e_ref, h0o_ref, h0es_ref, h0os_ref,
                yp_ref, ys_ref, hp_ref, hse_ref, hso_ref,
                m_ref, p_ref, qt_ref, coef_ref, *, rows_per_seq):
    _build_tables(lr_ref, li_ref, ls_ref, bre_ref, bim_ref, cre_ref, cim_ref,
                  m_ref, p_ref, qt_ref, coef_ref)
    lo = lax.broadcasted_iota(jnp.int32, (1, ROW_TOKENS), 1) < SSM_T

    def chunk_rows(ref):
        even, odd = [], []
        for k in range(SSM_GROUP // 2):
            a, b = ref[:, 2 * k, :], ref[:, 2 * k + 1, :]
            even.append(jnp.where(lo, a, pltpu.roll(b, SSM_T, 1)))
            odd.append(jnp.where(lo, pltpu.roll(a, SSM_T, 1), b))
        return jnp.concatenate(even, axis=1), jnp.concatenate(odd, axis=1)

    def store_rows(y_even, y_odd, ref):
        for k in range(SSM_GROUP // 2):
            te = y_even[:, k * ROW_TOKENS:(k + 1) * ROW_TOKENS]
            to = y_odd[:, k * ROW_TOKENS:(k + 1) * ROW_TOKENS]
            ref[:, 2 * k, :] = jnp.where(lo, te, pltpu.roll(to, SSM_T, 1))
            ref[:, 2 * k + 1, :] = jnp.where(lo, pltpu.roll(te, SSM_T, 1), to)

    pe, po = chunk_rows(up_ref)
    se, so = chunk_rows(us_ref)
    rp, rs = pe.shape[0], se.shape[0]
    u = jnp.concatenate([pe, po, se, so], axis=0).astype(BF16)
    s12 = jnp.dot(u, p_ref[...], preferred_element_type=F32)
    s1, s2 = s12[:, :STATE_LANES], s12[:, STATE_LANES:]
    a1, a2 = coef_ref[0:1, :], coef_ref[1:2, :]

    e1, e2, o1, o2 = s1[:rp], s2[:rp], s1[rp:2 * rp], s2[rp:2 * rp]
    x1, x2 = _cmul(a1, a2, e1, e2)
    x1, x2 = x1 + o1, x2 + o2
    pos = lax.broadcasted_iota(jnp.int32, (rp, STATE_LANES), 0) & (rows_per_seq - 1)
    for k in range(rows_per_seq.bit_length() - 1):
        d = 1 << k
        b1, b2 = coef_ref[2 + 2 * k:3 + 2 * k, :], coef_ref[3 + 2 * k:4 + 2 * k, :]
        sh1 = jnp.where(pos >= d, pltpu.roll(x1, d, 0), 0.0)
        sh2 = jnp.where(pos >= d, pltpu.roll(x2, d, 0), 0.0)
        y1, y2 = _cmul(b1, b2, sh1, sh2)
        x1, x2 = x1 + y1, x2 + y2
    g1 = jnp.where(pos >= 1, pltpu.roll(x1, 1, 0), 0.0)
    g2 = jnp.where(pos >= 1, pltpu.roll(x2, 1, 0), 0.0)
    ho1 = _cmul(a1, a2, g1, g2)[0] + e1
    hp_ref[...] = jnp.zeros(hp_ref.shape, F32)
    for b in range(rp // rows_per_seq):
        last = (b + 1) * rows_per_seq - 1
        hp_ref[b:b + 1, :] = x1[last:last + 1, :]

    h0e, h0o = h0e_ref[...], h0o_ref[...]
    hse_ref[...] = a1 * h0e + a2 * h0es_ref[...] + s1[2 * rp:2 * rp + rs]
    hso_ref[...] = a1 * h0o + a2 * h0os_ref[...] + s1[2 * rp + rs:]

    hprev = jnp.concatenate([g1, ho1, h0e, h0o], axis=0).astype(BF16)
    y = jnp.dot(u, m_ref[...], preferred_element_type=F32)
    y += lax.dot_general(hprev, qt_ref[...], (((1,), (1,)), ((), ())), preferred_element_type=F32)
    store_rows(y[:rp], y[rp:2 * rp], yp_ref)
    store_rows(y[2 * rp:2 * rp + rs], y[2 * rp + rs:], ys_ref)


def _ssm(table_params, u3p, u3s, h0e, h0o, h0es, h0os, *, rows_per_seq):
    rp, rs = u3p.shape[0], u3s.shape[0]
    assert rows_per_seq & (rows_per_seq - 1) == 0 and 2 * rows_per_seq.bit_length() <= COEF_ROWS
    g3 = lambda g: (g, 0, 0)
    u_spec = lambda rows: pl.BlockSpec((rows, SSM_GROUP, ROW_TOKENS), lambda g: (0, g, 0))
    h_spec = pl.BlockSpec((None, rs, STATE_LANES), g3)
    row_spec = pl.BlockSpec((None, 1, STATE_LANES), g3)
    mat_spec = pl.BlockSpec((None, SSM_GROUP, STATE_LANES), g3)
    return pl.pallas_call(
        functools.partial(_ssm_kernel, rows_per_seq=rows_per_seq),
        grid=(SSM_GROUPS,),
        in_specs=[
            row_spec, row_spec, row_spec, mat_spec, mat_spec, mat_spec, mat_spec,
            u_spec(rp), u_spec(rs),
            h_spec, h_spec, h_spec, h_spec,
        ],
        out_specs=[
            u_spec(rp), u_spec(rs),
            pl.BlockSpec((None, STATE_OUT_ROWS, STATE_LANES), g3),
            h_spec, h_spec,
        ],
        out_shape=[
            jax.ShapeDtypeStruct((rp, SSM_WIDTH, ROW_TOKENS), F32),
            jax.ShapeDtypeStruct((rs, SSM_WIDTH, ROW_TOKENS), F32),
            jax.ShapeDtypeStruct((SSM_GROUPS, STATE_OUT_ROWS, STATE_LANES), F32),
            jax.ShapeDtypeStruct((SSM_GROUPS, rs, STATE_LANES), F32),
            jax.ShapeDtypeStruct((SSM_GROUPS, rs, STATE_LANES), F32),
        ],
        scratch_shapes=[
            pltpu.VMEM((SSM_CW, SSM_CW), BF16),
            pltpu.VMEM((SSM_CW, 2 * STATE_LANES), BF16),
            pltpu.VMEM((SSM_CW, STATE_LANES), BF16),
            pltpu.VMEM((COEF_ROWS, STATE_LANES), F32),
        ],
        compiler_params=pltpu.CompilerParams(dimension_semantics=("arbitrary",)),
        name="ssm",
    )(*table_params, u3p, u3s, h0e, h0o, h0es, h0os)


def _post_kernel(x_ref, at_ref, ut_ref, yt_ref, lng_ref, lnb_ref, d_ref, wglut_ref, bglu_ref,
                 wout_ref, ln1g_ref, ln1b_ref, wgu_ref, wdown_ref, ln2g_ref, ln2b_ref, o_ref,
                 *, alpha):
    xn = _layer_norm(x_ref[...], lng_ref[...], lnb_ref[...])
    rows = range(ut_ref.shape[0])
    ut = jnp.concatenate([ut_ref[j] for j in rows], axis=1)
    yt = jnp.concatenate([yt_ref[j] for j in rows], axis=1)
    ys = yt + d_ref[...] * ut
    gl = 0.5 * ys * (1.0 + lax.erf(ys * math.sqrt(0.5)))
    z = jnp.dot(wglut_ref[...], gl.astype(BF16), preferred_element_type=F32) + bglu_ref[...]
    s = gl * jax.nn.sigmoid(z)
    tn = (((0,), (0,)), ((), ()))
    mix = lax.dot_general(at_ref[...], wout_ref[:PROJ_Q, :], tn, preferred_element_type=F32)
    mix += lax.dot_general(s.astype(BF16), wout_ref[PROJ_Q:, :], tn, preferred_element_type=F32)
    h = _layer_norm(alpha * xn + mix, ln1g_ref[...], ln1b_ref[...])
    hb = h.astype(BF16)
    f = jnp.zeros_like(h)
    for j in range(D_FF // FF_BLOCK):
        cols = slice(j * FF_BLOCK, (j + 1) * FF_BLOCK)
        up_cols = slice(D_FF + j * FF_BLOCK, D_FF + (j + 1) * FF_BLOCK)
        g = jnp.dot(hb, wgu_ref[:, cols], preferred_element_type=F32)
        up = jnp.dot(hb, wgu_ref[:, up_cols], preferred_element_type=F32)
        act = (g * jax.nn.sigmoid(g)) * up
        f += jnp.dot(act.astype(BF16), wdown_ref[cols, :], preferred_element_type=F32)
    o_ref[...] = _layer_norm(alpha * h + f, ln2g_ref[...], ln2b_ref[...])


def _post(x2d, at2d, ut2d, yt2d, ln_g, ln_b, d_col, w_glu_t, b_glu_col, w_out, ln1_g, ln1_b,
          w_gate_up, w_down, ln2_g, ln2_b, *, alpha, tm):
    n = x2d.shape[0]
    row = lambda i: (i, 0)
    col = lambda i: (0, i)
    const = lambda i: (0, 0)
    resident = lambda shape: pl.BlockSpec(shape, const, pipeline_mode=pl.Buffered(1))
    vec = lambda width: pl.BlockSpec((1, width), const)
    colvec = pl.BlockSpec((SSM_WIDTH, 1), const)
    return pl.pallas_call(
        functools.partial(_post_kernel, alpha=alpha),
        grid=(n // tm,),
        in_specs=[
            pl.BlockSpec((tm, D_MODEL), row),
            pl.BlockSpec((PROJ_Q, tm), col),
            pl.BlockSpec((tm // ROW_TOKENS, SSM_WIDTH, ROW_TOKENS), lambda i: (i, 0, 0)),
            pl.BlockSpec((tm // ROW_TOKENS, SSM_WIDTH, ROW_TOKENS), lambda i: (i, 0, 0)),
            vec(D_MODEL), vec(D_MODEL), colvec,
            resident((SSM_WIDTH, SSM_WIDTH)), colvec,
            resident((D_MODEL, D_MODEL)), vec(D_MODEL), vec(D_MODEL),
            resident((D_MODEL, 2 * D_FF)), resident((D_FF, D_MODEL)),
            vec(D_MODEL), vec(D_MODEL),
        ],
        out_specs=pl.BlockSpec((tm, D_MODEL), row),
        out_shape=jax.ShapeDtypeStruct((n, D_MODEL), F32),
        compiler_params=pltpu.CompilerParams(
            dimension_semantics=("arbitrary",), vmem_limit_bytes=V7X_VMEM_LIMIT_BYTES),
        name="post",
    )(x2d, at2d, ut2d, yt2d, ln_g, ln_b, d_col, w_glu_t, b_glu_col, w_out, ln1_g, ln1_b,
      w_gate_up, w_down, ln2_g, ln2_b)


def kernel(x_prompt, x_sample, cache_win_k, cache_win_v, state_ssm_re, state_ssm_im,
           ln_in_g, ln_in_b, w_in, attn_sinks, ssm_lambda_re, ssm_lambda_im, ssm_log_step,
           ssm_b_re, ssm_b_im, ssm_c_re, ssm_c_im, ssm_d, w_glu, b_glu, w_out,
           ln1_g, ln1_b, w_gate_up, w_down, ln2_g, ln2_b):
    depth = w_in.shape[0]
    assert depth == 1, "single-layer step"
    bp, lp, _ = x_prompt.shape
    bs, ls, _ = x_sample.shape
    assert ls == SSM_T and bs % 2 == 0 and lp % 512 == 0 and bp <= STATE_OUT_ROWS
    win_rows = cache_win_k.shape[2]
    assert win_rows == WINDOW
    alpha = (2.0 * depth) ** 0.25
    l = 0
    row = lambda a: a.reshape(1, -1)
    column = lambda a: a.reshape(-1, 1)

    w_in_b = w_in[l].astype(BF16)
    w_kv = w_in_b[:, PROJ_Q:PROJ_Q + 2 * PROJ_KV]
    w_qvu_t = jnp.concatenate([w_in_b[:, :PROJ_Q], w_in_b[:, PROJ_Q + PROJ_KV:]], axis=1).T
    lng, lnb = row(ln_in_g), row(ln_in_b)
    xp2 = x_prompt.reshape(bp * lp, D_MODEL)
    xs2 = x_sample.reshape(bs * ls, D_MODEL)
    qtp, vtp, u3p, kp, vp = _proj(xp2, lng, lnb, w_kv, w_qvu_t, 512)
    qts, vts, u3s, ks, vs = _proj(xs2, lng, lnb, w_kv, w_qvu_t, 512)

    sink_rows = jnp.repeat(attn_sinks[l].reshape(N_KV_HEADS, Q_PER_KV), PAIR, axis=1)[:, None, :]
    atp = _attention_prompt(sink_rows, qtp, kp, vtp, bp, lp, n_pairs=4)
    ck = cache_win_k[l].reshape(bs, win_rows, PROJ_KV)
    cv = cache_win_v[l].reshape(bs, win_rows, PROJ_KV)
    cvt = jnp.transpose(cv, (2, 0, 1)).reshape(PROJ_KV, bs * win_rows).astype(BF16)
    ats = _attention_sample(sink_rows, qts, ck, ks, cvt, vts, n_pairs=4)
    kp3, vp3 = kp.reshape(bp, lp, PROJ_KV), vp.reshape(bp, lp, PROJ_KV)
    ks3, vs3 = ks.reshape(bs, ls, PROJ_KV), vs.reshape(bs, ls, PROJ_KV)

    table_params = _table_params(
        ssm_lambda_re[l], ssm_lambda_im[l], ssm_log_step[l],
        ssm_b_re[l], ssm_b_im[l], ssm_c_re[l], ssm_c_im[l])
    sre = jnp.swapaxes(state_ssm_re[l], 0, 1)
    sim = jnp.swapaxes(state_ssm_im[l], 0, 1)
    h0 = jnp.concatenate([sre, sim], axis=-1)
    h0s = jnp.concatenate([sim, sre], axis=-1)
    y3p, y3s, hfin_p, hfin_e, hfin_o = _ssm(
        table_params, u3p, u3s, h0[:, 0::2], h0[:, 1::2], h0s[:, 0::2], h0s[:, 1::2],
        rows_per_seq=lp // ROW_TOKENS)
    hfin_s = jnp.stack([hfin_e, hfin_o], axis=2).reshape(SSM_GROUPS, bs, STATE_LANES)

    post_args = (lng, lnb, column(ssm_d[l]), w_glu[l].T.astype(BF16), column(b_glu[l]),
                 w_out[l].astype(BF16), row(ln1_g[l]), row(ln1_b[l]),
                 w_gate_up[l].astype(BF16), w_down[l].astype(BF16), row(ln2_g[l]), row(ln2_b[l]))
    out_p = _post(xp2, atp, u3p, y3p, *post_args, alpha=alpha, tm=512)
    out_s = _post(xs2, ats, u3s, y3s, *post_args, alpha=alpha, tm=512)

    kv_shape = (N_KV_HEADS, HEAD_DIM)
    win_k_p = kp3[:, -win_rows:].reshape(1, bp, win_rows, *kv_shape)
    win_v_p = vp3[:, -win_rows:].reshape(1, bp, win_rows, *kv_shape)
    win_k_s = jnp.concatenate([ck, ks3], axis=1)[:, -win_rows:].reshape(1, bs, win_rows, *kv_shape)
    win_v_s = jnp.concatenate([cv, vs3], axis=1)[:, -win_rows:].reshape(1, bs, win_rows, *kv_shape)
    state = lambda hf, n: jnp.swapaxes(hf[:, :n], 0, 1)
    sp, ss = state(hfin_p, bp), state(hfin_s, bs)
    return (out_p.reshape(bp, lp, D_MODEL), out_s.reshape(bs, ls, D_MODEL),
            win_k_p, win_v_p, sp[None, ..., :SSM_STATE], sp[None, ..., SSM_STATE:],
            win_k_s, win_v_s, ss[None, ..., :SSM_STATE], ss[None, ..., SSM_STATE:])
```

```python
import functools
import math

import jax
import jax.numpy as jnp
from jax import lax
from jax.experimental import pallas as pl
from jax.experimental.pallas import tpu as pltpu

F32 = jnp.float32
BF16 = jnp.bfloat16

D_MODEL = 1024
HEAD_DIM = 64
N_HEADS = 8
N_KV_HEADS = 2
Q_PER_KV = N_HEADS // N_KV_HEADS
CHUNK = 64
WINDOW = 128
WIN_CHUNKS = WINDOW // CHUNK
BAND = (WIN_CHUNKS + 1) * CHUNK
PROJ_Q = N_HEADS * HEAD_DIM
PROJ_KV = N_KV_HEADS * HEAD_DIM
SSM_WIDTH = 512
SSM_GROUP = 16
SSM_GROUPS = SSM_WIDTH // SSM_GROUP
SSM_STATE = 64
STATE_LANES = 2 * SSM_STATE
D_FF = 2816
D_IN_PROJ = PROJ_Q + 2 * PROJ_KV + SSM_WIDTH
LN_EPS = 1e-5
NEG_INF = -1e30

SSM_T = CHUNK
SSM_CW = SSM_T * SSM_GROUP
ROW_TOKENS = 2 * SSM_T
ROW_GROUP = 8
PROJ_TILE = ROW_GROUP * ROW_TOKENS
COEF_ROWS = 16
STATE_OUT_ROWS = 8
FF_BLOCK = 256
V7X_VMEM_LIMIT_BYTES = 56 * 1024 * 1024


def _layer_norm(x, g, b):
    mu = jnp.mean(x, axis=-1, keepdims=True)
    xc = x - mu
    var = jnp.mean(xc * xc, axis=-1, keepdims=True)
    return xc * lax.rsqrt(var + LN_EPS) * g + b


KV = 2 * PROJ_KV
QVU = PROJ_Q + PROJ_KV + SSM_WIDTH


def _proj_kernel(x_ref, g_ref, b_ref, w_ref, wt_ref, qt_ref, vt_ref, ut_ref, k_ref, v_ref):
    xb = _layer_norm(x_ref[...], g_ref[...], b_ref[...]).astype(BF16)
    p = jnp.dot(xb, w_ref[...], preferred_element_type=F32)
    k_ref[...] = p[:, :PROJ_KV]
    v_ref[...] = p[:, PROJ_KV:]
    pt = lax.dot_general(wt_ref[...], xb, (((1,), (1,)), ((), ())), preferred_element_type=F32)
    qt_ref[...] = (pt[:PROJ_Q] * (HEAD_DIM ** -0.5)).astype(BF16)
    vt_ref[...] = pt[PROJ_Q:PROJ_Q + PROJ_KV].astype(BF16)
    for j in range(ROW_GROUP):
        ut_ref[pl.ds(j, SSM_WIDTH, stride=ROW_GROUP), :] = (
            pt[PROJ_Q + PROJ_KV:, j * ROW_TOKENS:(j + 1) * ROW_TOKENS])


def _proj(x2d, ln_g, ln_b, w_kv, w_qvu_t):
    n = x2d.shape[0]
    tm = PROJ_TILE
    const = lambda i: (0, 0)
    row = lambda i: (i, 0)
    col = lambda i: (0, i)
    return pl.pallas_call(
        _proj_kernel,
        grid=(n // tm,),
        in_specs=[
            pl.BlockSpec((tm, D_MODEL), row),
            pl.BlockSpec((1, D_MODEL), const),
            pl.BlockSpec((1, D_MODEL), const),
            pl.BlockSpec((D_MODEL, KV), const),
            pl.BlockSpec((QVU, D_MODEL), const),
        ],
        out_specs=[
            pl.BlockSpec((PROJ_Q, tm), col),
            pl.BlockSpec((PROJ_KV, tm), col),
            pl.BlockSpec((None, SSM_WIDTH * ROW_GROUP, ROW_TOKENS), lambda i: (i, 0, 0)),
            pl.BlockSpec((tm, PROJ_KV), row),
            pl.BlockSpec((tm, PROJ_KV), row),
        ],
        out_shape=[
            jax.ShapeDtypeStruct((PROJ_Q, n), BF16),
            jax.ShapeDtypeStruct((PROJ_KV, n), BF16),
            jax.ShapeDtypeStruct((n // tm, SSM_WIDTH * ROW_GROUP, ROW_TOKENS), F32),
            jax.ShapeDtypeStruct((n, PROJ_KV), F32),
            jax.ShapeDtypeStruct((n, PROJ_KV), F32),
        ],
        compiler_params=pltpu.CompilerParams(
            dimension_semantics=("arbitrary",), vmem_limit_bytes=V7X_VMEM_LIMIT_BYTES),
        name="proj",
    )(x2d, ln_g, ln_b, w_kv, w_qvu_t)


PAIR = 2 * CHUNK
HEAD_LANES = Q_PER_KV * PAIR


def _attend_pair(kwin, vtwin, qt_ref, lanes, valid, sink_ref, o_ref):
    for h in range(N_KV_HEADS):
        base = h * Q_PER_KV * HEAD_DIM
        qrow = jnp.concatenate(
            [qt_ref[base + g * HEAD_DIM:base + (g + 1) * HEAD_DIM, lanes] for g in range(Q_PER_KV)],
            axis=1)
        zero = jnp.zeros_like(qrow)
        qstack = jnp.concatenate([qrow, zero] if h == 0 else [zero, qrow], axis=0)
        s = jnp.dot(kwin, qstack, preferred_element_type=F32)
        s = jnp.where(valid, s, NEG_INF)
        sink = sink_ref[h]
        m = jnp.maximum(jnp.max(s, axis=0, keepdims=True), sink)
        p = jnp.exp(s - m)
        den = jnp.sum(p, axis=0, keepdims=True) + jnp.exp(sink - m)
        w = (p * (1.0 / den)).astype(BF16)
        o = jnp.dot(vtwin[h * HEAD_DIM:(h + 1) * HEAD_DIM, :], w, preferred_element_type=F32)
        for g in range(Q_PER_KV):
            o_ref[base + g * HEAD_DIM:base + (g + 1) * HEAD_DIM, lanes] = (
                o[:, g * PAIR:(g + 1) * PAIR].astype(BF16))


def _attn_prompt_kernel(sink_ref, qt_ref, kp_ref, kc_ref, vtp_ref, vtc_ref, o_ref, *, n_pairs):
    tq = n_pairs * PAIR
    nk = WINDOW + PAIR
    kk = jnp.concatenate([kp_ref[...], kc_ref[...]], axis=0).astype(BF16)
    vt = jnp.concatenate([vtp_ref[...], vtc_ref[...]], axis=1)
    r = lax.broadcasted_iota(jnp.int32, (nk, HEAD_LANES), 0)
    first_chunk = (lax.broadcasted_iota(jnp.int32, (nk, HEAD_LANES), 1) & (PAIR - 1)) < CHUNK
    lo = jnp.where(first_chunk, 0, CHUNK)
    hi = jnp.where(first_chunk, BAND, nk)
    first_pos = pl.program_id(1) * tq - WINDOW
    for pp in range(n_pairs):
        lo_pp = jnp.maximum(lo, -first_pos) if pp == 0 else lo
        valid = (r >= lo_pp) & (r < hi)
        _attend_pair(kk[pp * PAIR:pp * PAIR + nk], vt[:, pp * PAIR:pp * PAIR + nk], qt_ref,
                     slice(pp * PAIR, (pp + 1) * PAIR), valid, sink_ref, o_ref)


def _attn_sample_kernel(sink_ref, qt_ref, ck_ref, kn_ref, cvt_ref, vtn_ref, o_ref, *, n_pairs):
    nk = 2 * WINDOW + PAIR
    r = lax.broadcasted_iota(jnp.int32, (nk, HEAD_LANES), 0)
    query_seq = (lax.broadcasted_iota(jnp.int32, (nk, HEAD_LANES), 1) & (PAIR - 1)) >> 6
    key_seq = jnp.where(r < 2 * WINDOW, r >> 7, (r - 2 * WINDOW) >> 6)
    valid = query_seq == key_seq
    for pp in range(n_pairs):
        lanes = slice(pp * PAIR, (pp + 1) * PAIR)
        kwin = jnp.concatenate([ck_ref[2 * pp], ck_ref[2 * pp + 1], kn_ref[lanes, :]],
                               axis=0).astype(BF16)
        vtwin = jnp.concatenate([cvt_ref[:, 2 * pp * WINDOW:2 * (pp + 1) * WINDOW], vtn_ref[:, lanes]],
                                axis=1)
        _attend_pair(kwin, vtwin, qt_ref, lanes, valid, sink_ref, o_ref)


_SINK_SPEC = pl.BlockSpec((N_KV_HEADS, 1, HEAD_LANES), lambda *_: (0, 0, 0))


def _attention_prompt(sink_rows, qt, k, vt, bsz, seq, n_pairs):
    tq = n_pairs * PAIR
    nt = seq // tq
    wpt = tq // WINDOW
    cur_c = lambda b, i: (0, b * nt + i)
    cur_r = lambda b, i: (b * nt + i, 0)
    prev = lambda b, i: jnp.maximum((b * nt + i) * wpt - 1, 0)
    return pl.pallas_call(
        functools.partial(_attn_prompt_kernel, n_pairs=n_pairs),
        grid=(bsz, nt),
        in_specs=[
            _SINK_SPEC,
            pl.BlockSpec((PROJ_Q, tq), cur_c),
            pl.BlockSpec((WINDOW, PROJ_KV), lambda b, i: (prev(b, i), 0)),
            pl.BlockSpec((tq, PROJ_KV), cur_r),
            pl.BlockSpec((PROJ_KV, WINDOW), lambda b, i: (0, prev(b, i))),
            pl.BlockSpec((PROJ_KV, tq), cur_c),
        ],
        out_specs=pl.BlockSpec((PROJ_Q, tq), cur_c),
        out_shape=jax.ShapeDtypeStruct((PROJ_Q, bsz * seq), BF16),
        compiler_params=pltpu.CompilerParams(dimension_semantics=("arbitrary", "arbitrary")),
        name="attn_prompt",
    )(sink_rows, qt, k, k, vt, vt)


def _attention_sample(sink_rows, qt, cache_k, k, cache_vt, vt, n_pairs):
    n = qt.shape[1]
    tq = n_pairs * PAIR
    return pl.pallas_call(
        functools.partial(_attn_sample_kernel, n_pairs=n_pairs),
        grid=(n // tq,),
        in_specs=[
            _SINK_SPEC,
            pl.BlockSpec((PROJ_Q, tq), lambda i: (0, i)),
            pl.BlockSpec((2 * n_pairs, WINDOW, PROJ_KV), lambda i: (i, 0, 0)),
            pl.BlockSpec((tq, PROJ_KV), lambda i: (i, 0)),
            pl.BlockSpec((PROJ_KV, 2 * n_pairs * WINDOW), lambda i: (0, i)),
            pl.BlockSpec((PROJ_KV, tq), lambda i: (0, i)),
        ],
        out_specs=pl.BlockSpec((PROJ_Q, tq), lambda i: (0, i)),
        out_shape=jax.ShapeDtypeStruct((PROJ_Q, n), BF16),
        compiler_params=pltpu.CompilerParams(dimension_semantics=("arbitrary",)),
        name="attn_sample",
    )(sink_rows, qt, cache_k, k, cache_vt, vt)


def _power_table(tau, nbits, a_re, a_im):
    rows = tau.shape[0]
    w_re = jnp.ones((rows, STATE_LANES), F32)
    w_im = jnp.zeros((rows, STATE_LANES), F32)
    p_re, p_im = a_re, a_im
    for k in range(nbits):
        bit = ((tau >> k) & 1) == 1
        f_re = jnp.where(bit, p_re, 1.0)
        f_im = jnp.where(bit, p_im, 0.0)
        w_re, w_im = w_re * f_re - w_im * f_im, w_re * f_im + w_im * f_re
        p_re, p_im = p_re * p_re - p_im * p_im, 2.0 * p_re * p_im
    return w_re, w_im


def _build_tables(lr_ref, li_ref, ls_ref, bre_ref, bim_ref, cre_ref, cim_ref,
                  m_ref, p_ref, qt_ref, coef_ref):
    lo = lax.broadcasted_iota(jnp.int32, (1, STATE_LANES), 1) < SSM_STATE
    lr, li = lr_ref[...], li_ref[...]
    dt = jnp.exp(ls_ref[...])
    mag = jnp.exp(lr * dt)
    a_re, a_im = mag * jnp.cos(li * dt), mag * jnp.sin(li * dt)
    nr, ni = a_re - 1.0, a_im
    den = lr * lr + li * li
    f_re, f_im = (nr * lr + ni * li) / den, (ni * lr - nr * li) / den
    b_re, b_im = bre_ref[...], bim_ref[...]
    bb_re = f_re * b_re - f_im * b_im
    bb_im = f_re * b_im + f_im * b_re
    c_re, c_im = cre_ref[...], cim_ref[...]

    tau = lax.broadcasted_iota(jnp.int32, (SSM_T, 1), 0)
    w_re, w_im = _power_table(tau, 6, a_re, a_im)
    w1_re, w1_im = _power_table(tau + 1, 7, a_re, a_im)
    wr_re, wr_im = _power_table(SSM_T - 1 - tau, 6, a_re, a_im)

    def outer(c, w):
        return (c[:, None, :] * w[None, :, :]).reshape(SSM_CW, STATE_LANES)

    cw_mix = (outer(c_re, jnp.where(lo, w_re, w_im)) + outer(c_im, jnp.where(lo, -w_im, w_re)))
    bb_mix = jnp.where(lo, bb_re, -bb_im)
    strip = lax.dot_general(bb_mix, cw_mix, (((1,), (1,)), ((), ())),
                            precision=lax.Precision.HIGHEST,
                            preferred_element_type=F32)
    s_idx = lax.broadcasted_iota(jnp.int32, (SSM_T, SSM_CW), 0)
    t_idx = lax.broadcasted_iota(jnp.int32, (SSM_T, SSM_CW), 1) & (SSM_T - 1)
    causal = t_idx >= s_idx
    for c in range(SSM_GROUP):
        rows = jnp.broadcast_to(strip[c:c + 1, :], (SSM_T, SSM_CW))
        shifted = pltpu.roll(rows, 0, 1, stride=1, stride_axis=0)
        m_ref[c * SSM_T:(c + 1) * SSM_T, :] = jnp.where(causal, shifted, 0.0).astype(BF16)

    x_a, y_a = jnp.where(lo, bb_re, bb_im), jnp.where(lo, -bb_im, bb_re)
    x_b, y_b = jnp.where(lo, bb_im, bb_re), jnp.where(lo, bb_re, -bb_im)
    p_ref[:, :STATE_LANES] = (outer(x_a, wr_re) + outer(y_a, wr_im)).astype(BF16)
    p_ref[:, STATE_LANES:] = (outer(x_b, wr_re) + outer(y_b, wr_im)).astype(BF16)

    qt_ref[...] = (outer(c_re, jnp.where(lo, w1_re, -w1_im))
                   + outer(c_im, jnp.where(lo, -w1_im, -w1_re))).astype(BF16)

    t_re, t_im = a_re, a_im
    for _ in range(6):
        t_re, t_im = t_re * t_re - t_im * t_im, 2.0 * t_re * t_im
    for k in range(COEF_ROWS // 2):
        coef_ref[2 * k:2 * k + 1, :] = t_re
        coef_ref[2 * k + 1:2 * k + 2, :] = jnp.where(lo, -t_im, t_im)
        t_re, t_im = t_re * t_re - t_im * t_im, 2.0 * t_re * t_im


def _table_params(lam_re, lam_im, log_step, b_re, b_im, c_re, c_im):
    dup = lambda a: jnp.concatenate([a, a], axis=-1)
    lr = dup(lam_re)[:, None, :]
    li = dup(lam_im)[:, None, :]
    ls = jnp.broadcast_to(log_step[:, None, None], (SSM_GROUPS, 1, STATE_LANES))
    bt_re = dup(jnp.swapaxes(b_re, 1, 2))
    bt_im = dup(jnp.swapaxes(b_im, 1, 2))
    return lr, li, ls, bt_re, bt_im, dup(c_re), dup(c_im)


def _cmul(a1, a2, h, hs):
    return a1 * h + a2 * hs, a1 * hs - a2 * h


def _ssm_kernel(lr_ref, li_ref, ls_ref, bre_ref, bim_ref, cre_ref, cim_ref,
                up_ref, us_ref, h0e_ref, h0o_ref, h0es_ref, h0os_ref,
                yp_ref, ys_ref, hp_ref, hse_ref, hso_ref,
                m_ref, p_ref, qt_ref, coef_ref, *, rows_per_seq):
    _build_tables(lr_ref, li_ref, ls_ref, bre_ref, bim_ref, cre_ref, cim_ref,
                  m_ref, p_ref, qt_ref, coef_ref)
    lo = lax.broadcasted_iota(jnp.int32, (1, ROW_TOKENS), 1) < SSM_T

    def chunk_rows(ref):
        even, odd = [], []
        rows = ref.shape[0] * ROW_GROUP
        for k in range(SSM_GROUP // 2):
            a = ref[:, 2 * k].reshape(rows, ROW_TOKENS)
            b = ref[:, 2 * k + 1].reshape(rows, ROW_TOKENS)
            even.append(jnp.where(lo, a, pltpu.roll(b, SSM_T, 1)))
            odd.append(jnp.where(lo, pltpu.roll(a, SSM_T, 1), b))
        return jnp.concatenate(even, axis=1), jnp.concatenate(odd, axis=1)

    def store_rows(y_even, y_odd, ref):
        for k in range(SSM_GROUP // 2):
            te = y_even[:, k * ROW_TOKENS:(k + 1) * ROW_TOKENS]
            to = y_odd[:, k * ROW_TOKENS:(k + 1) * ROW_TOKENS]
            tiles = (ref.shape[0], ROW_GROUP, ROW_TOKENS)
            ref[:, 2 * k] = jnp.where(lo, te, pltpu.roll(to, SSM_T, 1)).reshape(tiles)
            ref[:, 2 * k + 1] = jnp.where(lo, pltpu.roll(te, SSM_T, 1), to).reshape(tiles)

    pe, po = chunk_rows(up_ref)
    se, so = chunk_rows(us_ref)
    rp, rs = pe.shape[0], se.shape[0]
    u = jnp.concatenate([pe, po, se, so], axis=0).astype(BF16)
    s12 = jnp.dot(u, p_ref[...], preferred_element_type=F32)
    s1, s2 = s12[:, :STATE_LANES], s12[:, STATE_LANES:]
    a1, a2 = coef_ref[0:1, :], coef_ref[1:2, :]

    e1, e2, o1, o2 = s1[:rp], s2[:rp], s1[rp:2 * rp], s2[rp:2 * rp]
    x1, x2 = _cmul(a1, a2, e1, e2)
    x1, x2 = x1 + o1, x2 + o2
    pos = lax.broadcasted_iota(jnp.int32, (rp, STATE_LANES), 0) & (rows_per_seq - 1)
    for k in range(rows_per_seq.bit_length() - 1):
        d = 1 << k
        b1, b2 = coef_ref[2 + 2 * k:3 + 2 * k, :], coef_ref[3 + 2 * k:4 + 2 * k, :]
        sh1 = jnp.where(pos >= d, pltpu.roll(x1, d, 0), 0.0)
        sh2 = jnp.where(pos >= d, pltpu.roll(x2, d, 0), 0.0)
        y1, y2 = _cmul(b1, b2, sh1, sh2)
        x1, x2 = x1 + y1, x2 + y2
    g1 = jnp.where(pos >= 1, pltpu.roll(x1, 1, 0), 0.0)
    g2 = jnp.where(pos >= 1, pltpu.roll(x2, 1, 0), 0.0)
    ho1 = _cmul(a1, a2, g1, g2)[0] + e1
    hp_ref[...] = jnp.zeros(hp_ref.shape, F32)
    for b in range(rp // rows_per_seq):
        last = (b + 1) * rows_per_seq - 1
        hp_ref[b:b + 1, :] = x1[last:last + 1, :]

    h0e, h0o = h0e_ref[...], h0o_ref[...]
    hse_ref[...] = a1 * h0e + a2 * h0es_ref[...] + s1[2 * rp:2 * rp + rs]
    hso_ref[...] = a1 * h0o + a2 * h0os_ref[...] + s1[2 * rp + rs:]

    hprev = jnp.concatenate([g1, ho1, h0e, h0o], axis=0).astype(BF16)
    y = jnp.dot(u, m_ref[...], preferred_element_type=F32)
    y += lax.dot_general(hprev, qt_ref[...], (((1,), (1,)), ((), ())), preferred_element_type=F32)
    store_rows(y[:rp], y[rp:2 * rp], yp_ref)
    store_rows(y[2 * rp:2 * rp + rs], y[2 * rp + rs:], ys_ref)


def _ssm(table_params, u3p, u3s, h0e, h0o, h0es, h0os, *, rows_per_seq):
    gp, gs = u3p.shape[0], u3s.shape[0]
    rp, rs = gp * ROW_GROUP, gs * ROW_GROUP
    u3p = u3p.reshape(gp, SSM_WIDTH, ROW_GROUP, ROW_TOKENS)
    u3s = u3s.reshape(gs, SSM_WIDTH, ROW_GROUP, ROW_TOKENS)
    assert rows_per_seq & (rows_per_seq - 1) == 0 and 2 * rows_per_seq.bit_length() <= COEF_ROWS
    g3 = lambda g: (g, 0, 0)
    u_spec = lambda groups: pl.BlockSpec((groups, SSM_GROUP, ROW_GROUP, ROW_TOKENS),
                                         lambda g: (0, g, 0, 0))
    h_spec = pl.BlockSpec((None, rs, STATE_LANES), g3)
    row_spec = pl.BlockSpec((None, 1, STATE_LANES), g3)
    mat_spec = pl.BlockSpec((None, SSM_GROUP, STATE_LANES), g3)
    return pl.pallas_call(
        functools.partial(_ssm_kernel, rows_per_seq=rows_per_seq),
        grid=(SSM_GROUPS,),
        in_specs=[
            row_spec, row_spec, row_spec, mat_spec, mat_spec, mat_spec, mat_spec,
            u_spec(gp), u_spec(gs),
            h_spec, h_spec, h_spec, h_spec,
        ],
        out_specs=[
            u_spec(gp), u_spec(gs),
            pl.BlockSpec((None, STATE_OUT_ROWS, STATE_LANES), g3),
            h_spec, h_spec,
        ],
        out_shape=[
            jax.ShapeDtypeStruct((gp, SSM_WIDTH, ROW_GROUP, ROW_TOKENS), F32),
            jax.ShapeDtypeStruct((gs, SSM_WIDTH, ROW_GROUP, ROW_TOKENS), F32),
            jax.ShapeDtypeStruct((SSM_GROUPS, STATE_OUT_ROWS, STATE_LANES), F32),
            jax.ShapeDtypeStruct((SSM_GROUPS, rs, STATE_LANES), F32),
            jax.ShapeDtypeStruct((SSM_GROUPS, rs, STATE_LANES), F32),
        ],
        scratch_shapes=[
            pltpu.VMEM((SSM_CW, SSM_CW), BF16),
            pltpu.VMEM((SSM_CW, 2 * STATE_LANES), BF16),
            pltpu.VMEM((SSM_CW, STATE_LANES), BF16),
            pltpu.VMEM((COEF_ROWS, STATE_LANES), F32),
        ],
        compiler_params=pltpu.CompilerParams(dimension_semantics=("arbitrary",)),
        name="ssm",
    )(*table_params, u3p, u3s, h0e, h0o, h0es, h0os)


def _post_kernel(x_ref, at_ref, ut_ref, yt_ref, lng_ref, lnb_ref, d_ref, wglut_ref, bglu_ref,
                 wout_ref, ln1g_ref, ln1b_ref, wgu_ref, wdown_ref, ln2g_ref, ln2b_ref, o_ref,
                 *, alpha):
    xn = _layer_norm(x_ref[...], lng_ref[...], lnb_ref[...])
    n_rows = x_ref.shape[0] // ROW_TOKENS
    first_row = (pl.program_id(0) % (ROW_GROUP // n_rows)) * n_rows

    def feature_major(ref):
        return jnp.concatenate(
            [ref[pl.ds(first_row + j, SSM_WIDTH, stride=ROW_GROUP), :] for j in range(n_rows)], axis=1)

    ys = feature_major(yt_ref) + d_ref[...] * feature_major(ut_ref)
    gl = 0.5 * ys * (1.0 + lax.erf(ys * math.sqrt(0.5)))
    z = jnp.dot(wglut_ref[...], gl.astype(BF16), preferred_element_type=F32) + bglu_ref[...]
    s = gl * jax.nn.sigmoid(z)
    tn = (((0,), (0,)), ((), ()))
    mix = lax.dot_general(at_ref[...], wout_ref[:PROJ_Q, :], tn, preferred_element_type=F32)
    mix += lax.dot_general(s.astype(BF16), wout_ref[PROJ_Q:, :], tn, preferred_element_type=F32)
    h = _layer_norm(alpha * xn + mix, ln1g_ref[...], ln1b_ref[...])
    hb = h.astype(BF16)
    f = jnp.zeros_like(h)
    for j in range(D_FF // FF_BLOCK):
        cols = slice(j * FF_BLOCK, (j + 1) * FF_BLOCK)
        up_cols = slice(D_FF + j * FF_BLOCK, D_FF + (j + 1) * FF_BLOCK)
        g = jnp.dot(hb, wgu_ref[:, cols], preferred_element_type=F32)
        up = jnp.dot(hb, wgu_ref[:, up_cols], preferred_element_type=F32)
        act = (g * jax.nn.sigmoid(g)) * up
        f += jnp.dot(act.astype(BF16), wdown_ref[cols, :], preferred_element_type=F32)
    o_ref[...] = _layer_norm(alpha * h + f, ln2g_ref[...], ln2b_ref[...])


def _post(x2d, at2d, ut3, yt3, ln_g, ln_b, d_col, w_glu_t, b_glu_col, w_out, ln1_g, ln1_b,
          w_gate_up, w_down, ln2_g, ln2_b, *, alpha, tm):
    n = x2d.shape[0]
    assert PROJ_TILE % tm == 0
    group = lambda i: (i // (PROJ_TILE // tm), 0, 0)
    group_spec = pl.BlockSpec((None, SSM_WIDTH * ROW_GROUP, ROW_TOKENS), group)
    row = lambda i: (i, 0)
    col = lambda i: (0, i)
    const = lambda i: (0, 0)
    resident = lambda shape: pl.BlockSpec(shape, const, pipeline_mode=pl.Buffered(1))
    vec = lambda width: pl.BlockSpec((1, width), const)
    colvec = pl.BlockSpec((SSM_WIDTH, 1), const)
    return pl.pallas_call(
        functools.partial(_post_kernel, alpha=alpha),
        grid=(n // tm,),
        in_specs=[
            pl.BlockSpec((tm, D_MODEL), row),
            pl.BlockSpec((PROJ_Q, tm), col),
            group_spec, group_spec,
            vec(D_MODEL), vec(D_MODEL), colvec,
            resident((SSM_WIDTH, SSM_WIDTH)), colvec,
            resident((D_MODEL, D_MODEL)), vec(D_MODEL), vec(D_MODEL),
            resident((D_MODEL, 2 * D_FF)), resident((D_FF, D_MODEL)),
            vec(D_MODEL), vec(D_MODEL),
        ],
        out_specs=pl.BlockSpec((tm, D_MODEL), row),
        out_shape=jax.ShapeDtypeStruct((n, D_MODEL), F32),
        compiler_params=pltpu.CompilerParams(
            dimension_semantics=("arbitrary",), vmem_limit_bytes=V7X_VMEM_LIMIT_BYTES),
        name="post",
    )(x2d, at2d, ut3, yt3, ln_g, ln_b, d_col, w_glu_t, b_glu_col, w_out, ln1_g, ln1_b,
      w_gate_up, w_down, ln2_g, ln2_b)


def kernel(x_prompt, x_sample, cache_win_k, cache_win_v, state_ssm_re, state_ssm_im,
           ln_in_g, ln_in_b, w_in, attn_sinks, ssm_lambda_re, ssm_lambda_im, ssm_log_step,
           ssm_b_re, ssm_b_im, ssm_c_re, ssm_c_im, ssm_d, w_glu, b_glu, w_out,
           ln1_g, ln1_b, w_gate_up, w_down, ln2_g, ln2_b):
    depth = w_in.shape[0]
    assert depth == 1, "single-layer step"
    bp, lp, _ = x_prompt.shape
    bs, ls, _ = x_sample.shape
    assert ls == SSM_T and bs % 2 == 0 and lp % 512 == 0 and bp <= STATE_OUT_ROWS
    win_rows = cache_win_k.shape[2]
    assert win_rows == WINDOW
    alpha = (2.0 * depth) ** 0.25
    l = 0
    row = lambda a: a.reshape(1, -1)
    column = lambda a: a.reshape(-1, 1)

    w_in_b = w_in[l].astype(BF16)
    w_kv = w_in_b[:, PROJ_Q:PROJ_Q + 2 * PROJ_KV]
    w_qvu_t = jnp.concatenate([w_in_b[:, :PROJ_Q], w_in_b[:, PROJ_Q + PROJ_KV:]], axis=1).T
    lng, lnb = row(ln_in_g), row(ln_in_b)
    xp2 = x_prompt.reshape(bp * lp, D_MODEL)
    xs2 = x_sample.reshape(bs * ls, D_MODEL)
    qtp, vtp, u3p, kp, vp = _proj(xp2, lng, lnb, w_kv, w_qvu_t)
    qts, vts, u3s, ks, vs = _proj(xs2, lng, lnb, w_kv, w_qvu_t)

    sink_rows = jnp.repeat(attn_sinks[l].reshape(N_KV_HEADS, Q_PER_KV), PAIR, axis=1)[:, None, :]
    atp = _attention_prompt(sink_rows, qtp, kp, vtp, bp, lp, n_pairs=4)
    ck = cache_win_k[l].reshape(bs, win_rows, PROJ_KV)
    cv = cache_win_v[l].reshape(bs, win_rows, PROJ_KV)
    cvt = jnp.transpose(cv, (2, 0, 1)).reshape(PROJ_KV, bs * win_rows).astype(BF16)
    ats = _attention_sample(sink_rows, qts, ck, ks, cvt, vts, n_pairs=4)
    kp3, vp3 = kp.reshape(bp, lp, PROJ_KV), vp.reshape(bp, lp, PROJ_KV)
    ks3, vs3 = ks.reshape(bs, ls, PROJ_KV), vs.reshape(bs, ls, PROJ_KV)

    table_params = _table_params(
        ssm_lambda_re[l], ssm_lambda_im[l], ssm_log_step[l],
        ssm_b_re[l], ssm_b_im[l], ssm_c_re[l], ssm_c_im[l])
    sre = jnp.swapaxes(state_ssm_re[l], 0, 1)
    sim = jnp.swapaxes(state_ssm_im[l], 0, 1)
    h0 = jnp.concatenate([sre, sim], axis=-1)
    h0s = jnp.concatenate([sim, sre], axis=-1)
    y4p, y4s, hfin_p, hfin_e, hfin_o = _ssm(
        table_params, u3p, u3s, h0[:, 0::2], h0[:, 1::2], h0s[:, 0::2], h0s[:, 1::2],
        rows_per_seq=lp // ROW_TOKENS)
    hfin_s = jnp.stack([hfin_e, hfin_o], axis=2).reshape(SSM_GROUPS, bs, STATE_LANES)

    post_args = (lng, lnb, column(ssm_d[l]), w_glu[l].T.astype(BF16), column(b_glu[l]),
                 w_out[l].astype(BF16), row(ln1_g[l]), row(ln1_b[l]),
                 w_gate_up[l].astype(BF16), w_down[l].astype(BF16), row(ln2_g[l]), row(ln2_b[l]))
    out_p = _post(xp2, atp, u3p, y4p.reshape(u3p.shape), *post_args, alpha=alpha, tm=512)
    out_s = _post(xs2, ats, u3s, y4s.reshape(u3s.shape), *post_args, alpha=alpha, tm=512)

    kv_shape = (N_KV_HEADS, HEAD_DIM)
    win_k_p = kp3[:, -win_rows:].reshape(1, bp, win_rows, *kv_shape)
    win_v_p = vp3[:, -win_rows:].reshape(1, bp, win_rows, *kv_shape)
    win_k_s = jnp.concatenate([ck, ks3], axis=1)[:, -win_rows:].reshape(1, bs, win_rows, *kv_shape)
    win_v_s = jnp.concatenate([cv, vs3], axis=1)[:, -win_rows:].reshape(1, bs, win_rows, *kv_shape)
    state = lambda hf, n: jnp.swapaxes(hf[:, :n], 0, 1)
    sp, ss = state(hfin_p, bp), state(hfin_s, bs)
    return (out_p.reshape(bp, lp, D_MODEL), out_s.reshape(bs, ls, D_MODEL),
            win_k_p, win_v_p, sp[None, ..., :SSM_STATE], sp[None, ..., SSM_STATE:],
            win_k_s, win_v_s, ss[None, ..., :SSM_STATE], ss[None, ..., SSM_STATE:])
```

```python
import functools
import math

import jax
import jax.numpy as jnp
from jax import lax
from jax.experimental import pallas as pl
from jax.experimental.pallas import tpu as pltpu

F32 = jnp.float32
BF16 = jnp.bfloat16

D_MODEL = 1024
HEAD_DIM = 64
N_HEADS = 8
N_KV_HEADS = 2
Q_PER_KV = N_HEADS // N_KV_HEADS
CHUNK = 64
WINDOW = 128
WIN_CHUNKS = WINDOW // CHUNK
BAND = (WIN_CHUNKS + 1) * CHUNK
PROJ_Q = N_HEADS * HEAD_DIM
PROJ_KV = N_KV_HEADS * HEAD_DIM
SSM_WIDTH = 512
SSM_GROUP = 16
SSM_GROUPS = SSM_WIDTH // SSM_GROUP
SSM_STATE = 64
STATE_LANES = 2 * SSM_STATE
D_FF = 2816
D_IN_PROJ = PROJ_Q + 2 * PROJ_KV + SSM_WIDTH
LN_EPS = 1e-5
NEG_INF = -1e30

SSM_T = CHUNK
SSM_CW = SSM_T * SSM_GROUP
ROW_TOKENS = 2 * SSM_T
ROW_GROUP = 8
PROJ_TILE = ROW_GROUP * ROW_TOKENS
COEF_ROWS = 16
STATE_OUT_ROWS = 8
FF_BLOCK = 256
V7X_VMEM_LIMIT_BYTES = 56 * 1024 * 1024


def _layer_norm(x, g, b):
    mu = jnp.mean(x, axis=-1, keepdims=True)
    xc = x - mu
    var = jnp.mean(xc * xc, axis=-1, keepdims=True)
    return xc * lax.rsqrt(var + LN_EPS) * g + b


KV = 2 * PROJ_KV
QVU = PROJ_Q + PROJ_KV + SSM_WIDTH


def _proj_kernel(x_ref, g_ref, b_ref, w_ref, wt_ref, qt_ref, vt_ref, ut_ref, k_ref, v_ref):
    xb = _layer_norm(x_ref[...], g_ref[...], b_ref[...]).astype(BF16)
    p = jnp.dot(xb, w_ref[...], preferred_element_type=F32)
    k_ref[...] = p[:, :PROJ_KV]
    v_ref[...] = p[:, PROJ_KV:]
    pt = lax.dot_general(wt_ref[...], xb, (((1,), (1,)), ((), ())), preferred_element_type=F32)
    qt_ref[...] = (pt[:PROJ_Q] * (HEAD_DIM ** -0.5)).astype(BF16)
    vt_ref[...] = pt[PROJ_Q:PROJ_Q + PROJ_KV].astype(BF16)
    for j in range(ROW_GROUP):
        ut_ref[pl.ds(j, SSM_WIDTH, stride=ROW_GROUP), :] = (
            pt[PROJ_Q + PROJ_KV:, j * ROW_TOKENS:(j + 1) * ROW_TOKENS])


def _proj(x2d, ln_g, ln_b, w_kv, w_qvu_t):
    n = x2d.shape[0]
    tm = PROJ_TILE
    const = lambda i: (0, 0)
    row = lambda i: (i, 0)
    col = lambda i: (0, i)
    return pl.pallas_call(
        _proj_kernel,
        grid=(n // tm,),
        in_specs=[
            pl.BlockSpec((tm, D_MODEL), row),
            pl.BlockSpec((1, D_MODEL), const),
            pl.BlockSpec((1, D_MODEL), const),
            pl.BlockSpec((D_MODEL, KV), const),
            pl.BlockSpec((QVU, D_MODEL), const),
        ],
        out_specs=[
            pl.BlockSpec((PROJ_Q, tm), col),
            pl.BlockSpec((PROJ_KV, tm), col),
            pl.BlockSpec((None, SSM_WIDTH * ROW_GROUP, ROW_TOKENS), lambda i: (i, 0, 0)),
            pl.BlockSpec((tm, PROJ_KV), row),
            pl.BlockSpec((tm, PROJ_KV), row),
        ],
        out_shape=[
            jax.ShapeDtypeStruct((PROJ_Q, n), BF16),
            jax.ShapeDtypeStruct((PROJ_KV, n), BF16),
            jax.ShapeDtypeStruct((n // tm, SSM_WIDTH * ROW_GROUP, ROW_TOKENS), F32),
            jax.ShapeDtypeStruct((n, PROJ_KV), F32),
            jax.ShapeDtypeStruct((n, PROJ_KV), F32),
        ],
        compiler_params=pltpu.CompilerParams(
            dimension_semantics=("arbitrary",), vmem_limit_bytes=V7X_VMEM_LIMIT_BYTES),
        name="proj",
    )(x2d, ln_g, ln_b, w_kv, w_qvu_t)


PAIR = 2 * CHUNK
HEAD_LANES = Q_PER_KV * PAIR
ONES_ROWS = 16


def _scores(unit, qt_ref):
    kwin, _, lanes, _, _, h = unit
    base = h * Q_PER_KV * HEAD_DIM
    qrow = jnp.concatenate(
        [qt_ref[base + g * HEAD_DIM:base + (g + 1) * HEAD_DIM, lanes] for g in range(Q_PER_KV)],
        axis=1)
    zero = jnp.zeros_like(qrow)
    qstack = jnp.concatenate([qrow, zero] if h == 0 else [zero, qrow], axis=0)
    return jnp.dot(kwin, qstack, preferred_element_type=F32)


def _finish(unit, s, sink_ref, o_ref):
    _, vtwin, lanes, valid, masked_rows, h = unit
    nk = s.shape[0]
    base = h * Q_PER_KV * HEAD_DIM
    pieces, done = [], 0
    for start, stop in masked_rows:
        if start > done:
            pieces.append(s[done:start])
        pieces.append(jnp.where(valid[start:stop], s[start:stop], NEG_INF))
        done = stop
    if done < nk:
        pieces.append(s[done:])
    s = jnp.concatenate(pieces, axis=0)
    sink = sink_ref[h]
    m = jnp.maximum(jnp.max(s, axis=0, keepdims=True), sink)
    p = jnp.exp(s - m).astype(BF16)
    v_ones = jnp.concatenate(
        [vtwin[h * HEAD_DIM:(h + 1) * HEAD_DIM, :], jnp.ones((ONES_ROWS, nk), BF16)], axis=0)
    ov = jnp.dot(v_ones, p, preferred_element_type=F32)
    den = ov[HEAD_DIM:HEAD_DIM + 1, :] + jnp.exp(sink - m)
    o = ov[:HEAD_DIM, :] * (1.0 / den)
    for g in range(Q_PER_KV):
        o_ref[base + g * HEAD_DIM:base + (g + 1) * HEAD_DIM, lanes] = (
            o[:, g * PAIR:(g + 1) * PAIR].astype(BF16))


def _attend_units(units, qt_ref, sink_ref, o_ref):
    s_next = _scores(units[0], qt_ref)
    for i, unit in enumerate(units):
        s = s_next
        if i + 1 < len(units):
            s_next = _scores(units[i + 1], qt_ref)
        _finish(unit, s, sink_ref, o_ref)


def _attn_prompt_kernel(sink_ref, qt_ref, kp_ref, kc_ref, vtp_ref, vtc_ref, o_ref, *, n_pairs):
    tq = n_pairs * PAIR
    nk = WINDOW + PAIR
    kk = jnp.concatenate([kp_ref[...], kc_ref[...]], axis=0).astype(BF16)
    vt = jnp.concatenate([vtp_ref[...], vtc_ref[...]], axis=1)
    r = lax.broadcasted_iota(jnp.int32, (nk, HEAD_LANES), 0)
    first_chunk = (lax.broadcasted_iota(jnp.int32, (nk, HEAD_LANES), 1) & (PAIR - 1)) < CHUNK
    lo = jnp.where(first_chunk, 0, CHUNK)
    hi = jnp.where(first_chunk, BAND, nk)
    first_pos = pl.program_id(1) * tq - WINDOW
    units = []
    for pp in range(n_pairs):
        lo_pp = jnp.maximum(lo, -first_pos) if pp == 0 else lo
        valid = (r >= lo_pp) & (r < hi)
        masked_rows = ((0, nk),) if pp == 0 else ((0, CHUNK), (nk - CHUNK, nk))
        for h in range(N_KV_HEADS):
            units.append((kk[pp * PAIR:pp * PAIR + nk], vt[:, pp * PAIR:pp * PAIR + nk],
                          slice(pp * PAIR, (pp + 1) * PAIR), valid, masked_rows, h))
    _attend_units(units, qt_ref, sink_ref, o_ref)


def _attn_sample_kernel(sink_ref, qt_ref, ck_ref, kn_ref, cvt_ref, vtn_ref, o_ref, *, n_pairs):
    nk = 2 * WINDOW + PAIR
    r = lax.broadcasted_iota(jnp.int32, (nk, HEAD_LANES), 0)
    query_seq = (lax.broadcasted_iota(jnp.int32, (nk, HEAD_LANES), 1) & (PAIR - 1)) >> 6
    key_seq = jnp.where(r < 2 * WINDOW, r >> 7, (r - 2 * WINDOW) >> 6)
    valid = query_seq == key_seq
    units = []
    for pp in range(n_pairs):
        lanes = slice(pp * PAIR, (pp + 1) * PAIR)
        kwin = jnp.concatenate([ck_ref[2 * pp], ck_ref[2 * pp + 1], kn_ref[lanes, :]],
                               axis=0).astype(BF16)
        vtwin = jnp.concatenate([cvt_ref[:, 2 * pp * WINDOW:2 * (pp + 1) * WINDOW], vtn_ref[:, lanes]],
                                axis=1)
        units.extend((kwin, vtwin, lanes, valid, ((0, nk),), h) for h in range(N_KV_HEADS))
    _attend_units(units, qt_ref, sink_ref, o_ref)


_SINK_SPEC = pl.BlockSpec((N_KV_HEADS, 1, HEAD_LANES), lambda *_: (0, 0, 0))


def _attention_prompt(sink_rows, qt, k, vt, bsz, seq, n_pairs):
    tq = n_pairs * PAIR
    nt = seq // tq
    wpt = tq // WINDOW
    cur_c = lambda b, i: (0, b * nt + i)
    cur_r = lambda b, i: (b * nt + i, 0)
    prev = lambda b, i: jnp.maximum((b * nt + i) * wpt - 1, 0)
    return pl.pallas_call(
        functools.partial(_attn_prompt_kernel, n_pairs=n_pairs),
        grid=(bsz, nt),
        in_specs=[
            _SINK_SPEC,
            pl.BlockSpec((PROJ_Q, tq), cur_c),
            pl.BlockSpec((WINDOW, PROJ_KV), lambda b, i: (prev(b, i), 0)),
            pl.BlockSpec((tq, PROJ_KV), cur_r),
            pl.BlockSpec((PROJ_KV, WINDOW), lambda b, i: (0, prev(b, i))),
            pl.BlockSpec((PROJ_KV, tq), cur_c),
        ],
        out_specs=pl.BlockSpec((PROJ_Q, tq), cur_c),
        out_shape=jax.ShapeDtypeStruct((PROJ_Q, bsz * seq), BF16),
        compiler_params=pltpu.CompilerParams(dimension_semantics=("arbitrary", "arbitrary")),
        name="attn_prompt",
    )(sink_rows, qt, k, k, vt, vt)


def _attention_sample(sink_rows, qt, cache_k, k, cache_vt, vt, n_pairs):
    n = qt.shape[1]
    tq = n_pairs * PAIR
    return pl.pallas_call(
        functools.partial(_attn_sample_kernel, n_pairs=n_pairs),
        grid=(n // tq,),
        in_specs=[
            _SINK_SPEC,
            pl.BlockSpec((PROJ_Q, tq), lambda i: (0, i)),
            pl.BlockSpec((2 * n_pairs, WINDOW, PROJ_KV), lambda i: (i, 0, 0)),
            pl.BlockSpec((tq, PROJ_KV), lambda i: (i, 0)),
            pl.BlockSpec((PROJ_KV, 2 * n_pairs * WINDOW), lambda i: (0, i)),
            pl.BlockSpec((PROJ_KV, tq), lambda i: (0, i)),
        ],
        out_specs=pl.BlockSpec((PROJ_Q, tq), lambda i: (0, i)),
        out_shape=jax.ShapeDtypeStruct((PROJ_Q, n), BF16),
        compiler_params=pltpu.CompilerParams(dimension_semantics=("arbitrary",)),
        name="attn_sample",
    )(sink_rows, qt, cache_k, k, cache_vt, vt)


def _power_table(tau, nbits, a_re, a_im):
    rows = tau.shape[0]
    w_re = jnp.ones((rows, STATE_LANES), F32)
    w_im = jnp.zeros((rows, STATE_LANES), F32)
    p_re, p_im = a_re, a_im
    for k in range(nbits):
        bit = ((tau >> k) & 1) == 1
        f_re = jnp.where(bit, p_re, 1.0)
        f_im = jnp.where(bit, p_im, 0.0)
        w_re, w_im = w_re * f_re - w_im * f_im, w_re * f_im + w_im * f_re
        p_re, p_im = p_re * p_re - p_im * p_im, 2.0 * p_re * p_im
    return w_re, w_im


def _build_tables(lr_ref, li_ref, ls_ref, bre_ref, bim_ref, cre_ref, cim_ref,
                  m_ref, p_ref, qt_ref, coef_ref):
    lo = lax.broadcasted_iota(jnp.int32, (1, STATE_LANES), 1) < SSM_STATE
    lr, li = lr_ref[...], li_ref[...]
    dt = jnp.exp(ls_ref[...])
    mag = jnp.exp(lr * dt)
    a_re, a_im = mag * jnp.cos(li * dt), mag * jnp.sin(li * dt)
    nr, ni = a_re - 1.0, a_im
    den = lr * lr + li * li
    f_re, f_im = (nr * lr + ni * li) / den, (ni * lr - nr * li) / den
    b_re, b_im = bre_ref[...], bim_ref[...]
    bb_re = f_re * b_re - f_im * b_im
    bb_im = f_re * b_im + f_im * b_re
    c_re, c_im = cre_ref[...], cim_ref[...]

    tau = lax.broadcasted_iota(jnp.int32, (SSM_T, 1), 0)
    w_re, w_im = _power_table(tau, 6, a_re, a_im)
    w1_re, w1_im = _power_table(tau + 1, 7, a_re, a_im)
    wr_re, wr_im = _power_table(SSM_T - 1 - tau, 6, a_re, a_im)

    def outer(c, w):
        return (c[:, None, :] * w[None, :, :]).reshape(SSM_CW, STATE_LANES)

    cw_mix = (outer(c_re, jnp.where(lo, w_re, w_im)) + outer(c_im, jnp.where(lo, -w_im, w_re)))
    bb_mix = jnp.where(lo, bb_re, -bb_im)
    strip = lax.dot_general(bb_mix, cw_mix, (((1,), (1,)), ((), ())),
                            precision=lax.Precision.HIGHEST,
                            preferred_element_type=F32)
    s_idx = lax.broadcasted_iota(jnp.int32, (SSM_T, SSM_CW), 0)
    t_idx = lax.broadcasted_iota(jnp.int32, (SSM_T, SSM_CW), 1) & (SSM_T - 1)
    causal = t_idx >= s_idx
    for c in range(SSM_GROUP):
        rows = jnp.broadcast_to(strip[c:c + 1, :], (SSM_T, SSM_CW))
        shifted = pltpu.roll(rows, 0, 1, stride=1, stride_axis=0)
        m_ref[c * SSM_T:(c + 1) * SSM_T, :] = jnp.where(causal, shifted, 0.0).astype(BF16)

    x_a, y_a = jnp.where(lo, bb_re, bb_im), jnp.where(lo, -bb_im, bb_re)
    x_b, y_b = jnp.where(lo, bb_im, bb_re), jnp.where(lo, bb_re, -bb_im)
    p_ref[:, :STATE_LANES] = (outer(x_a, wr_re) + outer(y_a, wr_im)).astype(BF16)
    p_ref[:, STATE_LANES:] = (outer(x_b, wr_re) + outer(y_b, wr_im)).astype(BF16)

    qt_ref[...] = (outer(c_re, jnp.where(lo, w1_re, -w1_im))
                   + outer(c_im, jnp.where(lo, -w1_im, -w1_re))).astype(BF16)

    t_re, t_im = a_re, a_im
    for _ in range(6):
        t_re, t_im = t_re * t_re - t_im * t_im, 2.0 * t_re * t_im
    for k in range(COEF_ROWS // 2):
        coef_ref[2 * k:2 * k + 1, :] = t_re
        coef_ref[2 * k + 1:2 * k + 2, :] = jnp.where(lo, -t_im, t_im)
        t_re, t_im = t_re * t_re - t_im * t_im, 2.0 * t_re * t_im


def _table_params(lam_re, lam_im, log_step, b_re, b_im, c_re, c_im):
    dup = lambda a: jnp.concatenate([a, a], axis=-1)
    lr = dup(lam_re)[:, None, :]
    li = dup(lam_im)[:, None, :]
    ls = jnp.broadcast_to(log_step[:, None, None], (SSM_GROUPS, 1, STATE_LANES))
    bt_re = dup(jnp.swapaxes(b_re, 1, 2))
    bt_im = dup(jnp.swapaxes(b_im, 1, 2))
    return lr, li, ls, bt_re, bt_im, dup(c_re), dup(c_im)


def _cmul(a1, a2, h, hs):
    return a1 * h + a2 * hs, a1 * hs - a2 * h


def _ssm_kernel(lr_ref, li_ref, ls_ref, bre_ref, bim_ref, cre_ref, cim_ref,
                up_ref, us_ref, h0e_ref, h0o_ref, h0es_ref, h0os_ref,
                yp_ref, ys_ref, hp_ref, hse_ref, hso_ref,
                m_ref, p_ref, qt_ref, coef_ref, *, rows_per_seq):
    _build_tables(lr_ref, li_ref, ls_ref, bre_ref, bim_ref, cre_ref, cim_ref,
                  m_ref, p_ref, qt_ref, coef_ref)
    lo = lax.broadcasted_iota(jnp.int32, (1, ROW_TOKENS), 1) < SSM_T

    def chunk_rows(ref):
        even, odd = [], []
        rows = ref.shape[0] * ROW_GROUP
        for k in range(SSM_GROUP // 2):
            a = ref[:, 2 * k].reshape(rows, ROW_TOKENS)
            b = ref[:, 2 * k + 1].reshape(rows, ROW_TOKENS)
            even.append(jnp.where(lo, a, pltpu.roll(b, SSM_T, 1)))
            odd.append(jnp.where(lo, pltpu.roll(a, SSM_T, 1), b))
        return jnp.concatenate(even, axis=1), jnp.concatenate(odd, axis=1)

    def store_rows(y_even, y_odd, ref):
        for k in range(SSM_GROUP // 2):
            te = y_even[:, k * ROW_TOKENS:(k + 1) * ROW_TOKENS]
            to = y_odd[:, k * ROW_TOKENS:(k + 1) * ROW_TOKENS]
            tiles = (ref.shape[0], ROW_GROUP, ROW_TOKENS)
            ref[:, 2 * k] = jnp.where(lo, te, pltpu.roll(to, SSM_T, 1)).reshape(tiles)
            ref[:, 2 * k + 1] = jnp.where(lo, pltpu.roll(te, SSM_T, 1), to).reshape(tiles)

    pe, po = chunk_rows(up_ref)
    se, so = chunk_rows(us_ref)
    rp, rs = pe.shape[0], se.shape[0]
    u = jnp.concatenate([pe, po, se, so], axis=0).astype(BF16)
    s12 = jnp.dot(u, p_ref[...], preferred_element_type=F32)
    s1, s2 = s12[:, :STATE_LANES], s12[:, STATE_LANES:]
    a1, a2 = coef_ref[0:1, :], coef_ref[1:2, :]

    e1, e2, o1, o2 = s1[:rp], s2[:rp], s1[rp:2 * rp], s2[rp:2 * rp]
    x1, x2 = _cmul(a1, a2, e1, e2)
    x1, x2 = x1 + o1, x2 + o2
    pos = lax.broadcasted_iota(jnp.int32, (rp, STATE_LANES), 0) & (rows_per_seq - 1)
    for k in range(rows_per_seq.bit_length() - 1):
        d = 1 << k
        b1, b2 = coef_ref[2 + 2 * k:3 + 2 * k, :], coef_ref[3 + 2 * k:4 + 2 * k, :]
        sh1 = jnp.where(pos >= d, pltpu.roll(x1, d, 0), 0.0)
        sh2 = jnp.where(pos >= d, pltpu.roll(x2, d, 0), 0.0)
        y1, y2 = _cmul(b1, b2, sh1, sh2)
        x1, x2 = x1 + y1, x2 + y2
    g1 = jnp.where(pos >= 1, pltpu.roll(x1, 1, 0), 0.0)
    g2 = jnp.where(pos >= 1, pltpu.roll(x2, 1, 0), 0.0)
    ho1 = _cmul(a1, a2, g1, g2)[0] + e1
    hp_ref[...] = jnp.zeros(hp_ref.shape, F32)
    for b in range(rp // rows_per_seq):
        last = (b + 1) * rows_per_seq - 1
        hp_ref[b:b + 1, :] = x1[last:last + 1, :]

    h0e, h0o = h0e_ref[...], h0o_ref[...]
    hse_ref[...] = a1 * h0e + a2 * h0es_ref[...] + s1[2 * rp:2 * rp + rs]
    hso_ref[...] = a1 * h0o + a2 * h0os_ref[...] + s1[2 * rp + rs:]

    hprev = jnp.concatenate([g1, ho1, h0e, h0o], axis=0).astype(BF16)
    y = jnp.dot(u, m_ref[...], preferred_element_type=F32)
    y += lax.dot_general(hprev, qt_ref[...], (((1,), (1,)), ((), ())), preferred_element_type=F32)
    store_rows(y[:rp], y[rp:2 * rp], yp_ref)
    store_rows(y[2 * rp:2 * rp + rs], y[2 * rp + rs:], ys_ref)


def _ssm(table_params, u3p, u3s, h0e, h0o, h0es, h0os, *, rows_per_seq):
    gp, gs = u3p.shape[0], u3s.shape[0]
    rp, rs = gp * ROW_GROUP, gs * ROW_GROUP
    u3p = u3p.reshape(gp, SSM_WIDTH, ROW_GROUP, ROW_TOKENS)
    u3s = u3s.reshape(gs, SSM_WIDTH, ROW_GROUP, ROW_TOKENS)
    assert rows_per_seq & (rows_per_seq - 1) == 0 and 2 * rows_per_seq.bit_length() <= COEF_ROWS
    g3 = lambda g: (g, 0, 0)
    u_spec = lambda groups: pl.BlockSpec((groups, SSM_GROUP, ROW_GROUP, ROW_TOKENS),
                                         lambda g: (0, g, 0, 0))
    h_spec = pl.BlockSpec((None, rs, STATE_LANES), g3)
    row_spec = pl.BlockSpec((None, 1, STATE_LANES), g3)
    mat_spec = pl.BlockSpec((None, SSM_GROUP, STATE_LANES), g3)
    return pl.pallas_call(
        functools.partial(_ssm_kernel, rows_per_seq=rows_per_seq),
        grid=(SSM_GROUPS,),
        in_specs=[
            row_spec, row_spec, row_spec, mat_spec, mat_spec, mat_spec, mat_spec,
            u_spec(gp), u_spec(gs),
            h_spec, h_spec, h_spec, h_spec,
        ],
        out_specs=[
            u_spec(gp), u_spec(gs),
            pl.BlockSpec((None, STATE_OUT_ROWS, STATE_LANES), g3),
            h_spec, h_spec,
        ],
        out_shape=[
            jax.ShapeDtypeStruct((gp, SSM_WIDTH, ROW_GROUP, ROW_TOKENS), F32),
            jax.ShapeDtypeStruct((gs, SSM_WIDTH, ROW_GROUP, ROW_TOKENS), F32),
            jax.ShapeDtypeStruct((SSM_GROUPS, STATE_OUT_ROWS, STATE_LANES), F32),
            jax.ShapeDtypeStruct((SSM_GROUPS, rs, STATE_LANES), F32),
            jax.ShapeDtypeStruct((SSM_GROUPS, rs, STATE_LANES), F32),
        ],
        scratch_shapes=[
            pltpu.VMEM((SSM_CW, SSM_CW), BF16),
            pltpu.VMEM((SSM_CW, 2 * STATE_LANES), BF16),
            pltpu.VMEM((SSM_CW, STATE_LANES), BF16),
            pltpu.VMEM((COEF_ROWS, STATE_LANES), F32),
        ],
        compiler_params=pltpu.CompilerParams(dimension_semantics=("arbitrary",)),
        name="ssm",
    )(*table_params, u3p, u3s, h0e, h0o, h0es, h0os)


def _post_kernel(x_ref, at_ref, ut_ref, yt_ref, lng_ref, lnb_ref, d_ref, wglut_ref, bglu_ref,
                 wout_ref, ln1g_ref, ln1b_ref, wgu_ref, wdown_ref, ln2g_ref, ln2b_ref, o_ref,
                 *, alpha):
    xn = _layer_norm(x_ref[...], lng_ref[...], lnb_ref[...])
    n_rows = x_ref.shape[0] // ROW_TOKENS
    first_row = (pl.program_id(0) % (ROW_GROUP // n_rows)) * n_rows

    def feature_major(ref):
        return jnp.concatenate(
            [ref[pl.ds(first_row + j, SSM_WIDTH, stride=ROW_GROUP), :] for j in range(n_rows)], axis=1)

    ys = feature_major(yt_ref) + d_ref[...] * feature_major(ut_ref)
    gl = 0.5 * ys * (1.0 + lax.erf(ys * math.sqrt(0.5)))
    z = jnp.dot(wglut_ref[...], gl.astype(BF16), preferred_element_type=F32) + bglu_ref[...]
    s = gl * jax.nn.sigmoid(z)
    tn = (((0,), (0,)), ((), ()))
    mix = lax.dot_general(at_ref[...], wout_ref[:PROJ_Q, :], tn, preferred_element_type=F32)
    mix += lax.dot_general(s.astype(BF16), wout_ref[PROJ_Q:, :], tn, preferred_element_type=F32)
    h = _layer_norm(alpha * xn + mix, ln1g_ref[...], ln1b_ref[...])
    hb = h.astype(BF16)
    f = jnp.zeros_like(h)
    for j in range(D_FF // FF_BLOCK):
        cols = slice(j * FF_BLOCK, (j + 1) * FF_BLOCK)
        up_cols = slice(D_FF + j * FF_BLOCK, D_FF + (j + 1) * FF_BLOCK)
        g = jnp.dot(hb, wgu_ref[:, cols], preferred_element_type=F32)
        up = jnp.dot(hb, wgu_ref[:, up_cols], preferred_element_type=F32)
        act = (g * jax.nn.sigmoid(g)) * up
        f += jnp.dot(act.astype(BF16), wdown_ref[cols, :], preferred_element_type=F32)
    o_ref[...] = _layer_norm(alpha * h + f, ln2g_ref[...], ln2b_ref[...])


def _post(x2d, at2d, ut3, yt3, ln_g, ln_b, d_col, w_glu_t, b_glu_col, w_out, ln1_g, ln1_b,
          w_gate_up, w_down, ln2_g, ln2_b, *, alpha, tm):
    n = x2d.shape[0]
    assert PROJ_TILE % tm == 0
    group = lambda i: (i // (PROJ_TILE // tm), 0, 0)
    group_spec = pl.BlockSpec((None, SSM_WIDTH * ROW_GROUP, ROW_TOKENS), group)
    row = lambda i: (i, 0)
    col = lambda i: (0, i)
    const = lambda i: (0, 0)
    resident = lambda shape: pl.BlockSpec(shape, const, pipeline_mode=pl.Buffered(1))
    vec = lambda width: pl.BlockSpec((1, width), const)
    colvec = pl.BlockSpec((SSM_WIDTH, 1), const)
    return pl.pallas_call(
        functools.partial(_post_kernel, alpha=alpha),
        grid=(n // tm,),
        in_specs=[
            pl.BlockSpec((tm, D_MODEL), row),
            pl.BlockSpec((PROJ_Q, tm), col),
            group_spec, group_spec,
            vec(D_MODEL), vec(D_MODEL), colvec,
            resident((SSM_WIDTH, SSM_WIDTH)), colvec,
            resident((D_MODEL, D_MODEL)), vec(D_MODEL), vec(D_MODEL),
            resident((D_MODEL, 2 * D_FF)), resident((D_FF, D_MODEL)),
            vec(D_MODEL), vec(D_MODEL),
        ],
        out_specs=pl.BlockSpec((tm, D_MODEL), row),
        out_shape=jax.ShapeDtypeStruct((n, D_MODEL), F32),
        compiler_params=pltpu.CompilerParams(
            dimension_semantics=("arbitrary",), vmem_limit_bytes=V7X_VMEM_LIMIT_BYTES),
        name="post",
    )(x2d, at2d, ut3, yt3, ln_g, ln_b, d_col, w_glu_t, b_glu_col, w_out, ln1_g, ln1_b,
      w_gate_up, w_down, ln2_g, ln2_b)


def kernel(x_prompt, x_sample, cache_win_k, cache_win_v, state_ssm_re, state_ssm_im,
           ln_in_g, ln_in_b, w_in, attn_sinks, ssm_lambda_re, ssm_lambda_im, ssm_log_step,
           ssm_b_re, ssm_b_im, ssm_c_re, ssm_c_im, ssm_d, w_glu, b_glu, w_out,
           ln1_g, ln1_b, w_gate_up, w_down, ln2_g, ln2_b):
    depth = w_in.shape[0]
    assert depth == 1, "single-layer step"
    bp, lp, _ = x_prompt.shape
    bs, ls, _ = x_sample.shape
    assert ls == SSM_T and bs % 2 == 0 and lp % 512 == 0 and bp <= STATE_OUT_ROWS
    win_rows = cache_win_k.shape[2]
    assert win_rows == WINDOW
    alpha = (2.0 * depth) ** 0.25
    l = 0
    row = lambda a: a.reshape(1, -1)
    column = lambda a: a.reshape(-1, 1)

    w_in_b = w_in[l].astype(BF16)
    w_kv = w_in_b[:, PROJ_Q:PROJ_Q + 2 * PROJ_KV]
    w_qvu_t = jnp.concatenate([w_in_b[:, :PROJ_Q], w_in_b[:, PROJ_Q + PROJ_KV:]], axis=1).T
    lng, lnb = row(ln_in_g), row(ln_in_b)
    xp2 = x_prompt.reshape(bp * lp, D_MODEL)
    xs2 = x_sample.reshape(bs * ls, D_MODEL)
    qtp, vtp, u3p, kp, vp = _proj(xp2, lng, lnb, w_kv, w_qvu_t)
    qts, vts, u3s, ks, vs = _proj(xs2, lng, lnb, w_kv, w_qvu_t)

    sink_rows = jnp.repeat(attn_sinks[l].reshape(N_KV_HEADS, Q_PER_KV), PAIR, axis=1)[:, None, :]
    atp = _attention_prompt(sink_rows, qtp, kp, vtp, bp, lp, n_pairs=4)
    ck = cache_win_k[l].reshape(bs, win_rows, PROJ_KV)
    cv = cache_win_v[l].reshape(bs, win_rows, PROJ_KV)
    cvt = jnp.transpose(cv, (2, 0, 1)).reshape(PROJ_KV, bs * win_rows).astype(BF16)
    ats = _attention_sample(sink_rows, qts, ck, ks, cvt, vts, n_pairs=4)
    kp3, vp3 = kp.reshape(bp, lp, PROJ_KV), vp.reshape(bp, lp, PROJ_KV)
    ks3, vs3 = ks.reshape(bs, ls, PROJ_KV), vs.reshape(bs, ls, PROJ_KV)

    table_params = _table_params(
        ssm_lambda_re[l], ssm_lambda_im[l], ssm_log_step[l],
        ssm_b_re[l], ssm_b_im[l], ssm_c_re[l], ssm_c_im[l])
    sre = jnp.swapaxes(state_ssm_re[l], 0, 1)
    sim = jnp.swapaxes(state_ssm_im[l], 0, 1)
    h0 = jnp.concatenate([sre, sim], axis=-1)
    h0s = jnp.concatenate([sim, sre], axis=-1)
    y4p, y4s, hfin_p, hfin_e, hfin_o = _ssm(
        table_params, u3p, u3s, h0[:, 0::2], h0[:, 1::2], h0s[:, 0::2], h0s[:, 1::2],
        rows_per_seq=lp // ROW_TOKENS)
    hfin_s = jnp.stack([hfin_e, hfin_o], axis=2).reshape(SSM_GROUPS, bs, STATE_LANES)

    post_args = (lng, lnb, column(ssm_d[l]), w_glu[l].T.astype(BF16), column(b_glu[l]),
                 w_out[l].astype(BF16), row(ln1_g[l]), row(ln1_b[l]),
                 w_gate_up[l].astype(BF16), w_down[l].astype(BF16), row(ln2_g[l]), row(ln2_b[l]))
    out_p = _post(xp2, atp, u3p, y4p.reshape(u3p.shape), *post_args, alpha=alpha, tm=512)
    out_s = _post(xs2, ats, u3s, y4s.reshape(u3s.shape), *post_args, alpha=alpha, tm=512)

    kv_shape = (N_KV_HEADS, HEAD_DIM)
    win_k_p = kp3[:, -win_rows:].reshape(1, bp, win_rows, *kv_shape)
    win_v_p = vp3[:, -win_rows:].reshape(1, bp, win_rows, *kv_shape)
    win_k_s = jnp.concatenate([ck, ks3], axis=1)[:, -win_rows:].reshape(1, bs, win_rows, *kv_shape)
    win_v_s = jnp.concatenate([cv, vs3], axis=1)[:, -win_rows:].reshape(1, bs, win_rows, *kv_shape)
    state = lambda hf, n: jnp.swapaxes(hf[:, :n], 0, 1)
    sp, ss = state(hfin_p, bp), state(hfin_s, bs)
    return (out_p.reshape(bp, lp, D_MODEL), out_s.reshape(bs, ls, D_MODEL),
            win_k_p, win_v_p, sp[None, ..., :SSM_STATE], sp[None, ..., SSM_STATE:],
            win_k_s, win_v_s, ss[None, ..., :SSM_STATE], ss[None, ..., SSM_STATE:])
```

```python
import functools
import math

import jax
import jax.numpy as jnp
from jax import lax
from jax.experimental import pallas as pl
from jax.experimental.pallas import tpu as pltpu

F32 = jnp.float32
BF16 = jnp.bfloat16

D_MODEL = 1024
HEAD_DIM = 64
N_HEADS = 8
N_KV_HEADS = 2
Q_PER_KV = N_HEADS // N_KV_HEADS
CHUNK = 64
WINDOW = 128
WIN_CHUNKS = WINDOW // CHUNK
BAND = (WIN_CHUNKS + 1) * CHUNK
PROJ_Q = N_HEADS * HEAD_DIM
PROJ_KV = N_KV_HEADS * HEAD_DIM
SSM_WIDTH = 512
SSM_GROUP = 16
SSM_GROUPS = SSM_WIDTH // SSM_GROUP
SSM_STATE = 64
STATE_LANES = 2 * SSM_STATE
D_FF = 2816
D_IN_PROJ = PROJ_Q + 2 * PROJ_KV + SSM_WIDTH
LN_EPS = 1e-5
NEG_INF = -1e30

SSM_T = CHUNK
SSM_CW = SSM_T * SSM_GROUP
ROW_TOKENS = 2 * SSM_T
ROW_GROUP = 8
PROJ_TILE = ROW_GROUP * ROW_TOKENS
PROJ_PARTS = 2
COEF_ROWS = 16
STATE_OUT_ROWS = 8
M_BLOCK_CHANNELS = 4
FF_BLOCK = 256
V7X_VMEM_LIMIT_BYTES = 56 * 1024 * 1024


def _layer_norm(x, g, b):
    mu = jnp.mean(x, axis=-1, keepdims=True)
    xc = x - mu
    var = jnp.mean(xc * xc, axis=-1, keepdims=True)
    return xc * lax.rsqrt(var + LN_EPS) * g + b


KV = 2 * PROJ_KV
QVU = PROJ_Q + PROJ_KV + SSM_WIDTH


def _proj_kernel(x_ref, g_ref, b_ref, w_ref, wt_ref, qt_ref, vt_ref, ut_ref, k_ref, v_ref):
    rows_per_part = ROW_GROUP // PROJ_PARTS
    for part in range(PROJ_PARTS):
        tok = slice(part * rows_per_part * ROW_TOKENS, (part + 1) * rows_per_part * ROW_TOKENS)
        xb = _layer_norm(x_ref[tok, :], g_ref[...], b_ref[...]).astype(BF16)
        p = jnp.dot(xb, w_ref[...], preferred_element_type=F32)
        k_ref[tok, :] = p[:, :PROJ_KV]
        v_ref[tok, :] = p[:, PROJ_KV:]
        pt = lax.dot_general(wt_ref[...], xb, (((1,), (1,)), ((), ())), preferred_element_type=F32)
        qt_ref[:, tok] = (pt[:PROJ_Q] * (HEAD_DIM ** -0.5)).astype(BF16)
        vt_ref[:, tok] = pt[PROJ_Q:PROJ_Q + PROJ_KV].astype(BF16)
        for j in range(rows_per_part):
            ut_ref[pl.ds(part * rows_per_part + j, SSM_WIDTH, stride=ROW_GROUP), :] = (
                pt[PROJ_Q + PROJ_KV:, j * ROW_TOKENS:(j + 1) * ROW_TOKENS])


def _proj(x2d, ln_g, ln_b, w_kv, w_qvu_t):
    n = x2d.shape[0]
    tm = PROJ_TILE
    const = lambda i: (0, 0)
    row = lambda i: (i, 0)
    col = lambda i: (0, i)
    return pl.pallas_call(
        _proj_kernel,
        grid=(n // tm,),
        in_specs=[
            pl.BlockSpec((tm, D_MODEL), row),
            pl.BlockSpec((1, D_MODEL), const),
            pl.BlockSpec((1, D_MODEL), const),
            pl.BlockSpec((D_MODEL, KV), const),
            pl.BlockSpec((QVU, D_MODEL), const),
        ],
        out_specs=[
            pl.BlockSpec((PROJ_Q, tm), col),
            pl.BlockSpec((PROJ_KV, tm), col),
            pl.BlockSpec((None, SSM_WIDTH * ROW_GROUP, ROW_TOKENS), lambda i: (i, 0, 0)),
            pl.BlockSpec((tm, PROJ_KV), row),
            pl.BlockSpec((tm, PROJ_KV), row),
        ],
        out_shape=[
            jax.ShapeDtypeStruct((PROJ_Q, n), BF16),
            jax.ShapeDtypeStruct((PROJ_KV, n), BF16),
            jax.ShapeDtypeStruct((n // tm, SSM_WIDTH * ROW_GROUP, ROW_TOKENS), F32),
            jax.ShapeDtypeStruct((n, PROJ_KV), F32),
            jax.ShapeDtypeStruct((n, PROJ_KV), F32),
        ],
        compiler_params=pltpu.CompilerParams(
            dimension_semantics=("arbitrary",), vmem_limit_bytes=V7X_VMEM_LIMIT_BYTES),
        name="proj",
    )(x2d, ln_g, ln_b, w_kv, w_qvu_t)


PAIR = 2 * CHUNK
HEAD_LANES = Q_PER_KV * PAIR
ONES_ROWS = 16


def _scores(unit, qt_ref):
    kwin, _, lanes, _, _, h = unit
    base = h * Q_PER_KV * HEAD_DIM
    qrow = jnp.concatenate(
        [qt_ref[base + g * HEAD_DIM:base + (g + 1) * HEAD_DIM, lanes] for g in range(Q_PER_KV)],
        axis=1)
    zero = jnp.zeros_like(qrow)
    qstack = jnp.concatenate([qrow, zero] if h == 0 else [zero, qrow], axis=0)
    return jnp.dot(kwin, qstack, preferred_element_type=F32)


def _finish(unit, s, sink_ref, o_ref):
    _, vtwin, lanes, valid, masked_rows, h = unit
    nk = s.shape[0]
    base = h * Q_PER_KV * HEAD_DIM
    pieces, done = [], 0
    for start, stop in masked_rows:
        if start > done:
            pieces.append(s[done:start])
        pieces.append(jnp.where(valid[start:stop], s[start:stop], NEG_INF))
        done = stop
    if done < nk:
        pieces.append(s[done:])
    s = jnp.concatenate(pieces, axis=0)
    sink = sink_ref[h]
    m = jnp.maximum(jnp.max(s, axis=0, keepdims=True), sink)
    p = jnp.exp(s - m).astype(BF16)
    v_ones = jnp.concatenate(
        [vtwin[h * HEAD_DIM:(h + 1) * HEAD_DIM, :], jnp.ones((ONES_ROWS, nk), BF16)], axis=0)
    ov = jnp.dot(v_ones, p, preferred_element_type=F32)
    den = ov[HEAD_DIM:HEAD_DIM + 1, :] + jnp.exp(sink - m)
    o = ov[:HEAD_DIM, :] * (1.0 / den)
    for g in range(Q_PER_KV):
        o_ref[base + g * HEAD_DIM:base + (g + 1) * HEAD_DIM, lanes] = (
            o[:, g * PAIR:(g + 1) * PAIR].astype(BF16))


def _attend_units(units, qt_ref, sink_ref, o_ref):
    s_next = _scores(units[0], qt_ref)
    for i, unit in enumerate(units):
        s = s_next
        if i + 1 < len(units):
            s_next = _scores(units[i + 1], qt_ref)
        _finish(unit, s, sink_ref, o_ref)


def _attn_prompt_kernel(sink_ref, qt_ref, kp_ref, kc_ref, vtp_ref, vtc_ref, o_ref, *, n_pairs):
    tq = n_pairs * PAIR
    nk = WINDOW + PAIR
    kk = jnp.concatenate([kp_ref[...], kc_ref[...]], axis=0).astype(BF16)
    vt = jnp.concatenate([vtp_ref[...], vtc_ref[...]], axis=1)
    r = lax.broadcasted_iota(jnp.int32, (nk, HEAD_LANES), 0)
    first_chunk = (lax.broadcasted_iota(jnp.int32, (nk, HEAD_LANES), 1) & (PAIR - 1)) < CHUNK
    lo = jnp.where(first_chunk, 0, CHUNK)
    hi = jnp.where(first_chunk, BAND, nk)
    first_pos = pl.program_id(1) * tq - WINDOW
    units = []
    for pp in range(n_pairs):
        lo_pp = jnp.maximum(lo, -first_pos) if pp == 0 else lo
        valid = (r >= lo_pp) & (r < hi)
        masked_rows = ((0, nk),) if pp == 0 else ((0, CHUNK), (nk - CHUNK, nk))
        for h in range(N_KV_HEADS):
            units.append((kk[pp * PAIR:pp * PAIR + nk], vt[:, pp * PAIR:pp * PAIR + nk],
                          slice(pp * PAIR, (pp + 1) * PAIR), valid, masked_rows, h))
    _attend_units(units, qt_ref, sink_ref, o_ref)


def _attn_sample_kernel(sink_ref, qt_ref, ck_ref, kn_ref, cvt_ref, vtn_ref, o_ref, *, n_pairs):
    nk = 2 * WINDOW + PAIR
    r = lax.broadcasted_iota(jnp.int32, (nk, HEAD_LANES), 0)
    query_seq = (lax.broadcasted_iota(jnp.int32, (nk, HEAD_LANES), 1) & (PAIR - 1)) >> 6
    key_seq = jnp.where(r < 2 * WINDOW, r >> 7, (r - 2 * WINDOW) >> 6)
    valid = query_seq == key_seq
    units = []
    for pp in range(n_pairs):
        lanes = slice(pp * PAIR, (pp + 1) * PAIR)
        kwin = jnp.concatenate([ck_ref[2 * pp], ck_ref[2 * pp + 1], kn_ref[lanes, :]],
                               axis=0).astype(BF16)
        vtwin = jnp.concatenate([cvt_ref[:, 2 * pp * WINDOW:2 * (pp + 1) * WINDOW], vtn_ref[:, lanes]],
                                axis=1)
        units.extend((kwin, vtwin, lanes, valid, ((0, nk),), h) for h in range(N_KV_HEADS))
    _attend_units(units, qt_ref, sink_ref, o_ref)


_SINK_SPEC = pl.BlockSpec((N_KV_HEADS, 1, HEAD_LANES), lambda *_: (0, 0, 0))


def _attention_prompt(sink_rows, qt, k, vt, bsz, seq, n_pairs):
    tq = n_pairs * PAIR
    nt = seq // tq
    wpt = tq // WINDOW
    cur_c = lambda b, i: (0, b * nt + i)
    cur_r = lambda b, i: (b * nt + i, 0)
    prev = lambda b, i: jnp.maximum((b * nt + i) * wpt - 1, 0)
    return pl.pallas_call(
        functools.partial(_attn_prompt_kernel, n_pairs=n_pairs),
        grid=(bsz, nt),
        in_specs=[
            _SINK_SPEC,
            pl.BlockSpec((PROJ_Q, tq), cur_c),
            pl.BlockSpec((WINDOW, PROJ_KV), lambda b, i: (prev(b, i), 0)),
            pl.BlockSpec((tq, PROJ_KV), cur_r),
            pl.BlockSpec((PROJ_KV, WINDOW), lambda b, i: (0, prev(b, i))),
            pl.BlockSpec((PROJ_KV, tq), cur_c),
        ],
        out_specs=pl.BlockSpec((PROJ_Q, tq), cur_c),
        out_shape=jax.ShapeDtypeStruct((PROJ_Q, bsz * seq), BF16),
        compiler_params=pltpu.CompilerParams(dimension_semantics=("arbitrary", "arbitrary")),
        name="attn_prompt",
    )(sink_rows, qt, k, k, vt, vt)


def _attention_sample(sink_rows, qt, cache_k, k, cache_vt, vt, n_pairs):
    n = qt.shape[1]
    tq = n_pairs * PAIR
    return pl.pallas_call(
        functools.partial(_attn_sample_kernel, n_pairs=n_pairs),
        grid=(n // tq,),
        in_specs=[
            _SINK_SPEC,
            pl.BlockSpec((PROJ_Q, tq), lambda i: (0, i)),
            pl.BlockSpec((2 * n_pairs, WINDOW, PROJ_KV), lambda i: (i, 0, 0)),
            pl.BlockSpec((tq, PROJ_KV), lambda i: (i, 0)),
            pl.BlockSpec((PROJ_KV, 2 * n_pairs * WINDOW), lambda i: (0, i)),
            pl.BlockSpec((PROJ_KV, tq), lambda i: (0, i)),
        ],
        out_specs=pl.BlockSpec((PROJ_Q, tq), lambda i: (0, i)),
        out_shape=jax.ShapeDtypeStruct((PROJ_Q, n), BF16),
        compiler_params=pltpu.CompilerParams(dimension_semantics=("arbitrary",)),
        name="attn_sample",
    )(sink_rows, qt, cache_k, k, cache_vt, vt)


def _power_table(tau, nbits, a_re, a_im):
    rows = tau.shape[0]
    w_re = jnp.ones((rows, STATE_LANES), F32)
    w_im = jnp.zeros((rows, STATE_LANES), F32)
    p_re, p_im = a_re, a_im
    for k in range(nbits):
        bit = ((tau >> k) & 1) == 1
        f_re = jnp.where(bit, p_re, 1.0)
        f_im = jnp.where(bit, p_im, 0.0)
        w_re, w_im = w_re * f_re - w_im * f_im, w_re * f_im + w_im * f_re
        p_re, p_im = p_re * p_re - p_im * p_im, 2.0 * p_re * p_im
    return w_re, w_im


def _build_tables(lr_ref, li_ref, ls_ref, bre_ref, bim_ref, cre_ref, cim_ref,
                  p_ref, qt_ref, coef_ref):
    lo = lax.broadcasted_iota(jnp.int32, (1, STATE_LANES), 1) < SSM_STATE
    lr, li = lr_ref[...], li_ref[...]
    dt = jnp.exp(ls_ref[...])
    mag = jnp.exp(lr * dt)
    a_re, a_im = mag * jnp.cos(li * dt), mag * jnp.sin(li * dt)
    nr, ni = a_re - 1.0, a_im
    den = lr * lr + li * li
    f_re, f_im = (nr * lr + ni * li) / den, (ni * lr - nr * li) / den
    b_re, b_im = bre_ref[...], bim_ref[...]
    bb_re = f_re * b_re - f_im * b_im
    bb_im = f_re * b_im + f_im * b_re
    c_re, c_im = cre_ref[...], cim_ref[...]

    tau = lax.broadcasted_iota(jnp.int32, (SSM_T, 1), 0)
    w_re, w_im = _power_table(tau, 6, a_re, a_im)
    w1_re, w1_im = _power_table(tau + 1, 7, a_re, a_im)
    wr_re, wr_im = _power_table(SSM_T - 1 - tau, 6, a_re, a_im)

    def outer(c, w):
        return (c[:, None, :] * w[None, :, :]).reshape(SSM_CW, STATE_LANES)

    cw_mix = (outer(c_re, jnp.where(lo, w_re, w_im)) + outer(c_im, jnp.where(lo, -w_im, w_re)))
    bb_mix = jnp.where(lo, bb_re, -bb_im)
    strip = lax.dot_general(bb_mix, cw_mix, (((1,), (1,)), ((), ())),
                            precision=lax.Precision.HIGHEST,
                            preferred_element_type=F32)

    x_a, y_a = jnp.where(lo, bb_re, bb_im), jnp.where(lo, -bb_im, bb_re)
    x_b, y_b = jnp.where(lo, bb_im, bb_re), jnp.where(lo, bb_re, -bb_im)
    p_ref[:, :STATE_LANES] = (outer(x_a, wr_re) + outer(y_a, wr_im)).astype(BF16)
    p_ref[:, STATE_LANES:] = (outer(x_b, wr_re) + outer(y_b, wr_im)).astype(BF16)

    qt_ref[...] = (outer(c_re, jnp.where(lo, w1_re, -w1_im))
                   + outer(c_im, jnp.where(lo, -w1_im, -w1_re))).astype(BF16)

    t_re, t_im = a_re, a_im
    for _ in range(6):
        t_re, t_im = t_re * t_re - t_im * t_im, 2.0 * t_re * t_im
    for k in range(COEF_ROWS // 2):
        coef_ref[2 * k:2 * k + 1, :] = t_re
        coef_ref[2 * k + 1:2 * k + 2, :] = jnp.where(lo, -t_im, t_im)
        t_re, t_im = t_re * t_re - t_im * t_im, 2.0 * t_re * t_im
    return strip


def _toeplitz_rows(strip, m_ref, channels):
    s_idx = lax.broadcasted_iota(jnp.int32, (SSM_T, SSM_CW), 0)
    t_idx = lax.broadcasted_iota(jnp.int32, (SSM_T, SSM_CW), 1) & (SSM_T - 1)
    causal = t_idx >= s_idx
    for c in channels:
        rows = jnp.broadcast_to(strip[c:c + 1, :], (SSM_T, SSM_CW))
        shifted = pltpu.roll(rows, 0, 1, stride=1, stride_axis=0)
        m_ref[c * SSM_T:(c + 1) * SSM_T, :] = jnp.where(causal, shifted, 0.0).astype(BF16)


def _table_params(lam_re, lam_im, log_step, b_re, b_im, c_re, c_im):
    dup = lambda a: jnp.concatenate([a, a], axis=-1)
    lr = dup(lam_re)[:, None, :]
    li = dup(lam_im)[:, None, :]
    ls = jnp.broadcast_to(log_step[:, None, None], (SSM_GROUPS, 1, STATE_LANES))
    bt_re = dup(jnp.swapaxes(b_re, 1, 2))
    bt_im = dup(jnp.swapaxes(b_im, 1, 2))
    return lr, li, ls, bt_re, bt_im, dup(c_re), dup(c_im)


def _cmul(a1, a2, h, hs):
    return a1 * h + a2 * hs, a1 * hs - a2 * h


def _ssm_kernel(lr_ref, li_ref, ls_ref, bre_ref, bim_ref, cre_ref, cim_ref,
                up_ref, us_ref, h0e_ref, h0o_ref, h0es_ref, h0os_ref,
                yp_ref, ys_ref, hp_ref, hse_ref, hso_ref,
                m_ref, p_ref, qt_ref, coef_ref, *, rows_per_seq):
    strip = _build_tables(lr_ref, li_ref, ls_ref, bre_ref, bim_ref, cre_ref, cim_ref,
                          p_ref, qt_ref, coef_ref)
    lo = lax.broadcasted_iota(jnp.int32, (1, ROW_TOKENS), 1) < SSM_T

    def chunk_rows(ref):
        even, odd = [], []
        rows = ref.shape[0] * ROW_GROUP
        for k in range(SSM_GROUP // 2):
            a = ref[:, 2 * k].reshape(rows, ROW_TOKENS)
            b = ref[:, 2 * k + 1].reshape(rows, ROW_TOKENS)
            even.append(jnp.where(lo, a, pltpu.roll(b, SSM_T, 1)))
            odd.append(jnp.where(lo, pltpu.roll(a, SSM_T, 1), b))
        return jnp.concatenate(even, axis=1), jnp.concatenate(odd, axis=1)

    def store_rows(y_even, y_odd, ref):
        for k in range(SSM_GROUP // 2):
            te = y_even[:, k * ROW_TOKENS:(k + 1) * ROW_TOKENS]
            to = y_odd[:, k * ROW_TOKENS:(k + 1) * ROW_TOKENS]
            tiles = (ref.shape[0], ROW_GROUP, ROW_TOKENS)
            ref[:, 2 * k] = jnp.where(lo, te, pltpu.roll(to, SSM_T, 1)).reshape(tiles)
            ref[:, 2 * k + 1] = jnp.where(lo, pltpu.roll(te, SSM_T, 1), to).reshape(tiles)

    pe, po = chunk_rows(up_ref)
    se, so = chunk_rows(us_ref)
    rp, rs = pe.shape[0], se.shape[0]
    u = jnp.concatenate([pe, po, se, so], axis=0).astype(BF16)
    s12 = jnp.dot(u, p_ref[...], preferred_element_type=F32)
    s1, s2 = s12[:, :STATE_LANES], s12[:, STATE_LANES:]
    a1, a2 = coef_ref[0:1, :], coef_ref[1:2, :]

    e1, e2, o1, o2 = s1[:rp], s2[:rp], s1[rp:2 * rp], s2[rp:2 * rp]
    x1, x2 = _cmul(a1, a2, e1, e2)
    x1, x2 = x1 + o1, x2 + o2
    pos = lax.broadcasted_iota(jnp.int32, (rp, STATE_LANES), 0) & (rows_per_seq - 1)
    for k in range(rows_per_seq.bit_length() - 1):
        d = 1 << k
        b1, b2 = coef_ref[2 + 2 * k:3 + 2 * k, :], coef_ref[3 + 2 * k:4 + 2 * k, :]
        sh1 = jnp.where(pos >= d, pltpu.roll(x1, d, 0), 0.0)
        sh2 = jnp.where(pos >= d, pltpu.roll(x2, d, 0), 0.0)
        y1, y2 = _cmul(b1, b2, sh1, sh2)
        x1, x2 = x1 + y1, x2 + y2
    g1 = jnp.where(pos >= 1, pltpu.roll(x1, 1, 0), 0.0)
    g2 = jnp.where(pos >= 1, pltpu.roll(x2, 1, 0), 0.0)
    ho1 = _cmul(a1, a2, g1, g2)[0] + e1
    hp_ref[...] = jnp.zeros(hp_ref.shape, F32)
    for b in range(rp // rows_per_seq):
        last = (b + 1) * rows_per_seq - 1
        hp_ref[b:b + 1, :] = x1[last:last + 1, :]

    h0e, h0o = h0e_ref[...], h0o_ref[...]
    hse_ref[...] = a1 * h0e + a2 * h0es_ref[...] + s1[2 * rp:2 * rp + rs]
    hso_ref[...] = a1 * h0o + a2 * h0os_ref[...] + s1[2 * rp + rs:]

    hprev = jnp.concatenate([g1, ho1, h0e, h0o], axis=0).astype(BF16)
    y = lax.dot_general(hprev, qt_ref[...], (((1,), (1,)), ((), ())), preferred_element_type=F32)
    blocks = [range(c, c + M_BLOCK_CHANNELS) for c in range(0, SSM_GROUP, M_BLOCK_CHANNELS)]
    _toeplitz_rows(strip, m_ref, blocks[0])
    for i, channels in enumerate(blocks):
        rows = slice(channels[0] * SSM_T, (channels[-1] + 1) * SSM_T)
        y += jnp.dot(u[:, rows], m_ref[rows, :], preferred_element_type=F32)
        if i + 1 < len(blocks):
            _toeplitz_rows(strip, m_ref, blocks[i + 1])
    store_rows(y[:rp], y[rp:2 * rp], yp_ref)
    store_rows(y[2 * rp:2 * rp + rs], y[2 * rp + rs:], ys_ref)


def _ssm(table_params, u3p, u3s, h0e, h0o, h0es, h0os, *, rows_per_seq):
    gp, gs = u3p.shape[0], u3s.shape[0]
    rp, rs = gp * ROW_GROUP, gs * ROW_GROUP
    u3p = u3p.reshape(gp, SSM_WIDTH, ROW_GROUP, ROW_TOKENS)
    u3s = u3s.reshape(gs, SSM_WIDTH, ROW_GROUP, ROW_TOKENS)
    assert rows_per_seq & (rows_per_seq - 1) == 0 and 2 * rows_per_seq.bit_length() <= COEF_ROWS
    g3 = lambda g: (g, 0, 0)
    u_spec = lambda groups: pl.BlockSpec((groups, SSM_GROUP, ROW_GROUP, ROW_TOKENS),
                                         lambda g: (0, g, 0, 0))
    h_spec = pl.BlockSpec((None, rs, STATE_LANES), g3)
    row_spec = pl.BlockSpec((None, 1, STATE_LANES), g3)
    mat_spec = pl.BlockSpec((None, SSM_GROUP, STATE_LANES), g3)
    return pl.pallas_call(
        functools.partial(_ssm_kernel, rows_per_seq=rows_per_seq),
        grid=(SSM_GROUPS,),
        in_specs=[
            row_spec, row_spec, row_spec, mat_spec, mat_spec, mat_spec, mat_spec,
            u_spec(gp), u_spec(gs),
            h_spec, h_spec, h_spec, h_spec,
        ],
        out_specs=[
            u_spec(gp), u_spec(gs),
            pl.BlockSpec((None, STATE_OUT_ROWS, STATE_LANES), g3),
            h_spec, h_spec,
        ],
        out_shape=[
            jax.ShapeDtypeStruct((gp, SSM_WIDTH, ROW_GROUP, ROW_TOKENS), F32),
            jax.ShapeDtypeStruct((gs, SSM_WIDTH, ROW_GROUP, ROW_TOKENS), F32),
            jax.ShapeDtypeStruct((SSM_GROUPS, STATE_OUT_ROWS, STATE_LANES), F32),
            jax.ShapeDtypeStruct((SSM_GROUPS, rs, STATE_LANES), F32),
            jax.ShapeDtypeStruct((SSM_GROUPS, rs, STATE_LANES), F32),
        ],
        scratch_shapes=[
            pltpu.VMEM((SSM_CW, SSM_CW), BF16),
            pltpu.VMEM((SSM_CW, 2 * STATE_LANES), BF16),
            pltpu.VMEM((SSM_CW, STATE_LANES), BF16),
            pltpu.VMEM((COEF_ROWS, STATE_LANES), F32),
        ],
        compiler_params=pltpu.CompilerParams(dimension_semantics=("arbitrary",)),
        name="ssm",
    )(*table_params, u3p, u3s, h0e, h0o, h0es, h0os)


def _post_kernel(x_ref, at_ref, ut_ref, yt_ref, lng_ref, lnb_ref, d_ref, wglut_ref, bglu_ref,
                 wout_ref, ln1g_ref, ln1b_ref, wgu_ref, wdown_ref, ln2g_ref, ln2b_ref, o_ref,
                 *, alpha):
    xn = _layer_norm(x_ref[...], lng_ref[...], lnb_ref[...])
    n_rows = x_ref.shape[0] // ROW_TOKENS
    first_row = (pl.program_id(0) % (ROW_GROUP // n_rows)) * n_rows

    def feature_major(ref):
        return jnp.concatenate(
            [ref[pl.ds(first_row + j, SSM_WIDTH, stride=ROW_GROUP), :] for j in range(n_rows)], axis=1)

    ys = feature_major(yt_ref) + d_ref[...] * feature_major(ut_ref)
    gl = 0.5 * ys * (1.0 + lax.erf(ys * math.sqrt(0.5)))
    z = jnp.dot(wglut_ref[...], gl.astype(BF16), preferred_element_type=F32) + bglu_ref[...]
    s = gl * jax.nn.sigmoid(z)
    tn = (((0,), (0,)), ((), ()))
    mix = lax.dot_general(at_ref[...], wout_ref[:PROJ_Q, :], tn, preferred_element_type=F32)
    mix += lax.dot_general(s.astype(BF16), wout_ref[PROJ_Q:, :], tn, preferred_element_type=F32)
    h = _layer_norm(alpha * xn + mix, ln1g_ref[...], ln1b_ref[...])
    hb = h.astype(BF16)
    f = jnp.zeros_like(h)
    for j in range(D_FF // FF_BLOCK):
        cols = slice(j * FF_BLOCK, (j + 1) * FF_BLOCK)
        up_cols = slice(D_FF + j * FF_BLOCK, D_FF + (j + 1) * FF_BLOCK)
        g = jnp.dot(hb, wgu_ref[:, cols], preferred_element_type=F32)
        up = jnp.dot(hb, wgu_ref[:, up_cols], preferred_element_type=F32)
        act = (g * jax.nn.sigmoid(g)) * up
        f += jnp.dot(act.astype(BF16), wdown_ref[cols, :], preferred_element_type=F32)
    o_ref[...] = _layer_norm(alpha * h + f, ln2g_ref[...], ln2b_ref[...])


def _post(x2d, at2d, ut3, yt3, ln_g, ln_b, d_col, w_glu_t, b_glu_col, w_out, ln1_g, ln1_b,
          w_gate_up, w_down, ln2_g, ln2_b, *, alpha, tm):
    n = x2d.shape[0]
    assert PROJ_TILE % tm == 0
    group = lambda i: (i // (PROJ_TILE // tm), 0, 0)
    group_spec = pl.BlockSpec((None, SSM_WIDTH * ROW_GROUP, ROW_TOKENS), group)
    row = lambda i: (i, 0)
    col = lambda i: (0, i)
    const = lambda i: (0, 0)
    resident = lambda shape: pl.BlockSpec(shape, const, pipeline_mode=pl.Buffered(1))
    vec = lambda width: pl.BlockSpec((1, width), const)
    colvec = pl.BlockSpec((SSM_WIDTH, 1), const)
    return pl.pallas_call(
        functools.partial(_post_kernel, alpha=alpha),
        grid=(n // tm,),
        in_specs=[
            pl.BlockSpec((tm, D_MODEL), row),
            pl.BlockSpec((PROJ_Q, tm), col),
            group_spec, group_spec,
            vec(D_MODEL), vec(D_MODEL), colvec,
            resident((SSM_WIDTH, SSM_WIDTH)), colvec,
            resident((D_MODEL, D_MODEL)), vec(D_MODEL), vec(D_MODEL),
            resident((D_MODEL, 2 * D_FF)), resident((D_FF, D_MODEL)),
            vec(D_MODEL), vec(D_MODEL),
        ],
        out_specs=pl.BlockSpec((tm, D_MODEL), row),
        out_shape=jax.ShapeDtypeStruct((n, D_MODEL), F32),
        compiler_params=pltpu.CompilerParams(
            dimension_semantics=("arbitrary",), vmem_limit_bytes=V7X_VMEM_LIMIT_BYTES),
        name="post",
    )(x2d, at2d, ut3, yt3, ln_g, ln_b, d_col, w_glu_t, b_glu_col, w_out, ln1_g, ln1_b,
      w_gate_up, w_down, ln2_g, ln2_b)


def kernel(x_prompt, x_sample, cache_win_k, cache_win_v, state_ssm_re, state_ssm_im,
           ln_in_g, ln_in_b, w_in, attn_sinks, ssm_lambda_re, ssm_lambda_im, ssm_log_step,
           ssm_b_re, ssm_b_im, ssm_c_re, ssm_c_im, ssm_d, w_glu, b_glu, w_out,
           ln1_g, ln1_b, w_gate_up, w_down, ln2_g, ln2_b):
    depth = w_in.shape[0]
    assert depth == 1, "single-layer step"
    bp, lp, _ = x_prompt.shape
    bs, ls, _ = x_sample.shape
    assert ls == SSM_T and bs % 2 == 0 and lp % 512 == 0 and bp <= STATE_OUT_ROWS
    win_rows = cache_win_k.shape[2]
    assert win_rows == WINDOW
    alpha = (2.0 * depth) ** 0.25
    l = 0
    row = lambda a: a.reshape(1, -1)
    column = lambda a: a.reshape(-1, 1)

    w_in_b = w_in[l].astype(BF16)
    w_kv = w_in_b[:, PROJ_Q:PROJ_Q + 2 * PROJ_KV]
    w_qvu_t = jnp.concatenate([w_in_b[:, :PROJ_Q], w_in_b[:, PROJ_Q + PROJ_KV:]], axis=1).T
    lng, lnb = row(ln_in_g), row(ln_in_b)
    xp2 = x_prompt.reshape(bp * lp, D_MODEL)
    xs2 = x_sample.reshape(bs * ls, D_MODEL)
    qtp, vtp, u3p, kp, vp = _proj(xp2, lng, lnb, w_kv, w_qvu_t)
    qts, vts, u3s, ks, vs = _proj(xs2, lng, lnb, w_kv, w_qvu_t)

    sink_rows = jnp.repeat(attn_sinks[l].reshape(N_KV_HEADS, Q_PER_KV), PAIR, axis=1)[:, None, :]
    atp = _attention_prompt(sink_rows, qtp, kp, vtp, bp, lp, n_pairs=4)
    ck = cache_win_k[l].reshape(bs, win_rows, PROJ_KV)
    cv = cache_win_v[l].reshape(bs, win_rows, PROJ_KV)
    cvt = jnp.transpose(cv, (2, 0, 1)).reshape(PROJ_KV, bs * win_rows).astype(BF16)
    ats = _attention_sample(sink_rows, qts, ck, ks, cvt, vts, n_pairs=4)
    kp3, vp3 = kp.reshape(bp, lp, PROJ_KV), vp.reshape(bp, lp, PROJ_KV)
    ks3, vs3 = ks.reshape(bs, ls, PROJ_KV), vs.reshape(bs, ls, PROJ_KV)

    table_params = _table_params(
        ssm_lambda_re[l], ssm_lambda_im[l], ssm_log_step[l],
        ssm_b_re[l], ssm_b_im[l], ssm_c_re[l], ssm_c_im[l])
    sre = jnp.swapaxes(state_ssm_re[l], 0, 1)
    sim = jnp.swapaxes(state_ssm_im[l], 0, 1)
    h0 = jnp.concatenate([sre, sim], axis=-1)
    h0s = jnp.concatenate([sim, sre], axis=-1)
    y4p, y4s, hfin_p, hfin_e, hfin_o = _ssm(
        table_params, u3p, u3s, h0[:, 0::2], h0[:, 1::2], h0s[:, 0::2], h0s[:, 1::2],
        rows_per_seq=lp // ROW_TOKENS)
    hfin_s = jnp.stack([hfin_e, hfin_o], axis=2).reshape(SSM_GROUPS, bs, STATE_LANES)

    post_args = (lng, lnb, column(ssm_d[l]), w_glu[l].T.astype(BF16), column(b_glu[l]),
                 w_out[l].astype(BF16), row(ln1_g[l]), row(ln1_b[l]),
                 w_gate_up[l].astype(BF16), w_down[l].astype(BF16), row(ln2_g[l]), row(ln2_b[l]))
    out_p = _post(xp2, atp, u3p, y4p.reshape(u3p.shape), *post_args, alpha=alpha, tm=512)
    out_s = _post(xs2, ats, u3s, y4s.reshape(u3s.shape), *post_args, alpha=alpha, tm=512)

    kv_shape = (N_KV_HEADS, HEAD_DIM)
    win_k_p = kp3[:, -win_rows:].reshape(1, bp, win_rows, *kv_shape)
    win_v_p = vp3[:, -win_rows:].reshape(1, bp, win_rows, *kv_shape)
    win_k_s = jnp.concatenate([ck, ks3], axis=1)[:, -win_rows:].reshape(1, bs, win_rows, *kv_shape)
    win_v_s = jnp.concatenate([cv, vs3], axis=1)[:, -win_rows:].reshape(1, bs, win_rows, *kv_shape)
    state = lambda hf, n: jnp.swapaxes(hf[:, :n], 0, 1)
    sp, ss = state(hfin_p, bp), state(hfin_s, bs)
    return (out_p.reshape(bp, lp, D_MODEL), out_s.reshape(bs, ls, D_MODEL),
            win_k_p, win_v_p, sp[None, ..., :SSM_STATE], sp[None, ..., SSM_STATE:],
            win_k_s, win_v_s, ss[None, ..., :SSM_STATE], ss[None, ..., SSM_STATE:])
```

```python
import functools
import math

import jax
import jax.numpy as jnp
from jax import lax
from jax.experimental import pallas as pl
from jax.experimental.pallas import tpu as pltpu

F32 = jnp.float32
BF16 = jnp.bfloat16

D_MODEL = 1024
HEAD_DIM = 64
N_HEADS = 8
N_KV_HEADS = 2
Q_PER_KV = N_HEADS // N_KV_HEADS
CHUNK = 64
WINDOW = 128
WIN_CHUNKS = WINDOW // CHUNK
BAND = (WIN_CHUNKS + 1) * CHUNK
PROJ_Q = N_HEADS * HEAD_DIM
PROJ_KV = N_KV_HEADS * HEAD_DIM
SSM_WIDTH = 512
SSM_GROUP = 16
SSM_GROUPS = SSM_WIDTH // SSM_GROUP
SSM_STATE = 64
STATE_LANES = 2 * SSM_STATE
D_FF = 2816
D_IN_PROJ = PROJ_Q + 2 * PROJ_KV + SSM_WIDTH
LN_EPS = 1e-5
NEG_INF = -1e30

SSM_T = CHUNK
SSM_CW = SSM_T * SSM_GROUP
ROW_TOKENS = 2 * SSM_T
ROW_GROUP = 8
PROJ_TILE = ROW_GROUP * ROW_TOKENS
PROJ_PARTS = 2
COEF_ROWS = 16
STATE_OUT_ROWS = 8
M_BLOCK_CHANNELS = 4
FF_BLOCK = 256
MIX_STAGE_EVERY = 1
V7X_VMEM_LIMIT_BYTES = 56 * 1024 * 1024


def _layer_norm(x, g, b):
    mu = jnp.mean(x, axis=-1, keepdims=True)
    xc = x - mu
    var = jnp.mean(xc * xc, axis=-1, keepdims=True)
    return xc * lax.rsqrt(var + LN_EPS) * g + b


KV = 2 * PROJ_KV
QVU = PROJ_Q + PROJ_KV + SSM_WIDTH


def _proj_kernel(x_ref, g_ref, b_ref, w_ref, wt_ref, qt_ref, vt_ref, ut_ref, k_ref, v_ref):
    rows_per_part = ROW_GROUP // PROJ_PARTS
    for part in range(PROJ_PARTS):
        tok = slice(part * rows_per_part * ROW_TOKENS, (part + 1) * rows_per_part * ROW_TOKENS)
        xb = _layer_norm(x_ref[tok, :], g_ref[...], b_ref[...]).astype(BF16)
        p = jnp.dot(xb, w_ref[...], preferred_element_type=F32)
        k_ref[tok, :] = p[:, :PROJ_KV]
        v_ref[tok, :] = p[:, PROJ_KV:]
        pt = lax.dot_general(wt_ref[...], xb, (((1,), (1,)), ((), ())), preferred_element_type=F32)
        qt_ref[:, tok] = (pt[:PROJ_Q] * (HEAD_DIM ** -0.5)).astype(BF16)
        vt_ref[:, tok] = pt[PROJ_Q:PROJ_Q + PROJ_KV].astype(BF16)
        for j in range(rows_per_part):
            ut_ref[pl.ds(part * rows_per_part + j, SSM_WIDTH, stride=ROW_GROUP), :] = (
                pt[PROJ_Q + PROJ_KV:, j * ROW_TOKENS:(j + 1) * ROW_TOKENS])


def _proj(x2d, ln_g, ln_b, w_kv, w_qvu_t):
    n = x2d.shape[0]
    tm = PROJ_TILE
    const = lambda i: (0, 0)
    row = lambda i: (i, 0)
    col = lambda i: (0, i)
    return pl.pallas_call(
        _proj_kernel,
        grid=(n // tm,),
        in_specs=[
            pl.BlockSpec((tm, D_MODEL), row),
            pl.BlockSpec((1, D_MODEL), const),
            pl.BlockSpec((1, D_MODEL), const),
            pl.BlockSpec((D_MODEL, KV), const),
            pl.BlockSpec((QVU, D_MODEL), const),
        ],
        out_specs=[
            pl.BlockSpec((PROJ_Q, tm), col),
            pl.BlockSpec((PROJ_KV, tm), col),
            pl.BlockSpec((None, SSM_WIDTH * ROW_GROUP, ROW_TOKENS), lambda i: (i, 0, 0)),
            pl.BlockSpec((tm, PROJ_KV), row),
            pl.BlockSpec((tm, PROJ_KV), row),
        ],
        out_shape=[
            jax.ShapeDtypeStruct((PROJ_Q, n), BF16),
            jax.ShapeDtypeStruct((PROJ_KV, n), BF16),
            jax.ShapeDtypeStruct((n // tm, SSM_WIDTH * ROW_GROUP, ROW_TOKENS), F32),
            jax.ShapeDtypeStruct((n, PROJ_KV), F32),
            jax.ShapeDtypeStruct((n, PROJ_KV), F32),
        ],
        compiler_params=pltpu.CompilerParams(
            dimension_semantics=("arbitrary",), vmem_limit_bytes=V7X_VMEM_LIMIT_BYTES),
        name="proj",
    )(x2d, ln_g, ln_b, w_kv, w_qvu_t)


PAIR = 2 * CHUNK
HEAD_LANES = Q_PER_KV * PAIR
ONES_ROWS = 16


def _scores(unit, qt_ref):
    kwin, _, lanes, _, _, h = unit
    base = h * Q_PER_KV * HEAD_DIM
    qrow = jnp.concatenate(
        [qt_ref[base + g * HEAD_DIM:base + (g + 1) * HEAD_DIM, lanes] for g in range(Q_PER_KV)],
        axis=1)
    zero = jnp.zeros_like(qrow)
    qstack = jnp.concatenate([qrow, zero] if h == 0 else [zero, qrow], axis=0)
    return jnp.dot(kwin, qstack, preferred_element_type=F32)


def _finish(unit, s, sink_ref, o_ref):
    _, vtwin, lanes, valid, masked_rows, h = unit
    nk = s.shape[0]
    base = h * Q_PER_KV * HEAD_DIM
    pieces, done = [], 0
    for start, stop in masked_rows:
        if start > done:
            pieces.append(s[done:start])
        pieces.append(jnp.where(valid[start:stop], s[start:stop], NEG_INF))
        done = stop
    if done < nk:
        pieces.append(s[done:])
    s = jnp.concatenate(pieces, axis=0)
    sink = sink_ref[h]
    m = jnp.maximum(jnp.max(s, axis=0, keepdims=True), sink)
    p = jnp.exp(s - m).astype(BF16)
    v_ones = jnp.concatenate(
        [vtwin[h * HEAD_DIM:(h + 1) * HEAD_DIM, :], jnp.ones((ONES_ROWS, nk), BF16)], axis=0)
    ov = jnp.dot(v_ones, p, preferred_element_type=F32)
    den = ov[HEAD_DIM:HEAD_DIM + 1, :] + jnp.exp(sink - m)
    o = ov[:HEAD_DIM, :] * (1.0 / den)
    for g in range(Q_PER_KV):
        o_ref[base + g * HEAD_DIM:base + (g + 1) * HEAD_DIM, lanes] = (
            o[:, g * PAIR:(g + 1) * PAIR].astype(BF16))


def _attend_units(units, qt_ref, sink_ref, o_ref):
    s_next = _scores(units[0], qt_ref)
    for i, unit in enumerate(units):
        s = s_next
        if i + 1 < len(units):
            s_next = _scores(units[i + 1], qt_ref)
        _finish(unit, s, sink_ref, o_ref)


def _attn_prompt_kernel(sink_ref, qt_ref, kp_ref, kc_ref, vtp_ref, vtc_ref, o_ref, *, n_pairs):
    tq = n_pairs * PAIR
    nk = WINDOW + PAIR
    kk = jnp.concatenate([kp_ref[...], kc_ref[...]], axis=0).astype(BF16)
    vt = jnp.concatenate([vtp_ref[...], vtc_ref[...]], axis=1)
    r = lax.broadcasted_iota(jnp.int32, (nk, HEAD_LANES), 0)
    first_chunk = (lax.broadcasted_iota(jnp.int32, (nk, HEAD_LANES), 1) & (PAIR - 1)) < CHUNK
    lo = jnp.where(first_chunk, 0, CHUNK)
    hi = jnp.where(first_chunk, BAND, nk)
    first_pos = pl.program_id(1) * tq - WINDOW
    units = []
    for pp in range(n_pairs):
        lo_pp = jnp.maximum(lo, -first_pos) if pp == 0 else lo
        valid = (r >= lo_pp) & (r < hi)
        masked_rows = ((0, nk),) if pp == 0 else ((0, CHUNK), (nk - CHUNK, nk))
        for h in range(N_KV_HEADS):
            units.append((kk[pp * PAIR:pp * PAIR + nk], vt[:, pp * PAIR:pp * PAIR + nk],
                          slice(pp * PAIR, (pp + 1) * PAIR), valid, masked_rows, h))
    _attend_units(units, qt_ref, sink_ref, o_ref)


def _attn_sample_kernel(sink_ref, qt_ref, ck_ref, kn_ref, cvt_ref, vtn_ref, o_ref, *, n_pairs):
    nk = 2 * WINDOW + PAIR
    r = lax.broadcasted_iota(jnp.int32, (nk, HEAD_LANES), 0)
    query_seq = (lax.broadcasted_iota(jnp.int32, (nk, HEAD_LANES), 1) & (PAIR - 1)) >> 6
    key_seq = jnp.where(r < 2 * WINDOW, r >> 7, (r - 2 * WINDOW) >> 6)
    valid = query_seq == key_seq
    units = []
    for pp in range(n_pairs):
        lanes = slice(pp * PAIR, (pp + 1) * PAIR)
        kwin = jnp.concatenate([ck_ref[2 * pp], ck_ref[2 * pp + 1], kn_ref[lanes, :]],
                               axis=0).astype(BF16)
        vtwin = jnp.concatenate([cvt_ref[:, 2 * pp * WINDOW:2 * (pp + 1) * WINDOW], vtn_ref[:, lanes]],
                                axis=1)
        units.extend((kwin, vtwin, lanes, valid, ((0, nk),), h) for h in range(N_KV_HEADS))
    _attend_units(units, qt_ref, sink_ref, o_ref)


_SINK_SPEC = pl.BlockSpec((N_KV_HEADS, 1, HEAD_LANES), lambda *_: (0, 0, 0))


def _attention_prompt(sink_rows, qt, k, vt, bsz, seq, n_pairs):
    tq = n_pairs * PAIR
    nt = seq // tq
    wpt = tq // WINDOW
    cur_c = lambda b, i: (0, b * nt + i)
    cur_r = lambda b, i: (b * nt + i, 0)
    prev = lambda b, i: jnp.maximum((b * nt + i) * wpt - 1, 0)
    return pl.pallas_call(
        functools.partial(_attn_prompt_kernel, n_pairs=n_pairs),
        grid=(bsz, nt),
        in_specs=[
            _SINK_SPEC,
            pl.BlockSpec((PROJ_Q, tq), cur_c),
            pl.BlockSpec((WINDOW, PROJ_KV), lambda b, i: (prev(b, i), 0)),
            pl.BlockSpec((tq, PROJ_KV), cur_r),
            pl.BlockSpec((PROJ_KV, WINDOW), lambda b, i: (0, prev(b, i))),
            pl.BlockSpec((PROJ_KV, tq), cur_c),
        ],
        out_specs=pl.BlockSpec((PROJ_Q, tq), cur_c),
        out_shape=jax.ShapeDtypeStruct((PROJ_Q, bsz * seq), BF16),
        compiler_params=pltpu.CompilerParams(dimension_semantics=("arbitrary", "arbitrary")),
        name="attn_prompt",
    )(sink_rows, qt, k, k, vt, vt)


def _attention_sample(sink_rows, qt, cache_k, k, cache_vt, vt, n_pairs):
    n = qt.shape[1]
    tq = n_pairs * PAIR
    return pl.pallas_call(
        functools.partial(_attn_sample_kernel, n_pairs=n_pairs),
        grid=(n // tq,),
        in_specs=[
            _SINK_SPEC,
            pl.BlockSpec((PROJ_Q, tq), lambda i: (0, i)),
            pl.BlockSpec((2 * n_pairs, WINDOW, PROJ_KV), lambda i: (i, 0, 0)),
            pl.BlockSpec((tq, PROJ_KV), lambda i: (i, 0)),
            pl.BlockSpec((PROJ_KV, 2 * n_pairs * WINDOW), lambda i: (0, i)),
            pl.BlockSpec((PROJ_KV, tq), lambda i: (0, i)),
        ],
        out_specs=pl.BlockSpec((PROJ_Q, tq), lambda i: (0, i)),
        out_shape=jax.ShapeDtypeStruct((PROJ_Q, n), BF16),
        compiler_params=pltpu.CompilerParams(dimension_semantics=("arbitrary",)),
        name="attn_sample",
    )(sink_rows, qt, cache_k, k, cache_vt, vt)


def _power_table(tau, nbits, a_re, a_im):
    rows = tau.shape[0]
    w_re = jnp.ones((rows, STATE_LANES), F32)
    w_im = jnp.zeros((rows, STATE_LANES), F32)
    p_re, p_im = a_re, a_im
    for k in range(nbits):
        bit = ((tau >> k) & 1) == 1
        f_re = jnp.where(bit, p_re, 1.0)
        f_im = jnp.where(bit, p_im, 0.0)
        w_re, w_im = w_re * f_re - w_im * f_im, w_re * f_im + w_im * f_re
        p_re, p_im = p_re * p_re - p_im * p_im, 2.0 * p_re * p_im
    return w_re, w_im


def _build_tables(lr_ref, li_ref, ls_ref, bre_ref, bim_ref, cre_ref, cim_ref,
                  p_ref, qt_ref, coef_ref):
    lo = lax.broadcasted_iota(jnp.int32, (1, STATE_LANES), 1) < SSM_STATE
    lr, li = lr_ref[...], li_ref[...]
    dt = jnp.exp(ls_ref[...])
    mag = jnp.exp(lr * dt)
    a_re, a_im = mag * jnp.cos(li * dt), mag * jnp.sin(li * dt)
    nr, ni = a_re - 1.0, a_im
    den = lr * lr + li * li
    f_re, f_im = (nr * lr + ni * li) / den, (ni * lr - nr * li) / den
    b_re, b_im = bre_ref[...], bim_ref[...]
    bb_re = f_re * b_re - f_im * b_im
    bb_im = f_re * b_im + f_im * b_re
    c_re, c_im = cre_ref[...], cim_ref[...]

    tau = lax.broadcasted_iota(jnp.int32, (SSM_T, 1), 0)
    w_re, w_im = _power_table(tau, 6, a_re, a_im)
    w1_re, w1_im = _power_table(tau + 1, 7, a_re, a_im)
    wr_re, wr_im = _power_table(SSM_T - 1 - tau, 6, a_re, a_im)

    def outer(c, w):
        return (c[:, None, :] * w[None, :, :]).reshape(SSM_CW, STATE_LANES)

    cw_mix = (outer(c_re, jnp.where(lo, w_re, w_im)) + outer(c_im, jnp.where(lo, -w_im, w_re)))
    bb_mix = jnp.where(lo, bb_re, -bb_im)
    strip = lax.dot_general(bb_mix, cw_mix, (((1,), (1,)), ((), ())),
                            precision=lax.Precision.HIGHEST,
                            preferred_element_type=F32)

    x_a, y_a = jnp.where(lo, bb_re, bb_im), jnp.where(lo, -bb_im, bb_re)
    x_b, y_b = jnp.where(lo, bb_im, bb_re), jnp.where(lo, bb_re, -bb_im)
    p_ref[:, :STATE_LANES] = (outer(x_a, wr_re) + outer(y_a, wr_im)).astype(BF16)
    p_ref[:, STATE_LANES:] = (outer(x_b, wr_re) + outer(y_b, wr_im)).astype(BF16)

    qt_ref[...] = (outer(c_re, jnp.where(lo, w1_re, -w1_im))
                   + outer(c_im, jnp.where(lo, -w1_im, -w1_re))).astype(BF16)

    t_re, t_im = a_re, a_im
    for _ in range(6):
        t_re, t_im = t_re * t_re - t_im * t_im, 2.0 * t_re * t_im
    for k in range(COEF_ROWS // 2):
        coef_ref[2 * k:2 * k + 1, :] = t_re
        coef_ref[2 * k + 1:2 * k + 2, :] = jnp.where(lo, -t_im, t_im)
        t_re, t_im = t_re * t_re - t_im * t_im, 2.0 * t_re * t_im
    return strip


def _toeplitz_rows(strip, m_ref, channels):
    s_idx = lax.broadcasted_iota(jnp.int32, (SSM_T, SSM_CW), 0)
    t_idx = lax.broadcasted_iota(jnp.int32, (SSM_T, SSM_CW), 1) & (SSM_T - 1)
    causal = t_idx >= s_idx
    for c in channels:
        rows = jnp.broadcast_to(strip[c:c + 1, :], (SSM_T, SSM_CW))
        shifted = pltpu.roll(rows, 0, 1, stride=1, stride_axis=0)
        m_ref[c * SSM_T:(c + 1) * SSM_T, :] = jnp.where(causal, shifted, 0.0).astype(BF16)


def _table_params(lam_re, lam_im, log_step, b_re, b_im, c_re, c_im):
    dup = lambda a: jnp.concatenate([a, a], axis=-1)
    lr = dup(lam_re)[:, None, :]
    li = dup(lam_im)[:, None, :]
    ls = jnp.broadcast_to(log_step[:, None, None], (SSM_GROUPS, 1, STATE_LANES))
    bt_re = dup(jnp.swapaxes(b_re, 1, 2))
    bt_im = dup(jnp.swapaxes(b_im, 1, 2))
    return lr, li, ls, bt_re, bt_im, dup(c_re), dup(c_im)


def _cmul(a1, a2, h, hs):
    return a1 * h + a2 * hs, a1 * hs - a2 * h


def _ssm_kernel(lr_ref, li_ref, ls_ref, bre_ref, bim_ref, cre_ref, cim_ref,
                up_ref, us_ref, h0e_ref, h0o_ref, h0es_ref, h0os_ref,
                yp_ref, ys_ref, hp_ref, hse_ref, hso_ref,
                m_ref, p_ref, qt_ref, coef_ref, *, rows_per_seq):
    strip = _build_tables(lr_ref, li_ref, ls_ref, bre_ref, bim_ref, cre_ref, cim_ref,
                          p_ref, qt_ref, coef_ref)
    lo = lax.broadcasted_iota(jnp.int32, (1, ROW_TOKENS), 1) < SSM_T

    def chunk_rows(ref):
        even, odd = [], []
        rows = ref.shape[0] * ROW_GROUP
        for k in range(SSM_GROUP // 2):
            a = ref[:, 2 * k].reshape(rows, ROW_TOKENS)
            b = ref[:, 2 * k + 1].reshape(rows, ROW_TOKENS)
            even.append(jnp.where(lo, a, pltpu.roll(b, SSM_T, 1)))
            odd.append(jnp.where(lo, pltpu.roll(a, SSM_T, 1), b))
        return jnp.concatenate(even, axis=1), jnp.concatenate(odd, axis=1)

    def store_rows(y_even, y_odd, ref):
        for k in range(SSM_GROUP // 2):
            te = y_even[:, k * ROW_TOKENS:(k + 1) * ROW_TOKENS]
            to = y_odd[:, k * ROW_TOKENS:(k + 1) * ROW_TOKENS]
            tiles = (ref.shape[0], ROW_GROUP, ROW_TOKENS)
            ref[:, 2 * k] = jnp.where(lo, te, pltpu.roll(to, SSM_T, 1)).reshape(tiles)
            ref[:, 2 * k + 1] = jnp.where(lo, pltpu.roll(te, SSM_T, 1), to).reshape(tiles)

    pe, po = chunk_rows(up_ref)
    se, so = chunk_rows(us_ref)
    rp, rs = pe.shape[0], se.shape[0]
    u = jnp.concatenate([pe, po, se, so], axis=0).astype(BF16)
    s12 = jnp.dot(u, p_ref[...], preferred_element_type=F32)
    s1, s2 = s12[:, :STATE_LANES], s12[:, STATE_LANES:]
    a1, a2 = coef_ref[0:1, :], coef_ref[1:2, :]

    e1, e2, o1, o2 = s1[:rp], s2[:rp], s1[rp:2 * rp], s2[rp:2 * rp]
    x1, x2 = _cmul(a1, a2, e1, e2)
    x1, x2 = x1 + o1, x2 + o2
    pos = lax.broadcasted_iota(jnp.int32, (rp, STATE_LANES), 0) & (rows_per_seq - 1)
    for k in range(rows_per_seq.bit_length() - 1):
        d = 1 << k
        b1, b2 = coef_ref[2 + 2 * k:3 + 2 * k, :], coef_ref[3 + 2 * k:4 + 2 * k, :]
        sh1 = jnp.where(pos >= d, pltpu.roll(x1, d, 0), 0.0)
        sh2 = jnp.where(pos >= d, pltpu.roll(x2, d, 0), 0.0)
        y1, y2 = _cmul(b1, b2, sh1, sh2)
        x1, x2 = x1 + y1, x2 + y2
    g1 = jnp.where(pos >= 1, pltpu.roll(x1, 1, 0), 0.0)
    g2 = jnp.where(pos >= 1, pltpu.roll(x2, 1, 0), 0.0)
    ho1 = _cmul(a1, a2, g1, g2)[0] + e1
    hp_ref[...] = jnp.zeros(hp_ref.shape, F32)
    for b in range(rp // rows_per_seq):
        last = (b + 1) * rows_per_seq - 1
        hp_ref[b:b + 1, :] = x1[last:last + 1, :]

    h0e, h0o = h0e_ref[...], h0o_ref[...]
    hse_ref[...] = a1 * h0e + a2 * h0es_ref[...] + s1[2 * rp:2 * rp + rs]
    hso_ref[...] = a1 * h0o + a2 * h0os_ref[...] + s1[2 * rp + rs:]

    hprev = jnp.concatenate([g1, ho1, h0e, h0o], axis=0).astype(BF16)
    y = lax.dot_general(hprev, qt_ref[...], (((1,), (1,)), ((), ())), preferred_element_type=F32)
    blocks = [range(c, c + M_BLOCK_CHANNELS) for c in range(0, SSM_GROUP, M_BLOCK_CHANNELS)]
    _toeplitz_rows(strip, m_ref, blocks[0])
    for i, channels in enumerate(blocks):
        rows = slice(channels[0] * SSM_T, (channels[-1] + 1) * SSM_T)
        y += jnp.dot(u[:, rows], m_ref[rows, :], preferred_element_type=F32)
        if i + 1 < len(blocks):
            _toeplitz_rows(strip, m_ref, blocks[i + 1])
    store_rows(y[:rp], y[rp:2 * rp], yp_ref)
    store_rows(y[2 * rp:2 * rp + rs], y[2 * rp + rs:], ys_ref)


def _ssm(table_params, u3p, u3s, h0e, h0o, h0es, h0os, *, rows_per_seq):
    gp, gs = u3p.shape[0], u3s.shape[0]
    rp, rs = gp * ROW_GROUP, gs * ROW_GROUP
    u3p = u3p.reshape(gp, SSM_WIDTH, ROW_GROUP, ROW_TOKENS)
    u3s = u3s.reshape(gs, SSM_WIDTH, ROW_GROUP, ROW_TOKENS)
    assert rows_per_seq & (rows_per_seq - 1) == 0 and 2 * rows_per_seq.bit_length() <= COEF_ROWS
    g3 = lambda g: (g, 0, 0)
    u_spec = lambda groups: pl.BlockSpec((groups, SSM_GROUP, ROW_GROUP, ROW_TOKENS),
                                         lambda g: (0, g, 0, 0))
    h_spec = pl.BlockSpec((None, rs, STATE_LANES), g3)
    row_spec = pl.BlockSpec((None, 1, STATE_LANES), g3)
    mat_spec = pl.BlockSpec((None, SSM_GROUP, STATE_LANES), g3)
    return pl.pallas_call(
        functools.partial(_ssm_kernel, rows_per_seq=rows_per_seq),
        grid=(SSM_GROUPS,),
        in_specs=[
            row_spec, row_spec, row_spec, mat_spec, mat_spec, mat_spec, mat_spec,
            u_spec(gp), u_spec(gs),
            h_spec, h_spec, h_spec, h_spec,
        ],
        out_specs=[
            u_spec(gp), u_spec(gs),
            pl.BlockSpec((None, STATE_OUT_ROWS, STATE_LANES), g3),
            h_spec, h_spec,
        ],
        out_shape=[
            jax.ShapeDtypeStruct((gp, SSM_WIDTH, ROW_GROUP, ROW_TOKENS), F32),
            jax.ShapeDtypeStruct((gs, SSM_WIDTH, ROW_GROUP, ROW_TOKENS), F32),
            jax.ShapeDtypeStruct((SSM_GROUPS, STATE_OUT_ROWS, STATE_LANES), F32),
            jax.ShapeDtypeStruct((SSM_GROUPS, rs, STATE_LANES), F32),
            jax.ShapeDtypeStruct((SSM_GROUPS, rs, STATE_LANES), F32),
        ],
        scratch_shapes=[
            pltpu.VMEM((SSM_CW, SSM_CW), BF16),
            pltpu.VMEM((SSM_CW, 2 * STATE_LANES), BF16),
            pltpu.VMEM((SSM_CW, STATE_LANES), BF16),
            pltpu.VMEM((COEF_ROWS, STATE_LANES), F32),
        ],
        compiler_params=pltpu.CompilerParams(dimension_semantics=("arbitrary",)),
        name="ssm",
    )(*table_params, u3p, u3s, h0e, h0o, h0es, h0os)


def _post_mix_stages(x_ref, at_ref, ut_ref, yt_ref, lng_ref, lnb_ref, d_ref, wglut_ref, bglu_ref,
                     wout_ref, ln1g_ref, ln1b_ref, store_h, *, alpha):
    vals = {}
    n_rows = x_ref.shape[0] // ROW_TOKENS
    first_row = (pl.program_id(0) % (ROW_GROUP // n_rows)) * n_rows

    def feature_major(ref):
        return jnp.concatenate(
            [ref[pl.ds(first_row + j, SSM_WIDTH, stride=ROW_GROUP), :] for j in range(n_rows)], axis=1)

    def norm_in():
        vals["xn"] = _layer_norm(x_ref[...], lng_ref[...], lnb_ref[...])

    def gelu():
        ys = feature_major(yt_ref) + d_ref[...] * feature_major(ut_ref)
        vals["gl"] = 0.5 * ys * (1.0 + lax.erf(ys * math.sqrt(0.5)))

    def glu_matmul():
        vals["z"] = jnp.dot(wglut_ref[...], vals["gl"].astype(BF16),
                            preferred_element_type=F32) + bglu_ref[...]

    def gate():
        vals["s"] = (vals.pop("gl") * jax.nn.sigmoid(vals.pop("z"))).astype(BF16)

    def out_proj():
        tn = (((0,), (0,)), ((), ()))
        mix = lax.dot_general(at_ref[...], wout_ref[:PROJ_Q, :], tn, preferred_element_type=F32)
        mix += lax.dot_general(vals.pop("s"), wout_ref[PROJ_Q:, :], tn, preferred_element_type=F32)
        vals["mix"] = mix

    def norm_1():
        store_h(_layer_norm(alpha * vals.pop("xn") + vals.pop("mix"), ln1g_ref[...], ln1b_ref[...]))

    return [norm_in, gelu, glu_matmul, gate, out_proj, norm_1]


def _post_kernel(x_ref, at_ref, ut_ref, yt_ref, lng_ref, lnb_ref, d_ref, wglut_ref, bglu_ref,
                 wout_ref, ln1g_ref, ln1b_ref, wgu_ref, wdown_ref, ln2g_ref, ln2b_ref, o_ref,
                 h_ref, hb_ref, *, alpha, n_tiles):
    step = pl.program_id(0)
    slot = step % 2
    n_blocks = D_FF // FF_BLOCK

    def store_h(h):
        h_ref[slot] = h
        hb_ref[slot] = h.astype(BF16)

    def mixer_stages():
        return _post_mix_stages(x_ref, at_ref, ut_ref, yt_ref, lng_ref, lnb_ref, d_ref, wglut_ref,
                                bglu_ref, wout_ref, ln1g_ref, ln1b_ref, store_h, alpha=alpha)

    def ffn_stage(with_mixer):
        prev = 1 - slot
        stages = mixer_stages() if with_mixer else []
        assert len(stages) * MIX_STAGE_EVERY <= n_blocks
        f = jnp.zeros(o_ref.shape, F32)
        for j in range(n_blocks):
            cols = slice(j * FF_BLOCK, (j + 1) * FF_BLOCK)
            up_cols = slice(D_FF + j * FF_BLOCK, D_FF + (j + 1) * FF_BLOCK)
            g = jnp.dot(hb_ref[prev], wgu_ref[:, cols], preferred_element_type=F32)
            up = jnp.dot(hb_ref[prev], wgu_ref[:, up_cols], preferred_element_type=F32)
            if j % MIX_STAGE_EVERY == 0 and j // MIX_STAGE_EVERY < len(stages):
                stages[j // MIX_STAGE_EVERY]()
            act = (g * jax.nn.sigmoid(g)) * up
            f += jnp.dot(act.astype(BF16), wdown_ref[cols, :], preferred_element_type=F32)
        o_ref[...] = _layer_norm(alpha * h_ref[prev] + f, ln2g_ref[...], ln2b_ref[...])

    @pl.when(step == 0)
    def _():
        for stage in mixer_stages():
            stage()

    @pl.when((step > 0) & (step < n_tiles))
    def _():
        ffn_stage(True)

    @pl.when(step == n_tiles)
    def _():
        ffn_stage(False)


def _post(x2d, at2d, ut3, yt3, ln_g, ln_b, d_col, w_glu_t, b_glu_col, w_out, ln1_g, ln1_b,
          w_gate_up, w_down, ln2_g, ln2_b, *, alpha, tm):
    n = x2d.shape[0]
    assert PROJ_TILE % tm == 0
    n_tiles = n // tm
    tile = lambda i: jnp.minimum(i, n_tiles - 1)
    group = lambda i: (tile(i) // (PROJ_TILE // tm), 0, 0)
    group_spec = pl.BlockSpec((None, SSM_WIDTH * ROW_GROUP, ROW_TOKENS), group)
    row = lambda i: (tile(i), 0)
    col = lambda i: (0, tile(i))
    const = lambda i: (0, 0)
    resident = lambda shape: pl.BlockSpec(shape, const, pipeline_mode=pl.Buffered(1))
    vec = lambda width: pl.BlockSpec((1, width), const)
    colvec = pl.BlockSpec((SSM_WIDTH, 1), const)
    return pl.pallas_call(
        functools.partial(_post_kernel, alpha=alpha, n_tiles=n_tiles),
        grid=(n_tiles + 1,),
        in_specs=[
            pl.BlockSpec((tm, D_MODEL), row),
            pl.BlockSpec((PROJ_Q, tm), col),
            group_spec, group_spec,
            vec(D_MODEL), vec(D_MODEL), colvec,
            resident((SSM_WIDTH, SSM_WIDTH)), colvec,
            resident((D_MODEL, D_MODEL)), vec(D_MODEL), vec(D_MODEL),
            resident((D_MODEL, 2 * D_FF)), resident((D_FF, D_MODEL)),
            vec(D_MODEL), vec(D_MODEL),
        ],
        out_specs=pl.BlockSpec((tm, D_MODEL), lambda i: (jnp.maximum(i - 1, 0), 0)),
        out_shape=jax.ShapeDtypeStruct((n, D_MODEL), F32),
        scratch_shapes=[pltpu.VMEM((2, tm, D_MODEL), F32), pltpu.VMEM((2, tm, D_MODEL), BF16)],
        compiler_params=pltpu.CompilerParams(
            dimension_semantics=("arbitrary",), vmem_limit_bytes=V7X_VMEM_LIMIT_BYTES),
        name="post",
    )(x2d, at2d, ut3, yt3, ln_g, ln_b, d_col, w_glu_t, b_glu_col, w_out, ln1_g, ln1_b,
      w_gate_up, w_down, ln2_g, ln2_b)


def kernel(x_prompt, x_sample, cache_win_k, cache_win_v, state_ssm_re, state_ssm_im,
           ln_in_g, ln_in_b, w_in, attn_sinks, ssm_lambda_re, ssm_lambda_im, ssm_log_step,
           ssm_b_re, ssm_b_im, ssm_c_re, ssm_c_im, ssm_d, w_glu, b_glu, w_out,
           ln1_g, ln1_b, w_gate_up, w_down, ln2_g, ln2_b):
    depth = w_in.shape[0]
    assert depth == 1, "single-layer step"
    bp, lp, _ = x_prompt.shape
    bs, ls, _ = x_sample.shape
    assert ls == SSM_T and bs % 2 == 0 and lp % 512 == 0 and bp <= STATE_OUT_ROWS
    win_rows = cache_win_k.shape[2]
    assert win_rows == WINDOW
    alpha = (2.0 * depth) ** 0.25
    l = 0
    row = lambda a: a.reshape(1, -1)
    column = lambda a: a.reshape(-1, 1)

    w_in_b = w_in[l].astype(BF16)
    w_kv = w_in_b[:, PROJ_Q:PROJ_Q + 2 * PROJ_KV]
    w_qvu_t = jnp.concatenate([w_in_b[:, :PROJ_Q], w_in_b[:, PROJ_Q + PROJ_KV:]], axis=1).T
    lng, lnb = row(ln_in_g), row(ln_in_b)
    xp2 = x_prompt.reshape(bp * lp, D_MODEL)
    xs2 = x_sample.reshape(bs * ls, D_MODEL)
    qtp, vtp, u3p, kp, vp = _proj(xp2, lng, lnb, w_kv, w_qvu_t)
    qts, vts, u3s, ks, vs = _proj(xs2, lng, lnb, w_kv, w_qvu_t)

    sink_rows = jnp.repeat(attn_sinks[l].reshape(N_KV_HEADS, Q_PER_KV), PAIR, axis=1)[:, None, :]
    atp = _attention_prompt(sink_rows, qtp, kp, vtp, bp, lp, n_pairs=4)
    ck = cache_win_k[l].reshape(bs, win_rows, PROJ_KV)
    cv = cache_win_v[l].reshape(bs, win_rows, PROJ_KV)
    cvt = jnp.transpose(cv, (2, 0, 1)).reshape(PROJ_KV, bs * win_rows).astype(BF16)
    ats = _attention_sample(sink_rows, qts, ck, ks, cvt, vts, n_pairs=4)
    kp3, vp3 = kp.reshape(bp, lp, PROJ_KV), vp.reshape(bp, lp, PROJ_KV)
    ks3, vs3 = ks.reshape(bs, ls, PROJ_KV), vs.reshape(bs, ls, PROJ_KV)

    table_params = _table_params(
        ssm_lambda_re[l], ssm_lambda_im[l], ssm_log_step[l],
        ssm_b_re[l], ssm_b_im[l], ssm_c_re[l], ssm_c_im[l])
    sre = jnp.swapaxes(state_ssm_re[l], 0, 1)
    sim = jnp.swapaxes(state_ssm_im[l], 0, 1)
    h0 = jnp.concatenate([sre, sim], axis=-1)
    h0s = jnp.concatenate([sim, sre], axis=-1)
    y4p, y4s, hfin_p, hfin_e, hfin_o = _ssm(
        table_params, u3p, u3s, h0[:, 0::2], h0[:, 1::2], h0s[:, 0::2], h0s[:, 1::2],
        rows_per_seq=lp // ROW_TOKENS)
    hfin_s = jnp.stack([hfin_e, hfin_o], axis=2).reshape(SSM_GROUPS, bs, STATE_LANES)

    post_args = (lng, lnb, column(ssm_d[l]), w_glu[l].T.astype(BF16), column(b_glu[l]),
                 w_out[l].astype(BF16), row(ln1_g[l]), row(ln1_b[l]),
                 w_gate_up[l].astype(BF16), w_down[l].astype(BF16), row(ln2_g[l]), row(ln2_b[l]))
    out_p = _post(xp2, atp, u3p, y4p.reshape(u3p.shape), *post_args, alpha=alpha, tm=512)
    out_s = _post(xs2, ats, u3s, y4s.reshape(u3s.shape), *post_args, alpha=alpha, tm=512)

    kv_shape = (N_KV_HEADS, HEAD_DIM)
    win_k_p = kp3[:, -win_rows:].reshape(1, bp, win_rows, *kv_shape)
    win_v_p = vp3[:, -win_rows:].reshape(1, bp, win_rows, *kv_shape)
    win_k_s = jnp.concatenate([ck, ks3], axis=1)[:, -win_rows:].reshape(1, bs, win_rows, *kv_shape)
    win_v_s = jnp.concatenate([cv, vs3], axis=1)[:, -win_rows:].reshape(1, bs, win_rows, *kv_shape)
    state = lambda hf, n: jnp.swapaxes(hf[:, :n], 0, 1)
    sp, ss = state(hfin_p, bp), state(hfin_s, bs)
    return (out_p.reshape(bp, lp, D_MODEL), out_s.reshape(bs, ls, D_MODEL),
            win_k_p, win_v_p, sp[None, ..., :SSM_STATE], sp[None, ..., SSM_STATE:],
            win_k_s, win_v_s, ss[None, ..., :SSM_STATE], ss[None, ..., SSM_STATE:])
```

```python
import functools
import math

import jax
import jax.numpy as jnp
from jax import lax
from jax.experimental import pallas as pl
from jax.experimental.pallas import tpu as pltpu

F32 = jnp.float32
BF16 = jnp.bfloat16

D_MODEL = 1024
HEAD_DIM = 64
N_HEADS = 8
N_KV_HEADS = 2
Q_PER_KV = N_HEADS // N_KV_HEADS
CHUNK = 64
WINDOW = 128
WIN_CHUNKS = WINDOW // CHUNK
BAND = (WIN_CHUNKS + 1) * CHUNK
PROJ_Q = N_HEADS * HEAD_DIM
PROJ_KV = N_KV_HEADS * HEAD_DIM
SSM_WIDTH = 512
SSM_GROUP = 16
SSM_GROUPS = SSM_WIDTH // SSM_GROUP
SSM_STATE = 64
STATE_LANES = 2 * SSM_STATE
D_FF = 2816
D_IN_PROJ = PROJ_Q + 2 * PROJ_KV + SSM_WIDTH
LN_EPS = 1e-5
NEG_INF = -1e30

SSM_T = CHUNK
SSM_CW = SSM_T * SSM_GROUP
ROW_TOKENS = 2 * SSM_T
ROW_GROUP = 8
PROJ_TILE = ROW_GROUP * ROW_TOKENS
PROJ_PARTS = 2
COEF_ROWS = 16
STATE_OUT_ROWS = 8
M_BLOCK_CHANNELS = 4
FF_BLOCK = 256
V7X_VMEM_LIMIT_BYTES = 56 * 1024 * 1024


def _layer_norm(x, g, b):
    mu = jnp.mean(x, axis=-1, keepdims=True)
    xc = x - mu
    var = jnp.mean(xc * xc, axis=-1, keepdims=True)
    return xc * lax.rsqrt(var + LN_EPS) * g + b


KV = 2 * PROJ_KV
QVU = PROJ_Q + PROJ_KV + SSM_WIDTH


def _proj_kernel(xp_ref, xs_ref, g_ref, b_ref, w_ref, wt_ref, qt_ref, vt_ref, ut_ref, k_ref, v_ref,
                 *, prompt_tiles):
    outs = (g_ref, b_ref, w_ref, wt_ref, qt_ref, vt_ref, ut_ref, k_ref, v_ref)

    @pl.when(pl.program_id(0) < prompt_tiles)
    def _():
        _proj_tile(xp_ref, *outs)

    @pl.when(pl.program_id(0) >= prompt_tiles)
    def _():
        _proj_tile(xs_ref, *outs)


def _proj_tile(x_ref, g_ref, b_ref, w_ref, wt_ref, qt_ref, vt_ref, ut_ref, k_ref, v_ref):
    rows_per_part = ROW_GROUP // PROJ_PARTS
    for part in range(PROJ_PARTS):
        tok = slice(part * rows_per_part * ROW_TOKENS, (part + 1) * rows_per_part * ROW_TOKENS)
        xb = _layer_norm(x_ref[tok, :], g_ref[...], b_ref[...]).astype(BF16)
        p = jnp.dot(xb, w_ref[...], preferred_element_type=F32)
        k_ref[tok, :] = p[:, :PROJ_KV]
        v_ref[tok, :] = p[:, PROJ_KV:]
        pt = lax.dot_general(wt_ref[...], xb, (((1,), (1,)), ((), ())), preferred_element_type=F32)
        qt_ref[:, tok] = (pt[:PROJ_Q] * (HEAD_DIM ** -0.5)).astype(BF16)
        vt_ref[:, tok] = pt[PROJ_Q:PROJ_Q + PROJ_KV].astype(BF16)
        for j in range(rows_per_part):
            ut_ref[pl.ds(part * rows_per_part + j, SSM_WIDTH, stride=ROW_GROUP), :] = (
                pt[PROJ_Q + PROJ_KV:, j * ROW_TOKENS:(j + 1) * ROW_TOKENS])


def _proj(xp2d, xs2d, ln_g, ln_b, w_kv, w_qvu_t):
    tm = PROJ_TILE
    tp, ts = xp2d.shape[0] // tm, xs2d.shape[0] // tm
    n = (tp + ts) * tm
    const = lambda i: (0, 0)
    row = lambda i: (i, 0)
    col = lambda i: (0, i)
    return pl.pallas_call(
        functools.partial(_proj_kernel, prompt_tiles=tp),
        grid=(tp + ts,),
        in_specs=[
            pl.BlockSpec((tm, D_MODEL), lambda i: (jnp.minimum(i, tp - 1), 0)),
            pl.BlockSpec((tm, D_MODEL), lambda i: (jnp.maximum(i - tp, 0), 0)),
            pl.BlockSpec((1, D_MODEL), const),
            pl.BlockSpec((1, D_MODEL), const),
            pl.BlockSpec((D_MODEL, KV), const),
            pl.BlockSpec((QVU, D_MODEL), const),
        ],
        out_specs=[
            pl.BlockSpec((PROJ_Q, tm), col),
            pl.BlockSpec((PROJ_KV, tm), col),
            pl.BlockSpec((None, SSM_WIDTH * ROW_GROUP, ROW_TOKENS), lambda i: (i, 0, 0)),
            pl.BlockSpec((tm, PROJ_KV), row),
            pl.BlockSpec((tm, PROJ_KV), row),
        ],
        out_shape=[
            jax.ShapeDtypeStruct((PROJ_Q, n), BF16),
            jax.ShapeDtypeStruct((PROJ_KV, n), BF16),
            jax.ShapeDtypeStruct((n // tm, SSM_WIDTH * ROW_GROUP, ROW_TOKENS), F32),
            jax.ShapeDtypeStruct((n, PROJ_KV), F32),
            jax.ShapeDtypeStruct((n, PROJ_KV), F32),
        ],
        compiler_params=pltpu.CompilerParams(
            dimension_semantics=("arbitrary",), vmem_limit_bytes=V7X_VMEM_LIMIT_BYTES),
        name="proj",
    )(xp2d, xs2d, ln_g, ln_b, w_kv, w_qvu_t)


PAIR = 2 * CHUNK
HEAD_LANES = Q_PER_KV * PAIR
ONES_ROWS = 16


def _scores(unit, qt_ref):
    kwin, _, lanes, _, _, h = unit
    base = h * Q_PER_KV * HEAD_DIM
    qrow = jnp.concatenate(
        [qt_ref[base + g * HEAD_DIM:base + (g + 1) * HEAD_DIM, lanes] for g in range(Q_PER_KV)],
        axis=1)
    zero = jnp.zeros_like(qrow)
    qstack = jnp.concatenate([qrow, zero] if h == 0 else [zero, qrow], axis=0)
    return jnp.dot(kwin, qstack, preferred_element_type=F32)


def _finish(unit, s, sink_ref, o_ref):
    _, vtwin, lanes, valid, masked_rows, h = unit
    nk = s.shape[0]
    base = h * Q_PER_KV * HEAD_DIM
    pieces, done = [], 0
    for start, stop in masked_rows:
        if start > done:
            pieces.append(s[done:start])
        pieces.append(jnp.where(valid[start:stop], s[start:stop], NEG_INF))
        done = stop
    if done < nk:
        pieces.append(s[done:])
    s = jnp.concatenate(pieces, axis=0)
    sink = sink_ref[h]
    m = jnp.maximum(jnp.max(s, axis=0, keepdims=True), sink)
    p = jnp.exp(s - m).astype(BF16)
    v_ones = jnp.concatenate(
        [vtwin[h * HEAD_DIM:(h + 1) * HEAD_DIM, :], jnp.ones((ONES_ROWS, nk), BF16)], axis=0)
    ov = jnp.dot(v_ones, p, preferred_element_type=F32)
    den = ov[HEAD_DIM:HEAD_DIM + 1, :] + jnp.exp(sink - m)
    o = ov[:HEAD_DIM, :] * (1.0 / den)
    for g in range(Q_PER_KV):
        o_ref[base + g * HEAD_DIM:base + (g + 1) * HEAD_DIM, lanes] = (
            o[:, g * PAIR:(g + 1) * PAIR].astype(BF16))


def _attend_units(units, qt_ref, sink_ref, o_ref):
    s_next = _scores(units[0], qt_ref)
    for i, unit in enumerate(units):
        s = s_next
        if i + 1 < len(units):
            s_next = _scores(units[i + 1], qt_ref)
        _finish(unit, s, sink_ref, o_ref)


def _attn_prompt_kernel(sink_ref, qt_ref, kp_ref, kc_ref, vtp_ref, vtc_ref, o_ref, *, n_pairs):
    tq = n_pairs * PAIR
    nk = WINDOW + PAIR
    kk = jnp.concatenate([kp_ref[...], kc_ref[...]], axis=0).astype(BF16)
    vt = jnp.concatenate([vtp_ref[...], vtc_ref[...]], axis=1)
    r = lax.broadcasted_iota(jnp.int32, (nk, HEAD_LANES), 0)
    first_chunk = (lax.broadcasted_iota(jnp.int32, (nk, HEAD_LANES), 1) & (PAIR - 1)) < CHUNK
    lo = jnp.where(first_chunk, 0, CHUNK)
    hi = jnp.where(first_chunk, BAND, nk)
    first_pos = pl.program_id(1) * tq - WINDOW
    units = []
    for pp in range(n_pairs):
        lo_pp = jnp.maximum(lo, -first_pos) if pp == 0 else lo
        valid = (r >= lo_pp) & (r < hi)
        masked_rows = ((0, nk),) if pp == 0 else ((0, CHUNK), (nk - CHUNK, nk))
        for h in range(N_KV_HEADS):
            units.append((kk[pp * PAIR:pp * PAIR + nk], vt[:, pp * PAIR:pp * PAIR + nk],
                          slice(pp * PAIR, (pp + 1) * PAIR), valid, masked_rows, h))
    _attend_units(units, qt_ref, sink_ref, o_ref)


def _attn_sample_kernel(sink_ref, qt_ref, ck_ref, kn_ref, cvt_ref, vtn_ref, o_ref, *, n_pairs):
    nk = 2 * WINDOW + PAIR
    r = lax.broadcasted_iota(jnp.int32, (nk, HEAD_LANES), 0)
    query_seq = (lax.broadcasted_iota(jnp.int32, (nk, HEAD_LANES), 1) & (PAIR - 1)) >> 6
    key_seq = jnp.where(r < 2 * WINDOW, r >> 7, (r - 2 * WINDOW) >> 6)
    valid = query_seq == key_seq
    units = []
    for pp in range(n_pairs):
        lanes = slice(pp * PAIR, (pp + 1) * PAIR)
        kwin = jnp.concatenate([ck_ref[2 * pp], ck_ref[2 * pp + 1], kn_ref[lanes, :]],
                               axis=0).astype(BF16)
        vtwin = jnp.concatenate([cvt_ref[:, 2 * pp * WINDOW:2 * (pp + 1) * WINDOW], vtn_ref[:, lanes]],
                                axis=1)
        units.extend((kwin, vtwin, lanes, valid, ((0, nk),), h) for h in range(N_KV_HEADS))
    _attend_units(units, qt_ref, sink_ref, o_ref)


_SINK_SPEC = pl.BlockSpec((N_KV_HEADS, 1, HEAD_LANES), lambda *_: (0, 0, 0))


def _attention_prompt(sink_rows, qt, k, vt, bsz, seq, n_pairs):
    tq = n_pairs * PAIR
    nt = seq // tq
    wpt = tq // WINDOW
    cur_c = lambda b, i: (0, b * nt + i)
    cur_r = lambda b, i: (b * nt + i, 0)
    prev = lambda b, i: jnp.maximum((b * nt + i) * wpt - 1, 0)
    return pl.pallas_call(
        functools.partial(_attn_prompt_kernel, n_pairs=n_pairs),
        grid=(bsz, nt),
        in_specs=[
            _SINK_SPEC,
            pl.BlockSpec((PROJ_Q, tq), cur_c),
            pl.BlockSpec((WINDOW, PROJ_KV), lambda b, i: (prev(b, i), 0)),
            pl.BlockSpec((tq, PROJ_KV), cur_r),
            pl.BlockSpec((PROJ_KV, WINDOW), lambda b, i: (0, prev(b, i))),
            pl.BlockSpec((PROJ_KV, tq), cur_c),
        ],
        out_specs=pl.BlockSpec((PROJ_Q, tq), cur_c),
        out_shape=jax.ShapeDtypeStruct((PROJ_Q, bsz * seq), BF16),
        compiler_params=pltpu.CompilerParams(dimension_semantics=("arbitrary", "arbitrary")),
        name="attn_prompt",
    )(sink_rows, qt, k, k, vt, vt)


def _attention_sample(sink_rows, qt, cache_k, k, cache_vt, vt, first_token, n_pairs):
    n = cache_k.shape[0] * CHUNK
    tq = n_pairs * PAIR
    assert first_token % tq == 0
    off = first_token // tq
    return pl.pallas_call(
        functools.partial(_attn_sample_kernel, n_pairs=n_pairs),
        grid=(n // tq,),
        in_specs=[
            _SINK_SPEC,
            pl.BlockSpec((PROJ_Q, tq), lambda i: (0, i + off)),
            pl.BlockSpec((2 * n_pairs, WINDOW, PROJ_KV), lambda i: (i, 0, 0)),
            pl.BlockSpec((tq, PROJ_KV), lambda i: (i + off, 0)),
            pl.BlockSpec((PROJ_KV, 2 * n_pairs * WINDOW), lambda i: (0, i)),
            pl.BlockSpec((PROJ_KV, tq), lambda i: (0, i + off)),
        ],
        out_specs=pl.BlockSpec((PROJ_Q, tq), lambda i: (0, i)),
        out_shape=jax.ShapeDtypeStruct((PROJ_Q, n), BF16),
        compiler_params=pltpu.CompilerParams(dimension_semantics=("arbitrary",)),
        name="attn_sample",
    )(sink_rows, qt, cache_k, k, cache_vt, vt)


def _power_table(tau, nbits, a_re, a_im):
    rows = tau.shape[0]
    w_re = jnp.ones((rows, STATE_LANES), F32)
    w_im = jnp.zeros((rows, STATE_LANES), F32)
    p_re, p_im = a_re, a_im
    for k in range(nbits):
        bit = ((tau >> k) & 1) == 1
        f_re = jnp.where(bit, p_re, 1.0)
        f_im = jnp.where(bit, p_im, 0.0)
        w_re, w_im = w_re * f_re - w_im * f_im, w_re * f_im + w_im * f_re
        p_re, p_im = p_re * p_re - p_im * p_im, 2.0 * p_re * p_im
    return w_re, w_im


def _build_tables(lr_ref, li_ref, ls_ref, bre_ref, bim_ref, cre_ref, cim_ref,
                  p_ref, qt_ref, coef_ref):
    lo = lax.broadcasted_iota(jnp.int32, (1, STATE_LANES), 1) < SSM_STATE
    lr, li = lr_ref[...], li_ref[...]
    dt = jnp.exp(ls_ref[...])
    mag = jnp.exp(lr * dt)
    a_re, a_im = mag * jnp.cos(li * dt), mag * jnp.sin(li * dt)
    nr, ni = a_re - 1.0, a_im
    den = lr * lr + li * li
    f_re, f_im = (nr * lr + ni * li) / den, (ni * lr - nr * li) / den
    b_re, b_im = bre_ref[...], bim_ref[...]
    bb_re = f_re * b_re - f_im * b_im
    bb_im = f_re * b_im + f_im * b_re
    c_re, c_im = cre_ref[...], cim_ref[...]

    tau = lax.broadcasted_iota(jnp.int32, (SSM_T, 1), 0)
    w_re, w_im = _power_table(tau, 6, a_re, a_im)
    w1_re, w1_im = _power_table(tau + 1, 7, a_re, a_im)
    wr_re, wr_im = _power_table(SSM_T - 1 - tau, 6, a_re, a_im)

    def outer(c, w):
        return (c[:, None, :] * w[None, :, :]).reshape(SSM_CW, STATE_LANES)

    cw_mix = (outer(c_re, jnp.where(lo, w_re, w_im)) + outer(c_im, jnp.where(lo, -w_im, w_re)))
    bb_mix = jnp.where(lo, bb_re, -bb_im)
    strip = lax.dot_general(bb_mix, cw_mix, (((1,), (1,)), ((), ())),
                            precision=lax.Precision.HIGHEST,
                            preferred_element_type=F32)

    x_a, y_a = jnp.where(lo, bb_re, bb_im), jnp.where(lo, -bb_im, bb_re)
    x_b, y_b = jnp.where(lo, bb_im, bb_re), jnp.where(lo, bb_re, -bb_im)
    p_ref[:, :STATE_LANES] = (outer(x_a, wr_re) + outer(y_a, wr_im)).astype(BF16)
    p_ref[:, STATE_LANES:] = (outer(x_b, wr_re) + outer(y_b, wr_im)).astype(BF16)

    qt_ref[...] = (outer(c_re, jnp.where(lo, w1_re, -w1_im))
                   + outer(c_im, jnp.where(lo, -w1_im, -w1_re))).astype(BF16)

    t_re, t_im = a_re, a_im
    for _ in range(6):
        t_re, t_im = t_re * t_re - t_im * t_im, 2.0 * t_re * t_im
    for k in range(COEF_ROWS // 2):
        coef_ref[2 * k:2 * k + 1, :] = t_re
        coef_ref[2 * k + 1:2 * k + 2, :] = jnp.where(lo, -t_im, t_im)
        t_re, t_im = t_re * t_re - t_im * t_im, 2.0 * t_re * t_im
    return strip


def _toeplitz_rows(strip, m_ref, channels):
    s_idx = lax.broadcasted_iota(jnp.int32, (SSM_T, SSM_CW), 0)
    t_idx = lax.broadcasted_iota(jnp.int32, (SSM_T, SSM_CW), 1) & (SSM_T - 1)
    causal = t_idx >= s_idx
    for c in channels:
        rows = jnp.broadcast_to(strip[c:c + 1, :], (SSM_T, SSM_CW))
        shifted = pltpu.roll(rows, 0, 1, stride=1, stride_axis=0)
        m_ref[c * SSM_T:(c + 1) * SSM_T, :] = jnp.where(causal, shifted, 0.0).astype(BF16)


def _table_params(lam_re, lam_im, log_step, b_re, b_im, c_re, c_im):
    dup = lambda a: jnp.concatenate([a, a], axis=-1)
    lr = dup(lam_re)[:, None, :]
    li = dup(lam_im)[:, None, :]
    ls = jnp.broadcast_to(log_step[:, None, None], (SSM_GROUPS, 1, STATE_LANES))
    bt_re = dup(jnp.swapaxes(b_re, 1, 2))
    bt_im = dup(jnp.swapaxes(b_im, 1, 2))
    return lr, li, ls, bt_re, bt_im, dup(c_re), dup(c_im)


def _cmul(a1, a2, h, hs):
    return a1 * h + a2 * hs, a1 * hs - a2 * h


def _ssm_kernel(lr_ref, li_ref, ls_ref, bre_ref, bim_ref, cre_ref, cim_ref,
                u_ref, h0e_ref, h0o_ref, h0es_ref, h0os_ref,
                y_ref, hp_ref, hse_ref, hso_ref,
                m_ref, p_ref, qt_ref, coef_ref, *, rows_per_seq, prompt_rows):
    strip = _build_tables(lr_ref, li_ref, ls_ref, bre_ref, bim_ref, cre_ref, cim_ref,
                          p_ref, qt_ref, coef_ref)
    lo = lax.broadcasted_iota(jnp.int32, (1, ROW_TOKENS), 1) < SSM_T

    def chunk_rows(ref):
        even, odd = [], []
        rows = ref.shape[0] * ROW_GROUP
        for k in range(SSM_GROUP // 2):
            a = ref[:, 2 * k].reshape(rows, ROW_TOKENS)
            b = ref[:, 2 * k + 1].reshape(rows, ROW_TOKENS)
            even.append(jnp.where(lo, a, pltpu.roll(b, SSM_T, 1)))
            odd.append(jnp.where(lo, pltpu.roll(a, SSM_T, 1), b))
        return jnp.concatenate(even, axis=1), jnp.concatenate(odd, axis=1)

    def store_rows(y_even, y_odd, ref):
        for k in range(SSM_GROUP // 2):
            te = y_even[:, k * ROW_TOKENS:(k + 1) * ROW_TOKENS]
            to = y_odd[:, k * ROW_TOKENS:(k + 1) * ROW_TOKENS]
            tiles = (ref.shape[0], ROW_GROUP, ROW_TOKENS)
            ref[:, 2 * k] = jnp.where(lo, te, pltpu.roll(to, SSM_T, 1)).reshape(tiles)
            ref[:, 2 * k + 1] = jnp.where(lo, pltpu.roll(te, SSM_T, 1), to).reshape(tiles)

    even, odd = chunk_rows(u_ref)
    rp, rs = prompt_rows, even.shape[0] - prompt_rows
    u = jnp.concatenate([even[:rp], odd[:rp], even[rp:], odd[rp:]], axis=0).astype(BF16)
    s12 = jnp.dot(u, p_ref[...], preferred_element_type=F32)
    s1, s2 = s12[:, :STATE_LANES], s12[:, STATE_LANES:]
    a1, a2 = coef_ref[0:1, :], coef_ref[1:2, :]

    e1, e2, o1, o2 = s1[:rp], s2[:rp], s1[rp:2 * rp], s2[rp:2 * rp]
    x1, x2 = _cmul(a1, a2, e1, e2)
    x1, x2 = x1 + o1, x2 + o2
    pos = lax.broadcasted_iota(jnp.int32, (rp, STATE_LANES), 0) & (rows_per_seq - 1)
    for k in range(rows_per_seq.bit_length() - 1):
        d = 1 << k
        b1, b2 = coef_ref[2 + 2 * k:3 + 2 * k, :], coef_ref[3 + 2 * k:4 + 2 * k, :]
        sh1 = jnp.where(pos >= d, pltpu.roll(x1, d, 0), 0.0)
        sh2 = jnp.where(pos >= d, pltpu.roll(x2, d, 0), 0.0)
        y1, y2 = _cmul(b1, b2, sh1, sh2)
        x1, x2 = x1 + y1, x2 + y2
    g1 = jnp.where(pos >= 1, pltpu.roll(x1, 1, 0), 0.0)
    g2 = jnp.where(pos >= 1, pltpu.roll(x2, 1, 0), 0.0)
    ho1 = _cmul(a1, a2, g1, g2)[0] + e1
    hp_ref[...] = jnp.zeros(hp_ref.shape, F32)
    for b in range(rp // rows_per_seq):
        last = (b + 1) * rows_per_seq - 1
        hp_ref[b:b + 1, :] = x1[last:last + 1, :]

    h0e, h0o = h0e_ref[...], h0o_ref[...]
    hse_ref[...] = a1 * h0e + a2 * h0es_ref[...] + s1[2 * rp:2 * rp + rs]
    hso_ref[...] = a1 * h0o + a2 * h0os_ref[...] + s1[2 * rp + rs:]

    hprev = jnp.concatenate([g1, ho1, h0e, h0o], axis=0).astype(BF16)
    y = lax.dot_general(hprev, qt_ref[...], (((1,), (1,)), ((), ())), preferred_element_type=F32)
    blocks = [range(c, c + M_BLOCK_CHANNELS) for c in range(0, SSM_GROUP, M_BLOCK_CHANNELS)]
    _toeplitz_rows(strip, m_ref, blocks[0])
    for i, channels in enumerate(blocks):
        rows = slice(channels[0] * SSM_T, (channels[-1] + 1) * SSM_T)
        y += jnp.dot(u[:, rows], m_ref[rows, :], preferred_element_type=F32)
        if i + 1 < len(blocks):
            _toeplitz_rows(strip, m_ref, blocks[i + 1])
    store_rows(jnp.concatenate([y[:rp], y[2 * rp:2 * rp + rs]], axis=0),
               jnp.concatenate([y[rp:2 * rp], y[2 * rp + rs:]], axis=0), y_ref)


def _ssm(table_params, u3, h0e, h0o, h0es, h0os, *, rows_per_seq, prompt_rows):
    groups = u3.shape[0]
    rs = groups * ROW_GROUP - prompt_rows
    u4 = u3.reshape(groups, SSM_WIDTH, ROW_GROUP, ROW_TOKENS)
    assert rows_per_seq & (rows_per_seq - 1) == 0 and 2 * rows_per_seq.bit_length() <= COEF_ROWS
    g3 = lambda g: (g, 0, 0)
    u_spec = pl.BlockSpec((groups, SSM_GROUP, ROW_GROUP, ROW_TOKENS), lambda g: (0, g, 0, 0))
    h_spec = pl.BlockSpec((None, rs, STATE_LANES), g3)
    row_spec = pl.BlockSpec((None, 1, STATE_LANES), g3)
    mat_spec = pl.BlockSpec((None, SSM_GROUP, STATE_LANES), g3)
    return pl.pallas_call(
        functools.partial(_ssm_kernel, rows_per_seq=rows_per_seq, prompt_rows=prompt_rows),
        grid=(SSM_GROUPS,),
        in_specs=[
            row_spec, row_spec, row_spec, mat_spec, mat_spec, mat_spec, mat_spec,
            u_spec,
            h_spec, h_spec, h_spec, h_spec,
        ],
        out_specs=[
            u_spec,
            pl.BlockSpec((None, STATE_OUT_ROWS, STATE_LANES), g3),
            h_spec, h_spec,
        ],
        out_shape=[
            jax.ShapeDtypeStruct(u4.shape, F32),
            jax.ShapeDtypeStruct((SSM_GROUPS, STATE_OUT_ROWS, STATE_LANES), F32),
            jax.ShapeDtypeStruct((SSM_GROUPS, rs, STATE_LANES), F32),
            jax.ShapeDtypeStruct((SSM_GROUPS, rs, STATE_LANES), F32),
        ],
        scratch_shapes=[
            pltpu.VMEM((SSM_CW, SSM_CW), BF16),
            pltpu.VMEM((SSM_CW, 2 * STATE_LANES), BF16),
            pltpu.VMEM((SSM_CW, STATE_LANES), BF16),
            pltpu.VMEM((COEF_ROWS, STATE_LANES), F32),
        ],
        compiler_params=pltpu.CompilerParams(dimension_semantics=("arbitrary",)),
        name="ssm",
    )(*table_params, u4, h0e, h0o, h0es, h0os)


def _post_kernel(xp_ref, xs_ref, atp_ref, ats_ref, ut_ref, yt_ref, *refs, alpha, prompt_tiles):
    weights, (op_ref, os_ref) = refs[:-2], refs[-2:]

    @pl.when(pl.program_id(0) < prompt_tiles)
    def _():
        _post_tile(xp_ref, atp_ref, ut_ref, yt_ref, *weights, op_ref, alpha=alpha)

    @pl.when(pl.program_id(0) >= prompt_tiles)
    def _():
        _post_tile(xs_ref, ats_ref, ut_ref, yt_ref, *weights, os_ref, alpha=alpha)


def _post_tile(x_ref, at_ref, ut_ref, yt_ref, lng_ref, lnb_ref, d_ref, wglut_ref, bglu_ref,
               wout_ref, ln1g_ref, ln1b_ref, wgu_ref, wdown_ref, ln2g_ref, ln2b_ref, o_ref,
               *, alpha):
    xn = _layer_norm(x_ref[...], lng_ref[...], lnb_ref[...])
    n_rows = x_ref.shape[0] // ROW_TOKENS
    first_row = (pl.program_id(0) % (ROW_GROUP // n_rows)) * n_rows

    def feature_major(ref):
        return jnp.concatenate(
            [ref[pl.ds(first_row + j, SSM_WIDTH, stride=ROW_GROUP), :] for j in range(n_rows)], axis=1)

    ys = feature_major(yt_ref) + d_ref[...] * feature_major(ut_ref)
    gl = 0.5 * ys * (1.0 + lax.erf(ys * math.sqrt(0.5)))
    z = jnp.dot(wglut_ref[...], gl.astype(BF16), preferred_element_type=F32) + bglu_ref[...]
    s = gl * jax.nn.sigmoid(z)
    tn = (((0,), (0,)), ((), ()))
    mix = lax.dot_general(at_ref[...], wout_ref[:PROJ_Q, :], tn, preferred_element_type=F32)
    mix += lax.dot_general(s.astype(BF16), wout_ref[PROJ_Q:, :], tn, preferred_element_type=F32)
    h = _layer_norm(alpha * xn + mix, ln1g_ref[...], ln1b_ref[...])
    hb = h.astype(BF16)
    f = jnp.zeros_like(h)
    for j in range(D_FF // FF_BLOCK):
        cols = slice(j * FF_BLOCK, (j + 1) * FF_BLOCK)
        up_cols = slice(D_FF + j * FF_BLOCK, D_FF + (j + 1) * FF_BLOCK)
        g = jnp.dot(hb, wgu_ref[:, cols], preferred_element_type=F32)
        up = jnp.dot(hb, wgu_ref[:, up_cols], preferred_element_type=F32)
        act = (g * jax.nn.sigmoid(g)) * up
        f += jnp.dot(act.astype(BF16), wdown_ref[cols, :], preferred_element_type=F32)
    o_ref[...] = _layer_norm(alpha * h + f, ln2g_ref[...], ln2b_ref[...])


def _post(xp2d, xs2d, atp, ats, ut3, yt3, ln_g, ln_b, d_col, w_glu_t, b_glu_col, w_out, ln1_g, ln1_b,
          w_gate_up, w_down, ln2_g, ln2_b, *, alpha, tm):
    n_p, n_s = xp2d.shape[0], xs2d.shape[0]
    tp, ts = n_p // tm, n_s // tm
    assert PROJ_TILE % tm == 0 and n_p % PROJ_TILE == 0
    group = lambda i: (i // (PROJ_TILE // tm), 0, 0)
    group_spec = pl.BlockSpec((None, SSM_WIDTH * ROW_GROUP, ROW_TOKENS), group)
    prompt = lambda i: jnp.minimum(i, tp - 1)
    sample = lambda i: jnp.maximum(i - tp, 0)
    const = lambda i: (0, 0)
    resident = lambda shape: pl.BlockSpec(shape, const, pipeline_mode=pl.Buffered(1))
    vec = lambda width: pl.BlockSpec((1, width), const)
    colvec = pl.BlockSpec((SSM_WIDTH, 1), const)
    return pl.pallas_call(
        functools.partial(_post_kernel, alpha=alpha, prompt_tiles=tp),
        grid=(tp + ts,),
        in_specs=[
            pl.BlockSpec((tm, D_MODEL), lambda i: (prompt(i), 0)),
            pl.BlockSpec((tm, D_MODEL), lambda i: (sample(i), 0)),
            pl.BlockSpec((PROJ_Q, tm), lambda i: (0, prompt(i))),
            pl.BlockSpec((PROJ_Q, tm), lambda i: (0, sample(i))),
            group_spec, group_spec,
            vec(D_MODEL), vec(D_MODEL), colvec,
            resident((SSM_WIDTH, SSM_WIDTH)), colvec,
            resident((D_MODEL, D_MODEL)), vec(D_MODEL), vec(D_MODEL),
            resident((D_MODEL, 2 * D_FF)), resident((D_FF, D_MODEL)),
            vec(D_MODEL), vec(D_MODEL),
        ],
        out_specs=[
            pl.BlockSpec((tm, D_MODEL), lambda i: (prompt(i), 0)),
            pl.BlockSpec((tm, D_MODEL), lambda i: (sample(i), 0)),
        ],
        out_shape=[
            jax.ShapeDtypeStruct((n_p, D_MODEL), F32),
            jax.ShapeDtypeStruct((n_s, D_MODEL), F32),
        ],
        compiler_params=pltpu.CompilerParams(
            dimension_semantics=("arbitrary",), vmem_limit_bytes=V7X_VMEM_LIMIT_BYTES),
        name="post",
    )(xp2d, xs2d, atp, ats, ut3, yt3, ln_g, ln_b, d_col, w_glu_t, b_glu_col, w_out, ln1_g, ln1_b,
      w_gate_up, w_down, ln2_g, ln2_b)


def kernel(x_prompt, x_sample, cache_win_k, cache_win_v, state_ssm_re, state_ssm_im,
           ln_in_g, ln_in_b, w_in, attn_sinks, ssm_lambda_re, ssm_lambda_im, ssm_log_step,
           ssm_b_re, ssm_b_im, ssm_c_re, ssm_c_im, ssm_d, w_glu, b_glu, w_out,
           ln1_g, ln1_b, w_gate_up, w_down, ln2_g, ln2_b):
    depth = w_in.shape[0]
    assert depth == 1, "single-layer step"
    bp, lp, _ = x_prompt.shape
    bs, ls, _ = x_sample.shape
    assert ls == SSM_T and bs % 2 == 0 and lp % 512 == 0 and bp <= STATE_OUT_ROWS
    win_rows = cache_win_k.shape[2]
    assert win_rows == WINDOW
    alpha = (2.0 * depth) ** 0.25
    l = 0
    row = lambda a: a.reshape(1, -1)
    column = lambda a: a.reshape(-1, 1)

    w_in_b = w_in[l].astype(BF16)
    w_kv = w_in_b[:, PROJ_Q:PROJ_Q + 2 * PROJ_KV]
    w_qvu_t = jnp.concatenate([w_in_b[:, :PROJ_Q], w_in_b[:, PROJ_Q + PROJ_KV:]], axis=1).T
    lng, lnb = row(ln_in_g), row(ln_in_b)
    xp2 = x_prompt.reshape(bp * lp, D_MODEL)
    xs2 = x_sample.reshape(bs * ls, D_MODEL)
    n_p = bp * lp
    qt, vt, u3, k, v = _proj(xp2, xs2, lng, lnb, w_kv, w_qvu_t)

    sink_rows = jnp.repeat(attn_sinks[l].reshape(N_KV_HEADS, Q_PER_KV), PAIR, axis=1)[:, None, :]
    atp = _attention_prompt(sink_rows, qt, k, vt, bp, lp, n_pairs=4)
    ck = cache_win_k[l].reshape(bs, win_rows, PROJ_KV)
    cv = cache_win_v[l].reshape(bs, win_rows, PROJ_KV)
    cvt = jnp.transpose(cv, (2, 0, 1)).reshape(PROJ_KV, bs * win_rows).astype(BF16)
    ats = _attention_sample(sink_rows, qt, ck, k, cvt, vt, first_token=n_p, n_pairs=4)
    kp3, vp3 = k[:n_p].reshape(bp, lp, PROJ_KV), v[:n_p].reshape(bp, lp, PROJ_KV)
    ks3, vs3 = k[n_p:].reshape(bs, ls, PROJ_KV), v[n_p:].reshape(bs, ls, PROJ_KV)

    table_params = _table_params(
        ssm_lambda_re[l], ssm_lambda_im[l], ssm_log_step[l],
        ssm_b_re[l], ssm_b_im[l], ssm_c_re[l], ssm_c_im[l])
    sre = jnp.swapaxes(state_ssm_re[l], 0, 1)
    sim = jnp.swapaxes(state_ssm_im[l], 0, 1)
    h0 = jnp.concatenate([sre, sim], axis=-1)
    h0s = jnp.concatenate([sim, sre], axis=-1)
    y4, hfin_p, hfin_e, hfin_o = _ssm(
        table_params, u3, h0[:, 0::2], h0[:, 1::2], h0s[:, 0::2], h0s[:, 1::2],
        rows_per_seq=lp // ROW_TOKENS, prompt_rows=n_p // ROW_TOKENS)
    hfin_s = jnp.stack([hfin_e, hfin_o], axis=2).reshape(SSM_GROUPS, bs, STATE_LANES)

    post_args = (lng, lnb, column(ssm_d[l]), w_glu[l].T.astype(BF16), column(b_glu[l]),
                 w_out[l].astype(BF16), row(ln1_g[l]), row(ln1_b[l]),
                 w_gate_up[l].astype(BF16), w_down[l].astype(BF16), row(ln2_g[l]), row(ln2_b[l]))
    out_p, out_s = _post(xp2, xs2, atp, ats, u3, y4.reshape(u3.shape), *post_args, alpha=alpha, tm=512)

    kv_shape = (N_KV_HEADS, HEAD_DIM)
    win_k_p = kp3[:, -win_rows:].reshape(1, bp, win_rows, *kv_shape)
    win_v_p = vp3[:, -win_rows:].reshape(1, bp, win_rows, *kv_shape)
    win_k_s = jnp.concatenate([ck, ks3], axis=1)[:, -win_rows:].reshape(1, bs, win_rows, *kv_shape)
    win_v_s = jnp.concatenate([cv, vs3], axis=1)[:, -win_rows:].reshape(1, bs, win_rows, *kv_shape)
    state = lambda hf, n: jnp.swapaxes(hf[:, :n], 0, 1)
    sp, ss = state(hfin_p, bp), state(hfin_s, bs)
    return (out_p.reshape(bp, lp, D_MODEL), out_s.reshape(bs, ls, D_MODEL),
            win_k_p, win_v_p, sp[None, ..., :SSM_STATE], sp[None, ..., SSM_STATE:],
            win_k_s, win_v_s, ss[None, ..., :SSM_STATE], ss[None, ..., SSM_STATE:])
```

```python
import functools
import math

import jax
import jax.numpy as jnp
from jax import lax
from jax.experimental import pallas as pl
from jax.experimental.pallas import tpu as pltpu

F32 = jnp.float32
BF16 = jnp.bfloat16

D_MODEL = 1024
HEAD_DIM = 64
N_HEADS = 8
N_KV_HEADS = 2
Q_PER_KV = N_HEADS // N_KV_HEADS
CHUNK = 64
WINDOW = 128
WIN_CHUNKS = WINDOW // CHUNK
BAND = (WIN_CHUNKS + 1) * CHUNK
PROJ_Q = N_HEADS * HEAD_DIM
PROJ_KV = N_KV_HEADS * HEAD_DIM
SSM_WIDTH = 512
SSM_GROUP = 16
SSM_GROUPS = SSM_WIDTH // SSM_GROUP
SSM_STATE = 64
STATE_LANES = 2 * SSM_STATE
D_FF = 2816
D_IN_PROJ = PROJ_Q + 2 * PROJ_KV + SSM_WIDTH
LN_EPS = 1e-5
NEG_INF = -1e30

SSM_T = CHUNK
SSM_CW = SSM_T * SSM_GROUP
ROW_TOKENS = 2 * SSM_T
ROW_GROUP = 8
PROJ_TILE = ROW_GROUP * ROW_TOKENS
PROJ_PARTS = 2
COEF_ROWS = 16
STATE_OUT_ROWS = 8
M_BLOCK_CHANNELS = 4
FF_BLOCK = 256
V7X_VMEM_LIMIT_BYTES = 56 * 1024 * 1024


def _layer_norm(x, g, b):
    mu = jnp.mean(x, axis=-1, keepdims=True)
    xc = x - mu
    var = jnp.mean(xc * xc, axis=-1, keepdims=True)
    return xc * lax.rsqrt(var + LN_EPS) * g + b


KV = 2 * PROJ_KV
VU = PROJ_KV + SSM_WIDTH


def _proj_kernel(xp_ref, xs_ref, g_ref, b_ref, w_ref, wqt_ref, wvut_ref, qt_ref, vt_ref, ut_ref, k_ref, v_ref,
                 *, prompt_tiles):
    outs = (g_ref, b_ref, w_ref, wqt_ref, wvut_ref, qt_ref, vt_ref, ut_ref, k_ref, v_ref)

    @pl.when(pl.program_id(0) < prompt_tiles)
    def _():
        _proj_tile(xp_ref, *outs)

    @pl.when(pl.program_id(0) >= prompt_tiles)
    def _():
        _proj_tile(xs_ref, *outs)


def _proj_tile(x_ref, g_ref, b_ref, w_ref, wqt_ref, wvut_ref, qt_ref, vt_ref, ut_ref, k_ref, v_ref):
    rows_per_part = ROW_GROUP // PROJ_PARTS
    nt = (((1,), (1,)), ((), ()))
    for part in range(PROJ_PARTS):
        tok = slice(part * rows_per_part * ROW_TOKENS, (part + 1) * rows_per_part * ROW_TOKENS)
        xb = _layer_norm(x_ref[tok, :], g_ref[...], b_ref[...]).astype(BF16)
        p = jnp.dot(xb, w_ref[...], preferred_element_type=F32)
        k_ref[tok, :] = p[:, :PROJ_KV]
        v_ref[tok, :] = p[:, PROJ_KV:]
        qt = lax.dot_general(wqt_ref[...], xb, nt, preferred_element_type=F32)
        vut = lax.dot_general(wvut_ref[...], xb, nt, preferred_element_type=F32)
        qt_ref[:, tok] = (qt * (HEAD_DIM ** -0.5)).astype(BF16)
        vt_ref[:, tok] = vut[:PROJ_KV].astype(BF16)
        for j in range(rows_per_part):
            ut_ref[pl.ds(part * rows_per_part + j, SSM_WIDTH, stride=ROW_GROUP), :] = (
                vut[PROJ_KV:, j * ROW_TOKENS:(j + 1) * ROW_TOKENS])


def _proj(xp2d, xs2d, ln_g, ln_b, w_in, w_in_t):
    assert D_IN_PROJ == 2 * VU and PROJ_Q % KV == 0
    tm = PROJ_TILE
    tp, ts = xp2d.shape[0] // tm, xs2d.shape[0] // tm
    n = (tp + ts) * tm
    const = lambda i: (0, 0)
    row = lambda i: (i, 0)
    col = lambda i: (0, i)
    return pl.pallas_call(
        functools.partial(_proj_kernel, prompt_tiles=tp),
        grid=(tp + ts,),
        in_specs=[
            pl.BlockSpec((tm, D_MODEL), lambda i: (jnp.minimum(i, tp - 1), 0)),
            pl.BlockSpec((tm, D_MODEL), lambda i: (jnp.maximum(i - tp, 0), 0)),
            pl.BlockSpec((1, D_MODEL), const),
            pl.BlockSpec((1, D_MODEL), const),
            pl.BlockSpec((D_MODEL, KV), lambda i: (0, PROJ_Q // KV)),
            pl.BlockSpec((PROJ_Q, D_MODEL), const),
            pl.BlockSpec((VU, D_MODEL), lambda i: (1, 0)),
        ],
        out_specs=[
            pl.BlockSpec((PROJ_Q, tm), col),
            pl.BlockSpec((PROJ_KV, tm), col),
            pl.BlockSpec((None, SSM_WIDTH * ROW_GROUP, ROW_TOKENS), lambda i: (i, 0, 0)),
            pl.BlockSpec((tm, PROJ_KV), row),
            pl.BlockSpec((tm, PROJ_KV), row),
        ],
        out_shape=[
            jax.ShapeDtypeStruct((PROJ_Q, n), BF16),
            jax.ShapeDtypeStruct((PROJ_KV, n), BF16),
            jax.ShapeDtypeStruct((n // tm, SSM_WIDTH * ROW_GROUP, ROW_TOKENS), F32),
            jax.ShapeDtypeStruct((n, PROJ_KV), F32),
            jax.ShapeDtypeStruct((n, PROJ_KV), F32),
        ],
        compiler_params=pltpu.CompilerParams(
            dimension_semantics=("arbitrary",), vmem_limit_bytes=V7X_VMEM_LIMIT_BYTES),
        name="proj",
    )(xp2d, xs2d, ln_g, ln_b, w_in, w_in_t, w_in_t)


PAIR = 2 * CHUNK
HEAD_LANES = Q_PER_KV * PAIR
ONES_ROWS = 16


def _scores(unit, qt_ref):
    kwin, _, lanes, _, _, h = unit
    base = h * Q_PER_KV * HEAD_DIM
    qrow = jnp.concatenate(
        [qt_ref[base + g * HEAD_DIM:base + (g + 1) * HEAD_DIM, lanes] for g in range(Q_PER_KV)],
        axis=1)
    zero = jnp.zeros_like(qrow)
    qstack = jnp.concatenate([qrow, zero] if h == 0 else [zero, qrow], axis=0)
    return jnp.dot(kwin, qstack, preferred_element_type=F32)


def _finish(unit, s, sink_ref, o_ref):
    _, vtwin, lanes, valid, masked_rows, h = unit
    nk = s.shape[0]
    base = h * Q_PER_KV * HEAD_DIM
    pieces, done = [], 0
    for start, stop in masked_rows:
        if start > done:
            pieces.append(s[done:start])
        pieces.append(jnp.where(valid[start:stop], s[start:stop], NEG_INF))
        done = stop
    if done < nk:
        pieces.append(s[done:])
    s = jnp.concatenate(pieces, axis=0)
    sink = sink_ref[h]
    m = jnp.maximum(jnp.max(s, axis=0, keepdims=True), sink)
    p = jnp.exp(s - m).astype(BF16)
    v_ones = jnp.concatenate(
        [vtwin[h * HEAD_DIM:(h + 1) * HEAD_DIM, :], jnp.ones((ONES_ROWS, nk), BF16)], axis=0)
    ov = jnp.dot(v_ones, p, preferred_element_type=F32)
    den = ov[HEAD_DIM:HEAD_DIM + 1, :] + jnp.exp(sink - m)
    o = ov[:HEAD_DIM, :] * (1.0 / den)
    for g in range(Q_PER_KV):
        o_ref[base + g * HEAD_DIM:base + (g + 1) * HEAD_DIM, lanes] = (
            o[:, g * PAIR:(g + 1) * PAIR].astype(BF16))


def _attend_units(units, qt_ref, sink_ref, o_ref):
    s_next = _scores(units[0], qt_ref)
    for i, unit in enumerate(units):
        s = s_next
        if i + 1 < len(units):
            s_next = _scores(units[i + 1], qt_ref)
        _finish(unit, s, sink_ref, o_ref)


def _attn_prompt_kernel(sink_ref, qt_ref, kp_ref, kc_ref, vtp_ref, vtc_ref, o_ref, *, n_pairs):
    tq = n_pairs * PAIR
    nk = WINDOW + PAIR
    kk = jnp.concatenate([kp_ref[...], kc_ref[...]], axis=0).astype(BF16)
    vt = jnp.concatenate([vtp_ref[...], vtc_ref[...]], axis=1)
    r = lax.broadcasted_iota(jnp.int32, (nk, HEAD_LANES), 0)
    first_chunk = (lax.broadcasted_iota(jnp.int32, (nk, HEAD_LANES), 1) & (PAIR - 1)) < CHUNK
    lo = jnp.where(first_chunk, 0, CHUNK)
    hi = jnp.where(first_chunk, BAND, nk)
    first_pos = pl.program_id(1) * tq - WINDOW
    units = []
    for pp in range(n_pairs):
        lo_pp = jnp.maximum(lo, -first_pos) if pp == 0 else lo
        valid = (r >= lo_pp) & (r < hi)
        masked_rows = ((0, nk),) if pp == 0 else ((0, CHUNK), (nk - CHUNK, nk))
        for h in range(N_KV_HEADS):
            units.append((kk[pp * PAIR:pp * PAIR + nk], vt[:, pp * PAIR:pp * PAIR + nk],
                          slice(pp * PAIR, (pp + 1) * PAIR), valid, masked_rows, h))
    _attend_units(units, qt_ref, sink_ref, o_ref)


def _attn_sample_kernel(sink_ref, qt_ref, ck_ref, kn_ref, cvt_ref, vtn_ref, o_ref, *, n_pairs):
    nk = 2 * WINDOW + PAIR
    r = lax.broadcasted_iota(jnp.int32, (nk, HEAD_LANES), 0)
    query_seq = (lax.broadcasted_iota(jnp.int32, (nk, HEAD_LANES), 1) & (PAIR - 1)) >> 6
    key_seq = jnp.where(r < 2 * WINDOW, r >> 7, (r - 2 * WINDOW) >> 6)
    valid = query_seq == key_seq
    units = []
    for pp in range(n_pairs):
        lanes = slice(pp * PAIR, (pp + 1) * PAIR)
        kwin = jnp.concatenate([ck_ref[2 * pp], ck_ref[2 * pp + 1], kn_ref[lanes, :]],
                               axis=0).astype(BF16)
        vtwin = jnp.concatenate([cvt_ref[:, 2 * pp * WINDOW:2 * (pp + 1) * WINDOW], vtn_ref[:, lanes]],
                                axis=1)
        units.extend((kwin, vtwin, lanes, valid, ((0, nk),), h) for h in range(N_KV_HEADS))
    _attend_units(units, qt_ref, sink_ref, o_ref)


_SINK_SPEC = pl.BlockSpec((N_KV_HEADS, 1, HEAD_LANES), lambda *_: (0, 0, 0))


def _attention_prompt(sink_rows, qt, k, vt, bsz, seq, n_pairs):
    tq = n_pairs * PAIR
    nt = seq // tq
    wpt = tq // WINDOW
    cur_c = lambda b, i: (0, b * nt + i)
    cur_r = lambda b, i: (b * nt + i, 0)
    prev = lambda b, i: jnp.maximum((b * nt + i) * wpt - 1, 0)
    return pl.pallas_call(
        functools.partial(_attn_prompt_kernel, n_pairs=n_pairs),
        grid=(bsz, nt),
        in_specs=[
            _SINK_SPEC,
            pl.BlockSpec((PROJ_Q, tq), cur_c),
            pl.BlockSpec((WINDOW, PROJ_KV), lambda b, i: (prev(b, i), 0)),
            pl.BlockSpec((tq, PROJ_KV), cur_r),
            pl.BlockSpec((PROJ_KV, WINDOW), lambda b, i: (0, prev(b, i))),
            pl.BlockSpec((PROJ_KV, tq), cur_c),
        ],
        out_specs=pl.BlockSpec((PROJ_Q, tq), cur_c),
        out_shape=jax.ShapeDtypeStruct((PROJ_Q, bsz * seq), BF16),
        compiler_params=pltpu.CompilerParams(dimension_semantics=("arbitrary", "arbitrary")),
        name="attn_prompt",
    )(sink_rows, qt, k, k, vt, vt)


def _attention_sample(sink_rows, qt, cache_k, k, cache_vt, vt, first_token, n_pairs):
    n = cache_k.shape[0] * CHUNK
    tq = n_pairs * PAIR
    assert first_token % tq == 0
    off = first_token // tq
    return pl.pallas_call(
        functools.partial(_attn_sample_kernel, n_pairs=n_pairs),
        grid=(n // tq,),
        in_specs=[
            _SINK_SPEC,
            pl.BlockSpec((PROJ_Q, tq), lambda i: (0, i + off)),
            pl.BlockSpec((2 * n_pairs, WINDOW, PROJ_KV), lambda i: (i, 0, 0)),
            pl.BlockSpec((tq, PROJ_KV), lambda i: (i + off, 0)),
            pl.BlockSpec((PROJ_KV, 2 * n_pairs * WINDOW), lambda i: (0, i)),
            pl.BlockSpec((PROJ_KV, tq), lambda i: (0, i + off)),
        ],
        out_specs=pl.BlockSpec((PROJ_Q, tq), lambda i: (0, i)),
        out_shape=jax.ShapeDtypeStruct((PROJ_Q, n), BF16),
        compiler_params=pltpu.CompilerParams(dimension_semantics=("arbitrary",)),
        name="attn_sample",
    )(sink_rows, qt, cache_k, k, cache_vt, vt)


def _power_table(tau, nbits, a_re, a_im):
    rows = tau.shape[0]
    w_re = jnp.ones((rows, STATE_LANES), F32)
    w_im = jnp.zeros((rows, STATE_LANES), F32)
    p_re, p_im = a_re, a_im
    for k in range(nbits):
        bit = ((tau >> k) & 1) == 1
        f_re = jnp.where(bit, p_re, 1.0)
        f_im = jnp.where(bit, p_im, 0.0)
        w_re, w_im = w_re * f_re - w_im * f_im, w_re * f_im + w_im * f_re
        p_re, p_im = p_re * p_re - p_im * p_im, 2.0 * p_re * p_im
    return w_re, w_im


def _build_tables(par_ref, p_ref, qt_ref, coef_ref):
    lo = lax.broadcasted_iota(jnp.int32, (1, STATE_LANES), 1) < SSM_STATE
    mat = lambda i: par_ref[PAR_MATS + i * SSM_GROUP:PAR_MATS + (i + 1) * SSM_GROUP, :]
    lr, li = par_ref[0:1, :], par_ref[1:2, :]
    dt = jnp.exp(par_ref[2:3, :])
    mag = jnp.exp(lr * dt)
    a_re, a_im = mag * jnp.cos(li * dt), mag * jnp.sin(li * dt)
    nr, ni = a_re - 1.0, a_im
    den = lr * lr + li * li
    f_re, f_im = (nr * lr + ni * li) / den, (ni * lr - nr * li) / den
    b_re, b_im = mat(0), mat(1)
    bb_re = f_re * b_re - f_im * b_im
    bb_im = f_re * b_im + f_im * b_re
    c_re, c_im = mat(2), mat(3)

    tau = lax.broadcasted_iota(jnp.int32, (SSM_T, 1), 0)
    w_re, w_im = _power_table(tau, 6, a_re, a_im)
    w1_re, w1_im = _power_table(tau + 1, 7, a_re, a_im)
    wr_re, wr_im = _power_table(SSM_T - 1 - tau, 6, a_re, a_im)

    def outer(c, w):
        return (c[:, None, :] * w[None, :, :]).reshape(SSM_CW, STATE_LANES)

    cw_mix = (outer(c_re, jnp.where(lo, w_re, w_im)) + outer(c_im, jnp.where(lo, -w_im, w_re)))
    bb_mix = jnp.where(lo, bb_re, -bb_im)
    strip = lax.dot_general(bb_mix, cw_mix, (((1,), (1,)), ((), ())),
                            precision=lax.Precision.HIGHEST,
                            preferred_element_type=F32)

    x_a, y_a = jnp.where(lo, bb_re, bb_im), jnp.where(lo, -bb_im, bb_re)
    x_b, y_b = jnp.where(lo, bb_im, bb_re), jnp.where(lo, bb_re, -bb_im)
    p_ref[:, :STATE_LANES] = (outer(x_a, wr_re) + outer(y_a, wr_im)).astype(BF16)
    p_ref[:, STATE_LANES:] = (outer(x_b, wr_re) + outer(y_b, wr_im)).astype(BF16)

    qt_ref[...] = (outer(c_re, jnp.where(lo, w1_re, -w1_im))
                   + outer(c_im, jnp.where(lo, -w1_im, -w1_re))).astype(BF16)

    t_re, t_im = a_re, a_im
    for _ in range(6):
        t_re, t_im = t_re * t_re - t_im * t_im, 2.0 * t_re * t_im
    for k in range(COEF_ROWS // 2):
        coef_ref[2 * k:2 * k + 1, :] = t_re
        coef_ref[2 * k + 1:2 * k + 2, :] = jnp.where(lo, -t_im, t_im)
        t_re, t_im = t_re * t_re - t_im * t_im, 2.0 * t_re * t_im
    return strip


def _toeplitz_rows(strip, m_ref, channels):
    s_idx = lax.broadcasted_iota(jnp.int32, (SSM_T, SSM_CW), 0)
    t_idx = lax.broadcasted_iota(jnp.int32, (SSM_T, SSM_CW), 1) & (SSM_T - 1)
    causal = t_idx >= s_idx
    for c in channels:
        rows = jnp.broadcast_to(strip[c:c + 1, :], (SSM_T, SSM_CW))
        shifted = pltpu.roll(rows, 0, 1, stride=1, stride_axis=0)
        m_ref[c * SSM_T:(c + 1) * SSM_T, :] = jnp.where(causal, shifted, 0.0).astype(BF16)


PAR_MATS = 8
PAR_ROWS = PAR_MATS + 4 * SSM_GROUP


def _table_params(lam_re, lam_im, log_step, b_re, b_im, c_re, c_im):
    step = jnp.broadcast_to(log_step[:, None, None], (SSM_GROUPS, 1, SSM_STATE))
    filler = jnp.zeros((SSM_GROUPS, PAR_MATS - 3, SSM_STATE), F32)
    pack = jnp.concatenate([lam_re[:, None, :], lam_im[:, None, :], step, filler,
                            jnp.swapaxes(b_re, 1, 2), jnp.swapaxes(b_im, 1, 2), c_re, c_im], axis=1)
    return jnp.concatenate([pack, pack], axis=-1)


def _cmul(a1, a2, h, hs):
    return a1 * h + a2 * hs, a1 * hs - a2 * h


def _ssm_kernel(par_ref, u_ref, h0_ref, y_ref, hfin_ref,
                m_ref, p_ref, qt_ref, coef_ref, *, rows_per_seq, prompt_rows):
    strip = _build_tables(par_ref, p_ref, qt_ref, coef_ref)
    lo = lax.broadcasted_iota(jnp.int32, (1, ROW_TOKENS), 1) < SSM_T

    def chunk_rows(ref):
        even, odd = [], []
        rows = ref.shape[0] * ROW_GROUP
        for k in range(SSM_GROUP // 2):
            a = ref[:, 2 * k].reshape(rows, ROW_TOKENS)
            b = ref[:, 2 * k + 1].reshape(rows, ROW_TOKENS)
            even.append(jnp.where(lo, a, pltpu.roll(b, SSM_T, 1)))
            odd.append(jnp.where(lo, pltpu.roll(a, SSM_T, 1), b))
        return jnp.concatenate(even, axis=1), jnp.concatenate(odd, axis=1)

    def store_rows(y_even, y_odd, ref):
        for k in range(SSM_GROUP // 2):
            te = y_even[:, k * ROW_TOKENS:(k + 1) * ROW_TOKENS]
            to = y_odd[:, k * ROW_TOKENS:(k + 1) * ROW_TOKENS]
            tiles = (ref.shape[0], ROW_GROUP, ROW_TOKENS)
            ref[:, 2 * k] = jnp.where(lo, te, pltpu.roll(to, SSM_T, 1)).reshape(tiles)
            ref[:, 2 * k + 1] = jnp.where(lo, pltpu.roll(te, SSM_T, 1), to).reshape(tiles)

    even, odd = chunk_rows(u_ref)
    rp, rs = prompt_rows, even.shape[0] - prompt_rows
    u = jnp.concatenate([even[:rp], odd[:rp], even[rp:], odd[rp:]], axis=0).astype(BF16)
    s12 = jnp.dot(u, p_ref[...], preferred_element_type=F32)
    s1, s2 = s12[:, :STATE_LANES], s12[:, STATE_LANES:]
    a1, a2 = coef_ref[0:1, :], coef_ref[1:2, :]

    e1, e2, o1, o2 = s1[:rp], s2[:rp], s1[rp:2 * rp], s2[rp:2 * rp]
    x1, x2 = _cmul(a1, a2, e1, e2)
    x1, x2 = x1 + o1, x2 + o2
    pos = lax.broadcasted_iota(jnp.int32, (rp, STATE_LANES), 0) & (rows_per_seq - 1)
    for k in range(rows_per_seq.bit_length() - 1):
        d = 1 << k
        b1, b2 = coef_ref[2 + 2 * k:3 + 2 * k, :], coef_ref[3 + 2 * k:4 + 2 * k, :]
        sh1 = jnp.where(pos >= d, pltpu.roll(x1, d, 0), 0.0)
        sh2 = jnp.where(pos >= d, pltpu.roll(x2, d, 0), 0.0)
        y1, y2 = _cmul(b1, b2, sh1, sh2)
        x1, x2 = x1 + y1, x2 + y2
    g1 = jnp.where(pos >= 1, pltpu.roll(x1, 1, 0), 0.0)
    g2 = jnp.where(pos >= 1, pltpu.roll(x2, 1, 0), 0.0)
    ho1 = _cmul(a1, a2, g1, g2)[0] + e1
    hfin_ref[0:STATE_OUT_ROWS, :] = jnp.zeros((STATE_OUT_ROWS, STATE_LANES), F32)
    for b in range(rp // rows_per_seq):
        last = (b + 1) * rows_per_seq - 1
        hfin_ref[b:b + 1, :] = x1[last:last + 1, :]

    h0e, h0o = h0_ref[0], h0_ref[1]
    hfin_ref[STATE_OUT_ROWS:STATE_OUT_ROWS + rs, :] = a1 * h0e + a2 * h0_ref[2] + s1[2 * rp:2 * rp + rs]
    hfin_ref[STATE_OUT_ROWS + rs:, :] = a1 * h0o + a2 * h0_ref[3] + s1[2 * rp + rs:]

    hprev = jnp.concatenate([g1, ho1, h0e, h0o], axis=0).astype(BF16)
    y = lax.dot_general(hprev, qt_ref[...], (((1,), (1,)), ((), ())), preferred_element_type=F32)
    blocks = [range(c, c + M_BLOCK_CHANNELS) for c in range(0, SSM_GROUP, M_BLOCK_CHANNELS)]
    _toeplitz_rows(strip, m_ref, blocks[0])
    for i, channels in enumerate(blocks):
        rows = slice(channels[0] * SSM_T, (channels[-1] + 1) * SSM_T)
        y += jnp.dot(u[:, rows], m_ref[rows, :], preferred_element_type=F32)
        if i + 1 < len(blocks):
            _toeplitz_rows(strip, m_ref, blocks[i + 1])
    store_rows(jnp.concatenate([y[:rp], y[2 * rp:2 * rp + rs]], axis=0),
               jnp.concatenate([y[rp:2 * rp], y[2 * rp + rs:]], axis=0), y_ref)


def _ssm(table_params, u3, h0_pack, *, rows_per_seq, prompt_rows):
    groups = u3.shape[0]
    rs = groups * ROW_GROUP - prompt_rows
    u4 = u3.reshape(groups, SSM_WIDTH, ROW_GROUP, ROW_TOKENS)
    assert rows_per_seq & (rows_per_seq - 1) == 0 and 2 * rows_per_seq.bit_length() <= COEF_ROWS
    g3 = lambda g: (g, 0, 0)
    u_spec = pl.BlockSpec((groups, SSM_GROUP, ROW_GROUP, ROW_TOKENS), lambda g: (0, g, 0, 0))
    fin_rows = STATE_OUT_ROWS + 2 * rs
    return pl.pallas_call(
        functools.partial(_ssm_kernel, rows_per_seq=rows_per_seq, prompt_rows=prompt_rows),
        grid=(SSM_GROUPS,),
        in_specs=[
            pl.BlockSpec((None, PAR_ROWS, STATE_LANES), g3),
            u_spec,
            pl.BlockSpec((None, 4, rs, STATE_LANES), lambda g: (g, 0, 0, 0)),
        ],
        out_specs=[
            u_spec,
            pl.BlockSpec((None, fin_rows, STATE_LANES), g3),
        ],
        out_shape=[
            jax.ShapeDtypeStruct(u4.shape, F32),
            jax.ShapeDtypeStruct((SSM_GROUPS, fin_rows, STATE_LANES), F32),
        ],
        scratch_shapes=[
            pltpu.VMEM((SSM_CW, SSM_CW), BF16),
            pltpu.VMEM((SSM_CW, 2 * STATE_LANES), BF16),
            pltpu.VMEM((SSM_CW, STATE_LANES), BF16),
            pltpu.VMEM((COEF_ROWS, STATE_LANES), F32),
        ],
        compiler_params=pltpu.CompilerParams(dimension_semantics=("arbitrary",)),
        name="ssm",
    )(table_params, u4, h0_pack)


def _post_kernel(xp_ref, xs_ref, atp_ref, ats_ref, ut_ref, yt_ref, *refs, alpha, prompt_tiles):
    weights, (op_ref, os_ref) = refs[:-2], refs[-2:]

    @pl.when(pl.program_id(0) < prompt_tiles)
    def _():
        _post_tile(xp_ref, atp_ref, ut_ref, yt_ref, *weights, op_ref, alpha=alpha)

    @pl.when(pl.program_id(0) >= prompt_tiles)
    def _():
        _post_tile(xs_ref, ats_ref, ut_ref, yt_ref, *weights, os_ref, alpha=alpha)


def _post_tile(x_ref, at_ref, ut_ref, yt_ref, lng_ref, lnb_ref, d_ref, wglut_ref, bglu_ref,
               wout_ref, ln1g_ref, ln1b_ref, wgu_ref, wdown_ref, ln2g_ref, ln2b_ref, o_ref,
               *, alpha):
    xn = _layer_norm(x_ref[...], lng_ref[...], lnb_ref[...])
    n_rows = x_ref.shape[0] // ROW_TOKENS
    first_row = (pl.program_id(0) % (ROW_GROUP // n_rows)) * n_rows

    def feature_major(ref):
        return jnp.concatenate(
            [ref[pl.ds(first_row + j, SSM_WIDTH, stride=ROW_GROUP), :] for j in range(n_rows)], axis=1)

    ys = feature_major(yt_ref) + d_ref[...] * feature_major(ut_ref)
    gl = 0.5 * ys * (1.0 + lax.erf(ys * math.sqrt(0.5)))
    z = jnp.dot(wglut_ref[...], gl.astype(BF16), preferred_element_type=F32) + bglu_ref[...]
    s = gl * jax.nn.sigmoid(z)
    tn = (((0,), (0,)), ((), ()))
    mix = lax.dot_general(at_ref[...], wout_ref[:PROJ_Q, :], tn, preferred_element_type=F32)
    mix += lax.dot_general(s.astype(BF16), wout_ref[PROJ_Q:, :], tn, preferred_element_type=F32)
    h = _layer_norm(alpha * xn + mix, ln1g_ref[...], ln1b_ref[...])
    hb = h.astype(BF16)
    f = jnp.zeros_like(h)
    for j in range(D_FF // FF_BLOCK):
        cols = slice(j * FF_BLOCK, (j + 1) * FF_BLOCK)
        up_cols = slice(D_FF + j * FF_BLOCK, D_FF + (j + 1) * FF_BLOCK)
        g = jnp.dot(hb, wgu_ref[:, cols], preferred_element_type=F32)
        up = jnp.dot(hb, wgu_ref[:, up_cols], preferred_element_type=F32)
        act = (g * jax.nn.sigmoid(g)) * up
        f += jnp.dot(act.astype(BF16), wdown_ref[cols, :], preferred_element_type=F32)
    o_ref[...] = _layer_norm(alpha * h + f, ln2g_ref[...], ln2b_ref[...])


def _post(xp2d, xs2d, atp, ats, ut3, yt3, ln_g, ln_b, d_col, w_glu_t, b_glu_col, w_out, ln1_g, ln1_b,
          w_gate_up, w_down, ln2_g, ln2_b, *, alpha, tm):
    n_p, n_s = xp2d.shape[0], xs2d.shape[0]
    tp, ts = n_p // tm, n_s // tm
    assert PROJ_TILE % tm == 0 and n_p % PROJ_TILE == 0
    group = lambda i: (i // (PROJ_TILE // tm), 0, 0)
    group_spec = pl.BlockSpec((None, SSM_WIDTH * ROW_GROUP, ROW_TOKENS), group)
    prompt = lambda i: jnp.minimum(i, tp - 1)
    sample = lambda i: jnp.maximum(i - tp, 0)
    const = lambda i: (0, 0)
    resident = lambda shape: pl.BlockSpec(shape, const, pipeline_mode=pl.Buffered(1))
    vec = lambda width: pl.BlockSpec((1, width), const)
    colvec = pl.BlockSpec((SSM_WIDTH, 1), const)
    return pl.pallas_call(
        functools.partial(_post_kernel, alpha=alpha, prompt_tiles=tp),
        grid=(tp + ts,),
        in_specs=[
            pl.BlockSpec((tm, D_MODEL), lambda i: (prompt(i), 0)),
            pl.BlockSpec((tm, D_MODEL), lambda i: (sample(i), 0)),
            pl.BlockSpec((PROJ_Q, tm), lambda i: (0, prompt(i))),
            pl.BlockSpec((PROJ_Q, tm), lambda i: (0, sample(i))),
            group_spec, group_spec,
            vec(D_MODEL), vec(D_MODEL), colvec,
            resident((SSM_WIDTH, SSM_WIDTH)), colvec,
            resident((D_MODEL, D_MODEL)), vec(D_MODEL), vec(D_MODEL),
            resident((D_MODEL, 2 * D_FF)), resident((D_FF, D_MODEL)),
            vec(D_MODEL), vec(D_MODEL),
        ],
        out_specs=[
            pl.BlockSpec((tm, D_MODEL), lambda i: (prompt(i), 0)),
            pl.BlockSpec((tm, D_MODEL), lambda i: (sample(i), 0)),
        ],
        out_shape=[
            jax.ShapeDtypeStruct((n_p, D_MODEL), F32),
            jax.ShapeDtypeStruct((n_s, D_MODEL), F32),
        ],
        compiler_params=pltpu.CompilerParams(
            dimension_semantics=("arbitrary",), vmem_limit_bytes=V7X_VMEM_LIMIT_BYTES),
        name="post",
    )(xp2d, xs2d, atp, ats, ut3, yt3, ln_g, ln_b, d_col, w_glu_t, b_glu_col, w_out, ln1_g, ln1_b,
      w_gate_up, w_down, ln2_g, ln2_b)


def kernel(x_prompt, x_sample, cache_win_k, cache_win_v, state_ssm_re, state_ssm_im,
           ln_in_g, ln_in_b, w_in, attn_sinks, ssm_lambda_re, ssm_lambda_im, ssm_log_step,
           ssm_b_re, ssm_b_im, ssm_c_re, ssm_c_im, ssm_d, w_glu, b_glu, w_out,
           ln1_g, ln1_b, w_gate_up, w_down, ln2_g, ln2_b):
    depth = w_in.shape[0]
    assert depth == 1, "single-layer step"
    bp, lp, _ = x_prompt.shape
    bs, ls, _ = x_sample.shape
    assert ls == SSM_T and bs % 2 == 0 and lp % 512 == 0 and bp <= STATE_OUT_ROWS
    win_rows = cache_win_k.shape[2]
    assert win_rows == WINDOW
    alpha = (2.0 * depth) ** 0.25
    l = 0
    row = lambda a: a.reshape(1, -1)
    column = lambda a: a.reshape(-1, 1)

    lng, lnb = row(ln_in_g), row(ln_in_b)
    xp2 = x_prompt.reshape(bp * lp, D_MODEL)
    xs2 = x_sample.reshape(bs * ls, D_MODEL)
    n_p = bp * lp
    qt, vt, u3, k, v = _proj(xp2, xs2, lng, lnb, w_in[l].astype(BF16), w_in[l].T.astype(BF16))

    sink_rows = jnp.repeat(attn_sinks[l].reshape(N_KV_HEADS, Q_PER_KV), PAIR, axis=1)[:, None, :]
    atp = _attention_prompt(sink_rows, qt, k, vt, bp, lp, n_pairs=4)
    ck = cache_win_k[l].reshape(bs, win_rows, PROJ_KV)
    cv = cache_win_v[l].reshape(bs, win_rows, PROJ_KV)
    cvt = jnp.transpose(cv, (2, 0, 1)).reshape(PROJ_KV, bs * win_rows).astype(BF16)
    ats = _attention_sample(sink_rows, qt, ck, k, cvt, vt, first_token=n_p, n_pairs=4)
    ks3, vs3 = k[n_p:].reshape(bs, ls, PROJ_KV), v[n_p:].reshape(bs, ls, PROJ_KV)

    table_params = _table_params(
        ssm_lambda_re[l], ssm_lambda_im[l], ssm_log_step[l],
        ssm_b_re[l], ssm_b_im[l], ssm_c_re[l], ssm_c_im[l])
    sre, sim = state_ssm_re[l], state_ssm_im[l]
    both = jnp.stack([jnp.concatenate([sre, sim], axis=-1), jnp.concatenate([sim, sre], axis=-1)])
    h0_pack = jnp.transpose(both.reshape(2, bs // 2, 2, SSM_GROUPS, STATE_LANES),
                            (3, 0, 2, 1, 4)).reshape(SSM_GROUPS, 4, bs // 2, STATE_LANES)
    y4, hfin = _ssm(table_params, u3, h0_pack,
                    rows_per_seq=lp // ROW_TOKENS, prompt_rows=n_p // ROW_TOKENS)

    post_args = (lng, lnb, column(ssm_d[l]), w_glu[l].T.astype(BF16), column(b_glu[l]),
                 w_out[l].astype(BF16), row(ln1_g[l]), row(ln1_b[l]),
                 w_gate_up[l].astype(BF16), w_down[l].astype(BF16), row(ln2_g[l]), row(ln2_b[l]))
    out_p, out_s = _post(xp2, xs2, atp, ats, u3, y4.reshape(u3.shape), *post_args, alpha=alpha, tm=512)

    kv_shape = (N_KV_HEADS, HEAD_DIM)
    last_rows = lambda a: jnp.stack([a[(b + 1) * lp - win_rows:(b + 1) * lp] for b in range(bp)])
    win_k_p = last_rows(k).reshape(1, bp, win_rows, *kv_shape)
    win_v_p = last_rows(v).reshape(1, bp, win_rows, *kv_shape)
    win_k_s = jnp.concatenate([ck, ks3], axis=1)[:, -win_rows:].reshape(1, bs, win_rows, *kv_shape)
    win_v_s = jnp.concatenate([cv, vs3], axis=1)[:, -win_rows:].reshape(1, bs, win_rows, *kv_shape)
    sp = jnp.swapaxes(hfin[:, :bp], 0, 1)
    ss = jnp.transpose(hfin[:, STATE_OUT_ROWS:].reshape(SSM_GROUPS, 2, bs // 2, STATE_LANES),
                       (2, 1, 0, 3)).reshape(bs, SSM_GROUPS, STATE_LANES)
    return (out_p.reshape(bp, lp, D_MODEL), out_s.reshape(bs, ls, D_MODEL),
            win_k_p, win_v_p, sp[None, ..., :SSM_STATE], sp[None, ..., SSM_STATE:],
            win_k_s, win_v_s, ss[None, ..., :SSM_STATE], ss[None, ..., SSM_STATE:])
```

```python
import functools
import math

import jax
import jax.numpy as jnp
from jax import lax
from jax.experimental import pallas as pl
from jax.experimental.pallas import tpu as pltpu

F32 = jnp.float32
BF16 = jnp.bfloat16

D_MODEL = 1024
HEAD_DIM = 64
N_HEADS = 8
N_KV_HEADS = 2
Q_PER_KV = N_HEADS // N_KV_HEADS
CHUNK = 64
WINDOW = 128
WIN_CHUNKS = WINDOW // CHUNK
BAND = (WIN_CHUNKS + 1) * CHUNK
PROJ_Q = N_HEADS * HEAD_DIM
PROJ_KV = N_KV_HEADS * HEAD_DIM
SSM_WIDTH = 512
SSM_GROUP = 16
SSM_GROUPS = SSM_WIDTH // SSM_GROUP
SSM_STATE = 64
STATE_LANES = 2 * SSM_STATE
D_FF = 2816
D_IN_PROJ = PROJ_Q + 2 * PROJ_KV + SSM_WIDTH
LN_EPS = 1e-5
NEG_INF = -1e30

SSM_T = CHUNK
SSM_CW = SSM_T * SSM_GROUP
ROW_TOKENS = 2 * SSM_T
ROW_GROUP = 8
PROJ_TILE = ROW_GROUP * ROW_TOKENS
PROJ_PARTS = 2
COEF_ROWS = 16
STATE_OUT_ROWS = 8
M_BLOCK_CHANNELS = 4
FF_BLOCK = 256
V7X_VMEM_LIMIT_BYTES = 56 * 1024 * 1024


def _layer_norm(x, g, b):
    mu = jnp.mean(x, axis=-1, keepdims=True)
    xc = x - mu
    var = jnp.mean(xc * xc, axis=-1, keepdims=True)
    return xc * lax.rsqrt(var + LN_EPS) * g + b


KV = 2 * PROJ_KV
VU = PROJ_KV + SSM_WIDTH


def _proj_kernel(xp_ref, xs_ref, g_ref, b_ref, w_ref, wqt_ref, wvut_ref, qt_ref, vt_ref, ut_ref, k_ref, v_ref,
                 *, prompt_tiles):
    outs = (g_ref, b_ref, w_ref, wqt_ref, wvut_ref, qt_ref, vt_ref, ut_ref, k_ref, v_ref)

    @pl.when(pl.program_id(0) < prompt_tiles)
    def _():
        _proj_tile(xp_ref, *outs)

    @pl.when(pl.program_id(0) >= prompt_tiles)
    def _():
        _proj_tile(xs_ref, *outs)


def _proj_tile(x_ref, g_ref, b_ref, w_ref, wqt_ref, wvut_ref, qt_ref, vt_ref, ut_ref, k_ref, v_ref):
    rows_per_part = ROW_GROUP // PROJ_PARTS
    nt = (((1,), (1,)), ((), ()))
    for part in range(PROJ_PARTS):
        tok = slice(part * rows_per_part * ROW_TOKENS, (part + 1) * rows_per_part * ROW_TOKENS)
        xb = _layer_norm(x_ref[tok, :], g_ref[...], b_ref[...]).astype(BF16)
        p = jnp.dot(xb, w_ref[...], preferred_element_type=F32)
        k_ref[tok, :] = p[:, :PROJ_KV]
        v_ref[tok, :] = p[:, PROJ_KV:]
        qt = lax.dot_general(wqt_ref[...], xb, nt, preferred_element_type=F32)
        vut = lax.dot_general(wvut_ref[...], xb, nt, preferred_element_type=F32)
        qt_ref[:, tok] = (qt * (HEAD_DIM ** -0.5)).astype(BF16)
        vt_ref[:, tok] = vut[:PROJ_KV].astype(BF16)
        for j in range(rows_per_part):
            ut_ref[pl.ds(part * rows_per_part + j, SSM_WIDTH, stride=ROW_GROUP), :] = (
                vut[PROJ_KV:, j * ROW_TOKENS:(j + 1) * ROW_TOKENS])


def _proj(xp2d, xs2d, ln_g, ln_b, w_in, w_in_t):
    assert D_IN_PROJ == 2 * VU and PROJ_Q % KV == 0
    tm = PROJ_TILE
    tp, ts = xp2d.shape[0] // tm, xs2d.shape[0] // tm
    n = (tp + ts) * tm
    const = lambda i: (0, 0)
    row = lambda i: (i, 0)
    col = lambda i: (0, i)
    return pl.pallas_call(
        functools.partial(_proj_kernel, prompt_tiles=tp),
        grid=(tp + ts,),
        in_specs=[
            pl.BlockSpec((tm, D_MODEL), lambda i: (jnp.minimum(i, tp - 1), 0)),
            pl.BlockSpec((tm, D_MODEL), lambda i: (jnp.maximum(i - tp, 0), 0)),
            pl.BlockSpec((1, D_MODEL), const),
            pl.BlockSpec((1, D_MODEL), const),
            pl.BlockSpec((D_MODEL, KV), lambda i: (0, PROJ_Q // KV)),
            pl.BlockSpec((PROJ_Q, D_MODEL), const),
            pl.BlockSpec((VU, D_MODEL), lambda i: (1, 0)),
        ],
        out_specs=[
            pl.BlockSpec((PROJ_Q, tm), col),
            pl.BlockSpec((PROJ_KV, tm), col),
            pl.BlockSpec((None, SSM_WIDTH * ROW_GROUP, ROW_TOKENS), lambda i: (i, 0, 0)),
            pl.BlockSpec((tm, PROJ_KV), row),
            pl.BlockSpec((tm, PROJ_KV), row),
        ],
        out_shape=[
            jax.ShapeDtypeStruct((PROJ_Q, n), BF16),
            jax.ShapeDtypeStruct((PROJ_KV, n), BF16),
            jax.ShapeDtypeStruct((n // tm, SSM_WIDTH * ROW_GROUP, ROW_TOKENS), F32),
            jax.ShapeDtypeStruct((n, PROJ_KV), F32),
            jax.ShapeDtypeStruct((n, PROJ_KV), F32),
        ],
        compiler_params=pltpu.CompilerParams(
            dimension_semantics=("arbitrary",), vmem_limit_bytes=V7X_VMEM_LIMIT_BYTES),
        name="proj",
    )(xp2d, xs2d, ln_g, ln_b, w_in, w_in_t, w_in_t)


PAIR = 2 * CHUNK
HEAD_LANES = Q_PER_KV * PAIR
ONES_ROWS = 16


def _scores(unit, qt_ref):
    kwin, _, lanes, _, _, h = unit
    base = h * Q_PER_KV * HEAD_DIM
    qrow = jnp.concatenate(
        [qt_ref[base + g * HEAD_DIM:base + (g + 1) * HEAD_DIM, lanes] for g in range(Q_PER_KV)],
        axis=1)
    zero = jnp.zeros_like(qrow)
    qstack = jnp.concatenate([qrow, zero] if h == 0 else [zero, qrow], axis=0)
    return jnp.dot(kwin, qstack, preferred_element_type=F32)


def _finish(unit, s, sink_ref, o_ref):
    _, vtwin, lanes, valid, masked_rows, h = unit
    nk = s.shape[0]
    base = h * Q_PER_KV * HEAD_DIM
    pieces, done = [], 0
    for start, stop in masked_rows:
        if start > done:
            pieces.append(s[done:start])
        pieces.append(jnp.where(valid[start:stop], s[start:stop], NEG_INF))
        done = stop
    if done < nk:
        pieces.append(s[done:])
    s = jnp.concatenate(pieces, axis=0)
    sink = sink_ref[h]
    m = jnp.maximum(jnp.max(s, axis=0, keepdims=True), sink)
    p = jnp.exp(s - m).astype(BF16)
    v_ones = jnp.concatenate(
        [vtwin[h * HEAD_DIM:(h + 1) * HEAD_DIM, :], jnp.ones((ONES_ROWS, nk), BF16)], axis=0)
    ov = jnp.dot(v_ones, p, preferred_element_type=F32)
    den = ov[HEAD_DIM:HEAD_DIM + 1, :] + jnp.exp(sink - m)
    o = ov[:HEAD_DIM, :] * (1.0 / den)
    for g in range(Q_PER_KV):
        o_ref[base + g * HEAD_DIM:base + (g + 1) * HEAD_DIM, lanes] = (
            o[:, g * PAIR:(g + 1) * PAIR].astype(BF16))


def _attend_units(units, qt_ref, sink_ref, o_ref):
    s_next = _scores(units[0], qt_ref)
    for i, unit in enumerate(units):
        s = s_next
        if i + 1 < len(units):
            s_next = _scores(units[i + 1], qt_ref)
        _finish(unit, s, sink_ref, o_ref)


def _attn_prompt_kernel(sink_ref, qt_ref, kp_ref, kc_ref, vtp_ref, vtc_ref, o_ref, *, n_pairs):
    tq = n_pairs * PAIR
    nk = WINDOW + PAIR
    kk = jnp.concatenate([kp_ref[...], kc_ref[...]], axis=0).astype(BF16)
    vt = jnp.concatenate([vtp_ref[...], vtc_ref[...]], axis=1)
    r = lax.broadcasted_iota(jnp.int32, (nk, HEAD_LANES), 0)
    first_chunk = (lax.broadcasted_iota(jnp.int32, (nk, HEAD_LANES), 1) & (PAIR - 1)) < CHUNK
    lo = jnp.where(first_chunk, 0, CHUNK)
    hi = jnp.where(first_chunk, BAND, nk)
    first_pos = pl.program_id(1) * tq - WINDOW
    units = []
    for pp in range(n_pairs):
        lo_pp = jnp.maximum(lo, -first_pos) if pp == 0 else lo
        valid = (r >= lo_pp) & (r < hi)
        masked_rows = ((0, nk),) if pp == 0 else ((0, CHUNK), (nk - CHUNK, nk))
        for h in range(N_KV_HEADS):
            units.append((kk[pp * PAIR:pp * PAIR + nk], vt[:, pp * PAIR:pp * PAIR + nk],
                          slice(pp * PAIR, (pp + 1) * PAIR), valid, masked_rows, h))
    _attend_units(units, qt_ref, sink_ref, o_ref)


def _attn_sample_kernel(sink_ref, qt_ref, ck_ref, kn_ref, cvt_ref, vtn_ref, o_ref, *, n_pairs):
    nk = 2 * WINDOW + PAIR
    r = lax.broadcasted_iota(jnp.int32, (nk, HEAD_LANES), 0)
    query_seq = (lax.broadcasted_iota(jnp.int32, (nk, HEAD_LANES), 1) & (PAIR - 1)) >> 6
    key_seq = jnp.where(r < 2 * WINDOW, r >> 7, (r - 2 * WINDOW) >> 6)
    valid = query_seq == key_seq
    units = []
    for pp in range(n_pairs):
        lanes = slice(pp * PAIR, (pp + 1) * PAIR)
        kwin = jnp.concatenate([ck_ref[2 * pp], ck_ref[2 * pp + 1], kn_ref[lanes, :]],
                               axis=0).astype(BF16)
        vtwin = jnp.concatenate([cvt_ref[:, 2 * pp * WINDOW:2 * (pp + 1) * WINDOW], vtn_ref[:, lanes]],
                                axis=1)
        units.extend((kwin, vtwin, lanes, valid, ((0, nk),), h) for h in range(N_KV_HEADS))
    _attend_units(units, qt_ref, sink_ref, o_ref)


_SINK_SPEC = pl.BlockSpec((N_KV_HEADS, 1, HEAD_LANES), lambda *_: (0, 0, 0))


def _attention_prompt(sink_rows, qt, k, vt, bsz, seq, n_pairs):
    tq = n_pairs * PAIR
    nt = seq // tq
    wpt = tq // WINDOW
    cur_c = lambda b, i: (0, b * nt + i)
    cur_r = lambda b, i: (b * nt + i, 0)
    prev = lambda b, i: jnp.maximum((b * nt + i) * wpt - 1, 0)
    return pl.pallas_call(
        functools.partial(_attn_prompt_kernel, n_pairs=n_pairs),
        grid=(bsz, nt),
        in_specs=[
            _SINK_SPEC,
            pl.BlockSpec((PROJ_Q, tq), cur_c),
            pl.BlockSpec((WINDOW, PROJ_KV), lambda b, i: (prev(b, i), 0)),
            pl.BlockSpec((tq, PROJ_KV), cur_r),
            pl.BlockSpec((PROJ_KV, WINDOW), lambda b, i: (0, prev(b, i))),
            pl.BlockSpec((PROJ_KV, tq), cur_c),
        ],
        out_specs=pl.BlockSpec((PROJ_Q, tq), cur_c),
        out_shape=jax.ShapeDtypeStruct((PROJ_Q, bsz * seq), BF16),
        compiler_params=pltpu.CompilerParams(dimension_semantics=("arbitrary", "arbitrary")),
        name="attn_prompt",
    )(sink_rows, qt, k, k, vt, vt)


def _attention_sample(sink_rows, qt, cache_k, k, cache_vt, vt, first_token, n_pairs):
    n = cache_k.shape[0] * CHUNK
    tq = n_pairs * PAIR
    assert first_token % tq == 0
    off = first_token // tq
    return pl.pallas_call(
        functools.partial(_attn_sample_kernel, n_pairs=n_pairs),
        grid=(n // tq,),
        in_specs=[
            _SINK_SPEC,
            pl.BlockSpec((PROJ_Q, tq), lambda i: (0, i + off)),
            pl.BlockSpec((2 * n_pairs, WINDOW, PROJ_KV), lambda i: (i, 0, 0)),
            pl.BlockSpec((tq, PROJ_KV), lambda i: (i + off, 0)),
            pl.BlockSpec((PROJ_KV, 2 * n_pairs * WINDOW), lambda i: (0, i)),
            pl.BlockSpec((PROJ_KV, tq), lambda i: (0, i + off)),
        ],
        out_specs=pl.BlockSpec((PROJ_Q, tq), lambda i: (0, i)),
        out_shape=jax.ShapeDtypeStruct((PROJ_Q, n), BF16),
        compiler_params=pltpu.CompilerParams(dimension_semantics=("arbitrary",)),
        name="attn_sample",
    )(sink_rows, qt, cache_k, k, cache_vt, vt)


def _power_table(tau, nbits, a_re, a_im):
    rows = tau.shape[0]
    w_re = jnp.ones((rows, STATE_LANES), F32)
    w_im = jnp.zeros((rows, STATE_LANES), F32)
    p_re, p_im = a_re, a_im
    for k in range(nbits):
        bit = ((tau >> k) & 1) == 1
        f_re = jnp.where(bit, p_re, 1.0)
        f_im = jnp.where(bit, p_im, 0.0)
        w_re, w_im = w_re * f_re - w_im * f_im, w_re * f_im + w_im * f_re
        p_re, p_im = p_re * p_re - p_im * p_im, 2.0 * p_re * p_im
    return w_re, w_im


def _build_tables(par_ref, p_ref, qt_ref, coef_ref):
    lo = lax.broadcasted_iota(jnp.int32, (1, STATE_LANES), 1) < SSM_STATE
    mat = lambda i: par_ref[PAR_MATS + i * SSM_GROUP:PAR_MATS + (i + 1) * SSM_GROUP, :]
    lr, li = par_ref[0:1, :], par_ref[1:2, :]
    dt = jnp.exp(par_ref[2:3, :])
    mag = jnp.exp(lr * dt)
    a_re, a_im = mag * jnp.cos(li * dt), mag * jnp.sin(li * dt)
    nr, ni = a_re - 1.0, a_im
    den = lr * lr + li * li
    f_re, f_im = (nr * lr + ni * li) / den, (ni * lr - nr * li) / den
    b_re, b_im = mat(0), mat(1)
    bb_re = f_re * b_re - f_im * b_im
    bb_im = f_re * b_im + f_im * b_re
    c_re, c_im = mat(2), mat(3)

    tau = lax.broadcasted_iota(jnp.int32, (SSM_T, 1), 0)
    w_re, w_im = _power_table(tau, 6, a_re, a_im)
    w1_re, w1_im = w_re * a_re - w_im * a_im, w_re * a_im + w_im * a_re
    wr_re, wr_im = _power_table(SSM_T - 1 - tau, 6, a_re, a_im)

    def outer(c, w):
        return (c[:, None, :] * w[None, :, :]).reshape(SSM_CW, STATE_LANES)

    cw_mix = (outer(c_re, jnp.where(lo, w_re, w_im)) + outer(c_im, jnp.where(lo, -w_im, w_re)))
    bb_mix = jnp.where(lo, bb_re, -bb_im)
    strip = lax.dot_general(bb_mix, cw_mix, (((1,), (1,)), ((), ())),
                            precision=lax.Precision.HIGHEST,
                            preferred_element_type=F32)

    x_a, y_a = jnp.where(lo, bb_re, bb_im), jnp.where(lo, -bb_im, bb_re)
    p_ref[...] = (outer(x_a, wr_re) + outer(y_a, wr_im)).astype(BF16)

    qt_ref[...] = (outer(c_re, jnp.where(lo, w1_re, -w1_im))
                   + outer(c_im, jnp.where(lo, -w1_im, -w1_re))).astype(BF16)

    t_re, t_im = a_re, a_im
    for _ in range(6):
        t_re, t_im = t_re * t_re - t_im * t_im, 2.0 * t_re * t_im
    for k in range(COEF_ROWS // 2):
        coef_ref[2 * k:2 * k + 1, :] = t_re
        coef_ref[2 * k + 1:2 * k + 2, :] = jnp.where(lo, -t_im, t_im)
        t_re, t_im = t_re * t_re - t_im * t_im, 2.0 * t_re * t_im
    return strip


def _toeplitz_rows(strip, m_ref, channels):
    s_idx = lax.broadcasted_iota(jnp.int32, (SSM_T, SSM_CW), 0)
    t_idx = lax.broadcasted_iota(jnp.int32, (SSM_T, SSM_CW), 1) & (SSM_T - 1)
    causal = t_idx >= s_idx
    for c in channels:
        rows = jnp.broadcast_to(strip[c:c + 1, :], (SSM_T, SSM_CW))
        shifted = pltpu.roll(rows, 0, 1, stride=1, stride_axis=0)
        m_ref[c * SSM_T:(c + 1) * SSM_T, :] = jnp.where(causal, shifted, 0.0).astype(BF16)


PAR_MATS = 8
PAR_ROWS = PAR_MATS + 4 * SSM_GROUP


def _table_params(lam_re, lam_im, log_step, b_re, b_im, c_re, c_im):
    step = jnp.broadcast_to(log_step[:, None, None], (SSM_GROUPS, 1, SSM_STATE))
    filler = jnp.zeros((SSM_GROUPS, PAR_MATS - 3, SSM_STATE), F32)
    pack = jnp.concatenate([lam_re[:, None, :], lam_im[:, None, :], step, filler,
                            jnp.swapaxes(b_re, 1, 2), jnp.swapaxes(b_im, 1, 2), c_re, c_im], axis=1)
    return jnp.concatenate([pack, pack], axis=-1)


def _cmul(a1, a2, h, hs):
    return a1 * h + a2 * hs, a1 * hs - a2 * h


def _ssm_kernel(par_ref, u_ref, h0_ref, y_ref, hfin_ref,
                m_ref, p_ref, qt_ref, coef_ref, *, rows_per_seq, prompt_rows):
    strip = _build_tables(par_ref, p_ref, qt_ref, coef_ref)
    lo = lax.broadcasted_iota(jnp.int32, (1, ROW_TOKENS), 1) < SSM_T

    def chunk_rows(ref):
        even, odd = [], []
        rows = ref.shape[0] * ROW_GROUP
        for k in range(SSM_GROUP // 2):
            a = ref[:, 2 * k].reshape(rows, ROW_TOKENS)
            b = ref[:, 2 * k + 1].reshape(rows, ROW_TOKENS)
            even.append(jnp.where(lo, a, pltpu.roll(b, SSM_T, 1)))
            odd.append(jnp.where(lo, pltpu.roll(a, SSM_T, 1), b))
        return jnp.concatenate(even, axis=1), jnp.concatenate(odd, axis=1)

    def store_rows(y_even, y_odd, ref):
        for k in range(SSM_GROUP // 2):
            te = y_even[:, k * ROW_TOKENS:(k + 1) * ROW_TOKENS]
            to = y_odd[:, k * ROW_TOKENS:(k + 1) * ROW_TOKENS]
            tiles = (ref.shape[0], ROW_GROUP, ROW_TOKENS)
            ref[:, 2 * k] = jnp.where(lo, te, pltpu.roll(to, SSM_T, 1)).reshape(tiles)
            ref[:, 2 * k + 1] = jnp.where(lo, pltpu.roll(te, SSM_T, 1), to).reshape(tiles)

    even, odd = chunk_rows(u_ref)
    rp, rs = prompt_rows, even.shape[0] - prompt_rows
    u = jnp.concatenate([even[:rp], odd[:rp], even[rp:], odd[rp:]], axis=0).astype(BF16)
    s1 = jnp.dot(u, p_ref[...], preferred_element_type=F32)
    s2 = pltpu.roll(s1, SSM_STATE, 1)
    a1, a2 = coef_ref[0:1, :], coef_ref[1:2, :]

    e1, e2, o1, o2 = s1[:rp], s2[:rp], s1[rp:2 * rp], s2[rp:2 * rp]
    x1, x2 = _cmul(a1, a2, e1, e2)
    x1, x2 = x1 + o1, x2 + o2
    pos = lax.broadcasted_iota(jnp.int32, (rp, STATE_LANES), 0) & (rows_per_seq - 1)
    for k in range(rows_per_seq.bit_length() - 1):
        d = 1 << k
        b1, b2 = coef_ref[2 + 2 * k:3 + 2 * k, :], coef_ref[3 + 2 * k:4 + 2 * k, :]
        sh1 = jnp.where(pos >= d, pltpu.roll(x1, d, 0), 0.0)
        sh2 = jnp.where(pos >= d, pltpu.roll(x2, d, 0), 0.0)
        y1, y2 = _cmul(b1, b2, sh1, sh2)
        x1, x2 = x1 + y1, x2 + y2
    g1 = jnp.where(pos >= 1, pltpu.roll(x1, 1, 0), 0.0)
    g2 = jnp.where(pos >= 1, pltpu.roll(x2, 1, 0), 0.0)
    ho1 = _cmul(a1, a2, g1, g2)[0] + e1
    hfin_ref[0:STATE_OUT_ROWS, :] = jnp.zeros((STATE_OUT_ROWS, STATE_LANES), F32)
    for b in range(rp // rows_per_seq):
        last = (b + 1) * rows_per_seq - 1
        hfin_ref[b:b + 1, :] = x1[last:last + 1, :]

    h0e, h0o = h0_ref[0], h0_ref[1]
    swap = lambda h: pltpu.roll(h, SSM_STATE, 1)
    hfin_ref[STATE_OUT_ROWS:STATE_OUT_ROWS + rs, :] = a1 * h0e + a2 * swap(h0e) + s1[2 * rp:2 * rp + rs]
    hfin_ref[STATE_OUT_ROWS + rs:, :] = a1 * h0o + a2 * swap(h0o) + s1[2 * rp + rs:]

    hprev = jnp.concatenate([g1, ho1, h0e, h0o], axis=0).astype(BF16)
    y = lax.dot_general(hprev, qt_ref[...], (((1,), (1,)), ((), ())), preferred_element_type=F32)
    blocks = [range(c, c + M_BLOCK_CHANNELS) for c in range(0, SSM_GROUP, M_BLOCK_CHANNELS)]
    _toeplitz_rows(strip, m_ref, blocks[0])
    for i, channels in enumerate(blocks):
        rows = slice(channels[0] * SSM_T, (channels[-1] + 1) * SSM_T)
        y += jnp.dot(u[:, rows], m_ref[rows, :], preferred_element_type=F32)
        if i + 1 < len(blocks):
            _toeplitz_rows(strip, m_ref, blocks[i + 1])
    store_rows(jnp.concatenate([y[:rp], y[2 * rp:2 * rp + rs]], axis=0),
               jnp.concatenate([y[rp:2 * rp], y[2 * rp + rs:]], axis=0), y_ref)


def _ssm(table_params, u3, h0_pack, *, rows_per_seq, prompt_rows):
    groups = u3.shape[0]
    rs = groups * ROW_GROUP - prompt_rows
    u4 = u3.reshape(groups, SSM_WIDTH, ROW_GROUP, ROW_TOKENS)
    assert rows_per_seq & (rows_per_seq - 1) == 0 and 2 * rows_per_seq.bit_length() <= COEF_ROWS
    g3 = lambda g: (g, 0, 0)
    u_spec = pl.BlockSpec((groups, SSM_GROUP, ROW_GROUP, ROW_TOKENS), lambda g: (0, g, 0, 0))
    fin_rows = STATE_OUT_ROWS + 2 * rs
    return pl.pallas_call(
        functools.partial(_ssm_kernel, rows_per_seq=rows_per_seq, prompt_rows=prompt_rows),
        grid=(SSM_GROUPS,),
        in_specs=[
            pl.BlockSpec((None, PAR_ROWS, STATE_LANES), g3),
            u_spec,
            pl.BlockSpec((None, 2, rs, STATE_LANES), lambda g: (g, 0, 0, 0)),
        ],
        out_specs=[
            u_spec,
            pl.BlockSpec((None, fin_rows, STATE_LANES), g3),
        ],
        out_shape=[
            jax.ShapeDtypeStruct(u4.shape, F32),
            jax.ShapeDtypeStruct((SSM_GROUPS, fin_rows, STATE_LANES), F32),
        ],
        scratch_shapes=[
            pltpu.VMEM((SSM_CW, SSM_CW), BF16),
            pltpu.VMEM((SSM_CW, STATE_LANES), BF16),
            pltpu.VMEM((SSM_CW, STATE_LANES), BF16),
            pltpu.VMEM((COEF_ROWS, STATE_LANES), F32),
        ],
        compiler_params=pltpu.CompilerParams(dimension_semantics=("arbitrary",)),
        name="ssm",
    )(table_params, u4, h0_pack)


def _post_kernel(xp_ref, xs_ref, atp_ref, ats_ref, ut_ref, yt_ref, *refs, alpha, prompt_tiles):
    weights, (op_ref, os_ref) = refs[:-2], refs[-2:]

    @pl.when(pl.program_id(0) < prompt_tiles)
    def _():
        _post_tile(xp_ref, atp_ref, ut_ref, yt_ref, *weights, op_ref, alpha=alpha)

    @pl.when(pl.program_id(0) >= prompt_tiles)
    def _():
        _post_tile(xs_ref, ats_ref, ut_ref, yt_ref, *weights, os_ref, alpha=alpha)


def _post_tile(x_ref, at_ref, ut_ref, yt_ref, lng_ref, lnb_ref, d_ref, wglut_ref, bglu_ref,
               wout_ref, ln1g_ref, ln1b_ref, wgu_ref, wdown_ref, ln2g_ref, ln2b_ref, o_ref,
               *, alpha):
    xn = _layer_norm(x_ref[...], lng_ref[...], lnb_ref[...])
    n_rows = x_ref.shape[0] // ROW_TOKENS
    first_row = (pl.program_id(0) % (ROW_GROUP // n_rows)) * n_rows

    def feature_major(ref):
        return jnp.concatenate(
            [ref[pl.ds(first_row + j, SSM_WIDTH, stride=ROW_GROUP), :] for j in range(n_rows)], axis=1)

    ys = feature_major(yt_ref) + d_ref[...] * feature_major(ut_ref)
    gl = 0.5 * ys * (1.0 + lax.erf(ys * math.sqrt(0.5)))
    z = jnp.dot(wglut_ref[...], gl.astype(BF16), preferred_element_type=F32) + bglu_ref[...]
    s = gl * jax.nn.sigmoid(z)
    tn = (((0,), (0,)), ((), ()))
    mix = lax.dot_general(at_ref[...], wout_ref[:PROJ_Q, :], tn, preferred_element_type=F32)
    mix += lax.dot_general(s.astype(BF16), wout_ref[PROJ_Q:, :], tn, preferred_element_type=F32)
    h = _layer_norm(alpha * xn + mix, ln1g_ref[...], ln1b_ref[...])
    hb = h.astype(BF16)
    f = jnp.zeros_like(h)
    for start in range(0, D_FF, FF_BLOCK):
        stop = min(start + FF_BLOCK, D_FF)
        cols = slice(start, stop)
        up_cols = slice(D_FF + start, D_FF + stop)
        g = jnp.dot(hb, wgu_ref[:, cols], preferred_element_type=F32)
        up = jnp.dot(hb, wgu_ref[:, up_cols], preferred_element_type=F32)
        act = (g * jax.nn.sigmoid(g)) * up
        f += jnp.dot(act.astype(BF16), wdown_ref[cols, :], preferred_element_type=F32)
    o_ref[...] = _layer_norm(alpha * h + f, ln2g_ref[...], ln2b_ref[...])


def _post(xp2d, xs2d, atp, ats, ut3, yt3, ln_g, ln_b, d_col, w_glu_t, b_glu_col, w_out, ln1_g, ln1_b,
          w_gate_up, w_down, ln2_g, ln2_b, *, alpha, tm):
    n_p, n_s = xp2d.shape[0], xs2d.shape[0]
    tp, ts = n_p // tm, n_s // tm
    assert PROJ_TILE % tm == 0 and n_p % PROJ_TILE == 0
    group = lambda i: (i // (PROJ_TILE // tm), 0, 0)
    group_spec = pl.BlockSpec((None, SSM_WIDTH * ROW_GROUP, ROW_TOKENS), group)
    prompt = lambda i: jnp.minimum(i, tp - 1)
    sample = lambda i: jnp.maximum(i - tp, 0)
    const = lambda i: (0, 0)
    resident = lambda shape: pl.BlockSpec(shape, const, pipeline_mode=pl.Buffered(1))
    vec = lambda width: pl.BlockSpec((1, width), const)
    colvec = pl.BlockSpec((SSM_WIDTH, 1), const)
    return pl.pallas_call(
        functools.partial(_post_kernel, alpha=alpha, prompt_tiles=tp),
        grid=(tp + ts,),
        in_specs=[
            pl.BlockSpec((tm, D_MODEL), lambda i: (prompt(i), 0)),
            pl.BlockSpec((tm, D_MODEL), lambda i: (sample(i), 0)),
            pl.BlockSpec((PROJ_Q, tm), lambda i: (0, prompt(i))),
            pl.BlockSpec((PROJ_Q, tm), lambda i: (0, sample(i))),
            group_spec, group_spec,
            vec(D_MODEL), vec(D_MODEL), colvec,
            resident((SSM_WIDTH, SSM_WIDTH)), colvec,
            resident((D_MODEL, D_MODEL)), vec(D_MODEL), vec(D_MODEL),
            resident((D_MODEL, 2 * D_FF)), resident((D_FF, D_MODEL)),
            vec(D_MODEL), vec(D_MODEL),
        ],
        out_specs=[
            pl.BlockSpec((tm, D_MODEL), lambda i: (prompt(i), 0)),
            pl.BlockSpec((tm, D_MODEL), lambda i: (sample(i), 0)),
        ],
        out_shape=[
            jax.ShapeDtypeStruct((n_p, D_MODEL), F32),
            jax.ShapeDtypeStruct((n_s, D_MODEL), F32),
        ],
        compiler_params=pltpu.CompilerParams(
            dimension_semantics=("arbitrary",), vmem_limit_bytes=V7X_VMEM_LIMIT_BYTES),
        name="post",
    )(xp2d, xs2d, atp, ats, ut3, yt3, ln_g, ln_b, d_col, w_glu_t, b_glu_col, w_out, ln1_g, ln1_b,
      w_gate_up, w_down, ln2_g, ln2_b)


def kernel(x_prompt, x_sample, cache_win_k, cache_win_v, state_ssm_re, state_ssm_im,
           ln_in_g, ln_in_b, w_in, attn_sinks, ssm_lambda_re, ssm_lambda_im, ssm_log_step,
           ssm_b_re, ssm_b_im, ssm_c_re, ssm_c_im, ssm_d, w_glu, b_glu, w_out,
           ln1_g, ln1_b, w_gate_up, w_down, ln2_g, ln2_b):
    depth = w_in.shape[0]
    assert depth == 1, "single-layer step"
    bp, lp, _ = x_prompt.shape
    bs, ls, _ = x_sample.shape
    assert ls == SSM_T and bs % 2 == 0 and lp % 512 == 0 and bp <= STATE_OUT_ROWS
    win_rows = cache_win_k.shape[2]
    assert win_rows == WINDOW
    alpha = (2.0 * depth) ** 0.25
    l = 0
    row = lambda a: a.reshape(1, -1)
    column = lambda a: a.reshape(-1, 1)

    lng, lnb = row(ln_in_g), row(ln_in_b)
    xp2 = x_prompt.reshape(bp * lp, D_MODEL)
    xs2 = x_sample.reshape(bs * ls, D_MODEL)
    n_p = bp * lp
    qt, vt, u3, k, v = _proj(xp2, xs2, lng, lnb, w_in[l].astype(BF16), w_in[l].T.astype(BF16))

    sink_rows = jnp.repeat(attn_sinks[l].reshape(N_KV_HEADS, Q_PER_KV), PAIR, axis=1)[:, None, :]
    atp = _attention_prompt(sink_rows, qt, k, vt, bp, lp, n_pairs=4)
    ck = cache_win_k[l].reshape(bs, win_rows, PROJ_KV)
    cv = cache_win_v[l].reshape(bs, win_rows, PROJ_KV)
    cvt = jnp.transpose(cv, (2, 0, 1)).reshape(PROJ_KV, bs * win_rows).astype(BF16)
    ats = _attention_sample(sink_rows, qt, ck, k, cvt, vt, first_token=n_p, n_pairs=4)
    ks3, vs3 = k[n_p:].reshape(bs, ls, PROJ_KV), v[n_p:].reshape(bs, ls, PROJ_KV)

    table_params = _table_params(
        ssm_lambda_re[l], ssm_lambda_im[l], ssm_log_step[l],
        ssm_b_re[l], ssm_b_im[l], ssm_c_re[l], ssm_c_im[l])
    h0 = jnp.concatenate([state_ssm_re[l], state_ssm_im[l]], axis=-1)
    h0_pack = jnp.transpose(h0.reshape(bs // 2, 2, SSM_GROUPS, STATE_LANES), (2, 1, 0, 3))
    y4, hfin = _ssm(table_params, u3, h0_pack,
                    rows_per_seq=lp // ROW_TOKENS, prompt_rows=n_p // ROW_TOKENS)

    post_args = (lng, lnb, column(ssm_d[l]), w_glu[l].T.astype(BF16), column(b_glu[l]),
                 w_out[l].astype(BF16), row(ln1_g[l]), row(ln1_b[l]),
                 w_gate_up[l].astype(BF16), w_down[l].astype(BF16), row(ln2_g[l]), row(ln2_b[l]))
    out_p, out_s = _post(xp2, xs2, atp, ats, u3, y4.reshape(u3.shape), *post_args, alpha=alpha, tm=512)

    kv_shape = (N_KV_HEADS, HEAD_DIM)
    last_rows = lambda a: jnp.stack([a[(b + 1) * lp - win_rows:(b + 1) * lp] for b in range(bp)])
    win_k_p = last_rows(k).reshape(1, bp, win_rows, *kv_shape)
    win_v_p = last_rows(v).reshape(1, bp, win_rows, *kv_shape)
    win_k_s = jnp.concatenate([ck, ks3], axis=1)[:, -win_rows:].reshape(1, bs, win_rows, *kv_shape)
    win_v_s = jnp.concatenate([cv, vs3], axis=1)[:, -win_rows:].reshape(1, bs, win_rows, *kv_shape)
    sp = jnp.swapaxes(hfin[:, :bp], 0, 1)
    ss = jnp.transpose(hfin[:, STATE_OUT_ROWS:].reshape(SSM_GROUPS, 2, bs // 2, STATE_LANES),
                       (2, 1, 0, 3)).reshape(bs, SSM_GROUPS, STATE_LANES)
    return (out_p.reshape(bp, lp, D_MODEL), out_s.reshape(bs, ls, D_MODEL),
            win_k_p, win_v_p, sp[None, ..., :SSM_STATE], sp[None, ..., SSM_STATE:],
            win_k_s, win_v_s, ss[None, ..., :SSM_STATE], ss[None, ..., SSM_STATE:])
```

```python
import functools
import math

import jax
import jax.numpy as jnp
from jax import lax
from jax.experimental import pallas as pl
from jax.experimental.pallas import tpu as pltpu

F32 = jnp.float32
BF16 = jnp.bfloat16

D_MODEL = 1024
HEAD_DIM = 64
N_HEADS = 8
N_KV_HEADS = 2
Q_PER_KV = N_HEADS // N_KV_HEADS
CHUNK = 64
WINDOW = 128
WIN_CHUNKS = WINDOW // CHUNK
BAND = (WIN_CHUNKS + 1) * CHUNK
PROJ_Q = N_HEADS * HEAD_DIM
PROJ_KV = N_KV_HEADS * HEAD_DIM
SSM_WIDTH = 512
SSM_GROUP = 16
SSM_GROUPS = SSM_WIDTH // SSM_GROUP
SSM_STATE = 64
STATE_LANES = 2 * SSM_STATE
D_FF = 2816
D_IN_PROJ = PROJ_Q + 2 * PROJ_KV + SSM_WIDTH
LN_EPS = 1e-5
NEG_INF = -1e30

SSM_T = CHUNK
SSM_CW = SSM_T * SSM_GROUP
ROW_TOKENS = 2 * SSM_T
ROW_GROUP = 8
PROJ_TILE = ROW_GROUP * ROW_TOKENS
PROJ_PARTS = 2
COEF_ROWS = 16
STATE_OUT_ROWS = 8
M_BLOCK_CHANNELS = 4
FF_BLOCK = 256
V7X_VMEM_LIMIT_BYTES = 56 * 1024 * 1024


def _layer_norm(x, g, b):
    mu = jnp.mean(x, axis=-1, keepdims=True)
    xc = x - mu
    var = jnp.mean(xc * xc, axis=-1, keepdims=True)
    return xc * lax.rsqrt(var + LN_EPS) * g + b


KV = 2 * PROJ_KV
VU = PROJ_KV + SSM_WIDTH


def _proj_kernel(xp_ref, xs_ref, g_ref, b_ref, w_ref, wqt_ref, wvut_ref,
                 xn_ref, qt_ref, vt_ref, ut_ref, k_ref, v_ref, *, prompt_tiles):
    outs = (g_ref, b_ref, w_ref, wqt_ref, wvut_ref, xn_ref, qt_ref, vt_ref, ut_ref, k_ref, v_ref)

    @pl.when(pl.program_id(0) < prompt_tiles)
    def _():
        _proj_tile(xp_ref, *outs)

    @pl.when(pl.program_id(0) >= prompt_tiles)
    def _():
        _proj_tile(xs_ref, *outs)


def _proj_tile(x_ref, g_ref, b_ref, w_ref, wqt_ref, wvut_ref, xn_ref, qt_ref, vt_ref, ut_ref, k_ref, v_ref):
    rows_per_part = ROW_GROUP // PROJ_PARTS
    nt = (((1,), (1,)), ((), ()))
    for part in range(PROJ_PARTS):
        tok = slice(part * rows_per_part * ROW_TOKENS, (part + 1) * rows_per_part * ROW_TOKENS)
        xn = _layer_norm(x_ref[tok, :], g_ref[...], b_ref[...])
        xn_ref[tok, :] = xn
        xb = xn.astype(BF16)
        p = jnp.dot(xb, w_ref[...], preferred_element_type=F32)
        k_ref[tok, :] = p[:, :PROJ_KV]
        v_ref[tok, :] = p[:, PROJ_KV:]
        qt = lax.dot_general(wqt_ref[...], xb, nt, preferred_element_type=F32)
        vut = lax.dot_general(wvut_ref[...], xb, nt, preferred_element_type=F32)
        qt_ref[:, tok] = (qt * (HEAD_DIM ** -0.5)).astype(BF16)
        vt_ref[:, tok] = vut[:PROJ_KV].astype(BF16)
        for j in range(rows_per_part):
            ut_ref[pl.ds(part * rows_per_part + j, SSM_WIDTH, stride=ROW_GROUP), :] = (
                vut[PROJ_KV:, j * ROW_TOKENS:(j + 1) * ROW_TOKENS])


def _proj(xp2d, xs2d, ln_g, ln_b, w_in, w_in_t):
    assert D_IN_PROJ == 2 * VU and PROJ_Q % KV == 0
    tm = PROJ_TILE
    tp, ts = xp2d.shape[0] // tm, xs2d.shape[0] // tm
    n = (tp + ts) * tm
    const = lambda i: (0, 0)
    row = lambda i: (i, 0)
    col = lambda i: (0, i)
    return pl.pallas_call(
        functools.partial(_proj_kernel, prompt_tiles=tp),
        grid=(tp + ts,),
        in_specs=[
            pl.BlockSpec((tm, D_MODEL), lambda i: (jnp.minimum(i, tp - 1), 0)),
            pl.BlockSpec((tm, D_MODEL), lambda i: (jnp.maximum(i - tp, 0), 0)),
            pl.BlockSpec((1, D_MODEL), const),
            pl.BlockSpec((1, D_MODEL), const),
            pl.BlockSpec((D_MODEL, KV), lambda i: (0, PROJ_Q // KV)),
            pl.BlockSpec((PROJ_Q, D_MODEL), const),
            pl.BlockSpec((VU, D_MODEL), lambda i: (1, 0)),
        ],
        out_specs=[
            pl.BlockSpec((tm, D_MODEL), row),
            pl.BlockSpec((PROJ_Q, tm), col),
            pl.BlockSpec((PROJ_KV, tm), col),
            pl.BlockSpec((None, SSM_WIDTH * ROW_GROUP, ROW_TOKENS), lambda i: (i, 0, 0)),
            pl.BlockSpec((tm, PROJ_KV), row),
            pl.BlockSpec((tm, PROJ_KV), row),
        ],
        out_shape=[
            jax.ShapeDtypeStruct((n, D_MODEL), F32),
            jax.ShapeDtypeStruct((PROJ_Q, n), BF16),
            jax.ShapeDtypeStruct((PROJ_KV, n), BF16),
            jax.ShapeDtypeStruct((n // tm, SSM_WIDTH * ROW_GROUP, ROW_TOKENS), F32),
            jax.ShapeDtypeStruct((n, PROJ_KV), F32),
            jax.ShapeDtypeStruct((n, PROJ_KV), F32),
        ],
        compiler_params=pltpu.CompilerParams(
            dimension_semantics=("arbitrary",), vmem_limit_bytes=V7X_VMEM_LIMIT_BYTES),
        name="proj",
    )(xp2d, xs2d, ln_g, ln_b, w_in, w_in_t, w_in_t)


PAIR = 2 * CHUNK
HEAD_LANES = Q_PER_KV * PAIR
ONES_ROWS = 16


def _scores(unit, qt_ref):
    kwin, _, lanes, _, _, h = unit
    base = h * Q_PER_KV * HEAD_DIM
    qrow = jnp.concatenate(
        [qt_ref[base + g * HEAD_DIM:base + (g + 1) * HEAD_DIM, lanes] for g in range(Q_PER_KV)],
        axis=1)
    zero = jnp.zeros_like(qrow)
    qstack = jnp.concatenate([qrow, zero] if h == 0 else [zero, qrow], axis=0)
    return jnp.dot(kwin, qstack, preferred_element_type=F32)


def _finish(unit, s, sink_ref, o_ref):
    _, vtwin, lanes, valid, masked_rows, h = unit
    nk = s.shape[0]
    base = h * Q_PER_KV * HEAD_DIM
    pieces, done = [], 0
    for start, stop in masked_rows:
        if start > done:
            pieces.append(s[done:start])
        pieces.append(jnp.where(valid[start:stop], s[start:stop], NEG_INF))
        done = stop
    if done < nk:
        pieces.append(s[done:])
    s = jnp.concatenate(pieces, axis=0)
    sink = sink_ref[h]
    m = jnp.maximum(jnp.max(s, axis=0, keepdims=True), sink)
    p = jnp.exp(s - m).astype(BF16)
    v_ones = jnp.concatenate(
        [vtwin[h * HEAD_DIM:(h + 1) * HEAD_DIM, :], jnp.ones((ONES_ROWS, nk), BF16)], axis=0)
    ov = jnp.dot(v_ones, p, preferred_element_type=F32)
    den = ov[HEAD_DIM:HEAD_DIM + 1, :] + jnp.exp(sink - m)
    o = ov[:HEAD_DIM, :] * (1.0 / den)
    for g in range(Q_PER_KV):
        o_ref[base + g * HEAD_DIM:base + (g + 1) * HEAD_DIM, lanes] = (
            o[:, g * PAIR:(g + 1) * PAIR].astype(BF16))


def _attend_units(units, qt_ref, sink_ref, o_ref):
    s_next = _scores(units[0], qt_ref)
    for i, unit in enumerate(units):
        s = s_next
        if i + 1 < len(units):
            s_next = _scores(units[i + 1], qt_ref)
        _finish(unit, s, sink_ref, o_ref)


def _attn_prompt_kernel(sink_ref, qt_ref, kp_ref, kc_ref, vtp_ref, vtc_ref, o_ref, *, n_pairs):
    tq = n_pairs * PAIR
    nk = WINDOW + PAIR
    kk = jnp.concatenate([kp_ref[...], kc_ref[...]], axis=0).astype(BF16)
    vt = jnp.concatenate([vtp_ref[...], vtc_ref[...]], axis=1)
    r = lax.broadcasted_iota(jnp.int32, (nk, HEAD_LANES), 0)
    first_chunk = (lax.broadcasted_iota(jnp.int32, (nk, HEAD_LANES), 1) & (PAIR - 1)) < CHUNK
    lo = jnp.where(first_chunk, 0, CHUNK)
    hi = jnp.where(first_chunk, BAND, nk)
    first_pos = pl.program_id(1) * tq - WINDOW
    units = []
    for pp in range(n_pairs):
        lo_pp = jnp.maximum(lo, -first_pos) if pp == 0 else lo
        valid = (r >= lo_pp) & (r < hi)
        masked_rows = ((0, nk),) if pp == 0 else ((0, CHUNK), (nk - CHUNK, nk))
        for h in range(N_KV_HEADS):
            units.append((kk[pp * PAIR:pp * PAIR + nk], vt[:, pp * PAIR:pp * PAIR + nk],
                          slice(pp * PAIR, (pp + 1) * PAIR), valid, masked_rows, h))
    _attend_units(units, qt_ref, sink_ref, o_ref)


def _attn_sample_kernel(sink_ref, qt_ref, ck_ref, kn_ref, cvt_ref, vtn_ref, o_ref, *, n_pairs):
    nk = 2 * WINDOW + PAIR
    r = lax.broadcasted_iota(jnp.int32, (nk, HEAD_LANES), 0)
    query_seq = (lax.broadcasted_iota(jnp.int32, (nk, HEAD_LANES), 1) & (PAIR - 1)) >> 6
    key_seq = jnp.where(r < 2 * WINDOW, r >> 7, (r - 2 * WINDOW) >> 6)
    valid = query_seq == key_seq
    units = []
    for pp in range(n_pairs):
        lanes = slice(pp * PAIR, (pp + 1) * PAIR)
        kwin = jnp.concatenate([ck_ref[2 * pp], ck_ref[2 * pp + 1], kn_ref[lanes, :]],
                               axis=0).astype(BF16)
        vtwin = jnp.concatenate([cvt_ref[:, 2 * pp * WINDOW:2 * (pp + 1) * WINDOW], vtn_ref[:, lanes]],
                                axis=1)
        units.extend((kwin, vtwin, lanes, valid, ((0, nk),), h) for h in range(N_KV_HEADS))
    _attend_units(units, qt_ref, sink_ref, o_ref)


_SINK_SPEC = pl.BlockSpec((N_KV_HEADS, 1, HEAD_LANES), lambda *_: (0, 0, 0))


def _attention_prompt(sink_rows, qt, k, vt, bsz, seq, n_pairs):
    tq = n_pairs * PAIR
    nt = seq // tq
    wpt = tq // WINDOW
    cur_c = lambda b, i: (0, b * nt + i)
    cur_r = lambda b, i: (b * nt + i, 0)
    prev = lambda b, i: jnp.maximum((b * nt + i) * wpt - 1, 0)
    return pl.pallas_call(
        functools.partial(_attn_prompt_kernel, n_pairs=n_pairs),
        grid=(bsz, nt),
        in_specs=[
            _SINK_SPEC,
            pl.BlockSpec((PROJ_Q, tq), cur_c),
            pl.BlockSpec((WINDOW, PROJ_KV), lambda b, i: (prev(b, i), 0)),
            pl.BlockSpec((tq, PROJ_KV), cur_r),
            pl.BlockSpec((PROJ_KV, WINDOW), lambda b, i: (0, prev(b, i))),
            pl.BlockSpec((PROJ_KV, tq), cur_c),
        ],
        out_specs=pl.BlockSpec((PROJ_Q, tq), cur_c),
        out_shape=jax.ShapeDtypeStruct((PROJ_Q, bsz * seq), BF16),
        compiler_params=pltpu.CompilerParams(dimension_semantics=("arbitrary", "arbitrary")),
        name="attn_prompt",
    )(sink_rows, qt, k, k, vt, vt)


def _attention_sample(sink_rows, qt, cache_k, k, cache_vt, vt, first_token, n_pairs):
    n = cache_k.shape[0] * CHUNK
    tq = n_pairs * PAIR
    assert first_token % tq == 0
    off = first_token // tq
    return pl.pallas_call(
        functools.partial(_attn_sample_kernel, n_pairs=n_pairs),
        grid=(n // tq,),
        in_specs=[
            _SINK_SPEC,
            pl.BlockSpec((PROJ_Q, tq), lambda i: (0, i + off)),
            pl.BlockSpec((2 * n_pairs, WINDOW, PROJ_KV), lambda i: (i, 0, 0)),
            pl.BlockSpec((tq, PROJ_KV), lambda i: (i + off, 0)),
            pl.BlockSpec((PROJ_KV, 2 * n_pairs * WINDOW), lambda i: (0, i)),
            pl.BlockSpec((PROJ_KV, tq), lambda i: (0, i + off)),
        ],
        out_specs=pl.BlockSpec((PROJ_Q, tq), lambda i: (0, i)),
        out_shape=jax.ShapeDtypeStruct((PROJ_Q, n), BF16),
        compiler_params=pltpu.CompilerParams(dimension_semantics=("arbitrary",)),
        name="attn_sample",
    )(sink_rows, qt, cache_k, k, cache_vt, vt)


def _power_table(tau, nbits, a_re, a_im):
    rows = tau.shape[0]
    w_re = jnp.ones((rows, STATE_LANES), F32)
    w_im = jnp.zeros((rows, STATE_LANES), F32)
    p_re, p_im = a_re, a_im
    for k in range(nbits):
        bit = ((tau >> k) & 1) == 1
        f_re = jnp.where(bit, p_re, 1.0)
        f_im = jnp.where(bit, p_im, 0.0)
        w_re, w_im = w_re * f_re - w_im * f_im, w_re * f_im + w_im * f_re
        p_re, p_im = p_re * p_re - p_im * p_im, 2.0 * p_re * p_im
    return w_re, w_im


def _build_tables(par_ref, p_ref, qt_ref, coef_ref):
    lo = lax.broadcasted_iota(jnp.int32, (1, STATE_LANES), 1) < SSM_STATE
    mat = lambda i: par_ref[PAR_MATS + i * SSM_GROUP:PAR_MATS + (i + 1) * SSM_GROUP, :]
    lr, li = par_ref[0:1, :], par_ref[1:2, :]
    dt = jnp.exp(par_ref[2:3, :])
    mag = jnp.exp(lr * dt)
    a_re, a_im = mag * jnp.cos(li * dt), mag * jnp.sin(li * dt)
    nr, ni = a_re - 1.0, a_im
    den = lr * lr + li * li
    f_re, f_im = (nr * lr + ni * li) / den, (ni * lr - nr * li) / den
    b_re, b_im = mat(0), mat(1)
    bb_re = f_re * b_re - f_im * b_im
    bb_im = f_re * b_im + f_im * b_re
    c_re, c_im = mat(2), mat(3)

    tau = lax.broadcasted_iota(jnp.int32, (SSM_T, 1), 0)
    w_re, w_im = _power_table(tau, 6, a_re, a_im)
    w1_re, w1_im = w_re * a_re - w_im * a_im, w_re * a_im + w_im * a_re
    wr_re, wr_im = _power_table(SSM_T - 1 - tau, 6, a_re, a_im)

    def outer(c, w):
        return (c[:, None, :] * w[None, :, :]).reshape(SSM_CW, STATE_LANES)

    cw_mix = (outer(c_re, jnp.where(lo, w_re, w_im)) + outer(c_im, jnp.where(lo, -w_im, w_re)))
    bb_mix = jnp.where(lo, bb_re, -bb_im)
    strip = lax.dot_general(bb_mix, cw_mix, (((1,), (1,)), ((), ())),
                            precision=lax.Precision.HIGHEST,
                            preferred_element_type=F32)

    x_a, y_a = jnp.where(lo, bb_re, bb_im), jnp.where(lo, -bb_im, bb_re)
    p_ref[...] = (outer(x_a, wr_re) + outer(y_a, wr_im)).astype(BF16)

    qt_ref[...] = (outer(c_re, jnp.where(lo, w1_re, -w1_im))
                   + outer(c_im, jnp.where(lo, -w1_im, -w1_re))).astype(BF16)

    t_re, t_im = a_re, a_im
    for _ in range(6):
        t_re, t_im = t_re * t_re - t_im * t_im, 2.0 * t_re * t_im
    for k in range(COEF_ROWS // 2):
        coef_ref[2 * k:2 * k + 1, :] = t_re
        coef_ref[2 * k + 1:2 * k + 2, :] = jnp.where(lo, -t_im, t_im)
        t_re, t_im = t_re * t_re - t_im * t_im, 2.0 * t_re * t_im
    return strip


def _toeplitz_rows(strip, m_ref, channels):
    s_idx = lax.broadcasted_iota(jnp.int32, (SSM_T, SSM_CW), 0)
    t_idx = lax.broadcasted_iota(jnp.int32, (SSM_T, SSM_CW), 1) & (SSM_T - 1)
    causal = t_idx >= s_idx
    for c in channels:
        rows = jnp.broadcast_to(strip[c:c + 1, :], (SSM_T, SSM_CW))
        shifted = pltpu.roll(rows, 0, 1, stride=1, stride_axis=0)
        m_ref[c * SSM_T:(c + 1) * SSM_T, :] = jnp.where(causal, shifted, 0.0).astype(BF16)


PAR_MATS = 8
PAR_ROWS = PAR_MATS + 4 * SSM_GROUP


def _table_params(lam_re, lam_im, log_step, b_re, b_im, c_re, c_im):
    step = jnp.broadcast_to(log_step[:, None, None], (SSM_GROUPS, 1, SSM_STATE))
    filler = jnp.zeros((SSM_GROUPS, PAR_MATS - 3, SSM_STATE), F32)
    pack = jnp.concatenate([lam_re[:, None, :], lam_im[:, None, :], step, filler,
                            jnp.swapaxes(b_re, 1, 2), jnp.swapaxes(b_im, 1, 2), c_re, c_im], axis=1)
    return jnp.concatenate([pack, pack], axis=-1)


def _cmul(a1, a2, h, hs):
    return a1 * h + a2 * hs, a1 * hs - a2 * h


def _ssm_kernel(par_ref, u_ref, h0_ref, y_ref, hfin_ref,
                m_ref, p_ref, qt_ref, coef_ref, *, rows_per_seq, prompt_rows):
    strip = _build_tables(par_ref, p_ref, qt_ref, coef_ref)
    lo = lax.broadcasted_iota(jnp.int32, (1, ROW_TOKENS), 1) < SSM_T

    def chunk_rows(ref):
        even, odd = [], []
        rows = ref.shape[0] * ROW_GROUP
        for k in range(SSM_GROUP // 2):
            a = ref[:, 2 * k].reshape(rows, ROW_TOKENS)
            b = ref[:, 2 * k + 1].reshape(rows, ROW_TOKENS)
            even.append(jnp.where(lo, a, pltpu.roll(b, SSM_T, 1)))
            odd.append(jnp.where(lo, pltpu.roll(a, SSM_T, 1), b))
        return jnp.concatenate(even, axis=1), jnp.concatenate(odd, axis=1)

    def store_rows(y_even, y_odd, ref):
        for k in range(SSM_GROUP // 2):
            te = y_even[:, k * ROW_TOKENS:(k + 1) * ROW_TOKENS]
            to = y_odd[:, k * ROW_TOKENS:(k + 1) * ROW_TOKENS]
            tiles = (ref.shape[0], ROW_GROUP, ROW_TOKENS)
            ref[:, 2 * k] = jnp.where(lo, te, pltpu.roll(to, SSM_T, 1)).reshape(tiles)
            ref[:, 2 * k + 1] = jnp.where(lo, pltpu.roll(te, SSM_T, 1), to).reshape(tiles)

    even, odd = chunk_rows(u_ref)
    rp, rs = prompt_rows, even.shape[0] - prompt_rows
    u = jnp.concatenate([even[:rp], odd[:rp], even[rp:], odd[rp:]], axis=0).astype(BF16)
    s1 = jnp.dot(u, p_ref[...], preferred_element_type=F32)
    s2 = pltpu.roll(s1, SSM_STATE, 1)
    a1, a2 = coef_ref[0:1, :], coef_ref[1:2, :]

    e1, e2, o1, o2 = s1[:rp], s2[:rp], s1[rp:2 * rp], s2[rp:2 * rp]
    x1, x2 = _cmul(a1, a2, e1, e2)
    x1, x2 = x1 + o1, x2 + o2
    pos = lax.broadcasted_iota(jnp.int32, (rp, STATE_LANES), 0) & (rows_per_seq - 1)
    for k in range(rows_per_seq.bit_length() - 1):
        d = 1 << k
        b1, b2 = coef_ref[2 + 2 * k:3 + 2 * k, :], coef_ref[3 + 2 * k:4 + 2 * k, :]
        sh1 = jnp.where(pos >= d, pltpu.roll(x1, d, 0), 0.0)
        sh2 = jnp.where(pos >= d, pltpu.roll(x2, d, 0), 0.0)
        y1, y2 = _cmul(b1, b2, sh1, sh2)
        x1, x2 = x1 + y1, x2 + y2
    g1 = jnp.where(pos >= 1, pltpu.roll(x1, 1, 0), 0.0)
    g2 = jnp.where(pos >= 1, pltpu.roll(x2, 1, 0), 0.0)
    ho1 = _cmul(a1, a2, g1, g2)[0] + e1
    hfin_ref[0:STATE_OUT_ROWS, :] = jnp.zeros((STATE_OUT_ROWS, STATE_LANES), F32)
    for b in range(rp // rows_per_seq):
        last = (b + 1) * rows_per_seq - 1
        hfin_ref[b:b + 1, :] = x1[last:last + 1, :]

    h0e, h0o = h0_ref[0], h0_ref[1]
    swap = lambda h: pltpu.roll(h, SSM_STATE, 1)
    hfin_ref[STATE_OUT_ROWS:STATE_OUT_ROWS + rs, :] = a1 * h0e + a2 * swap(h0e) + s1[2 * rp:2 * rp + rs]
    hfin_ref[STATE_OUT_ROWS + rs:, :] = a1 * h0o + a2 * swap(h0o) + s1[2 * rp + rs:]

    hprev = jnp.concatenate([g1, ho1, h0e, h0o], axis=0).astype(BF16)
    y = lax.dot_general(hprev, qt_ref[...], (((1,), (1,)), ((), ())), preferred_element_type=F32)
    blocks = [range(c, c + M_BLOCK_CHANNELS) for c in range(0, SSM_GROUP, M_BLOCK_CHANNELS)]
    _toeplitz_rows(strip, m_ref, blocks[0])
    for i, channels in enumerate(blocks):
        rows = slice(channels[0] * SSM_T, (channels[-1] + 1) * SSM_T)
        y += jnp.dot(u[:, rows], m_ref[rows, :], preferred_element_type=F32)
        if i + 1 < len(blocks):
            _toeplitz_rows(strip, m_ref, blocks[i + 1])
    store_rows(jnp.concatenate([y[:rp], y[2 * rp:2 * rp + rs]], axis=0),
               jnp.concatenate([y[rp:2 * rp], y[2 * rp + rs:]], axis=0), y_ref)


def _ssm(table_params, u3, h0_pack, *, rows_per_seq, prompt_rows):
    groups = u3.shape[0]
    rs = groups * ROW_GROUP - prompt_rows
    u4 = u3.reshape(groups, SSM_WIDTH, ROW_GROUP, ROW_TOKENS)
    assert rows_per_seq & (rows_per_seq - 1) == 0 and 2 * rows_per_seq.bit_length() <= COEF_ROWS
    g3 = lambda g: (g, 0, 0)
    u_spec = pl.BlockSpec((groups, SSM_GROUP, ROW_GROUP, ROW_TOKENS), lambda g: (0, g, 0, 0))
    fin_rows = STATE_OUT_ROWS + 2 * rs
    return pl.pallas_call(
        functools.partial(_ssm_kernel, rows_per_seq=rows_per_seq, prompt_rows=prompt_rows),
        grid=(SSM_GROUPS,),
        in_specs=[
            pl.BlockSpec((None, PAR_ROWS, STATE_LANES), g3),
            u_spec,
            pl.BlockSpec((None, 2, rs, STATE_LANES), lambda g: (g, 0, 0, 0)),
        ],
        out_specs=[
            u_spec,
            pl.BlockSpec((None, fin_rows, STATE_LANES), g3),
        ],
        out_shape=[
            jax.ShapeDtypeStruct(u4.shape, F32),
            jax.ShapeDtypeStruct((SSM_GROUPS, fin_rows, STATE_LANES), F32),
        ],
        scratch_shapes=[
            pltpu.VMEM((SSM_CW, SSM_CW), BF16),
            pltpu.VMEM((SSM_CW, STATE_LANES), BF16),
            pltpu.VMEM((SSM_CW, STATE_LANES), BF16),
            pltpu.VMEM((COEF_ROWS, STATE_LANES), F32),
        ],
        compiler_params=pltpu.CompilerParams(dimension_semantics=("arbitrary",)),
        name="ssm",
    )(table_params, u4, h0_pack)


def _post_kernel(xn_ref, atp_ref, ats_ref, ut_ref, yt_ref, *refs, alpha, prompt_tiles):
    weights, (op_ref, os_ref) = refs[:-2], refs[-2:]

    @pl.when(pl.program_id(0) < prompt_tiles)
    def _():
        _post_tile(xn_ref, atp_ref, ut_ref, yt_ref, *weights, op_ref, alpha=alpha)

    @pl.when(pl.program_id(0) >= prompt_tiles)
    def _():
        _post_tile(xn_ref, ats_ref, ut_ref, yt_ref, *weights, os_ref, alpha=alpha)


def _post_tile(xn_ref, at_ref, ut_ref, yt_ref, d_ref, wglut_ref, bglu_ref,
               wout_ref, ln1g_ref, ln1b_ref, wgu_ref, wdown_ref, ln2g_ref, ln2b_ref, o_ref,
               *, alpha):
    xn = xn_ref[...]
    n_rows = xn_ref.shape[0] // ROW_TOKENS
    first_row = (pl.program_id(0) % (ROW_GROUP // n_rows)) * n_rows

    def feature_major(ref):
        return jnp.concatenate(
            [ref[pl.ds(first_row + j, SSM_WIDTH, stride=ROW_GROUP), :] for j in range(n_rows)], axis=1)

    ys = feature_major(yt_ref) + d_ref[...] * feature_major(ut_ref)
    gl = 0.5 * ys * (1.0 + lax.erf(ys * math.sqrt(0.5)))
    z = jnp.dot(wglut_ref[...], gl.astype(BF16), preferred_element_type=F32) + bglu_ref[...]
    s = gl * jax.nn.sigmoid(z)
    tn = (((0,), (0,)), ((), ()))
    mix = lax.dot_general(at_ref[...], wout_ref[:PROJ_Q, :], tn, preferred_element_type=F32)
    mix += lax.dot_general(s.astype(BF16), wout_ref[PROJ_Q:, :], tn, preferred_element_type=F32)
    h = _layer_norm(alpha * xn + mix, ln1g_ref[...], ln1b_ref[...])
    hb = h.astype(BF16)
    f = jnp.zeros_like(h)
    for start in range(0, D_FF, FF_BLOCK):
        stop = min(start + FF_BLOCK, D_FF)
        cols = slice(start, stop)
        up_cols = slice(D_FF + start, D_FF + stop)
        g = jnp.dot(hb, wgu_ref[:, cols], preferred_element_type=F32)
        up = jnp.dot(hb, wgu_ref[:, up_cols], preferred_element_type=F32)
        act = (g * jax.nn.sigmoid(g)) * up
        f += jnp.dot(act.astype(BF16), wdown_ref[cols, :], preferred_element_type=F32)
    o_ref[...] = _layer_norm(alpha * h + f, ln2g_ref[...], ln2b_ref[...])


def _post(xn, atp, ats, ut3, yt3, d_col, w_glu_t, b_glu_col, w_out, ln1_g, ln1_b,
          w_gate_up, w_down, ln2_g, ln2_b, *, alpha, tm):
    n_p, n_s = atp.shape[1], ats.shape[1]
    tp, ts = n_p // tm, n_s // tm
    assert PROJ_TILE % tm == 0 and n_p % PROJ_TILE == 0
    group = lambda i: (i // (PROJ_TILE // tm), 0, 0)
    group_spec = pl.BlockSpec((None, SSM_WIDTH * ROW_GROUP, ROW_TOKENS), group)
    prompt = lambda i: jnp.minimum(i, tp - 1)
    sample = lambda i: jnp.maximum(i - tp, 0)
    const = lambda i: (0, 0)
    resident = lambda shape: pl.BlockSpec(shape, const, pipeline_mode=pl.Buffered(1))
    vec = lambda width: pl.BlockSpec((1, width), const)
    colvec = pl.BlockSpec((SSM_WIDTH, 1), const)
    return pl.pallas_call(
        functools.partial(_post_kernel, alpha=alpha, prompt_tiles=tp),
        grid=(tp + ts,),
        in_specs=[
            pl.BlockSpec((tm, D_MODEL), lambda i: (i, 0)),
            pl.BlockSpec((PROJ_Q, tm), lambda i: (0, prompt(i))),
            pl.BlockSpec((PROJ_Q, tm), lambda i: (0, sample(i))),
            group_spec, group_spec,
            colvec,
            resident((SSM_WIDTH, SSM_WIDTH)), colvec,
            resident((D_MODEL, D_MODEL)), vec(D_MODEL), vec(D_MODEL),
            resident((D_MODEL, 2 * D_FF)), resident((D_FF, D_MODEL)),
            vec(D_MODEL), vec(D_MODEL),
        ],
        out_specs=[
            pl.BlockSpec((tm, D_MODEL), lambda i: (prompt(i), 0)),
            pl.BlockSpec((tm, D_MODEL), lambda i: (sample(i), 0)),
        ],
        out_shape=[
            jax.ShapeDtypeStruct((n_p, D_MODEL), F32),
            jax.ShapeDtypeStruct((n_s, D_MODEL), F32),
        ],
        compiler_params=pltpu.CompilerParams(
            dimension_semantics=("arbitrary",), vmem_limit_bytes=V7X_VMEM_LIMIT_BYTES),
        name="post",
    )(xn, atp, ats, ut3, yt3, d_col, w_glu_t, b_glu_col, w_out, ln1_g, ln1_b,
      w_gate_up, w_down, ln2_g, ln2_b)


def kernel(x_prompt, x_sample, cache_win_k, cache_win_v, state_ssm_re, state_ssm_im,
           ln_in_g, ln_in_b, w_in, attn_sinks, ssm_lambda_re, ssm_lambda_im, ssm_log_step,
           ssm_b_re, ssm_b_im, ssm_c_re, ssm_c_im, ssm_d, w_glu, b_glu, w_out,
           ln1_g, ln1_b, w_gate_up, w_down, ln2_g, ln2_b):
    depth = w_in.shape[0]
    assert depth == 1, "single-layer step"
    bp, lp, _ = x_prompt.shape
    bs, ls, _ = x_sample.shape
    assert ls == SSM_T and bs % 2 == 0 and lp % 512 == 0 and bp <= STATE_OUT_ROWS
    win_rows = cache_win_k.shape[2]
    assert win_rows == WINDOW
    alpha = (2.0 * depth) ** 0.25
    l = 0
    row = lambda a: a.reshape(1, -1)
    column = lambda a: a.reshape(-1, 1)

    lng, lnb = row(ln_in_g), row(ln_in_b)
    xp2 = x_prompt.reshape(bp * lp, D_MODEL)
    xs2 = x_sample.reshape(bs * ls, D_MODEL)
    n_p = bp * lp
    xn, qt, vt, u3, k, v = _proj(xp2, xs2, lng, lnb, w_in[l].astype(BF16), w_in[l].T.astype(BF16))

    sink_rows = jnp.repeat(attn_sinks[l].reshape(N_KV_HEADS, Q_PER_KV), PAIR, axis=1)[:, None, :]
    atp = _attention_prompt(sink_rows, qt, k, vt, bp, lp, n_pairs=4)
    ck = cache_win_k[l].reshape(bs, win_rows, PROJ_KV)
    cv = cache_win_v[l].reshape(bs, win_rows, PROJ_KV)
    cvt = jnp.transpose(cv, (2, 0, 1)).reshape(PROJ_KV, bs * win_rows).astype(BF16)
    ats = _attention_sample(sink_rows, qt, ck, k, cvt, vt, first_token=n_p, n_pairs=4)
    ks3, vs3 = k[n_p:].reshape(bs, ls, PROJ_KV), v[n_p:].reshape(bs, ls, PROJ_KV)

    table_params = _table_params(
        ssm_lambda_re[l], ssm_lambda_im[l], ssm_log_step[l],
        ssm_b_re[l], ssm_b_im[l], ssm_c_re[l], ssm_c_im[l])
    h0 = jnp.concatenate([state_ssm_re[l], state_ssm_im[l]], axis=-1)
    h0_pack = jnp.transpose(h0.reshape(bs // 2, 2, SSM_GROUPS, STATE_LANES), (2, 1, 0, 3))
    y4, hfin = _ssm(table_params, u3, h0_pack,
                    rows_per_seq=lp // ROW_TOKENS, prompt_rows=n_p // ROW_TOKENS)

    post_args = (column(ssm_d[l]), w_glu[l].T.astype(BF16), column(b_glu[l]),
                 w_out[l].astype(BF16), row(ln1_g[l]), row(ln1_b[l]),
                 w_gate_up[l].astype(BF16), w_down[l].astype(BF16), row(ln2_g[l]), row(ln2_b[l]))
    out_p, out_s = _post(xn, atp, ats, u3, y4.reshape(u3.shape), *post_args, alpha=alpha, tm=512)

    kv_shape = (N_KV_HEADS, HEAD_DIM)
    last_rows = lambda a: jnp.stack([a[(b + 1) * lp - win_rows:(b + 1) * lp] for b in range(bp)])
    win_k_p = last_rows(k).reshape(1, bp, win_rows, *kv_shape)
    win_v_p = last_rows(v).reshape(1, bp, win_rows, *kv_shape)
    win_k_s = jnp.concatenate([ck, ks3], axis=1)[:, -win_rows:].reshape(1, bs, win_rows, *kv_shape)
    win_v_s = jnp.concatenate([cv, vs3], axis=1)[:, -win_rows:].reshape(1, bs, win_rows, *kv_shape)
    sp = jnp.swapaxes(hfin[:, :bp], 0, 1)
    ss = jnp.transpose(hfin[:, STATE_OUT_ROWS:].reshape(SSM_GROUPS, 2, bs // 2, STATE_LANES),
                       (2, 1, 0, 3)).reshape(bs, SSM_GROUPS, STATE_LANES)
    return (out_p.reshape(bp, lp, D_MODEL), out_s.reshape(bs, ls, D_MODEL),
            win_k_p, win_v_p, sp[None, ..., :SSM_STATE], sp[None, ..., SSM_STATE:],
            win_k_s, win_v_s, ss[None, ..., :SSM_STATE], ss[None, ..., SSM_STATE:])
```

```python
import functools
import math

import jax
import jax.numpy as jnp
from jax import lax
from jax.experimental import pallas as pl
from jax.experimental.pallas import tpu as pltpu

F32 = jnp.float32
BF16 = jnp.bfloat16

D_MODEL = 1024
HEAD_DIM = 64
N_HEADS = 8
N_KV_HEADS = 2
Q_PER_KV = N_HEADS // N_KV_HEADS
CHUNK = 64
WINDOW = 128
WIN_CHUNKS = WINDOW // CHUNK
BAND = (WIN_CHUNKS + 1) * CHUNK
PROJ_Q = N_HEADS * HEAD_DIM
PROJ_KV = N_KV_HEADS * HEAD_DIM
SSM_WIDTH = 512
SSM_GROUP = 16
SSM_GROUPS = SSM_WIDTH // SSM_GROUP
SSM_STATE = 64
STATE_LANES = 2 * SSM_STATE
D_FF = 2816
D_IN_PROJ = PROJ_Q + 2 * PROJ_KV + SSM_WIDTH
LN_EPS = 1e-5
NEG_INF = -1e30

SSM_T = CHUNK
SSM_CW = SSM_T * SSM_GROUP
ROW_TOKENS = 2 * SSM_T
ROW_GROUP = 8
PROJ_TILE = ROW_GROUP * ROW_TOKENS
PROJ_PARTS = 2
COEF_ROWS = 16
STATE_OUT_ROWS = 8
M_BLOCK_CHANNELS = 4
SSM_GROUPS_PER_STEP = 2
FF_BLOCK = 256
V7X_VMEM_LIMIT_BYTES = 56 * 1024 * 1024


def _layer_norm(x, g, b):
    mu = jnp.mean(x, axis=-1, keepdims=True)
    xc = x - mu
    var = jnp.mean(xc * xc, axis=-1, keepdims=True)
    return xc * lax.rsqrt(var + LN_EPS) * g + b


KV = 2 * PROJ_KV
VU = PROJ_KV + SSM_WIDTH


def _proj_kernel(xp_ref, xs_ref, g_ref, b_ref, w_ref, wqt_ref, wvut_ref, qt_ref, vt_ref, ut_ref, k_ref, v_ref,
                 *, prompt_tiles):
    outs = (g_ref, b_ref, w_ref, wqt_ref, wvut_ref, qt_ref, vt_ref, ut_ref, k_ref, v_ref)

    @pl.when(pl.program_id(0) < prompt_tiles)
    def _():
        _proj_tile(xp_ref, *outs)

    @pl.when(pl.program_id(0) >= prompt_tiles)
    def _():
        _proj_tile(xs_ref, *outs)


def _proj_tile(x_ref, g_ref, b_ref, w_ref, wqt_ref, wvut_ref, qt_ref, vt_ref, ut_ref, k_ref, v_ref):
    rows_per_part = ROW_GROUP // PROJ_PARTS
    nt = (((1,), (1,)), ((), ()))
    for part in range(PROJ_PARTS):
        tok = slice(part * rows_per_part * ROW_TOKENS, (part + 1) * rows_per_part * ROW_TOKENS)
        xb = _layer_norm(x_ref[tok, :], g_ref[...], b_ref[...]).astype(BF16)
        p = jnp.dot(xb, w_ref[...], preferred_element_type=F32)
        k_ref[tok, :] = p[:, :PROJ_KV]
        v_ref[tok, :] = p[:, PROJ_KV:]
        qt = lax.dot_general(wqt_ref[...], xb, nt, preferred_element_type=F32)
        vut = lax.dot_general(wvut_ref[...], xb, nt, preferred_element_type=F32)
        qt_ref[:, tok] = (qt * (HEAD_DIM ** -0.5)).astype(BF16)
        vt_ref[:, tok] = vut[:PROJ_KV].astype(BF16)
        for j in range(rows_per_part):
            ut_ref[pl.ds(part * rows_per_part + j, SSM_WIDTH, stride=ROW_GROUP), :] = (
                vut[PROJ_KV:, j * ROW_TOKENS:(j + 1) * ROW_TOKENS])


def _proj(xp2d, xs2d, ln_g, ln_b, w_in, w_in_t):
    assert D_IN_PROJ == 2 * VU and PROJ_Q % KV == 0
    tm = PROJ_TILE
    tp, ts = xp2d.shape[0] // tm, xs2d.shape[0] // tm
    n = (tp + ts) * tm
    const = lambda i: (0, 0)
    row = lambda i: (i, 0)
    col = lambda i: (0, i)
    return pl.pallas_call(
        functools.partial(_proj_kernel, prompt_tiles=tp),
        grid=(tp + ts,),
        in_specs=[
            pl.BlockSpec((tm, D_MODEL), lambda i: (jnp.minimum(i, tp - 1), 0)),
            pl.BlockSpec((tm, D_MODEL), lambda i: (jnp.maximum(i - tp, 0), 0)),
            pl.BlockSpec((1, D_MODEL), const),
            pl.BlockSpec((1, D_MODEL), const),
            pl.BlockSpec((D_MODEL, KV), lambda i: (0, PROJ_Q // KV)),
            pl.BlockSpec((PROJ_Q, D_MODEL), const),
            pl.BlockSpec((VU, D_MODEL), lambda i: (1, 0)),
        ],
        out_specs=[
            pl.BlockSpec((PROJ_Q, tm), col),
            pl.BlockSpec((PROJ_KV, tm), col),
            pl.BlockSpec((None, SSM_WIDTH * ROW_GROUP, ROW_TOKENS), lambda i: (i, 0, 0)),
            pl.BlockSpec((tm, PROJ_KV), row),
            pl.BlockSpec((tm, PROJ_KV), row),
        ],
        out_shape=[
            jax.ShapeDtypeStruct((PROJ_Q, n), BF16),
            jax.ShapeDtypeStruct((PROJ_KV, n), BF16),
            jax.ShapeDtypeStruct((n // tm, SSM_WIDTH * ROW_GROUP, ROW_TOKENS), F32),
            jax.ShapeDtypeStruct((n, PROJ_KV), F32),
            jax.ShapeDtypeStruct((n, PROJ_KV), F32),
        ],
        compiler_params=pltpu.CompilerParams(
            dimension_semantics=("arbitrary",), vmem_limit_bytes=V7X_VMEM_LIMIT_BYTES),
        name="proj",
    )(xp2d, xs2d, ln_g, ln_b, w_in, w_in_t, w_in_t)


PAIR = 2 * CHUNK
HEAD_LANES = Q_PER_KV * PAIR
ONES_ROWS = 16
ATTN_PAIRS = 8


def _scores(unit, qt_ref):
    kwin, _, lanes, _, _, h = unit
    base = h * Q_PER_KV * HEAD_DIM
    qrow = jnp.concatenate(
        [qt_ref[base + g * HEAD_DIM:base + (g + 1) * HEAD_DIM, lanes] for g in range(Q_PER_KV)],
        axis=1)
    zero = jnp.zeros_like(qrow)
    qstack = jnp.concatenate([qrow, zero] if h == 0 else [zero, qrow], axis=0)
    return jnp.dot(kwin, qstack, preferred_element_type=F32)


def _finish(unit, s, sink_ref, o_ref):
    _, vtwin, lanes, valid, masked_rows, h = unit
    nk = s.shape[0]
    base = h * Q_PER_KV * HEAD_DIM
    pieces, done = [], 0
    for start, stop in masked_rows:
        if start > done:
            pieces.append(s[done:start])
        pieces.append(jnp.where(valid[start:stop], s[start:stop], NEG_INF))
        done = stop
    if done < nk:
        pieces.append(s[done:])
    s = jnp.concatenate(pieces, axis=0)
    sink = sink_ref[h]
    m = jnp.maximum(jnp.max(s, axis=0, keepdims=True), sink)
    p = jnp.exp(s - m).astype(BF16)
    v_ones = jnp.concatenate(
        [vtwin[h * HEAD_DIM:(h + 1) * HEAD_DIM, :], jnp.ones((ONES_ROWS, nk), BF16)], axis=0)
    ov = jnp.dot(v_ones, p, preferred_element_type=F32)
    den = ov[HEAD_DIM:HEAD_DIM + 1, :] + jnp.exp(sink - m)
    o = ov[:HEAD_DIM, :] * (1.0 / den)
    for g in range(Q_PER_KV):
        o_ref[base + g * HEAD_DIM:base + (g + 1) * HEAD_DIM, lanes] = (
            o[:, g * PAIR:(g + 1) * PAIR].astype(BF16))


def _attend_units(units, qt_ref, sink_ref, o_ref):
    s_next = _scores(units[0], qt_ref)
    for i, unit in enumerate(units):
        s = s_next
        if i + 1 < len(units):
            s_next = _scores(units[i + 1], qt_ref)
        _finish(unit, s, sink_ref, o_ref)


def _attn_prompt_kernel(sink_ref, qt_ref, kp_ref, kc_ref, vtp_ref, vtc_ref, o_ref, *, n_pairs):
    tq = n_pairs * PAIR
    nk = WINDOW + PAIR
    kk = jnp.concatenate([kp_ref[...], kc_ref[...]], axis=0).astype(BF16)
    vt = jnp.concatenate([vtp_ref[...], vtc_ref[...]], axis=1)
    r = lax.broadcasted_iota(jnp.int32, (nk, HEAD_LANES), 0)
    first_chunk = (lax.broadcasted_iota(jnp.int32, (nk, HEAD_LANES), 1) & (PAIR - 1)) < CHUNK
    lo = jnp.where(first_chunk, 0, CHUNK)
    hi = jnp.where(first_chunk, BAND, nk)
    first_pos = pl.program_id(1) * tq - WINDOW
    units = []
    for pp in range(n_pairs):
        lo_pp = jnp.maximum(lo, -first_pos) if pp == 0 else lo
        valid = (r >= lo_pp) & (r < hi)
        masked_rows = ((0, nk),) if pp == 0 else ((0, CHUNK), (nk - CHUNK, nk))
        for h in range(N_KV_HEADS):
            units.append((kk[pp * PAIR:pp * PAIR + nk], vt[:, pp * PAIR:pp * PAIR + nk],
                          slice(pp * PAIR, (pp + 1) * PAIR), valid, masked_rows, h))
    _attend_units(units, qt_ref, sink_ref, o_ref)


def _attn_sample_kernel(sink_ref, qt_ref, ck_ref, kn_ref, cvt_ref, vtn_ref, o_ref, *, n_pairs):
    nk = 2 * WINDOW + PAIR
    r = lax.broadcasted_iota(jnp.int32, (nk, HEAD_LANES), 0)
    query_seq = (lax.broadcasted_iota(jnp.int32, (nk, HEAD_LANES), 1) & (PAIR - 1)) >> 6
    key_seq = jnp.where(r < 2 * WINDOW, r >> 7, (r - 2 * WINDOW) >> 6)
    valid = query_seq == key_seq
    units = []
    for pp in range(n_pairs):
        lanes = slice(pp * PAIR, (pp + 1) * PAIR)
        kwin = jnp.concatenate([ck_ref[2 * pp], ck_ref[2 * pp + 1], kn_ref[lanes, :]],
                               axis=0).astype(BF16)
        vtwin = jnp.concatenate([cvt_ref[:, 2 * pp * WINDOW:2 * (pp + 1) * WINDOW], vtn_ref[:, lanes]],
                                axis=1)
        units.extend((kwin, vtwin, lanes, valid, ((0, nk),), h) for h in range(N_KV_HEADS))
    _attend_units(units, qt_ref, sink_ref, o_ref)


_SINK_SPEC = pl.BlockSpec((N_KV_HEADS, 1, HEAD_LANES), lambda *_: (0, 0, 0))


def _attention_prompt(sink_rows, qt, k, vt, bsz, seq, n_pairs):
    tq = n_pairs * PAIR
    nt = seq // tq
    wpt = tq // WINDOW
    cur_c = lambda b, i: (0, b * nt + i)
    cur_r = lambda b, i: (b * nt + i, 0)
    prev = lambda b, i: jnp.maximum((b * nt + i) * wpt - 1, 0)
    return pl.pallas_call(
        functools.partial(_attn_prompt_kernel, n_pairs=n_pairs),
        grid=(bsz, nt),
        in_specs=[
            _SINK_SPEC,
            pl.BlockSpec((PROJ_Q, tq), cur_c),
            pl.BlockSpec((WINDOW, PROJ_KV), lambda b, i: (prev(b, i), 0)),
            pl.BlockSpec((tq, PROJ_KV), cur_r),
            pl.BlockSpec((PROJ_KV, WINDOW), lambda b, i: (0, prev(b, i))),
            pl.BlockSpec((PROJ_KV, tq), cur_c),
        ],
        out_specs=pl.BlockSpec((PROJ_Q, tq), cur_c),
        out_shape=jax.ShapeDtypeStruct((PROJ_Q, bsz * seq), BF16),
        compiler_params=pltpu.CompilerParams(dimension_semantics=("arbitrary", "arbitrary")),
        name="attn_prompt",
    )(sink_rows, qt, k, k, vt, vt)


def _attention_sample(sink_rows, qt, cache_k, k, cache_vt, vt, first_token, n_pairs):
    n = cache_k.shape[0] * CHUNK
    tq = n_pairs * PAIR
    assert first_token % tq == 0
    off = first_token // tq
    return pl.pallas_call(
        functools.partial(_attn_sample_kernel, n_pairs=n_pairs),
        grid=(n // tq,),
        in_specs=[
            _SINK_SPEC,
            pl.BlockSpec((PROJ_Q, tq), lambda i: (0, i + off)),
            pl.BlockSpec((2 * n_pairs, WINDOW, PROJ_KV), lambda i: (i, 0, 0)),
            pl.BlockSpec((tq, PROJ_KV), lambda i: (i + off, 0)),
            pl.BlockSpec((PROJ_KV, 2 * n_pairs * WINDOW), lambda i: (0, i)),
            pl.BlockSpec((PROJ_KV, tq), lambda i: (0, i + off)),
        ],
        out_specs=pl.BlockSpec((PROJ_Q, tq), lambda i: (0, i)),
        out_shape=jax.ShapeDtypeStruct((PROJ_Q, n), BF16),
        compiler_params=pltpu.CompilerParams(dimension_semantics=("arbitrary",)),
        name="attn_sample",
    )(sink_rows, qt, cache_k, k, cache_vt, vt)


def _power_table(tau, nbits, a_re, a_im):
    rows = tau.shape[0]
    w_re = jnp.ones((rows, STATE_LANES), F32)
    w_im = jnp.zeros((rows, STATE_LANES), F32)
    p_re, p_im = a_re, a_im
    for k in range(nbits):
        bit = ((tau >> k) & 1) == 1
        f_re = jnp.where(bit, p_re, 1.0)
        f_im = jnp.where(bit, p_im, 0.0)
        w_re, w_im = w_re * f_re - w_im * f_im, w_re * f_im + w_im * f_re
        p_re, p_im = p_re * p_re - p_im * p_im, 2.0 * p_re * p_im
    return w_re, w_im


def _build_tables(par_ref, p_ref, qt_ref, coef_ref):
    lo = lax.broadcasted_iota(jnp.int32, (1, STATE_LANES), 1) < SSM_STATE
    mat = lambda i: par_ref[PAR_MATS + i * SSM_GROUP:PAR_MATS + (i + 1) * SSM_GROUP, :]
    lr, li = par_ref[0:1, :], par_ref[1:2, :]
    dt = jnp.exp(par_ref[2:3, :])
    mag = jnp.exp(lr * dt)
    a_re, a_im = mag * jnp.cos(li * dt), mag * jnp.sin(li * dt)
    nr, ni = a_re - 1.0, a_im
    den = lr * lr + li * li
    f_re, f_im = (nr * lr + ni * li) / den, (ni * lr - nr * li) / den
    b_re, b_im = mat(0), mat(1)
    bb_re = f_re * b_re - f_im * b_im
    bb_im = f_re * b_im + f_im * b_re
    c_re, c_im = mat(2), mat(3)

    tau = lax.broadcasted_iota(jnp.int32, (SSM_T, 1), 0)
    w_re, w_im = _power_table(tau, 6, a_re, a_im)
    w1_re, w1_im = w_re * a_re - w_im * a_im, w_re * a_im + w_im * a_re
    wr_re, wr_im = _power_table(SSM_T - 1 - tau, 6, a_re, a_im)

    def outer(c, w):
        return (c[:, None, :] * w[None, :, :]).reshape(SSM_CW, STATE_LANES)

    cw_mix = (outer(c_re, jnp.where(lo, w_re, w_im)) + outer(c_im, jnp.where(lo, -w_im, w_re)))
    bb_mix = jnp.where(lo, bb_re, -bb_im)
    strip = lax.dot_general(bb_mix, cw_mix, (((1,), (1,)), ((), ())),
                            precision=lax.Precision.HIGHEST,
                            preferred_element_type=F32)

    x_a, y_a = jnp.where(lo, bb_re, bb_im), jnp.where(lo, -bb_im, bb_re)
    p_ref[...] = (outer(x_a, wr_re) + outer(y_a, wr_im)).astype(BF16)

    qt_ref[...] = (outer(c_re, jnp.where(lo, w1_re, -w1_im))
                   + outer(c_im, jnp.where(lo, -w1_im, -w1_re))).astype(BF16)

    t_re, t_im = a_re, a_im
    for _ in range(6):
        t_re, t_im = t_re * t_re - t_im * t_im, 2.0 * t_re * t_im
    for k in range(COEF_ROWS // 2):
        coef_ref[2 * k:2 * k + 1, :] = t_re
        coef_ref[2 * k + 1:2 * k + 2, :] = jnp.where(lo, -t_im, t_im)
        t_re, t_im = t_re * t_re - t_im * t_im, 2.0 * t_re * t_im
    return strip


def _toeplitz_rows(strip, m_ref, channels):
    s_idx = lax.broadcasted_iota(jnp.int32, (SSM_T, SSM_CW), 0)
    t_idx = lax.broadcasted_iota(jnp.int32, (SSM_T, SSM_CW), 1) & (SSM_T - 1)
    causal = t_idx >= s_idx
    for c in channels:
        rows = jnp.broadcast_to(strip[c:c + 1, :], (SSM_T, SSM_CW))
        shifted = pltpu.roll(rows, 0, 1, stride=1, stride_axis=0)
        m_ref[c * SSM_T:(c + 1) * SSM_T, :] = jnp.where(causal, shifted, 0.0).astype(BF16)


PAR_MATS = 8
PAR_ROWS = PAR_MATS + 4 * SSM_GROUP


def _table_params(lam_re, lam_im, log_step, b_re, b_im, c_re, c_im):
    step = jnp.broadcast_to(log_step[:, None, None], (SSM_GROUPS, 1, SSM_STATE))
    filler = jnp.zeros((SSM_GROUPS, PAR_MATS - 3, SSM_STATE), F32)
    pack = jnp.concatenate([lam_re[:, None, :], lam_im[:, None, :], step, filler,
                            jnp.swapaxes(b_re, 1, 2), jnp.swapaxes(b_im, 1, 2), c_re, c_im], axis=1)
    return jnp.concatenate([pack, pack], axis=-1)


def _cmul(a1, a2, h, hs):
    return a1 * h + a2 * hs, a1 * hs - a2 * h


def _ssm_kernel(par_ref, u_ref, h0_ref, y_ref, hfin_ref, m_ref, p_ref, qt_ref, coef_ref,
                *, rows_per_seq, prompt_rows):
    for gi in range(SSM_GROUPS_PER_STEP):
        _ssm_group(par_ref.at[gi], u_ref, gi * SSM_GROUP, h0_ref.at[gi], y_ref, hfin_ref.at[gi],
                   m_ref.at[gi], p_ref.at[gi], qt_ref.at[gi], coef_ref.at[gi],
                   rows_per_seq=rows_per_seq, prompt_rows=prompt_rows)


def _ssm_group(par_ref, u_ref, c0, h0_ref, y_ref, hfin_ref, m_ref, p_ref, qt_ref, coef_ref,
               *, rows_per_seq, prompt_rows):
    strip = _build_tables(par_ref, p_ref, qt_ref, coef_ref)
    lo = lax.broadcasted_iota(jnp.int32, (1, ROW_TOKENS), 1) < SSM_T

    def chunk_rows(ref):
        even, odd = [], []
        rows = ref.shape[0] * ROW_GROUP
        for k in range(SSM_GROUP // 2):
            a = ref[:, c0 + 2 * k].reshape(rows, ROW_TOKENS)
            b = ref[:, c0 + 2 * k + 1].reshape(rows, ROW_TOKENS)
            even.append(jnp.where(lo, a, pltpu.roll(b, SSM_T, 1)))
            odd.append(jnp.where(lo, pltpu.roll(a, SSM_T, 1), b))
        return jnp.concatenate(even, axis=1), jnp.concatenate(odd, axis=1)

    def store_rows(y_even, y_odd, ref):
        for k in range(SSM_GROUP // 2):
            te = y_even[:, k * ROW_TOKENS:(k + 1) * ROW_TOKENS]
            to = y_odd[:, k * ROW_TOKENS:(k + 1) * ROW_TOKENS]
            tiles = (ref.shape[0], ROW_GROUP, ROW_TOKENS)
            ref[:, c0 + 2 * k] = jnp.where(lo, te, pltpu.roll(to, SSM_T, 1)).reshape(tiles)
            ref[:, c0 + 2 * k + 1] = jnp.where(lo, pltpu.roll(te, SSM_T, 1), to).reshape(tiles)

    even, odd = chunk_rows(u_ref)
    rp, rs = prompt_rows, even.shape[0] - prompt_rows
    u = jnp.concatenate([even[:rp], odd[:rp], even[rp:], odd[rp:]], axis=0).astype(BF16)
    s1 = jnp.dot(u, p_ref[...], preferred_element_type=F32)
    s2 = pltpu.roll(s1, SSM_STATE, 1)
    a1, a2 = coef_ref[0:1, :], coef_ref[1:2, :]

    e1, e2, o1, o2 = s1[:rp], s2[:rp], s1[rp:2 * rp], s2[rp:2 * rp]
    x1, x2 = _cmul(a1, a2, e1, e2)
    x1, x2 = x1 + o1, x2 + o2
    pos = lax.broadcasted_iota(jnp.int32, (rp, STATE_LANES), 0) & (rows_per_seq - 1)
    for k in range(rows_per_seq.bit_length() - 1):
        d = 1 << k
        b1, b2 = coef_ref[2 + 2 * k:3 + 2 * k, :], coef_ref[3 + 2 * k:4 + 2 * k, :]
        sh1 = jnp.where(pos >= d, pltpu.roll(x1, d, 0), 0.0)
        sh2 = jnp.where(pos >= d, pltpu.roll(x2, d, 0), 0.0)
        y1, y2 = _cmul(b1, b2, sh1, sh2)
        x1, x2 = x1 + y1, x2 + y2
    g1 = jnp.where(pos >= 1, pltpu.roll(x1, 1, 0), 0.0)
    g2 = jnp.where(pos >= 1, pltpu.roll(x2, 1, 0), 0.0)
    ho1 = _cmul(a1, a2, g1, g2)[0] + e1
    hfin_ref[0:STATE_OUT_ROWS, :] = jnp.zeros((STATE_OUT_ROWS, STATE_LANES), F32)
    for b in range(rp // rows_per_seq):
        last = (b + 1) * rows_per_seq - 1
        hfin_ref[b:b + 1, :] = x1[last:last + 1, :]

    h0e, h0o = h0_ref[0], h0_ref[1]
    swap = lambda h: pltpu.roll(h, SSM_STATE, 1)
    hfin_ref[STATE_OUT_ROWS:STATE_OUT_ROWS + rs, :] = a1 * h0e + a2 * swap(h0e) + s1[2 * rp:2 * rp + rs]
    hfin_ref[STATE_OUT_ROWS + rs:, :] = a1 * h0o + a2 * swap(h0o) + s1[2 * rp + rs:]

    hprev = jnp.concatenate([g1, ho1, h0e, h0o], axis=0).astype(BF16)
    y = lax.dot_general(hprev, qt_ref[...], (((1,), (1,)), ((), ())), preferred_element_type=F32)
    blocks = [range(c, c + M_BLOCK_CHANNELS) for c in range(0, SSM_GROUP, M_BLOCK_CHANNELS)]
    _toeplitz_rows(strip, m_ref, blocks[0])
    for i, channels in enumerate(blocks):
        rows = slice(channels[0] * SSM_T, (channels[-1] + 1) * SSM_T)
        y += jnp.dot(u[:, rows], m_ref[rows, :], preferred_element_type=F32)
        if i + 1 < len(blocks):
            _toeplitz_rows(strip, m_ref, blocks[i + 1])
    store_rows(jnp.concatenate([y[:rp], y[2 * rp:2 * rp + rs]], axis=0),
               jnp.concatenate([y[rp:2 * rp], y[2 * rp + rs:]], axis=0), y_ref)


def _ssm(table_params, u3, h0_pack, *, rows_per_seq, prompt_rows):
    groups = u3.shape[0]
    rs = groups * ROW_GROUP - prompt_rows
    u4 = u3.reshape(groups, SSM_WIDTH, ROW_GROUP, ROW_TOKENS)
    assert rows_per_seq & (rows_per_seq - 1) == 0 and 2 * rows_per_seq.bit_length() <= COEF_ROWS
    gps = SSM_GROUPS_PER_STEP
    g3 = lambda g: (g, 0, 0)
    u_spec = pl.BlockSpec((groups, gps * SSM_GROUP, ROW_GROUP, ROW_TOKENS), lambda g: (0, g, 0, 0))
    fin_rows = STATE_OUT_ROWS + 2 * rs
    return pl.pallas_call(
        functools.partial(_ssm_kernel, rows_per_seq=rows_per_seq, prompt_rows=prompt_rows),
        grid=(SSM_GROUPS // gps,),
        in_specs=[
            pl.BlockSpec((gps, PAR_ROWS, STATE_LANES), g3),
            u_spec,
            pl.BlockSpec((gps, 2, rs, STATE_LANES), lambda g: (g, 0, 0, 0)),
        ],
        out_specs=[
            u_spec,
            pl.BlockSpec((gps, fin_rows, STATE_LANES), g3),
        ],
        out_shape=[
            jax.ShapeDtypeStruct(u4.shape, F32),
            jax.ShapeDtypeStruct((SSM_GROUPS, fin_rows, STATE_LANES), F32),
        ],
        scratch_shapes=[
            pltpu.VMEM((gps, SSM_CW, SSM_CW), BF16),
            pltpu.VMEM((gps, SSM_CW, STATE_LANES), BF16),
            pltpu.VMEM((gps, SSM_CW, STATE_LANES), BF16),
            pltpu.VMEM((gps, COEF_ROWS, STATE_LANES), F32),
        ],
        compiler_params=pltpu.CompilerParams(
            dimension_semantics=("arbitrary",), vmem_limit_bytes=V7X_VMEM_LIMIT_BYTES),
        name="ssm",
    )(table_params, u4, h0_pack)


def _post_kernel(xp_ref, xs_ref, atp_ref, ats_ref, ut_ref, yt_ref, *refs, alpha, prompt_tiles):
    weights, (op_ref, os_ref) = refs[:-2], refs[-2:]

    @pl.when(pl.program_id(0) < prompt_tiles)
    def _():
        _post_tile(xp_ref, atp_ref, ut_ref, yt_ref, *weights, op_ref, alpha=alpha)

    @pl.when(pl.program_id(0) >= prompt_tiles)
    def _():
        _post_tile(xs_ref, ats_ref, ut_ref, yt_ref, *weights, os_ref, alpha=alpha)


def _post_tile(x_ref, at_ref, ut_ref, yt_ref, lng_ref, lnb_ref, d_ref, wglut_ref, bglu_ref,
               wout_ref, ln1g_ref, ln1b_ref, wgu_ref, wdown_ref, ln2g_ref, ln2b_ref, o_ref,
               *, alpha):
    xn = _layer_norm(x_ref[...], lng_ref[...], lnb_ref[...])
    n_rows = x_ref.shape[0] // ROW_TOKENS
    first_row = (pl.program_id(0) % (ROW_GROUP // n_rows)) * n_rows

    def feature_major(ref):
        return jnp.concatenate(
            [ref[pl.ds(first_row + j, SSM_WIDTH, stride=ROW_GROUP), :] for j in range(n_rows)], axis=1)

    ys = feature_major(yt_ref) + d_ref[...] * feature_major(ut_ref)
    gl = 0.5 * ys * (1.0 + lax.erf(ys * math.sqrt(0.5)))
    z = jnp.dot(wglut_ref[...], gl.astype(BF16), preferred_element_type=F32) + bglu_ref[...]
    s = gl * jax.nn.sigmoid(z)
    tn = (((0,), (0,)), ((), ()))
    mix = lax.dot_general(at_ref[...], wout_ref[:PROJ_Q, :], tn, preferred_element_type=F32)
    mix += lax.dot_general(s.astype(BF16), wout_ref[PROJ_Q:, :], tn, preferred_element_type=F32)
    h = _layer_norm(alpha * xn + mix, ln1g_ref[...], ln1b_ref[...])
    hb = h.astype(BF16)
    f = jnp.zeros_like(h)
    for start in range(0, D_FF, FF_BLOCK):
        stop = min(start + FF_BLOCK, D_FF)
        cols = slice(start, stop)
        up_cols = slice(D_FF + start, D_FF + stop)
        g = jnp.dot(hb, wgu_ref[:, cols], preferred_element_type=F32)
        up = jnp.dot(hb, wgu_ref[:, up_cols], preferred_element_type=F32)
        act = (g * jax.nn.sigmoid(g)) * up
        f += jnp.dot(act.astype(BF16), wdown_ref[cols, :], preferred_element_type=F32)
    o_ref[...] = _layer_norm(alpha * h + f, ln2g_ref[...], ln2b_ref[...])


def _post(xp2d, xs2d, atp, ats, ut3, yt3, ln_g, ln_b, d_col, w_glu_t, b_glu_col, w_out, ln1_g, ln1_b,
          w_gate_up, w_down, ln2_g, ln2_b, *, alpha, tm):
    n_p, n_s = xp2d.shape[0], xs2d.shape[0]
    tp, ts = n_p // tm, n_s // tm
    assert PROJ_TILE % tm == 0 and n_p % PROJ_TILE == 0
    group = lambda i: (i // (PROJ_TILE // tm), 0, 0)
    group_spec = pl.BlockSpec((None, SSM_WIDTH * ROW_GROUP, ROW_TOKENS), group)
    prompt = lambda i: jnp.minimum(i, tp - 1)
    sample = lambda i: jnp.maximum(i - tp, 0)
    const = lambda i: (0, 0)
    resident = lambda shape: pl.BlockSpec(shape, const, pipeline_mode=pl.Buffered(1))
    vec = lambda width: pl.BlockSpec((1, width), const)
    colvec = pl.BlockSpec((SSM_WIDTH, 1), const)
    return pl.pallas_call(
        functools.partial(_post_kernel, alpha=alpha, prompt_tiles=tp),
        grid=(tp + ts,),
        in_specs=[
            pl.BlockSpec((tm, D_MODEL), lambda i: (prompt(i), 0)),
            pl.BlockSpec((tm, D_MODEL), lambda i: (sample(i), 0)),
            pl.BlockSpec((PROJ_Q, tm), lambda i: (0, prompt(i))),
            pl.BlockSpec((PROJ_Q, tm), lambda i: (0, sample(i))),
            group_spec, group_spec,
            vec(D_MODEL), vec(D_MODEL), colvec,
            resident((SSM_WIDTH, SSM_WIDTH)), colvec,
            resident((D_MODEL, D_MODEL)), vec(D_MODEL), vec(D_MODEL),
            resident((D_MODEL, 2 * D_FF)), resident((D_FF, D_MODEL)),
            vec(D_MODEL), vec(D_MODEL),
        ],
        out_specs=[
            pl.BlockSpec((tm, D_MODEL), lambda i: (prompt(i), 0)),
            pl.BlockSpec((tm, D_MODEL), lambda i: (sample(i), 0)),
        ],
        out_shape=[
            jax.ShapeDtypeStruct((n_p, D_MODEL), F32),
            jax.ShapeDtypeStruct((n_s, D_MODEL), F32),
        ],
        compiler_params=pltpu.CompilerParams(
            dimension_semantics=("arbitrary",), vmem_limit_bytes=V7X_VMEM_LIMIT_BYTES),
        name="post",
    )(xp2d, xs2d, atp, ats, ut3, yt3, ln_g, ln_b, d_col, w_glu_t, b_glu_col, w_out, ln1_g, ln1_b,
      w_gate_up, w_down, ln2_g, ln2_b)


def kernel(x_prompt, x_sample, cache_win_k, cache_win_v, state_ssm_re, state_ssm_im,
           ln_in_g, ln_in_b, w_in, attn_sinks, ssm_lambda_re, ssm_lambda_im, ssm_log_step,
           ssm_b_re, ssm_b_im, ssm_c_re, ssm_c_im, ssm_d, w_glu, b_glu, w_out,
           ln1_g, ln1_b, w_gate_up, w_down, ln2_g, ln2_b):
    depth = w_in.shape[0]
    assert depth == 1, "single-layer step"
    bp, lp, _ = x_prompt.shape
    bs, ls, _ = x_sample.shape
    assert ls == SSM_T and bs % 2 == 0 and lp % 512 == 0 and bp <= STATE_OUT_ROWS
    win_rows = cache_win_k.shape[2]
    assert win_rows == WINDOW
    alpha = (2.0 * depth) ** 0.25
    l = 0
    row = lambda a: a.reshape(1, -1)
    column = lambda a: a.reshape(-1, 1)

    lng, lnb = row(ln_in_g), row(ln_in_b)
    xp2 = x_prompt.reshape(bp * lp, D_MODEL)
    xs2 = x_sample.reshape(bs * ls, D_MODEL)
    n_p = bp * lp
    qt, vt, u3, k, v = _proj(xp2, xs2, lng, lnb, w_in[l].astype(BF16), w_in[l].T.astype(BF16))

    sink_rows = jnp.repeat(attn_sinks[l].reshape(N_KV_HEADS, Q_PER_KV), PAIR, axis=1)[:, None, :]
    atp = _attention_prompt(sink_rows, qt, k, vt, bp, lp, n_pairs=ATTN_PAIRS)
    ck = cache_win_k[l].reshape(bs, win_rows, PROJ_KV)
    cv = cache_win_v[l].reshape(bs, win_rows, PROJ_KV)
    cvt = jnp.transpose(cv, (2, 0, 1)).reshape(PROJ_KV, bs * win_rows).astype(BF16)
    ats = _attention_sample(sink_rows, qt, ck, k, cvt, vt, first_token=n_p, n_pairs=ATTN_PAIRS)
    ks3, vs3 = k[n_p:].reshape(bs, ls, PROJ_KV), v[n_p:].reshape(bs, ls, PROJ_KV)

    table_params = _table_params(
        ssm_lambda_re[l], ssm_lambda_im[l], ssm_log_step[l],
        ssm_b_re[l], ssm_b_im[l], ssm_c_re[l], ssm_c_im[l])
    h0 = jnp.concatenate([state_ssm_re[l], state_ssm_im[l]], axis=-1)
    h0_pack = jnp.transpose(h0.reshape(bs // 2, 2, SSM_GROUPS, STATE_LANES), (2, 1, 0, 3))
    y4, hfin = _ssm(table_params, u3, h0_pack,
                    rows_per_seq=lp // ROW_TOKENS, prompt_rows=n_p // ROW_TOKENS)

    post_args = (lng, lnb, column(ssm_d[l]), w_glu[l].T.astype(BF16), column(b_glu[l]),
                 w_out[l].astype(BF16), row(ln1_g[l]), row(ln1_b[l]),
                 w_gate_up[l].astype(BF16), w_down[l].astype(BF16), row(ln2_g[l]), row(ln2_b[l]))
    out_p, out_s = _post(xp2, xs2, atp, ats, u3, y4.reshape(u3.shape), *post_args, alpha=alpha, tm=512)

    kv_shape = (N_KV_HEADS, HEAD_DIM)
    last_rows = lambda a: jnp.stack([a[(b + 1) * lp - win_rows:(b + 1) * lp] for b in range(bp)])
    win_k_p = last_rows(k).reshape(1, bp, win_rows, *kv_shape)
    win_v_p = last_rows(v).reshape(1, bp, win_rows, *kv_shape)
    win_k_s = jnp.concatenate([ck, ks3], axis=1)[:, -win_rows:].reshape(1, bs, win_rows, *kv_shape)
    win_v_s = jnp.concatenate([cv, vs3], axis=1)[:, -win_rows:].reshape(1, bs, win_rows, *kv_shape)
    sp = jnp.swapaxes(hfin[:, :bp], 0, 1)
    ss = jnp.transpose(hfin[:, STATE_OUT_ROWS:].reshape(SSM_GROUPS, 2, bs // 2, STATE_LANES),
                       (2, 1, 0, 3)).reshape(bs, SSM_GROUPS, STATE_LANES)
    return (out_p.reshape(bp, lp, D_MODEL), out_s.reshape(bs, ls, D_MODEL),
            win_k_p, win_v_p, sp[None, ..., :SSM_STATE], sp[None, ..., SSM_STATE:],
            win_k_s, win_v_s, ss[None, ..., :SSM_STATE], ss[None, ..., SSM_STATE:])
```

```python
import functools
import math

import jax
import jax.numpy as jnp
from jax import lax
from jax.experimental import pallas as pl
from jax.experimental.pallas import tpu as pltpu

F32 = jnp.float32
BF16 = jnp.bfloat16

D_MODEL = 1024
HEAD_DIM = 64
N_HEADS = 8
N_KV_HEADS = 2
Q_PER_KV = N_HEADS // N_KV_HEADS
CHUNK = 64
WINDOW = 128
WIN_CHUNKS = WINDOW // CHUNK
BAND = (WIN_CHUNKS + 1) * CHUNK
PROJ_Q = N_HEADS * HEAD_DIM
PROJ_KV = N_KV_HEADS * HEAD_DIM
SSM_WIDTH = 512
SSM_GROUP = 16
SSM_GROUPS = SSM_WIDTH // SSM_GROUP
SSM_STATE = 64
STATE_LANES = 2 * SSM_STATE
D_FF = 2816
D_IN_PROJ = PROJ_Q + 2 * PROJ_KV + SSM_WIDTH
LN_EPS = 1e-5
NEG_INF = -1e30

SSM_T = CHUNK
SSM_CW = SSM_T * SSM_GROUP
ROW_TOKENS = 2 * SSM_T
ROW_GROUP = 8
PROJ_TILE = ROW_GROUP * ROW_TOKENS
PROJ_PARTS = 2
COEF_ROWS = 16
STATE_OUT_ROWS = 8
M_BLOCK_CHANNELS = 4
SSM_GROUPS_PER_STEP = 4
FF_BLOCK = 256
V7X_VMEM_LIMIT_BYTES = 56 * 1024 * 1024


def _layer_norm(x, g, b):
    mu = jnp.mean(x, axis=-1, keepdims=True)
    xc = x - mu
    var = jnp.mean(xc * xc, axis=-1, keepdims=True)
    return xc * lax.rsqrt(var + LN_EPS) * g + b


KV = 2 * PROJ_KV
VU = PROJ_KV + SSM_WIDTH


def _proj_kernel(xp_ref, xs_ref, g_ref, b_ref, w_ref, wqt_ref, wvut_ref, qt_ref, vt_ref, ut_ref, k_ref, v_ref,
                 *, prompt_tiles):
    outs = (g_ref, b_ref, w_ref, wqt_ref, wvut_ref, qt_ref, vt_ref, ut_ref, k_ref, v_ref)

    @pl.when(pl.program_id(0) < prompt_tiles)
    def _():
        _proj_tile(xp_ref, *outs)

    @pl.when(pl.program_id(0) >= prompt_tiles)
    def _():
        _proj_tile(xs_ref, *outs)


def _proj_tile(x_ref, g_ref, b_ref, w_ref, wqt_ref, wvut_ref, qt_ref, vt_ref, ut_ref, k_ref, v_ref):
    rows_per_part = ROW_GROUP // PROJ_PARTS
    nt = (((1,), (1,)), ((), ()))
    for part in range(PROJ_PARTS):
        tok = slice(part * rows_per_part * ROW_TOKENS, (part + 1) * rows_per_part * ROW_TOKENS)
        xb = _layer_norm(x_ref[tok, :], g_ref[...], b_ref[...]).astype(BF16)
        p = jnp.dot(xb, w_ref[...], preferred_element_type=F32)
        k_ref[tok, :] = p[:, :PROJ_KV]
        v_ref[tok, :] = p[:, PROJ_KV:]
        qt = lax.dot_general(wqt_ref[...], xb, nt, preferred_element_type=F32)
        vut = lax.dot_general(wvut_ref[...], xb, nt, preferred_element_type=F32)
        qt_ref[:, tok] = (qt * (HEAD_DIM ** -0.5)).astype(BF16)
        vt_ref[:, tok] = vut[:PROJ_KV].astype(BF16)
        for j in range(rows_per_part):
            ut_ref[pl.ds(part * rows_per_part + j, SSM_WIDTH, stride=ROW_GROUP), :] = (
                vut[PROJ_KV:, j * ROW_TOKENS:(j + 1) * ROW_TOKENS])


def _proj(xp2d, xs2d, ln_g, ln_b, w_in, w_in_t):
    assert D_IN_PROJ == 2 * VU and PROJ_Q % KV == 0
    tm = PROJ_TILE
    tp, ts = xp2d.shape[0] // tm, xs2d.shape[0] // tm
    n = (tp + ts) * tm
    const = lambda i: (0, 0)
    row = lambda i: (i, 0)
    col = lambda i: (0, i)
    return pl.pallas_call(
        functools.partial(_proj_kernel, prompt_tiles=tp),
        grid=(tp + ts,),
        in_specs=[
            pl.BlockSpec((tm, D_MODEL), lambda i: (jnp.minimum(i, tp - 1), 0)),
            pl.BlockSpec((tm, D_MODEL), lambda i: (jnp.maximum(i - tp, 0), 0)),
            pl.BlockSpec((1, D_MODEL), const),
            pl.BlockSpec((1, D_MODEL), const),
            pl.BlockSpec((D_MODEL, KV), lambda i: (0, PROJ_Q // KV)),
            pl.BlockSpec((PROJ_Q, D_MODEL), const),
            pl.BlockSpec((VU, D_MODEL), lambda i: (1, 0)),
        ],
        out_specs=[
            pl.BlockSpec((PROJ_Q, tm), col),
            pl.BlockSpec((PROJ_KV, tm), col),
            pl.BlockSpec((None, SSM_WIDTH * ROW_GROUP, ROW_TOKENS), lambda i: (i, 0, 0)),
            pl.BlockSpec((tm, PROJ_KV), row),
            pl.BlockSpec((tm, PROJ_KV), row),
        ],
        out_shape=[
            jax.ShapeDtypeStruct((PROJ_Q, n), BF16),
            jax.ShapeDtypeStruct((PROJ_KV, n), BF16),
            jax.ShapeDtypeStruct((n // tm, SSM_WIDTH * ROW_GROUP, ROW_TOKENS), F32),
            jax.ShapeDtypeStruct((n, PROJ_KV), F32),
            jax.ShapeDtypeStruct((n, PROJ_KV), F32),
        ],
        compiler_params=pltpu.CompilerParams(
            dimension_semantics=("arbitrary",), vmem_limit_bytes=V7X_VMEM_LIMIT_BYTES),
        name="proj",
    )(xp2d, xs2d, ln_g, ln_b, w_in, w_in_t, w_in_t)


PAIR = 2 * CHUNK
HEAD_LANES = Q_PER_KV * PAIR
ONES_ROWS = 16
ATTN_PAIRS = 16


def _scores(unit, qt_ref):
    kwin, _, lanes, _, _, h = unit
    base = h * Q_PER_KV * HEAD_DIM
    qrow = jnp.concatenate(
        [qt_ref[base + g * HEAD_DIM:base + (g + 1) * HEAD_DIM, lanes] for g in range(Q_PER_KV)],
        axis=1)
    zero = jnp.zeros_like(qrow)
    qstack = jnp.concatenate([qrow, zero] if h == 0 else [zero, qrow], axis=0)
    return jnp.dot(kwin, qstack, preferred_element_type=F32)


def _finish(unit, s, sink_ref, o_ref):
    _, vtwin, lanes, valid, masked_rows, h = unit
    nk = s.shape[0]
    base = h * Q_PER_KV * HEAD_DIM
    pieces, done = [], 0
    for start, stop in masked_rows:
        if start > done:
            pieces.append(s[done:start])
        pieces.append(jnp.where(valid[start:stop], s[start:stop], NEG_INF))
        done = stop
    if done < nk:
        pieces.append(s[done:])
    s = jnp.concatenate(pieces, axis=0)
    sink = sink_ref[h]
    m = jnp.maximum(jnp.max(s, axis=0, keepdims=True), sink)
    p = jnp.exp(s - m).astype(BF16)
    v_ones = jnp.concatenate(
        [vtwin[h * HEAD_DIM:(h + 1) * HEAD_DIM, :], jnp.ones((ONES_ROWS, nk), BF16)], axis=0)
    ov = jnp.dot(v_ones, p, preferred_element_type=F32)
    den = ov[HEAD_DIM:HEAD_DIM + 1, :] + jnp.exp(sink - m)
    o = ov[:HEAD_DIM, :] * (1.0 / den)
    for g in range(Q_PER_KV):
        o_ref[base + g * HEAD_DIM:base + (g + 1) * HEAD_DIM, lanes] = (
            o[:, g * PAIR:(g + 1) * PAIR].astype(BF16))


def _attend_units(units, qt_ref, sink_ref, o_ref):
    s_next = _scores(units[0], qt_ref)
    for i, unit in enumerate(units):
        s = s_next
        if i + 1 < len(units):
            s_next = _scores(units[i + 1], qt_ref)
        _finish(unit, s, sink_ref, o_ref)


def _attn_prompt_kernel(sink_ref, qt_ref, kp_ref, kc_ref, vtp_ref, vtc_ref, o_ref, *, n_pairs):
    tq = n_pairs * PAIR
    nk = WINDOW + PAIR
    kk = jnp.concatenate([kp_ref[...], kc_ref[...]], axis=0).astype(BF16)
    vt = jnp.concatenate([vtp_ref[...], vtc_ref[...]], axis=1)
    r = lax.broadcasted_iota(jnp.int32, (nk, HEAD_LANES), 0)
    first_chunk = (lax.broadcasted_iota(jnp.int32, (nk, HEAD_LANES), 1) & (PAIR - 1)) < CHUNK
    lo = jnp.where(first_chunk, 0, CHUNK)
    hi = jnp.where(first_chunk, BAND, nk)
    first_pos = pl.program_id(1) * tq - WINDOW
    units = []
    for pp in range(n_pairs):
        lo_pp = jnp.maximum(lo, -first_pos) if pp == 0 else lo
        valid = (r >= lo_pp) & (r < hi)
        masked_rows = ((0, nk),) if pp == 0 else ((0, CHUNK), (nk - CHUNK, nk))
        for h in range(N_KV_HEADS):
            units.append((kk[pp * PAIR:pp * PAIR + nk], vt[:, pp * PAIR:pp * PAIR + nk],
                          slice(pp * PAIR, (pp + 1) * PAIR), valid, masked_rows, h))
    _attend_units(units, qt_ref, sink_ref, o_ref)


def _attn_sample_kernel(sink_ref, qt_ref, ck_ref, kn_ref, cvt_ref, vtn_ref, o_ref, *, n_pairs):
    nk = 2 * WINDOW + PAIR
    r = lax.broadcasted_iota(jnp.int32, (nk, HEAD_LANES), 0)
    query_seq = (lax.broadcasted_iota(jnp.int32, (nk, HEAD_LANES), 1) & (PAIR - 1)) >> 6
    key_seq = jnp.where(r < 2 * WINDOW, r >> 7, (r - 2 * WINDOW) >> 6)
    valid = query_seq == key_seq
    units = []
    for pp in range(n_pairs):
        lanes = slice(pp * PAIR, (pp + 1) * PAIR)
        kwin = jnp.concatenate([ck_ref[2 * pp], ck_ref[2 * pp + 1], kn_ref[lanes, :]],
                               axis=0).astype(BF16)
        vtwin = jnp.concatenate([cvt_ref[:, 2 * pp * WINDOW:2 * (pp + 1) * WINDOW], vtn_ref[:, lanes]],
                                axis=1)
        units.extend((kwin, vtwin, lanes, valid, ((0, nk),), h) for h in range(N_KV_HEADS))
    _attend_units(units, qt_ref, sink_ref, o_ref)


_SINK_SPEC = pl.BlockSpec((N_KV_HEADS, 1, HEAD_LANES), lambda *_: (0, 0, 0))


def _attention_prompt(sink_rows, qt, k, vt, bsz, seq, n_pairs):
    tq = n_pairs * PAIR
    nt = seq // tq
    wpt = tq // WINDOW
    cur_c = lambda b, i: (0, b * nt + i)
    cur_r = lambda b, i: (b * nt + i, 0)
    prev = lambda b, i: jnp.maximum((b * nt + i) * wpt - 1, 0)
    return pl.pallas_call(
        functools.partial(_attn_prompt_kernel, n_pairs=n_pairs),
        grid=(bsz, nt),
        in_specs=[
            _SINK_SPEC,
            pl.BlockSpec((PROJ_Q, tq), cur_c),
            pl.BlockSpec((WINDOW, PROJ_KV), lambda b, i: (prev(b, i), 0)),
            pl.BlockSpec((tq, PROJ_KV), cur_r),
            pl.BlockSpec((PROJ_KV, WINDOW), lambda b, i: (0, prev(b, i))),
            pl.BlockSpec((PROJ_KV, tq), cur_c),
        ],
        out_specs=pl.BlockSpec((PROJ_Q, tq), cur_c),
        out_shape=jax.ShapeDtypeStruct((PROJ_Q, bsz * seq), BF16),
        compiler_params=pltpu.CompilerParams(dimension_semantics=("arbitrary", "arbitrary")),
        name="attn_prompt",
    )(sink_rows, qt, k, k, vt, vt)


def _attention_sample(sink_rows, qt, cache_k, k, cache_vt, vt, first_token, n_pairs):
    n = cache_k.shape[0] * CHUNK
    tq = n_pairs * PAIR
    assert first_token % tq == 0
    off = first_token // tq
    return pl.pallas_call(
        functools.partial(_attn_sample_kernel, n_pairs=n_pairs),
        grid=(n // tq,),
        in_specs=[
            _SINK_SPEC,
            pl.BlockSpec((PROJ_Q, tq), lambda i: (0, i + off)),
            pl.BlockSpec((2 * n_pairs, WINDOW, PROJ_KV), lambda i: (i, 0, 0)),
            pl.BlockSpec((tq, PROJ_KV), lambda i: (i + off, 0)),
            pl.BlockSpec((PROJ_KV, 2 * n_pairs * WINDOW), lambda i: (0, i)),
            pl.BlockSpec((PROJ_KV, tq), lambda i: (0, i + off)),
        ],
        out_specs=pl.BlockSpec((PROJ_Q, tq), lambda i: (0, i)),
        out_shape=jax.ShapeDtypeStruct((PROJ_Q, n), BF16),
        compiler_params=pltpu.CompilerParams(dimension_semantics=("arbitrary",)),
        name="attn_sample",
    )(sink_rows, qt, cache_k, k, cache_vt, vt)


def _power_table(tau, nbits, a_re, a_im):
    rows = tau.shape[0]
    w_re = jnp.ones((rows, STATE_LANES), F32)
    w_im = jnp.zeros((rows, STATE_LANES), F32)
    p_re, p_im = a_re, a_im
    for k in range(nbits):
        bit = ((tau >> k) & 1) == 1
        f_re = jnp.where(bit, p_re, 1.0)
        f_im = jnp.where(bit, p_im, 0.0)
        w_re, w_im = w_re * f_re - w_im * f_im, w_re * f_im + w_im * f_re
        p_re, p_im = p_re * p_re - p_im * p_im, 2.0 * p_re * p_im
    return w_re, w_im


def _build_tables(par_ref, p_ref, qt_ref, coef_ref):
    lo = lax.broadcasted_iota(jnp.int32, (1, STATE_LANES), 1) < SSM_STATE
    mat = lambda i: par_ref[PAR_MATS + i * SSM_GROUP:PAR_MATS + (i + 1) * SSM_GROUP, :]
    lr, li = par_ref[0:1, :], par_ref[1:2, :]
    dt = jnp.exp(par_ref[2:3, :])
    mag = jnp.exp(lr * dt)
    a_re, a_im = mag * jnp.cos(li * dt), mag * jnp.sin(li * dt)
    nr, ni = a_re - 1.0, a_im
    den = lr * lr + li * li
    f_re, f_im = (nr * lr + ni * li) / den, (ni * lr - nr * li) / den
    b_re, b_im = mat(0), mat(1)
    bb_re = f_re * b_re - f_im * b_im
    bb_im = f_re * b_im + f_im * b_re
    c_re, c_im = mat(2), mat(3)

    tau = lax.broadcasted_iota(jnp.int32, (SSM_T, 1), 0)
    w_re, w_im = _power_table(tau, 6, a_re, a_im)
    w1_re, w1_im = w_re * a_re - w_im * a_im, w_re * a_im + w_im * a_re
    wr_re, wr_im = _power_table(SSM_T - 1 - tau, 6, a_re, a_im)

    def outer(c, w):
        return (c[:, None, :] * w[None, :, :]).reshape(SSM_CW, STATE_LANES)

    cw_mix = (outer(c_re, jnp.where(lo, w_re, w_im)) + outer(c_im, jnp.where(lo, -w_im, w_re)))
    bb_mix = jnp.where(lo, bb_re, -bb_im)
    strip = lax.dot_general(bb_mix, cw_mix, (((1,), (1,)), ((), ())),
                            precision=lax.Precision.HIGHEST,
                            preferred_element_type=F32)

    x_a, y_a = jnp.where(lo, bb_re, bb_im), jnp.where(lo, -bb_im, bb_re)
    p_ref[...] = (outer(x_a, wr_re) + outer(y_a, wr_im)).astype(BF16)

    qt_ref[...] = (outer(c_re, jnp.where(lo, w1_re, -w1_im))
                   + outer(c_im, jnp.where(lo, -w1_im, -w1_re))).astype(BF16)

    t_re, t_im = a_re, a_im
    for _ in range(6):
        t_re, t_im = t_re * t_re - t_im * t_im, 2.0 * t_re * t_im
    for k in range(COEF_ROWS // 2):
        coef_ref[2 * k:2 * k + 1, :] = t_re
        coef_ref[2 * k + 1:2 * k + 2, :] = jnp.where(lo, -t_im, t_im)
        t_re, t_im = t_re * t_re - t_im * t_im, 2.0 * t_re * t_im
    return strip


def _toeplitz_rows(strip, m_ref, channels):
    s_idx = lax.broadcasted_iota(jnp.int32, (SSM_T, SSM_CW), 0)
    t_idx = lax.broadcasted_iota(jnp.int32, (SSM_T, SSM_CW), 1) & (SSM_T - 1)
    causal = t_idx >= s_idx
    for c in channels:
        rows = jnp.broadcast_to(strip[c:c + 1, :], (SSM_T, SSM_CW))
        shifted = pltpu.roll(rows, 0, 1, stride=1, stride_axis=0)
        m_ref[c * SSM_T:(c + 1) * SSM_T, :] = jnp.where(causal, shifted, 0.0).astype(BF16)


PAR_MATS = 8
PAR_ROWS = PAR_MATS + 4 * SSM_GROUP


def _table_params(lam_re, lam_im, log_step, b_re, b_im, c_re, c_im):
    step = jnp.broadcast_to(log_step[:, None, None], (SSM_GROUPS, 1, SSM_STATE))
    filler = jnp.zeros((SSM_GROUPS, PAR_MATS - 3, SSM_STATE), F32)
    pack = jnp.concatenate([lam_re[:, None, :], lam_im[:, None, :], step, filler,
                            jnp.swapaxes(b_re, 1, 2), jnp.swapaxes(b_im, 1, 2), c_re, c_im], axis=1)
    return jnp.concatenate([pack, pack], axis=-1)


def _cmul(a1, a2, h, hs):
    return a1 * h + a2 * hs, a1 * hs - a2 * h


def _ssm_kernel(par_ref, u_ref, h0_ref, y_ref, hfin_ref, m_ref, p_ref, qt_ref, coef_ref,
                *, rows_per_seq, prompt_rows):
    for gi in range(SSM_GROUPS_PER_STEP):
        _ssm_group(par_ref.at[gi], u_ref, gi * SSM_GROUP, h0_ref.at[gi], y_ref, hfin_ref.at[gi],
                   m_ref.at[gi], p_ref.at[gi], qt_ref.at[gi], coef_ref.at[gi],
                   rows_per_seq=rows_per_seq, prompt_rows=prompt_rows)


def _ssm_group(par_ref, u_ref, c0, h0_ref, y_ref, hfin_ref, m_ref, p_ref, qt_ref, coef_ref,
               *, rows_per_seq, prompt_rows):
    strip = _build_tables(par_ref, p_ref, qt_ref, coef_ref)
    lo = lax.broadcasted_iota(jnp.int32, (1, ROW_TOKENS), 1) < SSM_T

    def chunk_rows(ref):
        even, odd = [], []
        rows = ref.shape[0] * ROW_GROUP
        for k in range(SSM_GROUP // 2):
            a = ref[:, c0 + 2 * k].reshape(rows, ROW_TOKENS)
            b = ref[:, c0 + 2 * k + 1].reshape(rows, ROW_TOKENS)
            even.append(jnp.where(lo, a, pltpu.roll(b, SSM_T, 1)))
            odd.append(jnp.where(lo, pltpu.roll(a, SSM_T, 1), b))
        return jnp.concatenate(even, axis=1), jnp.concatenate(odd, axis=1)

    def store_rows(y_even, y_odd, ref):
        for k in range(SSM_GROUP // 2):
            te = y_even[:, k * ROW_TOKENS:(k + 1) * ROW_TOKENS]
            to = y_odd[:, k * ROW_TOKENS:(k + 1) * ROW_TOKENS]
            tiles = (ref.shape[0], ROW_GROUP, ROW_TOKENS)
            ref[:, c0 + 2 * k] = jnp.where(lo, te, pltpu.roll(to, SSM_T, 1)).reshape(tiles)
            ref[:, c0 + 2 * k + 1] = jnp.where(lo, pltpu.roll(te, SSM_T, 1), to).reshape(tiles)

    even, odd = chunk_rows(u_ref)
    rp, rs = prompt_rows, even.shape[0] - prompt_rows
    u = jnp.concatenate([even[:rp], odd[:rp], even[rp:], odd[rp:]], axis=0).astype(BF16)
    s1 = jnp.dot(u, p_ref[...], preferred_element_type=F32)
    s2 = pltpu.roll(s1, SSM_STATE, 1)
    a1, a2 = coef_ref[0:1, :], coef_ref[1:2, :]

    e1, e2, o1, o2 = s1[:rp], s2[:rp], s1[rp:2 * rp], s2[rp:2 * rp]
    x1, x2 = _cmul(a1, a2, e1, e2)
    x1, x2 = x1 + o1, x2 + o2
    pos = lax.broadcasted_iota(jnp.int32, (rp, STATE_LANES), 0) & (rows_per_seq - 1)
    for k in range(rows_per_seq.bit_length() - 1):
        d = 1 << k
        b1, b2 = coef_ref[2 + 2 * k:3 + 2 * k, :], coef_ref[3 + 2 * k:4 + 2 * k, :]
        sh1 = jnp.where(pos >= d, pltpu.roll(x1, d, 0), 0.0)
        sh2 = jnp.where(pos >= d, pltpu.roll(x2, d, 0), 0.0)
        y1, y2 = _cmul(b1, b2, sh1, sh2)
        x1, x2 = x1 + y1, x2 + y2
    g1 = jnp.where(pos >= 1, pltpu.roll(x1, 1, 0), 0.0)
    g2 = jnp.where(pos >= 1, pltpu.roll(x2, 1, 0), 0.0)
    ho1 = _cmul(a1, a2, g1, g2)[0] + e1
    hfin_ref[0:STATE_OUT_ROWS, :] = jnp.zeros((STATE_OUT_ROWS, STATE_LANES), F32)
    for b in range(rp // rows_per_seq):
        last = (b + 1) * rows_per_seq - 1
        hfin_ref[b:b + 1, :] = x1[last:last + 1, :]

    h0e, h0o = h0_ref[0], h0_ref[1]
    swap = lambda h: pltpu.roll(h, SSM_STATE, 1)
    hfin_ref[STATE_OUT_ROWS:STATE_OUT_ROWS + rs, :] = a1 * h0e + a2 * swap(h0e) + s1[2 * rp:2 * rp + rs]
    hfin_ref[STATE_OUT_ROWS + rs:, :] = a1 * h0o + a2 * swap(h0o) + s1[2 * rp + rs:]

    hprev = jnp.concatenate([g1, ho1, h0e, h0o], axis=0).astype(BF16)
    y = lax.dot_general(hprev, qt_ref[...], (((1,), (1,)), ((), ())), preferred_element_type=F32)
    blocks = [range(c, c + M_BLOCK_CHANNELS) for c in range(0, SSM_GROUP, M_BLOCK_CHANNELS)]
    _toeplitz_rows(strip, m_ref, blocks[0])
    for i, channels in enumerate(blocks):
        rows = slice(channels[0] * SSM_T, (channels[-1] + 1) * SSM_T)
        y += jnp.dot(u[:, rows], m_ref[rows, :], preferred_element_type=F32)
        if i + 1 < len(blocks):
            _toeplitz_rows(strip, m_ref, blocks[i + 1])
    store_rows(jnp.concatenate([y[:rp], y[2 * rp:2 * rp + rs]], axis=0),
               jnp.concatenate([y[rp:2 * rp], y[2 * rp + rs:]], axis=0), y_ref)


def _ssm(table_params, u3, h0_pack, *, rows_per_seq, prompt_rows):
    groups = u3.shape[0]
    rs = groups * ROW_GROUP - prompt_rows
    u4 = u3.reshape(groups, SSM_WIDTH, ROW_GROUP, ROW_TOKENS)
    assert rows_per_seq & (rows_per_seq - 1) == 0 and 2 * rows_per_seq.bit_length() <= COEF_ROWS
    gps = SSM_GROUPS_PER_STEP
    g3 = lambda g: (g, 0, 0)
    u_spec = pl.BlockSpec((groups, gps * SSM_GROUP, ROW_GROUP, ROW_TOKENS), lambda g: (0, g, 0, 0))
    fin_rows = STATE_OUT_ROWS + 2 * rs
    return pl.pallas_call(
        functools.partial(_ssm_kernel, rows_per_seq=rows_per_seq, prompt_rows=prompt_rows),
        grid=(SSM_GROUPS // gps,),
        in_specs=[
            pl.BlockSpec((gps, PAR_ROWS, STATE_LANES), g3),
            u_spec,
            pl.BlockSpec((gps, 2, rs, STATE_LANES), lambda g: (g, 0, 0, 0)),
        ],
        out_specs=[
            u_spec,
            pl.BlockSpec((gps, fin_rows, STATE_LANES), g3),
        ],
        out_shape=[
            jax.ShapeDtypeStruct(u4.shape, F32),
            jax.ShapeDtypeStruct((SSM_GROUPS, fin_rows, STATE_LANES), F32),
        ],
        scratch_shapes=[
            pltpu.VMEM((gps, SSM_CW, SSM_CW), BF16),
            pltpu.VMEM((gps, SSM_CW, STATE_LANES), BF16),
            pltpu.VMEM((gps, SSM_CW, STATE_LANES), BF16),
            pltpu.VMEM((gps, COEF_ROWS, STATE_LANES), F32),
        ],
        compiler_params=pltpu.CompilerParams(
            dimension_semantics=("arbitrary",), vmem_limit_bytes=V7X_VMEM_LIMIT_BYTES),
        name="ssm",
    )(table_params, u4, h0_pack)


def _post_kernel(xp_ref, xs_ref, atp_ref, ats_ref, ut_ref, yt_ref, *refs, alpha, prompt_tiles):
    weights, (op_ref, os_ref) = refs[:-2], refs[-2:]

    @pl.when(pl.program_id(0) < prompt_tiles)
    def _():
        _post_tile(xp_ref, atp_ref, ut_ref, yt_ref, *weights, op_ref, alpha=alpha)

    @pl.when(pl.program_id(0) >= prompt_tiles)
    def _():
        _post_tile(xs_ref, ats_ref, ut_ref, yt_ref, *weights, os_ref, alpha=alpha)


def _post_tile(x_ref, at_ref, ut_ref, yt_ref, lng_ref, lnb_ref, d_ref, wglut_ref, bglu_ref,
               wout_ref, ln1g_ref, ln1b_ref, wgu_ref, wdown_ref, ln2g_ref, ln2b_ref, o_ref,
               *, alpha):
    xn = _layer_norm(x_ref[...], lng_ref[...], lnb_ref[...])
    n_rows = x_ref.shape[0] // ROW_TOKENS
    first_row = (pl.program_id(0) % (ROW_GROUP // n_rows)) * n_rows

    def feature_major(ref):
        return jnp.concatenate(
            [ref[pl.ds(first_row + j, SSM_WIDTH, stride=ROW_GROUP), :] for j in range(n_rows)], axis=1)

    ys = feature_major(yt_ref) + d_ref[...] * feature_major(ut_ref)
    gl = 0.5 * ys * (1.0 + lax.erf(ys * math.sqrt(0.5)))
    z = jnp.dot(wglut_ref[...], gl.astype(BF16), preferred_element_type=F32) + bglu_ref[...]
    s = gl * jax.nn.sigmoid(z)
    tn = (((0,), (0,)), ((), ()))
    mix = lax.dot_general(at_ref[...], wout_ref[:PROJ_Q, :], tn, preferred_element_type=F32)
    mix += lax.dot_general(s.astype(BF16), wout_ref[PROJ_Q:, :], tn, preferred_element_type=F32)
    h = _layer_norm(alpha * xn + mix, ln1g_ref[...], ln1b_ref[...])
    hb = h.astype(BF16)
    f = jnp.zeros_like(h)
    for start in range(0, D_FF, FF_BLOCK):
        stop = min(start + FF_BLOCK, D_FF)
        cols = slice(start, stop)
        up_cols = slice(D_FF + start, D_FF + stop)
        g = jnp.dot(hb, wgu_ref[:, cols], preferred_element_type=F32)
        up = jnp.dot(hb, wgu_ref[:, up_cols], preferred_element_type=F32)
        act = (g * jax.nn.sigmoid(g)) * up
        f += jnp.dot(act.astype(BF16), wdown_ref[cols, :], preferred_element_type=F32)
    o_ref[...] = _layer_norm(alpha * h + f, ln2g_ref[...], ln2b_ref[...])


def _post(xp2d, xs2d, atp, ats, ut3, yt3, ln_g, ln_b, d_col, w_glu_t, b_glu_col, w_out, ln1_g, ln1_b,
          w_gate_up, w_down, ln2_g, ln2_b, *, alpha, tm):
    n_p, n_s = xp2d.shape[0], xs2d.shape[0]
    tp, ts = n_p // tm, n_s // tm
    assert PROJ_TILE % tm == 0 and n_p % PROJ_TILE == 0
    group = lambda i: (i // (PROJ_TILE // tm), 0, 0)
    group_spec = pl.BlockSpec((None, SSM_WIDTH * ROW_GROUP, ROW_TOKENS), group)
    prompt = lambda i: jnp.minimum(i, tp - 1)
    sample = lambda i: jnp.maximum(i - tp, 0)
    const = lambda i: (0, 0)
    resident = lambda shape: pl.BlockSpec(shape, const, pipeline_mode=pl.Buffered(1))
    vec = lambda width: pl.BlockSpec((1, width), const)
    colvec = pl.BlockSpec((SSM_WIDTH, 1), const)
    return pl.pallas_call(
        functools.partial(_post_kernel, alpha=alpha, prompt_tiles=tp),
        grid=(tp + ts,),
        in_specs=[
            pl.BlockSpec((tm, D_MODEL), lambda i: (prompt(i), 0)),
            pl.BlockSpec((tm, D_MODEL), lambda i: (sample(i), 0)),
            pl.BlockSpec((PROJ_Q, tm), lambda i: (0, prompt(i))),
            pl.BlockSpec((PROJ_Q, tm), lambda i: (0, sample(i))),
            group_spec, group_spec,
            vec(D_MODEL), vec(D_MODEL), colvec,
            resident((SSM_WIDTH, SSM_WIDTH)), colvec,
            resident((D_MODEL, D_MODEL)), vec(D_MODEL), vec(D_MODEL),
            resident((D_MODEL, 2 * D_FF)), resident((D_FF, D_MODEL)),
            vec(D_MODEL), vec(D_MODEL),
        ],
        out_specs=[
            pl.BlockSpec((tm, D_MODEL), lambda i: (prompt(i), 0)),
            pl.BlockSpec((tm, D_MODEL), lambda i: (sample(i), 0)),
        ],
        out_shape=[
            jax.ShapeDtypeStruct((n_p, D_MODEL), F32),
            jax.ShapeDtypeStruct((n_s, D_MODEL), F32),
        ],
        compiler_params=pltpu.CompilerParams(
            dimension_semantics=("arbitrary",), vmem_limit_bytes=V7X_VMEM_LIMIT_BYTES),
        name="post",
    )(xp2d, xs2d, atp, ats, ut3, yt3, ln_g, ln_b, d_col, w_glu_t, b_glu_col, w_out, ln1_g, ln1_b,
      w_gate_up, w_down, ln2_g, ln2_b)


def kernel(x_prompt, x_sample, cache_win_k, cache_win_v, state_ssm_re, state_ssm_im,
           ln_in_g, ln_in_b, w_in, attn_sinks, ssm_lambda_re, ssm_lambda_im, ssm_log_step,
           ssm_b_re, ssm_b_im, ssm_c_re, ssm_c_im, ssm_d, w_glu, b_glu, w_out,
           ln1_g, ln1_b, w_gate_up, w_down, ln2_g, ln2_b):
    depth = w_in.shape[0]
    assert depth == 1, "single-layer step"
    bp, lp, _ = x_prompt.shape
    bs, ls, _ = x_sample.shape
    assert ls == SSM_T and bs % 2 == 0 and lp % 512 == 0 and bp <= STATE_OUT_ROWS
    win_rows = cache_win_k.shape[2]
    assert win_rows == WINDOW
    alpha = (2.0 * depth) ** 0.25
    l = 0
    row = lambda a: a.reshape(1, -1)
    column = lambda a: a.reshape(-1, 1)

    lng, lnb = row(ln_in_g), row(ln_in_b)
    xp2 = x_prompt.reshape(bp * lp, D_MODEL)
    xs2 = x_sample.reshape(bs * ls, D_MODEL)
    n_p = bp * lp
    qt, vt, u3, k, v = _proj(xp2, xs2, lng, lnb, w_in[l].astype(BF16), w_in[l].T.astype(BF16))

    sink_rows = jnp.repeat(attn_sinks[l].reshape(N_KV_HEADS, Q_PER_KV), PAIR, axis=1)[:, None, :]
    atp = _attention_prompt(sink_rows, qt, k, vt, bp, lp, n_pairs=ATTN_PAIRS)
    ck = cache_win_k[l].reshape(bs, win_rows, PROJ_KV)
    cv = cache_win_v[l].reshape(bs, win_rows, PROJ_KV)
    cvt = jnp.transpose(cv, (2, 0, 1)).reshape(PROJ_KV, bs * win_rows).astype(BF16)
    ats = _attention_sample(sink_rows, qt, ck, k, cvt, vt, first_token=n_p, n_pairs=ATTN_PAIRS)
    ks3, vs3 = k[n_p:].reshape(bs, ls, PROJ_KV), v[n_p:].reshape(bs, ls, PROJ_KV)

    table_params = _table_params(
        ssm_lambda_re[l], ssm_lambda_im[l], ssm_log_step[l],
        ssm_b_re[l], ssm_b_im[l], ssm_c_re[l], ssm_c_im[l])
    h0 = jnp.concatenate([state_ssm_re[l], state_ssm_im[l]], axis=-1)
    h0_pack = jnp.transpose(h0.reshape(bs // 2, 2, SSM_GROUPS, STATE_LANES), (2, 1, 0, 3))
    y4, hfin = _ssm(table_params, u3, h0_pack,
                    rows_per_seq=lp // ROW_TOKENS, prompt_rows=n_p // ROW_TOKENS)

    post_args = (lng, lnb, column(ssm_d[l]), w_glu[l].T.astype(BF16), column(b_glu[l]),
                 w_out[l].astype(BF16), row(ln1_g[l]), row(ln1_b[l]),
                 w_gate_up[l].astype(BF16), w_down[l].astype(BF16), row(ln2_g[l]), row(ln2_b[l]))
    out_p, out_s = _post(xp2, xs2, atp, ats, u3, y4.reshape(u3.shape), *post_args, alpha=alpha, tm=512)

    kv_shape = (N_KV_HEADS, HEAD_DIM)
    last_rows = lambda a: jnp.stack([a[(b + 1) * lp - win_rows:(b + 1) * lp] for b in range(bp)])
    win_k_p = last_rows(k).reshape(1, bp, win_rows, *kv_shape)
    win_v_p = last_rows(v).reshape(1, bp, win_rows, *kv_shape)
    win_k_s = jnp.concatenate([ck, ks3], axis=1)[:, -win_rows:].reshape(1, bs, win_rows, *kv_shape)
    win_v_s = jnp.concatenate([cv, vs3], axis=1)[:, -win_rows:].reshape(1, bs, win_rows, *kv_shape)
    sp = jnp.swapaxes(hfin[:, :bp], 0, 1)
    ss = jnp.transpose(hfin[:, STATE_OUT_ROWS:].reshape(SSM_GROUPS, 2, bs // 2, STATE_LANES),
                       (2, 1, 0, 3)).reshape(bs, SSM_GROUPS, STATE_LANES)
    return (out_p.reshape(bp, lp, D_MODEL), out_s.reshape(bs, ls, D_MODEL),
            win_k_p, win_v_p, sp[None, ..., :SSM_STATE], sp[None, ..., SSM_STATE:],
            win_k_s, win_v_s, ss[None, ..., :SSM_STATE], ss[None, ..., SSM_STATE:])
```

```python
import functools
import math

import jax
import jax.numpy as jnp
from jax import lax
from jax.experimental import pallas as pl
from jax.experimental.pallas import tpu as pltpu

F32 = jnp.float32
BF16 = jnp.bfloat16

D_MODEL = 1024
HEAD_DIM = 64
N_HEADS = 8
N_KV_HEADS = 2
Q_PER_KV = N_HEADS // N_KV_HEADS
CHUNK = 64
WINDOW = 128
WIN_CHUNKS = WINDOW // CHUNK
BAND = (WIN_CHUNKS + 1) * CHUNK
PROJ_Q = N_HEADS * HEAD_DIM
PROJ_KV = N_KV_HEADS * HEAD_DIM
SSM_WIDTH = 512
SSM_GROUP = 16
SSM_GROUPS = SSM_WIDTH // SSM_GROUP
SSM_STATE = 64
STATE_LANES = 2 * SSM_STATE
D_FF = 2816
D_IN_PROJ = PROJ_Q + 2 * PROJ_KV + SSM_WIDTH
LN_EPS = 1e-5
NEG_INF = -1e30

SSM_T = CHUNK
SSM_CW = SSM_T * SSM_GROUP
ROW_TOKENS = 2 * SSM_T
ROW_GROUP = 8
PROJ_TILE = ROW_GROUP * ROW_TOKENS
PROJ_PARTS = 2
COEF_ROWS = 16
STATE_OUT_ROWS = 8
M_BLOCK_CHANNELS = 4
SSM_GROUPS_PER_STEP = 2
FF_BLOCK = 256
V7X_VMEM_LIMIT_BYTES = 56 * 1024 * 1024


def _layer_norm(x, g, b):
    mu = jnp.mean(x, axis=-1, keepdims=True)
    xc = x - mu
    var = jnp.mean(xc * xc, axis=-1, keepdims=True)
    return xc * lax.rsqrt(var + LN_EPS) * g + b


KV = 2 * PROJ_KV
VU = PROJ_KV + SSM_WIDTH


def _proj_kernel(xp_ref, xs_ref, g_ref, b_ref, w_ref, wqt_ref, wvut_ref, qt_ref, vt_ref, ut_ref, k_ref, v_ref,
                 *, prompt_tiles):
    outs = (g_ref, b_ref, w_ref, wqt_ref, wvut_ref, qt_ref, vt_ref, ut_ref, k_ref, v_ref)

    @pl.when(pl.program_id(0) < prompt_tiles)
    def _():
        _proj_tile(xp_ref, *outs)

    @pl.when(pl.program_id(0) >= prompt_tiles)
    def _():
        _proj_tile(xs_ref, *outs)


def _proj_tile(x_ref, g_ref, b_ref, w_ref, wqt_ref, wvut_ref, qt_ref, vt_ref, ut_ref, k_ref, v_ref):
    rows_per_part = ROW_GROUP // PROJ_PARTS
    nt = (((1,), (1,)), ((), ()))
    for part in range(PROJ_PARTS):
        tok = slice(part * rows_per_part * ROW_TOKENS, (part + 1) * rows_per_part * ROW_TOKENS)
        xb = _layer_norm(x_ref[tok, :], g_ref[...], b_ref[...]).astype(BF16)
        p = jnp.dot(xb, w_ref[...], preferred_element_type=F32)
        k_ref[tok, :] = p[:, :PROJ_KV]
        v_ref[tok, :] = p[:, PROJ_KV:]
        qt = lax.dot_general(wqt_ref[...], xb, nt, preferred_element_type=F32)
        vut = lax.dot_general(wvut_ref[...], xb, nt, preferred_element_type=F32)
        qt_ref[:, tok] = (qt * (HEAD_DIM ** -0.5)).astype(BF16)
        vt_ref[:, tok] = vut[:PROJ_KV].astype(BF16)
        for j in range(rows_per_part):
            ut_ref[pl.ds(part * rows_per_part + j, SSM_WIDTH, stride=ROW_GROUP), :] = (
                vut[PROJ_KV:, j * ROW_TOKENS:(j + 1) * ROW_TOKENS])


def _proj(xp2d, xs2d, ln_g, ln_b, w_in, w_in_t):
    assert D_IN_PROJ == 2 * VU and PROJ_Q % KV == 0
    tm = PROJ_TILE
    tp, ts = xp2d.shape[0] // tm, xs2d.shape[0] // tm
    n = (tp + ts) * tm
    const = lambda i: (0, 0)
    row = lambda i: (i, 0)
    col = lambda i: (0, i)
    return pl.pallas_call(
        functools.partial(_proj_kernel, prompt_tiles=tp),
        grid=(tp + ts,),
        in_specs=[
            pl.BlockSpec((tm, D_MODEL), lambda i: (jnp.minimum(i, tp - 1), 0)),
            pl.BlockSpec((tm, D_MODEL), lambda i: (jnp.maximum(i - tp, 0), 0)),
            pl.BlockSpec((1, D_MODEL), const),
            pl.BlockSpec((1, D_MODEL), const),
            pl.BlockSpec((D_MODEL, KV), lambda i: (0, PROJ_Q // KV)),
            pl.BlockSpec((PROJ_Q, D_MODEL), const),
            pl.BlockSpec((VU, D_MODEL), lambda i: (1, 0)),
        ],
        out_specs=[
            pl.BlockSpec((PROJ_Q, tm), col),
            pl.BlockSpec((PROJ_KV, tm), col),
            pl.BlockSpec((None, SSM_WIDTH * ROW_GROUP, ROW_TOKENS), lambda i: (i, 0, 0)),
            pl.BlockSpec((tm, PROJ_KV), row),
            pl.BlockSpec((tm, PROJ_KV), row),
        ],
        out_shape=[
            jax.ShapeDtypeStruct((PROJ_Q, n), BF16),
            jax.ShapeDtypeStruct((PROJ_KV, n), BF16),
            jax.ShapeDtypeStruct((n // tm, SSM_WIDTH * ROW_GROUP, ROW_TOKENS), F32),
            jax.ShapeDtypeStruct((n, PROJ_KV), F32),
            jax.ShapeDtypeStruct((n, PROJ_KV), F32),
        ],
        compiler_params=pltpu.CompilerParams(
            dimension_semantics=("arbitrary",), vmem_limit_bytes=V7X_VMEM_LIMIT_BYTES),
        name="proj",
    )(xp2d, xs2d, ln_g, ln_b, w_in, w_in_t, w_in_t)


PAIR = 2 * CHUNK
HEAD_LANES = Q_PER_KV * PAIR
ONES_ROWS = 16
ATTN_PAIRS = 16
BF16_TILE_ROWS = 16


def _scores(unit, qt_ref):
    kwin, _, lanes, _, _, h = unit
    base = h * Q_PER_KV * HEAD_DIM
    qrow = jnp.concatenate(
        [qt_ref[base + g * HEAD_DIM:base + (g + 1) * HEAD_DIM, lanes] for g in range(Q_PER_KV)],
        axis=1)
    zero = jnp.zeros_like(qrow)
    qstack = jnp.concatenate([qrow, zero] if h == 0 else [zero, qrow], axis=0)
    return jnp.dot(kwin, qstack, preferred_element_type=F32)


def _finish(unit, s, sink_ref, o_ref):
    _, vtwin, lanes, valid, masked_rows, h = unit
    nk = s.shape[0]
    base = h * Q_PER_KV * HEAD_DIM
    pieces, done = [], 0
    for start, stop in masked_rows:
        if start > done:
            pieces.append(s[done:start])
        pieces.append(jnp.where(valid[start:stop], s[start:stop], NEG_INF))
        done = stop
    if done < nk:
        pieces.append(s[done:])
    s = jnp.concatenate(pieces, axis=0)
    sink = sink_ref[h]
    m = jnp.maximum(jnp.max(s, axis=0, keepdims=True), sink)
    p = jnp.exp(s - m).astype(BF16)
    v_ones = jnp.concatenate(
        [vtwin[h * HEAD_DIM:(h + 1) * HEAD_DIM, :], jnp.ones((ONES_ROWS, nk), BF16)], axis=0)
    ov = jnp.dot(v_ones, p, preferred_element_type=F32)
    den = ov[HEAD_DIM:HEAD_DIM + 1, :] + jnp.exp(sink - m)
    o = ov[:HEAD_DIM, :] * (1.0 / den)
    for g in range(Q_PER_KV):
        o_ref[base + g * HEAD_DIM:base + (g + 1) * HEAD_DIM, lanes] = (
            o[:, g * PAIR:(g + 1) * PAIR].astype(BF16))


def _attend_units(units, qt_ref, sink_ref, o_ref):
    s_next = _scores(units[0], qt_ref)
    for i, unit in enumerate(units):
        s = s_next
        if i + 1 < len(units):
            s_next = _scores(units[i + 1], qt_ref)
        _finish(unit, s, sink_ref, o_ref)


def _attn_prompt_kernel(sink_ref, qt_ref, kp_ref, kc_ref, vtp_ref, vtc_ref, *refs, n_pairs):
    n_cast = (len(refs) - 1) // 2
    o_ref = refs[n_cast]
    for w_ref, wb_ref in zip(refs[:n_cast], refs[n_cast + 1:]):
        wb_ref[...] = w_ref[...].astype(BF16)
    tq = n_pairs * PAIR
    nk = WINDOW + PAIR
    kk = jnp.concatenate([kp_ref[...], kc_ref[...]], axis=0).astype(BF16)
    vt = jnp.concatenate([vtp_ref[...], vtc_ref[...]], axis=1)
    r = lax.broadcasted_iota(jnp.int32, (nk, HEAD_LANES), 0)
    first_chunk = (lax.broadcasted_iota(jnp.int32, (nk, HEAD_LANES), 1) & (PAIR - 1)) < CHUNK
    lo = jnp.where(first_chunk, 0, CHUNK)
    hi = jnp.where(first_chunk, BAND, nk)
    first_pos = pl.program_id(1) * tq - WINDOW
    units = []
    for pp in range(n_pairs):
        lo_pp = jnp.maximum(lo, -first_pos) if pp == 0 else lo
        valid = (r >= lo_pp) & (r < hi)
        masked_rows = ((0, nk),) if pp == 0 else ((0, CHUNK), (nk - CHUNK, nk))
        for h in range(N_KV_HEADS):
            units.append((kk[pp * PAIR:pp * PAIR + nk], vt[:, pp * PAIR:pp * PAIR + nk],
                          slice(pp * PAIR, (pp + 1) * PAIR), valid, masked_rows, h))
    _attend_units(units, qt_ref, sink_ref, o_ref)


def _attn_sample_kernel(sink_ref, qt_ref, ck_ref, kn_ref, cvt_ref, vtn_ref, o_ref, *, n_pairs):
    nk = 2 * WINDOW + PAIR
    r = lax.broadcasted_iota(jnp.int32, (nk, HEAD_LANES), 0)
    query_seq = (lax.broadcasted_iota(jnp.int32, (nk, HEAD_LANES), 1) & (PAIR - 1)) >> 6
    key_seq = jnp.where(r < 2 * WINDOW, r >> 7, (r - 2 * WINDOW) >> 6)
    valid = query_seq == key_seq
    units = []
    for pp in range(n_pairs):
        lanes = slice(pp * PAIR, (pp + 1) * PAIR)
        kwin = jnp.concatenate([ck_ref[2 * pp], ck_ref[2 * pp + 1], kn_ref[lanes, :]],
                               axis=0).astype(BF16)
        vtwin = jnp.concatenate([cvt_ref[:, 2 * pp * WINDOW:2 * (pp + 1) * WINDOW], vtn_ref[:, lanes]],
                                axis=1)
        units.extend((kwin, vtwin, lanes, valid, ((0, nk),), h) for h in range(N_KV_HEADS))
    _attend_units(units, qt_ref, sink_ref, o_ref)


_SINK_SPEC = pl.BlockSpec((N_KV_HEADS, 1, HEAD_LANES), lambda *_: (0, 0, 0))


def _attention_prompt(sink_rows, qt, k, vt, bsz, seq, n_pairs, f32_weights):
    tq = n_pairs * PAIR
    nt = seq // tq
    steps = bsz * nt
    wpt = tq // WINDOW
    cur_c = lambda b, i: (0, b * nt + i)
    cur_r = lambda b, i: (b * nt + i, 0)
    prev = lambda b, i: jnp.maximum((b * nt + i) * wpt - 1, 0)
    assert all(w.shape[0] % (steps * BF16_TILE_ROWS) == 0 for w in f32_weights)
    cast_specs = [pl.BlockSpec((w.shape[0] // steps, w.shape[1]), cur_r) for w in f32_weights]
    return pl.pallas_call(
        functools.partial(_attn_prompt_kernel, n_pairs=n_pairs),
        grid=(bsz, nt),
        in_specs=[
            _SINK_SPEC,
            pl.BlockSpec((PROJ_Q, tq), cur_c),
            pl.BlockSpec((WINDOW, PROJ_KV), lambda b, i: (prev(b, i), 0)),
            pl.BlockSpec((tq, PROJ_KV), cur_r),
            pl.BlockSpec((PROJ_KV, WINDOW), lambda b, i: (0, prev(b, i))),
            pl.BlockSpec((PROJ_KV, tq), cur_c),
            *cast_specs,
        ],
        out_specs=[pl.BlockSpec((PROJ_Q, tq), cur_c), *cast_specs],
        out_shape=[jax.ShapeDtypeStruct((PROJ_Q, bsz * seq), BF16),
                   *[jax.ShapeDtypeStruct(w.shape, BF16) for w in f32_weights]],
        compiler_params=pltpu.CompilerParams(
            dimension_semantics=("arbitrary", "arbitrary"), vmem_limit_bytes=V7X_VMEM_LIMIT_BYTES),
        name="attn_prompt",
    )(sink_rows, qt, k, k, vt, vt, *f32_weights)


def _attention_sample(sink_rows, qt, cache_k, k, cache_vt, vt, first_token, n_pairs):
    n = cache_k.shape[0] * CHUNK
    tq = n_pairs * PAIR
    assert first_token % tq == 0
    off = first_token // tq
    return pl.pallas_call(
        functools.partial(_attn_sample_kernel, n_pairs=n_pairs),
        grid=(n // tq,),
        in_specs=[
            _SINK_SPEC,
            pl.BlockSpec((PROJ_Q, tq), lambda i: (0, i + off)),
            pl.BlockSpec((2 * n_pairs, WINDOW, PROJ_KV), lambda i: (i, 0, 0)),
            pl.BlockSpec((tq, PROJ_KV), lambda i: (i + off, 0)),
            pl.BlockSpec((PROJ_KV, 2 * n_pairs * WINDOW), lambda i: (0, i)),
            pl.BlockSpec((PROJ_KV, tq), lambda i: (0, i + off)),
        ],
        out_specs=pl.BlockSpec((PROJ_Q, tq), lambda i: (0, i)),
        out_shape=jax.ShapeDtypeStruct((PROJ_Q, n), BF16),
        compiler_params=pltpu.CompilerParams(dimension_semantics=("arbitrary",)),
        name="attn_sample",
    )(sink_rows, qt, cache_k, k, cache_vt, vt)


def _power_table(tau, nbits, a_re, a_im):
    rows = tau.shape[0]
    w_re = jnp.ones((rows, STATE_LANES), F32)
    w_im = jnp.zeros((rows, STATE_LANES), F32)
    p_re, p_im = a_re, a_im
    for k in range(nbits):
        bit = ((tau >> k) & 1) == 1
        f_re = jnp.where(bit, p_re, 1.0)
        f_im = jnp.where(bit, p_im, 0.0)
        w_re, w_im = w_re * f_re - w_im * f_im, w_re * f_im + w_im * f_re
        p_re, p_im = p_re * p_re - p_im * p_im, 2.0 * p_re * p_im
    return w_re, w_im


def _build_tables(par_ref, p_ref, qt_ref, coef_ref):
    lo = lax.broadcasted_iota(jnp.int32, (1, STATE_LANES), 1) < SSM_STATE
    mat = lambda i: par_ref[PAR_MATS + i * SSM_GROUP:PAR_MATS + (i + 1) * SSM_GROUP, :]
    lr, li = par_ref[0:1, :], par_ref[1:2, :]
    dt = jnp.exp(par_ref[2:3, :])
    mag = jnp.exp(lr * dt)
    a_re, a_im = mag * jnp.cos(li * dt), mag * jnp.sin(li * dt)
    nr, ni = a_re - 1.0, a_im
    den = lr * lr + li * li
    f_re, f_im = (nr * lr + ni * li) / den, (ni * lr - nr * li) / den
    b_re, b_im = mat(0), mat(1)
    bb_re = f_re * b_re - f_im * b_im
    bb_im = f_re * b_im + f_im * b_re
    c_re, c_im = mat(2), mat(3)

    tau = lax.broadcasted_iota(jnp.int32, (SSM_T, 1), 0)
    w_re, w_im = _power_table(tau, 6, a_re, a_im)
    w1_re, w1_im = w_re * a_re - w_im * a_im, w_re * a_im + w_im * a_re
    wr_re, wr_im = _power_table(SSM_T - 1 - tau, 6, a_re, a_im)

    def outer(c, w):
        return (c[:, None, :] * w[None, :, :]).reshape(SSM_CW, STATE_LANES)

    cw_mix = (outer(c_re, jnp.where(lo, w_re, w_im)) + outer(c_im, jnp.where(lo, -w_im, w_re)))
    bb_mix = jnp.where(lo, bb_re, -bb_im)
    strip = lax.dot_general(bb_mix, cw_mix, (((1,), (1,)), ((), ())),
                            precision=lax.Precision.HIGHEST,
                            preferred_element_type=F32)

    x_a, y_a = jnp.where(lo, bb_re, bb_im), jnp.where(lo, -bb_im, bb_re)
    p_ref[...] = (outer(x_a, wr_re) + outer(y_a, wr_im)).astype(BF16)

    qt_ref[...] = (outer(c_re, jnp.where(lo, w1_re, -w1_im))
                   + outer(c_im, jnp.where(lo, -w1_im, -w1_re))).astype(BF16)

    t_re, t_im = a_re, a_im
    for _ in range(6):
        t_re, t_im = t_re * t_re - t_im * t_im, 2.0 * t_re * t_im
    for k in range(COEF_ROWS // 2):
        coef_ref[2 * k:2 * k + 1, :] = t_re
        coef_ref[2 * k + 1:2 * k + 2, :] = jnp.where(lo, -t_im, t_im)
        t_re, t_im = t_re * t_re - t_im * t_im, 2.0 * t_re * t_im
    return strip


def _toeplitz_rows(strip, m_ref, channels):
    s_idx = lax.broadcasted_iota(jnp.int32, (SSM_T, SSM_CW), 0)
    t_idx = lax.broadcasted_iota(jnp.int32, (SSM_T, SSM_CW), 1) & (SSM_T - 1)
    causal = t_idx >= s_idx
    for c in channels:
        rows = jnp.broadcast_to(strip[c:c + 1, :], (SSM_T, SSM_CW))
        shifted = pltpu.roll(rows, 0, 1, stride=1, stride_axis=0)
        m_ref[c * SSM_T:(c + 1) * SSM_T, :] = jnp.where(causal, shifted, 0.0).astype(BF16)


PAR_MATS = 8
PAR_ROWS = PAR_MATS + 4 * SSM_GROUP


def _table_params(lam_re, lam_im, log_step, b_re, b_im, c_re, c_im):
    step = jnp.broadcast_to(log_step[:, None, None], (SSM_GROUPS, 1, SSM_STATE))
    filler = jnp.zeros((SSM_GROUPS, PAR_MATS - 3, SSM_STATE), F32)
    pack = jnp.concatenate([lam_re[:, None, :], lam_im[:, None, :], step, filler,
                            jnp.swapaxes(b_re, 1, 2), jnp.swapaxes(b_im, 1, 2), c_re, c_im], axis=1)
    return jnp.concatenate([pack, pack], axis=-1)


def _cmul(a1, a2, h, hs):
    return a1 * h + a2 * hs, a1 * hs - a2 * h


def _ssm_kernel(par_ref, u_ref, h0_ref, y_ref, hfin_ref, m_ref, p_ref, qt_ref, coef_ref,
                *, rows_per_seq, prompt_rows):
    for gi in range(SSM_GROUPS_PER_STEP):
        _ssm_group(par_ref.at[gi], u_ref, gi * SSM_GROUP, h0_ref.at[gi], y_ref, hfin_ref.at[gi],
                   m_ref.at[gi], p_ref.at[gi], qt_ref.at[gi], coef_ref.at[gi],
                   rows_per_seq=rows_per_seq, prompt_rows=prompt_rows)


def _ssm_group(par_ref, u_ref, c0, h0_ref, y_ref, hfin_ref, m_ref, p_ref, qt_ref, coef_ref,
               *, rows_per_seq, prompt_rows):
    strip = _build_tables(par_ref, p_ref, qt_ref, coef_ref)
    lo = lax.broadcasted_iota(jnp.int32, (1, ROW_TOKENS), 1) < SSM_T

    def chunk_rows(ref):
        even, odd = [], []
        rows = ref.shape[0] * ROW_GROUP
        for k in range(SSM_GROUP // 2):
            a = ref[:, c0 + 2 * k].reshape(rows, ROW_TOKENS)
            b = ref[:, c0 + 2 * k + 1].reshape(rows, ROW_TOKENS)
            even.append(jnp.where(lo, a, pltpu.roll(b, SSM_T, 1)))
            odd.append(jnp.where(lo, pltpu.roll(a, SSM_T, 1), b))
        return jnp.concatenate(even, axis=1), jnp.concatenate(odd, axis=1)

    def store_rows(y_even, y_odd, ref):
        for k in range(SSM_GROUP // 2):
            te = y_even[:, k * ROW_TOKENS:(k + 1) * ROW_TOKENS]
            to = y_odd[:, k * ROW_TOKENS:(k + 1) * ROW_TOKENS]
            tiles = (ref.shape[0], ROW_GROUP, ROW_TOKENS)
            ref[:, c0 + 2 * k] = jnp.where(lo, te, pltpu.roll(to, SSM_T, 1)).reshape(tiles)
            ref[:, c0 + 2 * k + 1] = jnp.where(lo, pltpu.roll(te, SSM_T, 1), to).reshape(tiles)

    even, odd = chunk_rows(u_ref)
    rp, rs = prompt_rows, even.shape[0] - prompt_rows
    u = jnp.concatenate([even[:rp], odd[:rp], even[rp:], odd[rp:]], axis=0).astype(BF16)
    s1 = jnp.dot(u, p_ref[...], preferred_element_type=F32)
    s2 = pltpu.roll(s1, SSM_STATE, 1)
    a1, a2 = coef_ref[0:1, :], coef_ref[1:2, :]

    e1, e2, o1, o2 = s1[:rp], s2[:rp], s1[rp:2 * rp], s2[rp:2 * rp]
    x1, x2 = _cmul(a1, a2, e1, e2)
    x1, x2 = x1 + o1, x2 + o2
    pos = lax.broadcasted_iota(jnp.int32, (rp, STATE_LANES), 0) & (rows_per_seq - 1)
    for k in range(rows_per_seq.bit_length() - 1):
        d = 1 << k
        b1, b2 = coef_ref[2 + 2 * k:3 + 2 * k, :], coef_ref[3 + 2 * k:4 + 2 * k, :]
        sh1 = jnp.where(pos >= d, pltpu.roll(x1, d, 0), 0.0)
        sh2 = jnp.where(pos >= d, pltpu.roll(x2, d, 0), 0.0)
        y1, y2 = _cmul(b1, b2, sh1, sh2)
        x1, x2 = x1 + y1, x2 + y2
    g1 = jnp.where(pos >= 1, pltpu.roll(x1, 1, 0), 0.0)
    g2 = jnp.where(pos >= 1, pltpu.roll(x2, 1, 0), 0.0)
    ho1 = _cmul(a1, a2, g1, g2)[0] + e1
    hfin_ref[0:STATE_OUT_ROWS, :] = jnp.zeros((STATE_OUT_ROWS, STATE_LANES), F32)
    for b in range(rp // rows_per_seq):
        last = (b + 1) * rows_per_seq - 1
        hfin_ref[b:b + 1, :] = x1[last:last + 1, :]

    h0e, h0o = h0_ref[0], h0_ref[1]
    swap = lambda h: pltpu.roll(h, SSM_STATE, 1)
    hfin_ref[STATE_OUT_ROWS:STATE_OUT_ROWS + rs, :] = a1 * h0e + a2 * swap(h0e) + s1[2 * rp:2 * rp + rs]
    hfin_ref[STATE_OUT_ROWS + rs:, :] = a1 * h0o + a2 * swap(h0o) + s1[2 * rp + rs:]

    hprev = jnp.concatenate([g1, ho1, h0e, h0o], axis=0).astype(BF16)
    y = lax.dot_general(hprev, qt_ref[...], (((1,), (1,)), ((), ())), preferred_element_type=F32)
    blocks = [range(c, c + M_BLOCK_CHANNELS) for c in range(0, SSM_GROUP, M_BLOCK_CHANNELS)]
    _toeplitz_rows(strip, m_ref, blocks[0])
    for i, channels in enumerate(blocks):
        rows = slice(channels[0] * SSM_T, (channels[-1] + 1) * SSM_T)
        y += jnp.dot(u[:, rows], m_ref[rows, :], preferred_element_type=F32)
        if i + 1 < len(blocks):
            _toeplitz_rows(strip, m_ref, blocks[i + 1])
    store_rows(jnp.concatenate([y[:rp], y[2 * rp:2 * rp + rs]], axis=0),
               jnp.concatenate([y[rp:2 * rp], y[2 * rp + rs:]], axis=0), y_ref)


def _ssm(table_params, u3, h0_pack, *, rows_per_seq, prompt_rows):
    groups = u3.shape[0]
    rs = groups * ROW_GROUP - prompt_rows
    u4 = u3.reshape(groups, SSM_WIDTH, ROW_GROUP, ROW_TOKENS)
    assert rows_per_seq & (rows_per_seq - 1) == 0 and 2 * rows_per_seq.bit_length() <= COEF_ROWS
    gps = SSM_GROUPS_PER_STEP
    g3 = lambda g: (g, 0, 0)
    u_spec = pl.BlockSpec((groups, gps * SSM_GROUP, ROW_GROUP, ROW_TOKENS), lambda g: (0, g, 0, 0))
    fin_rows = STATE_OUT_ROWS + 2 * rs
    return pl.pallas_call(
        functools.partial(_ssm_kernel, rows_per_seq=rows_per_seq, prompt_rows=prompt_rows),
        grid=(SSM_GROUPS // gps,),
        in_specs=[
            pl.BlockSpec((gps, PAR_ROWS, STATE_LANES), g3),
            u_spec,
            pl.BlockSpec((gps, 2, rs, STATE_LANES), lambda g: (g, 0, 0, 0)),
        ],
        out_specs=[
            u_spec,
            pl.BlockSpec((gps, fin_rows, STATE_LANES), g3),
        ],
        out_shape=[
            jax.ShapeDtypeStruct(u4.shape, F32),
            jax.ShapeDtypeStruct((SSM_GROUPS, fin_rows, STATE_LANES), F32),
        ],
        scratch_shapes=[
            pltpu.VMEM((gps, SSM_CW, SSM_CW), BF16),
            pltpu.VMEM((gps, SSM_CW, STATE_LANES), BF16),
            pltpu.VMEM((gps, SSM_CW, STATE_LANES), BF16),
            pltpu.VMEM((gps, COEF_ROWS, STATE_LANES), F32),
        ],
        compiler_params=pltpu.CompilerParams(
            dimension_semantics=("arbitrary",), vmem_limit_bytes=V7X_VMEM_LIMIT_BYTES),
        name="ssm",
    )(table_params, u4, h0_pack)


def _post_kernel(xp_ref, xs_ref, atp_ref, ats_ref, ut_ref, yt_ref, *refs, alpha, prompt_tiles):
    weights, (op_ref, os_ref) = refs[:-2], refs[-2:]

    @pl.when(pl.program_id(0) < prompt_tiles)
    def _():
        _post_tile(xp_ref, atp_ref, ut_ref, yt_ref, *weights, op_ref, alpha=alpha)

    @pl.when(pl.program_id(0) >= prompt_tiles)
    def _():
        _post_tile(xs_ref, ats_ref, ut_ref, yt_ref, *weights, os_ref, alpha=alpha)


def _post_tile(x_ref, at_ref, ut_ref, yt_ref, lng_ref, lnb_ref, d_ref, wglut_ref, bglu_ref,
               wout_ref, ln1g_ref, ln1b_ref, wgu_ref, wdown_ref, ln2g_ref, ln2b_ref, o_ref,
               *, alpha):
    xn = _layer_norm(x_ref[...], lng_ref[...], lnb_ref[...])
    n_rows = x_ref.shape[0] // ROW_TOKENS
    first_row = (pl.program_id(0) % (ROW_GROUP // n_rows)) * n_rows

    def feature_major(ref):
        return jnp.concatenate(
            [ref[pl.ds(first_row + j, SSM_WIDTH, stride=ROW_GROUP), :] for j in range(n_rows)], axis=1)

    ys = feature_major(yt_ref) + d_ref[...] * feature_major(ut_ref)
    gl = 0.5 * ys * (1.0 + lax.erf(ys * math.sqrt(0.5)))
    z = jnp.dot(wglut_ref[...], gl.astype(BF16), preferred_element_type=F32) + bglu_ref[...]
    s = gl * jax.nn.sigmoid(z)
    tn = (((0,), (0,)), ((), ()))
    mix = lax.dot_general(at_ref[...], wout_ref[:PROJ_Q, :], tn, preferred_element_type=F32)
    mix += lax.dot_general(s.astype(BF16), wout_ref[PROJ_Q:, :], tn, preferred_element_type=F32)
    h = _layer_norm(alpha * xn + mix, ln1g_ref[...], ln1b_ref[...])
    hb = h.astype(BF16)
    f = jnp.zeros_like(h)
    for start in range(0, D_FF, FF_BLOCK):
        stop = min(start + FF_BLOCK, D_FF)
        cols = slice(start, stop)
        up_cols = slice(D_FF + start, D_FF + stop)
        g = jnp.dot(hb, wgu_ref[:, cols], preferred_element_type=F32)
        up = jnp.dot(hb, wgu_ref[:, up_cols], preferred_element_type=F32)
        act = (g * jax.nn.sigmoid(g)) * up
        f += jnp.dot(act.astype(BF16), wdown_ref[cols, :], preferred_element_type=F32)
    o_ref[...] = _layer_norm(alpha * h + f, ln2g_ref[...], ln2b_ref[...])


def _post(xp2d, xs2d, atp, ats, ut3, yt3, ln_g, ln_b, d_col, w_glu_t, b_glu_col, w_out, ln1_g, ln1_b,
          w_gate_up, w_down, ln2_g, ln2_b, *, alpha, tm):
    n_p, n_s = xp2d.shape[0], xs2d.shape[0]
    tp, ts = n_p // tm, n_s // tm
    assert PROJ_TILE % tm == 0 and n_p % PROJ_TILE == 0
    group = lambda i: (i // (PROJ_TILE // tm), 0, 0)
    group_spec = pl.BlockSpec((None, SSM_WIDTH * ROW_GROUP, ROW_TOKENS), group)
    prompt = lambda i: jnp.minimum(i, tp - 1)
    sample = lambda i: jnp.maximum(i - tp, 0)
    const = lambda i: (0, 0)
    resident = lambda shape: pl.BlockSpec(shape, const, pipeline_mode=pl.Buffered(1))
    vec = lambda width: pl.BlockSpec((1, width), const)
    colvec = pl.BlockSpec((SSM_WIDTH, 1), const)
    return pl.pallas_call(
        functools.partial(_post_kernel, alpha=alpha, prompt_tiles=tp),
        grid=(tp + ts,),
        in_specs=[
            pl.BlockSpec((tm, D_MODEL), lambda i: (prompt(i), 0)),
            pl.BlockSpec((tm, D_MODEL), lambda i: (sample(i), 0)),
            pl.BlockSpec((PROJ_Q, tm), lambda i: (0, prompt(i))),
            pl.BlockSpec((PROJ_Q, tm), lambda i: (0, sample(i))),
            group_spec, group_spec,
            vec(D_MODEL), vec(D_MODEL), colvec,
            resident((SSM_WIDTH, SSM_WIDTH)), colvec,
            resident((D_MODEL, D_MODEL)), vec(D_MODEL), vec(D_MODEL),
            resident((D_MODEL, 2 * D_FF)), resident((D_FF, D_MODEL)),
            vec(D_MODEL), vec(D_MODEL),
        ],
        out_specs=[
            pl.BlockSpec((tm, D_MODEL), lambda i: (prompt(i), 0)),
            pl.BlockSpec((tm, D_MODEL), lambda i: (sample(i), 0)),
        ],
        out_shape=[
            jax.ShapeDtypeStruct((n_p, D_MODEL), F32),
            jax.ShapeDtypeStruct((n_s, D_MODEL), F32),
        ],
        compiler_params=pltpu.CompilerParams(
            dimension_semantics=("arbitrary",), vmem_limit_bytes=V7X_VMEM_LIMIT_BYTES),
        name="post",
    )(xp2d, xs2d, atp, ats, ut3, yt3, ln_g, ln_b, d_col, w_glu_t, b_glu_col, w_out, ln1_g, ln1_b,
      w_gate_up, w_down, ln2_g, ln2_b)


def kernel(x_prompt, x_sample, cache_win_k, cache_win_v, state_ssm_re, state_ssm_im,
           ln_in_g, ln_in_b, w_in, attn_sinks, ssm_lambda_re, ssm_lambda_im, ssm_log_step,
           ssm_b_re, ssm_b_im, ssm_c_re, ssm_c_im, ssm_d, w_glu, b_glu, w_out,
           ln1_g, ln1_b, w_gate_up, w_down, ln2_g, ln2_b):
    depth = w_in.shape[0]
    assert depth == 1, "single-layer step"
    bp, lp, _ = x_prompt.shape
    bs, ls, _ = x_sample.shape
    assert ls == SSM_T and bs % 2 == 0 and lp % 512 == 0 and bp <= STATE_OUT_ROWS
    win_rows = cache_win_k.shape[2]
    assert win_rows == WINDOW
    alpha = (2.0 * depth) ** 0.25
    l = 0
    row = lambda a: a.reshape(1, -1)
    column = lambda a: a.reshape(-1, 1)

    lng, lnb = row(ln_in_g), row(ln_in_b)
    xp2 = x_prompt.reshape(bp * lp, D_MODEL)
    xs2 = x_sample.reshape(bs * ls, D_MODEL)
    n_p = bp * lp
    qt, vt, u3, k, v = _proj(xp2, xs2, lng, lnb, w_in[l].astype(BF16), w_in[l].T.astype(BF16))

    sink_rows = jnp.repeat(attn_sinks[l].reshape(N_KV_HEADS, Q_PER_KV), PAIR, axis=1)[:, None, :]
    atp, w_gu_b, w_down_b, w_out_b = _attention_prompt(
        sink_rows, qt, k, vt, bp, lp, n_pairs=ATTN_PAIRS, f32_weights=(w_gate_up[l], w_down[l], w_out[l]))
    ck = cache_win_k[l].reshape(bs, win_rows, PROJ_KV)
    cv = cache_win_v[l].reshape(bs, win_rows, PROJ_KV)
    cvt = jnp.transpose(cv, (2, 0, 1)).reshape(PROJ_KV, bs * win_rows).astype(BF16)
    ats = _attention_sample(sink_rows, qt, ck, k, cvt, vt, first_token=n_p, n_pairs=ATTN_PAIRS)
    ks3, vs3 = k[n_p:].reshape(bs, ls, PROJ_KV), v[n_p:].reshape(bs, ls, PROJ_KV)

    table_params = _table_params(
        ssm_lambda_re[l], ssm_lambda_im[l], ssm_log_step[l],
        ssm_b_re[l], ssm_b_im[l], ssm_c_re[l], ssm_c_im[l])
    h0 = jnp.concatenate([state_ssm_re[l], state_ssm_im[l]], axis=-1)
    h0_pack = jnp.transpose(h0.reshape(bs // 2, 2, SSM_GROUPS, STATE_LANES), (2, 1, 0, 3))
    y4, hfin = _ssm(table_params, u3, h0_pack,
                    rows_per_seq=lp // ROW_TOKENS, prompt_rows=n_p // ROW_TOKENS)

    post_args = (lng, lnb, column(ssm_d[l]), w_glu[l].T.astype(BF16), column(b_glu[l]),
                 w_out_b, row(ln1_g[l]), row(ln1_b[l]), w_gu_b, w_down_b, row(ln2_g[l]), row(ln2_b[l]))
    out_p, out_s = _post(xp2, xs2, atp, ats, u3, y4.reshape(u3.shape), *post_args, alpha=alpha, tm=512)

    kv_shape = (N_KV_HEADS, HEAD_DIM)
    last_rows = lambda a: jnp.stack([a[(b + 1) * lp - win_rows:(b + 1) * lp] for b in range(bp)])
    win_k_p = last_rows(k).reshape(1, bp, win_rows, *kv_shape)
    win_v_p = last_rows(v).reshape(1, bp, win_rows, *kv_shape)
    win_k_s = jnp.concatenate([ck, ks3], axis=1)[:, -win_rows:].reshape(1, bs, win_rows, *kv_shape)
    win_v_s = jnp.concatenate([cv, vs3], axis=1)[:, -win_rows:].reshape(1, bs, win_rows, *kv_shape)
    sp = jnp.swapaxes(hfin[:, :bp], 0, 1)
    ss = jnp.transpose(hfin[:, STATE_OUT_ROWS:].reshape(SSM_GROUPS, 2, bs // 2, STATE_LANES),
                       (2, 1, 0, 3)).reshape(bs, SSM_GROUPS, STATE_LANES)
    return (out_p.reshape(bp, lp, D_MODEL), out_s.reshape(bs, ls, D_MODEL),
            win_k_p, win_v_p, sp[None, ..., :SSM_STATE], sp[None, ..., SSM_STATE:],
            win_k_s, win_v_s, ss[None, ..., :SSM_STATE], ss[None, ..., SSM_STATE:])
```

```python
import functools
import math

import jax
import jax.numpy as jnp
from jax import lax
from jax.experimental import pallas as pl
from jax.experimental.pallas import tpu as pltpu

F32 = jnp.float32
BF16 = jnp.bfloat16

D_MODEL = 1024
HEAD_DIM = 64
N_HEADS = 8
N_KV_HEADS = 2
Q_PER_KV = N_HEADS // N_KV_HEADS
CHUNK = 64
WINDOW = 128
WIN_CHUNKS = WINDOW // CHUNK
BAND = (WIN_CHUNKS + 1) * CHUNK
PROJ_Q = N_HEADS * HEAD_DIM
PROJ_KV = N_KV_HEADS * HEAD_DIM
SSM_WIDTH = 512
SSM_GROUP = 16
SSM_GROUPS = SSM_WIDTH // SSM_GROUP
SSM_STATE = 64
STATE_LANES = 2 * SSM_STATE
D_FF = 2816
D_IN_PROJ = PROJ_Q + 2 * PROJ_KV + SSM_WIDTH
LN_EPS = 1e-5
NEG_INF = -1e30

SSM_T = CHUNK
SSM_CW = SSM_T * SSM_GROUP
ROW_TOKENS = 2 * SSM_T
ROW_GROUP = 8
PROJ_TILE = ROW_GROUP * ROW_TOKENS
PROJ_PARTS = 2
COEF_ROWS = 16
STATE_OUT_ROWS = 8
M_BLOCK_CHANNELS = 4
SSM_GROUPS_PER_STEP = 2
FF_BLOCK = 256
V7X_VMEM_LIMIT_BYTES = 56 * 1024 * 1024


def _layer_norm(x, g, b):
    mu = jnp.mean(x, axis=-1, keepdims=True)
    xc = x - mu
    var = jnp.mean(xc * xc, axis=-1, keepdims=True)
    return xc * lax.rsqrt(var + LN_EPS) * g + b


KV = 2 * PROJ_KV
VU = PROJ_KV + SSM_WIDTH


def _proj_kernel(xp_ref, xs_ref, g_ref, b_ref, wkvt_ref, wqt_ref, wvut_ref, qt_ref, vt_ref, ut_ref, k_ref, v_ref,
                 *, prompt_tiles):
    outs = (g_ref, b_ref, wkvt_ref, wqt_ref, wvut_ref, qt_ref, vt_ref, ut_ref, k_ref, v_ref)

    @pl.when(pl.program_id(0) < prompt_tiles)
    def _():
        _proj_tile(xp_ref, *outs)

    @pl.when(pl.program_id(0) >= prompt_tiles)
    def _():
        _proj_tile(xs_ref, *outs)


def _proj_tile(x_ref, g_ref, b_ref, wkvt_ref, wqt_ref, wvut_ref, qt_ref, vt_ref, ut_ref, k_ref, v_ref):
    rows_per_part = ROW_GROUP // PROJ_PARTS
    nt = (((1,), (1,)), ((), ()))
    for part in range(PROJ_PARTS):
        tok = slice(part * rows_per_part * ROW_TOKENS, (part + 1) * rows_per_part * ROW_TOKENS)
        xb = _layer_norm(x_ref[tok, :], g_ref[...], b_ref[...]).astype(BF16)
        p = lax.dot_general(xb, wkvt_ref[...], nt, preferred_element_type=F32)
        k_ref[tok, :] = p[:, :PROJ_KV]
        v_ref[tok, :] = p[:, PROJ_KV:]
        qt = lax.dot_general(wqt_ref[...], xb, nt, preferred_element_type=F32)
        vut = lax.dot_general(wvut_ref[...], xb, nt, preferred_element_type=F32)
        qt_ref[:, tok] = (qt * (HEAD_DIM ** -0.5)).astype(BF16)
        vt_ref[:, tok] = vut[:PROJ_KV].astype(BF16)
        for j in range(rows_per_part):
            ut_ref[pl.ds(part * rows_per_part + j, SSM_WIDTH, stride=ROW_GROUP), :] = (
                vut[PROJ_KV:, j * ROW_TOKENS:(j + 1) * ROW_TOKENS])


def _proj(xp2d, xs2d, ln_g, ln_b, w_in_t):
    assert D_IN_PROJ == 2 * VU and PROJ_Q % KV == 0
    tm = PROJ_TILE
    tp, ts = xp2d.shape[0] // tm, xs2d.shape[0] // tm
    n = (tp + ts) * tm
    const = lambda i: (0, 0)
    row = lambda i: (i, 0)
    col = lambda i: (0, i)
    return pl.pallas_call(
        functools.partial(_proj_kernel, prompt_tiles=tp),
        grid=(tp + ts,),
        in_specs=[
            pl.BlockSpec((tm, D_MODEL), lambda i: (jnp.minimum(i, tp - 1), 0)),
            pl.BlockSpec((tm, D_MODEL), lambda i: (jnp.maximum(i - tp, 0), 0)),
            pl.BlockSpec((1, D_MODEL), const),
            pl.BlockSpec((1, D_MODEL), const),
            pl.BlockSpec((KV, D_MODEL), lambda i: (PROJ_Q // KV, 0)),
            pl.BlockSpec((PROJ_Q, D_MODEL), const),
            pl.BlockSpec((VU, D_MODEL), lambda i: (1, 0)),
        ],
        out_specs=[
            pl.BlockSpec((PROJ_Q, tm), col),
            pl.BlockSpec((PROJ_KV, tm), col),
            pl.BlockSpec((None, SSM_WIDTH * ROW_GROUP, ROW_TOKENS), lambda i: (i, 0, 0)),
            pl.BlockSpec((tm, PROJ_KV), row),
            pl.BlockSpec((tm, PROJ_KV), row),
        ],
        out_shape=[
            jax.ShapeDtypeStruct((PROJ_Q, n), BF16),
            jax.ShapeDtypeStruct((PROJ_KV, n), BF16),
            jax.ShapeDtypeStruct((n // tm, SSM_WIDTH * ROW_GROUP, ROW_TOKENS), F32),
            jax.ShapeDtypeStruct((n, PROJ_KV), F32),
            jax.ShapeDtypeStruct((n, PROJ_KV), F32),
        ],
        compiler_params=pltpu.CompilerParams(
            dimension_semantics=("arbitrary",), vmem_limit_bytes=V7X_VMEM_LIMIT_BYTES),
        name="proj",
    )(xp2d, xs2d, ln_g, ln_b, w_in_t, w_in_t, w_in_t)


PAIR = 2 * CHUNK
HEAD_LANES = Q_PER_KV * PAIR
ONES_ROWS = 16
ATTN_PAIRS = 16
BF16_TILE_ROWS = 16


def _scores(unit, qt_ref):
    kwin, _, lanes, _, _, h = unit
    base = h * Q_PER_KV * HEAD_DIM
    qrow = jnp.concatenate(
        [qt_ref[base + g * HEAD_DIM:base + (g + 1) * HEAD_DIM, lanes] for g in range(Q_PER_KV)],
        axis=1)
    zero = jnp.zeros_like(qrow)
    qstack = jnp.concatenate([qrow, zero] if h == 0 else [zero, qrow], axis=0)
    return jnp.dot(kwin, qstack, preferred_element_type=F32)


def _finish(unit, s, sink_ref, o_ref):
    _, vtwin, lanes, valid, masked_rows, h = unit
    nk = s.shape[0]
    base = h * Q_PER_KV * HEAD_DIM
    pieces, done = [], 0
    for start, stop in masked_rows:
        if start > done:
            pieces.append(s[done:start])
        pieces.append(jnp.where(valid[start:stop], s[start:stop], NEG_INF))
        done = stop
    if done < nk:
        pieces.append(s[done:])
    s = jnp.concatenate(pieces, axis=0)
    sink = sink_ref[h]
    m = jnp.maximum(jnp.max(s, axis=0, keepdims=True), sink)
    p = jnp.exp(s - m).astype(BF16)
    v_ones = jnp.concatenate(
        [vtwin[h * HEAD_DIM:(h + 1) * HEAD_DIM, :], jnp.ones((ONES_ROWS, nk), BF16)], axis=0)
    ov = jnp.dot(v_ones, p, preferred_element_type=F32)
    den = ov[HEAD_DIM:HEAD_DIM + 1, :] + jnp.exp(sink - m)
    o = ov[:HEAD_DIM, :] * (1.0 / den)
    for g in range(Q_PER_KV):
        o_ref[base + g * HEAD_DIM:base + (g + 1) * HEAD_DIM, lanes] = (
            o[:, g * PAIR:(g + 1) * PAIR].astype(BF16))


def _attend_units(units, qt_ref, sink_ref, o_ref):
    s_next = _scores(units[0], qt_ref)
    for i, unit in enumerate(units):
        s = s_next
        if i + 1 < len(units):
            s_next = _scores(units[i + 1], qt_ref)
        _finish(unit, s, sink_ref, o_ref)


def _attn_prompt_kernel(sink_ref, qt_ref, kp_ref, kc_ref, vtp_ref, vtc_ref, *refs, n_pairs):
    n_cast = (len(refs) - 1) // 2
    o_ref = refs[n_cast]
    for w_ref, wb_ref in zip(refs[:n_cast], refs[n_cast + 1:]):
        wb_ref[...] = w_ref[...].astype(BF16)
    tq = n_pairs * PAIR
    nk = WINDOW + PAIR
    kk = jnp.concatenate([kp_ref[...], kc_ref[...]], axis=0).astype(BF16)
    vt = jnp.concatenate([vtp_ref[...], vtc_ref[...]], axis=1)
    r = lax.broadcasted_iota(jnp.int32, (nk, HEAD_LANES), 0)
    first_chunk = (lax.broadcasted_iota(jnp.int32, (nk, HEAD_LANES), 1) & (PAIR - 1)) < CHUNK
    lo = jnp.where(first_chunk, 0, CHUNK)
    hi = jnp.where(first_chunk, BAND, nk)
    first_pos = pl.program_id(1) * tq - WINDOW
    units = []
    for pp in range(n_pairs):
        lo_pp = jnp.maximum(lo, -first_pos) if pp == 0 else lo
        valid = (r >= lo_pp) & (r < hi)
        masked_rows = ((0, nk),) if pp == 0 else ((0, CHUNK), (nk - CHUNK, nk))
        for h in range(N_KV_HEADS):
            units.append((kk[pp * PAIR:pp * PAIR + nk], vt[:, pp * PAIR:pp * PAIR + nk],
                          slice(pp * PAIR, (pp + 1) * PAIR), valid, masked_rows, h))
    _attend_units(units, qt_ref, sink_ref, o_ref)


def _attn_sample_kernel(sink_ref, qt_ref, ck_ref, kn_ref, cvt_ref, vtn_ref, cv_ref, vn_ref,
                        o_ref, wk_ref, wv_ref, *, n_pairs):
    for b in range(2 * n_pairs):
        new = slice(b * CHUNK, (b + 1) * CHUNK)
        for cache_ref, new_ref, win_ref in ((ck_ref, kn_ref, wk_ref), (cv_ref, vn_ref, wv_ref)):
            win_ref[b, :WINDOW - CHUNK, :] = cache_ref[b, CHUNK:, :]
            win_ref[b, WINDOW - CHUNK:, :] = new_ref[new, :]
    nk = 2 * WINDOW + PAIR
    r = lax.broadcasted_iota(jnp.int32, (nk, HEAD_LANES), 0)
    query_seq = (lax.broadcasted_iota(jnp.int32, (nk, HEAD_LANES), 1) & (PAIR - 1)) >> 6
    key_seq = jnp.where(r < 2 * WINDOW, r >> 7, (r - 2 * WINDOW) >> 6)
    valid = query_seq == key_seq
    units = []
    for pp in range(n_pairs):
        lanes = slice(pp * PAIR, (pp + 1) * PAIR)
        kwin = jnp.concatenate([ck_ref[2 * pp], ck_ref[2 * pp + 1], kn_ref[lanes, :]],
                               axis=0).astype(BF16)
        vtwin = jnp.concatenate([cvt_ref[:, 2 * pp * WINDOW:2 * (pp + 1) * WINDOW], vtn_ref[:, lanes]],
                                axis=1)
        units.extend((kwin, vtwin, lanes, valid, ((0, nk),), h) for h in range(N_KV_HEADS))
    _attend_units(units, qt_ref, sink_ref, o_ref)


_SINK_SPEC = pl.BlockSpec((N_KV_HEADS, 1, HEAD_LANES), lambda *_: (0, 0, 0))


def _attention_prompt(sink_rows, qt, k, vt, bsz, seq, n_pairs, f32_weights):
    tq = n_pairs * PAIR
    nt = seq // tq
    steps = bsz * nt
    wpt = tq // WINDOW
    cur_c = lambda b, i: (0, b * nt + i)
    cur_r = lambda b, i: (b * nt + i, 0)
    prev = lambda b, i: jnp.maximum((b * nt + i) * wpt - 1, 0)
    assert all(w.shape[0] % (steps * BF16_TILE_ROWS) == 0 for w in f32_weights)
    cast_specs = [pl.BlockSpec((w.shape[0] // steps, w.shape[1]), cur_r) for w in f32_weights]
    return pl.pallas_call(
        functools.partial(_attn_prompt_kernel, n_pairs=n_pairs),
        grid=(bsz, nt),
        in_specs=[
            _SINK_SPEC,
            pl.BlockSpec((PROJ_Q, tq), cur_c),
            pl.BlockSpec((WINDOW, PROJ_KV), lambda b, i: (prev(b, i), 0)),
            pl.BlockSpec((tq, PROJ_KV), cur_r),
            pl.BlockSpec((PROJ_KV, WINDOW), lambda b, i: (0, prev(b, i))),
            pl.BlockSpec((PROJ_KV, tq), cur_c),
            *cast_specs,
        ],
        out_specs=[pl.BlockSpec((PROJ_Q, tq), cur_c), *cast_specs],
        out_shape=[jax.ShapeDtypeStruct((PROJ_Q, bsz * seq), BF16),
                   *[jax.ShapeDtypeStruct(w.shape, BF16) for w in f32_weights]],
        compiler_params=pltpu.CompilerParams(
            dimension_semantics=("arbitrary", "arbitrary"), vmem_limit_bytes=V7X_VMEM_LIMIT_BYTES),
        name="attn_prompt",
    )(sink_rows, qt, k, k, vt, vt, *f32_weights)


def _attention_sample(sink_rows, qt, cache_k, k, cache_vt, vt, cache_v, v, first_token, n_pairs):
    n = cache_k.shape[0] * CHUNK
    tq = n_pairs * PAIR
    assert first_token % tq == 0
    off = first_token // tq
    cache_spec = pl.BlockSpec((2 * n_pairs, WINDOW, PROJ_KV), lambda i: (i, 0, 0))
    return pl.pallas_call(
        functools.partial(_attn_sample_kernel, n_pairs=n_pairs),
        grid=(n // tq,),
        in_specs=[
            _SINK_SPEC,
            pl.BlockSpec((PROJ_Q, tq), lambda i: (0, i + off)),
            cache_spec,
            pl.BlockSpec((tq, PROJ_KV), lambda i: (i + off, 0)),
            pl.BlockSpec((PROJ_KV, 2 * n_pairs * WINDOW), lambda i: (0, i)),
            pl.BlockSpec((PROJ_KV, tq), lambda i: (0, i + off)),
            cache_spec,
            pl.BlockSpec((tq, PROJ_KV), lambda i: (i + off, 0)),
        ],
        out_specs=[pl.BlockSpec((PROJ_Q, tq), lambda i: (0, i)), cache_spec, cache_spec],
        out_shape=[jax.ShapeDtypeStruct((PROJ_Q, n), BF16),
                   jax.ShapeDtypeStruct(cache_k.shape, F32), jax.ShapeDtypeStruct(cache_v.shape, F32)],
        compiler_params=pltpu.CompilerParams(dimension_semantics=("arbitrary",)),
        name="attn_sample",
    )(sink_rows, qt, cache_k, k, cache_vt, vt, cache_v, v)


def _power_table(tau, nbits, a_re, a_im):
    rows = tau.shape[0]
    w_re = jnp.ones((rows, STATE_LANES), F32)
    w_im = jnp.zeros((rows, STATE_LANES), F32)
    p_re, p_im = a_re, a_im
    for k in range(nbits):
        bit = ((tau >> k) & 1) == 1
        f_re = jnp.where(bit, p_re, 1.0)
        f_im = jnp.where(bit, p_im, 0.0)
        w_re, w_im = w_re * f_re - w_im * f_im, w_re * f_im + w_im * f_re
        p_re, p_im = p_re * p_re - p_im * p_im, 2.0 * p_re * p_im
    return w_re, w_im


def _build_tables(par_ref, p_ref, qt_ref, coef_ref):
    lo = lax.broadcasted_iota(jnp.int32, (1, STATE_LANES), 1) < SSM_STATE
    mat = lambda i: par_ref[PAR_MATS + i * SSM_GROUP:PAR_MATS + (i + 1) * SSM_GROUP, :]
    lr, li = par_ref[0:1, :], par_ref[1:2, :]
    dt = jnp.exp(par_ref[2:3, :])
    mag = jnp.exp(lr * dt)
    a_re, a_im = mag * jnp.cos(li * dt), mag * jnp.sin(li * dt)
    nr, ni = a_re - 1.0, a_im
    den = lr * lr + li * li
    f_re, f_im = (nr * lr + ni * li) / den, (ni * lr - nr * li) / den
    b_re, b_im = mat(0), mat(1)
    bb_re = f_re * b_re - f_im * b_im
    bb_im = f_re * b_im + f_im * b_re
    c_re, c_im = mat(2), mat(3)

    tau = lax.broadcasted_iota(jnp.int32, (SSM_T, 1), 0)
    w_re, w_im = _power_table(tau, 6, a_re, a_im)
    w1_re, w1_im = w_re * a_re - w_im * a_im, w_re * a_im + w_im * a_re
    wr_re, wr_im = _power_table(SSM_T - 1 - tau, 6, a_re, a_im)

    def outer(c, w):
        return (c[:, None, :] * w[None, :, :]).reshape(SSM_CW, STATE_LANES)

    cw_mix = (outer(c_re, jnp.where(lo, w_re, w_im)) + outer(c_im, jnp.where(lo, -w_im, w_re)))
    bb_mix = jnp.where(lo, bb_re, -bb_im)
    strip = lax.dot_general(bb_mix, cw_mix, (((1,), (1,)), ((), ())),
                            precision=lax.Precision.HIGHEST,
                            preferred_element_type=F32)

    x_a, y_a = jnp.where(lo, bb_re, bb_im), jnp.where(lo, -bb_im, bb_re)
    p_ref[...] = (outer(x_a, wr_re) + outer(y_a, wr_im)).astype(BF16)

    qt_ref[...] = (outer(c_re, jnp.where(lo, w1_re, -w1_im))
                   + outer(c_im, jnp.where(lo, -w1_im, -w1_re))).astype(BF16)

    t_re, t_im = a_re, a_im
    for _ in range(6):
        t_re, t_im = t_re * t_re - t_im * t_im, 2.0 * t_re * t_im
    for k in range(COEF_ROWS // 2):
        coef_ref[2 * k:2 * k + 1, :] = t_re
        coef_ref[2 * k + 1:2 * k + 2, :] = jnp.where(lo, -t_im, t_im)
        t_re, t_im = t_re * t_re - t_im * t_im, 2.0 * t_re * t_im
    return strip


def _toeplitz_rows(strip, m_ref, channels):
    s_idx = lax.broadcasted_iota(jnp.int32, (SSM_T, SSM_CW), 0)
    t_idx = lax.broadcasted_iota(jnp.int32, (SSM_T, SSM_CW), 1) & (SSM_T - 1)
    causal = t_idx >= s_idx
    for c in channels:
        rows = jnp.broadcast_to(strip[c:c + 1, :], (SSM_T, SSM_CW))
        shifted = pltpu.roll(rows, 0, 1, stride=1, stride_axis=0)
        m_ref[c * SSM_T:(c + 1) * SSM_T, :] = jnp.where(causal, shifted, 0.0).astype(BF16)


PAR_MATS = 8
PAR_ROWS = PAR_MATS + 4 * SSM_GROUP


def _table_params(lam_re, lam_im, log_step, b_re, b_im, c_re, c_im):
    step = jnp.broadcast_to(log_step[:, None, None], (SSM_GROUPS, 1, SSM_STATE))
    filler = jnp.zeros((SSM_GROUPS, PAR_MATS - 3, SSM_STATE), F32)
    pack = jnp.concatenate([lam_re[:, None, :], lam_im[:, None, :], step, filler,
                            jnp.swapaxes(b_re, 1, 2), jnp.swapaxes(b_im, 1, 2), c_re, c_im], axis=1)
    return jnp.concatenate([pack, pack], axis=-1)


def _cmul(a1, a2, h, hs):
    return a1 * h + a2 * hs, a1 * hs - a2 * h


def _ssm_kernel(par_ref, u_ref, h0_ref, y_ref, hfin_ref, m_ref, p_ref, qt_ref, coef_ref,
                *, rows_per_seq, prompt_rows):
    for gi in range(SSM_GROUPS_PER_STEP):
        _ssm_group(par_ref.at[gi], u_ref, gi * SSM_GROUP, h0_ref.at[gi], y_ref, hfin_ref.at[gi],
                   m_ref.at[gi], p_ref.at[gi], qt_ref.at[gi], coef_ref.at[gi],
                   rows_per_seq=rows_per_seq, prompt_rows=prompt_rows)


def _ssm_group(par_ref, u_ref, c0, h0_ref, y_ref, hfin_ref, m_ref, p_ref, qt_ref, coef_ref,
               *, rows_per_seq, prompt_rows):
    strip = _build_tables(par_ref, p_ref, qt_ref, coef_ref)
    lo = lax.broadcasted_iota(jnp.int32, (1, ROW_TOKENS), 1) < SSM_T

    def chunk_rows(ref):
        even, odd = [], []
        rows = ref.shape[0] * ROW_GROUP
        for k in range(SSM_GROUP // 2):
            a = ref[:, c0 + 2 * k].reshape(rows, ROW_TOKENS)
            b = ref[:, c0 + 2 * k + 1].reshape(rows, ROW_TOKENS)
            even.append(jnp.where(lo, a, pltpu.roll(b, SSM_T, 1)))
            odd.append(jnp.where(lo, pltpu.roll(a, SSM_T, 1), b))
        return jnp.concatenate(even, axis=1), jnp.concatenate(odd, axis=1)

    def store_rows(y_even, y_odd, ref):
        for k in range(SSM_GROUP // 2):
            te = y_even[:, k * ROW_TOKENS:(k + 1) * ROW_TOKENS]
            to = y_odd[:, k * ROW_TOKENS:(k + 1) * ROW_TOKENS]
            tiles = (ref.shape[0], ROW_GROUP, ROW_TOKENS)
            ref[:, c0 + 2 * k] = jnp.where(lo, te, pltpu.roll(to, SSM_T, 1)).reshape(tiles)
            ref[:, c0 + 2 * k + 1] = jnp.where(lo, pltpu.roll(te, SSM_T, 1), to).reshape(tiles)

    even, odd = chunk_rows(u_ref)
    rp, rs = prompt_rows, even.shape[0] - prompt_rows
    u = jnp.concatenate([even[:rp], odd[:rp], even[rp:], odd[rp:]], axis=0).astype(BF16)
    s1 = jnp.dot(u, p_ref[...], preferred_element_type=F32)
    s2 = pltpu.roll(s1, SSM_STATE, 1)
    a1, a2 = coef_ref[0:1, :], coef_ref[1:2, :]

    e1, e2, o1, o2 = s1[:rp], s2[:rp], s1[rp:2 * rp], s2[rp:2 * rp]
    x1, x2 = _cmul(a1, a2, e1, e2)
    x1, x2 = x1 + o1, x2 + o2
    pos = lax.broadcasted_iota(jnp.int32, (rp, STATE_LANES), 0) & (rows_per_seq - 1)
    for k in range(rows_per_seq.bit_length() - 1):
        d = 1 << k
        b1, b2 = coef_ref[2 + 2 * k:3 + 2 * k, :], coef_ref[3 + 2 * k:4 + 2 * k, :]
        sh1 = jnp.where(pos >= d, pltpu.roll(x1, d, 0), 0.0)
        sh2 = jnp.where(pos >= d, pltpu.roll(x2, d, 0), 0.0)
        y1, y2 = _cmul(b1, b2, sh1, sh2)
        x1, x2 = x1 + y1, x2 + y2
    g1 = jnp.where(pos >= 1, pltpu.roll(x1, 1, 0), 0.0)
    g2 = jnp.where(pos >= 1, pltpu.roll(x2, 1, 0), 0.0)
    ho1 = _cmul(a1, a2, g1, g2)[0] + e1
    hfin_ref[0:STATE_OUT_ROWS, :] = jnp.zeros((STATE_OUT_ROWS, STATE_LANES), F32)
    for b in range(rp // rows_per_seq):
        last = (b + 1) * rows_per_seq - 1
        hfin_ref[b:b + 1, :] = x1[last:last + 1, :]

    h0e, h0o = h0_ref[0], h0_ref[1]
    swap = lambda h: pltpu.roll(h, SSM_STATE, 1)
    hfin_ref[STATE_OUT_ROWS:STATE_OUT_ROWS + rs, :] = a1 * h0e + a2 * swap(h0e) + s1[2 * rp:2 * rp + rs]
    hfin_ref[STATE_OUT_ROWS + rs:, :] = a1 * h0o + a2 * swap(h0o) + s1[2 * rp + rs:]

    hprev = jnp.concatenate([g1, ho1, h0e, h0o], axis=0).astype(BF16)
    y = lax.dot_general(hprev, qt_ref[...], (((1,), (1,)), ((), ())), preferred_element_type=F32)
    blocks = [range(c, c + M_BLOCK_CHANNELS) for c in range(0, SSM_GROUP, M_BLOCK_CHANNELS)]
    _toeplitz_rows(strip, m_ref, blocks[0])
    for i, channels in enumerate(blocks):
        rows = slice(channels[0] * SSM_T, (channels[-1] + 1) * SSM_T)
        y += jnp.dot(u[:, rows], m_ref[rows, :], preferred_element_type=F32)
        if i + 1 < len(blocks):
            _toeplitz_rows(strip, m_ref, blocks[i + 1])
    store_rows(jnp.concatenate([y[:rp], y[2 * rp:2 * rp + rs]], axis=0),
               jnp.concatenate([y[rp:2 * rp], y[2 * rp + rs:]], axis=0), y_ref)


def _ssm(table_params, u3, h0_pack, *, rows_per_seq, prompt_rows):
    groups = u3.shape[0]
    rs = groups * ROW_GROUP - prompt_rows
    u4 = u3.reshape(groups, SSM_WIDTH, ROW_GROUP, ROW_TOKENS)
    assert rows_per_seq & (rows_per_seq - 1) == 0 and 2 * rows_per_seq.bit_length() <= COEF_ROWS
    gps = SSM_GROUPS_PER_STEP
    g3 = lambda g: (g, 0, 0)
    u_spec = pl.BlockSpec((groups, gps * SSM_GROUP, ROW_GROUP, ROW_TOKENS), lambda g: (0, g, 0, 0))
    fin_rows = STATE_OUT_ROWS + 2 * rs
    return pl.pallas_call(
        functools.partial(_ssm_kernel, rows_per_seq=rows_per_seq, prompt_rows=prompt_rows),
        grid=(SSM_GROUPS // gps,),
        in_specs=[
            pl.BlockSpec((gps, PAR_ROWS, STATE_LANES), g3),
            u_spec,
            pl.BlockSpec((gps, 2, rs, STATE_LANES), lambda g: (g, 0, 0, 0)),
        ],
        out_specs=[
            u_spec,
            pl.BlockSpec((gps, fin_rows, STATE_LANES), g3),
        ],
        out_shape=[
            jax.ShapeDtypeStruct(u4.shape, F32),
            jax.ShapeDtypeStruct((SSM_GROUPS, fin_rows, STATE_LANES), F32),
        ],
        scratch_shapes=[
            pltpu.VMEM((gps, SSM_CW, SSM_CW), BF16),
            pltpu.VMEM((gps, SSM_CW, STATE_LANES), BF16),
            pltpu.VMEM((gps, SSM_CW, STATE_LANES), BF16),
            pltpu.VMEM((gps, COEF_ROWS, STATE_LANES), F32),
        ],
        compiler_params=pltpu.CompilerParams(
            dimension_semantics=("arbitrary",), vmem_limit_bytes=V7X_VMEM_LIMIT_BYTES),
        name="ssm",
    )(table_params, u4, h0_pack)


def _post_kernel(xp_ref, xs_ref, atp_ref, ats_ref, ut_ref, yt_ref, *refs, alpha, prompt_tiles):
    weights, (op_ref, os_ref) = refs[:-2], refs[-2:]

    @pl.when(pl.program_id(0) < prompt_tiles)
    def _():
        _post_tile(xp_ref, atp_ref, ut_ref, yt_ref, *weights, op_ref, alpha=alpha)

    @pl.when(pl.program_id(0) >= prompt_tiles)
    def _():
        _post_tile(xs_ref, ats_ref, ut_ref, yt_ref, *weights, os_ref, alpha=alpha)


def _post_tile(x_ref, at_ref, ut_ref, yt_ref, lng_ref, lnb_ref, d_ref, wglut_ref, bglu_ref,
               wout_ref, ln1g_ref, ln1b_ref, wgu_ref, wdown_ref, ln2g_ref, ln2b_ref, o_ref,
               *, alpha):
    xn = _layer_norm(x_ref[...], lng_ref[...], lnb_ref[...])
    n_rows = x_ref.shape[0] // ROW_TOKENS
    first_row = (pl.program_id(0) % (ROW_GROUP // n_rows)) * n_rows

    def feature_major(ref):
        return jnp.concatenate(
            [ref[pl.ds(first_row + j, SSM_WIDTH, stride=ROW_GROUP), :] for j in range(n_rows)], axis=1)

    ys = feature_major(yt_ref) + d_ref[...] * feature_major(ut_ref)
    gl = 0.5 * ys * (1.0 + lax.erf(ys * math.sqrt(0.5)))
    z = jnp.dot(wglut_ref[...], gl.astype(BF16), preferred_element_type=F32) + bglu_ref[...]
    s = gl * jax.nn.sigmoid(z)
    tn = (((0,), (0,)), ((), ()))
    mix = lax.dot_general(at_ref[...], wout_ref[:PROJ_Q, :], tn, preferred_element_type=F32)
    mix += lax.dot_general(s.astype(BF16), wout_ref[PROJ_Q:, :], tn, preferred_element_type=F32)
    h = _layer_norm(alpha * xn + mix, ln1g_ref[...], ln1b_ref[...])
    hb = h.astype(BF16)
    f = jnp.zeros_like(h)
    for start in range(0, D_FF, FF_BLOCK):
        stop = min(start + FF_BLOCK, D_FF)
        cols = slice(start, stop)
        up_cols = slice(D_FF + start, D_FF + stop)
        g = jnp.dot(hb, wgu_ref[:, cols], preferred_element_type=F32)
        up = jnp.dot(hb, wgu_ref[:, up_cols], preferred_element_type=F32)
        act = (g * jax.nn.sigmoid(g)) * up
        f += jnp.dot(act.astype(BF16), wdown_ref[cols, :], preferred_element_type=F32)
    o_ref[...] = _layer_norm(alpha * h + f, ln2g_ref[...], ln2b_ref[...])


def _post(xp2d, xs2d, atp, ats, ut3, yt3, ln_g, ln_b, d_col, w_glu_t, b_glu_col, w_out, ln1_g, ln1_b,
          w_gate_up, w_down, ln2_g, ln2_b, *, alpha, tm):
    n_p, n_s = xp2d.shape[0], xs2d.shape[0]
    tp, ts = n_p // tm, n_s // tm
    assert PROJ_TILE % tm == 0 and n_p % PROJ_TILE == 0
    group = lambda i: (i // (PROJ_TILE // tm), 0, 0)
    group_spec = pl.BlockSpec((None, SSM_WIDTH * ROW_GROUP, ROW_TOKENS), group)
    prompt = lambda i: jnp.minimum(i, tp - 1)
    sample = lambda i: jnp.maximum(i - tp, 0)
    const = lambda i: (0, 0)
    resident = lambda shape: pl.BlockSpec(shape, const, pipeline_mode=pl.Buffered(1))
    vec = lambda width: pl.BlockSpec((1, width), const)
    colvec = pl.BlockSpec((SSM_WIDTH, 1), const)
    return pl.pallas_call(
        functools.partial(_post_kernel, alpha=alpha, prompt_tiles=tp),
        grid=(tp + ts,),
        in_specs=[
            pl.BlockSpec((tm, D_MODEL), lambda i: (prompt(i), 0)),
            pl.BlockSpec((tm, D_MODEL), lambda i: (sample(i), 0)),
            pl.BlockSpec((PROJ_Q, tm), lambda i: (0, prompt(i))),
            pl.BlockSpec((PROJ_Q, tm), lambda i: (0, sample(i))),
            group_spec, group_spec,
            vec(D_MODEL), vec(D_MODEL), colvec,
            resident((SSM_WIDTH, SSM_WIDTH)), colvec,
            resident((D_MODEL, D_MODEL)), vec(D_MODEL), vec(D_MODEL),
            resident((D_MODEL, 2 * D_FF)), resident((D_FF, D_MODEL)),
            vec(D_MODEL), vec(D_MODEL),
        ],
        out_specs=[
            pl.BlockSpec((tm, D_MODEL), lambda i: (prompt(i), 0)),
            pl.BlockSpec((tm, D_MODEL), lambda i: (sample(i), 0)),
        ],
        out_shape=[
            jax.ShapeDtypeStruct((n_p, D_MODEL), F32),
            jax.ShapeDtypeStruct((n_s, D_MODEL), F32),
        ],
        compiler_params=pltpu.CompilerParams(
            dimension_semantics=("arbitrary",), vmem_limit_bytes=V7X_VMEM_LIMIT_BYTES),
        name="post",
    )(xp2d, xs2d, atp, ats, ut3, yt3, ln_g, ln_b, d_col, w_glu_t, b_glu_col, w_out, ln1_g, ln1_b,
      w_gate_up, w_down, ln2_g, ln2_b)


def kernel(x_prompt, x_sample, cache_win_k, cache_win_v, state_ssm_re, state_ssm_im,
           ln_in_g, ln_in_b, w_in, attn_sinks, ssm_lambda_re, ssm_lambda_im, ssm_log_step,
           ssm_b_re, ssm_b_im, ssm_c_re, ssm_c_im, ssm_d, w_glu, b_glu, w_out,
           ln1_g, ln1_b, w_gate_up, w_down, ln2_g, ln2_b):
    depth = w_in.shape[0]
    assert depth == 1, "single-layer step"
    bp, lp, _ = x_prompt.shape
    bs, ls, _ = x_sample.shape
    assert ls == SSM_T and bs % 2 == 0 and lp % 512 == 0 and bp <= STATE_OUT_ROWS
    win_rows = cache_win_k.shape[2]
    assert win_rows == WINDOW
    alpha = (2.0 * depth) ** 0.25
    l = 0
    row = lambda a: a.reshape(1, -1)
    column = lambda a: a.reshape(-1, 1)

    lng, lnb = row(ln_in_g), row(ln_in_b)
    xp2 = x_prompt.reshape(bp * lp, D_MODEL)
    xs2 = x_sample.reshape(bs * ls, D_MODEL)
    n_p = bp * lp
    qt, vt, u3, k, v = _proj(xp2, xs2, lng, lnb, w_in[l].T.astype(BF16))

    sink_rows = jnp.repeat(attn_sinks[l].reshape(N_KV_HEADS, Q_PER_KV), PAIR, axis=1)[:, None, :]
    atp, w_gu_b, w_down_b, w_out_b = _attention_prompt(
        sink_rows, qt, k, vt, bp, lp, n_pairs=ATTN_PAIRS, f32_weights=(w_gate_up[l], w_down[l], w_out[l]))
    ck = cache_win_k[l].reshape(bs, win_rows, PROJ_KV)
    cv = cache_win_v[l].reshape(bs, win_rows, PROJ_KV)
    cvt = jnp.transpose(cv, (2, 0, 1)).reshape(PROJ_KV, bs * win_rows).astype(BF16)
    ats, win_k_s, win_v_s = _attention_sample(sink_rows, qt, ck, k, cvt, vt, cv, v,
                                              first_token=n_p, n_pairs=ATTN_PAIRS)

    table_params = _table_params(
        ssm_lambda_re[l], ssm_lambda_im[l], ssm_log_step[l],
        ssm_b_re[l], ssm_b_im[l], ssm_c_re[l], ssm_c_im[l])
    h0 = jnp.concatenate([state_ssm_re[l], state_ssm_im[l]], axis=-1)
    h0_pack = jnp.transpose(h0.reshape(bs // 2, 2, SSM_GROUPS, STATE_LANES), (2, 1, 0, 3))
    y4, hfin = _ssm(table_params, u3, h0_pack,
                    rows_per_seq=lp // ROW_TOKENS, prompt_rows=n_p // ROW_TOKENS)

    post_args = (lng, lnb, column(ssm_d[l]), w_glu[l].T.astype(BF16), column(b_glu[l]),
                 w_out_b, row(ln1_g[l]), row(ln1_b[l]), w_gu_b, w_down_b, row(ln2_g[l]), row(ln2_b[l]))
    out_p, out_s = _post(xp2, xs2, atp, ats, u3, y4.reshape(u3.shape), *post_args, alpha=alpha, tm=512)

    kv_shape = (N_KV_HEADS, HEAD_DIM)
    last_rows = lambda a: jnp.stack([a[(b + 1) * lp - win_rows:(b + 1) * lp] for b in range(bp)])
    win_k_p = last_rows(k).reshape(1, bp, win_rows, *kv_shape)
    win_v_p = last_rows(v).reshape(1, bp, win_rows, *kv_shape)
    win_k_s = win_k_s.reshape(1, bs, win_rows, *kv_shape)
    win_v_s = win_v_s.reshape(1, bs, win_rows, *kv_shape)
    sp = jnp.swapaxes(hfin[:, :bp], 0, 1)
    ss = jnp.transpose(hfin[:, STATE_OUT_ROWS:].reshape(SSM_GROUPS, 2, bs // 2, STATE_LANES),
                       (2, 1, 0, 3)).reshape(bs, SSM_GROUPS, STATE_LANES)
    return (out_p.reshape(bp, lp, D_MODEL), out_s.reshape(bs, ls, D_MODEL),
            win_k_p, win_v_p, sp[None, ..., :SSM_STATE], sp[None, ..., SSM_STATE:],
            win_k_s, win_v_s, ss[None, ..., :SSM_STATE], ss[None, ..., SSM_STATE:])
```

```python
import functools
import math

import jax
import jax.numpy as jnp
from jax import lax
from jax.experimental import pallas as pl
from jax.experimental.pallas import tpu as pltpu

F32 = jnp.float32
BF16 = jnp.bfloat16

D_MODEL = 1024
HEAD_DIM = 64
N_HEADS = 8
N_KV_HEADS = 2
Q_PER_KV = N_HEADS // N_KV_HEADS
CHUNK = 64
WINDOW = 128
WIN_CHUNKS = WINDOW // CHUNK
BAND = (WIN_CHUNKS + 1) * CHUNK
PROJ_Q = N_HEADS * HEAD_DIM
PROJ_KV = N_KV_HEADS * HEAD_DIM
SSM_WIDTH = 512
SSM_GROUP = 16
SSM_GROUPS = SSM_WIDTH // SSM_GROUP
SSM_STATE = 64
STATE_LANES = 2 * SSM_STATE
D_FF = 2816
D_IN_PROJ = PROJ_Q + 2 * PROJ_KV + SSM_WIDTH
LN_EPS = 1e-5
NEG_INF = -1e30

SSM_T = CHUNK
SSM_CW = SSM_T * SSM_GROUP
ROW_TOKENS = 2 * SSM_T
ROW_GROUP = 8
PROJ_TILE = ROW_GROUP * ROW_TOKENS
PROJ_PARTS = 2
COEF_ROWS = 16
STATE_OUT_ROWS = 8
M_BLOCK_CHANNELS = 4
SSM_GROUPS_PER_STEP = 2
FF_BLOCK = 256
V7X_VMEM_LIMIT_BYTES = 56 * 1024 * 1024


def _layer_norm(x, g, b):
    mu = jnp.mean(x, axis=-1, keepdims=True)
    xc = x - mu
    var = jnp.mean(xc * xc, axis=-1, keepdims=True)
    return xc * lax.rsqrt(var + LN_EPS) * g + b


KV = 2 * PROJ_KV
VU = PROJ_KV + SSM_WIDTH


def _proj_kernel(xp_ref, xs_ref, g_ref, b_ref, wkvt_ref, wqt_ref, wvut_ref, qt_ref, vt_ref, ut_ref, k_ref, v_ref,
                 *, prompt_tiles):
    outs = (g_ref, b_ref, wkvt_ref, wqt_ref, wvut_ref, qt_ref, vt_ref, ut_ref, k_ref, v_ref)

    @pl.when(pl.program_id(0) < prompt_tiles)
    def _():
        _proj_tile(xp_ref, *outs)

    @pl.when(pl.program_id(0) >= prompt_tiles)
    def _():
        _proj_tile(xs_ref, *outs)


def _proj_tile(x_ref, g_ref, b_ref, wkvt_ref, wqt_ref, wvut_ref, qt_ref, vt_ref, ut_ref, k_ref, v_ref):
    rows_per_part = ROW_GROUP // PROJ_PARTS
    nt = (((1,), (1,)), ((), ()))
    for part in range(PROJ_PARTS):
        tok = slice(part * rows_per_part * ROW_TOKENS, (part + 1) * rows_per_part * ROW_TOKENS)
        xb = _layer_norm(x_ref[tok, :], g_ref[...], b_ref[...]).astype(BF16)
        p = lax.dot_general(xb, wkvt_ref[...], nt, preferred_element_type=F32)
        k_ref[tok, :] = p[:, :PROJ_KV]
        v_ref[tok, :] = p[:, PROJ_KV:]
        qt = lax.dot_general(wqt_ref[...], xb, nt, preferred_element_type=F32)
        vut = lax.dot_general(wvut_ref[...], xb, nt, preferred_element_type=F32)
        qt_ref[:, tok] = (qt * (HEAD_DIM ** -0.5)).astype(BF16)
        vt_ref[:, tok] = vut[:PROJ_KV].astype(BF16)
        for j in range(rows_per_part):
            ut_ref[pl.ds(part * rows_per_part + j, SSM_WIDTH, stride=ROW_GROUP), :] = (
                vut[PROJ_KV:, j * ROW_TOKENS:(j + 1) * ROW_TOKENS])


def _proj(xp2d, xs2d, ln_g, ln_b, w_in_t):
    assert D_IN_PROJ == 2 * VU and PROJ_Q % KV == 0
    tm = PROJ_TILE
    tp, ts = xp2d.shape[0] // tm, xs2d.shape[0] // tm
    n = (tp + ts) * tm
    const = lambda i: (0, 0)
    row = lambda i: (i, 0)
    col = lambda i: (0, i)
    return pl.pallas_call(
        functools.partial(_proj_kernel, prompt_tiles=tp),
        grid=(tp + ts,),
        in_specs=[
            pl.BlockSpec((tm, D_MODEL), lambda i: (jnp.minimum(i, tp - 1), 0)),
            pl.BlockSpec((tm, D_MODEL), lambda i: (jnp.maximum(i - tp, 0), 0)),
            pl.BlockSpec((1, D_MODEL), const),
            pl.BlockSpec((1, D_MODEL), const),
            pl.BlockSpec((KV, D_MODEL), lambda i: (PROJ_Q // KV, 0)),
            pl.BlockSpec((PROJ_Q, D_MODEL), const),
            pl.BlockSpec((VU, D_MODEL), lambda i: (1, 0)),
        ],
        out_specs=[
            pl.BlockSpec((PROJ_Q, tm), col),
            pl.BlockSpec((PROJ_KV, tm), col),
            pl.BlockSpec((None, SSM_WIDTH * ROW_GROUP, ROW_TOKENS), lambda i: (i, 0, 0)),
            pl.BlockSpec((tm, PROJ_KV), row),
            pl.BlockSpec((tm, PROJ_KV), row),
        ],
        out_shape=[
            jax.ShapeDtypeStruct((PROJ_Q, n), BF16),
            jax.ShapeDtypeStruct((PROJ_KV, n), BF16),
            jax.ShapeDtypeStruct((n // tm, SSM_WIDTH * ROW_GROUP, ROW_TOKENS), F32),
            jax.ShapeDtypeStruct((n, PROJ_KV), F32),
            jax.ShapeDtypeStruct((n, PROJ_KV), F32),
        ],
        compiler_params=pltpu.CompilerParams(
            dimension_semantics=("arbitrary",), vmem_limit_bytes=V7X_VMEM_LIMIT_BYTES),
        name="proj",
    )(xp2d, xs2d, ln_g, ln_b, w_in_t, w_in_t, w_in_t)


PAIR = 2 * CHUNK
HEAD_LANES = Q_PER_KV * PAIR
ONES_ROWS = 16
ATTN_PAIRS = 16
BF16_TILE_ROWS = 16


def _scores(unit, qt_ref):
    kwin, _, lanes, _, _, h = unit
    base = h * Q_PER_KV * HEAD_DIM
    qrow = jnp.concatenate(
        [qt_ref[base + g * HEAD_DIM:base + (g + 1) * HEAD_DIM, lanes] for g in range(Q_PER_KV)],
        axis=1)
    zero = jnp.zeros_like(qrow)
    qstack = jnp.concatenate([qrow, zero] if h == 0 else [zero, qrow], axis=0)
    return jnp.dot(kwin, qstack, preferred_element_type=F32)


def _finish(unit, s, sink_ref, o_ref):
    _, vtwin, lanes, valid, masked_rows, h = unit
    nk = s.shape[0]
    base = h * Q_PER_KV * HEAD_DIM
    pieces, done = [], 0
    for start, stop in masked_rows:
        if start > done:
            pieces.append(s[done:start])
        pieces.append(jnp.where(valid[start:stop], s[start:stop], NEG_INF))
        done = stop
    if done < nk:
        pieces.append(s[done:])
    s = jnp.concatenate(pieces, axis=0)
    sink = sink_ref[h]
    m = jnp.maximum(jnp.max(s, axis=0, keepdims=True), sink)
    p = jnp.exp(s - m).astype(BF16)
    v_ones = jnp.concatenate(
        [vtwin[h * HEAD_DIM:(h + 1) * HEAD_DIM, :], jnp.ones((ONES_ROWS, nk), BF16)], axis=0)
    ov = jnp.dot(v_ones, p, preferred_element_type=F32)
    den = ov[HEAD_DIM:HEAD_DIM + 1, :] + jnp.exp(sink - m)
    o = ov[:HEAD_DIM, :] * (1.0 / den)
    for g in range(Q_PER_KV):
        o_ref[base + g * HEAD_DIM:base + (g + 1) * HEAD_DIM, lanes] = (
            o[:, g * PAIR:(g + 1) * PAIR].astype(BF16))


def _attend_units(units, qt_ref, sink_ref, o_ref):
    s_next = _scores(units[0], qt_ref)
    for i, unit in enumerate(units):
        s = s_next
        if i + 1 < len(units):
            s_next = _scores(units[i + 1], qt_ref)
        _finish(unit, s, sink_ref, o_ref)


def _attn_prompt_kernel(sink_ref, qt_ref, kp_ref, kc_ref, vtp_ref, vtc_ref, *refs, n_pairs):
    n_cast = (len(refs) - 1) // 2
    o_ref = refs[n_cast]
    for w_ref, wb_ref in zip(refs[:n_cast], refs[n_cast + 1:]):
        wb_ref[...] = w_ref[...].astype(BF16)
    tq = n_pairs * PAIR
    nk = WINDOW + PAIR
    kk = jnp.concatenate([kp_ref[...], kc_ref[...]], axis=0).astype(BF16)
    vt = jnp.concatenate([vtp_ref[...], vtc_ref[...]], axis=1)
    r = lax.broadcasted_iota(jnp.int32, (nk, HEAD_LANES), 0)
    first_chunk = (lax.broadcasted_iota(jnp.int32, (nk, HEAD_LANES), 1) & (PAIR - 1)) < CHUNK
    lo = jnp.where(first_chunk, 0, CHUNK)
    hi = jnp.where(first_chunk, BAND, nk)
    first_pos = pl.program_id(1) * tq - WINDOW
    units = []
    for pp in range(n_pairs):
        lo_pp = jnp.maximum(lo, -first_pos) if pp == 0 else lo
        valid = (r >= lo_pp) & (r < hi)
        masked_rows = ((0, nk),) if pp == 0 else ((0, CHUNK), (nk - CHUNK, nk))
        for h in range(N_KV_HEADS):
            units.append((kk[pp * PAIR:pp * PAIR + nk], vt[:, pp * PAIR:pp * PAIR + nk],
                          slice(pp * PAIR, (pp + 1) * PAIR), valid, masked_rows, h))
    _attend_units(units, qt_ref, sink_ref, o_ref)


def _attn_sample_kernel(sink_ref, qt_ref, ck_ref, kn_ref, cvt_ref, vtn_ref, o_ref, *, n_pairs):
    nk = 2 * WINDOW + PAIR
    r = lax.broadcasted_iota(jnp.int32, (nk, HEAD_LANES), 0)
    query_seq = (lax.broadcasted_iota(jnp.int32, (nk, HEAD_LANES), 1) & (PAIR - 1)) >> 6
    key_seq = jnp.where(r < 2 * WINDOW, r >> 7, (r - 2 * WINDOW) >> 6)
    valid = query_seq == key_seq
    units = []
    for pp in range(n_pairs):
        lanes = slice(pp * PAIR, (pp + 1) * PAIR)
        kwin = jnp.concatenate([ck_ref[2 * pp], ck_ref[2 * pp + 1], kn_ref[lanes, :]],
                               axis=0).astype(BF16)
        vtwin = jnp.concatenate([cvt_ref[:, 2 * pp * WINDOW:2 * (pp + 1) * WINDOW], vtn_ref[:, lanes]],
                                axis=1)
        units.extend((kwin, vtwin, lanes, valid, ((0, nk),), h) for h in range(N_KV_HEADS))
    _attend_units(units, qt_ref, sink_ref, o_ref)


_SINK_SPEC = pl.BlockSpec((N_KV_HEADS, 1, HEAD_LANES), lambda *_: (0, 0, 0))


def _attention_prompt(sink_rows, qt, k, vt, bsz, seq, n_pairs, f32_weights):
    tq = n_pairs * PAIR
    nt = seq // tq
    steps = bsz * nt
    wpt = tq // WINDOW
    cur_c = lambda b, i: (0, b * nt + i)
    cur_r = lambda b, i: (b * nt + i, 0)
    prev = lambda b, i: jnp.maximum((b * nt + i) * wpt - 1, 0)
    assert all(w.shape[0] % (steps * BF16_TILE_ROWS) == 0 for w in f32_weights)
    cast_specs = [pl.BlockSpec((w.shape[0] // steps, w.shape[1]), cur_r) for w in f32_weights]
    return pl.pallas_call(
        functools.partial(_attn_prompt_kernel, n_pairs=n_pairs),
        grid=(bsz, nt),
        in_specs=[
            _SINK_SPEC,
            pl.BlockSpec((PROJ_Q, tq), cur_c),
            pl.BlockSpec((WINDOW, PROJ_KV), lambda b, i: (prev(b, i), 0)),
            pl.BlockSpec((tq, PROJ_KV), cur_r),
            pl.BlockSpec((PROJ_KV, WINDOW), lambda b, i: (0, prev(b, i))),
            pl.BlockSpec((PROJ_KV, tq), cur_c),
            *cast_specs,
        ],
        out_specs=[pl.BlockSpec((PROJ_Q, tq), cur_c), *cast_specs],
        out_shape=[jax.ShapeDtypeStruct((PROJ_Q, bsz * seq), BF16),
                   *[jax.ShapeDtypeStruct(w.shape, BF16) for w in f32_weights]],
        compiler_params=pltpu.CompilerParams(
            dimension_semantics=("arbitrary", "arbitrary"), vmem_limit_bytes=V7X_VMEM_LIMIT_BYTES),
        name="attn_prompt",
    )(sink_rows, qt, k, k, vt, vt, *f32_weights)


def _attention_sample(sink_rows, qt, cache_k, k, cache_vt, vt, first_token, n_pairs):
    n = cache_k.shape[0] * CHUNK
    tq = n_pairs * PAIR
    assert first_token % tq == 0
    off = first_token // tq
    return pl.pallas_call(
        functools.partial(_attn_sample_kernel, n_pairs=n_pairs),
        grid=(n // tq,),
        in_specs=[
            _SINK_SPEC,
            pl.BlockSpec((PROJ_Q, tq), lambda i: (0, i + off)),
            pl.BlockSpec((2 * n_pairs, WINDOW, PROJ_KV), lambda i: (i, 0, 0)),
            pl.BlockSpec((tq, PROJ_KV), lambda i: (i + off, 0)),
            pl.BlockSpec((PROJ_KV, 2 * n_pairs * WINDOW), lambda i: (0, i)),
            pl.BlockSpec((PROJ_KV, tq), lambda i: (0, i + off)),
        ],
        out_specs=pl.BlockSpec((PROJ_Q, tq), lambda i: (0, i)),
        out_shape=jax.ShapeDtypeStruct((PROJ_Q, n), BF16),
        compiler_params=pltpu.CompilerParams(dimension_semantics=("arbitrary",)),
        name="attn_sample",
    )(sink_rows, qt, cache_k, k, cache_vt, vt)


def _power_table(tau, nbits, a_re, a_im):
    rows = tau.shape[0]
    w_re = jnp.ones((rows, STATE_LANES), F32)
    w_im = jnp.zeros((rows, STATE_LANES), F32)
    p_re, p_im = a_re, a_im
    for k in range(nbits):
        bit = ((tau >> k) & 1) == 1
        f_re = jnp.where(bit, p_re, 1.0)
        f_im = jnp.where(bit, p_im, 0.0)
        w_re, w_im = w_re * f_re - w_im * f_im, w_re * f_im + w_im * f_re
        p_re, p_im = p_re * p_re - p_im * p_im, 2.0 * p_re * p_im
    return w_re, w_im


def _build_tables(par_ref, p_ref, qt_ref, coef_ref):
    lo = lax.broadcasted_iota(jnp.int32, (1, STATE_LANES), 1) < SSM_STATE
    mat = lambda i: par_ref[PAR_MATS + i * SSM_GROUP:PAR_MATS + (i + 1) * SSM_GROUP, :]
    lr, li = par_ref[0:1, :], par_ref[1:2, :]
    dt = jnp.exp(par_ref[2:3, :])
    mag = jnp.exp(lr * dt)
    a_re, a_im = mag * jnp.cos(li * dt), mag * jnp.sin(li * dt)
    nr, ni = a_re - 1.0, a_im
    den = lr * lr + li * li
    f_re, f_im = (nr * lr + ni * li) / den, (ni * lr - nr * li) / den
    b_re, b_im = mat(0), mat(1)
    bb_re = f_re * b_re - f_im * b_im
    bb_im = f_re * b_im + f_im * b_re
    c_re, c_im = mat(2), mat(3)

    tau = lax.broadcasted_iota(jnp.int32, (SSM_T, 1), 0)
    w_re, w_im = _power_table(tau, 6, a_re, a_im)
    w1_re, w1_im = w_re * a_re - w_im * a_im, w_re * a_im + w_im * a_re
    wr_re, wr_im = _power_table(SSM_T - 1 - tau, 6, a_re, a_im)

    def outer(c, w):
        return (c[:, None, :] * w[None, :, :]).reshape(SSM_CW, STATE_LANES)

    cw_mix = (outer(c_re, jnp.where(lo, w_re, w_im)) + outer(c_im, jnp.where(lo, -w_im, w_re)))
    bb_mix = jnp.where(lo, bb_re, -bb_im)
    strip = lax.dot_general(bb_mix, cw_mix, (((1,), (1,)), ((), ())),
                            precision=lax.Precision.HIGHEST,
                            preferred_element_type=F32)

    x_a, y_a = jnp.where(lo, bb_re, bb_im), jnp.where(lo, -bb_im, bb_re)
    p_ref[...] = (outer(x_a, wr_re) + outer(y_a, wr_im)).astype(BF16)

    qt_ref[...] = (outer(c_re, jnp.where(lo, w1_re, -w1_im))
                   + outer(c_im, jnp.where(lo, -w1_im, -w1_re))).astype(BF16)

    t_re, t_im = a_re, a_im
    for _ in range(6):
        t_re, t_im = t_re * t_re - t_im * t_im, 2.0 * t_re * t_im
    for k in range(COEF_ROWS // 2):
        coef_ref[2 * k:2 * k + 1, :] = t_re
        coef_ref[2 * k + 1:2 * k + 2, :] = jnp.where(lo, -t_im, t_im)
        t_re, t_im = t_re * t_re - t_im * t_im, 2.0 * t_re * t_im
    return strip


def _toeplitz_rows(strip, m_ref, channels):
    s_idx = lax.broadcasted_iota(jnp.int32, (SSM_T, SSM_CW), 0)
    t_idx = lax.broadcasted_iota(jnp.int32, (SSM_T, SSM_CW), 1) & (SSM_T - 1)
    causal = t_idx >= s_idx
    for c in channels:
        rows = jnp.broadcast_to(strip[c:c + 1, :], (SSM_T, SSM_CW))
        shifted = pltpu.roll(rows, 0, 1, stride=1, stride_axis=0)
        m_ref[c * SSM_T:(c + 1) * SSM_T, :] = jnp.where(causal, shifted, 0.0).astype(BF16)


PAR_MATS = 8
PAR_ROWS = PAR_MATS + 4 * SSM_GROUP


def _table_params(lam_re, lam_im, log_step, b_re, b_im, c_re, c_im):
    step = jnp.broadcast_to(log_step[:, None, None], (SSM_GROUPS, 1, SSM_STATE))
    filler = jnp.zeros((SSM_GROUPS, PAR_MATS - 3, SSM_STATE), F32)
    pack = jnp.concatenate([lam_re[:, None, :], lam_im[:, None, :], step, filler,
                            jnp.swapaxes(b_re, 1, 2), jnp.swapaxes(b_im, 1, 2), c_re, c_im], axis=1)
    return jnp.concatenate([pack, pack], axis=-1)


def _cmul(a1, a2, h, hs):
    return a1 * h + a2 * hs, a1 * hs - a2 * h


def _ssm_kernel(par_ref, u_ref, h0_ref, y_ref, hfin_ref, m_ref, p_ref, qt_ref, coef_ref,
                *, rows_per_seq, prompt_rows):
    for gi in range(SSM_GROUPS_PER_STEP):
        _ssm_group(par_ref.at[gi], u_ref, gi * SSM_GROUP, h0_ref.at[gi], y_ref, hfin_ref.at[gi],
                   m_ref.at[gi], p_ref.at[gi], qt_ref.at[gi], coef_ref.at[gi],
                   rows_per_seq=rows_per_seq, prompt_rows=prompt_rows)


def _ssm_group(par_ref, u_ref, c0, h0_ref, y_ref, hfin_ref, m_ref, p_ref, qt_ref, coef_ref,
               *, rows_per_seq, prompt_rows):
    strip = _build_tables(par_ref, p_ref, qt_ref, coef_ref)
    lo = lax.broadcasted_iota(jnp.int32, (1, ROW_TOKENS), 1) < SSM_T

    def chunk_rows(ref):
        even, odd = [], []
        rows = ref.shape[0] * ROW_GROUP
        for k in range(SSM_GROUP // 2):
            a = ref[:, c0 + 2 * k].reshape(rows, ROW_TOKENS)
            b = ref[:, c0 + 2 * k + 1].reshape(rows, ROW_TOKENS)
            even.append(jnp.where(lo, a, pltpu.roll(b, SSM_T, 1)))
            odd.append(jnp.where(lo, pltpu.roll(a, SSM_T, 1), b))
        return jnp.concatenate(even, axis=1), jnp.concatenate(odd, axis=1)

    def store_rows(y_even, y_odd, ref):
        for k in range(SSM_GROUP // 2):
            te = y_even[:, k * ROW_TOKENS:(k + 1) * ROW_TOKENS]
            to = y_odd[:, k * ROW_TOKENS:(k + 1) * ROW_TOKENS]
            tiles = (ref.shape[0], ROW_GROUP, ROW_TOKENS)
            ref[:, c0 + 2 * k] = jnp.where(lo, te, pltpu.roll(to, SSM_T, 1)).reshape(tiles)
            ref[:, c0 + 2 * k + 1] = jnp.where(lo, pltpu.roll(te, SSM_T, 1), to).reshape(tiles)

    even, odd = chunk_rows(u_ref)
    rp, rs = prompt_rows, even.shape[0] - prompt_rows
    u = jnp.concatenate([even[:rp], odd[:rp], even[rp:], odd[rp:]], axis=0).astype(BF16)
    s1 = jnp.dot(u, p_ref[...], preferred_element_type=F32)
    s2 = pltpu.roll(s1, SSM_STATE, 1)
    a1, a2 = coef_ref[0:1, :], coef_ref[1:2, :]

    e1, e2, o1, o2 = s1[:rp], s2[:rp], s1[rp:2 * rp], s2[rp:2 * rp]
    x1, x2 = _cmul(a1, a2, e1, e2)
    x1, x2 = x1 + o1, x2 + o2
    pos = lax.broadcasted_iota(jnp.int32, (rp, STATE_LANES), 0) & (rows_per_seq - 1)
    for k in range(rows_per_seq.bit_length() - 1):
        d = 1 << k
        b1, b2 = coef_ref[2 + 2 * k:3 + 2 * k, :], coef_ref[3 + 2 * k:4 + 2 * k, :]
        sh1 = jnp.where(pos >= d, pltpu.roll(x1, d, 0), 0.0)
        sh2 = jnp.where(pos >= d, pltpu.roll(x2, d, 0), 0.0)
        y1, y2 = _cmul(b1, b2, sh1, sh2)
        x1, x2 = x1 + y1, x2 + y2
    g1 = jnp.where(pos >= 1, pltpu.roll(x1, 1, 0), 0.0)
    g2 = jnp.where(pos >= 1, pltpu.roll(x2, 1, 0), 0.0)
    ho1 = _cmul(a1, a2, g1, g2)[0] + e1
    hfin_ref[0:STATE_OUT_ROWS, :] = jnp.zeros((STATE_OUT_ROWS, STATE_LANES), F32)
    for b in range(rp // rows_per_seq):
        last = (b + 1) * rows_per_seq - 1
        hfin_ref[b:b + 1, :] = x1[last:last + 1, :]

    h0e, h0o = h0_ref[0], h0_ref[1]
    swap = lambda h: pltpu.roll(h, SSM_STATE, 1)
    hfin_ref[STATE_OUT_ROWS:STATE_OUT_ROWS + rs, :] = a1 * h0e + a2 * swap(h0e) + s1[2 * rp:2 * rp + rs]
    hfin_ref[STATE_OUT_ROWS + rs:, :] = a1 * h0o + a2 * swap(h0o) + s1[2 * rp + rs:]

    hprev = jnp.concatenate([g1, ho1, h0e, h0o], axis=0).astype(BF16)
    y = lax.dot_general(hprev, qt_ref[...], (((1,), (1,)), ((), ())), preferred_element_type=F32)
    blocks = [range(c, c + M_BLOCK_CHANNELS) for c in range(0, SSM_GROUP, M_BLOCK_CHANNELS)]
    _toeplitz_rows(strip, m_ref, blocks[0])
    for i, channels in enumerate(blocks):
        rows = slice(channels[0] * SSM_T, (channels[-1] + 1) * SSM_T)
        y += jnp.dot(u[:, rows], m_ref[rows, :], preferred_element_type=F32)
        if i + 1 < len(blocks):
            _toeplitz_rows(strip, m_ref, blocks[i + 1])
    store_rows(jnp.concatenate([y[:rp], y[2 * rp:2 * rp + rs]], axis=0),
               jnp.concatenate([y[rp:2 * rp], y[2 * rp + rs:]], axis=0), y_ref)


def _ssm(table_params, u3, h0_pack, *, rows_per_seq, prompt_rows):
    groups = u3.shape[0]
    rs = groups * ROW_GROUP - prompt_rows
    u4 = u3.reshape(groups, SSM_WIDTH, ROW_GROUP, ROW_TOKENS)
    assert rows_per_seq & (rows_per_seq - 1) == 0 and 2 * rows_per_seq.bit_length() <= COEF_ROWS
    gps = SSM_GROUPS_PER_STEP
    g3 = lambda g: (g, 0, 0)
    u_spec = pl.BlockSpec((groups, gps * SSM_GROUP, ROW_GROUP, ROW_TOKENS), lambda g: (0, g, 0, 0))
    fin_rows = STATE_OUT_ROWS + 2 * rs
    return pl.pallas_call(
        functools.partial(_ssm_kernel, rows_per_seq=rows_per_seq, prompt_rows=prompt_rows),
        grid=(SSM_GROUPS // gps,),
        in_specs=[
            pl.BlockSpec((gps, PAR_ROWS, STATE_LANES), g3),
            u_spec,
            pl.BlockSpec((gps, 2, rs, STATE_LANES), lambda g: (g, 0, 0, 0)),
        ],
        out_specs=[
            u_spec,
            pl.BlockSpec((gps, fin_rows, STATE_LANES), g3),
        ],
        out_shape=[
            jax.ShapeDtypeStruct(u4.shape, F32),
            jax.ShapeDtypeStruct((SSM_GROUPS, fin_rows, STATE_LANES), F32),
        ],
        scratch_shapes=[
            pltpu.VMEM((gps, SSM_CW, SSM_CW), BF16),
            pltpu.VMEM((gps, SSM_CW, STATE_LANES), BF16),
            pltpu.VMEM((gps, SSM_CW, STATE_LANES), BF16),
            pltpu.VMEM((gps, COEF_ROWS, STATE_LANES), F32),
        ],
        compiler_params=pltpu.CompilerParams(
            dimension_semantics=("arbitrary",), vmem_limit_bytes=V7X_VMEM_LIMIT_BYTES),
        name="ssm",
    )(table_params, u4, h0_pack)


def _post_kernel(xp_ref, xs_ref, atp_ref, ats_ref, ut_ref, yt_ref, *refs, alpha, prompt_tiles):
    weights, (op_ref, os_ref) = refs[:-2], refs[-2:]

    @pl.when(pl.program_id(0) < prompt_tiles)
    def _():
        _post_tile(xp_ref, atp_ref, ut_ref, yt_ref, *weights, op_ref, alpha=alpha)

    @pl.when(pl.program_id(0) >= prompt_tiles)
    def _():
        _post_tile(xs_ref, ats_ref, ut_ref, yt_ref, *weights, os_ref, alpha=alpha)


def _post_tile(x_ref, at_ref, ut_ref, yt_ref, lng_ref, lnb_ref, d_ref, wglut_ref, bglu_ref,
               wout_ref, ln1g_ref, ln1b_ref, wgu_ref, wdown_ref, ln2g_ref, ln2b_ref, o_ref,
               *, alpha):
    xn = _layer_norm(x_ref[...], lng_ref[...], lnb_ref[...])
    n_rows = x_ref.shape[0] // ROW_TOKENS
    first_row = (pl.program_id(0) % (ROW_GROUP // n_rows)) * n_rows

    def feature_major(ref):
        return jnp.concatenate(
            [ref[pl.ds(first_row + j, SSM_WIDTH, stride=ROW_GROUP), :] for j in range(n_rows)], axis=1)

    ys = feature_major(yt_ref) + d_ref[...] * feature_major(ut_ref)
    gl = 0.5 * ys * (1.0 + lax.erf(ys * math.sqrt(0.5)))
    z = jnp.dot(wglut_ref[...], gl.astype(BF16), preferred_element_type=F32) + bglu_ref[...]
    s = gl * jax.nn.sigmoid(z)
    tn = (((0,), (0,)), ((), ()))
    mix = lax.dot_general(at_ref[...], wout_ref[:PROJ_Q, :], tn, preferred_element_type=F32)
    mix += lax.dot_general(s.astype(BF16), wout_ref[PROJ_Q:, :], tn, preferred_element_type=F32)
    h = _layer_norm(alpha * xn + mix, ln1g_ref[...], ln1b_ref[...])
    hb = h.astype(BF16)
    f = jnp.zeros_like(h)
    for start in range(0, D_FF, FF_BLOCK):
        stop = min(start + FF_BLOCK, D_FF)
        cols = slice(start, stop)
        up_cols = slice(D_FF + start, D_FF + stop)
        g = jnp.dot(hb, wgu_ref[:, cols], preferred_element_type=F32)
        up = jnp.dot(hb, wgu_ref[:, up_cols], preferred_element_type=F32)
        act = (g * jax.nn.sigmoid(g)) * up
        f += jnp.dot(act.astype(BF16), wdown_ref[cols, :], preferred_element_type=F32)
    o_ref[...] = _layer_norm(alpha * h + f, ln2g_ref[...], ln2b_ref[...])


def _post(xp2d, xs2d, atp, ats, ut3, yt3, ln_g, ln_b, d_col, w_glu_t, b_glu_col, w_out, ln1_g, ln1_b,
          w_gate_up, w_down, ln2_g, ln2_b, *, alpha, tm):
    n_p, n_s = xp2d.shape[0], xs2d.shape[0]
    tp, ts = n_p // tm, n_s // tm
    assert PROJ_TILE % tm == 0 and n_p % PROJ_TILE == 0
    group = lambda i: (i // (PROJ_TILE // tm), 0, 0)
    group_spec = pl.BlockSpec((None, SSM_WIDTH * ROW_GROUP, ROW_TOKENS), group)
    prompt = lambda i: jnp.minimum(i, tp - 1)
    sample = lambda i: jnp.maximum(i - tp, 0)
    const = lambda i: (0, 0)
    resident = lambda shape: pl.BlockSpec(shape, const, pipeline_mode=pl.Buffered(1))
    vec = lambda width: pl.BlockSpec((1, width), const)
    colvec = pl.BlockSpec((SSM_WIDTH, 1), const)
    return pl.pallas_call(
        functools.partial(_post_kernel, alpha=alpha, prompt_tiles=tp),
        grid=(tp + ts,),
        in_specs=[
            pl.BlockSpec((tm, D_MODEL), lambda i: (prompt(i), 0)),
            pl.BlockSpec((tm, D_MODEL), lambda i: (sample(i), 0)),
            pl.BlockSpec((PROJ_Q, tm), lambda i: (0, prompt(i))),
            pl.BlockSpec((PROJ_Q, tm), lambda i: (0, sample(i))),
            group_spec, group_spec,
            vec(D_MODEL), vec(D_MODEL), colvec,
            resident((SSM_WIDTH, SSM_WIDTH)), colvec,
            resident((D_MODEL, D_MODEL)), vec(D_MODEL), vec(D_MODEL),
            resident((D_MODEL, 2 * D_FF)), resident((D_FF, D_MODEL)),
            vec(D_MODEL), vec(D_MODEL),
        ],
        out_specs=[
            pl.BlockSpec((tm, D_MODEL), lambda i: (prompt(i), 0)),
            pl.BlockSpec((tm, D_MODEL), lambda i: (sample(i), 0)),
        ],
        out_shape=[
            jax.ShapeDtypeStruct((n_p, D_MODEL), F32),
            jax.ShapeDtypeStruct((n_s, D_MODEL), F32),
        ],
        compiler_params=pltpu.CompilerParams(
            dimension_semantics=("arbitrary",), vmem_limit_bytes=V7X_VMEM_LIMIT_BYTES),
        name="post",
    )(xp2d, xs2d, atp, ats, ut3, yt3, ln_g, ln_b, d_col, w_glu_t, b_glu_col, w_out, ln1_g, ln1_b,
      w_gate_up, w_down, ln2_g, ln2_b)


def kernel(x_prompt, x_sample, cache_win_k, cache_win_v, state_ssm_re, state_ssm_im,
           ln_in_g, ln_in_b, w_in, attn_sinks, ssm_lambda_re, ssm_lambda_im, ssm_log_step,
           ssm_b_re, ssm_b_im, ssm_c_re, ssm_c_im, ssm_d, w_glu, b_glu, w_out,
           ln1_g, ln1_b, w_gate_up, w_down, ln2_g, ln2_b):
    depth = w_in.shape[0]
    assert depth == 1, "single-layer step"
    bp, lp, _ = x_prompt.shape
    bs, ls, _ = x_sample.shape
    assert ls == SSM_T and bs % 2 == 0 and lp % 512 == 0 and bp <= STATE_OUT_ROWS
    win_rows = cache_win_k.shape[2]
    assert win_rows == WINDOW
    alpha = (2.0 * depth) ** 0.25
    l = 0
    row = lambda a: a.reshape(1, -1)
    column = lambda a: a.reshape(-1, 1)

    lng, lnb = row(ln_in_g), row(ln_in_b)
    xp2 = x_prompt.reshape(bp * lp, D_MODEL)
    xs2 = x_sample.reshape(bs * ls, D_MODEL)
    n_p = bp * lp
    qt, vt, u3, k, v = _proj(xp2, xs2, lng, lnb, w_in[l].T.astype(BF16))

    sink_rows = jnp.repeat(attn_sinks[l].reshape(N_KV_HEADS, Q_PER_KV), PAIR, axis=1)[:, None, :]
    atp, w_gu_b, w_down_b, w_out_b = _attention_prompt(
        sink_rows, qt, k, vt, bp, lp, n_pairs=ATTN_PAIRS, f32_weights=(w_gate_up[l], w_down[l], w_out[l]))
    ck = cache_win_k[l].reshape(bs, win_rows, PROJ_KV)
    cv = cache_win_v[l].reshape(bs, win_rows, PROJ_KV)
    cvt = jnp.transpose(cv, (2, 0, 1)).reshape(PROJ_KV, bs * win_rows).astype(BF16)
    ats = _attention_sample(sink_rows, qt, ck, k, cvt, vt, first_token=n_p, n_pairs=ATTN_PAIRS)
    ks3, vs3 = k[n_p:].reshape(bs, ls, PROJ_KV), v[n_p:].reshape(bs, ls, PROJ_KV)

    table_params = _table_params(
        ssm_lambda_re[l], ssm_lambda_im[l], ssm_log_step[l],
        ssm_b_re[l], ssm_b_im[l], ssm_c_re[l], ssm_c_im[l])
    h0 = jnp.concatenate([state_ssm_re[l], state_ssm_im[l]], axis=-1)
    h0_pack = jnp.transpose(h0.reshape(bs // 2, 2, SSM_GROUPS, STATE_LANES), (2, 1, 0, 3))
    y4, hfin = _ssm(table_params, u3, h0_pack,
                    rows_per_seq=lp // ROW_TOKENS, prompt_rows=n_p // ROW_TOKENS)

    post_args = (lng, lnb, column(ssm_d[l]), w_glu[l].T.astype(BF16), column(b_glu[l]),
                 w_out_b, row(ln1_g[l]), row(ln1_b[l]), w_gu_b, w_down_b, row(ln2_g[l]), row(ln2_b[l]))
    out_p, out_s = _post(xp2, xs2, atp, ats, u3, y4.reshape(u3.shape), *post_args, alpha=alpha, tm=512)

    kv_shape = (N_KV_HEADS, HEAD_DIM)
    last_rows = lambda a: jnp.stack([a[(b + 1) * lp - win_rows:(b + 1) * lp] for b in range(bp)])
    win_k_p = last_rows(k).reshape(1, bp, win_rows, *kv_shape)
    win_v_p = last_rows(v).reshape(1, bp, win_rows, *kv_shape)
    win_k_s = jnp.concatenate([ck, ks3], axis=1)[:, -win_rows:].reshape(1, bs, win_rows, *kv_shape)
    win_v_s = jnp.concatenate([cv, vs3], axis=1)[:, -win_rows:].reshape(1, bs, win_rows, *kv_shape)
    sp = jnp.swapaxes(hfin[:, :bp], 0, 1)
    ss = jnp.transpose(hfin[:, STATE_OUT_ROWS:].reshape(SSM_GROUPS, 2, bs // 2, STATE_LANES),
                       (2, 1, 0, 3)).reshape(bs, SSM_GROUPS, STATE_LANES)
    return (out_p.reshape(bp, lp, D_MODEL), out_s.reshape(bs, ls, D_MODEL),
            win_k_p, win_v_p, sp[None, ..., :SSM_STATE], sp[None, ..., SSM_STATE:],
            win_k_s, win_v_s, ss[None, ..., :SSM_STATE], ss[None, ..., SSM_STATE:])
```

```python
import functools
import math

import jax
import jax.numpy as jnp
from jax import lax
from jax.experimental import pallas as pl
from jax.experimental.pallas import tpu as pltpu

F32 = jnp.float32
BF16 = jnp.bfloat16

D_MODEL = 1024
HEAD_DIM = 64
N_HEADS = 8
N_KV_HEADS = 2
Q_PER_KV = N_HEADS // N_KV_HEADS
CHUNK = 64
WINDOW = 128
WIN_CHUNKS = WINDOW // CHUNK
BAND = (WIN_CHUNKS + 1) * CHUNK
PROJ_Q = N_HEADS * HEAD_DIM
PROJ_KV = N_KV_HEADS * HEAD_DIM
SSM_WIDTH = 512
SSM_GROUP = 16
SSM_GROUPS = SSM_WIDTH // SSM_GROUP
SSM_STATE = 64
STATE_LANES = 2 * SSM_STATE
D_FF = 2816
D_IN_PROJ = PROJ_Q + 2 * PROJ_KV + SSM_WIDTH
LN_EPS = 1e-5
NEG_INF = -1e30

SSM_T = CHUNK
SSM_CW = SSM_T * SSM_GROUP
ROW_TOKENS = 2 * SSM_T
ROW_GROUP = 8
PROJ_TILE = ROW_GROUP * ROW_TOKENS
PROJ_PARTS = 2
COEF_ROWS = 16
STATE_OUT_ROWS = 8
M_BLOCK_CHANNELS = 4
SSM_GROUPS_PER_STEP = 2
FF_BLOCK = 256
V7X_VMEM_LIMIT_BYTES = 56 * 1024 * 1024


def _layer_norm(x, g, b):
    mu = jnp.mean(x, axis=-1, keepdims=True)
    xc = x - mu
    var = jnp.mean(xc * xc, axis=-1, keepdims=True)
    return xc * lax.rsqrt(var + LN_EPS) * g + b


KV = 2 * PROJ_KV
VU = PROJ_KV + SSM_WIDTH


def _proj_kernel(xp_ref, xs_ref, g_ref, b_ref, w_ref, wqt_ref, wvut_ref, qt_ref, vt_ref, ut_ref, k_ref, v_ref,
                 *, prompt_tiles):
    outs = (g_ref, b_ref, w_ref, wqt_ref, wvut_ref, qt_ref, vt_ref, ut_ref, k_ref, v_ref)

    @pl.when(pl.program_id(0) < prompt_tiles)
    def _():
        _proj_tile(xp_ref, *outs)

    @pl.when(pl.program_id(0) >= prompt_tiles)
    def _():
        _proj_tile(xs_ref, *outs)


def _proj_tile(x_ref, g_ref, b_ref, w_ref, wqt_ref, wvut_ref, qt_ref, vt_ref, ut_ref, k_ref, v_ref):
    rows_per_part = ROW_GROUP // PROJ_PARTS
    nt = (((1,), (1,)), ((), ()))
    for part in range(PROJ_PARTS):
        tok = slice(part * rows_per_part * ROW_TOKENS, (part + 1) * rows_per_part * ROW_TOKENS)
        xb = _layer_norm(x_ref[tok, :], g_ref[...], b_ref[...]).astype(BF16)
        p = jnp.dot(xb, w_ref[...], preferred_element_type=F32)
        k_ref[tok, :] = p[:, :PROJ_KV]
        v_ref[tok, :] = p[:, PROJ_KV:]
        qt = lax.dot_general(wqt_ref[...], xb, nt, preferred_element_type=F32)
        vut = lax.dot_general(wvut_ref[...], xb, nt, preferred_element_type=F32)
        qt_ref[:, tok] = (qt * (HEAD_DIM ** -0.5)).astype(BF16)
        vt_ref[:, tok] = vut[:PROJ_KV].astype(BF16)
        for j in range(rows_per_part):
            ut_ref[pl.ds(part * rows_per_part + j, SSM_WIDTH, stride=ROW_GROUP), :] = (
                vut[PROJ_KV:, j * ROW_TOKENS:(j + 1) * ROW_TOKENS])


def _proj(xp2d, xs2d, ln_g, ln_b, w_in, w_in_t):
    assert D_IN_PROJ == 2 * VU and PROJ_Q % KV == 0
    tm = PROJ_TILE
    tp, ts = xp2d.shape[0] // tm, xs2d.shape[0] // tm
    n = (tp + ts) * tm
    const = lambda i: (0, 0)
    row = lambda i: (i, 0)
    col = lambda i: (0, i)
    return pl.pallas_call(
        functools.partial(_proj_kernel, prompt_tiles=tp),
        grid=(tp + ts,),
        in_specs=[
            pl.BlockSpec((tm, D_MODEL), lambda i: (jnp.minimum(i, tp - 1), 0)),
            pl.BlockSpec((tm, D_MODEL), lambda i: (jnp.maximum(i - tp, 0), 0)),
            pl.BlockSpec((1, D_MODEL), const),
            pl.BlockSpec((1, D_MODEL), const),
            pl.BlockSpec((D_MODEL, KV), lambda i: (0, PROJ_Q // KV)),
            pl.BlockSpec((PROJ_Q, D_MODEL), const),
            pl.BlockSpec((VU, D_MODEL), lambda i: (1, 0)),
        ],
        out_specs=[
            pl.BlockSpec((PROJ_Q, tm), col),
            pl.BlockSpec((PROJ_KV, tm), col),
            pl.BlockSpec((None, SSM_WIDTH * ROW_GROUP, ROW_TOKENS), lambda i: (i, 0, 0)),
            pl.BlockSpec((tm, PROJ_KV), row),
            pl.BlockSpec((tm, PROJ_KV), row),
        ],
        out_shape=[
            jax.ShapeDtypeStruct((PROJ_Q, n), BF16),
            jax.ShapeDtypeStruct((PROJ_KV, n), BF16),
            jax.ShapeDtypeStruct((n // tm, SSM_WIDTH * ROW_GROUP, ROW_TOKENS), F32),
            jax.ShapeDtypeStruct((n, PROJ_KV), F32),
            jax.ShapeDtypeStruct((n, PROJ_KV), F32),
        ],
        compiler_params=pltpu.CompilerParams(
            dimension_semantics=("arbitrary",), vmem_limit_bytes=V7X_VMEM_LIMIT_BYTES),
        name="proj",
    )(xp2d, xs2d, ln_g, ln_b, w_in, w_in_t, w_in_t)


PAIR = 2 * CHUNK
HEAD_LANES = Q_PER_KV * PAIR
ONES_ROWS = 16
ATTN_PAIRS = 16
BF16_TILE_ROWS = 16


def _scores(unit, qt_ref):
    kwin, _, lanes, _, _, h = unit
    base = h * Q_PER_KV * HEAD_DIM
    qrow = jnp.concatenate(
        [qt_ref[base + g * HEAD_DIM:base + (g + 1) * HEAD_DIM, lanes] for g in range(Q_PER_KV)],
        axis=1)
    zero = jnp.zeros_like(qrow)
    qstack = jnp.concatenate([qrow, zero] if h == 0 else [zero, qrow], axis=0)
    return jnp.dot(kwin, qstack, preferred_element_type=F32)


def _finish(unit, s, sink_ref, o_ref):
    _, vtwin, lanes, valid, masked_rows, h = unit
    nk = s.shape[0]
    base = h * Q_PER_KV * HEAD_DIM
    pieces, done = [], 0
    for start, stop in masked_rows:
        if start > done:
            pieces.append(s[done:start])
        pieces.append(jnp.where(valid[start:stop], s[start:stop], NEG_INF))
        done = stop
    if done < nk:
        pieces.append(s[done:])
    s = jnp.concatenate(pieces, axis=0)
    sink = sink_ref[h]
    m = jnp.maximum(jnp.max(s, axis=0, keepdims=True), sink)
    p = jnp.exp(s - m).astype(BF16)
    v_ones = jnp.concatenate(
        [vtwin[h * HEAD_DIM:(h + 1) * HEAD_DIM, :], jnp.ones((ONES_ROWS, nk), BF16)], axis=0)
    ov = jnp.dot(v_ones, p, preferred_element_type=F32)
    den = ov[HEAD_DIM:HEAD_DIM + 1, :] + jnp.exp(sink - m)
    o = ov[:HEAD_DIM, :] * (1.0 / den)
    for g in range(Q_PER_KV):
        o_ref[base + g * HEAD_DIM:base + (g + 1) * HEAD_DIM, lanes] = (
            o[:, g * PAIR:(g + 1) * PAIR].astype(BF16))


def _attend_units(units, qt_ref, sink_ref, o_ref):
    s_next = _scores(units[0], qt_ref)
    for i, unit in enumerate(units):
        s = s_next
        if i + 1 < len(units):
            s_next = _scores(units[i + 1], qt_ref)
        _finish(unit, s, sink_ref, o_ref)


def _attn_prompt_kernel(sink_ref, qt_ref, kp_ref, kc_ref, vtp_ref, vtc_ref, *refs, n_pairs):
    n_cast = (len(refs) - 1) // 2
    o_ref = refs[n_cast]
    for w_ref, wb_ref in zip(refs[:n_cast], refs[n_cast + 1:]):
        wb_ref[...] = w_ref[...].astype(BF16)
    tq = n_pairs * PAIR
    nk = WINDOW + PAIR
    kk = jnp.concatenate([kp_ref[...], kc_ref[...]], axis=0).astype(BF16)
    vt = jnp.concatenate([vtp_ref[...], vtc_ref[...]], axis=1)
    r = lax.broadcasted_iota(jnp.int32, (nk, HEAD_LANES), 0)
    first_chunk = (lax.broadcasted_iota(jnp.int32, (nk, HEAD_LANES), 1) & (PAIR - 1)) < CHUNK
    lo = jnp.where(first_chunk, 0, CHUNK)
    hi = jnp.where(first_chunk, BAND, nk)
    first_pos = pl.program_id(1) * tq - WINDOW
    units = []
    for pp in range(n_pairs):
        lo_pp = jnp.maximum(lo, -first_pos) if pp == 0 else lo
        valid = (r >= lo_pp) & (r < hi)
        masked_rows = ((0, nk),) if pp == 0 else ((0, CHUNK), (nk - CHUNK, nk))
        for h in range(N_KV_HEADS):
            units.append((kk[pp * PAIR:pp * PAIR + nk], vt[:, pp * PAIR:pp * PAIR + nk],
                          slice(pp * PAIR, (pp + 1) * PAIR), valid, masked_rows, h))
    _attend_units(units, qt_ref, sink_ref, o_ref)


def _attn_sample_kernel(sink_ref, qt_ref, ck_ref, kn_ref, cvt_ref, vtn_ref, o_ref, *, n_pairs):
    nk = 2 * WINDOW + PAIR
    r = lax.broadcasted_iota(jnp.int32, (nk, HEAD_LANES), 0)
    query_seq = (lax.broadcasted_iota(jnp.int32, (nk, HEAD_LANES), 1) & (PAIR - 1)) >> 6
    key_seq = jnp.where(r < 2 * WINDOW, r >> 7, (r - 2 * WINDOW) >> 6)
    valid = query_seq == key_seq
    units = []
    for pp in range(n_pairs):
        lanes = slice(pp * PAIR, (pp + 1) * PAIR)
        kwin = jnp.concatenate([ck_ref[2 * pp], ck_ref[2 * pp + 1], kn_ref[lanes, :]],
                               axis=0).astype(BF16)
        vtwin = jnp.concatenate([cvt_ref[2 * pp].astype(BF16), cvt_ref[2 * pp + 1].astype(BF16),
                                 vtn_ref[:, lanes]], axis=1)
        units.extend((kwin, vtwin, lanes, valid, ((0, nk),), h) for h in range(N_KV_HEADS))
    _attend_units(units, qt_ref, sink_ref, o_ref)


_SINK_SPEC = pl.BlockSpec((N_KV_HEADS, 1, HEAD_LANES), lambda *_: (0, 0, 0))


def _attention_prompt(sink_rows, qt, k, vt, bsz, seq, n_pairs, f32_weights):
    tq = n_pairs * PAIR
    nt = seq // tq
    steps = bsz * nt
    wpt = tq // WINDOW
    cur_c = lambda b, i: (0, b * nt + i)
    cur_r = lambda b, i: (b * nt + i, 0)
    prev = lambda b, i: jnp.maximum((b * nt + i) * wpt - 1, 0)
    assert all(w.shape[0] % (steps * BF16_TILE_ROWS) == 0 for w in f32_weights)
    cast_specs = [pl.BlockSpec((w.shape[0] // steps, w.shape[1]), cur_r) for w in f32_weights]
    return pl.pallas_call(
        functools.partial(_attn_prompt_kernel, n_pairs=n_pairs),
        grid=(bsz, nt),
        in_specs=[
            _SINK_SPEC,
            pl.BlockSpec((PROJ_Q, tq), cur_c),
            pl.BlockSpec((WINDOW, PROJ_KV), lambda b, i: (prev(b, i), 0)),
            pl.BlockSpec((tq, PROJ_KV), cur_r),
            pl.BlockSpec((PROJ_KV, WINDOW), lambda b, i: (0, prev(b, i))),
            pl.BlockSpec((PROJ_KV, tq), cur_c),
            *cast_specs,
        ],
        out_specs=[pl.BlockSpec((PROJ_Q, tq), cur_c), *cast_specs],
        out_shape=[jax.ShapeDtypeStruct((PROJ_Q, bsz * seq), BF16),
                   *[jax.ShapeDtypeStruct(w.shape, BF16) for w in f32_weights]],
        compiler_params=pltpu.CompilerParams(
            dimension_semantics=("arbitrary", "arbitrary"), vmem_limit_bytes=V7X_VMEM_LIMIT_BYTES),
        name="attn_prompt",
    )(sink_rows, qt, k, k, vt, vt, *f32_weights)


def _attention_sample(sink_rows, qt, cache_k, k, cache_vt, vt, first_token, n_pairs):
    n = cache_k.shape[0] * CHUNK
    tq = n_pairs * PAIR
    assert first_token % tq == 0
    off = first_token // tq
    return pl.pallas_call(
        functools.partial(_attn_sample_kernel, n_pairs=n_pairs),
        grid=(n // tq,),
        in_specs=[
            _SINK_SPEC,
            pl.BlockSpec((PROJ_Q, tq), lambda i: (0, i + off)),
            pl.BlockSpec((2 * n_pairs, WINDOW, PROJ_KV), lambda i: (i, 0, 0)),
            pl.BlockSpec((tq, PROJ_KV), lambda i: (i + off, 0)),
            pl.BlockSpec((2 * n_pairs, PROJ_KV, WINDOW), lambda i: (i, 0, 0)),
            pl.BlockSpec((PROJ_KV, tq), lambda i: (0, i + off)),
        ],
        out_specs=pl.BlockSpec((PROJ_Q, tq), lambda i: (0, i)),
        out_shape=jax.ShapeDtypeStruct((PROJ_Q, n), BF16),
        compiler_params=pltpu.CompilerParams(dimension_semantics=("arbitrary",)),
        name="attn_sample",
    )(sink_rows, qt, cache_k, k, cache_vt, vt)


def _windows_kernel(kp_ref, vp_ref, kn_ref, vn_ref, ckt_ref, cvt_ref, wkp_ref, wvp_ref, wks_ref, wvs_ref):
    first_half = lax.broadcasted_iota(jnp.int32, (PROJ_KV, WINDOW), 1) < CHUNK
    for new_p, new_s, cache_t, win_p, win_s in ((kp_ref, kn_ref, ckt_ref, wkp_ref, wks_ref),
                                                (vp_ref, vn_ref, cvt_ref, wvp_ref, wvs_ref)):
        win_p[0] = new_p[...].T
        for pair in range(cache_t.shape[0] // 2):
            new_t = new_s[pair * PAIR:(pair + 1) * PAIR, :].T
            swapped = pltpu.roll(new_t, CHUNK, 1)
            for j, new in enumerate((swapped, new_t)):
                old = pltpu.roll(cache_t[2 * pair + j], CHUNK, 1)
                win_s[2 * pair + j] = jnp.where(first_half, old, new)


def _windows(k, v, cache_kt, cache_vt, bp, seq):
    bs = cache_kt.shape[0]
    assert bs % (2 * bp) == 0 and seq % WINDOW == 0
    share = bs // bp
    assert (bp * seq) % (share * CHUNK) == 0
    first_share = bp * seq // (share * CHUNK)
    last_window = pl.BlockSpec((WINDOW, PROJ_KV), lambda b: ((b + 1) * (seq // WINDOW) - 1, 0))
    new_rows = pl.BlockSpec((share * CHUNK, PROJ_KV), lambda b: (first_share + b, 0))
    one = pl.BlockSpec((1, PROJ_KV, WINDOW), lambda b: (b, 0, 0))
    many = pl.BlockSpec((share, PROJ_KV, WINDOW), lambda b: (b, 0, 0))
    return pl.pallas_call(
        _windows_kernel,
        grid=(bp,),
        in_specs=[last_window, last_window, new_rows, new_rows, many, many],
        out_specs=[one, one, many, many],
        out_shape=[jax.ShapeDtypeStruct((bp, PROJ_KV, WINDOW), F32)] * 2
        + [jax.ShapeDtypeStruct((bs, PROJ_KV, WINDOW), F32)] * 2,
        compiler_params=pltpu.CompilerParams(dimension_semantics=("arbitrary",)),
        name="windows",
    )(k, v, k, v, cache_kt, cache_vt)


def _power_table(tau, nbits, a_re, a_im):
    rows = tau.shape[0]
    w_re = jnp.ones((rows, STATE_LANES), F32)
    w_im = jnp.zeros((rows, STATE_LANES), F32)
    p_re, p_im = a_re, a_im
    for k in range(nbits):
        bit = ((tau >> k) & 1) == 1
        f_re = jnp.where(bit, p_re, 1.0)
        f_im = jnp.where(bit, p_im, 0.0)
        w_re, w_im = w_re * f_re - w_im * f_im, w_re * f_im + w_im * f_re
        p_re, p_im = p_re * p_re - p_im * p_im, 2.0 * p_re * p_im
    return w_re, w_im


def _build_tables(par_ref, p_ref, qt_ref, coef_ref):
    lo = lax.broadcasted_iota(jnp.int32, (1, STATE_LANES), 1) < SSM_STATE
    mat = lambda i: par_ref[PAR_MATS + i * SSM_GROUP:PAR_MATS + (i + 1) * SSM_GROUP, :]
    lr, li = par_ref[0:1, :], par_ref[1:2, :]
    dt = jnp.exp(par_ref[2:3, :])
    mag = jnp.exp(lr * dt)
    a_re, a_im = mag * jnp.cos(li * dt), mag * jnp.sin(li * dt)
    nr, ni = a_re - 1.0, a_im
    den = lr * lr + li * li
    f_re, f_im = (nr * lr + ni * li) / den, (ni * lr - nr * li) / den
    b_re, b_im = mat(0), mat(1)
    bb_re = f_re * b_re - f_im * b_im
    bb_im = f_re * b_im + f_im * b_re
    c_re, c_im = mat(2), mat(3)

    tau = lax.broadcasted_iota(jnp.int32, (SSM_T, 1), 0)
    w_re, w_im = _power_table(tau, 6, a_re, a_im)
    w1_re, w1_im = w_re * a_re - w_im * a_im, w_re * a_im + w_im * a_re
    wr_re, wr_im = _power_table(SSM_T - 1 - tau, 6, a_re, a_im)

    def outer(c, w):
        return (c[:, None, :] * w[None, :, :]).reshape(SSM_CW, STATE_LANES)

    cw_mix = (outer(c_re, jnp.where(lo, w_re, w_im)) + outer(c_im, jnp.where(lo, -w_im, w_re)))
    bb_mix = jnp.where(lo, bb_re, -bb_im)
    strip = lax.dot_general(bb_mix, cw_mix, (((1,), (1,)), ((), ())),
                            precision=lax.Precision.HIGHEST,
                            preferred_element_type=F32)

    x_a, y_a = jnp.where(lo, bb_re, bb_im), jnp.where(lo, -bb_im, bb_re)
    p_ref[...] = (outer(x_a, wr_re) + outer(y_a, wr_im)).astype(BF16)

    qt_ref[...] = (outer(c_re, jnp.where(lo, w1_re, -w1_im))
                   + outer(c_im, jnp.where(lo, -w1_im, -w1_re))).astype(BF16)

    t_re, t_im = a_re, a_im
    for _ in range(6):
        t_re, t_im = t_re * t_re - t_im * t_im, 2.0 * t_re * t_im
    for k in range(COEF_ROWS // 2):
        coef_ref[2 * k:2 * k + 1, :] = t_re
        coef_ref[2 * k + 1:2 * k + 2, :] = jnp.where(lo, -t_im, t_im)
        t_re, t_im = t_re * t_re - t_im * t_im, 2.0 * t_re * t_im
    return strip


def _toeplitz_rows(strip, m_ref, channels):
    s_idx = lax.broadcasted_iota(jnp.int32, (SSM_T, SSM_CW), 0)
    t_idx = lax.broadcasted_iota(jnp.int32, (SSM_T, SSM_CW), 1) & (SSM_T - 1)
    causal = t_idx >= s_idx
    for c in channels:
        rows = jnp.broadcast_to(strip[c:c + 1, :], (SSM_T, SSM_CW))
        shifted = pltpu.roll(rows, 0, 1, stride=1, stride_axis=0)
        m_ref[c * SSM_T:(c + 1) * SSM_T, :] = jnp.where(causal, shifted, 0.0).astype(BF16)


PAR_MATS = 8
PAR_ROWS = PAR_MATS + 4 * SSM_GROUP


def _table_params(lam_re, lam_im, log_step, b_re, b_im, c_re, c_im):
    step = jnp.broadcast_to(log_step[:, None, None], (SSM_GROUPS, 1, SSM_STATE))
    filler = jnp.zeros((SSM_GROUPS, PAR_MATS - 3, SSM_STATE), F32)
    pack = jnp.concatenate([lam_re[:, None, :], lam_im[:, None, :], step, filler,
                            jnp.swapaxes(b_re, 1, 2), jnp.swapaxes(b_im, 1, 2), c_re, c_im], axis=1)
    return jnp.concatenate([pack, pack], axis=-1)


def _cmul(a1, a2, h, hs):
    return a1 * h + a2 * hs, a1 * hs - a2 * h


def _ssm_kernel(par_ref, u_ref, h0_ref, y_ref, hfin_ref, m_ref, p_ref, qt_ref, coef_ref,
                *, rows_per_seq, prompt_rows):
    for gi in range(SSM_GROUPS_PER_STEP):
        _ssm_group(par_ref.at[gi], u_ref, gi * SSM_GROUP, h0_ref.at[gi], y_ref, hfin_ref.at[gi],
                   m_ref.at[gi], p_ref.at[gi], qt_ref.at[gi], coef_ref.at[gi],
                   rows_per_seq=rows_per_seq, prompt_rows=prompt_rows)


def _ssm_group(par_ref, u_ref, c0, h0_ref, y_ref, hfin_ref, m_ref, p_ref, qt_ref, coef_ref,
               *, rows_per_seq, prompt_rows):
    strip = _build_tables(par_ref, p_ref, qt_ref, coef_ref)
    lo = lax.broadcasted_iota(jnp.int32, (1, ROW_TOKENS), 1) < SSM_T

    def chunk_rows(ref):
        even, odd = [], []
        rows = ref.shape[0] * ROW_GROUP
        for k in range(SSM_GROUP // 2):
            a = ref[:, c0 + 2 * k].reshape(rows, ROW_TOKENS)
            b = ref[:, c0 + 2 * k + 1].reshape(rows, ROW_TOKENS)
            even.append(jnp.where(lo, a, pltpu.roll(b, SSM_T, 1)))
            odd.append(jnp.where(lo, pltpu.roll(a, SSM_T, 1), b))
        return jnp.concatenate(even, axis=1), jnp.concatenate(odd, axis=1)

    def store_rows(y_even, y_odd, ref):
        for k in range(SSM_GROUP // 2):
            te = y_even[:, k * ROW_TOKENS:(k + 1) * ROW_TOKENS]
            to = y_odd[:, k * ROW_TOKENS:(k + 1) * ROW_TOKENS]
            tiles = (ref.shape[0], ROW_GROUP, ROW_TOKENS)
            ref[:, c0 + 2 * k] = jnp.where(lo, te, pltpu.roll(to, SSM_T, 1)).reshape(tiles)
            ref[:, c0 + 2 * k + 1] = jnp.where(lo, pltpu.roll(te, SSM_T, 1), to).reshape(tiles)

    even, odd = chunk_rows(u_ref)
    rp, rs = prompt_rows, even.shape[0] - prompt_rows
    u = jnp.concatenate([even[:rp], odd[:rp], even[rp:], odd[rp:]], axis=0).astype(BF16)
    s1 = jnp.dot(u, p_ref[...], preferred_element_type=F32)
    s2 = pltpu.roll(s1, SSM_STATE, 1)
    a1, a2 = coef_ref[0:1, :], coef_ref[1:2, :]

    e1, e2, o1, o2 = s1[:rp], s2[:rp], s1[rp:2 * rp], s2[rp:2 * rp]
    x1, x2 = _cmul(a1, a2, e1, e2)
    x1, x2 = x1 + o1, x2 + o2
    pos = lax.broadcasted_iota(jnp.int32, (rp, STATE_LANES), 0) & (rows_per_seq - 1)
    for k in range(rows_per_seq.bit_length() - 1):
        d = 1 << k
        b1, b2 = coef_ref[2 + 2 * k:3 + 2 * k, :], coef_ref[3 + 2 * k:4 + 2 * k, :]
        sh1 = jnp.where(pos >= d, pltpu.roll(x1, d, 0), 0.0)
        sh2 = jnp.where(pos >= d, pltpu.roll(x2, d, 0), 0.0)
        y1, y2 = _cmul(b1, b2, sh1, sh2)
        x1, x2 = x1 + y1, x2 + y2
    g1 = jnp.where(pos >= 1, pltpu.roll(x1, 1, 0), 0.0)
    g2 = jnp.where(pos >= 1, pltpu.roll(x2, 1, 0), 0.0)
    ho1 = _cmul(a1, a2, g1, g2)[0] + e1
    hfin_ref[0:STATE_OUT_ROWS, :] = jnp.zeros((STATE_OUT_ROWS, STATE_LANES), F32)
    for b in range(rp // rows_per_seq):
        last = (b + 1) * rows_per_seq - 1
        hfin_ref[b:b + 1, :] = x1[last:last + 1, :]

    h0e, h0o = h0_ref[0], h0_ref[1]
    swap = lambda h: pltpu.roll(h, SSM_STATE, 1)
    hfin_ref[STATE_OUT_ROWS:STATE_OUT_ROWS + rs, :] = a1 * h0e + a2 * swap(h0e) + s1[2 * rp:2 * rp + rs]
    hfin_ref[STATE_OUT_ROWS + rs:, :] = a1 * h0o + a2 * swap(h0o) + s1[2 * rp + rs:]

    hprev = jnp.concatenate([g1, ho1, h0e, h0o], axis=0).astype(BF16)
    y = lax.dot_general(hprev, qt_ref[...], (((1,), (1,)), ((), ())), preferred_element_type=F32)
    blocks = [range(c, c + M_BLOCK_CHANNELS) for c in range(0, SSM_GROUP, M_BLOCK_CHANNELS)]
    _toeplitz_rows(strip, m_ref, blocks[0])
    for i, channels in enumerate(blocks):
        rows = slice(channels[0] * SSM_T, (channels[-1] + 1) * SSM_T)
        y += jnp.dot(u[:, rows], m_ref[rows, :], preferred_element_type=F32)
        if i + 1 < len(blocks):
            _toeplitz_rows(strip, m_ref, blocks[i + 1])
    store_rows(jnp.concatenate([y[:rp], y[2 * rp:2 * rp + rs]], axis=0),
               jnp.concatenate([y[rp:2 * rp], y[2 * rp + rs:]], axis=0), y_ref)


def _ssm(table_params, u3, h0_pack, *, rows_per_seq, prompt_rows):
    groups = u3.shape[0]
    rs = groups * ROW_GROUP - prompt_rows
    u4 = u3.reshape(groups, SSM_WIDTH, ROW_GROUP, ROW_TOKENS)
    assert rows_per_seq & (rows_per_seq - 1) == 0 and 2 * rows_per_seq.bit_length() <= COEF_ROWS
    gps = SSM_GROUPS_PER_STEP
    g3 = lambda g: (g, 0, 0)
    u_spec = pl.BlockSpec((groups, gps * SSM_GROUP, ROW_GROUP, ROW_TOKENS), lambda g: (0, g, 0, 0))
    fin_rows = STATE_OUT_ROWS + 2 * rs
    return pl.pallas_call(
        functools.partial(_ssm_kernel, rows_per_seq=rows_per_seq, prompt_rows=prompt_rows),
        grid=(SSM_GROUPS // gps,),
        in_specs=[
            pl.BlockSpec((gps, PAR_ROWS, STATE_LANES), g3),
            u_spec,
            pl.BlockSpec((gps, 2, rs, STATE_LANES), lambda g: (g, 0, 0, 0)),
        ],
        out_specs=[
            u_spec,
            pl.BlockSpec((gps, fin_rows, STATE_LANES), g3),
        ],
        out_shape=[
            jax.ShapeDtypeStruct(u4.shape, F32),
            jax.ShapeDtypeStruct((SSM_GROUPS, fin_rows, STATE_LANES), F32),
        ],
        scratch_shapes=[
            pltpu.VMEM((gps, SSM_CW, SSM_CW), BF16),
            pltpu.VMEM((gps, SSM_CW, STATE_LANES), BF16),
            pltpu.VMEM((gps, SSM_CW, STATE_LANES), BF16),
            pltpu.VMEM((gps, COEF_ROWS, STATE_LANES), F32),
        ],
        compiler_params=pltpu.CompilerParams(
            dimension_semantics=("arbitrary",), vmem_limit_bytes=V7X_VMEM_LIMIT_BYTES),
        name="ssm",
    )(table_params, u4, h0_pack)


def _post_kernel(xp_ref, xs_ref, atp_ref, ats_ref, ut_ref, yt_ref, *refs, alpha, prompt_tiles):
    weights, (op_ref, os_ref) = refs[:-2], refs[-2:]

    @pl.when(pl.program_id(0) < prompt_tiles)
    def _():
        _post_tile(xp_ref, atp_ref, ut_ref, yt_ref, *weights, op_ref, alpha=alpha)

    @pl.when(pl.program_id(0) >= prompt_tiles)
    def _():
        _post_tile(xs_ref, ats_ref, ut_ref, yt_ref, *weights, os_ref, alpha=alpha)


def _post_tile(x_ref, at_ref, ut_ref, yt_ref, lng_ref, lnb_ref, d_ref, wglut_ref, bglu_ref,
               wout_ref, ln1g_ref, ln1b_ref, wgu_ref, wdown_ref, ln2g_ref, ln2b_ref, o_ref,
               *, alpha):
    xn = _layer_norm(x_ref[...], lng_ref[...], lnb_ref[...])
    n_rows = x_ref.shape[0] // ROW_TOKENS
    first_row = (pl.program_id(0) % (ROW_GROUP // n_rows)) * n_rows

    def feature_major(ref):
        return jnp.concatenate(
            [ref[pl.ds(first_row + j, SSM_WIDTH, stride=ROW_GROUP), :] for j in range(n_rows)], axis=1)

    ys = feature_major(yt_ref) + d_ref[...] * feature_major(ut_ref)
    gl = 0.5 * ys * (1.0 + lax.erf(ys * math.sqrt(0.5)))
    z = jnp.dot(wglut_ref[...], gl.astype(BF16), preferred_element_type=F32) + bglu_ref[...]
    s = gl * jax.nn.sigmoid(z)
    tn = (((0,), (0,)), ((), ()))
    mix = lax.dot_general(at_ref[...], wout_ref[:PROJ_Q, :], tn, preferred_element_type=F32)
    mix += lax.dot_general(s.astype(BF16), wout_ref[PROJ_Q:, :], tn, preferred_element_type=F32)
    h = _layer_norm(alpha * xn + mix, ln1g_ref[...], ln1b_ref[...])
    hb = h.astype(BF16)
    f = jnp.zeros_like(h)
    for start in range(0, D_FF, FF_BLOCK):
        stop = min(start + FF_BLOCK, D_FF)
        cols = slice(start, stop)
        up_cols = slice(D_FF + start, D_FF + stop)
        g = jnp.dot(hb, wgu_ref[:, cols], preferred_element_type=F32)
        up = jnp.dot(hb, wgu_ref[:, up_cols], preferred_element_type=F32)
        act = (g * jax.nn.sigmoid(g)) * up
        f += jnp.dot(act.astype(BF16), wdown_ref[cols, :], preferred_element_type=F32)
    o_ref[...] = _layer_norm(alpha * h + f, ln2g_ref[...], ln2b_ref[...])


def _post(xp2d, xs2d, atp, ats, ut3, yt3, ln_g, ln_b, d_col, w_glu_t, b_glu_col, w_out, ln1_g, ln1_b,
          w_gate_up, w_down, ln2_g, ln2_b, *, alpha, tm):
    n_p, n_s = xp2d.shape[0], xs2d.shape[0]
    tp, ts = n_p // tm, n_s // tm
    assert PROJ_TILE % tm == 0 and n_p % PROJ_TILE == 0
    group = lambda i: (i // (PROJ_TILE // tm), 0, 0)
    group_spec = pl.BlockSpec((None, SSM_WIDTH * ROW_GROUP, ROW_TOKENS), group)
    prompt = lambda i: jnp.minimum(i, tp - 1)
    sample = lambda i: jnp.maximum(i - tp, 0)
    const = lambda i: (0, 0)
    resident = lambda shape: pl.BlockSpec(shape, const, pipeline_mode=pl.Buffered(1))
    vec = lambda width: pl.BlockSpec((1, width), const)
    colvec = pl.BlockSpec((SSM_WIDTH, 1), const)
    return pl.pallas_call(
        functools.partial(_post_kernel, alpha=alpha, prompt_tiles=tp),
        grid=(tp + ts,),
        in_specs=[
            pl.BlockSpec((tm, D_MODEL), lambda i: (prompt(i), 0)),
            pl.BlockSpec((tm, D_MODEL), lambda i: (sample(i), 0)),
            pl.BlockSpec((PROJ_Q, tm), lambda i: (0, prompt(i))),
            pl.BlockSpec((PROJ_Q, tm), lambda i: (0, sample(i))),
            group_spec, group_spec,
            vec(D_MODEL), vec(D_MODEL), colvec,
            resident((SSM_WIDTH, SSM_WIDTH)), colvec,
            resident((D_MODEL, D_MODEL)), vec(D_MODEL), vec(D_MODEL),
            resident((D_MODEL, 2 * D_FF)), resident((D_FF, D_MODEL)),
            vec(D_MODEL), vec(D_MODEL),
        ],
        out_specs=[
            pl.BlockSpec((tm, D_MODEL), lambda i: (prompt(i), 0)),
            pl.BlockSpec((tm, D_MODEL), lambda i: (sample(i), 0)),
        ],
        out_shape=[
            jax.ShapeDtypeStruct((n_p, D_MODEL), F32),
            jax.ShapeDtypeStruct((n_s, D_MODEL), F32),
        ],
        compiler_params=pltpu.CompilerParams(
            dimension_semantics=("arbitrary",), vmem_limit_bytes=V7X_VMEM_LIMIT_BYTES),
        name="post",
    )(xp2d, xs2d, atp, ats, ut3, yt3, ln_g, ln_b, d_col, w_glu_t, b_glu_col, w_out, ln1_g, ln1_b,
      w_gate_up, w_down, ln2_g, ln2_b)


def kernel(x_prompt, x_sample, cache_win_k, cache_win_v, state_ssm_re, state_ssm_im,
           ln_in_g, ln_in_b, w_in, attn_sinks, ssm_lambda_re, ssm_lambda_im, ssm_log_step,
           ssm_b_re, ssm_b_im, ssm_c_re, ssm_c_im, ssm_d, w_glu, b_glu, w_out,
           ln1_g, ln1_b, w_gate_up, w_down, ln2_g, ln2_b):
    depth = w_in.shape[0]
    assert depth == 1, "single-layer step"
    bp, lp, _ = x_prompt.shape
    bs, ls, _ = x_sample.shape
    assert ls == SSM_T and bs % 2 == 0 and lp % 512 == 0 and bp <= STATE_OUT_ROWS
    win_rows = cache_win_k.shape[2]
    assert win_rows == WINDOW
    alpha = (2.0 * depth) ** 0.25
    l = 0
    row = lambda a: a.reshape(1, -1)
    column = lambda a: a.reshape(-1, 1)

    lng, lnb = row(ln_in_g), row(ln_in_b)
    xp2 = x_prompt.reshape(bp * lp, D_MODEL)
    xs2 = x_sample.reshape(bs * ls, D_MODEL)
    n_p = bp * lp
    qt, vt, u3, k, v = _proj(xp2, xs2, lng, lnb, w_in[l].astype(BF16), w_in[l].T.astype(BF16))

    sink_rows = jnp.repeat(attn_sinks[l].reshape(N_KV_HEADS, Q_PER_KV), PAIR, axis=1)[:, None, :]
    atp, w_gu_b, w_down_b, w_out_b = _attention_prompt(
        sink_rows, qt, k, vt, bp, lp, n_pairs=ATTN_PAIRS, f32_weights=(w_gate_up[l], w_down[l], w_out[l]))
    ck = cache_win_k[l].reshape(bs, win_rows, PROJ_KV)
    ckt = jnp.swapaxes(ck, 1, 2)
    cvt = jnp.swapaxes(cache_win_v[l].reshape(bs, win_rows, PROJ_KV), 1, 2)
    ats = _attention_sample(sink_rows, qt, ck, k, cvt, vt, first_token=n_p, n_pairs=ATTN_PAIRS)
    windows = _windows(k, v, ckt, cvt, bp, lp)

    table_params = _table_params(
        ssm_lambda_re[l], ssm_lambda_im[l], ssm_log_step[l],
        ssm_b_re[l], ssm_b_im[l], ssm_c_re[l], ssm_c_im[l])
    h0 = jnp.concatenate([state_ssm_re[l], state_ssm_im[l]], axis=-1)
    h0_pack = jnp.transpose(h0.reshape(bs // 2, 2, SSM_GROUPS, STATE_LANES), (2, 1, 0, 3))
    y4, hfin = _ssm(table_params, u3, h0_pack,
                    rows_per_seq=lp // ROW_TOKENS, prompt_rows=n_p // ROW_TOKENS)

    post_args = (lng, lnb, column(ssm_d[l]), w_glu[l].T.astype(BF16), column(b_glu[l]),
                 w_out_b, row(ln1_g[l]), row(ln1_b[l]), w_gu_b, w_down_b, row(ln2_g[l]), row(ln2_b[l]))
    out_p, out_s = _post(xp2, xs2, atp, ats, u3, y4.reshape(u3.shape), *post_args, alpha=alpha, tm=512)

    win_k_p, win_v_p, win_k_s, win_v_s = (
        jnp.swapaxes(w, 1, 2).reshape(1, -1, win_rows, N_KV_HEADS, HEAD_DIM) for w in windows)
    sp = jnp.swapaxes(hfin[:, :bp], 0, 1)
    ss = jnp.transpose(hfin[:, STATE_OUT_ROWS:].reshape(SSM_GROUPS, 2, bs // 2, STATE_LANES),
                       (2, 1, 0, 3)).reshape(bs, SSM_GROUPS, STATE_LANES)
    return (out_p.reshape(bp, lp, D_MODEL), out_s.reshape(bs, ls, D_MODEL),
            win_k_p, win_v_p, sp[None, ..., :SSM_STATE], sp[None, ..., SSM_STATE:],
            win_k_s, win_v_s, ss[None, ..., :SSM_STATE], ss[None, ..., SSM_STATE:])
```

```python
import functools
import math

import jax
import jax.numpy as jnp
from jax import lax
from jax.experimental import pallas as pl
from jax.experimental.pallas import tpu as pltpu

F32 = jnp.float32
BF16 = jnp.bfloat16

D_MODEL = 1024
HEAD_DIM = 64
N_HEADS = 8
N_KV_HEADS = 2
Q_PER_KV = N_HEADS // N_KV_HEADS
CHUNK = 64
WINDOW = 128
WIN_CHUNKS = WINDOW // CHUNK
BAND = (WIN_CHUNKS + 1) * CHUNK
PROJ_Q = N_HEADS * HEAD_DIM
PROJ_KV = N_KV_HEADS * HEAD_DIM
SSM_WIDTH = 512
SSM_GROUP = 16
SSM_GROUPS = SSM_WIDTH // SSM_GROUP
SSM_STATE = 64
STATE_LANES = 2 * SSM_STATE
D_FF = 2816
D_IN_PROJ = PROJ_Q + 2 * PROJ_KV + SSM_WIDTH
LN_EPS = 1e-5
NEG_INF = -1e30

SSM_T = CHUNK
SSM_CW = SSM_T * SSM_GROUP
ROW_TOKENS = 2 * SSM_T
ROW_GROUP = 8
PROJ_TILE = ROW_GROUP * ROW_TOKENS
PROJ_PARTS = 2
COEF_ROWS = 16
STATE_OUT_ROWS = 8
M_BLOCK_CHANNELS = 4
SSM_GROUPS_PER_STEP = 2
FF_BLOCK = 256
V7X_VMEM_LIMIT_BYTES = 56 * 1024 * 1024


def _layer_norm(x, g, b):
    mu = jnp.mean(x, axis=-1, keepdims=True)
    xc = x - mu
    var = jnp.mean(xc * xc, axis=-1, keepdims=True)
    return xc * lax.rsqrt(var + LN_EPS) * g + b


KV = 2 * PROJ_KV
VU = PROJ_KV + SSM_WIDTH


def _proj_kernel(xp_ref, xs_ref, g_ref, b_ref, w_ref, wqt_ref, wvut_ref, qt_ref, vt_ref, ut_ref, k_ref, v_ref,
                 *, prompt_tiles):
    outs = (g_ref, b_ref, w_ref, wqt_ref, wvut_ref, qt_ref, vt_ref, ut_ref, k_ref, v_ref)

    @pl.when(pl.program_id(0) < prompt_tiles)
    def _():
        _proj_tile(xp_ref, *outs)

    @pl.when(pl.program_id(0) >= prompt_tiles)
    def _():
        _proj_tile(xs_ref, *outs)


def _proj_tile(x_ref, g_ref, b_ref, w_ref, wqt_ref, wvut_ref, qt_ref, vt_ref, ut_ref, k_ref, v_ref):
    rows_per_part = ROW_GROUP // PROJ_PARTS
    nt = (((1,), (1,)), ((), ()))
    for part in range(PROJ_PARTS):
        tok = slice(part * rows_per_part * ROW_TOKENS, (part + 1) * rows_per_part * ROW_TOKENS)
        xb = _layer_norm(x_ref[tok, :], g_ref[...], b_ref[...]).astype(BF16)
        p = jnp.dot(xb, w_ref[...], preferred_element_type=F32)
        k_ref[tok, :] = p[:, :PROJ_KV]
        v_ref[tok, :] = p[:, PROJ_KV:]
        qt = lax.dot_general(wqt_ref[...], xb, nt, preferred_element_type=F32)
        vut = lax.dot_general(wvut_ref[...], xb, nt, preferred_element_type=F32)
        qt_ref[:, tok] = (qt * (HEAD_DIM ** -0.5)).astype(BF16)
        vt_ref[:, tok] = vut[:PROJ_KV].astype(BF16)
        for j in range(rows_per_part):
            ut_ref[pl.ds(part * rows_per_part + j, SSM_WIDTH, stride=ROW_GROUP), :] = (
                vut[PROJ_KV:, j * ROW_TOKENS:(j + 1) * ROW_TOKENS])


def _proj(xp2d, xs2d, ln_g, ln_b, w_kv, w_in_t):
    assert D_IN_PROJ == 2 * VU
    tm = PROJ_TILE
    tp, ts = xp2d.shape[0] // tm, xs2d.shape[0] // tm
    n = (tp + ts) * tm
    const = lambda i: (0, 0)
    row = lambda i: (i, 0)
    col = lambda i: (0, i)
    return pl.pallas_call(
        functools.partial(_proj_kernel, prompt_tiles=tp),
        grid=(tp + ts,),
        in_specs=[
            pl.BlockSpec((tm, D_MODEL), lambda i: (jnp.minimum(i, tp - 1), 0)),
            pl.BlockSpec((tm, D_MODEL), lambda i: (jnp.maximum(i - tp, 0), 0)),
            pl.BlockSpec((1, D_MODEL), const),
            pl.BlockSpec((1, D_MODEL), const),
            pl.BlockSpec((D_MODEL, KV), const),
            pl.BlockSpec((PROJ_Q, D_MODEL), const),
            pl.BlockSpec((VU, D_MODEL), lambda i: (1, 0)),
        ],
        out_specs=[
            pl.BlockSpec((PROJ_Q, tm), col),
            pl.BlockSpec((PROJ_KV, tm), col),
            pl.BlockSpec((None, SSM_WIDTH * ROW_GROUP, ROW_TOKENS), lambda i: (i, 0, 0)),
            pl.BlockSpec((tm, PROJ_KV), row),
            pl.BlockSpec((tm, PROJ_KV), row),
        ],
        out_shape=[
            jax.ShapeDtypeStruct((PROJ_Q, n), BF16),
            jax.ShapeDtypeStruct((PROJ_KV, n), BF16),
            jax.ShapeDtypeStruct((n // tm, SSM_WIDTH * ROW_GROUP, ROW_TOKENS), F32),
            jax.ShapeDtypeStruct((n, PROJ_KV), F32),
            jax.ShapeDtypeStruct((n, PROJ_KV), F32),
        ],
        compiler_params=pltpu.CompilerParams(
            dimension_semantics=("arbitrary",), vmem_limit_bytes=V7X_VMEM_LIMIT_BYTES),
        name="proj",
    )(xp2d, xs2d, ln_g, ln_b, w_kv, w_in_t, w_in_t)


PAIR = 2 * CHUNK
HEAD_LANES = Q_PER_KV * PAIR
ONES_ROWS = 16
ATTN_PAIRS = 16
BF16_TILE_ROWS = 16


def _scores(unit, qt_ref):
    kwin, _, lanes, _, _, h = unit
    base = h * Q_PER_KV * HEAD_DIM
    qrow = jnp.concatenate(
        [qt_ref[base + g * HEAD_DIM:base + (g + 1) * HEAD_DIM, lanes] for g in range(Q_PER_KV)],
        axis=1)
    zero = jnp.zeros_like(qrow)
    qstack = jnp.concatenate([qrow, zero] if h == 0 else [zero, qrow], axis=0)
    return jnp.dot(kwin, qstack, preferred_element_type=F32)


def _sink_rows(sinks_ref):
    tile = lax.broadcasted_iota(jnp.int32, (1, HEAD_LANES), 1) // PAIR
    rows = []
    for h in range(N_KV_HEADS):
        row = jnp.full((1, HEAD_LANES), sinks_ref[0, h * Q_PER_KV], F32)
        for g in range(1, Q_PER_KV):
            row = jnp.where(tile == g, sinks_ref[0, h * Q_PER_KV + g], row)
        rows.append(row)
    return rows


def _finish(unit, s, sinks, o_ref):
    _, vtwin, lanes, valid, masked_rows, h = unit
    nk = s.shape[0]
    base = h * Q_PER_KV * HEAD_DIM
    pieces, done = [], 0
    for start, stop in masked_rows:
        if start > done:
            pieces.append(s[done:start])
        pieces.append(jnp.where(valid[start:stop], s[start:stop], NEG_INF))
        done = stop
    if done < nk:
        pieces.append(s[done:])
    s = jnp.concatenate(pieces, axis=0)
    sink = sinks[h]
    m = jnp.maximum(jnp.max(s, axis=0, keepdims=True), sink)
    p = jnp.exp(s - m).astype(BF16)
    v_ones = jnp.concatenate(
        [vtwin[h * HEAD_DIM:(h + 1) * HEAD_DIM, :], jnp.ones((ONES_ROWS, nk), BF16)], axis=0)
    ov = jnp.dot(v_ones, p, preferred_element_type=F32)
    den = ov[HEAD_DIM:HEAD_DIM + 1, :] + jnp.exp(sink - m)
    o = ov[:HEAD_DIM, :] * (1.0 / den)
    for g in range(Q_PER_KV):
        o_ref[base + g * HEAD_DIM:base + (g + 1) * HEAD_DIM, lanes] = (
            o[:, g * PAIR:(g + 1) * PAIR].astype(BF16))


def _attend_units(units, qt_ref, sinks_ref, o_ref):
    sinks = _sink_rows(sinks_ref)
    s_next = _scores(units[0], qt_ref)
    for i, unit in enumerate(units):
        s = s_next
        if i + 1 < len(units):
            s_next = _scores(units[i + 1], qt_ref)
        _finish(unit, s, sinks, o_ref)


def _attn_prompt_kernel(sinks_ref, qt_ref, kp_ref, kc_ref, vtp_ref, vtc_ref, *refs, n_pairs):
    n_cast = (len(refs) - 1) // 2
    o_ref = refs[n_cast]
    for w_ref, wb_ref in zip(refs[:n_cast], refs[n_cast + 1:]):
        wb_ref[...] = w_ref[...].astype(BF16)
    tq = n_pairs * PAIR
    nk = WINDOW + PAIR
    kk = jnp.concatenate([kp_ref[...], kc_ref[...]], axis=0).astype(BF16)
    vt = jnp.concatenate([vtp_ref[...], vtc_ref[...]], axis=1)
    r = lax.broadcasted_iota(jnp.int32, (nk, HEAD_LANES), 0)
    first_chunk = (lax.broadcasted_iota(jnp.int32, (nk, HEAD_LANES), 1) & (PAIR - 1)) < CHUNK
    lo = jnp.where(first_chunk, 0, CHUNK)
    hi = jnp.where(first_chunk, BAND, nk)
    first_pos = pl.program_id(1) * tq - WINDOW
    units = []
    for pp in range(n_pairs):
        lo_pp = jnp.maximum(lo, -first_pos) if pp == 0 else lo
        valid = (r >= lo_pp) & (r < hi)
        masked_rows = ((0, nk),) if pp == 0 else ((0, CHUNK), (nk - CHUNK, nk))
        for h in range(N_KV_HEADS):
            units.append((kk[pp * PAIR:pp * PAIR + nk], vt[:, pp * PAIR:pp * PAIR + nk],
                          slice(pp * PAIR, (pp + 1) * PAIR), valid, masked_rows, h))
    _attend_units(units, qt_ref, sinks_ref, o_ref)


def _attn_sample_kernel(sinks_ref, qt_ref, ck_ref, kn_ref, cvt_ref, vtn_ref, o_ref, *, n_pairs):
    nk = 2 * WINDOW + PAIR
    r = lax.broadcasted_iota(jnp.int32, (nk, HEAD_LANES), 0)
    query_seq = (lax.broadcasted_iota(jnp.int32, (nk, HEAD_LANES), 1) & (PAIR - 1)) >> 6
    key_seq = jnp.where(r < 2 * WINDOW, r >> 7, (r - 2 * WINDOW) >> 6)
    valid = query_seq == key_seq
    units = []
    for pp in range(n_pairs):
        lanes = slice(pp * PAIR, (pp + 1) * PAIR)
        kwin = jnp.concatenate([ck_ref[2 * pp], ck_ref[2 * pp + 1], kn_ref[lanes, :]],
                               axis=0).astype(BF16)
        vtwin = jnp.concatenate([cvt_ref[2 * pp].astype(BF16), cvt_ref[2 * pp + 1].astype(BF16),
                                 vtn_ref[:, lanes]], axis=1)
        units.extend((kwin, vtwin, lanes, valid, ((0, nk),), h) for h in range(N_KV_HEADS))
    _attend_units(units, qt_ref, sinks_ref, o_ref)


_SINK_SPEC = pl.BlockSpec(memory_space=pltpu.SMEM)


def _attention_prompt(sinks, qt, k, vt, bsz, seq, n_pairs, f32_weights):
    tq = n_pairs * PAIR
    nt = seq // tq
    steps = bsz * nt
    wpt = tq // WINDOW
    cur_c = lambda b, i: (0, b * nt + i)
    cur_r = lambda b, i: (b * nt + i, 0)
    prev = lambda b, i: jnp.maximum((b * nt + i) * wpt - 1, 0)
    assert all(w.shape[0] % (steps * BF16_TILE_ROWS) == 0 for w in f32_weights)
    cast_specs = [pl.BlockSpec((w.shape[0] // steps, w.shape[1]), cur_r) for w in f32_weights]
    return pl.pallas_call(
        functools.partial(_attn_prompt_kernel, n_pairs=n_pairs),
        grid=(bsz, nt),
        in_specs=[
            _SINK_SPEC,
            pl.BlockSpec((PROJ_Q, tq), cur_c),
            pl.BlockSpec((WINDOW, PROJ_KV), lambda b, i: (prev(b, i), 0)),
            pl.BlockSpec((tq, PROJ_KV), cur_r),
            pl.BlockSpec((PROJ_KV, WINDOW), lambda b, i: (0, prev(b, i))),
            pl.BlockSpec((PROJ_KV, tq), cur_c),
            *cast_specs,
        ],
        out_specs=[pl.BlockSpec((PROJ_Q, tq), cur_c), *cast_specs],
        out_shape=[jax.ShapeDtypeStruct((PROJ_Q, bsz * seq), BF16),
                   *[jax.ShapeDtypeStruct(w.shape, BF16) for w in f32_weights]],
        compiler_params=pltpu.CompilerParams(
            dimension_semantics=("arbitrary", "arbitrary"), vmem_limit_bytes=V7X_VMEM_LIMIT_BYTES),
        name="attn_prompt",
    )(sinks, qt, k, k, vt, vt, *f32_weights)


def _attention_sample(sinks, qt, cache_k, k, cache_vt, vt, first_token, n_pairs):
    n = cache_k.shape[0] * CHUNK
    tq = n_pairs * PAIR
    assert first_token % tq == 0
    off = first_token // tq
    return pl.pallas_call(
        functools.partial(_attn_sample_kernel, n_pairs=n_pairs),
        grid=(n // tq,),
        in_specs=[
            _SINK_SPEC,
            pl.BlockSpec((PROJ_Q, tq), lambda i: (0, i + off)),
            pl.BlockSpec((2 * n_pairs, WINDOW, PROJ_KV), lambda i: (i, 0, 0)),
            pl.BlockSpec((tq, PROJ_KV), lambda i: (i + off, 0)),
            pl.BlockSpec((2 * n_pairs, PROJ_KV, WINDOW), lambda i: (i, 0, 0)),
            pl.BlockSpec((PROJ_KV, tq), lambda i: (0, i + off)),
        ],
        out_specs=pl.BlockSpec((PROJ_Q, tq), lambda i: (0, i)),
        out_shape=jax.ShapeDtypeStruct((PROJ_Q, n), BF16),
        compiler_params=pltpu.CompilerParams(dimension_semantics=("arbitrary",)),
        name="attn_sample",
    )(sinks, qt, cache_k, k, cache_vt, vt)


def _windows_kernel(kp_ref, vp_ref, kn_ref, vn_ref, ckt_ref, cvt_ref, wkp_ref, wvp_ref, wks_ref, wvs_ref):
    first_half = lax.broadcasted_iota(jnp.int32, (PROJ_KV, WINDOW), 1) < CHUNK
    for new_p, new_s, cache_t, win_p, win_s in ((kp_ref, kn_ref, ckt_ref, wkp_ref, wks_ref),
                                                (vp_ref, vn_ref, cvt_ref, wvp_ref, wvs_ref)):
        win_p[0] = new_p[...].T
        for pair in range(cache_t.shape[0] // 2):
            new_t = new_s[pair * PAIR:(pair + 1) * PAIR, :].T
            swapped = pltpu.roll(new_t, CHUNK, 1)
            for j, new in enumerate((swapped, new_t)):
                old = pltpu.roll(cache_t[2 * pair + j], CHUNK, 1)
                win_s[2 * pair + j] = jnp.where(first_half, old, new)


def _windows(k, v, cache_kt, cache_vt, bp, seq):
    bs = cache_kt.shape[0]
    assert bs % (2 * bp) == 0 and seq % WINDOW == 0
    share = bs // bp
    assert (bp * seq) % (share * CHUNK) == 0
    first_share = bp * seq // (share * CHUNK)
    last_window = pl.BlockSpec((WINDOW, PROJ_KV), lambda b: ((b + 1) * (seq // WINDOW) - 1, 0))
    new_rows = pl.BlockSpec((share * CHUNK, PROJ_KV), lambda b: (first_share + b, 0))
    one = pl.BlockSpec((1, PROJ_KV, WINDOW), lambda b: (b, 0, 0))
    many = pl.BlockSpec((share, PROJ_KV, WINDOW), lambda b: (b, 0, 0))
    return pl.pallas_call(
        _windows_kernel,
        grid=(bp,),
        in_specs=[last_window, last_window, new_rows, new_rows, many, many],
        out_specs=[one, one, many, many],
        out_shape=[jax.ShapeDtypeStruct((bp, PROJ_KV, WINDOW), F32)] * 2
        + [jax.ShapeDtypeStruct((bs, PROJ_KV, WINDOW), F32)] * 2,
        compiler_params=pltpu.CompilerParams(dimension_semantics=("arbitrary",)),
        name="windows",
    )(k, v, k, v, cache_kt, cache_vt)


def _power_table(tau, nbits, a_re, a_im):
    rows = tau.shape[0]
    w_re = jnp.ones((rows, STATE_LANES), F32)
    w_im = jnp.zeros((rows, STATE_LANES), F32)
    p_re, p_im = a_re, a_im
    for k in range(nbits):
        bit = ((tau >> k) & 1) == 1
        f_re = jnp.where(bit, p_re, 1.0)
        f_im = jnp.where(bit, p_im, 0.0)
        w_re, w_im = w_re * f_re - w_im * f_im, w_re * f_im + w_im * f_re
        p_re, p_im = p_re * p_re - p_im * p_im, 2.0 * p_re * p_im
    return w_re, w_im


def _build_tables(par_ref, p_ref, qt_ref, coef_ref):
    lo = lax.broadcasted_iota(jnp.int32, (1, STATE_LANES), 1) < SSM_STATE
    mat = lambda i: par_ref[PAR_MATS + i * SSM_GROUP:PAR_MATS + (i + 1) * SSM_GROUP, :]
    lr, li = par_ref[0:1, :], par_ref[1:2, :]
    dt = jnp.exp(par_ref[2:3, :])
    mag = jnp.exp(lr * dt)
    a_re, a_im = mag * jnp.cos(li * dt), mag * jnp.sin(li * dt)
    nr, ni = a_re - 1.0, a_im
    den = lr * lr + li * li
    f_re, f_im = (nr * lr + ni * li) / den, (ni * lr - nr * li) / den
    b_re, b_im = mat(0), mat(1)
    bb_re = f_re * b_re - f_im * b_im
    bb_im = f_re * b_im + f_im * b_re
    c_re, c_im = mat(2), mat(3)

    tau = lax.broadcasted_iota(jnp.int32, (SSM_T, 1), 0)
    w_re, w_im = _power_table(tau, 6, a_re, a_im)
    w1_re, w1_im = w_re * a_re - w_im * a_im, w_re * a_im + w_im * a_re
    wr_re, wr_im = _power_table(SSM_T - 1 - tau, 6, a_re, a_im)

    def outer(c, w):
        return (c[:, None, :] * w[None, :, :]).reshape(SSM_CW, STATE_LANES)

    cw_mix = (outer(c_re, jnp.where(lo, w_re, w_im)) + outer(c_im, jnp.where(lo, -w_im, w_re)))
    bb_mix = jnp.where(lo, bb_re, -bb_im)
    strip = lax.dot_general(bb_mix, cw_mix, (((1,), (1,)), ((), ())),
                            precision=lax.Precision.HIGHEST,
                            preferred_element_type=F32)

    x_a, y_a = jnp.where(lo, bb_re, bb_im), jnp.where(lo, -bb_im, bb_re)
    p_ref[...] = (outer(x_a, wr_re) + outer(y_a, wr_im)).astype(BF16)

    qt_ref[...] = (outer(c_re, jnp.where(lo, w1_re, -w1_im))
                   + outer(c_im, jnp.where(lo, -w1_im, -w1_re))).astype(BF16)

    t_re, t_im = a_re, a_im
    for _ in range(6):
        t_re, t_im = t_re * t_re - t_im * t_im, 2.0 * t_re * t_im
    for k in range(COEF_ROWS // 2):
        coef_ref[2 * k:2 * k + 1, :] = t_re
        coef_ref[2 * k + 1:2 * k + 2, :] = jnp.where(lo, -t_im, t_im)
        t_re, t_im = t_re * t_re - t_im * t_im, 2.0 * t_re * t_im
    return strip


def _toeplitz_rows(strip, m_ref, channels):
    s_idx = lax.broadcasted_iota(jnp.int32, (SSM_T, SSM_CW), 0)
    t_idx = lax.broadcasted_iota(jnp.int32, (SSM_T, SSM_CW), 1) & (SSM_T - 1)
    causal = t_idx >= s_idx
    for c in channels:
        rows = jnp.broadcast_to(strip[c:c + 1, :], (SSM_T, SSM_CW))
        shifted = pltpu.roll(rows, 0, 1, stride=1, stride_axis=0)
        m_ref[c * SSM_T:(c + 1) * SSM_T, :] = jnp.where(causal, shifted, 0.0).astype(BF16)


PAR_MATS = 8
PAR_ROWS = PAR_MATS + 4 * SSM_GROUP


def _table_params(lam_re, lam_im, log_step, b_re, b_im, c_re, c_im):
    step = jnp.broadcast_to(log_step[:, None, None], (SSM_GROUPS, 1, SSM_STATE))
    filler = jnp.zeros((SSM_GROUPS, PAR_MATS - 3, SSM_STATE), F32)
    pack = jnp.concatenate([lam_re[:, None, :], lam_im[:, None, :], step, filler,
                            jnp.swapaxes(b_re, 1, 2), jnp.swapaxes(b_im, 1, 2), c_re, c_im], axis=1)
    return jnp.concatenate([pack, pack], axis=-1)


def _cmul(a1, a2, h, hs):
    return a1 * h + a2 * hs, a1 * hs - a2 * h


def _ssm_kernel(par_ref, u_ref, h0_ref, y_ref, hfin_ref, m_ref, p_ref, qt_ref, coef_ref,
                *, rows_per_seq, prompt_rows):
    for gi in range(SSM_GROUPS_PER_STEP):
        _ssm_group(par_ref.at[gi], u_ref, gi * SSM_GROUP, h0_ref.at[gi], y_ref, hfin_ref.at[gi],
                   m_ref.at[gi], p_ref.at[gi], qt_ref.at[gi], coef_ref.at[gi],
                   rows_per_seq=rows_per_seq, prompt_rows=prompt_rows)


def _ssm_group(par_ref, u_ref, c0, h0_ref, y_ref, hfin_ref, m_ref, p_ref, qt_ref, coef_ref,
               *, rows_per_seq, prompt_rows):
    strip = _build_tables(par_ref, p_ref, qt_ref, coef_ref)
    lo = lax.broadcasted_iota(jnp.int32, (1, ROW_TOKENS), 1) < SSM_T

    def chunk_rows(ref):
        even, odd = [], []
        rows = ref.shape[0] * ROW_GROUP
        for k in range(SSM_GROUP // 2):
            a = ref[:, c0 + 2 * k].reshape(rows, ROW_TOKENS)
            b = ref[:, c0 + 2 * k + 1].reshape(rows, ROW_TOKENS)
            even.append(jnp.where(lo, a, pltpu.roll(b, SSM_T, 1)))
            odd.append(jnp.where(lo, pltpu.roll(a, SSM_T, 1), b))
        return jnp.concatenate(even, axis=1), jnp.concatenate(odd, axis=1)

    def store_rows(y_even, y_odd, ref):
        for k in range(SSM_GROUP // 2):
            te = y_even[:, k * ROW_TOKENS:(k + 1) * ROW_TOKENS]
            to = y_odd[:, k * ROW_TOKENS:(k + 1) * ROW_TOKENS]
            tiles = (ref.shape[0], ROW_GROUP, ROW_TOKENS)
            ref[:, c0 + 2 * k] = jnp.where(lo, te, pltpu.roll(to, SSM_T, 1)).reshape(tiles)
            ref[:, c0 + 2 * k + 1] = jnp.where(lo, pltpu.roll(te, SSM_T, 1), to).reshape(tiles)

    even, odd = chunk_rows(u_ref)
    rp, rs = prompt_rows, even.shape[0] - prompt_rows
    u = jnp.concatenate([even[:rp], odd[:rp], even[rp:], odd[rp:]], axis=0).astype(BF16)
    s1 = jnp.dot(u, p_ref[...], preferred_element_type=F32)
    s2 = pltpu.roll(s1, SSM_STATE, 1)
    a1, a2 = coef_ref[0:1, :], coef_ref[1:2, :]

    e1, e2, o1, o2 = s1[:rp], s2[:rp], s1[rp:2 * rp], s2[rp:2 * rp]
    x1, x2 = _cmul(a1, a2, e1, e2)
    x1, x2 = x1 + o1, x2 + o2
    pos = lax.broadcasted_iota(jnp.int32, (rp, STATE_LANES), 0) & (rows_per_seq - 1)
    for k in range(rows_per_seq.bit_length() - 1):
        d = 1 << k
        b1, b2 = coef_ref[2 + 2 * k:3 + 2 * k, :], coef_ref[3 + 2 * k:4 + 2 * k, :]
        sh1 = jnp.where(pos >= d, pltpu.roll(x1, d, 0), 0.0)
        sh2 = jnp.where(pos >= d, pltpu.roll(x2, d, 0), 0.0)
        y1, y2 = _cmul(b1, b2, sh1, sh2)
        x1, x2 = x1 + y1, x2 + y2
    g1 = jnp.where(pos >= 1, pltpu.roll(x1, 1, 0), 0.0)
    g2 = jnp.where(pos >= 1, pltpu.roll(x2, 1, 0), 0.0)
    ho1 = _cmul(a1, a2, g1, g2)[0] + e1
    hfin_ref[0:STATE_OUT_ROWS, :] = jnp.zeros((STATE_OUT_ROWS, STATE_LANES), F32)
    for b in range(rp // rows_per_seq):
        last = (b + 1) * rows_per_seq - 1
        hfin_ref[b:b + 1, :] = x1[last:last + 1, :]

    h0e, h0o = h0_ref[0], h0_ref[1]
    swap = lambda h: pltpu.roll(h, SSM_STATE, 1)
    hfin_ref[STATE_OUT_ROWS:STATE_OUT_ROWS + rs, :] = a1 * h0e + a2 * swap(h0e) + s1[2 * rp:2 * rp + rs]
    hfin_ref[STATE_OUT_ROWS + rs:, :] = a1 * h0o + a2 * swap(h0o) + s1[2 * rp + rs:]

    hprev = jnp.concatenate([g1, ho1, h0e, h0o], axis=0).astype(BF16)
    y = lax.dot_general(hprev, qt_ref[...], (((1,), (1,)), ((), ())), preferred_element_type=F32)
    blocks = [range(c, c + M_BLOCK_CHANNELS) for c in range(0, SSM_GROUP, M_BLOCK_CHANNELS)]
    _toeplitz_rows(strip, m_ref, blocks[0])
    for i, channels in enumerate(blocks):
        rows = slice(channels[0] * SSM_T, (channels[-1] + 1) * SSM_T)
        y += jnp.dot(u[:, rows], m_ref[rows, :], preferred_element_type=F32)
        if i + 1 < len(blocks):
            _toeplitz_rows(strip, m_ref, blocks[i + 1])
    store_rows(jnp.concatenate([y[:rp], y[2 * rp:2 * rp + rs]], axis=0),
               jnp.concatenate([y[rp:2 * rp], y[2 * rp + rs:]], axis=0), y_ref)


def _ssm(table_params, u3, h0_pack, *, rows_per_seq, prompt_rows):
    groups = u3.shape[0]
    rs = groups * ROW_GROUP - prompt_rows
    u4 = u3.reshape(groups, SSM_WIDTH, ROW_GROUP, ROW_TOKENS)
    assert rows_per_seq & (rows_per_seq - 1) == 0 and 2 * rows_per_seq.bit_length() <= COEF_ROWS
    gps = SSM_GROUPS_PER_STEP
    g3 = lambda g: (g, 0, 0)
    u_spec = pl.BlockSpec((groups, gps * SSM_GROUP, ROW_GROUP, ROW_TOKENS), lambda g: (0, g, 0, 0))
    fin_rows = STATE_OUT_ROWS + 2 * rs
    return pl.pallas_call(
        functools.partial(_ssm_kernel, rows_per_seq=rows_per_seq, prompt_rows=prompt_rows),
        grid=(SSM_GROUPS // gps,),
        in_specs=[
            pl.BlockSpec((gps, PAR_ROWS, STATE_LANES), g3),
            u_spec,
            pl.BlockSpec((gps, 2, rs, STATE_LANES), lambda g: (g, 0, 0, 0)),
        ],
        out_specs=[
            u_spec,
            pl.BlockSpec((gps, fin_rows, STATE_LANES), g3),
        ],
        out_shape=[
            jax.ShapeDtypeStruct(u4.shape, F32),
            jax.ShapeDtypeStruct((SSM_GROUPS, fin_rows, STATE_LANES), F32),
        ],
        scratch_shapes=[
            pltpu.VMEM((gps, SSM_CW, SSM_CW), BF16),
            pltpu.VMEM((gps, SSM_CW, STATE_LANES), BF16),
            pltpu.VMEM((gps, SSM_CW, STATE_LANES), BF16),
            pltpu.VMEM((gps, COEF_ROWS, STATE_LANES), F32),
        ],
        compiler_params=pltpu.CompilerParams(
            dimension_semantics=("arbitrary",), vmem_limit_bytes=V7X_VMEM_LIMIT_BYTES),
        name="ssm",
    )(table_params, u4, h0_pack)


def _post_kernel(xp_ref, xs_ref, atp_ref, ats_ref, ut_ref, yt_ref, *refs, alpha, prompt_tiles):
    weights, (op_ref, os_ref) = refs[:-2], refs[-2:]

    @pl.when(pl.program_id(0) < prompt_tiles)
    def _():
        _post_tile(xp_ref, atp_ref, ut_ref, yt_ref, *weights, op_ref, alpha=alpha)

    @pl.when(pl.program_id(0) >= prompt_tiles)
    def _():
        _post_tile(xs_ref, ats_ref, ut_ref, yt_ref, *weights, os_ref, alpha=alpha)


def _post_tile(x_ref, at_ref, ut_ref, yt_ref, lng_ref, lnb_ref, cols_ref, wglut_ref,
               wout_ref, ln1g_ref, ln1b_ref, wgu_ref, wdown_ref, ln2g_ref, ln2b_ref, o_ref,
               *, alpha):
    xn = _layer_norm(x_ref[...], lng_ref[...], lnb_ref[...])
    n_rows = x_ref.shape[0] // ROW_TOKENS
    first_row = (pl.program_id(0) % (ROW_GROUP // n_rows)) * n_rows

    def feature_major(ref):
        return jnp.concatenate(
            [ref[pl.ds(first_row + j, SSM_WIDTH, stride=ROW_GROUP), :] for j in range(n_rows)], axis=1)

    ys = feature_major(yt_ref) + cols_ref[:, 0:1] * feature_major(ut_ref)
    gl = 0.5 * ys * (1.0 + lax.erf(ys * math.sqrt(0.5)))
    z = jnp.dot(wglut_ref[...], gl.astype(BF16), preferred_element_type=F32) + cols_ref[:, 1:2]
    s = gl * jax.nn.sigmoid(z)
    tn = (((0,), (0,)), ((), ()))
    mix = lax.dot_general(at_ref[...], wout_ref[:PROJ_Q, :], tn, preferred_element_type=F32)
    mix += lax.dot_general(s.astype(BF16), wout_ref[PROJ_Q:, :], tn, preferred_element_type=F32)
    h = _layer_norm(alpha * xn + mix, ln1g_ref[...], ln1b_ref[...])
    hb = h.astype(BF16)
    f = jnp.zeros_like(h)
    for start in range(0, D_FF, FF_BLOCK):
        stop = min(start + FF_BLOCK, D_FF)
        cols = slice(start, stop)
        up_cols = slice(D_FF + start, D_FF + stop)
        g = jnp.dot(hb, wgu_ref[:, cols], preferred_element_type=F32)
        up = jnp.dot(hb, wgu_ref[:, up_cols], preferred_element_type=F32)
        act = (g * jax.nn.sigmoid(g)) * up
        f += jnp.dot(act.astype(BF16), wdown_ref[cols, :], preferred_element_type=F32)
    o_ref[...] = _layer_norm(alpha * h + f, ln2g_ref[...], ln2b_ref[...])


def _post(xp2d, xs2d, atp, ats, ut3, yt3, ln_g, ln_b, d_bglu_cols, w_glu_t, w_out, ln1_g, ln1_b,
          w_gate_up, w_down, ln2_g, ln2_b, *, alpha, tm):
    n_p, n_s = xp2d.shape[0], xs2d.shape[0]
    tp, ts = n_p // tm, n_s // tm
    assert PROJ_TILE % tm == 0 and n_p % PROJ_TILE == 0
    group = lambda i: (i // (PROJ_TILE // tm), 0, 0)
    group_spec = pl.BlockSpec((None, SSM_WIDTH * ROW_GROUP, ROW_TOKENS), group)
    prompt = lambda i: jnp.minimum(i, tp - 1)
    sample = lambda i: jnp.maximum(i - tp, 0)
    const = lambda i: (0, 0)
    resident = lambda shape: pl.BlockSpec(shape, const, pipeline_mode=pl.Buffered(1))
    vec = lambda width: pl.BlockSpec((1, width), const)
    columns = pl.BlockSpec((SSM_WIDTH, 2), const)
    return pl.pallas_call(
        functools.partial(_post_kernel, alpha=alpha, prompt_tiles=tp),
        grid=(tp + ts,),
        in_specs=[
            pl.BlockSpec((tm, D_MODEL), lambda i: (prompt(i), 0)),
            pl.BlockSpec((tm, D_MODEL), lambda i: (sample(i), 0)),
            pl.BlockSpec((PROJ_Q, tm), lambda i: (0, prompt(i))),
            pl.BlockSpec((PROJ_Q, tm), lambda i: (0, sample(i))),
            group_spec, group_spec,
            vec(D_MODEL), vec(D_MODEL), columns,
            resident((SSM_WIDTH, SSM_WIDTH)),
            resident((D_MODEL, D_MODEL)), vec(D_MODEL), vec(D_MODEL),
            resident((D_MODEL, 2 * D_FF)), resident((D_FF, D_MODEL)),
            vec(D_MODEL), vec(D_MODEL),
        ],
        out_specs=[
            pl.BlockSpec((tm, D_MODEL), lambda i: (prompt(i), 0)),
            pl.BlockSpec((tm, D_MODEL), lambda i: (sample(i), 0)),
        ],
        out_shape=[
            jax.ShapeDtypeStruct((n_p, D_MODEL), F32),
            jax.ShapeDtypeStruct((n_s, D_MODEL), F32),
        ],
        compiler_params=pltpu.CompilerParams(
            dimension_semantics=("arbitrary",), vmem_limit_bytes=V7X_VMEM_LIMIT_BYTES),
        name="post",
    )(xp2d, xs2d, atp, ats, ut3, yt3, ln_g, ln_b, d_bglu_cols, w_glu_t, w_out, ln1_g, ln1_b,
      w_gate_up, w_down, ln2_g, ln2_b)


def kernel(x_prompt, x_sample, cache_win_k, cache_win_v, state_ssm_re, state_ssm_im,
           ln_in_g, ln_in_b, w_in, attn_sinks, ssm_lambda_re, ssm_lambda_im, ssm_log_step,
           ssm_b_re, ssm_b_im, ssm_c_re, ssm_c_im, ssm_d, w_glu, b_glu, w_out,
           ln1_g, ln1_b, w_gate_up, w_down, ln2_g, ln2_b):
    depth = w_in.shape[0]
    assert depth == 1, "single-layer step"
    bp, lp, _ = x_prompt.shape
    bs, ls, _ = x_sample.shape
    assert ls == SSM_T and bs % 2 == 0 and lp % 512 == 0 and bp <= STATE_OUT_ROWS
    win_rows = cache_win_k.shape[2]
    assert win_rows == WINDOW
    alpha = (2.0 * depth) ** 0.25
    l = 0
    row = lambda a: a.reshape(1, -1)

    lng, lnb = row(ln_in_g), row(ln_in_b)
    xp2 = x_prompt.reshape(bp * lp, D_MODEL)
    xs2 = x_sample.reshape(bs * ls, D_MODEL)
    n_p = bp * lp
    qt, vt, u3, k, v = _proj(xp2, xs2, lng, lnb, w_in[l][:, PROJ_Q:PROJ_Q + KV].astype(BF16),
                             w_in[l].T.astype(BF16))

    atp, w_gu_b, w_down_b, w_out_b = _attention_prompt(
        attn_sinks, qt, k, vt, bp, lp, n_pairs=ATTN_PAIRS, f32_weights=(w_gate_up[l], w_down[l], w_out[l]))
    ck = cache_win_k[l].reshape(bs, win_rows, PROJ_KV)
    ckt = jnp.swapaxes(ck, 1, 2)
    cvt = jnp.swapaxes(cache_win_v[l].reshape(bs, win_rows, PROJ_KV), 1, 2)
    ats = _attention_sample(attn_sinks, qt, ck, k, cvt, vt, first_token=n_p, n_pairs=ATTN_PAIRS)
    windows = _windows(k, v, ckt, cvt, bp, lp)

    table_params = _table_params(
        ssm_lambda_re[l], ssm_lambda_im[l], ssm_log_step[l],
        ssm_b_re[l], ssm_b_im[l], ssm_c_re[l], ssm_c_im[l])
    h0 = jnp.concatenate([state_ssm_re[l], state_ssm_im[l]], axis=-1)
    h0_pack = jnp.transpose(h0.reshape(bs // 2, 2, SSM_GROUPS, STATE_LANES), (2, 1, 0, 3))
    y4, hfin = _ssm(table_params, u3, h0_pack,
                    rows_per_seq=lp // ROW_TOKENS, prompt_rows=n_p // ROW_TOKENS)

    post_args = (lng, lnb, jnp.stack([ssm_d[l], b_glu[l]], axis=1), w_glu[l].T.astype(BF16),
                 w_out_b, row(ln1_g[l]), row(ln1_b[l]), w_gu_b, w_down_b, row(ln2_g[l]), row(ln2_b[l]))
    out_p, out_s = _post(xp2, xs2, atp, ats, u3, y4.reshape(u3.shape), *post_args, alpha=alpha, tm=512)

    win_k_p, win_v_p, win_k_s, win_v_s = (
        jnp.swapaxes(w, 1, 2).reshape(1, -1, win_rows, N_KV_HEADS, HEAD_DIM) for w in windows)
    sp = jnp.swapaxes(hfin[:, :bp], 0, 1)
    ss = jnp.transpose(hfin[:, STATE_OUT_ROWS:].reshape(SSM_GROUPS, 2, bs // 2, STATE_LANES),
                       (2, 1, 0, 3)).reshape(bs, SSM_GROUPS, STATE_LANES)
    return (out_p.reshape(bp, lp, D_MODEL), out_s.reshape(bs, ls, D_MODEL),
            win_k_p, win_v_p, sp[None, ..., :SSM_STATE], sp[None, ..., SSM_STATE:],
            win_k_s, win_v_s, ss[None, ..., :SSM_STATE], ss[None, ..., SSM_STATE:])
```

```python
import functools
import math

import jax
import jax.numpy as jnp
from jax import lax
from jax.experimental import pallas as pl
from jax.experimental.pallas import tpu as pltpu

F32 = jnp.float32
BF16 = jnp.bfloat16

D_MODEL = 1024
HEAD_DIM = 64
N_HEADS = 8
N_KV_HEADS = 2
Q_PER_KV = N_HEADS // N_KV_HEADS
CHUNK = 64
WINDOW = 128
WIN_CHUNKS = WINDOW // CHUNK
BAND = (WIN_CHUNKS + 1) * CHUNK
PROJ_Q = N_HEADS * HEAD_DIM
PROJ_KV = N_KV_HEADS * HEAD_DIM
SSM_WIDTH = 512
SSM_GROUP = 16
SSM_GROUPS = SSM_WIDTH // SSM_GROUP
SSM_STATE = 64
STATE_LANES = 2 * SSM_STATE
D_FF = 2816
D_IN_PROJ = PROJ_Q + 2 * PROJ_KV + SSM_WIDTH
LN_EPS = 1e-5
NEG_INF = -1e30

SSM_T = CHUNK
SSM_CW = SSM_T * SSM_GROUP
ROW_TOKENS = 2 * SSM_T
ROW_GROUP = 8
PROJ_TILE = ROW_GROUP * ROW_TOKENS
PROJ_PARTS = 2
COEF_ROWS = 16
STATE_OUT_ROWS = 8
M_BLOCK_CHANNELS = 4
SSM_GROUPS_PER_STEP = 2
FF_BLOCK = 256
V7X_VMEM_LIMIT_BYTES = 56 * 1024 * 1024


def _layer_norm(x, g, b):
    mu = jnp.mean(x, axis=-1, keepdims=True)
    xc = x - mu
    var = jnp.mean(xc * xc, axis=-1, keepdims=True)
    return xc * lax.rsqrt(var + LN_EPS) * g + b


KV = 2 * PROJ_KV
VU = PROJ_KV + SSM_WIDTH


def _proj_kernel(xp_ref, xs_ref, g_ref, b_ref, w_ref, wqt_ref, wvut_ref, qt_ref, vt_ref, ut_ref, k_ref, v_ref,
                 *, prompt_tiles):
    outs = (g_ref, b_ref, w_ref, wqt_ref, wvut_ref, qt_ref, vt_ref, ut_ref, k_ref, v_ref)

    @pl.when(pl.program_id(0) < prompt_tiles)
    def _():
        _proj_tile(xp_ref, *outs)

    @pl.when(pl.program_id(0) >= prompt_tiles)
    def _():
        _proj_tile(xs_ref, *outs)


def _proj_tile(x_ref, g_ref, b_ref, w_ref, wqt_ref, wvut_ref, qt_ref, vt_ref, ut_ref, k_ref, v_ref):
    rows_per_part = ROW_GROUP // PROJ_PARTS
    nt = (((1,), (1,)), ((), ()))
    for part in range(PROJ_PARTS):
        tok = slice(part * rows_per_part * ROW_TOKENS, (part + 1) * rows_per_part * ROW_TOKENS)
        xb = _layer_norm(x_ref[tok, :], g_ref[...], b_ref[...]).astype(BF16)
        p = jnp.dot(xb, w_ref[...], preferred_element_type=F32)
        k_ref[tok, :] = p[:, :PROJ_KV]
        v_ref[tok, :] = p[:, PROJ_KV:]
        qt = lax.dot_general(wqt_ref[...], xb, nt, preferred_element_type=F32)
        vut = lax.dot_general(wvut_ref[...], xb, nt, preferred_element_type=F32)
        qt_ref[:, tok] = (qt * (HEAD_DIM ** -0.5)).astype(BF16)
        vt_ref[:, tok] = vut[:PROJ_KV].astype(BF16)
        for j in range(rows_per_part):
            ut_ref[pl.ds(part * rows_per_part + j, SSM_WIDTH, stride=ROW_GROUP), :] = (
                vut[PROJ_KV:, j * ROW_TOKENS:(j + 1) * ROW_TOKENS])


def _proj(xp2d, xs2d, ln_g, ln_b, w_kv, w_in_t):
    assert D_IN_PROJ == 2 * VU
    tm = PROJ_TILE
    tp, ts = xp2d.shape[0] // tm, xs2d.shape[0] // tm
    n = (tp + ts) * tm
    const = lambda i: (0, 0)
    row = lambda i: (i, 0)
    col = lambda i: (0, i)
    return pl.pallas_call(
        functools.partial(_proj_kernel, prompt_tiles=tp),
        grid=(tp + ts,),
        in_specs=[
            pl.BlockSpec((tm, D_MODEL), lambda i: (jnp.minimum(i, tp - 1), 0)),
            pl.BlockSpec((tm, D_MODEL), lambda i: (jnp.maximum(i - tp, 0), 0)),
            pl.BlockSpec((1, D_MODEL), const),
            pl.BlockSpec((1, D_MODEL), const),
            pl.BlockSpec((D_MODEL, KV), const),
            pl.BlockSpec((PROJ_Q, D_MODEL), const),
            pl.BlockSpec((VU, D_MODEL), lambda i: (1, 0)),
        ],
        out_specs=[
            pl.BlockSpec((PROJ_Q, tm), col),
            pl.BlockSpec((PROJ_KV, tm), col),
            pl.BlockSpec((None, SSM_WIDTH * ROW_GROUP, ROW_TOKENS), lambda i: (i, 0, 0)),
            pl.BlockSpec((tm, PROJ_KV), row),
            pl.BlockSpec((tm, PROJ_KV), row),
        ],
        out_shape=[
            jax.ShapeDtypeStruct((PROJ_Q, n), BF16),
            jax.ShapeDtypeStruct((PROJ_KV, n), BF16),
            jax.ShapeDtypeStruct((n // tm, SSM_WIDTH * ROW_GROUP, ROW_TOKENS), F32),
            jax.ShapeDtypeStruct((n, PROJ_KV), F32),
            jax.ShapeDtypeStruct((n, PROJ_KV), F32),
        ],
        compiler_params=pltpu.CompilerParams(
            dimension_semantics=("arbitrary",), vmem_limit_bytes=V7X_VMEM_LIMIT_BYTES),
        name="proj",
    )(xp2d, xs2d, ln_g, ln_b, w_kv, w_in_t, w_in_t)


PAIR = 2 * CHUNK
HEAD_LANES = Q_PER_KV * PAIR
ONES_ROWS = 16
ATTN_PAIRS = 16
BF16_TILE_ROWS = 16


def _scores(unit, qt_ref):
    kwin, _, lanes, _, _, h = unit
    base = h * Q_PER_KV * HEAD_DIM
    qrow = jnp.concatenate(
        [qt_ref[base + g * HEAD_DIM:base + (g + 1) * HEAD_DIM, lanes] for g in range(Q_PER_KV)],
        axis=1)
    zero = jnp.zeros_like(qrow)
    qstack = jnp.concatenate([qrow, zero] if h == 0 else [zero, qrow], axis=0)
    return jnp.dot(kwin, qstack, preferred_element_type=F32)


def _sink_rows(sinks_ref):
    tile = lax.broadcasted_iota(jnp.int32, (1, HEAD_LANES), 1) // PAIR
    rows = []
    for h in range(N_KV_HEADS):
        row = jnp.full((1, HEAD_LANES), sinks_ref[0, h * Q_PER_KV], F32)
        for g in range(1, Q_PER_KV):
            row = jnp.where(tile == g, sinks_ref[0, h * Q_PER_KV + g], row)
        rows.append(row)
    return rows


def _finish(unit, s, sinks, o_ref):
    _, vtwin, lanes, valid, masked_rows, h = unit
    nk = s.shape[0]
    base = h * Q_PER_KV * HEAD_DIM
    pieces, done = [], 0
    for start, stop in masked_rows:
        if start > done:
            pieces.append(s[done:start])
        pieces.append(jnp.where(valid[start:stop], s[start:stop], NEG_INF))
        done = stop
    if done < nk:
        pieces.append(s[done:])
    s = jnp.concatenate(pieces, axis=0)
    sink = sinks[h]
    m = jnp.maximum(jnp.max(s, axis=0, keepdims=True), sink)
    p = jnp.exp(s - m).astype(BF16)
    v_ones = jnp.concatenate(
        [vtwin[h * HEAD_DIM:(h + 1) * HEAD_DIM, :], jnp.ones((ONES_ROWS, nk), BF16)], axis=0)
    ov = jnp.dot(v_ones, p, preferred_element_type=F32)
    den = ov[HEAD_DIM:HEAD_DIM + 1, :] + jnp.exp(sink - m)
    o = ov[:HEAD_DIM, :] * (1.0 / den)
    for g in range(Q_PER_KV):
        o_ref[base + g * HEAD_DIM:base + (g + 1) * HEAD_DIM, lanes] = (
            o[:, g * PAIR:(g + 1) * PAIR].astype(BF16))


def _attend_units(units, qt_ref, sinks_ref, o_ref):
    sinks = _sink_rows(sinks_ref)
    s_next = _scores(units[0], qt_ref)
    for i, unit in enumerate(units):
        s = s_next
        if i + 1 < len(units):
            s_next = _scores(units[i + 1], qt_ref)
        _finish(unit, s, sinks, o_ref)


def _attn_prompt_kernel(sinks_ref, qt_ref, kp_ref, kc_ref, vtp_ref, vtc_ref, *refs, n_pairs):
    n_cast = (len(refs) - 1) // 2
    o_ref = refs[n_cast]
    for w_ref, wb_ref in zip(refs[:n_cast], refs[n_cast + 1:]):
        wb_ref[...] = w_ref[...].astype(BF16)
    tq = n_pairs * PAIR
    nk = WINDOW + PAIR
    kk = jnp.concatenate([kp_ref[...], kc_ref[...]], axis=0).astype(BF16)
    vt = jnp.concatenate([vtp_ref[...], vtc_ref[...]], axis=1)
    r = lax.broadcasted_iota(jnp.int32, (nk, HEAD_LANES), 0)
    first_chunk = (lax.broadcasted_iota(jnp.int32, (nk, HEAD_LANES), 1) & (PAIR - 1)) < CHUNK
    lo = jnp.where(first_chunk, 0, CHUNK)
    hi = jnp.where(first_chunk, BAND, nk)
    first_pos = pl.program_id(1) * tq - WINDOW
    units = []
    for pp in range(n_pairs):
        lo_pp = jnp.maximum(lo, -first_pos) if pp == 0 else lo
        valid = (r >= lo_pp) & (r < hi)
        masked_rows = ((0, nk),) if pp == 0 else ((0, CHUNK), (nk - CHUNK, nk))
        for h in range(N_KV_HEADS):
            units.append((kk[pp * PAIR:pp * PAIR + nk], vt[:, pp * PAIR:pp * PAIR + nk],
                          slice(pp * PAIR, (pp + 1) * PAIR), valid, masked_rows, h))
    _attend_units(units, qt_ref, sinks_ref, o_ref)


def _attn_sample_kernel(sinks_ref, qt_ref, ck_ref, kn_ref, cvt_ref, vtn_ref, o_ref, *, n_pairs):
    nk = 2 * WINDOW + PAIR
    r = lax.broadcasted_iota(jnp.int32, (nk, HEAD_LANES), 0)
    query_seq = (lax.broadcasted_iota(jnp.int32, (nk, HEAD_LANES), 1) & (PAIR - 1)) >> 6
    key_seq = jnp.where(r < 2 * WINDOW, r >> 7, (r - 2 * WINDOW) >> 6)
    valid = query_seq == key_seq
    units = []
    for pp in range(n_pairs):
        lanes = slice(pp * PAIR, (pp + 1) * PAIR)
        kwin = jnp.concatenate([ck_ref[2 * pp], ck_ref[2 * pp + 1], kn_ref[lanes, :]],
                               axis=0).astype(BF16)
        vtwin = jnp.concatenate([cvt_ref[2 * pp].astype(BF16), cvt_ref[2 * pp + 1].astype(BF16),
                                 vtn_ref[:, lanes]], axis=1)
        units.extend((kwin, vtwin, lanes, valid, ((0, nk),), h) for h in range(N_KV_HEADS))
    _attend_units(units, qt_ref, sinks_ref, o_ref)


_SINK_SPEC = pl.BlockSpec(memory_space=pltpu.SMEM)


def _attention_prompt(sinks, qt, k, vt, bsz, seq, n_pairs, f32_weights):
    tq = n_pairs * PAIR
    nt = seq // tq
    steps = bsz * nt
    wpt = tq // WINDOW
    cur_c = lambda b, i: (0, b * nt + i)
    cur_r = lambda b, i: (b * nt + i, 0)
    prev = lambda b, i: jnp.maximum((b * nt + i) * wpt - 1, 0)
    assert all(w.shape[0] % (steps * BF16_TILE_ROWS) == 0 for w in f32_weights)
    cast_specs = [pl.BlockSpec((w.shape[0] // steps, w.shape[1]), cur_r) for w in f32_weights]
    return pl.pallas_call(
        functools.partial(_attn_prompt_kernel, n_pairs=n_pairs),
        grid=(bsz, nt),
        in_specs=[
            _SINK_SPEC,
            pl.BlockSpec((PROJ_Q, tq), cur_c),
            pl.BlockSpec((WINDOW, PROJ_KV), lambda b, i: (prev(b, i), 0)),
            pl.BlockSpec((tq, PROJ_KV), cur_r),
            pl.BlockSpec((PROJ_KV, WINDOW), lambda b, i: (0, prev(b, i))),
            pl.BlockSpec((PROJ_KV, tq), cur_c),
            *cast_specs,
        ],
        out_specs=[pl.BlockSpec((PROJ_Q, tq), cur_c), *cast_specs],
        out_shape=[jax.ShapeDtypeStruct((PROJ_Q, bsz * seq), BF16),
                   *[jax.ShapeDtypeStruct(w.shape, BF16) for w in f32_weights]],
        compiler_params=pltpu.CompilerParams(
            dimension_semantics=("arbitrary", "arbitrary"), vmem_limit_bytes=V7X_VMEM_LIMIT_BYTES),
        name="attn_prompt",
    )(sinks, qt, k, k, vt, vt, *f32_weights)


def _attention_sample(sinks, qt, cache_k, k, cache_vt, vt, first_token, n_pairs):
    n = cache_k.shape[0] * CHUNK
    tq = n_pairs * PAIR
    assert first_token % tq == 0
    off = first_token // tq
    return pl.pallas_call(
        functools.partial(_attn_sample_kernel, n_pairs=n_pairs),
        grid=(n // tq,),
        in_specs=[
            _SINK_SPEC,
            pl.BlockSpec((PROJ_Q, tq), lambda i: (0, i + off)),
            pl.BlockSpec((2 * n_pairs, WINDOW, PROJ_KV), lambda i: (i, 0, 0)),
            pl.BlockSpec((tq, PROJ_KV), lambda i: (i + off, 0)),
            pl.BlockSpec((2 * n_pairs, PROJ_KV, WINDOW), lambda i: (i, 0, 0)),
            pl.BlockSpec((PROJ_KV, tq), lambda i: (0, i + off)),
        ],
        out_specs=pl.BlockSpec((PROJ_Q, tq), lambda i: (0, i)),
        out_shape=jax.ShapeDtypeStruct((PROJ_Q, n), BF16),
        compiler_params=pltpu.CompilerParams(dimension_semantics=("arbitrary",)),
        name="attn_sample",
    )(sinks, qt, cache_k, k, cache_vt, vt)


def _windows_kernel(kp_ref, vp_ref, kn_ref, vn_ref, ckt_ref, cvt_ref, wkp_ref, wvp_ref, wks_ref, wvs_ref):
    first_half = lax.broadcasted_iota(jnp.int32, (PROJ_KV, WINDOW), 1) < CHUNK
    for new_p, new_s, cache_t, win_p, win_s in ((kp_ref, kn_ref, ckt_ref, wkp_ref, wks_ref),
                                                (vp_ref, vn_ref, cvt_ref, wvp_ref, wvs_ref)):
        win_p[0] = new_p[...].T
        for pair in range(cache_t.shape[0] // 2):
            new_t = new_s[pair * PAIR:(pair + 1) * PAIR, :].T
            swapped = pltpu.roll(new_t, CHUNK, 1)
            for j, new in enumerate((swapped, new_t)):
                old = pltpu.roll(cache_t[2 * pair + j], CHUNK, 1)
                win_s[2 * pair + j] = jnp.where(first_half, old, new)


def _windows(k, v, cache_kt, cache_vt, bp, seq):
    bs = cache_kt.shape[0]
    assert bs % (2 * bp) == 0 and seq % WINDOW == 0
    share = bs // bp
    assert (bp * seq) % (share * CHUNK) == 0
    first_share = bp * seq // (share * CHUNK)
    last_window = pl.BlockSpec((WINDOW, PROJ_KV), lambda b: ((b + 1) * (seq // WINDOW) - 1, 0))
    new_rows = pl.BlockSpec((share * CHUNK, PROJ_KV), lambda b: (first_share + b, 0))
    one = pl.BlockSpec((1, PROJ_KV, WINDOW), lambda b: (b, 0, 0))
    many = pl.BlockSpec((share, PROJ_KV, WINDOW), lambda b: (b, 0, 0))
    return pl.pallas_call(
        _windows_kernel,
        grid=(bp,),
        in_specs=[last_window, last_window, new_rows, new_rows, many, many],
        out_specs=[one, one, many, many],
        out_shape=[jax.ShapeDtypeStruct((bp, PROJ_KV, WINDOW), F32)] * 2
        + [jax.ShapeDtypeStruct((bs, PROJ_KV, WINDOW), F32)] * 2,
        compiler_params=pltpu.CompilerParams(dimension_semantics=("arbitrary",)),
        name="windows",
    )(k, v, k, v, cache_kt, cache_vt)


def _power_table(tau, nbits, a_re, a_im):
    rows = tau.shape[0]
    w_re = jnp.ones((rows, STATE_LANES), F32)
    w_im = jnp.zeros((rows, STATE_LANES), F32)
    p_re, p_im = a_re, a_im
    for k in range(nbits):
        bit = ((tau >> k) & 1) == 1
        f_re = jnp.where(bit, p_re, 1.0)
        f_im = jnp.where(bit, p_im, 0.0)
        w_re, w_im = w_re * f_re - w_im * f_im, w_re * f_im + w_im * f_re
        p_re, p_im = p_re * p_re - p_im * p_im, 2.0 * p_re * p_im
    return w_re, w_im


def _build_tables(par_ref, p_ref, qt_ref, coef_ref):
    lo = lax.broadcasted_iota(jnp.int32, (1, STATE_LANES), 1) < SSM_STATE
    mat = lambda i: par_ref[PAR_MATS + i * SSM_GROUP:PAR_MATS + (i + 1) * SSM_GROUP, :]
    lr, li = par_ref[0:1, :], par_ref[1:2, :]
    dt = jnp.exp(par_ref[2:3, :])
    mag = jnp.exp(lr * dt)
    a_re, a_im = mag * jnp.cos(li * dt), mag * jnp.sin(li * dt)
    nr, ni = a_re - 1.0, a_im
    den = lr * lr + li * li
    f_re, f_im = (nr * lr + ni * li) / den, (ni * lr - nr * li) / den
    b_re, b_im = mat(0), mat(1)
    bb_re = f_re * b_re - f_im * b_im
    bb_im = f_re * b_im + f_im * b_re
    c_re, c_im = mat(2), mat(3)

    tau = lax.broadcasted_iota(jnp.int32, (SSM_T, 1), 0)
    w_re, w_im = _power_table(tau, 6, a_re, a_im)
    w1_re, w1_im = w_re * a_re - w_im * a_im, w_re * a_im + w_im * a_re
    wr_re, wr_im = _power_table(SSM_T - 1 - tau, 6, a_re, a_im)

    def outer(c, w):
        return (c[:, None, :] * w[None, :, :]).reshape(SSM_CW, STATE_LANES)

    cw_mix = (outer(c_re, jnp.where(lo, w_re, w_im)) + outer(c_im, jnp.where(lo, -w_im, w_re)))
    bb_mix = jnp.where(lo, bb_re, -bb_im)
    strip = lax.dot_general(bb_mix, cw_mix, (((1,), (1,)), ((), ())),
                            precision=lax.Precision.HIGHEST,
                            preferred_element_type=F32)

    x_a, y_a = jnp.where(lo, bb_re, bb_im), jnp.where(lo, -bb_im, bb_re)
    p_ref[...] = (outer(x_a, wr_re) + outer(y_a, wr_im)).astype(BF16)

    qt_ref[...] = (outer(c_re, jnp.where(lo, w1_re, -w1_im))
                   + outer(c_im, jnp.where(lo, -w1_im, -w1_re))).astype(BF16)

    t_re, t_im = a_re, a_im
    for _ in range(6):
        t_re, t_im = t_re * t_re - t_im * t_im, 2.0 * t_re * t_im
    for k in range(COEF_ROWS // 2):
        coef_ref[2 * k:2 * k + 1, :] = t_re
        coef_ref[2 * k + 1:2 * k + 2, :] = jnp.where(lo, -t_im, t_im)
        t_re, t_im = t_re * t_re - t_im * t_im, 2.0 * t_re * t_im
    return strip


def _toeplitz_columns(strip, m_ref, cols):
    width = cols.stop - cols.start
    s_idx = lax.broadcasted_iota(jnp.int32, (SSM_T, width), 0)
    t_idx = lax.broadcasted_iota(jnp.int32, (SSM_T, width), 1) & (SSM_T - 1)
    causal = t_idx >= s_idx
    for c in range(SSM_GROUP):
        rows = jnp.broadcast_to(strip[c:c + 1, cols], (SSM_T, width))
        shifted = pltpu.roll(rows, 0, 1, stride=1, stride_axis=0)
        m_ref[c * SSM_T:(c + 1) * SSM_T, cols] = jnp.where(causal, shifted, 0.0).astype(BF16)


PAR_MATS = 8
PAR_ROWS = PAR_MATS + 4 * SSM_GROUP


def _table_params(lam_re, lam_im, log_step, b_re, b_im, c_re, c_im):
    step = jnp.broadcast_to(log_step[:, None, None], (SSM_GROUPS, 1, SSM_STATE))
    filler = jnp.zeros((SSM_GROUPS, PAR_MATS - 3, SSM_STATE), F32)
    pack = jnp.concatenate([lam_re[:, None, :], lam_im[:, None, :], step, filler,
                            jnp.swapaxes(b_re, 1, 2), jnp.swapaxes(b_im, 1, 2), c_re, c_im], axis=1)
    return jnp.concatenate([pack, pack], axis=-1)


def _cmul(a1, a2, h, hs):
    return a1 * h + a2 * hs, a1 * hs - a2 * h


def _ssm_kernel(par_ref, u_ref, h0_ref, y_ref, hfin_ref, m_ref, p_ref, qt_ref, coef_ref,
                *, rows_per_seq, prompt_rows):
    for gi in range(SSM_GROUPS_PER_STEP):
        _ssm_group(par_ref.at[gi], u_ref, gi * SSM_GROUP, h0_ref.at[gi], y_ref, hfin_ref.at[gi],
                   m_ref.at[gi], p_ref.at[gi], qt_ref.at[gi], coef_ref.at[gi],
                   rows_per_seq=rows_per_seq, prompt_rows=prompt_rows)


def _ssm_group(par_ref, u_ref, c0, h0_ref, y_ref, hfin_ref, m_ref, p_ref, qt_ref, coef_ref,
               *, rows_per_seq, prompt_rows):
    strip = _build_tables(par_ref, p_ref, qt_ref, coef_ref)
    lo = lax.broadcasted_iota(jnp.int32, (1, ROW_TOKENS), 1) < SSM_T

    def chunk_rows(ref):
        even, odd = [], []
        rows = ref.shape[0] * ROW_GROUP
        for k in range(SSM_GROUP // 2):
            a = ref[:, c0 + 2 * k].reshape(rows, ROW_TOKENS)
            b = ref[:, c0 + 2 * k + 1].reshape(rows, ROW_TOKENS)
            even.append(jnp.where(lo, a, pltpu.roll(b, SSM_T, 1)))
            odd.append(jnp.where(lo, pltpu.roll(a, SSM_T, 1), b))
        return jnp.concatenate(even, axis=1), jnp.concatenate(odd, axis=1)

    def store_rows(y_even, y_odd, ref, first_channel):
        for k in range(y_even.shape[1] // ROW_TOKENS):
            te = y_even[:, k * ROW_TOKENS:(k + 1) * ROW_TOKENS]
            to = y_odd[:, k * ROW_TOKENS:(k + 1) * ROW_TOKENS]
            tiles = (ref.shape[0], ROW_GROUP, ROW_TOKENS)
            c = c0 + first_channel + 2 * k
            ref[:, c] = jnp.where(lo, te, pltpu.roll(to, SSM_T, 1)).reshape(tiles)
            ref[:, c + 1] = jnp.where(lo, pltpu.roll(te, SSM_T, 1), to).reshape(tiles)

    even, odd = chunk_rows(u_ref)
    rp, rs = prompt_rows, even.shape[0] - prompt_rows
    u = jnp.concatenate([even[:rp], odd[:rp], even[rp:], odd[rp:]], axis=0).astype(BF16)
    s1 = jnp.dot(u, p_ref[...], preferred_element_type=F32)
    s2 = pltpu.roll(s1, SSM_STATE, 1)
    a1, a2 = coef_ref[0:1, :], coef_ref[1:2, :]

    e1, e2, o1, o2 = s1[:rp], s2[:rp], s1[rp:2 * rp], s2[rp:2 * rp]
    x1, x2 = _cmul(a1, a2, e1, e2)
    x1, x2 = x1 + o1, x2 + o2
    pos = lax.broadcasted_iota(jnp.int32, (rp, STATE_LANES), 0) & (rows_per_seq - 1)
    for k in range(rows_per_seq.bit_length() - 1):
        d = 1 << k
        b1, b2 = coef_ref[2 + 2 * k:3 + 2 * k, :], coef_ref[3 + 2 * k:4 + 2 * k, :]
        sh1 = jnp.where(pos >= d, pltpu.roll(x1, d, 0), 0.0)
        sh2 = jnp.where(pos >= d, pltpu.roll(x2, d, 0), 0.0)
        y1, y2 = _cmul(b1, b2, sh1, sh2)
        x1, x2 = x1 + y1, x2 + y2
    g1 = jnp.where(pos >= 1, pltpu.roll(x1, 1, 0), 0.0)
    g2 = jnp.where(pos >= 1, pltpu.roll(x2, 1, 0), 0.0)
    ho1 = _cmul(a1, a2, g1, g2)[0] + e1
    hfin_ref[0:STATE_OUT_ROWS, :] = jnp.zeros((STATE_OUT_ROWS, STATE_LANES), F32)
    for b in range(rp // rows_per_seq):
        last = (b + 1) * rows_per_seq - 1
        hfin_ref[b:b + 1, :] = x1[last:last + 1, :]

    h0e, h0o = h0_ref[0], h0_ref[1]
    swap = lambda h: pltpu.roll(h, SSM_STATE, 1)
    hfin_ref[STATE_OUT_ROWS:STATE_OUT_ROWS + rs, :] = a1 * h0e + a2 * swap(h0e) + s1[2 * rp:2 * rp + rs]
    hfin_ref[STATE_OUT_ROWS + rs:, :] = a1 * h0o + a2 * swap(h0o) + s1[2 * rp + rs:]

    hprev = jnp.concatenate([g1, ho1, h0e, h0o], axis=0).astype(BF16)
    blocks = [slice(c * SSM_T, (c + M_BLOCK_CHANNELS) * SSM_T) for c in range(0, SSM_GROUP, M_BLOCK_CHANNELS)]
    _toeplitz_columns(strip, m_ref, blocks[0])
    for i, cols in enumerate(blocks):
        y = lax.dot_general(hprev, qt_ref[cols, :], (((1,), (1,)), ((), ())), preferred_element_type=F32)
        y += jnp.dot(u, m_ref[:, cols], preferred_element_type=F32)
        if i + 1 < len(blocks):
            _toeplitz_columns(strip, m_ref, blocks[i + 1])
        store_rows(jnp.concatenate([y[:rp], y[2 * rp:2 * rp + rs]], axis=0),
                   jnp.concatenate([y[rp:2 * rp], y[2 * rp + rs:]], axis=0), y_ref, i * M_BLOCK_CHANNELS)


def _ssm(table_params, u3, h0_pack, *, rows_per_seq, prompt_rows):
    groups = u3.shape[0]
    rs = groups * ROW_GROUP - prompt_rows
    u4 = u3.reshape(groups, SSM_WIDTH, ROW_GROUP, ROW_TOKENS)
    assert rows_per_seq & (rows_per_seq - 1) == 0 and 2 * rows_per_seq.bit_length() <= COEF_ROWS
    gps = SSM_GROUPS_PER_STEP
    g3 = lambda g: (g, 0, 0)
    u_spec = pl.BlockSpec((groups, gps * SSM_GROUP, ROW_GROUP, ROW_TOKENS), lambda g: (0, g, 0, 0))
    fin_rows = STATE_OUT_ROWS + 2 * rs
    return pl.pallas_call(
        functools.partial(_ssm_kernel, rows_per_seq=rows_per_seq, prompt_rows=prompt_rows),
        grid=(SSM_GROUPS // gps,),
        in_specs=[
            pl.BlockSpec((gps, PAR_ROWS, STATE_LANES), g3),
            u_spec,
            pl.BlockSpec((gps, 2, rs, STATE_LANES), lambda g: (g, 0, 0, 0)),
        ],
        out_specs=[
            u_spec,
            pl.BlockSpec((gps, fin_rows, STATE_LANES), g3),
        ],
        out_shape=[
            jax.ShapeDtypeStruct(u4.shape, F32),
            jax.ShapeDtypeStruct((SSM_GROUPS, fin_rows, STATE_LANES), F32),
        ],
        scratch_shapes=[
            pltpu.VMEM((gps, SSM_CW, SSM_CW), BF16),
            pltpu.VMEM((gps, SSM_CW, STATE_LANES), BF16),
            pltpu.VMEM((gps, SSM_CW, STATE_LANES), BF16),
            pltpu.VMEM((gps, COEF_ROWS, STATE_LANES), F32),
        ],
        compiler_params=pltpu.CompilerParams(
            dimension_semantics=("arbitrary",), vmem_limit_bytes=V7X_VMEM_LIMIT_BYTES),
        name="ssm",
    )(table_params, u4, h0_pack)


def _post_kernel(xp_ref, xs_ref, atp_ref, ats_ref, ut_ref, yt_ref, *refs, alpha, prompt_tiles):
    weights, (op_ref, os_ref) = refs[:-2], refs[-2:]

    @pl.when(pl.program_id(0) < prompt_tiles)
    def _():
        _post_tile(xp_ref, atp_ref, ut_ref, yt_ref, *weights, op_ref, alpha=alpha)

    @pl.when(pl.program_id(0) >= prompt_tiles)
    def _():
        _post_tile(xs_ref, ats_ref, ut_ref, yt_ref, *weights, os_ref, alpha=alpha)


def _post_tile(x_ref, at_ref, ut_ref, yt_ref, lng_ref, lnb_ref, cols_ref, wglut_ref,
               wout_ref, ln1g_ref, ln1b_ref, wgu_ref, wdown_ref, ln2g_ref, ln2b_ref, o_ref,
               *, alpha):
    xn = _layer_norm(x_ref[...], lng_ref[...], lnb_ref[...])
    n_rows = x_ref.shape[0] // ROW_TOKENS
    first_row = (pl.program_id(0) % (ROW_GROUP // n_rows)) * n_rows

    def feature_major(ref):
        return jnp.concatenate(
            [ref[pl.ds(first_row + j, SSM_WIDTH, stride=ROW_GROUP), :] for j in range(n_rows)], axis=1)

    ys = feature_major(yt_ref) + cols_ref[0] * feature_major(ut_ref)
    gl = 0.5 * ys * (1.0 + lax.erf(ys * math.sqrt(0.5)))
    z = jnp.dot(wglut_ref[...], gl.astype(BF16), preferred_element_type=F32) + cols_ref[1]
    s = gl * jax.nn.sigmoid(z)
    tn = (((0,), (0,)), ((), ()))
    mix = lax.dot_general(at_ref[...], wout_ref[:PROJ_Q, :], tn, preferred_element_type=F32)
    mix += lax.dot_general(s.astype(BF16), wout_ref[PROJ_Q:, :], tn, preferred_element_type=F32)
    h = _layer_norm(alpha * xn + mix, ln1g_ref[...], ln1b_ref[...])
    hb = h.astype(BF16)
    f = jnp.zeros_like(h)
    for start in range(0, D_FF, FF_BLOCK):
        stop = min(start + FF_BLOCK, D_FF)
        cols = slice(start, stop)
        up_cols = slice(D_FF + start, D_FF + stop)
        g = jnp.dot(hb, wgu_ref[:, cols], preferred_element_type=F32)
        up = jnp.dot(hb, wgu_ref[:, up_cols], preferred_element_type=F32)
        act = (g * jax.nn.sigmoid(g)) * up
        f += jnp.dot(act.astype(BF16), wdown_ref[cols, :], preferred_element_type=F32)
    o_ref[...] = _layer_norm(alpha * h + f, ln2g_ref[...], ln2b_ref[...])


def _post(xp2d, xs2d, atp, ats, ut3, yt3, ln_g, ln_b, d_bglu_cols, w_glu_t, w_out, ln1_g, ln1_b,
          w_gate_up, w_down, ln2_g, ln2_b, *, alpha, tm):
    n_p, n_s = xp2d.shape[0], xs2d.shape[0]
    tp, ts = n_p // tm, n_s // tm
    assert PROJ_TILE % tm == 0 and n_p % PROJ_TILE == 0
    group = lambda i: (i // (PROJ_TILE // tm), 0, 0)
    group_spec = pl.BlockSpec((None, SSM_WIDTH * ROW_GROUP, ROW_TOKENS), group)
    prompt = lambda i: jnp.minimum(i, tp - 1)
    sample = lambda i: jnp.maximum(i - tp, 0)
    const = lambda i: (0, 0)
    resident = lambda shape: pl.BlockSpec(shape, const, pipeline_mode=pl.Buffered(1))
    vec = lambda width: pl.BlockSpec((1, width), const)
    columns = pl.BlockSpec((2, SSM_WIDTH, 1), lambda i: (0, 0, 0))
    return pl.pallas_call(
        functools.partial(_post_kernel, alpha=alpha, prompt_tiles=tp),
        grid=(tp + ts,),
        in_specs=[
            pl.BlockSpec((tm, D_MODEL), lambda i: (prompt(i), 0)),
            pl.BlockSpec((tm, D_MODEL), lambda i: (sample(i), 0)),
            pl.BlockSpec((PROJ_Q, tm), lambda i: (0, prompt(i))),
            pl.BlockSpec((PROJ_Q, tm), lambda i: (0, sample(i))),
            group_spec, group_spec,
            vec(D_MODEL), vec(D_MODEL), columns,
            resident((SSM_WIDTH, SSM_WIDTH)),
            resident((D_MODEL, D_MODEL)), vec(D_MODEL), vec(D_MODEL),
            resident((D_MODEL, 2 * D_FF)), resident((D_FF, D_MODEL)),
            vec(D_MODEL), vec(D_MODEL),
        ],
        out_specs=[
            pl.BlockSpec((tm, D_MODEL), lambda i: (prompt(i), 0)),
            pl.BlockSpec((tm, D_MODEL), lambda i: (sample(i), 0)),
        ],
        out_shape=[
            jax.ShapeDtypeStruct((n_p, D_MODEL), F32),
            jax.ShapeDtypeStruct((n_s, D_MODEL), F32),
        ],
        compiler_params=pltpu.CompilerParams(
            dimension_semantics=("arbitrary",), vmem_limit_bytes=V7X_VMEM_LIMIT_BYTES),
        name="post",
    )(xp2d, xs2d, atp, ats, ut3, yt3, ln_g, ln_b, d_bglu_cols, w_glu_t, w_out, ln1_g, ln1_b,
      w_gate_up, w_down, ln2_g, ln2_b)


def kernel(x_prompt, x_sample, cache_win_k, cache_win_v, state_ssm_re, state_ssm_im,
           ln_in_g, ln_in_b, w_in, attn_sinks, ssm_lambda_re, ssm_lambda_im, ssm_log_step,
           ssm_b_re, ssm_b_im, ssm_c_re, ssm_c_im, ssm_d, w_glu, b_glu, w_out,
           ln1_g, ln1_b, w_gate_up, w_down, ln2_g, ln2_b):
    depth = w_in.shape[0]
    assert depth == 1, "single-layer step"
    bp, lp, _ = x_prompt.shape
    bs, ls, _ = x_sample.shape
    assert ls == SSM_T and bs % 2 == 0 and lp % 512 == 0 and bp <= STATE_OUT_ROWS
    win_rows = cache_win_k.shape[2]
    assert win_rows == WINDOW
    alpha = (2.0 * depth) ** 0.25
    l = 0
    row = lambda a: a.reshape(1, -1)

    lng, lnb = row(ln_in_g), row(ln_in_b)
    xp2 = x_prompt.reshape(bp * lp, D_MODEL)
    xs2 = x_sample.reshape(bs * ls, D_MODEL)
    n_p = bp * lp
    qt, vt, u3, k, v = _proj(xp2, xs2, lng, lnb, w_in[l][:, PROJ_Q:PROJ_Q + KV].astype(BF16),
                             w_in[l].T.astype(BF16))

    atp, w_gu_b, w_down_b, w_out_b = _attention_prompt(
        attn_sinks, qt, k, vt, bp, lp, n_pairs=ATTN_PAIRS, f32_weights=(w_gate_up[l], w_down[l], w_out[l]))
    ck = cache_win_k[l].reshape(bs, win_rows, PROJ_KV)
    ckt = jnp.swapaxes(ck, 1, 2)
    cvt = jnp.swapaxes(cache_win_v[l].reshape(bs, win_rows, PROJ_KV), 1, 2)
    ats = _attention_sample(attn_sinks, qt, ck, k, cvt, vt, first_token=n_p, n_pairs=ATTN_PAIRS)
    windows = _windows(k, v, ckt, cvt, bp, lp)

    table_params = _table_params(
        ssm_lambda_re[l], ssm_lambda_im[l], ssm_log_step[l],
        ssm_b_re[l], ssm_b_im[l], ssm_c_re[l], ssm_c_im[l])
    h0 = jnp.concatenate([state_ssm_re[l], state_ssm_im[l]], axis=-1)
    h0_pack = jnp.transpose(h0.reshape(bs // 2, 2, SSM_GROUPS, STATE_LANES), (2, 1, 0, 3))
    y4, hfin = _ssm(table_params, u3, h0_pack,
                    rows_per_seq=lp // ROW_TOKENS, prompt_rows=n_p // ROW_TOKENS)

    post_args = (lng, lnb, jnp.stack([ssm_d[l], b_glu[l]])[:, :, None], w_glu[l].T.astype(BF16),
                 w_out_b, row(ln1_g[l]), row(ln1_b[l]), w_gu_b, w_down_b, row(ln2_g[l]), row(ln2_b[l]))
    out_p, out_s = _post(xp2, xs2, atp, ats, u3, y4.reshape(u3.shape), *post_args, alpha=alpha, tm=512)

    win_k_p, win_v_p, win_k_s, win_v_s = (
        jnp.swapaxes(w, 1, 2).reshape(1, -1, win_rows, N_KV_HEADS, HEAD_DIM) for w in windows)
    sp = jnp.swapaxes(hfin[:, :bp], 0, 1)
    ss = jnp.transpose(hfin[:, STATE_OUT_ROWS:].reshape(SSM_GROUPS, 2, bs // 2, STATE_LANES),
                       (2, 1, 0, 3)).reshape(bs, SSM_GROUPS, STATE_LANES)
    return (out_p.reshape(bp, lp, D_MODEL), out_s.reshape(bs, ls, D_MODEL),
            win_k_p, win_v_p, sp[None, ..., :SSM_STATE], sp[None, ..., SSM_STATE:],
            win_k_s, win_v_s, ss[None, ..., :SSM_STATE], ss[None, ..., SSM_STATE:])
```

```python
import functools
import math

import jax
import jax.numpy as jnp
from jax import lax
from jax.experimental import pallas as pl
from jax.experimental.pallas import tpu as pltpu

F32 = jnp.float32
BF16 = jnp.bfloat16

D_MODEL = 1024
HEAD_DIM = 64
N_HEADS = 8
N_KV_HEADS = 2
Q_PER_KV = N_HEADS // N_KV_HEADS
CHUNK = 64
WINDOW = 128
WIN_CHUNKS = WINDOW // CHUNK
BAND = (WIN_CHUNKS + 1) * CHUNK
PROJ_Q = N_HEADS * HEAD_DIM
PROJ_KV = N_KV_HEADS * HEAD_DIM
SSM_WIDTH = 512
SSM_GROUP = 16
SSM_GROUPS = SSM_WIDTH // SSM_GROUP
SSM_STATE = 64
STATE_LANES = 2 * SSM_STATE
D_FF = 2816
D_IN_PROJ = PROJ_Q + 2 * PROJ_KV + SSM_WIDTH
LN_EPS = 1e-5
NEG_INF = -1e30

SSM_T = CHUNK
SSM_CW = SSM_T * SSM_GROUP
ROW_TOKENS = 2 * SSM_T
ROW_GROUP = 8
PROJ_TILE = ROW_GROUP * ROW_TOKENS
PROJ_PARTS = 2
COEF_ROWS = 16
STATE_OUT_ROWS = 8
M_BLOCK_CHANNELS = 4
SSM_GROUPS_PER_STEP = 2
FF_BLOCK = 256
V7X_VMEM_LIMIT_BYTES = 56 * 1024 * 1024


def _layer_norm(x, g, b):
    mu = jnp.mean(x, axis=-1, keepdims=True)
    xc = x - mu
    var = jnp.mean(xc * xc, axis=-1, keepdims=True)
    return xc * lax.rsqrt(var + LN_EPS) * g + b


KV = 2 * PROJ_KV
VU = PROJ_KV + SSM_WIDTH


def _proj_kernel(xp_ref, xs_ref, g_ref, b_ref, w_ref, wqt_ref, wvut_ref, qt_ref, vt_ref, ut_ref, k_ref, v_ref,
                 *, prompt_tiles):
    outs = (g_ref, b_ref, w_ref, wqt_ref, wvut_ref, qt_ref, vt_ref, ut_ref, k_ref, v_ref)

    @pl.when(pl.program_id(0) < prompt_tiles)
    def _():
        _proj_tile(xp_ref, *outs)

    @pl.when(pl.program_id(0) >= prompt_tiles)
    def _():
        _proj_tile(xs_ref, *outs)


def _proj_tile(x_ref, g_ref, b_ref, w_ref, wqt_ref, wvut_ref, qt_ref, vt_ref, ut_ref, k_ref, v_ref):
    rows_per_part = ROW_GROUP // PROJ_PARTS
    nt = (((1,), (1,)), ((), ()))
    for part in range(PROJ_PARTS):
        tok = slice(part * rows_per_part * ROW_TOKENS, (part + 1) * rows_per_part * ROW_TOKENS)
        xb = _layer_norm(x_ref[tok, :], g_ref[...], b_ref[...]).astype(BF16)
        p = jnp.dot(xb, w_ref[...], preferred_element_type=F32)
        k_ref[tok, :] = p[:, :PROJ_KV]
        v_ref[tok, :] = p[:, PROJ_KV:]
        qt = lax.dot_general(wqt_ref[...], xb, nt, preferred_element_type=F32)
        vut = lax.dot_general(wvut_ref[...], xb, nt, preferred_element_type=F32)
        qt_ref[:, tok] = (qt * (HEAD_DIM ** -0.5)).astype(BF16)
        vt_ref[:, tok] = vut[:PROJ_KV].astype(BF16)
        for j in range(rows_per_part):
            ut_ref[pl.ds(part * rows_per_part + j, SSM_WIDTH, stride=ROW_GROUP), :] = (
                vut[PROJ_KV:, j * ROW_TOKENS:(j + 1) * ROW_TOKENS])


def _proj(xp2d, xs2d, ln_g, ln_b, w_kv, w_in_t):
    assert D_IN_PROJ == 2 * VU
    tm = PROJ_TILE
    tp, ts = xp2d.shape[0] // tm, xs2d.shape[0] // tm
    n = (tp + ts) * tm
    const = lambda i: (0, 0)
    row = lambda i: (i, 0)
    col = lambda i: (0, i)
    return pl.pallas_call(
        functools.partial(_proj_kernel, prompt_tiles=tp),
        grid=(tp + ts,),
        in_specs=[
            pl.BlockSpec((tm, D_MODEL), lambda i: (jnp.minimum(i, tp - 1), 0)),
            pl.BlockSpec((tm, D_MODEL), lambda i: (jnp.maximum(i - tp, 0), 0)),
            pl.BlockSpec((1, D_MODEL), const),
            pl.BlockSpec((1, D_MODEL), const),
            pl.BlockSpec((D_MODEL, KV), const),
            pl.BlockSpec((PROJ_Q, D_MODEL), const),
            pl.BlockSpec((VU, D_MODEL), lambda i: (1, 0)),
        ],
        out_specs=[
            pl.BlockSpec((PROJ_Q, tm), col),
            pl.BlockSpec((PROJ_KV, tm), col),
            pl.BlockSpec((None, SSM_WIDTH * ROW_GROUP, ROW_TOKENS), lambda i: (i, 0, 0)),
            pl.BlockSpec((tm, PROJ_KV), row),
            pl.BlockSpec((tm, PROJ_KV), row),
        ],
        out_shape=[
            jax.ShapeDtypeStruct((PROJ_Q, n), BF16),
            jax.ShapeDtypeStruct((PROJ_KV, n), BF16),
            jax.ShapeDtypeStruct((n // tm, SSM_WIDTH * ROW_GROUP, ROW_TOKENS), F32),
            jax.ShapeDtypeStruct((n, PROJ_KV), F32),
            jax.ShapeDtypeStruct((n, PROJ_KV), F32),
        ],
        compiler_params=pltpu.CompilerParams(
            dimension_semantics=("arbitrary",), vmem_limit_bytes=V7X_VMEM_LIMIT_BYTES),
        name="proj",
    )(xp2d, xs2d, ln_g, ln_b, w_kv, w_in_t, w_in_t)


PAIR = 2 * CHUNK
HEAD_LANES = Q_PER_KV * PAIR
ONES_ROWS = 16
ATTN_PAIRS = 16
BF16_TILE_ROWS = 16


def _scores(unit, qt_ref):
    kwin, _, lanes, _, _, h = unit
    base = h * Q_PER_KV * HEAD_DIM
    qrow = jnp.concatenate(
        [qt_ref[base + g * HEAD_DIM:base + (g + 1) * HEAD_DIM, lanes] for g in range(Q_PER_KV)],
        axis=1)
    zero = jnp.zeros_like(qrow)
    qstack = jnp.concatenate([qrow, zero] if h == 0 else [zero, qrow], axis=0)
    return jnp.dot(kwin, qstack, preferred_element_type=F32)


def _sink_rows(sinks_ref):
    tile = lax.broadcasted_iota(jnp.int32, (1, HEAD_LANES), 1) // PAIR
    rows = []
    for h in range(N_KV_HEADS):
        row = jnp.full((1, HEAD_LANES), sinks_ref[0, h * Q_PER_KV], F32)
        for g in range(1, Q_PER_KV):
            row = jnp.where(tile == g, sinks_ref[0, h * Q_PER_KV + g], row)
        rows.append(row)
    return rows


def _finish(unit, s, sinks, o_ref):
    _, vtwin, lanes, valid, masked_rows, h = unit
    nk = s.shape[0]
    base = h * Q_PER_KV * HEAD_DIM
    pieces, done = [], 0
    for start, stop in masked_rows:
        if start > done:
            pieces.append(s[done:start])
        pieces.append(jnp.where(valid[start:stop], s[start:stop], NEG_INF))
        done = stop
    if done < nk:
        pieces.append(s[done:])
    s = jnp.concatenate(pieces, axis=0)
    sink = sinks[h]
    m = jnp.maximum(jnp.max(s, axis=0, keepdims=True), sink)
    p = jnp.exp(s - m).astype(BF16)
    v_ones = jnp.concatenate(
        [vtwin[h * HEAD_DIM:(h + 1) * HEAD_DIM, :], jnp.ones((ONES_ROWS, nk), BF16)], axis=0)
    ov = jnp.dot(v_ones, p, preferred_element_type=F32)
    den = ov[HEAD_DIM:HEAD_DIM + 1, :] + jnp.exp(sink - m)
    o = ov[:HEAD_DIM, :] * (1.0 / den)
    for g in range(Q_PER_KV):
        o_ref[base + g * HEAD_DIM:base + (g + 1) * HEAD_DIM, lanes] = (
            o[:, g * PAIR:(g + 1) * PAIR].astype(BF16))


def _attend_units(units, qt_ref, sinks_ref, o_ref):
    sinks = _sink_rows(sinks_ref)
    s_next = _scores(units[0], qt_ref)
    for i, unit in enumerate(units):
        s = s_next
        if i + 1 < len(units):
            s_next = _scores(units[i + 1], qt_ref)
        _finish(unit, s, sinks, o_ref)


def _attn_prompt_kernel(sinks_ref, qt_ref, kp_ref, kc_ref, vtp_ref, vtc_ref, *refs, n_pairs):
    n_cast = (len(refs) - 1) // 2
    o_ref = refs[n_cast]
    for w_ref, wb_ref in zip(refs[:n_cast], refs[n_cast + 1:]):
        wb_ref[...] = w_ref[...].astype(BF16)
    tq = n_pairs * PAIR
    nk = WINDOW + PAIR
    kk = jnp.concatenate([kp_ref[...], kc_ref[...]], axis=0).astype(BF16)
    vt = jnp.concatenate([vtp_ref[...], vtc_ref[...]], axis=1)
    r = lax.broadcasted_iota(jnp.int32, (nk, HEAD_LANES), 0)
    first_chunk = (lax.broadcasted_iota(jnp.int32, (nk, HEAD_LANES), 1) & (PAIR - 1)) < CHUNK
    lo = jnp.where(first_chunk, 0, CHUNK)
    hi = jnp.where(first_chunk, BAND, nk)
    first_pos = pl.program_id(1) * tq - WINDOW
    units = []
    for pp in range(n_pairs):
        lo_pp = jnp.maximum(lo, -first_pos) if pp == 0 else lo
        valid = (r >= lo_pp) & (r < hi)
        masked_rows = ((0, nk),) if pp == 0 else ((0, CHUNK), (nk - CHUNK, nk))
        for h in range(N_KV_HEADS):
            units.append((kk[pp * PAIR:pp * PAIR + nk], vt[:, pp * PAIR:pp * PAIR + nk],
                          slice(pp * PAIR, (pp + 1) * PAIR), valid, masked_rows, h))
    _attend_units(units, qt_ref, sinks_ref, o_ref)


def _attn_sample_kernel(sinks_ref, qt_ref, ck_ref, kn_ref, cvt_ref, vtn_ref, o_ref, *, n_pairs):
    nk = 2 * WINDOW + PAIR
    r = lax.broadcasted_iota(jnp.int32, (nk, HEAD_LANES), 0)
    query_seq = (lax.broadcasted_iota(jnp.int32, (nk, HEAD_LANES), 1) & (PAIR - 1)) >> 6
    key_seq = jnp.where(r < 2 * WINDOW, r >> 7, (r - 2 * WINDOW) >> 6)
    valid = query_seq == key_seq
    units = []
    for pp in range(n_pairs):
        lanes = slice(pp * PAIR, (pp + 1) * PAIR)
        kwin = jnp.concatenate([ck_ref[2 * pp], ck_ref[2 * pp + 1], kn_ref[lanes, :]],
                               axis=0).astype(BF16)
        vtwin = jnp.concatenate([cvt_ref[2 * pp].astype(BF16), cvt_ref[2 * pp + 1].astype(BF16),
                                 vtn_ref[:, lanes]], axis=1)
        units.extend((kwin, vtwin, lanes, valid, ((0, nk),), h) for h in range(N_KV_HEADS))
    _attend_units(units, qt_ref, sinks_ref, o_ref)


_SINK_SPEC = pl.BlockSpec(memory_space=pltpu.SMEM)


def _attention_prompt(sinks, qt, k, vt, bsz, seq, n_pairs, f32_weights):
    tq = n_pairs * PAIR
    nt = seq // tq
    steps = bsz * nt
    wpt = tq // WINDOW
    cur_c = lambda b, i: (0, b * nt + i)
    cur_r = lambda b, i: (b * nt + i, 0)
    prev = lambda b, i: jnp.maximum((b * nt + i) * wpt - 1, 0)
    assert all(w.shape[0] % (steps * BF16_TILE_ROWS) == 0 for w in f32_weights)
    cast_specs = [pl.BlockSpec((w.shape[0] // steps, w.shape[1]), cur_r) for w in f32_weights]
    return pl.pallas_call(
        functools.partial(_attn_prompt_kernel, n_pairs=n_pairs),
        grid=(bsz, nt),
        in_specs=[
            _SINK_SPEC,
            pl.BlockSpec((PROJ_Q, tq), cur_c),
            pl.BlockSpec((WINDOW, PROJ_KV), lambda b, i: (prev(b, i), 0)),
            pl.BlockSpec((tq, PROJ_KV), cur_r),
            pl.BlockSpec((PROJ_KV, WINDOW), lambda b, i: (0, prev(b, i))),
            pl.BlockSpec((PROJ_KV, tq), cur_c),
            *cast_specs,
        ],
        out_specs=[pl.BlockSpec((PROJ_Q, tq), cur_c), *cast_specs],
        out_shape=[jax.ShapeDtypeStruct((PROJ_Q, bsz * seq), BF16),
                   *[jax.ShapeDtypeStruct(w.shape, BF16) for w in f32_weights]],
        compiler_params=pltpu.CompilerParams(
            dimension_semantics=("arbitrary", "arbitrary"), vmem_limit_bytes=V7X_VMEM_LIMIT_BYTES),
        name="attn_prompt",
    )(sinks, qt, k, k, vt, vt, *f32_weights)


def _attention_sample(sinks, qt, cache_k, k, cache_vt, vt, first_token, n_pairs):
    n = cache_k.shape[0] * CHUNK
    tq = n_pairs * PAIR
    assert first_token % tq == 0
    off = first_token // tq
    return pl.pallas_call(
        functools.partial(_attn_sample_kernel, n_pairs=n_pairs),
        grid=(n // tq,),
        in_specs=[
            _SINK_SPEC,
            pl.BlockSpec((PROJ_Q, tq), lambda i: (0, i + off)),
            pl.BlockSpec((2 * n_pairs, WINDOW, PROJ_KV), lambda i: (i, 0, 0)),
            pl.BlockSpec((tq, PROJ_KV), lambda i: (i + off, 0)),
            pl.BlockSpec((2 * n_pairs, PROJ_KV, WINDOW), lambda i: (i, 0, 0)),
            pl.BlockSpec((PROJ_KV, tq), lambda i: (0, i + off)),
        ],
        out_specs=pl.BlockSpec((PROJ_Q, tq), lambda i: (0, i)),
        out_shape=jax.ShapeDtypeStruct((PROJ_Q, n), BF16),
        compiler_params=pltpu.CompilerParams(dimension_semantics=("arbitrary",)),
        name="attn_sample",
    )(sinks, qt, cache_k, k, cache_vt, vt)


def _windows_kernel(kp_ref, vp_ref, kn_ref, vn_ref, ckt_ref, cvt_ref, wkp_ref, wvp_ref, wks_ref, wvs_ref):
    first_half = lax.broadcasted_iota(jnp.int32, (PROJ_KV, WINDOW), 1) < CHUNK
    for new_p, new_s, cache_t, win_p, win_s in ((kp_ref, kn_ref, ckt_ref, wkp_ref, wks_ref),
                                                (vp_ref, vn_ref, cvt_ref, wvp_ref, wvs_ref)):
        win_p[0] = new_p[...].T
        for pair in range(cache_t.shape[0] // 2):
            new_t = new_s[pair * PAIR:(pair + 1) * PAIR, :].T
            swapped = pltpu.roll(new_t, CHUNK, 1)
            for j, new in enumerate((swapped, new_t)):
                old = pltpu.roll(cache_t[2 * pair + j], CHUNK, 1)
                win_s[2 * pair + j] = jnp.where(first_half, old, new)


def _windows(k, v, cache_kt, cache_vt, bp, seq):
    bs = cache_kt.shape[0]
    assert bs % (2 * bp) == 0 and seq % WINDOW == 0
    share = bs // bp
    assert (bp * seq) % (share * CHUNK) == 0
    first_share = bp * seq // (share * CHUNK)
    last_window = pl.BlockSpec((WINDOW, PROJ_KV), lambda b: ((b + 1) * (seq // WINDOW) - 1, 0))
    new_rows = pl.BlockSpec((share * CHUNK, PROJ_KV), lambda b: (first_share + b, 0))
    one = pl.BlockSpec((1, PROJ_KV, WINDOW), lambda b: (b, 0, 0))
    many = pl.BlockSpec((share, PROJ_KV, WINDOW), lambda b: (b, 0, 0))
    return pl.pallas_call(
        _windows_kernel,
        grid=(bp,),
        in_specs=[last_window, last_window, new_rows, new_rows, many, many],
        out_specs=[one, one, many, many],
        out_shape=[jax.ShapeDtypeStruct((bp, PROJ_KV, WINDOW), F32)] * 2
        + [jax.ShapeDtypeStruct((bs, PROJ_KV, WINDOW), F32)] * 2,
        compiler_params=pltpu.CompilerParams(dimension_semantics=("arbitrary",)),
        name="windows",
    )(k, v, k, v, cache_kt, cache_vt)


def _power_table(tau, nbits, a_re, a_im):
    rows = tau.shape[0]
    w_re = jnp.ones((rows, STATE_LANES), F32)
    w_im = jnp.zeros((rows, STATE_LANES), F32)
    p_re, p_im = a_re, a_im
    for k in range(nbits):
        bit = ((tau >> k) & 1) == 1
        f_re = jnp.where(bit, p_re, 1.0)
        f_im = jnp.where(bit, p_im, 0.0)
        w_re, w_im = w_re * f_re - w_im * f_im, w_re * f_im + w_im * f_re
        p_re, p_im = p_re * p_re - p_im * p_im, 2.0 * p_re * p_im
    return w_re, w_im


def _power_rows(a_re, a_im, descending):
    t = lax.broadcasted_iota(jnp.int32, (ROW_GROUP, 1), 0)
    w_re, w_im = _power_table(ROW_GROUP - 1 - t if descending else t, 3, a_re, a_im)
    p_re, p_im = a_re, a_im
    for _ in range(3):
        p_re, p_im = p_re * p_re - p_im * p_im, 2.0 * p_re * p_im
    while w_re.shape[0] < SSM_T:
        n_re, n_im = w_re * p_re - w_im * p_im, w_re * p_im + w_im * p_re
        w_re = jnp.concatenate([n_re, w_re] if descending else [w_re, n_re], axis=0)
        w_im = jnp.concatenate([n_im, w_im] if descending else [w_im, n_im], axis=0)
        p_re, p_im = p_re * p_re - p_im * p_im, 2.0 * p_re * p_im
    return w_re, w_im


def _build_tables(par_ref, p_ref, qt_ref, coef_ref):
    lo = lax.broadcasted_iota(jnp.int32, (1, STATE_LANES), 1) < SSM_STATE
    mat = lambda i: par_ref[PAR_MATS + i * SSM_GROUP:PAR_MATS + (i + 1) * SSM_GROUP, :]
    lr, li = par_ref[0:1, :], par_ref[1:2, :]
    dt = jnp.exp(par_ref[2:3, :])
    mag = jnp.exp(lr * dt)
    a_re, a_im = mag * jnp.cos(li * dt), mag * jnp.sin(li * dt)
    nr, ni = a_re - 1.0, a_im
    den = lr * lr + li * li
    f_re, f_im = (nr * lr + ni * li) / den, (ni * lr - nr * li) / den
    b_re, b_im = mat(0), mat(1)
    bb_re = f_re * b_re - f_im * b_im
    bb_im = f_re * b_im + f_im * b_re
    c_re, c_im = mat(2), mat(3)

    w_re, w_im = _power_rows(a_re, a_im, descending=False)
    w1_re, w1_im = w_re * a_re - w_im * a_im, w_re * a_im + w_im * a_re
    wr_re, wr_im = _power_rows(a_re, a_im, descending=True)

    def outer(c, w):
        return (c[:, None, :] * w[None, :, :]).reshape(SSM_CW, STATE_LANES)

    cw_mix = (outer(c_re, jnp.where(lo, w_re, w_im)) + outer(c_im, jnp.where(lo, -w_im, w_re)))
    bb_mix = jnp.where(lo, bb_re, -bb_im)
    strip = lax.dot_general(bb_mix, cw_mix, (((1,), (1,)), ((), ())),
                            precision=lax.Precision.HIGHEST,
                            preferred_element_type=F32)

    x_a, y_a = jnp.where(lo, bb_re, bb_im), jnp.where(lo, -bb_im, bb_re)
    p_ref[...] = (outer(x_a, wr_re) + outer(y_a, wr_im)).astype(BF16)

    qt_ref[...] = (outer(c_re, jnp.where(lo, w1_re, -w1_im))
                   + outer(c_im, jnp.where(lo, -w1_im, -w1_re))).astype(BF16)

    t_re, t_im = a_re, a_im
    for _ in range(6):
        t_re, t_im = t_re * t_re - t_im * t_im, 2.0 * t_re * t_im
    for k in range(COEF_ROWS // 2):
        coef_ref[2 * k:2 * k + 1, :] = t_re
        coef_ref[2 * k + 1:2 * k + 2, :] = jnp.where(lo, -t_im, t_im)
        t_re, t_im = t_re * t_re - t_im * t_im, 2.0 * t_re * t_im
    return strip


def _toeplitz_columns(strip, m_ref, cols):
    width = cols.stop - cols.start
    s_idx = lax.broadcasted_iota(jnp.int32, (SSM_T, width), 0)
    t_idx = lax.broadcasted_iota(jnp.int32, (SSM_T, width), 1) & (SSM_T - 1)
    causal = t_idx >= s_idx
    for c in range(SSM_GROUP):
        rows = jnp.broadcast_to(strip[c:c + 1, cols], (SSM_T, width))
        shifted = pltpu.roll(rows, 0, 1, stride=1, stride_axis=0)
        m_ref[c * SSM_T:(c + 1) * SSM_T, cols] = jnp.where(causal, shifted, 0.0).astype(BF16)


PAR_MATS = 8
PAR_ROWS = PAR_MATS + 4 * SSM_GROUP


def _table_params(lam_re, lam_im, log_step, b_re, b_im, c_re, c_im):
    step = jnp.broadcast_to(log_step[:, None, None], (SSM_GROUPS, 1, SSM_STATE))
    filler = jnp.zeros((SSM_GROUPS, PAR_MATS - 3, SSM_STATE), F32)
    pack = jnp.concatenate([lam_re[:, None, :], lam_im[:, None, :], step, filler,
                            jnp.swapaxes(b_re, 1, 2), jnp.swapaxes(b_im, 1, 2), c_re, c_im], axis=1)
    return jnp.concatenate([pack, pack], axis=-1)


def _cmul(a1, a2, h, hs):
    return a1 * h + a2 * hs, a1 * hs - a2 * h


def _ssm_kernel(par_ref, u_ref, h0_ref, y_ref, hfin_ref, m_ref, p_ref, qt_ref, coef_ref,
                *, rows_per_seq, prompt_rows):
    for gi in range(SSM_GROUPS_PER_STEP):
        _ssm_group(par_ref.at[gi], u_ref, gi * SSM_GROUP, h0_ref.at[gi], y_ref, hfin_ref.at[gi],
                   m_ref.at[gi], p_ref.at[gi], qt_ref.at[gi], coef_ref.at[gi],
                   rows_per_seq=rows_per_seq, prompt_rows=prompt_rows)


def _ssm_group(par_ref, u_ref, c0, h0_ref, y_ref, hfin_ref, m_ref, p_ref, qt_ref, coef_ref,
               *, rows_per_seq, prompt_rows):
    strip = _build_tables(par_ref, p_ref, qt_ref, coef_ref)
    lo = lax.broadcasted_iota(jnp.int32, (1, ROW_TOKENS), 1) < SSM_T

    def chunk_rows(ref):
        even, odd = [], []
        rows = ref.shape[0] * ROW_GROUP
        for k in range(SSM_GROUP // 2):
            a = ref[:, c0 + 2 * k].reshape(rows, ROW_TOKENS)
            b = ref[:, c0 + 2 * k + 1].reshape(rows, ROW_TOKENS)
            even.append(jnp.where(lo, a, pltpu.roll(b, SSM_T, 1)))
            odd.append(jnp.where(lo, pltpu.roll(a, SSM_T, 1), b))
        return jnp.concatenate(even, axis=1), jnp.concatenate(odd, axis=1)

    def store_rows(y_even, y_odd, ref, first_channel):
        for k in range(y_even.shape[1] // ROW_TOKENS):
            te = y_even[:, k * ROW_TOKENS:(k + 1) * ROW_TOKENS]
            to = y_odd[:, k * ROW_TOKENS:(k + 1) * ROW_TOKENS]
            tiles = (ref.shape[0], ROW_GROUP, ROW_TOKENS)
            c = c0 + first_channel + 2 * k
            ref[:, c] = jnp.where(lo, te, pltpu.roll(to, SSM_T, 1)).reshape(tiles)
            ref[:, c + 1] = jnp.where(lo, pltpu.roll(te, SSM_T, 1), to).reshape(tiles)

    even, odd = chunk_rows(u_ref)
    rp, rs = prompt_rows, even.shape[0] - prompt_rows
    u = jnp.concatenate([even[:rp], odd[:rp], even[rp:], odd[rp:]], axis=0).astype(BF16)
    s1 = jnp.dot(u, p_ref[...], preferred_element_type=F32)
    s2 = pltpu.roll(s1, SSM_STATE, 1)
    a1, a2 = coef_ref[0:1, :], coef_ref[1:2, :]

    e1, e2, o1, o2 = s1[:rp], s2[:rp], s1[rp:2 * rp], s2[rp:2 * rp]
    x1, x2 = _cmul(a1, a2, e1, e2)
    x1, x2 = x1 + o1, x2 + o2
    pos = lax.broadcasted_iota(jnp.int32, (rp, STATE_LANES), 0) & (rows_per_seq - 1)
    for k in range(rows_per_seq.bit_length() - 1):
        d = 1 << k
        b1, b2 = coef_ref[2 + 2 * k:3 + 2 * k, :], coef_ref[3 + 2 * k:4 + 2 * k, :]
        sh1 = jnp.where(pos >= d, pltpu.roll(x1, d, 0), 0.0)
        sh2 = jnp.where(pos >= d, pltpu.roll(x2, d, 0), 0.0)
        y1, y2 = _cmul(b1, b2, sh1, sh2)
        x1, x2 = x1 + y1, x2 + y2
    g1 = jnp.where(pos >= 1, pltpu.roll(x1, 1, 0), 0.0)
    g2 = jnp.where(pos >= 1, pltpu.roll(x2, 1, 0), 0.0)
    ho1 = _cmul(a1, a2, g1, g2)[0] + e1
    hfin_ref[0:STATE_OUT_ROWS, :] = jnp.zeros((STATE_OUT_ROWS, STATE_LANES), F32)
    for b in range(rp // rows_per_seq):
        last = (b + 1) * rows_per_seq - 1
        hfin_ref[b:b + 1, :] = x1[last:last + 1, :]

    h0e, h0o = h0_ref[0], h0_ref[1]
    swap = lambda h: pltpu.roll(h, SSM_STATE, 1)
    hfin_ref[STATE_OUT_ROWS:STATE_OUT_ROWS + rs, :] = a1 * h0e + a2 * swap(h0e) + s1[2 * rp:2 * rp + rs]
    hfin_ref[STATE_OUT_ROWS + rs:, :] = a1 * h0o + a2 * swap(h0o) + s1[2 * rp + rs:]

    hprev = jnp.concatenate([g1, ho1, h0e, h0o], axis=0).astype(BF16)
    blocks = [slice(c * SSM_T, (c + M_BLOCK_CHANNELS) * SSM_T) for c in range(0, SSM_GROUP, M_BLOCK_CHANNELS)]
    _toeplitz_columns(strip, m_ref, blocks[0])
    for i, cols in enumerate(blocks):
        y = lax.dot_general(hprev, qt_ref[cols, :], (((1,), (1,)), ((), ())), preferred_element_type=F32)
        y += jnp.dot(u, m_ref[:, cols], preferred_element_type=F32)
        if i + 1 < len(blocks):
            _toeplitz_columns(strip, m_ref, blocks[i + 1])
        store_rows(jnp.concatenate([y[:rp], y[2 * rp:2 * rp + rs]], axis=0),
                   jnp.concatenate([y[rp:2 * rp], y[2 * rp + rs:]], axis=0), y_ref, i * M_BLOCK_CHANNELS)


def _ssm(table_params, u3, h0_pack, *, rows_per_seq, prompt_rows):
    groups = u3.shape[0]
    rs = groups * ROW_GROUP - prompt_rows
    u4 = u3.reshape(groups, SSM_WIDTH, ROW_GROUP, ROW_TOKENS)
    assert rows_per_seq & (rows_per_seq - 1) == 0 and 2 * rows_per_seq.bit_length() <= COEF_ROWS
    gps = SSM_GROUPS_PER_STEP
    g3 = lambda g: (g, 0, 0)
    u_spec = pl.BlockSpec((groups, gps * SSM_GROUP, ROW_GROUP, ROW_TOKENS), lambda g: (0, g, 0, 0))
    fin_rows = STATE_OUT_ROWS + 2 * rs
    return pl.pallas_call(
        functools.partial(_ssm_kernel, rows_per_seq=rows_per_seq, prompt_rows=prompt_rows),
        grid=(SSM_GROUPS // gps,),
        in_specs=[
            pl.BlockSpec((gps, PAR_ROWS, STATE_LANES), g3),
            u_spec,
            pl.BlockSpec((gps, 2, rs, STATE_LANES), lambda g: (g, 0, 0, 0)),
        ],
        out_specs=[
            u_spec,
            pl.BlockSpec((gps, fin_rows, STATE_LANES), g3),
        ],
        out_shape=[
            jax.ShapeDtypeStruct(u4.shape, F32),
            jax.ShapeDtypeStruct((SSM_GROUPS, fin_rows, STATE_LANES), F32),
        ],
        scratch_shapes=[
            pltpu.VMEM((gps, SSM_CW, SSM_CW), BF16),
            pltpu.VMEM((gps, SSM_CW, STATE_LANES), BF16),
            pltpu.VMEM((gps, SSM_CW, STATE_LANES), BF16),
            pltpu.VMEM((gps, COEF_ROWS, STATE_LANES), F32),
        ],
        compiler_params=pltpu.CompilerParams(
            dimension_semantics=("arbitrary",), vmem_limit_bytes=V7X_VMEM_LIMIT_BYTES),
        name="ssm",
    )(table_params, u4, h0_pack)


def _post_kernel(xp_ref, xs_ref, atp_ref, ats_ref, ut_ref, yt_ref, *refs, alpha, prompt_tiles):
    weights, (op_ref, os_ref) = refs[:-2], refs[-2:]

    @pl.when(pl.program_id(0) < prompt_tiles)
    def _():
        _post_tile(xp_ref, atp_ref, ut_ref, yt_ref, *weights, op_ref, alpha=alpha)

    @pl.when(pl.program_id(0) >= prompt_tiles)
    def _():
        _post_tile(xs_ref, ats_ref, ut_ref, yt_ref, *weights, os_ref, alpha=alpha)


def _post_tile(x_ref, at_ref, ut_ref, yt_ref, lng_ref, lnb_ref, cols_ref, wglut_ref,
               wout_ref, ln1g_ref, ln1b_ref, wgu_ref, wdown_ref, ln2g_ref, ln2b_ref, o_ref,
               *, alpha):
    xn = _layer_norm(x_ref[...], lng_ref[...], lnb_ref[...])
    n_rows = x_ref.shape[0] // ROW_TOKENS
    first_row = (pl.program_id(0) % (ROW_GROUP // n_rows)) * n_rows

    def feature_major(ref):
        return jnp.concatenate(
            [ref[pl.ds(first_row + j, SSM_WIDTH, stride=ROW_GROUP), :] for j in range(n_rows)], axis=1)

    ys = feature_major(yt_ref) + cols_ref[0] * feature_major(ut_ref)
    gl = 0.5 * ys * (1.0 + lax.erf(ys * math.sqrt(0.5)))
    z = jnp.dot(wglut_ref[...], gl.astype(BF16), preferred_element_type=F32) + cols_ref[1]
    s = gl * jax.nn.sigmoid(z)
    tn = (((0,), (0,)), ((), ()))
    mix = lax.dot_general(at_ref[...], wout_ref[:PROJ_Q, :], tn, preferred_element_type=F32)
    mix += lax.dot_general(s.astype(BF16), wout_ref[PROJ_Q:, :], tn, preferred_element_type=F32)
    h = _layer_norm(alpha * xn + mix, ln1g_ref[...], ln1b_ref[...])
    hb = h.astype(BF16)
    f = jnp.zeros_like(h)
    for start in range(0, D_FF, FF_BLOCK):
        stop = min(start + FF_BLOCK, D_FF)
        cols = slice(start, stop)
        up_cols = slice(D_FF + start, D_FF + stop)
        g = jnp.dot(hb, wgu_ref[:, cols], preferred_element_type=F32)
        up = jnp.dot(hb, wgu_ref[:, up_cols], preferred_element_type=F32)
        act = (g * jax.nn.sigmoid(g)) * up
        f += jnp.dot(act.astype(BF16), wdown_ref[cols, :], preferred_element_type=F32)
    o_ref[...] = _layer_norm(alpha * h + f, ln2g_ref[...], ln2b_ref[...])


def _post(xp2d, xs2d, atp, ats, ut3, yt3, ln_g, ln_b, d_bglu_cols, w_glu_t, w_out, ln1_g, ln1_b,
          w_gate_up, w_down, ln2_g, ln2_b, *, alpha, tm):
    n_p, n_s = xp2d.shape[0], xs2d.shape[0]
    tp, ts = n_p // tm, n_s // tm
    assert PROJ_TILE % tm == 0 and n_p % PROJ_TILE == 0
    group = lambda i: (i // (PROJ_TILE // tm), 0, 0)
    group_spec = pl.BlockSpec((None, SSM_WIDTH * ROW_GROUP, ROW_TOKENS), group)
    prompt = lambda i: jnp.minimum(i, tp - 1)
    sample = lambda i: jnp.maximum(i - tp, 0)
    const = lambda i: (0, 0)
    resident = lambda shape: pl.BlockSpec(shape, const, pipeline_mode=pl.Buffered(1))
    vec = lambda width: pl.BlockSpec((1, width), const)
    columns = pl.BlockSpec((2, SSM_WIDTH, 1), lambda i: (0, 0, 0))
    return pl.pallas_call(
        functools.partial(_post_kernel, alpha=alpha, prompt_tiles=tp),
        grid=(tp + ts,),
        in_specs=[
            pl.BlockSpec((tm, D_MODEL), lambda i: (prompt(i), 0)),
            pl.BlockSpec((tm, D_MODEL), lambda i: (sample(i), 0)),
            pl.BlockSpec((PROJ_Q, tm), lambda i: (0, prompt(i))),
            pl.BlockSpec((PROJ_Q, tm), lambda i: (0, sample(i))),
            group_spec, group_spec,
            vec(D_MODEL), vec(D_MODEL), columns,
            resident((SSM_WIDTH, SSM_WIDTH)),
            resident((D_MODEL, D_MODEL)), vec(D_MODEL), vec(D_MODEL),
            resident((D_MODEL, 2 * D_FF)), resident((D_FF, D_MODEL)),
            vec(D_MODEL), vec(D_MODEL),
        ],
        out_specs=[
            pl.BlockSpec((tm, D_MODEL), lambda i: (prompt(i), 0)),
            pl.BlockSpec((tm, D_MODEL), lambda i: (sample(i), 0)),
        ],
        out_shape=[
            jax.ShapeDtypeStruct((n_p, D_MODEL), F32),
            jax.ShapeDtypeStruct((n_s, D_MODEL), F32),
        ],
        compiler_params=pltpu.CompilerParams(
            dimension_semantics=("arbitrary",), vmem_limit_bytes=V7X_VMEM_LIMIT_BYTES),
        name="post",
    )(xp2d, xs2d, atp, ats, ut3, yt3, ln_g, ln_b, d_bglu_cols, w_glu_t, w_out, ln1_g, ln1_b,
      w_gate_up, w_down, ln2_g, ln2_b)


def kernel(x_prompt, x_sample, cache_win_k, cache_win_v, state_ssm_re, state_ssm_im,
           ln_in_g, ln_in_b, w_in, attn_sinks, ssm_lambda_re, ssm_lambda_im, ssm_log_step,
           ssm_b_re, ssm_b_im, ssm_c_re, ssm_c_im, ssm_d, w_glu, b_glu, w_out,
           ln1_g, ln1_b, w_gate_up, w_down, ln2_g, ln2_b):
    depth = w_in.shape[0]
    assert depth == 1, "single-layer step"
    bp, lp, _ = x_prompt.shape
    bs, ls, _ = x_sample.shape
    assert ls == SSM_T and bs % 2 == 0 and lp % 512 == 0 and bp <= STATE_OUT_ROWS
    win_rows = cache_win_k.shape[2]
    assert win_rows == WINDOW
    alpha = (2.0 * depth) ** 0.25
    l = 0
    row = lambda a: a.reshape(1, -1)

    lng, lnb = row(ln_in_g), row(ln_in_b)
    xp2 = x_prompt.reshape(bp * lp, D_MODEL)
    xs2 = x_sample.reshape(bs * ls, D_MODEL)
    n_p = bp * lp
    qt, vt, u3, k, v = _proj(xp2, xs2, lng, lnb, w_in[l][:, PROJ_Q:PROJ_Q + KV].astype(BF16),
                             w_in[l].T.astype(BF16))

    atp, w_gu_b, w_down_b, w_out_b = _attention_prompt(
        attn_sinks, qt, k, vt, bp, lp, n_pairs=ATTN_PAIRS, f32_weights=(w_gate_up[l], w_down[l], w_out[l]))
    ck = cache_win_k[l].reshape(bs, win_rows, PROJ_KV)
    ckt = jnp.swapaxes(ck, 1, 2)
    cvt = jnp.swapaxes(cache_win_v[l].reshape(bs, win_rows, PROJ_KV), 1, 2)
    ats = _attention_sample(attn_sinks, qt, ck, k, cvt, vt, first_token=n_p, n_pairs=ATTN_PAIRS)
    windows = _windows(k, v, ckt, cvt, bp, lp)

    table_params = _table_params(
        ssm_lambda_re[l], ssm_lambda_im[l], ssm_log_step[l],
        ssm_b_re[l], ssm_b_im[l], ssm_c_re[l], ssm_c_im[l])
    h0 = jnp.concatenate([state_ssm_re[l], state_ssm_im[l]], axis=-1)
    h0_pack = jnp.transpose(h0.reshape(bs // 2, 2, SSM_GROUPS, STATE_LANES), (2, 1, 0, 3))
    y4, hfin = _ssm(table_params, u3, h0_pack,
                    rows_per_seq=lp // ROW_TOKENS, prompt_rows=n_p // ROW_TOKENS)

    post_args = (lng, lnb, jnp.stack([ssm_d[l], b_glu[l]])[:, :, None], w_glu[l].T.astype(BF16),
                 w_out_b, row(ln1_g[l]), row(ln1_b[l]), w_gu_b, w_down_b, row(ln2_g[l]), row(ln2_b[l]))
    out_p, out_s = _post(xp2, xs2, atp, ats, u3, y4.reshape(u3.shape), *post_args, alpha=alpha, tm=512)

    win_k_p, win_v_p, win_k_s, win_v_s = (
        jnp.swapaxes(w, 1, 2).reshape(1, -1, win_rows, N_KV_HEADS, HEAD_DIM) for w in windows)
    sp = jnp.swapaxes(hfin[:, :bp], 0, 1)
    ss = jnp.transpose(hfin[:, STATE_OUT_ROWS:].reshape(SSM_GROUPS, 2, bs // 2, STATE_LANES),
                       (2, 1, 0, 3)).reshape(bs, SSM_GROUPS, STATE_LANES)
    return (out_p.reshape(bp, lp, D_MODEL), out_s.reshape(bs, ls, D_MODEL),
            win_k_p, win_v_p, sp[None, ..., :SSM_STATE], sp[None, ..., SSM_STATE:],
            win_k_s, win_v_s, ss[None, ..., :SSM_STATE], ss[None, ..., SSM_STATE:])
```

```python
import functools
import math

import jax
import jax.numpy as jnp
from jax import lax
from jax.experimental import pallas as pl
from jax.experimental.pallas import tpu as pltpu

F32 = jnp.float32
BF16 = jnp.bfloat16

D_MODEL = 1024
HEAD_DIM = 64
N_HEADS = 8
N_KV_HEADS = 2
Q_PER_KV = N_HEADS // N_KV_HEADS
CHUNK = 64
WINDOW = 128
WIN_CHUNKS = WINDOW // CHUNK
BAND = (WIN_CHUNKS + 1) * CHUNK
PROJ_Q = N_HEADS * HEAD_DIM
PROJ_KV = N_KV_HEADS * HEAD_DIM
SSM_WIDTH = 512
SSM_GROUP = 16
SSM_GROUPS = SSM_WIDTH // SSM_GROUP
SSM_STATE = 64
STATE_LANES = 2 * SSM_STATE
D_FF = 2816
D_IN_PROJ = PROJ_Q + 2 * PROJ_KV + SSM_WIDTH
LN_EPS = 1e-5
NEG_INF = -1e30

SSM_T = CHUNK
SSM_CW = SSM_T * SSM_GROUP
ROW_TOKENS = 2 * SSM_T
ROW_GROUP = 8
PROJ_TILE = ROW_GROUP * ROW_TOKENS
PROJ_PARTS = 2
COEF_ROWS = 16
STATE_OUT_ROWS = 8
M_BLOCK_CHANNELS = 4
SSM_GROUPS_PER_STEP = 2
FF_BLOCK = 256
V7X_VMEM_LIMIT_BYTES = 56 * 1024 * 1024


def _layer_norm(x, g, b):
    mu = jnp.mean(x, axis=-1, keepdims=True)
    xc = x - mu
    var = jnp.mean(xc * xc, axis=-1, keepdims=True)
    return xc * lax.rsqrt(var + LN_EPS) * g + b


KV = 2 * PROJ_KV
VU = PROJ_KV + SSM_WIDTH


def _proj_kernel(x0_ref, xp_second_ref, xp_next_ref, xs_second_ref, xs_next_ref, g_ref, b_ref,
                 w_ref, wqt_ref, wvut_ref, qt_ref, vt_ref, ut_ref, k_ref, v_ref, first_ref, *, prompt_tiles):
    rest = (g_ref, b_ref, w_ref, wqt_ref, wvut_ref, qt_ref, vt_ref, ut_ref, k_ref, v_ref, first_ref)
    i = pl.program_id(0)

    @pl.when(i == 0)
    def _():
        first_ref[...] = _layer_norm(x0_ref[...], g_ref[...], b_ref[...]).astype(BF16)

    @pl.when(i < prompt_tiles - 1)
    def _():
        _proj_tile(xp_second_ref, xp_next_ref, *rest)

    @pl.when(i == prompt_tiles - 1)
    def _():
        _proj_tile(xp_second_ref, xs_next_ref, *rest)

    @pl.when(i >= prompt_tiles)
    def _():
        _proj_tile(xs_second_ref, xs_next_ref, *rest)


def _proj_tile(second_ref, next_ref, g_ref, b_ref, w_ref, wqt_ref, wvut_ref, qt_ref, vt_ref, ut_ref, k_ref, v_ref,
               first_ref):
    rows_per_part = ROW_GROUP // PROJ_PARTS
    nt = (((1,), (1,)), ((), ()))
    for part in range(PROJ_PARTS):
        tok = slice(part * rows_per_part * ROW_TOKENS, (part + 1) * rows_per_part * ROW_TOKENS)
        if part == 0:
            xb = first_ref[...]
        else:
            xb = _layer_norm(second_ref[...], g_ref[...], b_ref[...]).astype(BF16)
        p = jnp.dot(xb, w_ref[...], preferred_element_type=F32)
        k_ref[tok, :] = p[:, :PROJ_KV]
        v_ref[tok, :] = p[:, PROJ_KV:]
        qt = lax.dot_general(wqt_ref[...], xb, nt, preferred_element_type=F32)
        vut = lax.dot_general(wvut_ref[...], xb, nt, preferred_element_type=F32)
        qt_ref[:, tok] = (qt * (HEAD_DIM ** -0.5)).astype(BF16)
        vt_ref[:, tok] = vut[:PROJ_KV].astype(BF16)
        for j in range(rows_per_part):
            ut_ref[pl.ds(part * rows_per_part + j, SSM_WIDTH, stride=ROW_GROUP), :] = (
                vut[PROJ_KV:, j * ROW_TOKENS:(j + 1) * ROW_TOKENS])
    first_ref[...] = _layer_norm(next_ref[...], g_ref[...], b_ref[...]).astype(BF16)


def _proj(xp2d, xs2d, ln_g, ln_b, w_kv, w_in_t):
    assert D_IN_PROJ == 2 * VU and PROJ_PARTS == 2
    tm = PROJ_TILE
    tp, ts = xp2d.shape[0] // tm, xs2d.shape[0] // tm
    n = (tp + ts) * tm
    const = lambda i: (0, 0)
    row = lambda i: (i, 0)
    col = lambda i: (0, i)
    half = lambda index: pl.BlockSpec((tm // 2, D_MODEL), lambda i: (index(i), 0))
    return pl.pallas_call(
        functools.partial(_proj_kernel, prompt_tiles=tp),
        grid=(tp + ts,),
        in_specs=[
            half(lambda i: 0),
            half(lambda i: 2 * jnp.minimum(i, tp - 1) + 1),
            half(lambda i: 2 * jnp.minimum(i + 1, tp - 1)),
            half(lambda i: 2 * jnp.maximum(i - tp, 0) + 1),
            half(lambda i: 2 * jnp.clip(i + 1 - tp, 0, ts - 1)),
            pl.BlockSpec((1, D_MODEL), const),
            pl.BlockSpec((1, D_MODEL), const),
            pl.BlockSpec((D_MODEL, KV), const),
            pl.BlockSpec((PROJ_Q, D_MODEL), const),
            pl.BlockSpec((VU, D_MODEL), lambda i: (1, 0)),
        ],
        out_specs=[
            pl.BlockSpec((PROJ_Q, tm), col),
            pl.BlockSpec((PROJ_KV, tm), col),
            pl.BlockSpec((None, SSM_WIDTH * ROW_GROUP, ROW_TOKENS), lambda i: (i, 0, 0)),
            pl.BlockSpec((tm, PROJ_KV), row),
            pl.BlockSpec((tm, PROJ_KV), row),
        ],
        out_shape=[
            jax.ShapeDtypeStruct((PROJ_Q, n), BF16),
            jax.ShapeDtypeStruct((PROJ_KV, n), BF16),
            jax.ShapeDtypeStruct((n // tm, SSM_WIDTH * ROW_GROUP, ROW_TOKENS), F32),
            jax.ShapeDtypeStruct((n, PROJ_KV), F32),
            jax.ShapeDtypeStruct((n, PROJ_KV), F32),
        ],
        scratch_shapes=[pltpu.VMEM((tm // 2, D_MODEL), BF16)],
        compiler_params=pltpu.CompilerParams(
            dimension_semantics=("arbitrary",), vmem_limit_bytes=V7X_VMEM_LIMIT_BYTES),
        name="proj",
    )(xp2d, xp2d, xp2d, xs2d, xs2d, ln_g, ln_b, w_kv, w_in_t, w_in_t)


PAIR = 2 * CHUNK
HEAD_LANES = Q_PER_KV * PAIR
ONES_ROWS = 16
ATTN_PAIRS = 16
BF16_TILE_ROWS = 16


def _scores(unit, qt_ref):
    kwin, _, lanes, _, _, h = unit
    base = h * Q_PER_KV * HEAD_DIM
    qrow = jnp.concatenate(
        [qt_ref[base + g * HEAD_DIM:base + (g + 1) * HEAD_DIM, lanes] for g in range(Q_PER_KV)],
        axis=1)
    zero = jnp.zeros_like(qrow)
    qstack = jnp.concatenate([qrow, zero] if h == 0 else [zero, qrow], axis=0)
    return jnp.dot(kwin, qstack, preferred_element_type=F32)


def _sink_rows(sinks_ref):
    tile = lax.broadcasted_iota(jnp.int32, (1, HEAD_LANES), 1) // PAIR
    rows = []
    for h in range(N_KV_HEADS):
        row = jnp.full((1, HEAD_LANES), sinks_ref[0, h * Q_PER_KV], F32)
        for g in range(1, Q_PER_KV):
            row = jnp.where(tile == g, sinks_ref[0, h * Q_PER_KV + g], row)
        rows.append(row)
    return rows


def _finish(unit, s, sinks, o_ref):
    _, vtwin, lanes, valid, masked_rows, h = unit
    nk = s.shape[0]
    base = h * Q_PER_KV * HEAD_DIM
    pieces, done = [], 0
    for start, stop in masked_rows:
        if start > done:
            pieces.append(s[done:start])
        pieces.append(jnp.where(valid[start:stop], s[start:stop], NEG_INF))
        done = stop
    if done < nk:
        pieces.append(s[done:])
    s = jnp.concatenate(pieces, axis=0)
    sink = sinks[h]
    m = jnp.maximum(jnp.max(s, axis=0, keepdims=True), sink)
    p = jnp.exp(s - m).astype(BF16)
    v_ones = jnp.concatenate(
        [vtwin[h * HEAD_DIM:(h + 1) * HEAD_DIM, :], jnp.ones((ONES_ROWS, nk), BF16)], axis=0)
    ov = jnp.dot(v_ones, p, preferred_element_type=F32)
    den = ov[HEAD_DIM:HEAD_DIM + 1, :] + jnp.exp(sink - m)
    o = ov[:HEAD_DIM, :] * (1.0 / den)
    for g in range(Q_PER_KV):
        o_ref[base + g * HEAD_DIM:base + (g + 1) * HEAD_DIM, lanes] = (
            o[:, g * PAIR:(g + 1) * PAIR].astype(BF16))


def _attend_units(units, qt_ref, sinks_ref, o_ref):
    sinks = _sink_rows(sinks_ref)
    s_next = _scores(units[0], qt_ref)
    for i, unit in enumerate(units):
        s = s_next
        if i + 1 < len(units):
            s_next = _scores(units[i + 1], qt_ref)
        _finish(unit, s, sinks, o_ref)


def _attn_prompt_kernel(sinks_ref, qt_ref, kp_ref, kc_ref, vtp_ref, vtc_ref, *refs, n_pairs):
    n_cast = (len(refs) - 1) // 2
    o_ref = refs[n_cast]
    for w_ref, wb_ref in zip(refs[:n_cast], refs[n_cast + 1:]):
        wb_ref[...] = w_ref[...].astype(BF16)
    tq = n_pairs * PAIR
    nk = WINDOW + PAIR
    kk = jnp.concatenate([kp_ref[...], kc_ref[...]], axis=0).astype(BF16)
    vt = jnp.concatenate([vtp_ref[...], vtc_ref[...]], axis=1)
    r = lax.broadcasted_iota(jnp.int32, (nk, HEAD_LANES), 0)
    first_chunk = (lax.broadcasted_iota(jnp.int32, (nk, HEAD_LANES), 1) & (PAIR - 1)) < CHUNK
    lo = jnp.where(first_chunk, 0, CHUNK)
    hi = jnp.where(first_chunk, BAND, nk)
    first_pos = pl.program_id(1) * tq - WINDOW
    units = []
    for pp in range(n_pairs):
        lo_pp = jnp.maximum(lo, -first_pos) if pp == 0 else lo
        valid = (r >= lo_pp) & (r < hi)
        masked_rows = ((0, nk),) if pp == 0 else ((0, CHUNK), (nk - CHUNK, nk))
        for h in range(N_KV_HEADS):
            units.append((kk[pp * PAIR:pp * PAIR + nk], vt[:, pp * PAIR:pp * PAIR + nk],
                          slice(pp * PAIR, (pp + 1) * PAIR), valid, masked_rows, h))
    _attend_units(units, qt_ref, sinks_ref, o_ref)


def _attn_sample_kernel(sinks_ref, qt_ref, ck_ref, kn_ref, cvt_ref, vtn_ref, o_ref, *, n_pairs):
    nk = 2 * WINDOW + PAIR
    r = lax.broadcasted_iota(jnp.int32, (nk, HEAD_LANES), 0)
    query_seq = (lax.broadcasted_iota(jnp.int32, (nk, HEAD_LANES), 1) & (PAIR - 1)) >> 6
    key_seq = jnp.where(r < 2 * WINDOW, r >> 7, (r - 2 * WINDOW) >> 6)
    valid = query_seq == key_seq
    units = []
    for pp in range(n_pairs):
        lanes = slice(pp * PAIR, (pp + 1) * PAIR)
        kwin = jnp.concatenate([ck_ref[2 * pp], ck_ref[2 * pp + 1], kn_ref[lanes, :]],
                               axis=0).astype(BF16)
        vtwin = jnp.concatenate([cvt_ref[2 * pp].astype(BF16), cvt_ref[2 * pp + 1].astype(BF16),
                                 vtn_ref[:, lanes]], axis=1)
        units.extend((kwin, vtwin, lanes, valid, ((0, nk),), h) for h in range(N_KV_HEADS))
    _attend_units(units, qt_ref, sinks_ref, o_ref)


_SINK_SPEC = pl.BlockSpec(memory_space=pltpu.SMEM)


def _attention_prompt(sinks, qt, k, vt, bsz, seq, n_pairs, f32_weights):
    tq = n_pairs * PAIR
    nt = seq // tq
    steps = bsz * nt
    wpt = tq // WINDOW
    cur_c = lambda b, i: (0, b * nt + i)
    cur_r = lambda b, i: (b * nt + i, 0)
    prev = lambda b, i: jnp.maximum((b * nt + i) * wpt - 1, 0)
    assert all(w.shape[0] % (steps * BF16_TILE_ROWS) == 0 for w in f32_weights)
    cast_specs = [pl.BlockSpec((w.shape[0] // steps, w.shape[1]), cur_r) for w in f32_weights]
    return pl.pallas_call(
        functools.partial(_attn_prompt_kernel, n_pairs=n_pairs),
        grid=(bsz, nt),
        in_specs=[
            _SINK_SPEC,
            pl.BlockSpec((PROJ_Q, tq), cur_c),
            pl.BlockSpec((WINDOW, PROJ_KV), lambda b, i: (prev(b, i), 0)),
            pl.BlockSpec((tq, PROJ_KV), cur_r),
            pl.BlockSpec((PROJ_KV, WINDOW), lambda b, i: (0, prev(b, i))),
            pl.BlockSpec((PROJ_KV, tq), cur_c),
            *cast_specs,
        ],
        out_specs=[pl.BlockSpec((PROJ_Q, tq), cur_c), *cast_specs],
        out_shape=[jax.ShapeDtypeStruct((PROJ_Q, bsz * seq), BF16),
                   *[jax.ShapeDtypeStruct(w.shape, BF16) for w in f32_weights]],
        compiler_params=pltpu.CompilerParams(
            dimension_semantics=("arbitrary", "arbitrary"), vmem_limit_bytes=V7X_VMEM_LIMIT_BYTES),
        name="attn_prompt",
    )(sinks, qt, k, k, vt, vt, *f32_weights)


def _attention_sample(sinks, qt, cache_k, k, cache_vt, vt, first_token, n_pairs):
    n = cache_k.shape[0] * CHUNK
    tq = n_pairs * PAIR
    assert first_token % tq == 0
    off = first_token // tq
    return pl.pallas_call(
        functools.partial(_attn_sample_kernel, n_pairs=n_pairs),
        grid=(n // tq,),
        in_specs=[
            _SINK_SPEC,
            pl.BlockSpec((PROJ_Q, tq), lambda i: (0, i + off)),
            pl.BlockSpec((2 * n_pairs, WINDOW, PROJ_KV), lambda i: (i, 0, 0)),
            pl.BlockSpec((tq, PROJ_KV), lambda i: (i + off, 0)),
            pl.BlockSpec((2 * n_pairs, PROJ_KV, WINDOW), lambda i: (i, 0, 0)),
            pl.BlockSpec((PROJ_KV, tq), lambda i: (0, i + off)),
        ],
        out_specs=pl.BlockSpec((PROJ_Q, tq), lambda i: (0, i)),
        out_shape=jax.ShapeDtypeStruct((PROJ_Q, n), BF16),
        compiler_params=pltpu.CompilerParams(dimension_semantics=("arbitrary",)),
        name="attn_sample",
    )(sinks, qt, cache_k, k, cache_vt, vt)


def _windows_kernel(kp_ref, vp_ref, kn_ref, vn_ref, ckt_ref, cvt_ref, wkp_ref, wvp_ref, wks_ref, wvs_ref):
    first_half = lax.broadcasted_iota(jnp.int32, (PROJ_KV, WINDOW), 1) < CHUNK
    for new_p, new_s, cache_t, win_p, win_s in ((kp_ref, kn_ref, ckt_ref, wkp_ref, wks_ref),
                                                (vp_ref, vn_ref, cvt_ref, wvp_ref, wvs_ref)):
        win_p[0] = new_p[...].T
        for pair in range(cache_t.shape[0] // 2):
            new_t = new_s[pair * PAIR:(pair + 1) * PAIR, :].T
            swapped = pltpu.roll(new_t, CHUNK, 1)
            for j, new in enumerate((swapped, new_t)):
                old = pltpu.roll(cache_t[2 * pair + j], CHUNK, 1)
                win_s[2 * pair + j] = jnp.where(first_half, old, new)


def _windows(k, v, cache_kt, cache_vt, bp, seq):
    bs = cache_kt.shape[0]
    assert bs % (2 * bp) == 0 and seq % WINDOW == 0
    share = bs // bp
    assert (bp * seq) % (share * CHUNK) == 0
    first_share = bp * seq // (share * CHUNK)
    last_window = pl.BlockSpec((WINDOW, PROJ_KV), lambda b: ((b + 1) * (seq // WINDOW) - 1, 0))
    new_rows = pl.BlockSpec((share * CHUNK, PROJ_KV), lambda b: (first_share + b, 0))
    one = pl.BlockSpec((1, PROJ_KV, WINDOW), lambda b: (b, 0, 0))
    many = pl.BlockSpec((share, PROJ_KV, WINDOW), lambda b: (b, 0, 0))
    return pl.pallas_call(
        _windows_kernel,
        grid=(bp,),
        in_specs=[last_window, last_window, new_rows, new_rows, many, many],
        out_specs=[one, one, many, many],
        out_shape=[jax.ShapeDtypeStruct((bp, PROJ_KV, WINDOW), F32)] * 2
        + [jax.ShapeDtypeStruct((bs, PROJ_KV, WINDOW), F32)] * 2,
        compiler_params=pltpu.CompilerParams(dimension_semantics=("arbitrary",)),
        name="windows",
    )(k, v, k, v, cache_kt, cache_vt)


def _power_table(tau, nbits, a_re, a_im):
    rows = tau.shape[0]
    w_re = jnp.ones((rows, STATE_LANES), F32)
    w_im = jnp.zeros((rows, STATE_LANES), F32)
    p_re, p_im = a_re, a_im
    for k in range(nbits):
        bit = ((tau >> k) & 1) == 1
        f_re = jnp.where(bit, p_re, 1.0)
        f_im = jnp.where(bit, p_im, 0.0)
        w_re, w_im = w_re * f_re - w_im * f_im, w_re * f_im + w_im * f_re
        p_re, p_im = p_re * p_re - p_im * p_im, 2.0 * p_re * p_im
    return w_re, w_im


def _power_rows(a_re, a_im, descending):
    t = lax.broadcasted_iota(jnp.int32, (ROW_GROUP, 1), 0)
    w_re, w_im = _power_table(ROW_GROUP - 1 - t if descending else t, 3, a_re, a_im)
    p_re, p_im = a_re, a_im
    for _ in range(3):
        p_re, p_im = p_re * p_re - p_im * p_im, 2.0 * p_re * p_im
    while w_re.shape[0] < SSM_T:
        n_re, n_im = w_re * p_re - w_im * p_im, w_re * p_im + w_im * p_re
        w_re = jnp.concatenate([n_re, w_re] if descending else [w_re, n_re], axis=0)
        w_im = jnp.concatenate([n_im, w_im] if descending else [w_im, n_im], axis=0)
        p_re, p_im = p_re * p_re - p_im * p_im, 2.0 * p_re * p_im
    return w_re, w_im


def _build_tables(par_ref, p_ref, qt_ref, coef_ref):
    lo = lax.broadcasted_iota(jnp.int32, (1, STATE_LANES), 1) < SSM_STATE
    mat = lambda i: par_ref[PAR_MATS + i * SSM_GROUP:PAR_MATS + (i + 1) * SSM_GROUP, :]
    lr, li = par_ref[0:1, :], par_ref[1:2, :]
    dt = jnp.exp(par_ref[2:3, :])
    mag = jnp.exp(lr * dt)
    a_re, a_im = mag * jnp.cos(li * dt), mag * jnp.sin(li * dt)
    nr, ni = a_re - 1.0, a_im
    den = lr * lr + li * li
    f_re, f_im = (nr * lr + ni * li) / den, (ni * lr - nr * li) / den
    b_re, b_im = mat(0), mat(1)
    bb_re = f_re * b_re - f_im * b_im
    bb_im = f_re * b_im + f_im * b_re
    c_re, c_im = mat(2), mat(3)

    w_re, w_im = _power_rows(a_re, a_im, descending=False)
    w1_re, w1_im = w_re * a_re - w_im * a_im, w_re * a_im + w_im * a_re
    wr_re, wr_im = _power_rows(a_re, a_im, descending=True)

    def outer(c, w):
        return (c[:, None, :] * w[None, :, :]).reshape(SSM_CW, STATE_LANES)

    cw_mix = (outer(c_re, jnp.where(lo, w_re, w_im)) + outer(c_im, jnp.where(lo, -w_im, w_re)))
    bb_mix = jnp.where(lo, bb_re, -bb_im)
    strip = lax.dot_general(bb_mix, cw_mix, (((1,), (1,)), ((), ())),
                            precision=lax.Precision.HIGHEST,
                            preferred_element_type=F32)

    x_a, y_a = jnp.where(lo, bb_re, bb_im), jnp.where(lo, -bb_im, bb_re)
    p_ref[...] = (outer(x_a, wr_re) + outer(y_a, wr_im)).astype(BF16)

    qt_ref[...] = (outer(c_re, jnp.where(lo, w1_re, -w1_im))
                   + outer(c_im, jnp.where(lo, -w1_im, -w1_re))).astype(BF16)

    t_re, t_im = a_re, a_im
    for _ in range(6):
        t_re, t_im = t_re * t_re - t_im * t_im, 2.0 * t_re * t_im
    for k in range(COEF_ROWS // 2):
        coef_ref[2 * k:2 * k + 1, :] = t_re
        coef_ref[2 * k + 1:2 * k + 2, :] = jnp.where(lo, -t_im, t_im)
        t_re, t_im = t_re * t_re - t_im * t_im, 2.0 * t_re * t_im
    return strip


def _toeplitz_columns(strip, m_ref, cols):
    width = cols.stop - cols.start
    s_idx = lax.broadcasted_iota(jnp.int32, (SSM_T, width), 0)
    t_idx = lax.broadcasted_iota(jnp.int32, (SSM_T, width), 1) & (SSM_T - 1)
    causal = t_idx >= s_idx
    for c in range(SSM_GROUP):
        rows = jnp.broadcast_to(strip[c:c + 1, cols], (SSM_T, width))
        shifted = pltpu.roll(rows, 0, 1, stride=1, stride_axis=0)
        m_ref[c * SSM_T:(c + 1) * SSM_T, cols] = jnp.where(causal, shifted, 0.0).astype(BF16)


PAR_MATS = 8
PAR_ROWS = PAR_MATS + 4 * SSM_GROUP


def _table_params(lam_re, lam_im, log_step, b_re, b_im, c_re, c_im):
    step = jnp.broadcast_to(log_step[:, None, None], (SSM_GROUPS, 1, SSM_STATE))
    filler = jnp.zeros((SSM_GROUPS, PAR_MATS - 3, SSM_STATE), F32)
    pack = jnp.concatenate([lam_re[:, None, :], lam_im[:, None, :], step, filler,
                            jnp.swapaxes(b_re, 1, 2), jnp.swapaxes(b_im, 1, 2), c_re, c_im], axis=1)
    return jnp.concatenate([pack, pack], axis=-1)


def _cmul(a1, a2, h, hs):
    return a1 * h + a2 * hs, a1 * hs - a2 * h


def _ssm_kernel(par_ref, u_ref, h0_ref, y_ref, hfin_ref, m_ref, p_ref, qt_ref, coef_ref,
                *, rows_per_seq, prompt_rows):
    for gi in range(SSM_GROUPS_PER_STEP):
        _ssm_group(par_ref.at[gi], u_ref, gi * SSM_GROUP, h0_ref.at[gi], y_ref, hfin_ref.at[gi],
                   m_ref.at[gi], p_ref.at[gi], qt_ref.at[gi], coef_ref.at[gi],
                   rows_per_seq=rows_per_seq, prompt_rows=prompt_rows)


def _ssm_group(par_ref, u_ref, c0, h0_ref, y_ref, hfin_ref, m_ref, p_ref, qt_ref, coef_ref,
               *, rows_per_seq, prompt_rows):
    strip = _build_tables(par_ref, p_ref, qt_ref, coef_ref)
    lo = lax.broadcasted_iota(jnp.int32, (1, ROW_TOKENS), 1) < SSM_T

    def chunk_rows(ref):
        even, odd = [], []
        rows = ref.shape[0] * ROW_GROUP
        for k in range(SSM_GROUP // 2):
            a = ref[:, c0 + 2 * k].reshape(rows, ROW_TOKENS)
            b = ref[:, c0 + 2 * k + 1].reshape(rows, ROW_TOKENS)
            even.append(jnp.where(lo, a, pltpu.roll(b, SSM_T, 1)))
            odd.append(jnp.where(lo, pltpu.roll(a, SSM_T, 1), b))
        return jnp.concatenate(even, axis=1), jnp.concatenate(odd, axis=1)

    def store_rows(y_even, y_odd, ref, first_channel):
        for k in range(y_even.shape[1] // ROW_TOKENS):
            te = y_even[:, k * ROW_TOKENS:(k + 1) * ROW_TOKENS]
            to = y_odd[:, k * ROW_TOKENS:(k + 1) * ROW_TOKENS]
            tiles = (ref.shape[0], ROW_GROUP, ROW_TOKENS)
            c = c0 + first_channel + 2 * k
            ref[:, c] = jnp.where(lo, te, pltpu.roll(to, SSM_T, 1)).reshape(tiles)
            ref[:, c + 1] = jnp.where(lo, pltpu.roll(te, SSM_T, 1), to).reshape(tiles)

    even, odd = chunk_rows(u_ref)
    rp, rs = prompt_rows, even.shape[0] - prompt_rows
    u = jnp.concatenate([even[:rp], odd[:rp], even[rp:], odd[rp:]], axis=0).astype(BF16)
    s1 = jnp.dot(u, p_ref[...], preferred_element_type=F32)
    s2 = pltpu.roll(s1, SSM_STATE, 1)
    a1, a2 = coef_ref[0:1, :], coef_ref[1:2, :]

    e1, e2, o1, o2 = s1[:rp], s2[:rp], s1[rp:2 * rp], s2[rp:2 * rp]
    x1, x2 = _cmul(a1, a2, e1, e2)
    x1, x2 = x1 + o1, x2 + o2
    pos = lax.broadcasted_iota(jnp.int32, (rp, STATE_LANES), 0) & (rows_per_seq - 1)
    for k in range(rows_per_seq.bit_length() - 1):
        d = 1 << k
        b1, b2 = coef_ref[2 + 2 * k:3 + 2 * k, :], coef_ref[3 + 2 * k:4 + 2 * k, :]
        sh1 = jnp.where(pos >= d, pltpu.roll(x1, d, 0), 0.0)
        sh2 = jnp.where(pos >= d, pltpu.roll(x2, d, 0), 0.0)
        y1, y2 = _cmul(b1, b2, sh1, sh2)
        x1, x2 = x1 + y1, x2 + y2
    g1 = jnp.where(pos >= 1, pltpu.roll(x1, 1, 0), 0.0)
    g2 = jnp.where(pos >= 1, pltpu.roll(x2, 1, 0), 0.0)
    ho1 = _cmul(a1, a2, g1, g2)[0] + e1
    hfin_ref[0:STATE_OUT_ROWS, :] = jnp.zeros((STATE_OUT_ROWS, STATE_LANES), F32)
    for b in range(rp // rows_per_seq):
        last = (b + 1) * rows_per_seq - 1
        hfin_ref[b:b + 1, :] = x1[last:last + 1, :]

    h0e, h0o = h0_ref[0], h0_ref[1]
    swap = lambda h: pltpu.roll(h, SSM_STATE, 1)
    hfin_ref[STATE_OUT_ROWS:STATE_OUT_ROWS + rs, :] = a1 * h0e + a2 * swap(h0e) + s1[2 * rp:2 * rp + rs]
    hfin_ref[STATE_OUT_ROWS + rs:, :] = a1 * h0o + a2 * swap(h0o) + s1[2 * rp + rs:]

    hprev = jnp.concatenate([g1, ho1, h0e, h0o], axis=0).astype(BF16)
    blocks = [slice(c * SSM_T, (c + M_BLOCK_CHANNELS) * SSM_T) for c in range(0, SSM_GROUP, M_BLOCK_CHANNELS)]
    _toeplitz_columns(strip, m_ref, blocks[0])
    for i, cols in enumerate(blocks):
        y = lax.dot_general(hprev, qt_ref[cols, :], (((1,), (1,)), ((), ())), preferred_element_type=F32)
        y += jnp.dot(u, m_ref[:, cols], preferred_element_type=F32)
        if i + 1 < len(blocks):
            _toeplitz_columns(strip, m_ref, blocks[i + 1])
        store_rows(jnp.concatenate([y[:rp], y[2 * rp:2 * rp + rs]], axis=0),
                   jnp.concatenate([y[rp:2 * rp], y[2 * rp + rs:]], axis=0), y_ref, i * M_BLOCK_CHANNELS)


def _ssm(table_params, u3, h0_pack, *, rows_per_seq, prompt_rows):
    groups = u3.shape[0]
    rs = groups * ROW_GROUP - prompt_rows
    u4 = u3.reshape(groups, SSM_WIDTH, ROW_GROUP, ROW_TOKENS)
    assert rows_per_seq & (rows_per_seq - 1) == 0 and 2 * rows_per_seq.bit_length() <= COEF_ROWS
    gps = SSM_GROUPS_PER_STEP
    g3 = lambda g: (g, 0, 0)
    u_spec = pl.BlockSpec((groups, gps * SSM_GROUP, ROW_GROUP, ROW_TOKENS), lambda g: (0, g, 0, 0))
    fin_rows = STATE_OUT_ROWS + 2 * rs
    return pl.pallas_call(
        functools.partial(_ssm_kernel, rows_per_seq=rows_per_seq, prompt_rows=prompt_rows),
        grid=(SSM_GROUPS // gps,),
        in_specs=[
            pl.BlockSpec((gps, PAR_ROWS, STATE_LANES), g3),
            u_spec,
            pl.BlockSpec((gps, 2, rs, STATE_LANES), lambda g: (g, 0, 0, 0)),
        ],
        out_specs=[
            u_spec,
            pl.BlockSpec((gps, fin_rows, STATE_LANES), g3),
        ],
        out_shape=[
            jax.ShapeDtypeStruct(u4.shape, F32),
            jax.ShapeDtypeStruct((SSM_GROUPS, fin_rows, STATE_LANES), F32),
        ],
        scratch_shapes=[
            pltpu.VMEM((gps, SSM_CW, SSM_CW), BF16),
            pltpu.VMEM((gps, SSM_CW, STATE_LANES), BF16),
            pltpu.VMEM((gps, SSM_CW, STATE_LANES), BF16),
            pltpu.VMEM((gps, COEF_ROWS, STATE_LANES), F32),
        ],
        compiler_params=pltpu.CompilerParams(
            dimension_semantics=("arbitrary",), vmem_limit_bytes=V7X_VMEM_LIMIT_BYTES),
        name="ssm",
    )(table_params, u4, h0_pack)


def _post_kernel(xp_ref, xs_ref, atp_ref, ats_ref, ut_ref, yt_ref, *refs, alpha, prompt_tiles):
    weights, (op_ref, os_ref) = refs[:-2], refs[-2:]

    @pl.when(pl.program_id(0) < prompt_tiles)
    def _():
        _post_tile(xp_ref, atp_ref, ut_ref, yt_ref, *weights, op_ref, alpha=alpha)

    @pl.when(pl.program_id(0) >= prompt_tiles)
    def _():
        _post_tile(xs_ref, ats_ref, ut_ref, yt_ref, *weights, os_ref, alpha=alpha)


def _post_tile(x_ref, at_ref, ut_ref, yt_ref, lng_ref, lnb_ref, cols_ref, wglut_ref,
               wout_ref, ln1g_ref, ln1b_ref, wgu_ref, wdown_ref, ln2g_ref, ln2b_ref, o_ref,
               *, alpha):
    xn = _layer_norm(x_ref[...], lng_ref[...], lnb_ref[...])
    n_rows = x_ref.shape[0] // ROW_TOKENS
    first_row = (pl.program_id(0) % (ROW_GROUP // n_rows)) * n_rows

    def feature_major(ref):
        return jnp.concatenate(
            [ref[pl.ds(first_row + j, SSM_WIDTH, stride=ROW_GROUP), :] for j in range(n_rows)], axis=1)

    ys = feature_major(yt_ref) + cols_ref[0] * feature_major(ut_ref)
    gl = 0.5 * ys * (1.0 + lax.erf(ys * math.sqrt(0.5)))
    z = jnp.dot(wglut_ref[...], gl.astype(BF16), preferred_element_type=F32) + cols_ref[1]
    s = gl * jax.nn.sigmoid(z)
    tn = (((0,), (0,)), ((), ()))
    mix = lax.dot_general(at_ref[...], wout_ref[:PROJ_Q, :], tn, preferred_element_type=F32)
    mix += lax.dot_general(s.astype(BF16), wout_ref[PROJ_Q:, :], tn, preferred_element_type=F32)
    h = _layer_norm(alpha * xn + mix, ln1g_ref[...], ln1b_ref[...])
    hb = h.astype(BF16)
    f = jnp.zeros_like(h)
    for start in range(0, D_FF, FF_BLOCK):
        stop = min(start + FF_BLOCK, D_FF)
        cols = slice(start, stop)
        up_cols = slice(D_FF + start, D_FF + stop)
        g = jnp.dot(hb, wgu_ref[:, cols], preferred_element_type=F32)
        up = jnp.dot(hb, wgu_ref[:, up_cols], preferred_element_type=F32)
        act = (g * jax.nn.sigmoid(g)) * up
        f += jnp.dot(act.astype(BF16), wdown_ref[cols, :], preferred_element_type=F32)
    o_ref[...] = _layer_norm(alpha * h + f, ln2g_ref[...], ln2b_ref[...])


def _post(xp2d, xs2d, atp, ats, ut3, yt3, ln_g, ln_b, d_bglu_cols, w_glu_t, w_out, ln1_g, ln1_b,
          w_gate_up, w_down, ln2_g, ln2_b, *, alpha, tm):
    n_p, n_s = xp2d.shape[0], xs2d.shape[0]
    tp, ts = n_p // tm, n_s // tm
    assert PROJ_TILE % tm == 0 and n_p % PROJ_TILE == 0
    group = lambda i: (i // (PROJ_TILE // tm), 0, 0)
    group_spec = pl.BlockSpec((None, SSM_WIDTH * ROW_GROUP, ROW_TOKENS), group)
    prompt = lambda i: jnp.minimum(i, tp - 1)
    sample = lambda i: jnp.maximum(i - tp, 0)
    const = lambda i: (0, 0)
    resident = lambda shape: pl.BlockSpec(shape, const, pipeline_mode=pl.Buffered(1))
    vec = lambda width: pl.BlockSpec((1, width), const)
    columns = pl.BlockSpec((2, SSM_WIDTH, 1), lambda i: (0, 0, 0))
    return pl.pallas_call(
        functools.partial(_post_kernel, alpha=alpha, prompt_tiles=tp),
        grid=(tp + ts,),
        in_specs=[
            pl.BlockSpec((tm, D_MODEL), lambda i: (prompt(i), 0)),
            pl.BlockSpec((tm, D_MODEL), lambda i: (sample(i), 0)),
            pl.BlockSpec((PROJ_Q, tm), lambda i: (0, prompt(i))),
            pl.BlockSpec((PROJ_Q, tm), lambda i: (0, sample(i))),
            group_spec, group_spec,
            vec(D_MODEL), vec(D_MODEL), columns,
            resident((SSM_WIDTH, SSM_WIDTH)),
            resident((D_MODEL, D_MODEL)), vec(D_MODEL), vec(D_MODEL),
            resident((D_MODEL, 2 * D_FF)), resident((D_FF, D_MODEL)),
            vec(D_MODEL), vec(D_MODEL),
        ],
        out_specs=[
            pl.BlockSpec((tm, D_MODEL), lambda i: (prompt(i), 0)),
            pl.BlockSpec((tm, D_MODEL), lambda i: (sample(i), 0)),
        ],
        out_shape=[
            jax.ShapeDtypeStruct((n_p, D_MODEL), F32),
            jax.ShapeDtypeStruct((n_s, D_MODEL), F32),
        ],
        compiler_params=pltpu.CompilerParams(
            dimension_semantics=("arbitrary",), vmem_limit_bytes=V7X_VMEM_LIMIT_BYTES),
        name="post",
    )(xp2d, xs2d, atp, ats, ut3, yt3, ln_g, ln_b, d_bglu_cols, w_glu_t, w_out, ln1_g, ln1_b,
      w_gate_up, w_down, ln2_g, ln2_b)


def kernel(x_prompt, x_sample, cache_win_k, cache_win_v, state_ssm_re, state_ssm_im,
           ln_in_g, ln_in_b, w_in, attn_sinks, ssm_lambda_re, ssm_lambda_im, ssm_log_step,
           ssm_b_re, ssm_b_im, ssm_c_re, ssm_c_im, ssm_d, w_glu, b_glu, w_out,
           ln1_g, ln1_b, w_gate_up, w_down, ln2_g, ln2_b):
    depth = w_in.shape[0]
    assert depth == 1, "single-layer step"
    bp, lp, _ = x_prompt.shape
    bs, ls, _ = x_sample.shape
    assert ls == SSM_T and bs % 2 == 0 and lp % 512 == 0 and bp <= STATE_OUT_ROWS
    win_rows = cache_win_k.shape[2]
    assert win_rows == WINDOW
    alpha = (2.0 * depth) ** 0.25
    l = 0
    row = lambda a: a.reshape(1, -1)

    lng, lnb = row(ln_in_g), row(ln_in_b)
    xp2 = x_prompt.reshape(bp * lp, D_MODEL)
    xs2 = x_sample.reshape(bs * ls, D_MODEL)
    n_p = bp * lp
    qt, vt, u3, k, v = _proj(xp2, xs2, lng, lnb, w_in[l][:, PROJ_Q:PROJ_Q + KV].astype(BF16),
                             w_in[l].T.astype(BF16))

    atp, w_gu_b, w_down_b, w_out_b = _attention_prompt(
        attn_sinks, qt, k, vt, bp, lp, n_pairs=ATTN_PAIRS, f32_weights=(w_gate_up[l], w_down[l], w_out[l]))
    ck = cache_win_k[l].reshape(bs, win_rows, PROJ_KV)
    ckt = jnp.swapaxes(ck, 1, 2)
    cvt = jnp.swapaxes(cache_win_v[l].reshape(bs, win_rows, PROJ_KV), 1, 2)
    ats = _attention_sample(attn_sinks, qt, ck, k, cvt, vt, first_token=n_p, n_pairs=ATTN_PAIRS)
    windows = _windows(k, v, ckt, cvt, bp, lp)

    table_params = _table_params(
        ssm_lambda_re[l], ssm_lambda_im[l], ssm_log_step[l],
        ssm_b_re[l], ssm_b_im[l], ssm_c_re[l], ssm_c_im[l])
    h0 = jnp.concatenate([state_ssm_re[l], state_ssm_im[l]], axis=-1)
    h0_pack = jnp.transpose(h0.reshape(bs // 2, 2, SSM_GROUPS, STATE_LANES), (2, 1, 0, 3))
    y4, hfin = _ssm(table_params, u3, h0_pack,
                    rows_per_seq=lp // ROW_TOKENS, prompt_rows=n_p // ROW_TOKENS)

    post_args = (lng, lnb, jnp.stack([ssm_d[l], b_glu[l]])[:, :, None], w_glu[l].T.astype(BF16),
                 w_out_b, row(ln1_g[l]), row(ln1_b[l]), w_gu_b, w_down_b, row(ln2_g[l]), row(ln2_b[l]))
    out_p, out_s = _post(xp2, xs2, atp, ats, u3, y4.reshape(u3.shape), *post_args, alpha=alpha, tm=512)

    win_k_p, win_v_p, win_k_s, win_v_s = (
        jnp.swapaxes(w, 1, 2).reshape(1, -1, win_rows, N_KV_HEADS, HEAD_DIM) for w in windows)
    sp = jnp.swapaxes(hfin[:, :bp], 0, 1)
    ss = jnp.transpose(hfin[:, STATE_OUT_ROWS:].reshape(SSM_GROUPS, 2, bs // 2, STATE_LANES),
                       (2, 1, 0, 3)).reshape(bs, SSM_GROUPS, STATE_LANES)
    return (out_p.reshape(bp, lp, D_MODEL), out_s.reshape(bs, ls, D_MODEL),
            win_k_p, win_v_p, sp[None, ..., :SSM_STATE], sp[None, ..., SSM_STATE:],
            win_k_s, win_v_s, ss[None, ..., :SSM_STATE], ss[None, ..., SSM_STATE:])
```

```python
import functools
import math

import jax
import jax.numpy as jnp
from jax import lax
from jax.experimental import pallas as pl
from jax.experimental.pallas import tpu as pltpu

F32 = jnp.float32
BF16 = jnp.bfloat16

D_MODEL = 1024
HEAD_DIM = 64
N_HEADS = 8
N_KV_HEADS = 2
Q_PER_KV = N_HEADS // N_KV_HEADS
CHUNK = 64
WINDOW = 128
WIN_CHUNKS = WINDOW // CHUNK
BAND = (WIN_CHUNKS + 1) * CHUNK
PROJ_Q = N_HEADS * HEAD_DIM
PROJ_KV = N_KV_HEADS * HEAD_DIM
SSM_WIDTH = 512
SSM_GROUP = 16
SSM_GROUPS = SSM_WIDTH // SSM_GROUP
SSM_STATE = 64
STATE_LANES = 2 * SSM_STATE
D_FF = 2816
D_IN_PROJ = PROJ_Q + 2 * PROJ_KV + SSM_WIDTH
LN_EPS = 1e-5
NEG_INF = -1e30

SSM_T = CHUNK
SSM_CW = SSM_T * SSM_GROUP
ROW_TOKENS = 2 * SSM_T
ROW_GROUP = 8
PROJ_TILE = ROW_GROUP * ROW_TOKENS
PROJ_PARTS = 2
COEF_ROWS = 16
STATE_OUT_ROWS = 8
M_BLOCK_CHANNELS = 4
SSM_GROUPS_PER_STEP = 2
FF_BLOCK = 256
V7X_VMEM_LIMIT_BYTES = 56 * 1024 * 1024


def _layer_norm(x, g, b):
    mu = jnp.mean(x, axis=-1, keepdims=True)
    xc = x - mu
    var = jnp.mean(xc * xc, axis=-1, keepdims=True)
    return xc * lax.rsqrt(var + LN_EPS) * g + b


KV = 2 * PROJ_KV
VU = PROJ_KV + SSM_WIDTH


def _proj_kernel(xp_ref, xs_ref, g_ref, b_ref, w_ref, wqt_ref, wvut_ref, qt_ref, vt_ref, ut_ref, k_ref, v_ref,
                 *, prompt_tiles):
    outs = (g_ref, b_ref, w_ref, wqt_ref, wvut_ref, qt_ref, vt_ref, ut_ref, k_ref, v_ref)

    @pl.when(pl.program_id(0) < prompt_tiles)
    def _():
        _proj_tile(xp_ref, *outs)

    @pl.when(pl.program_id(0) >= prompt_tiles)
    def _():
        _proj_tile(xs_ref, *outs)


def _proj_tile(x_ref, g_ref, b_ref, w_ref, wqt_ref, wvut_ref, qt_ref, vt_ref, ut_ref, k_ref, v_ref):
    rows_per_part = ROW_GROUP // PROJ_PARTS
    nt = (((1,), (1,)), ((), ()))
    for part in range(PROJ_PARTS):
        tok = slice(part * rows_per_part * ROW_TOKENS, (part + 1) * rows_per_part * ROW_TOKENS)
        xb = _layer_norm(x_ref[tok, :], g_ref[...], b_ref[...]).astype(BF16)
        p = jnp.dot(xb, w_ref[...], preferred_element_type=F32)
        k_ref[tok, :] = p[:, :PROJ_KV]
        v_ref[tok, :] = p[:, PROJ_KV:]
        qt = lax.dot_general(wqt_ref[...], xb, nt, preferred_element_type=F32)
        vut = lax.dot_general(wvut_ref[...], xb, nt, preferred_element_type=F32)
        qt_ref[:, tok] = (qt * (HEAD_DIM ** -0.5)).astype(BF16)
        vt_ref[:, tok] = vut[:PROJ_KV].astype(BF16)
        for j in range(rows_per_part):
            ut_ref[pl.ds(part * rows_per_part + j, SSM_WIDTH, stride=ROW_GROUP), :] = (
                vut[PROJ_KV:, j * ROW_TOKENS:(j + 1) * ROW_TOKENS])


def _proj(xp2d, xs2d, ln_g, ln_b, w_kv, w_in_t):
    assert D_IN_PROJ == 2 * VU
    tm = PROJ_TILE
    tp, ts = xp2d.shape[0] // tm, xs2d.shape[0] // tm
    n = (tp + ts) * tm
    const = lambda i: (0, 0)
    row = lambda i: (i, 0)
    col = lambda i: (0, i)
    return pl.pallas_call(
        functools.partial(_proj_kernel, prompt_tiles=tp),
        grid=(tp + ts,),
        in_specs=[
            pl.BlockSpec((tm, D_MODEL), lambda i: (jnp.minimum(i, tp - 1), 0)),
            pl.BlockSpec((tm, D_MODEL), lambda i: (jnp.maximum(i - tp, 0), 0)),
            pl.BlockSpec((1, D_MODEL), const),
            pl.BlockSpec((1, D_MODEL), const),
            pl.BlockSpec((D_MODEL, KV), const),
            pl.BlockSpec((PROJ_Q, D_MODEL), const),
            pl.BlockSpec((VU, D_MODEL), lambda i: (1, 0)),
        ],
        out_specs=[
            pl.BlockSpec((PROJ_Q, tm), col),
            pl.BlockSpec((PROJ_KV, tm), col),
            pl.BlockSpec((None, SSM_WIDTH * ROW_GROUP, ROW_TOKENS), lambda i: (i, 0, 0)),
            pl.BlockSpec((tm, PROJ_KV), row),
            pl.BlockSpec((tm, PROJ_KV), row),
        ],
        out_shape=[
            jax.ShapeDtypeStruct((PROJ_Q, n), BF16),
            jax.ShapeDtypeStruct((PROJ_KV, n), BF16),
            jax.ShapeDtypeStruct((n // tm, SSM_WIDTH * ROW_GROUP, ROW_TOKENS), F32),
            jax.ShapeDtypeStruct((n, PROJ_KV), F32),
            jax.ShapeDtypeStruct((n, PROJ_KV), F32),
        ],
        compiler_params=pltpu.CompilerParams(
            dimension_semantics=("arbitrary",), vmem_limit_bytes=V7X_VMEM_LIMIT_BYTES),
        name="proj",
    )(xp2d, xs2d, ln_g, ln_b, w_kv, w_in_t, w_in_t)


PAIR = 2 * CHUNK
HEAD_LANES = Q_PER_KV * PAIR
ONES_ROWS = 16
ATTN_PAIRS = 16
BF16_TILE_ROWS = 16


def _scores(unit, qt_ref):
    kwin, _, lanes, _, _, h = unit
    base = h * Q_PER_KV * HEAD_DIM
    qrow = jnp.concatenate(
        [qt_ref[base + g * HEAD_DIM:base + (g + 1) * HEAD_DIM, lanes] for g in range(Q_PER_KV)],
        axis=1)
    zero = jnp.zeros_like(qrow)
    qstack = jnp.concatenate([qrow, zero] if h == 0 else [zero, qrow], axis=0)
    return jnp.dot(kwin, qstack, preferred_element_type=F32)


def _sink_rows(sinks_ref):
    tile = lax.broadcasted_iota(jnp.int32, (1, HEAD_LANES), 1) // PAIR
    rows = []
    for h in range(N_KV_HEADS):
        row = jnp.full((1, HEAD_LANES), sinks_ref[0, h * Q_PER_KV], F32)
        for g in range(1, Q_PER_KV):
            row = jnp.where(tile == g, sinks_ref[0, h * Q_PER_KV + g], row)
        rows.append(row)
    return rows


def _finish(unit, s, sinks, o_ref):
    _, vtwin, lanes, valid, masked_rows, h = unit
    nk = s.shape[0]
    base = h * Q_PER_KV * HEAD_DIM
    pieces, done = [], 0
    for start, stop in masked_rows:
        if start > done:
            pieces.append(s[done:start])
        pieces.append(jnp.where(valid[start:stop], s[start:stop], NEG_INF))
        done = stop
    if done < nk:
        pieces.append(s[done:])
    s = jnp.concatenate(pieces, axis=0)
    sink = sinks[h]
    m = jnp.maximum(jnp.max(s, axis=0, keepdims=True), sink)
    p = jnp.exp(s - m).astype(BF16)
    v_ones = jnp.concatenate(
        [vtwin[h * HEAD_DIM:(h + 1) * HEAD_DIM, :], jnp.ones((ONES_ROWS, nk), BF16)], axis=0)
    ov = jnp.dot(v_ones, p, preferred_element_type=F32)
    den = ov[HEAD_DIM:HEAD_DIM + 1, :] + jnp.exp(sink - m)
    o = ov[:HEAD_DIM, :] * (1.0 / den)
    for g in range(Q_PER_KV):
        o_ref[base + g * HEAD_DIM:base + (g + 1) * HEAD_DIM, lanes] = (
            o[:, g * PAIR:(g + 1) * PAIR].astype(BF16))


def _attend_units(units, qt_ref, sinks_ref, o_ref):
    sinks = _sink_rows(sinks_ref)
    s_next = _scores(units[0], qt_ref)
    for i, unit in enumerate(units):
        s = s_next
        if i + 1 < len(units):
            s_next = _scores(units[i + 1], qt_ref)
        _finish(unit, s, sinks, o_ref)


def _attn_prompt_kernel(sinks_ref, qt_ref, kp_ref, kc_ref, vtp_ref, vtc_ref, *refs, n_pairs):
    n_cast = (len(refs) - 1) // 2
    o_ref = refs[n_cast]
    for w_ref, wb_ref in zip(refs[:n_cast], refs[n_cast + 1:]):
        wb_ref[...] = w_ref[...].astype(BF16)
    tq = n_pairs * PAIR
    nk = WINDOW + PAIR
    kk = jnp.concatenate([kp_ref[...], kc_ref[...]], axis=0).astype(BF16)
    vt = jnp.concatenate([vtp_ref[...], vtc_ref[...]], axis=1)
    r = lax.broadcasted_iota(jnp.int32, (nk, HEAD_LANES), 0)
    first_chunk = (lax.broadcasted_iota(jnp.int32, (nk, HEAD_LANES), 1) & (PAIR - 1)) < CHUNK
    lo = jnp.where(first_chunk, 0, CHUNK)
    hi = jnp.where(first_chunk, BAND, nk)
    first_pos = pl.program_id(1) * tq - WINDOW
    units = []
    for pp in range(n_pairs):
        lo_pp = jnp.maximum(lo, -first_pos) if pp == 0 else lo
        valid = (r >= lo_pp) & (r < hi)
        masked_rows = ((0, nk),) if pp == 0 else ((0, CHUNK), (nk - CHUNK, nk))
        for h in range(N_KV_HEADS):
            units.append((kk[pp * PAIR:pp * PAIR + nk], vt[:, pp * PAIR:pp * PAIR + nk],
                          slice(pp * PAIR, (pp + 1) * PAIR), valid, masked_rows, h))
    _attend_units(units, qt_ref, sinks_ref, o_ref)


def _attn_sample_kernel(sinks_ref, qt_ref, ck_ref, kn_ref, cvt_ref, vtn_ref, o_ref, *, n_pairs):
    nk = 2 * WINDOW + PAIR
    r = lax.broadcasted_iota(jnp.int32, (nk, HEAD_LANES), 0)
    query_seq = (lax.broadcasted_iota(jnp.int32, (nk, HEAD_LANES), 1) & (PAIR - 1)) >> 6
    key_seq = jnp.where(r < 2 * WINDOW, r >> 7, (r - 2 * WINDOW) >> 6)
    valid = query_seq == key_seq
    units = []
    for pp in range(n_pairs):
        lanes = slice(pp * PAIR, (pp + 1) * PAIR)
        kwin = jnp.concatenate([ck_ref[2 * pp], ck_ref[2 * pp + 1], kn_ref[lanes, :]],
                               axis=0).astype(BF16)
        vtwin = jnp.concatenate([cvt_ref[2 * pp].astype(BF16), cvt_ref[2 * pp + 1].astype(BF16),
                                 vtn_ref[:, lanes]], axis=1)
        units.extend((kwin, vtwin, lanes, valid, ((0, nk),), h) for h in range(N_KV_HEADS))
    _attend_units(units, qt_ref, sinks_ref, o_ref)


_SINK_SPEC = pl.BlockSpec(memory_space=pltpu.SMEM)


def _attention_prompt(sinks, qt, k, vt, bsz, seq, n_pairs, f32_weights):
    tq = n_pairs * PAIR
    nt = seq // tq
    steps = bsz * nt
    wpt = tq // WINDOW
    cur_c = lambda b, i: (0, b * nt + i)
    cur_r = lambda b, i: (b * nt + i, 0)
    prev = lambda b, i: jnp.maximum((b * nt + i) * wpt - 1, 0)
    assert all(w.shape[0] % (steps * BF16_TILE_ROWS) == 0 for w in f32_weights)
    cast_specs = [pl.BlockSpec((w.shape[0] // steps, w.shape[1]), cur_r) for w in f32_weights]
    return pl.pallas_call(
        functools.partial(_attn_prompt_kernel, n_pairs=n_pairs),
        grid=(bsz, nt),
        in_specs=[
            _SINK_SPEC,
            pl.BlockSpec((PROJ_Q, tq), cur_c),
            pl.BlockSpec((WINDOW, PROJ_KV), lambda b, i: (prev(b, i), 0)),
            pl.BlockSpec((tq, PROJ_KV), cur_r),
            pl.BlockSpec((PROJ_KV, WINDOW), lambda b, i: (0, prev(b, i))),
            pl.BlockSpec((PROJ_KV, tq), cur_c),
            *cast_specs,
        ],
        out_specs=[pl.BlockSpec((PROJ_Q, tq), cur_c), *cast_specs],
        out_shape=[jax.ShapeDtypeStruct((PROJ_Q, bsz * seq), BF16),
                   *[jax.ShapeDtypeStruct(w.shape, BF16) for w in f32_weights]],
        compiler_params=pltpu.CompilerParams(
            dimension_semantics=("arbitrary", "arbitrary"), vmem_limit_bytes=V7X_VMEM_LIMIT_BYTES),
        name="attn_prompt",
    )(sinks, qt, k, k, vt, vt, *f32_weights)


def _attention_sample(sinks, qt, cache_k, k, cache_vt, vt, first_token, n_pairs):
    n = cache_k.shape[0] * CHUNK
    tq = n_pairs * PAIR
    assert first_token % tq == 0
    off = first_token // tq
    return pl.pallas_call(
        functools.partial(_attn_sample_kernel, n_pairs=n_pairs),
        grid=(n // tq,),
        in_specs=[
            _SINK_SPEC,
            pl.BlockSpec((PROJ_Q, tq), lambda i: (0, i + off)),
            pl.BlockSpec((2 * n_pairs, WINDOW, PROJ_KV), lambda i: (i, 0, 0)),
            pl.BlockSpec((tq, PROJ_KV), lambda i: (i + off, 0)),
            pl.BlockSpec((2 * n_pairs, PROJ_KV, WINDOW), lambda i: (i, 0, 0)),
            pl.BlockSpec((PROJ_KV, tq), lambda i: (0, i + off)),
        ],
        out_specs=pl.BlockSpec((PROJ_Q, tq), lambda i: (0, i)),
        out_shape=jax.ShapeDtypeStruct((PROJ_Q, n), BF16),
        compiler_params=pltpu.CompilerParams(dimension_semantics=("arbitrary",)),
        name="attn_sample",
    )(sinks, qt, cache_k, k, cache_vt, vt)


def _windows_kernel(kp_ref, vp_ref, kn_ref, vn_ref, ckt_ref, cvt_ref, wkp_ref, wvp_ref, wks_ref, wvs_ref):
    first_half = lax.broadcasted_iota(jnp.int32, (PROJ_KV, WINDOW), 1) < CHUNK
    for new_p, new_s, cache_t, win_p, win_s in ((kp_ref, kn_ref, ckt_ref, wkp_ref, wks_ref),
                                                (vp_ref, vn_ref, cvt_ref, wvp_ref, wvs_ref)):
        win_p[0] = new_p[...].T
        for pair in range(cache_t.shape[0] // 2):
            new_t = new_s[pair * PAIR:(pair + 1) * PAIR, :].T
            swapped = pltpu.roll(new_t, CHUNK, 1)
            for j, new in enumerate((swapped, new_t)):
                old = pltpu.roll(cache_t[2 * pair + j], CHUNK, 1)
                win_s[2 * pair + j] = jnp.where(first_half, old, new)


def _windows(k, v, cache_kt, cache_vt, bp, seq):
    bs = cache_kt.shape[0]
    assert bs % (2 * bp) == 0 and seq % WINDOW == 0
    share = bs // bp
    assert (bp * seq) % (share * CHUNK) == 0
    first_share = bp * seq // (share * CHUNK)
    last_window = pl.BlockSpec((WINDOW, PROJ_KV), lambda b: ((b + 1) * (seq // WINDOW) - 1, 0))
    new_rows = pl.BlockSpec((share * CHUNK, PROJ_KV), lambda b: (first_share + b, 0))
    one = pl.BlockSpec((1, PROJ_KV, WINDOW), lambda b: (b, 0, 0))
    many = pl.BlockSpec((share, PROJ_KV, WINDOW), lambda b: (b, 0, 0))
    return pl.pallas_call(
        _windows_kernel,
        grid=(bp,),
        in_specs=[last_window, last_window, new_rows, new_rows, many, many],
        out_specs=[one, one, many, many],
        out_shape=[jax.ShapeDtypeStruct((bp, PROJ_KV, WINDOW), F32)] * 2
        + [jax.ShapeDtypeStruct((bs, PROJ_KV, WINDOW), F32)] * 2,
        compiler_params=pltpu.CompilerParams(dimension_semantics=("arbitrary",)),
        name="windows",
    )(k, v, k, v, cache_kt, cache_vt)


def _power_table(tau, nbits, a_re, a_im):
    rows = tau.shape[0]
    w_re = jnp.ones((rows, STATE_LANES), F32)
    w_im = jnp.zeros((rows, STATE_LANES), F32)
    p_re, p_im = a_re, a_im
    for k in range(nbits):
        bit = ((tau >> k) & 1) == 1
        f_re = jnp.where(bit, p_re, 1.0)
        f_im = jnp.where(bit, p_im, 0.0)
        w_re, w_im = w_re * f_re - w_im * f_im, w_re * f_im + w_im * f_re
        p_re, p_im = p_re * p_re - p_im * p_im, 2.0 * p_re * p_im
    return w_re, w_im


def _power_rows(a_re, a_im, descending):
    t = lax.broadcasted_iota(jnp.int32, (ROW_GROUP, 1), 0)
    w_re, w_im = _power_table(ROW_GROUP - 1 - t if descending else t, 3, a_re, a_im)
    p_re, p_im = a_re, a_im
    for _ in range(3):
        p_re, p_im = p_re * p_re - p_im * p_im, 2.0 * p_re * p_im
    while w_re.shape[0] < SSM_T:
        n_re, n_im = w_re * p_re - w_im * p_im, w_re * p_im + w_im * p_re
        w_re = jnp.concatenate([n_re, w_re] if descending else [w_re, n_re], axis=0)
        w_im = jnp.concatenate([n_im, w_im] if descending else [w_im, n_im], axis=0)
        p_re, p_im = p_re * p_re - p_im * p_im, 2.0 * p_re * p_im
    return w_re, w_im


def _build_tables(par_ref, p_ref, qt_ref, coef_ref):
    lo = lax.broadcasted_iota(jnp.int32, (1, STATE_LANES), 1) < SSM_STATE
    mat = lambda i: par_ref[PAR_MATS + i * SSM_GROUP:PAR_MATS + (i + 1) * SSM_GROUP, :]
    lr, li = par_ref[0:1, :], par_ref[1:2, :]
    dt = jnp.exp(par_ref[2:3, :])
    mag = jnp.exp(lr * dt)
    a_re, a_im = mag * jnp.cos(li * dt), mag * jnp.sin(li * dt)
    nr, ni = a_re - 1.0, a_im
    den = lr * lr + li * li
    f_re, f_im = (nr * lr + ni * li) / den, (ni * lr - nr * li) / den
    b_re, b_im = mat(0), mat(1)
    bb_re = f_re * b_re - f_im * b_im
    bb_im = f_re * b_im + f_im * b_re
    c_re, c_im = mat(2), mat(3)

    w_re, w_im = _power_rows(a_re, a_im, descending=False)
    w1_re, w1_im = w_re * a_re - w_im * a_im, w_re * a_im + w_im * a_re
    wr_re, wr_im = _power_rows(a_re, a_im, descending=True)

    def outer(c, w):
        return (c[:, None, :] * w[None, :, :]).reshape(SSM_CW, STATE_LANES)

    cw_mix = (outer(c_re, jnp.where(lo, w_re, w_im)) + outer(c_im, jnp.where(lo, -w_im, w_re)))
    bb_mix = jnp.where(lo, bb_re, -bb_im)
    strip = lax.dot_general(bb_mix, cw_mix, (((1,), (1,)), ((), ())),
                            precision=lax.Precision.HIGHEST,
                            preferred_element_type=F32)

    x_a, y_a = jnp.where(lo, bb_re, bb_im), jnp.where(lo, -bb_im, bb_re)
    p_ref[...] = (outer(x_a, wr_re) + outer(y_a, wr_im)).astype(BF16)

    qt_ref[...] = (outer(c_re, jnp.where(lo, w1_re, -w1_im))
                   + outer(c_im, jnp.where(lo, -w1_im, -w1_re))).astype(BF16)

    t_re, t_im = a_re, a_im
    for _ in range(6):
        t_re, t_im = t_re * t_re - t_im * t_im, 2.0 * t_re * t_im
    for k in range(COEF_ROWS // 2):
        coef_ref[2 * k:2 * k + 1, :] = t_re
        coef_ref[2 * k + 1:2 * k + 2, :] = jnp.where(lo, -t_im, t_im)
        t_re, t_im = t_re * t_re - t_im * t_im, 2.0 * t_re * t_im
    return strip


def _toeplitz_columns(strip, m_ref, cols):
    width = cols.stop - cols.start
    s_idx = lax.broadcasted_iota(jnp.int32, (SSM_T, width), 0)
    t_idx = lax.broadcasted_iota(jnp.int32, (SSM_T, width), 1) & (SSM_T - 1)
    causal = t_idx >= s_idx
    for c in range(SSM_GROUP):
        rows = jnp.broadcast_to(strip[c:c + 1, cols], (SSM_T, width))
        shifted = pltpu.roll(rows, 0, 1, stride=1, stride_axis=0)
        m_ref[c * SSM_T:(c + 1) * SSM_T, cols] = jnp.where(causal, shifted, 0.0).astype(BF16)


PAR_MATS = 8
PAR_ROWS = PAR_MATS + 4 * SSM_GROUP


def _table_params(lam_re, lam_im, log_step, b_re, b_im, c_re, c_im):
    step = jnp.broadcast_to(log_step[:, None, None], (SSM_GROUPS, 1, SSM_STATE))
    filler = jnp.zeros((SSM_GROUPS, PAR_MATS - 3, SSM_STATE), F32)
    pack = jnp.concatenate([lam_re[:, None, :], lam_im[:, None, :], step, filler,
                            jnp.swapaxes(b_re, 1, 2), jnp.swapaxes(b_im, 1, 2), c_re, c_im], axis=1)
    return jnp.concatenate([pack, pack], axis=-1)


def _cmul(a1, a2, h, hs):
    return a1 * h + a2 * hs, a1 * hs - a2 * h


def _ssm_kernel(par_ref, u_ref, h0_ref, y_ref, hfin_ref, m_ref, p_ref, qt_ref, coef_ref,
                *, rows_per_seq, prompt_rows):
    for gi in range(SSM_GROUPS_PER_STEP):
        _ssm_group(par_ref.at[gi], u_ref, gi * SSM_GROUP, h0_ref.at[gi], y_ref, hfin_ref.at[gi],
                   m_ref.at[gi], p_ref.at[gi], qt_ref.at[gi], coef_ref.at[gi],
                   rows_per_seq=rows_per_seq, prompt_rows=prompt_rows)


def _ssm_group(par_ref, u_ref, c0, h0_ref, y_ref, hfin_ref, m_ref, p_ref, qt_ref, coef_ref,
               *, rows_per_seq, prompt_rows):
    strip = _build_tables(par_ref, p_ref, qt_ref, coef_ref)
    lo = lax.broadcasted_iota(jnp.int32, (1, ROW_TOKENS), 1) < SSM_T

    def chunk_rows(ref):
        even, odd = [], []
        rows = ref.shape[0] * ROW_GROUP
        for k in range(SSM_GROUP // 2):
            a = ref[:, c0 + 2 * k].reshape(rows, ROW_TOKENS)
            b = ref[:, c0 + 2 * k + 1].reshape(rows, ROW_TOKENS)
            even.append(jnp.where(lo, a, pltpu.roll(b, SSM_T, 1)))
            odd.append(jnp.where(lo, pltpu.roll(a, SSM_T, 1), b))
        return jnp.concatenate(even, axis=1), jnp.concatenate(odd, axis=1)

    def store_rows(y_even, y_odd, ref, first_channel):
        for k in range(y_even.shape[1] // ROW_TOKENS):
            te = y_even[:, k * ROW_TOKENS:(k + 1) * ROW_TOKENS]
            to = y_odd[:, k * ROW_TOKENS:(k + 1) * ROW_TOKENS]
            tiles = (ref.shape[0], ROW_GROUP, ROW_TOKENS)
            c = c0 + first_channel + 2 * k
            ref[:, c] = jnp.where(lo, te, pltpu.roll(to, SSM_T, 1)).reshape(tiles)
            ref[:, c + 1] = jnp.where(lo, pltpu.roll(te, SSM_T, 1), to).reshape(tiles)

    even, odd = chunk_rows(u_ref)
    rp, rs = prompt_rows, even.shape[0] - prompt_rows
    u = jnp.concatenate([even[:rp], odd[:rp], even[rp:], odd[rp:]], axis=0).astype(BF16)
    s1 = jnp.dot(u, p_ref[...], preferred_element_type=F32)
    s2 = pltpu.roll(s1, SSM_STATE, 1)
    a1, a2 = coef_ref[0:1, :], coef_ref[1:2, :]

    e1, e2, o1, o2 = s1[:rp], s2[:rp], s1[rp:2 * rp], s2[rp:2 * rp]
    x1, x2 = _cmul(a1, a2, e1, e2)
    x1, x2 = x1 + o1, x2 + o2
    pos = lax.broadcasted_iota(jnp.int32, (rp, STATE_LANES), 0) & (rows_per_seq - 1)
    for k in range(rows_per_seq.bit_length() - 1):
        d = 1 << k
        b1, b2 = coef_ref[2 + 2 * k:3 + 2 * k, :], coef_ref[3 + 2 * k:4 + 2 * k, :]
        sh1 = jnp.where(pos >= d, pltpu.roll(x1, d, 0), 0.0)
        sh2 = jnp.where(pos >= d, pltpu.roll(x2, d, 0), 0.0)
        y1, y2 = _cmul(b1, b2, sh1, sh2)
        x1, x2 = x1 + y1, x2 + y2
    g1 = jnp.where(pos >= 1, pltpu.roll(x1, 1, 0), 0.0)
    g2 = jnp.where(pos >= 1, pltpu.roll(x2, 1, 0), 0.0)
    ho1 = _cmul(a1, a2, g1, g2)[0] + e1
    hfin_ref[0:STATE_OUT_ROWS, :] = jnp.zeros((STATE_OUT_ROWS, STATE_LANES), F32)
    for b in range(rp // rows_per_seq):
        last = (b + 1) * rows_per_seq - 1
        hfin_ref[b:b + 1, :] = x1[last:last + 1, :]

    h0e, h0o = h0_ref[0], h0_ref[1]
    swap = lambda h: pltpu.roll(h, SSM_STATE, 1)
    hfin_ref[STATE_OUT_ROWS:STATE_OUT_ROWS + rs, :] = a1 * h0e + a2 * swap(h0e) + s1[2 * rp:2 * rp + rs]
    hfin_ref[STATE_OUT_ROWS + rs:, :] = a1 * h0o + a2 * swap(h0o) + s1[2 * rp + rs:]

    hprev = jnp.concatenate([g1, ho1, h0e, h0o], axis=0).astype(BF16)
    blocks = [slice(c * SSM_T, (c + M_BLOCK_CHANNELS) * SSM_T) for c in range(0, SSM_GROUP, M_BLOCK_CHANNELS)]
    _toeplitz_columns(strip, m_ref, blocks[0])
    for i, cols in enumerate(blocks):
        y = lax.dot_general(hprev, qt_ref[cols, :], (((1,), (1,)), ((), ())), preferred_element_type=F32)
        y += jnp.dot(u, m_ref[:, cols], preferred_element_type=F32)
        if i + 1 < len(blocks):
            _toeplitz_columns(strip, m_ref, blocks[i + 1])
        store_rows(jnp.concatenate([y[:rp], y[2 * rp:2 * rp + rs]], axis=0),
                   jnp.concatenate([y[rp:2 * rp], y[2 * rp + rs:]], axis=0), y_ref, i * M_BLOCK_CHANNELS)


def _ssm(table_params, u3, h0_pack, *, rows_per_seq, prompt_rows):
    groups = u3.shape[0]
    rs = groups * ROW_GROUP - prompt_rows
    u4 = u3.reshape(groups, SSM_WIDTH, ROW_GROUP, ROW_TOKENS)
    assert rows_per_seq & (rows_per_seq - 1) == 0 and 2 * rows_per_seq.bit_length() <= COEF_ROWS
    gps = SSM_GROUPS_PER_STEP
    g3 = lambda g: (g, 0, 0)
    u_spec = pl.BlockSpec((groups, gps * SSM_GROUP, ROW_GROUP, ROW_TOKENS), lambda g: (0, g, 0, 0))
    fin_rows = STATE_OUT_ROWS + 2 * rs
    return pl.pallas_call(
        functools.partial(_ssm_kernel, rows_per_seq=rows_per_seq, prompt_rows=prompt_rows),
        grid=(SSM_GROUPS // gps,),
        in_specs=[
            pl.BlockSpec((gps, PAR_ROWS, STATE_LANES), g3),
            u_spec,
            pl.BlockSpec((gps, 2, rs, STATE_LANES), lambda g: (g, 0, 0, 0)),
        ],
        out_specs=[
            u_spec,
            pl.BlockSpec((gps, fin_rows, STATE_LANES), g3),
        ],
        out_shape=[
            jax.ShapeDtypeStruct(u4.shape, F32),
            jax.ShapeDtypeStruct((SSM_GROUPS, fin_rows, STATE_LANES), F32),
        ],
        scratch_shapes=[
            pltpu.VMEM((gps, SSM_CW, SSM_CW), BF16),
            pltpu.VMEM((gps, SSM_CW, STATE_LANES), BF16),
            pltpu.VMEM((gps, SSM_CW, STATE_LANES), BF16),
            pltpu.VMEM((gps, COEF_ROWS, STATE_LANES), F32),
        ],
        compiler_params=pltpu.CompilerParams(
            dimension_semantics=("arbitrary",), vmem_limit_bytes=V7X_VMEM_LIMIT_BYTES),
        name="ssm",
    )(table_params, u4, h0_pack)


def _post_kernel(xp_ref, xs_ref, atp_ref, ats_ref, ut_ref, yt_ref, *refs, alpha, prompt_tiles):
    weights, (op_ref, os_ref) = refs[:-2], refs[-2:]

    @pl.when(pl.program_id(0) < prompt_tiles)
    def _():
        _post_tile(xp_ref, atp_ref, ut_ref, yt_ref, *weights, op_ref, alpha=alpha)

    @pl.when(pl.program_id(0) >= prompt_tiles)
    def _():
        _post_tile(xs_ref, ats_ref, ut_ref, yt_ref, *weights, os_ref, alpha=alpha)


def _post_tile(x_ref, at_ref, ut_ref, yt_ref, lng_ref, lnb_ref, cols_ref, wglut_ref,
               wout_ref, ln1g_ref, ln1b_ref, wgu_ref, wdown_ref, ln2g_ref, ln2b_ref, o_ref,
               *, alpha):
    xn = _layer_norm(x_ref[...], lng_ref[...], lnb_ref[...])
    n_rows = x_ref.shape[0] // ROW_TOKENS
    first_row = (pl.program_id(0) % (ROW_GROUP // n_rows)) * n_rows

    def feature_major(ref):
        return jnp.concatenate(
            [ref[pl.ds(first_row + j, SSM_WIDTH, stride=ROW_GROUP), :] for j in range(n_rows)], axis=1)

    ys = feature_major(yt_ref) + cols_ref[0] * feature_major(ut_ref)
    gl = 0.5 * ys * (1.0 + lax.erf(ys * math.sqrt(0.5)))
    z = jnp.dot(wglut_ref[...], gl.astype(BF16), preferred_element_type=F32) + cols_ref[1]
    s = gl * jax.nn.sigmoid(z)
    tn = (((0,), (0,)), ((), ()))
    mix = lax.dot_general(at_ref[...], wout_ref[:PROJ_Q, :], tn, preferred_element_type=F32)
    mix += lax.dot_general(s.astype(BF16), wout_ref[PROJ_Q:, :], tn, preferred_element_type=F32)
    h = _layer_norm(alpha * xn + mix, ln1g_ref[...], ln1b_ref[...])
    hb = h.astype(BF16)
    f = jnp.zeros_like(h)
    for start in range(0, D_FF, FF_BLOCK):
        stop = min(start + FF_BLOCK, D_FF)
        cols = slice(start, stop)
        up_cols = slice(D_FF + start, D_FF + stop)
        g = jnp.dot(hb, wgu_ref[:, cols], preferred_element_type=F32)
        up = jnp.dot(hb, wgu_ref[:, up_cols], preferred_element_type=F32)
        act = (g * jax.nn.sigmoid(g)) * up
        f += jnp.dot(act.astype(BF16), wdown_ref[cols, :], preferred_element_type=F32)
    o_ref[...] = _layer_norm(alpha * h + f, ln2g_ref[...], ln2b_ref[...])


def _post(xp2d, xs2d, atp, ats, ut3, yt3, ln_g, ln_b, d_bglu_cols, w_glu_t, w_out, ln1_g, ln1_b,
          w_gate_up, w_down, ln2_g, ln2_b, *, alpha, tm):
    n_p, n_s = xp2d.shape[0], xs2d.shape[0]
    tp, ts = n_p // tm, n_s // tm
    assert PROJ_TILE % tm == 0 and n_p % PROJ_TILE == 0
    group = lambda i: (i // (PROJ_TILE // tm), 0, 0)
    group_spec = pl.BlockSpec((None, SSM_WIDTH * ROW_GROUP, ROW_TOKENS), group)
    prompt = lambda i: jnp.minimum(i, tp - 1)
    sample = lambda i: jnp.maximum(i - tp, 0)
    const = lambda i: (0, 0)
    resident = lambda shape: pl.BlockSpec(shape, const, pipeline_mode=pl.Buffered(1))
    vec = lambda width: pl.BlockSpec((1, width), const)
    columns = pl.BlockSpec((2, SSM_WIDTH, 1), lambda i: (0, 0, 0))
    return pl.pallas_call(
        functools.partial(_post_kernel, alpha=alpha, prompt_tiles=tp),
        grid=(tp + ts,),
        in_specs=[
            pl.BlockSpec((tm, D_MODEL), lambda i: (prompt(i), 0)),
            pl.BlockSpec((tm, D_MODEL), lambda i: (sample(i), 0)),
            pl.BlockSpec((PROJ_Q, tm), lambda i: (0, prompt(i))),
            pl.BlockSpec((PROJ_Q, tm), lambda i: (0, sample(i))),
            group_spec, group_spec,
            vec(D_MODEL), vec(D_MODEL), columns,
            resident((SSM_WIDTH, SSM_WIDTH)),
            resident((D_MODEL, D_MODEL)), vec(D_MODEL), vec(D_MODEL),
            resident((D_MODEL, 2 * D_FF)), resident((D_FF, D_MODEL)),
            vec(D_MODEL), vec(D_MODEL),
        ],
        out_specs=[
            pl.BlockSpec((tm, D_MODEL), lambda i: (prompt(i), 0)),
            pl.BlockSpec((tm, D_MODEL), lambda i: (sample(i), 0)),
        ],
        out_shape=[
            jax.ShapeDtypeStruct((n_p, D_MODEL), F32),
            jax.ShapeDtypeStruct((n_s, D_MODEL), F32),
        ],
        compiler_params=pltpu.CompilerParams(
            dimension_semantics=("arbitrary",), vmem_limit_bytes=V7X_VMEM_LIMIT_BYTES),
        name="post",
    )(xp2d, xs2d, atp, ats, ut3, yt3, ln_g, ln_b, d_bglu_cols, w_glu_t, w_out, ln1_g, ln1_b,
      w_gate_up, w_down, ln2_g, ln2_b)


def kernel(x_prompt, x_sample, cache_win_k, cache_win_v, state_ssm_re, state_ssm_im,
           ln_in_g, ln_in_b, w_in, attn_sinks, ssm_lambda_re, ssm_lambda_im, ssm_log_step,
           ssm_b_re, ssm_b_im, ssm_c_re, ssm_c_im, ssm_d, w_glu, b_glu, w_out,
           ln1_g, ln1_b, w_gate_up, w_down, ln2_g, ln2_b):
    depth = w_in.shape[0]
    assert depth == 1, "single-layer step"
    bp, lp, _ = x_prompt.shape
    bs, ls, _ = x_sample.shape
    assert ls == SSM_T and bs % 2 == 0 and lp % 512 == 0 and bp <= STATE_OUT_ROWS
    win_rows = cache_win_k.shape[2]
    assert win_rows == WINDOW
    alpha = (2.0 * depth) ** 0.25
    l = 0
    row = lambda a: a.reshape(1, -1)

    lng, lnb = row(ln_in_g), row(ln_in_b)
    xp2 = x_prompt.reshape(bp * lp, D_MODEL)
    xs2 = x_sample.reshape(bs * ls, D_MODEL)
    n_p = bp * lp
    qt, vt, u3, k, v = _proj(xp2, xs2, lng, lnb, w_in[l][:, PROJ_Q:PROJ_Q + KV].astype(BF16),
                             w_in[l].T.astype(BF16))

    atp, w_gu_b, w_down_b, w_out_b = _attention_prompt(
        attn_sinks, qt, k, vt, bp, lp, n_pairs=ATTN_PAIRS, f32_weights=(w_gate_up[l], w_down[l], w_out[l]))
    ck = cache_win_k[l].reshape(bs, win_rows, PROJ_KV)
    ckt = jnp.swapaxes(ck, 1, 2)
    cvt = jnp.swapaxes(cache_win_v[l].reshape(bs, win_rows, PROJ_KV), 1, 2)
    ats = _attention_sample(attn_sinks, qt, ck, k, cvt, vt, first_token=n_p, n_pairs=ATTN_PAIRS // 2)
    windows = _windows(k, v, ckt, cvt, bp, lp)

    table_params = _table_params(
        ssm_lambda_re[l], ssm_lambda_im[l], ssm_log_step[l],
        ssm_b_re[l], ssm_b_im[l], ssm_c_re[l], ssm_c_im[l])
    h0 = jnp.concatenate([state_ssm_re[l], state_ssm_im[l]], axis=-1)
    h0_pack = jnp.transpose(h0.reshape(bs // 2, 2, SSM_GROUPS, STATE_LANES), (2, 1, 0, 3))
    y4, hfin = _ssm(table_params, u3, h0_pack,
                    rows_per_seq=lp // ROW_TOKENS, prompt_rows=n_p // ROW_TOKENS)

    post_args = (lng, lnb, jnp.stack([ssm_d[l], b_glu[l]])[:, :, None], w_glu[l].T.astype(BF16),
                 w_out_b, row(ln1_g[l]), row(ln1_b[l]), w_gu_b, w_down_b, row(ln2_g[l]), row(ln2_b[l]))
    out_p, out_s = _post(xp2, xs2, atp, ats, u3, y4.reshape(u3.shape), *post_args, alpha=alpha, tm=512)

    win_k_p, win_v_p, win_k_s, win_v_s = (
        jnp.swapaxes(w, 1, 2).reshape(1, -1, win_rows, N_KV_HEADS, HEAD_DIM) for w in windows)
    sp = jnp.swapaxes(hfin[:, :bp], 0, 1)
    ss = jnp.transpose(hfin[:, STATE_OUT_ROWS:].reshape(SSM_GROUPS, 2, bs // 2, STATE_LANES),
                       (2, 1, 0, 3)).reshape(bs, SSM_GROUPS, STATE_LANES)
    return (out_p.reshape(bp, lp, D_MODEL), out_s.reshape(bs, ls, D_MODEL),
            win_k_p, win_v_p, sp[None, ..., :SSM_STATE], sp[None, ..., SSM_STATE:],
            win_k_s, win_v_s, ss[None, ..., :SSM_STATE], ss[None, ..., SSM_STATE:])
```

```python
import functools
import math

import jax
import jax.numpy as jnp
from jax import lax
from jax.experimental import pallas as pl
from jax.experimental.pallas import tpu as pltpu

F32 = jnp.float32
BF16 = jnp.bfloat16

D_MODEL = 1024
HEAD_DIM = 64
N_HEADS = 8
N_KV_HEADS = 2
Q_PER_KV = N_HEADS // N_KV_HEADS
CHUNK = 64
WINDOW = 128
WIN_CHUNKS = WINDOW // CHUNK
BAND = (WIN_CHUNKS + 1) * CHUNK
PROJ_Q = N_HEADS * HEAD_DIM
PROJ_KV = N_KV_HEADS * HEAD_DIM
SSM_WIDTH = 512
SSM_GROUP = 16
SSM_GROUPS = SSM_WIDTH // SSM_GROUP
SSM_STATE = 64
STATE_LANES = 2 * SSM_STATE
D_FF = 2816
D_IN_PROJ = PROJ_Q + 2 * PROJ_KV + SSM_WIDTH
LN_EPS = 1e-5
NEG_INF = -1e30

SSM_T = CHUNK
SSM_CW = SSM_T * SSM_GROUP
ROW_TOKENS = 2 * SSM_T
ROW_GROUP = 8
PROJ_TILE = ROW_GROUP * ROW_TOKENS
PROJ_PARTS = 2
COEF_ROWS = 16
STATE_OUT_ROWS = 8
M_BLOCK_CHANNELS = 4
SSM_GROUPS_PER_STEP = 2
FF_BLOCK = 256
V7X_VMEM_LIMIT_BYTES = 56 * 1024 * 1024


def _layer_norm(x, g, b):
    mu = jnp.mean(x, axis=-1, keepdims=True)
    xc = x - mu
    var = jnp.mean(xc * xc, axis=-1, keepdims=True)
    return xc * lax.rsqrt(var + LN_EPS) * g + b


KV = 2 * PROJ_KV
VU = PROJ_KV + SSM_WIDTH


def _proj_kernel(xp_ref, xs_ref, g_ref, b_ref, w_ref, wqt_ref, wvut_ref, qt_ref, vt_ref, ut_ref, k_ref, v_ref,
                 *, prompt_tiles):
    outs = (g_ref, b_ref, w_ref, wqt_ref, wvut_ref, qt_ref, vt_ref, ut_ref, k_ref, v_ref)

    @pl.when(pl.program_id(0) < prompt_tiles)
    def _():
        _proj_tile(xp_ref, *outs)

    @pl.when(pl.program_id(0) >= prompt_tiles)
    def _():
        _proj_tile(xs_ref, *outs)


def _proj_tile(x_ref, g_ref, b_ref, w_ref, wqt_ref, wvut_ref, qt_ref, vt_ref, ut_ref, k_ref, v_ref):
    rows_per_part = ROW_GROUP // PROJ_PARTS
    nt = (((1,), (1,)), ((), ()))
    for part in range(PROJ_PARTS):
        tok = slice(part * rows_per_part * ROW_TOKENS, (part + 1) * rows_per_part * ROW_TOKENS)
        xb = _layer_norm(x_ref[tok, :], g_ref[...], b_ref[...]).astype(BF16)
        p = jnp.dot(xb, w_ref[...], preferred_element_type=F32)
        k_ref[tok, :] = p[:, :PROJ_KV]
        v_ref[tok, :] = p[:, PROJ_KV:]
        qt = lax.dot_general(wqt_ref[...], xb, nt, preferred_element_type=F32)
        vut = lax.dot_general(wvut_ref[...], xb, nt, preferred_element_type=F32)
        qt_ref[:, tok] = (qt * (HEAD_DIM ** -0.5)).astype(BF16)
        vt_ref[:, tok] = vut[:PROJ_KV].astype(BF16)
        for j in range(rows_per_part):
            ut_ref[pl.ds(part * rows_per_part + j, SSM_WIDTH, stride=ROW_GROUP), :] = (
                vut[PROJ_KV:, j * ROW_TOKENS:(j + 1) * ROW_TOKENS])


def _proj(xp2d, xs2d, ln_g, ln_b, w_kv, w_in_t):
    assert D_IN_PROJ == 2 * VU
    tm = PROJ_TILE
    tp, ts = xp2d.shape[0] // tm, xs2d.shape[0] // tm
    n = (tp + ts) * tm
    const = lambda i: (0, 0)
    row = lambda i: (i, 0)
    col = lambda i: (0, i)
    return pl.pallas_call(
        functools.partial(_proj_kernel, prompt_tiles=tp),
        grid=(tp + ts,),
        in_specs=[
            pl.BlockSpec((tm, D_MODEL), lambda i: (jnp.minimum(i, tp - 1), 0)),
            pl.BlockSpec((tm, D_MODEL), lambda i: (jnp.maximum(i - tp, 0), 0)),
            pl.BlockSpec((1, D_MODEL), const),
            pl.BlockSpec((1, D_MODEL), const),
            pl.BlockSpec((D_MODEL, KV), const),
            pl.BlockSpec((PROJ_Q, D_MODEL), const),
            pl.BlockSpec((VU, D_MODEL), lambda i: (1, 0)),
        ],
        out_specs=[
            pl.BlockSpec((PROJ_Q, tm), col),
            pl.BlockSpec((PROJ_KV, tm), col),
            pl.BlockSpec((None, SSM_WIDTH * ROW_GROUP, ROW_TOKENS), lambda i: (i, 0, 0)),
            pl.BlockSpec((tm, PROJ_KV), row),
            pl.BlockSpec((tm, PROJ_KV), row),
        ],
        out_shape=[
            jax.ShapeDtypeStruct((PROJ_Q, n), BF16),
            jax.ShapeDtypeStruct((PROJ_KV, n), BF16),
            jax.ShapeDtypeStruct((n // tm, SSM_WIDTH * ROW_GROUP, ROW_TOKENS), F32),
            jax.ShapeDtypeStruct((n, PROJ_KV), F32),
            jax.ShapeDtypeStruct((n, PROJ_KV), F32),
        ],
        compiler_params=pltpu.CompilerParams(
            dimension_semantics=("arbitrary",), vmem_limit_bytes=V7X_VMEM_LIMIT_BYTES),
        name="proj",
    )(xp2d, xs2d, ln_g, ln_b, w_kv, w_in_t, w_in_t)


PAIR = 2 * CHUNK
HEAD_LANES = Q_PER_KV * PAIR
ONES_ROWS = 16
ATTN_PAIRS = 16
BF16_TILE_ROWS = 16


def _scores(unit, qt_ref):
    kwin, _, lanes, _, _, h = unit
    base = h * Q_PER_KV * HEAD_DIM
    qrow = jnp.concatenate(
        [qt_ref[base + g * HEAD_DIM:base + (g + 1) * HEAD_DIM, lanes] for g in range(Q_PER_KV)],
        axis=1)
    zero = jnp.zeros_like(qrow)
    qstack = jnp.concatenate([qrow, zero] if h == 0 else [zero, qrow], axis=0)
    return jnp.dot(kwin, qstack, preferred_element_type=F32)


def _sink_rows(sinks_ref):
    tile = lax.broadcasted_iota(jnp.int32, (1, HEAD_LANES), 1) // PAIR
    rows = []
    for h in range(N_KV_HEADS):
        row = jnp.full((1, HEAD_LANES), sinks_ref[0, h * Q_PER_KV], F32)
        for g in range(1, Q_PER_KV):
            row = jnp.where(tile == g, sinks_ref[0, h * Q_PER_KV + g], row)
        rows.append(row)
    return rows


def _finish(unit, s, sinks, o_ref):
    _, vtwin, lanes, valid, masked_rows, h = unit
    nk = s.shape[0]
    base = h * Q_PER_KV * HEAD_DIM
    pieces, done = [], 0
    for start, stop in masked_rows:
        if start > done:
            pieces.append(s[done:start])
        pieces.append(jnp.where(valid[start:stop], s[start:stop], NEG_INF))
        done = stop
    if done < nk:
        pieces.append(s[done:])
    s = jnp.concatenate(pieces, axis=0)
    sink = sinks[h]
    m = jnp.maximum(jnp.max(s, axis=0, keepdims=True), sink)
    p = jnp.exp(s - m).astype(BF16)
    v_ones = jnp.concatenate(
        [vtwin[h * HEAD_DIM:(h + 1) * HEAD_DIM, :], jnp.ones((ONES_ROWS, nk), BF16)], axis=0)
    ov = jnp.dot(v_ones, p, preferred_element_type=F32)
    den = ov[HEAD_DIM:HEAD_DIM + 1, :] + jnp.exp(sink - m)
    o = ov[:HEAD_DIM, :] * (1.0 / den)
    for g in range(Q_PER_KV):
        o_ref[base + g * HEAD_DIM:base + (g + 1) * HEAD_DIM, lanes] = (
            o[:, g * PAIR:(g + 1) * PAIR].astype(BF16))


def _attend_units(units, qt_ref, sinks_ref, o_ref):
    sinks = _sink_rows(sinks_ref)
    s_next = _scores(units[0], qt_ref)
    for i, unit in enumerate(units):
        s = s_next
        if i + 1 < len(units):
            s_next = _scores(units[i + 1], qt_ref)
        _finish(unit, s, sinks, o_ref)


def _attn_prompt_kernel(sinks_ref, qt_ref, kp_ref, kc_ref, vtp_ref, vtc_ref, *refs, n_pairs):
    n_cast = (len(refs) - 1) // 2
    o_ref = refs[n_cast]
    for w_ref, wb_ref in zip(refs[:n_cast], refs[n_cast + 1:]):
        wb_ref[...] = w_ref[...].astype(BF16)
    tq = n_pairs * PAIR
    nk = WINDOW + PAIR
    kk = jnp.concatenate([kp_ref[...], kc_ref[...]], axis=0).astype(BF16)
    vt = jnp.concatenate([vtp_ref[...], vtc_ref[...]], axis=1)
    r = lax.broadcasted_iota(jnp.int32, (nk, HEAD_LANES), 0)
    first_chunk = (lax.broadcasted_iota(jnp.int32, (nk, HEAD_LANES), 1) & (PAIR - 1)) < CHUNK
    lo = jnp.where(first_chunk, 0, CHUNK)
    hi = jnp.where(first_chunk, BAND, nk)
    first_pos = pl.program_id(1) * tq - WINDOW
    units = []
    for pp in range(n_pairs):
        lo_pp = jnp.maximum(lo, -first_pos) if pp == 0 else lo
        valid = (r >= lo_pp) & (r < hi)
        masked_rows = ((0, nk),) if pp == 0 else ((0, CHUNK), (nk - CHUNK, nk))
        for h in range(N_KV_HEADS):
            units.append((kk[pp * PAIR:pp * PAIR + nk], vt[:, pp * PAIR:pp * PAIR + nk],
                          slice(pp * PAIR, (pp + 1) * PAIR), valid, masked_rows, h))
    _attend_units(units, qt_ref, sinks_ref, o_ref)


def _attn_sample_kernel(sinks_ref, qt_ref, ck_ref, kn_ref, cvt_ref, vtn_ref, o_ref, *, n_pairs):
    nk = 2 * WINDOW + PAIR
    r = lax.broadcasted_iota(jnp.int32, (nk, HEAD_LANES), 0)
    query_seq = (lax.broadcasted_iota(jnp.int32, (nk, HEAD_LANES), 1) & (PAIR - 1)) >> 6
    key_seq = jnp.where(r < 2 * WINDOW, r >> 7, (r - 2 * WINDOW) >> 6)
    valid = query_seq == key_seq
    units = []
    for pp in range(n_pairs):
        lanes = slice(pp * PAIR, (pp + 1) * PAIR)
        kwin = jnp.concatenate([ck_ref[2 * pp], ck_ref[2 * pp + 1], kn_ref[lanes, :]],
                               axis=0).astype(BF16)
        vtwin = jnp.concatenate([cvt_ref[2 * pp].astype(BF16), cvt_ref[2 * pp + 1].astype(BF16),
                                 vtn_ref[:, lanes]], axis=1)
        units.extend((kwin, vtwin, lanes, valid, ((0, nk),), h) for h in range(N_KV_HEADS))
    _attend_units(units, qt_ref, sinks_ref, o_ref)


_SINK_SPEC = pl.BlockSpec(memory_space=pltpu.SMEM)


def _attention_prompt(sinks, qt, k, vt, bsz, seq, n_pairs, f32_weights):
    tq = n_pairs * PAIR
    nt = seq // tq
    steps = bsz * nt
    wpt = tq // WINDOW
    cur_c = lambda b, i: (0, b * nt + i)
    cur_r = lambda b, i: (b * nt + i, 0)
    prev = lambda b, i: jnp.maximum((b * nt + i) * wpt - 1, 0)
    assert all(w.shape[0] % (steps * BF16_TILE_ROWS) == 0 for w in f32_weights)
    cast_specs = [pl.BlockSpec((w.shape[0] // steps, w.shape[1]), cur_r) for w in f32_weights]
    return pl.pallas_call(
        functools.partial(_attn_prompt_kernel, n_pairs=n_pairs),
        grid=(bsz, nt),
        in_specs=[
            _SINK_SPEC,
            pl.BlockSpec((PROJ_Q, tq), cur_c),
            pl.BlockSpec((WINDOW, PROJ_KV), lambda b, i: (prev(b, i), 0)),
            pl.BlockSpec((tq, PROJ_KV), cur_r),
            pl.BlockSpec((PROJ_KV, WINDOW), lambda b, i: (0, prev(b, i))),
            pl.BlockSpec((PROJ_KV, tq), cur_c),
            *cast_specs,
        ],
        out_specs=[pl.BlockSpec((PROJ_Q, tq), cur_c), *cast_specs],
        out_shape=[jax.ShapeDtypeStruct((PROJ_Q, bsz * seq), BF16),
                   *[jax.ShapeDtypeStruct(w.shape, BF16) for w in f32_weights]],
        compiler_params=pltpu.CompilerParams(
            dimension_semantics=("arbitrary", "arbitrary"), vmem_limit_bytes=V7X_VMEM_LIMIT_BYTES),
        name="attn_prompt",
    )(sinks, qt, k, k, vt, vt, *f32_weights)


def _attention_sample(sinks, qt, cache_k, k, cache_vt, vt, first_token, n_pairs):
    n = cache_k.shape[0] * CHUNK
    tq = n_pairs * PAIR
    assert first_token % tq == 0
    off = first_token // tq
    return pl.pallas_call(
        functools.partial(_attn_sample_kernel, n_pairs=n_pairs),
        grid=(n // tq,),
        in_specs=[
            _SINK_SPEC,
            pl.BlockSpec((PROJ_Q, tq), lambda i: (0, i + off)),
            pl.BlockSpec((2 * n_pairs, WINDOW, PROJ_KV), lambda i: (i, 0, 0)),
            pl.BlockSpec((tq, PROJ_KV), lambda i: (i + off, 0)),
            pl.BlockSpec((2 * n_pairs, PROJ_KV, WINDOW), lambda i: (i, 0, 0)),
            pl.BlockSpec((PROJ_KV, tq), lambda i: (0, i + off)),
        ],
        out_specs=pl.BlockSpec((PROJ_Q, tq), lambda i: (0, i)),
        out_shape=jax.ShapeDtypeStruct((PROJ_Q, n), BF16),
        compiler_params=pltpu.CompilerParams(dimension_semantics=("arbitrary",)),
        name="attn_sample",
    )(sinks, qt, cache_k, k, cache_vt, vt)


def _windows_kernel(kp_ref, vp_ref, kn_ref, vn_ref, ckt_ref, cvt_ref, wkp_ref, wvp_ref, wks_ref, wvs_ref):
    first_half = lax.broadcasted_iota(jnp.int32, (PROJ_KV, WINDOW), 1) < CHUNK
    for new_p, new_s, cache_t, win_p, win_s in ((kp_ref, kn_ref, ckt_ref, wkp_ref, wks_ref),
                                                (vp_ref, vn_ref, cvt_ref, wvp_ref, wvs_ref)):
        win_p[0] = new_p[...].T
        for pair in range(cache_t.shape[0] // 2):
            new_t = new_s[pair * PAIR:(pair + 1) * PAIR, :].T
            swapped = pltpu.roll(new_t, CHUNK, 1)
            for j, new in enumerate((swapped, new_t)):
                old = pltpu.roll(cache_t[2 * pair + j], CHUNK, 1)
                win_s[2 * pair + j] = jnp.where(first_half, old, new)


def _windows(k, v, cache_kt, cache_vt, bp, seq):
    bs = cache_kt.shape[0]
    assert bs % (2 * bp) == 0 and seq % WINDOW == 0
    share = bs // bp
    assert (bp * seq) % (share * CHUNK) == 0
    first_share = bp * seq // (share * CHUNK)
    last_window = pl.BlockSpec((WINDOW, PROJ_KV), lambda b: ((b + 1) * (seq // WINDOW) - 1, 0))
    new_rows = pl.BlockSpec((share * CHUNK, PROJ_KV), lambda b: (first_share + b, 0))
    one = pl.BlockSpec((1, PROJ_KV, WINDOW), lambda b: (b, 0, 0))
    many = pl.BlockSpec((share, PROJ_KV, WINDOW), lambda b: (b, 0, 0))
    return pl.pallas_call(
        _windows_kernel,
        grid=(bp,),
        in_specs=[last_window, last_window, new_rows, new_rows, many, many],
        out_specs=[one, one, many, many],
        out_shape=[jax.ShapeDtypeStruct((bp, PROJ_KV, WINDOW), F32)] * 2
        + [jax.ShapeDtypeStruct((bs, PROJ_KV, WINDOW), F32)] * 2,
        compiler_params=pltpu.CompilerParams(dimension_semantics=("arbitrary",)),
        name="windows",
    )(k, v, k, v, cache_kt, cache_vt)


def _power_table(tau, nbits, a_re, a_im):
    rows = tau.shape[0]
    w_re = jnp.ones((rows, STATE_LANES), F32)
    w_im = jnp.zeros((rows, STATE_LANES), F32)
    p_re, p_im = a_re, a_im
    for k in range(nbits):
        bit = ((tau >> k) & 1) == 1
        f_re = jnp.where(bit, p_re, 1.0)
        f_im = jnp.where(bit, p_im, 0.0)
        w_re, w_im = w_re * f_re - w_im * f_im, w_re * f_im + w_im * f_re
        p_re, p_im = p_re * p_re - p_im * p_im, 2.0 * p_re * p_im
    return w_re, w_im


def _power_rows(a_re, a_im, descending):
    t = lax.broadcasted_iota(jnp.int32, (ROW_GROUP, 1), 0)
    w_re, w_im = _power_table(ROW_GROUP - 1 - t if descending else t, 3, a_re, a_im)
    p_re, p_im = a_re, a_im
    for _ in range(3):
        p_re, p_im = p_re * p_re - p_im * p_im, 2.0 * p_re * p_im
    while w_re.shape[0] < SSM_T:
        n_re, n_im = w_re * p_re - w_im * p_im, w_re * p_im + w_im * p_re
        w_re = jnp.concatenate([n_re, w_re] if descending else [w_re, n_re], axis=0)
        w_im = jnp.concatenate([n_im, w_im] if descending else [w_im, n_im], axis=0)
        p_re, p_im = p_re * p_re - p_im * p_im, 2.0 * p_re * p_im
    return w_re, w_im


def _build_tables(par_ref, p_ref, qt_ref, coef_ref):
    lo = lax.broadcasted_iota(jnp.int32, (1, STATE_LANES), 1) < SSM_STATE
    mat = lambda i: par_ref[PAR_MATS + i * SSM_GROUP:PAR_MATS + (i + 1) * SSM_GROUP, :]
    lr, li = par_ref[0:1, :], par_ref[1:2, :]
    dt = jnp.exp(par_ref[2:3, :])
    mag = jnp.exp(lr * dt)
    a_re, a_im = mag * jnp.cos(li * dt), mag * jnp.sin(li * dt)
    nr, ni = a_re - 1.0, a_im
    den = lr * lr + li * li
    f_re, f_im = (nr * lr + ni * li) / den, (ni * lr - nr * li) / den
    b_re, b_im = mat(0), mat(1)
    bb_re = f_re * b_re - f_im * b_im
    bb_im = f_re * b_im + f_im * b_re
    c_re, c_im = mat(2), mat(3)

    w_re, w_im = _power_rows(a_re, a_im, descending=False)
    w1_re, w1_im = w_re * a_re - w_im * a_im, w_re * a_im + w_im * a_re
    wr_re, wr_im = _power_rows(a_re, a_im, descending=True)

    def outer(c, w):
        return (c[:, None, :] * w[None, :, :]).reshape(SSM_CW, STATE_LANES)

    cw_mix = (outer(c_re, jnp.where(lo, w_re, w_im)) + outer(c_im, jnp.where(lo, -w_im, w_re)))
    bb_mix = jnp.where(lo, bb_re, -bb_im)
    strip = lax.dot_general(bb_mix, cw_mix, (((1,), (1,)), ((), ())),
                            precision=lax.Precision.HIGHEST,
                            preferred_element_type=F32)

    x_a, y_a = jnp.where(lo, bb_re, bb_im), jnp.where(lo, -bb_im, bb_re)
    p_ref[...] = (outer(x_a, wr_re) + outer(y_a, wr_im)).astype(BF16)

    qt_ref[...] = (outer(c_re, jnp.where(lo, w1_re, -w1_im))
                   + outer(c_im, jnp.where(lo, -w1_im, -w1_re))).astype(BF16)

    t_re, t_im = a_re, a_im
    for _ in range(6):
        t_re, t_im = t_re * t_re - t_im * t_im, 2.0 * t_re * t_im
    for k in range(COEF_ROWS // 2):
        coef_ref[2 * k:2 * k + 1, :] = t_re
        coef_ref[2 * k + 1:2 * k + 2, :] = jnp.where(lo, -t_im, t_im)
        t_re, t_im = t_re * t_re - t_im * t_im, 2.0 * t_re * t_im
    return strip


def _toeplitz_columns(strip, m_ref, cols):
    width = cols.stop - cols.start
    s_idx = lax.broadcasted_iota(jnp.int32, (SSM_T, width), 0)
    t_idx = lax.broadcasted_iota(jnp.int32, (SSM_T, width), 1) & (SSM_T - 1)
    causal = t_idx >= s_idx
    for c in range(SSM_GROUP):
        rows = jnp.broadcast_to(strip[c:c + 1, cols], (SSM_T, width))
        shifted = pltpu.roll(rows, 0, 1, stride=1, stride_axis=0)
        m_ref[c * SSM_T:(c + 1) * SSM_T, cols] = jnp.where(causal, shifted, 0.0).astype(BF16)


PAR_MATS = 8
PAR_ROWS = PAR_MATS + 4 * SSM_GROUP


def _table_params(lam_re, lam_im, log_step, b_re, b_im, c_re, c_im):
    step = jnp.broadcast_to(log_step[:, None, None], (SSM_GROUPS, 1, SSM_STATE))
    filler = jnp.zeros((SSM_GROUPS, PAR_MATS - 3, SSM_STATE), F32)
    pack = jnp.concatenate([lam_re[:, None, :], lam_im[:, None, :], step, filler,
                            jnp.swapaxes(b_re, 1, 2), jnp.swapaxes(b_im, 1, 2), c_re, c_im], axis=1)
    return jnp.concatenate([pack, pack], axis=-1)


def _cmul(a1, a2, h, hs):
    return a1 * h + a2 * hs, a1 * hs - a2 * h


def _ssm_kernel(par_ref, u_ref, h0_ref, y_ref, hfin_ref, m_ref, p_ref, qt_ref, coef_ref,
                *, rows_per_seq, prompt_rows):
    for gi in range(SSM_GROUPS_PER_STEP):
        _ssm_group(par_ref.at[gi], u_ref, gi * SSM_GROUP, h0_ref.at[gi], y_ref, hfin_ref.at[gi],
                   m_ref.at[gi], p_ref.at[gi], qt_ref.at[gi], coef_ref.at[gi],
                   rows_per_seq=rows_per_seq, prompt_rows=prompt_rows)


def _ssm_group(par_ref, u_ref, c0, h0_ref, y_ref, hfin_ref, m_ref, p_ref, qt_ref, coef_ref,
               *, rows_per_seq, prompt_rows):
    strip = _build_tables(par_ref, p_ref, qt_ref, coef_ref)
    lo = lax.broadcasted_iota(jnp.int32, (1, ROW_TOKENS), 1) < SSM_T

    def chunk_rows(ref):
        even, odd = [], []
        rows = ref.shape[0] * ROW_GROUP
        for k in range(SSM_GROUP // 2):
            a = ref[:, c0 + 2 * k].reshape(rows, ROW_TOKENS)
            b = ref[:, c0 + 2 * k + 1].reshape(rows, ROW_TOKENS)
            even.append(jnp.where(lo, a, pltpu.roll(b, SSM_T, 1)))
            odd.append(jnp.where(lo, pltpu.roll(a, SSM_T, 1), b))
        return jnp.concatenate(even, axis=1), jnp.concatenate(odd, axis=1)

    def store_rows(y_even, y_odd, ref, first_channel):
        for k in range(y_even.shape[1] // ROW_TOKENS):
            te = y_even[:, k * ROW_TOKENS:(k + 1) * ROW_TOKENS]
            to = y_odd[:, k * ROW_TOKENS:(k + 1) * ROW_TOKENS]
            tiles = (ref.shape[0], ROW_GROUP, ROW_TOKENS)
            c = c0 + first_channel + 2 * k
            ref[:, c] = jnp.where(lo, te, pltpu.roll(to, SSM_T, 1)).reshape(tiles)
            ref[:, c + 1] = jnp.where(lo, pltpu.roll(te, SSM_T, 1), to).reshape(tiles)

    even, odd = chunk_rows(u_ref)
    rp, rs = prompt_rows, even.shape[0] - prompt_rows
    u = jnp.concatenate([even[:rp], odd[:rp], even[rp:], odd[rp:]], axis=0).astype(BF16)
    s1 = jnp.dot(u, p_ref[...], preferred_element_type=F32)
    s2 = pltpu.roll(s1, SSM_STATE, 1)
    a1, a2 = coef_ref[0:1, :], coef_ref[1:2, :]

    e1, e2, o1, o2 = s1[:rp], s2[:rp], s1[rp:2 * rp], s2[rp:2 * rp]
    x1, x2 = _cmul(a1, a2, e1, e2)
    x1, x2 = x1 + o1, x2 + o2
    pos = lax.broadcasted_iota(jnp.int32, (rp, STATE_LANES), 0) & (rows_per_seq - 1)
    for k in range(rows_per_seq.bit_length() - 1):
        d = 1 << k
        b1, b2 = coef_ref[2 + 2 * k:3 + 2 * k, :], coef_ref[3 + 2 * k:4 + 2 * k, :]
        sh1 = jnp.where(pos >= d, pltpu.roll(x1, d, 0), 0.0)
        sh2 = jnp.where(pos >= d, pltpu.roll(x2, d, 0), 0.0)
        y1, y2 = _cmul(b1, b2, sh1, sh2)
        x1, x2 = x1 + y1, x2 + y2
    g1 = jnp.where(pos >= 1, pltpu.roll(x1, 1, 0), 0.0)
    g2 = jnp.where(pos >= 1, pltpu.roll(x2, 1, 0), 0.0)
    ho1 = _cmul(a1, a2, g1, g2)[0] + e1
    hfin_ref[0:STATE_OUT_ROWS, :] = jnp.zeros((STATE_OUT_ROWS, STATE_LANES), F32)
    for b in range(rp // rows_per_seq):
        last = (b + 1) * rows_per_seq - 1
        hfin_ref[b:b + 1, :] = x1[last:last + 1, :]

    h0e, h0o = h0_ref[0], h0_ref[1]
    swap = lambda h: pltpu.roll(h, SSM_STATE, 1)
    hfin_ref[STATE_OUT_ROWS:STATE_OUT_ROWS + rs, :] = a1 * h0e + a2 * swap(h0e) + s1[2 * rp:2 * rp + rs]
    hfin_ref[STATE_OUT_ROWS + rs:, :] = a1 * h0o + a2 * swap(h0o) + s1[2 * rp + rs:]

    hprev = jnp.concatenate([g1, ho1, h0e, h0o], axis=0).astype(BF16)
    blocks = [slice(c * SSM_T, (c + M_BLOCK_CHANNELS) * SSM_T) for c in range(0, SSM_GROUP, M_BLOCK_CHANNELS)]
    _toeplitz_columns(strip, m_ref, blocks[0])
    for i, cols in enumerate(blocks):
        y = lax.dot_general(hprev, qt_ref[cols, :], (((1,), (1,)), ((), ())), preferred_element_type=F32)
        y += jnp.dot(u, m_ref[:, cols], preferred_element_type=F32)
        if i + 1 < len(blocks):
            _toeplitz_columns(strip, m_ref, blocks[i + 1])
        store_rows(jnp.concatenate([y[:rp], y[2 * rp:2 * rp + rs]], axis=0),
                   jnp.concatenate([y[rp:2 * rp], y[2 * rp + rs:]], axis=0), y_ref, i * M_BLOCK_CHANNELS)


def _ssm(table_params, u3, h0_pack, *, rows_per_seq, prompt_rows):
    groups = u3.shape[0]
    rs = groups * ROW_GROUP - prompt_rows
    u4 = u3.reshape(groups, SSM_WIDTH, ROW_GROUP, ROW_TOKENS)
    assert rows_per_seq & (rows_per_seq - 1) == 0 and 2 * rows_per_seq.bit_length() <= COEF_ROWS
    gps = SSM_GROUPS_PER_STEP
    g3 = lambda g: (g, 0, 0)
    u_spec = pl.BlockSpec((groups, gps * SSM_GROUP, ROW_GROUP, ROW_TOKENS), lambda g: (0, g, 0, 0))
    fin_rows = STATE_OUT_ROWS + 2 * rs
    return pl.pallas_call(
        functools.partial(_ssm_kernel, rows_per_seq=rows_per_seq, prompt_rows=prompt_rows),
        grid=(SSM_GROUPS // gps,),
        in_specs=[
            pl.BlockSpec((gps, PAR_ROWS, STATE_LANES), g3),
            u_spec,
            pl.BlockSpec((gps, 2, rs, STATE_LANES), lambda g: (g, 0, 0, 0)),
        ],
        out_specs=[
            u_spec,
            pl.BlockSpec((gps, fin_rows, STATE_LANES), g3),
        ],
        out_shape=[
            jax.ShapeDtypeStruct(u4.shape, F32),
            jax.ShapeDtypeStruct((SSM_GROUPS, fin_rows, STATE_LANES), F32),
        ],
        scratch_shapes=[
            pltpu.VMEM((gps, SSM_CW, SSM_CW), BF16),
            pltpu.VMEM((gps, SSM_CW, STATE_LANES), BF16),
            pltpu.VMEM((gps, SSM_CW, STATE_LANES), BF16),
            pltpu.VMEM((gps, COEF_ROWS, STATE_LANES), F32),
        ],
        compiler_params=pltpu.CompilerParams(
            dimension_semantics=("arbitrary",), vmem_limit_bytes=V7X_VMEM_LIMIT_BYTES),
        name="ssm",
    )(table_params, u4, h0_pack)


def _post_kernel(xp_ref, xs_ref, atp_ref, ats_ref, ut_ref, yt_ref, *refs, alpha, prompt_tiles):
    weights, (op_ref, os_ref) = refs[:-2], refs[-2:]

    @pl.when(pl.program_id(0) < prompt_tiles)
    def _():
        _post_tile(xp_ref, atp_ref, ut_ref, yt_ref, *weights, op_ref, alpha=alpha)

    @pl.when(pl.program_id(0) >= prompt_tiles)
    def _():
        _post_tile(xs_ref, ats_ref, ut_ref, yt_ref, *weights, os_ref, alpha=alpha)


def _post_tile(x_ref, at_ref, ut_ref, yt_ref, lng_ref, lnb_ref, cols_ref, wglut_ref,
               wout_ref, ln1g_ref, ln1b_ref, wgu_ref, wdown_ref, ln2g_ref, ln2b_ref, o_ref,
               *, alpha):
    xn = _layer_norm(x_ref[...], lng_ref[...], lnb_ref[...])
    n_rows = x_ref.shape[0] // ROW_TOKENS
    first_row = (pl.program_id(0) % (ROW_GROUP // n_rows)) * n_rows

    def feature_major(ref):
        return jnp.concatenate(
            [ref[pl.ds(first_row + j, SSM_WIDTH, stride=ROW_GROUP), :] for j in range(n_rows)], axis=1)

    widen = lambda col: jnp.tile(col, (1, n_rows))
    ys = feature_major(yt_ref) + widen(cols_ref[0]) * feature_major(ut_ref)
    gl = 0.5 * ys * (1.0 + lax.erf(ys * math.sqrt(0.5)))
    z = jnp.dot(wglut_ref[...], gl.astype(BF16), preferred_element_type=F32) + widen(cols_ref[1])
    s = gl * jax.nn.sigmoid(z)
    tn = (((0,), (0,)), ((), ()))
    mix = lax.dot_general(at_ref[...], wout_ref[:PROJ_Q, :], tn, preferred_element_type=F32)
    mix += lax.dot_general(s.astype(BF16), wout_ref[PROJ_Q:, :], tn, preferred_element_type=F32)
    h = _layer_norm(alpha * xn + mix, ln1g_ref[...], ln1b_ref[...])
    hb = h.astype(BF16)
    f = jnp.zeros_like(h)
    for start in range(0, D_FF, FF_BLOCK):
        stop = min(start + FF_BLOCK, D_FF)
        cols = slice(start, stop)
        up_cols = slice(D_FF + start, D_FF + stop)
        g = jnp.dot(hb, wgu_ref[:, cols], preferred_element_type=F32)
        up = jnp.dot(hb, wgu_ref[:, up_cols], preferred_element_type=F32)
        act = (g * jax.nn.sigmoid(g)) * up
        f += jnp.dot(act.astype(BF16), wdown_ref[cols, :], preferred_element_type=F32)
    o_ref[...] = _layer_norm(alpha * h + f, ln2g_ref[...], ln2b_ref[...])


def _post(xp2d, xs2d, atp, ats, ut3, yt3, ln_g, ln_b, d_bglu_cols, w_glu_t, w_out, ln1_g, ln1_b,
          w_gate_up, w_down, ln2_g, ln2_b, *, alpha, tm):
    n_p, n_s = xp2d.shape[0], xs2d.shape[0]
    tp, ts = n_p // tm, n_s // tm
    assert PROJ_TILE % tm == 0 and n_p % PROJ_TILE == 0
    group = lambda i: (i // (PROJ_TILE // tm), 0, 0)
    group_spec = pl.BlockSpec((None, SSM_WIDTH * ROW_GROUP, ROW_TOKENS), group)
    prompt = lambda i: jnp.minimum(i, tp - 1)
    sample = lambda i: jnp.maximum(i - tp, 0)
    const = lambda i: (0, 0)
    resident = lambda shape: pl.BlockSpec(shape, const, pipeline_mode=pl.Buffered(1))
    vec = lambda width: pl.BlockSpec((1, width), const)
    columns = pl.BlockSpec((2, SSM_WIDTH, ROW_TOKENS), lambda i: (0, 0, 0))
    return pl.pallas_call(
        functools.partial(_post_kernel, alpha=alpha, prompt_tiles=tp),
        grid=(tp + ts,),
        in_specs=[
            pl.BlockSpec((tm, D_MODEL), lambda i: (prompt(i), 0)),
            pl.BlockSpec((tm, D_MODEL), lambda i: (sample(i), 0)),
            pl.BlockSpec((PROJ_Q, tm), lambda i: (0, prompt(i))),
            pl.BlockSpec((PROJ_Q, tm), lambda i: (0, sample(i))),
            group_spec, group_spec,
            vec(D_MODEL), vec(D_MODEL), columns,
            resident((SSM_WIDTH, SSM_WIDTH)),
            resident((D_MODEL, D_MODEL)), vec(D_MODEL), vec(D_MODEL),
            resident((D_MODEL, 2 * D_FF)), resident((D_FF, D_MODEL)),
            vec(D_MODEL), vec(D_MODEL),
        ],
        out_specs=[
            pl.BlockSpec((tm, D_MODEL), lambda i: (prompt(i), 0)),
            pl.BlockSpec((tm, D_MODEL), lambda i: (sample(i), 0)),
        ],
        out_shape=[
            jax.ShapeDtypeStruct((n_p, D_MODEL), F32),
            jax.ShapeDtypeStruct((n_s, D_MODEL), F32),
        ],
        compiler_params=pltpu.CompilerParams(
            dimension_semantics=("arbitrary",), vmem_limit_bytes=V7X_VMEM_LIMIT_BYTES),
        name="post",
    )(xp2d, xs2d, atp, ats, ut3, yt3, ln_g, ln_b, d_bglu_cols, w_glu_t, w_out, ln1_g, ln1_b,
      w_gate_up, w_down, ln2_g, ln2_b)


def kernel(x_prompt, x_sample, cache_win_k, cache_win_v, state_ssm_re, state_ssm_im,
           ln_in_g, ln_in_b, w_in, attn_sinks, ssm_lambda_re, ssm_lambda_im, ssm_log_step,
           ssm_b_re, ssm_b_im, ssm_c_re, ssm_c_im, ssm_d, w_glu, b_glu, w_out,
           ln1_g, ln1_b, w_gate_up, w_down, ln2_g, ln2_b):
    depth = w_in.shape[0]
    assert depth == 1, "single-layer step"
    bp, lp, _ = x_prompt.shape
    bs, ls, _ = x_sample.shape
    assert ls == SSM_T and bs % 2 == 0 and lp % 512 == 0 and bp <= STATE_OUT_ROWS
    win_rows = cache_win_k.shape[2]
    assert win_rows == WINDOW
    alpha = (2.0 * depth) ** 0.25
    l = 0
    row = lambda a: a.reshape(1, -1)

    lng, lnb = row(ln_in_g), row(ln_in_b)
    xp2 = x_prompt.reshape(bp * lp, D_MODEL)
    xs2 = x_sample.reshape(bs * ls, D_MODEL)
    n_p = bp * lp
    qt, vt, u3, k, v = _proj(xp2, xs2, lng, lnb, w_in[l][:, PROJ_Q:PROJ_Q + KV].astype(BF16),
                             w_in[l].T.astype(BF16))

    atp, w_gu_b, w_down_b, w_out_b = _attention_prompt(
        attn_sinks, qt, k, vt, bp, lp, n_pairs=ATTN_PAIRS, f32_weights=(w_gate_up[l], w_down[l], w_out[l]))
    ck = cache_win_k[l].reshape(bs, win_rows, PROJ_KV)
    ckt = jnp.swapaxes(ck, 1, 2)
    cvt = jnp.swapaxes(cache_win_v[l].reshape(bs, win_rows, PROJ_KV), 1, 2)
    ats = _attention_sample(attn_sinks, qt, ck, k, cvt, vt, first_token=n_p, n_pairs=ATTN_PAIRS)
    windows = _windows(k, v, ckt, cvt, bp, lp)

    table_params = _table_params(
        ssm_lambda_re[l], ssm_lambda_im[l], ssm_log_step[l],
        ssm_b_re[l], ssm_b_im[l], ssm_c_re[l], ssm_c_im[l])
    h0 = jnp.concatenate([state_ssm_re[l], state_ssm_im[l]], axis=-1)
    h0_pack = jnp.transpose(h0.reshape(bs // 2, 2, SSM_GROUPS, STATE_LANES), (2, 1, 0, 3))
    y4, hfin = _ssm(table_params, u3, h0_pack,
                    rows_per_seq=lp // ROW_TOKENS, prompt_rows=n_p // ROW_TOKENS)

    post_args = (lng, lnb, jnp.broadcast_to(jnp.stack([ssm_d[l], b_glu[l]])[:, :, None], (2, SSM_WIDTH, ROW_TOKENS)),
                 w_glu[l].T.astype(BF16),
                 w_out_b, row(ln1_g[l]), row(ln1_b[l]), w_gu_b, w_down_b, row(ln2_g[l]), row(ln2_b[l]))
    out_p, out_s = _post(xp2, xs2, atp, ats, u3, y4.reshape(u3.shape), *post_args, alpha=alpha, tm=512)

    win_k_p, win_v_p, win_k_s, win_v_s = (
        jnp.swapaxes(w, 1, 2).reshape(1, -1, win_rows, N_KV_HEADS, HEAD_DIM) for w in windows)
    sp = jnp.swapaxes(hfin[:, :bp], 0, 1)
    ss = jnp.transpose(hfin[:, STATE_OUT_ROWS:].reshape(SSM_GROUPS, 2, bs // 2, STATE_LANES),
                       (2, 1, 0, 3)).reshape(bs, SSM_GROUPS, STATE_LANES)
    return (out_p.reshape(bp, lp, D_MODEL), out_s.reshape(bs, ls, D_MODEL),
            win_k_p, win_v_p, sp[None, ..., :SSM_STATE], sp[None, ..., SSM_STATE:],
            win_k_s, win_v_s, ss[None, ..., :SSM_STATE], ss[None, ..., SSM_STATE:])
```

```python
import functools
import math

import jax
import jax.numpy as jnp
from jax import lax
from jax.experimental import pallas as pl
from jax.experimental.pallas import tpu as pltpu

F32 = jnp.float32
BF16 = jnp.bfloat16

D_MODEL = 1024
HEAD_DIM = 64
N_HEADS = 8
N_KV_HEADS = 2
Q_PER_KV = N_HEADS // N_KV_HEADS
CHUNK = 64
WINDOW = 128
WIN_CHUNKS = WINDOW // CHUNK
BAND = (WIN_CHUNKS + 1) * CHUNK
PROJ_Q = N_HEADS * HEAD_DIM
PROJ_KV = N_KV_HEADS * HEAD_DIM
SSM_WIDTH = 512
SSM_GROUP = 16
SSM_GROUPS = SSM_WIDTH // SSM_GROUP
SSM_STATE = 64
STATE_LANES = 2 * SSM_STATE
D_FF = 2816
D_IN_PROJ = PROJ_Q + 2 * PROJ_KV + SSM_WIDTH
LN_EPS = 1e-5
NEG_INF = -1e30

SSM_T = CHUNK
SSM_CW = SSM_T * SSM_GROUP
ROW_TOKENS = 2 * SSM_T
ROW_GROUP = 8
PROJ_TILE = ROW_GROUP * ROW_TOKENS
PROJ_PARTS = 2
COEF_ROWS = 16
STATE_OUT_ROWS = 8
M_BLOCK_CHANNELS = 4
SSM_GROUPS_PER_STEP = 2
FF_BLOCK = 256
V7X_VMEM_LIMIT_BYTES = 56 * 1024 * 1024


def _layer_norm(x, g, b):
    mu = jnp.mean(x, axis=-1, keepdims=True)
    xc = x - mu
    var = jnp.mean(xc * xc, axis=-1, keepdims=True)
    return xc * lax.rsqrt(var + LN_EPS) * g + b


KV = 2 * PROJ_KV
VU = PROJ_KV + SSM_WIDTH


def _proj_kernel(xp_ref, xs_ref, g_ref, b_ref, w_ref, wqt_ref, wvut_ref, qt_ref, vt_ref, ut_ref, k_ref, v_ref,
                 *, prompt_tiles):
    outs = (g_ref, b_ref, w_ref, wqt_ref, wvut_ref, qt_ref, vt_ref, ut_ref, k_ref, v_ref)

    @pl.when(pl.program_id(0) < prompt_tiles)
    def _():
        _proj_tile(xp_ref, *outs)

    @pl.when(pl.program_id(0) >= prompt_tiles)
    def _():
        _proj_tile(xs_ref, *outs)


def _proj_tile(x_ref, g_ref, b_ref, w_ref, wqt_ref, wvut_ref, qt_ref, vt_ref, ut_ref, k_ref, v_ref):
    rows_per_part = ROW_GROUP // PROJ_PARTS
    nt = (((1,), (1,)), ((), ()))
    for part in range(PROJ_PARTS):
        tok = slice(part * rows_per_part * ROW_TOKENS, (part + 1) * rows_per_part * ROW_TOKENS)
        xb = _layer_norm(x_ref[tok, :], g_ref[...], b_ref[...]).astype(BF16)
        p = jnp.dot(xb, w_ref[...], preferred_element_type=F32)
        k_ref[tok, :] = p[:, :PROJ_KV]
        v_ref[tok, :] = p[:, PROJ_KV:]
        qt = lax.dot_general(wqt_ref[...], xb, nt, preferred_element_type=F32)
        vut = lax.dot_general(wvut_ref[...], xb, nt, preferred_element_type=F32)
        qt_ref[:, tok] = (qt * (HEAD_DIM ** -0.5)).astype(BF16)
        vt_ref[:, tok] = vut[:PROJ_KV].astype(BF16)
        for j in range(rows_per_part):
            ut_ref[pl.ds(part * rows_per_part + j, SSM_WIDTH, stride=ROW_GROUP), :] = (
                vut[PROJ_KV:, j * ROW_TOKENS:(j + 1) * ROW_TOKENS])


def _proj(xp2d, xs2d, ln_g, ln_b, w_kv, w_in_t):
    assert D_IN_PROJ == 2 * VU
    tm = PROJ_TILE
    tp, ts = xp2d.shape[0] // tm, xs2d.shape[0] // tm
    n = (tp + ts) * tm
    const = lambda i: (0, 0)
    row = lambda i: (i, 0)
    col = lambda i: (0, i)
    return pl.pallas_call(
        functools.partial(_proj_kernel, prompt_tiles=tp),
        grid=(tp + ts,),
        in_specs=[
            pl.BlockSpec((tm, D_MODEL), lambda i: (jnp.minimum(i, tp - 1), 0)),
            pl.BlockSpec((tm, D_MODEL), lambda i: (jnp.maximum(i - tp, 0), 0)),
            pl.BlockSpec((1, D_MODEL), const),
            pl.BlockSpec((1, D_MODEL), const),
            pl.BlockSpec((D_MODEL, KV), const),
            pl.BlockSpec((PROJ_Q, D_MODEL), const),
            pl.BlockSpec((VU, D_MODEL), lambda i: (1, 0)),
        ],
        out_specs=[
            pl.BlockSpec((PROJ_Q, tm), col),
            pl.BlockSpec((PROJ_KV, tm), col),
            pl.BlockSpec((None, SSM_WIDTH * ROW_GROUP, ROW_TOKENS), lambda i: (i, 0, 0)),
            pl.BlockSpec((tm, PROJ_KV), row),
            pl.BlockSpec((tm, PROJ_KV), row),
        ],
        out_shape=[
            jax.ShapeDtypeStruct((PROJ_Q, n), BF16),
            jax.ShapeDtypeStruct((PROJ_KV, n), BF16),
            jax.ShapeDtypeStruct((n // tm, SSM_WIDTH * ROW_GROUP, ROW_TOKENS), F32),
            jax.ShapeDtypeStruct((n, PROJ_KV), F32),
            jax.ShapeDtypeStruct((n, PROJ_KV), F32),
        ],
        compiler_params=pltpu.CompilerParams(
            dimension_semantics=("arbitrary",), vmem_limit_bytes=V7X_VMEM_LIMIT_BYTES),
        name="proj",
    )(xp2d, xs2d, ln_g, ln_b, w_kv, w_in_t, w_in_t)


PAIR = 2 * CHUNK
HEAD_LANES = Q_PER_KV * PAIR
ONES_ROWS = 16
ATTN_PAIRS = 16
BF16_TILE_ROWS = 16


def _scores(unit, qt_ref):
    kwin, _, lanes, _, _, h = unit
    base = h * Q_PER_KV * HEAD_DIM
    qrow = jnp.concatenate(
        [qt_ref[base + g * HEAD_DIM:base + (g + 1) * HEAD_DIM, lanes] for g in range(Q_PER_KV)],
        axis=1)
    zero = jnp.zeros_like(qrow)
    qstack = jnp.concatenate([qrow, zero] if h == 0 else [zero, qrow], axis=0)
    return jnp.dot(kwin, qstack, preferred_element_type=F32)


def _sink_rows(sinks_ref):
    tile = lax.broadcasted_iota(jnp.int32, (1, HEAD_LANES), 1) // PAIR
    rows = []
    for h in range(N_KV_HEADS):
        row = jnp.full((1, HEAD_LANES), sinks_ref[0, h * Q_PER_KV], F32)
        for g in range(1, Q_PER_KV):
            row = jnp.where(tile == g, sinks_ref[0, h * Q_PER_KV + g], row)
        rows.append(row)
    return rows


def _finish(unit, s, sinks, o_ref):
    _, vtwin, lanes, valid, masked_rows, h = unit
    nk = s.shape[0]
    base = h * Q_PER_KV * HEAD_DIM
    pieces, done = [], 0
    for start, stop in masked_rows:
        if start > done:
            pieces.append(s[done:start])
        pieces.append(jnp.where(valid[start:stop], s[start:stop], NEG_INF))
        done = stop
    if done < nk:
        pieces.append(s[done:])
    s = jnp.concatenate(pieces, axis=0)
    sink = sinks[h]
    m = jnp.maximum(jnp.max(s, axis=0, keepdims=True), sink)
    p = jnp.exp(s - m).astype(BF16)
    v_ones = jnp.concatenate(
        [vtwin[h * HEAD_DIM:(h + 1) * HEAD_DIM, :], jnp.ones((ONES_ROWS, nk), BF16)], axis=0)
    ov = jnp.dot(v_ones, p, preferred_element_type=F32)
    den = ov[HEAD_DIM:HEAD_DIM + 1, :] + jnp.exp(sink - m)
    o = ov[:HEAD_DIM, :] * (1.0 / den)
    for g in range(Q_PER_KV):
        o_ref[base + g * HEAD_DIM:base + (g + 1) * HEAD_DIM, lanes] = (
            o[:, g * PAIR:(g + 1) * PAIR].astype(BF16))


def _attend_units(units, qt_ref, sinks_ref, o_ref):
    sinks = _sink_rows(sinks_ref)
    s_next = _scores(units[0], qt_ref)
    for i, unit in enumerate(units):
        s = s_next
        if i + 1 < len(units):
            s_next = _scores(units[i + 1], qt_ref)
        _finish(unit, s, sinks, o_ref)


def _attn_prompt_kernel(sinks_ref, qt_ref, kp_ref, kc_ref, vtp_ref, vtc_ref, *refs, n_pairs):
    n_cast = (len(refs) - 1) // 2
    o_ref = refs[n_cast]
    for w_ref, wb_ref in zip(refs[:n_cast], refs[n_cast + 1:]):
        wb_ref[...] = w_ref[...].astype(BF16)
    tq = n_pairs * PAIR
    nk = WINDOW + PAIR
    kk = jnp.concatenate([kp_ref[...], kc_ref[...]], axis=0).astype(BF16)
    vt = jnp.concatenate([vtp_ref[...], vtc_ref[...]], axis=1)
    r = lax.broadcasted_iota(jnp.int32, (nk, HEAD_LANES), 0)
    first_chunk = (lax.broadcasted_iota(jnp.int32, (nk, HEAD_LANES), 1) & (PAIR - 1)) < CHUNK
    lo = jnp.where(first_chunk, 0, CHUNK)
    hi = jnp.where(first_chunk, BAND, nk)
    first_pos = pl.program_id(1) * tq - WINDOW
    units = []
    for pp in range(n_pairs):
        lo_pp = jnp.maximum(lo, -first_pos) if pp == 0 else lo
        valid = (r >= lo_pp) & (r < hi)
        masked_rows = ((0, nk),) if pp == 0 else ((0, CHUNK), (nk - CHUNK, nk))
        for h in range(N_KV_HEADS):
            units.append((kk[pp * PAIR:pp * PAIR + nk], vt[:, pp * PAIR:pp * PAIR + nk],
                          slice(pp * PAIR, (pp + 1) * PAIR), valid, masked_rows, h))
    _attend_units(units, qt_ref, sinks_ref, o_ref)


def _attn_sample_kernel(sinks_ref, qt_ref, ck_ref, kn_ref, cvt_ref, vtn_ref, o_ref, *, n_pairs):
    nk = 2 * WINDOW + PAIR
    r = lax.broadcasted_iota(jnp.int32, (nk, HEAD_LANES), 0)
    query_seq = (lax.broadcasted_iota(jnp.int32, (nk, HEAD_LANES), 1) & (PAIR - 1)) >> 6
    key_seq = jnp.where(r < 2 * WINDOW, r >> 7, (r - 2 * WINDOW) >> 6)
    valid = query_seq == key_seq
    units = []
    for pp in range(n_pairs):
        lanes = slice(pp * PAIR, (pp + 1) * PAIR)
        kwin = jnp.concatenate([ck_ref[2 * pp], ck_ref[2 * pp + 1], kn_ref[lanes, :]],
                               axis=0).astype(BF16)
        vtwin = jnp.concatenate([cvt_ref[2 * pp].astype(BF16), cvt_ref[2 * pp + 1].astype(BF16),
                                 vtn_ref[:, lanes]], axis=1)
        units.extend((kwin, vtwin, lanes, valid, ((0, nk),), h) for h in range(N_KV_HEADS))
    _attend_units(units, qt_ref, sinks_ref, o_ref)


_SINK_SPEC = pl.BlockSpec(memory_space=pltpu.SMEM)


def _attention_prompt(sinks, qt, k, vt, bsz, seq, n_pairs, f32_weights):
    tq = n_pairs * PAIR
    nt = seq // tq
    steps = bsz * nt
    wpt = tq // WINDOW
    cur_c = lambda b, i: (0, b * nt + i)
    cur_r = lambda b, i: (b * nt + i, 0)
    prev = lambda b, i: jnp.maximum((b * nt + i) * wpt - 1, 0)
    assert all(w.shape[0] % (steps * BF16_TILE_ROWS) == 0 for w in f32_weights)
    cast_specs = [pl.BlockSpec((w.shape[0] // steps, w.shape[1]), cur_r) for w in f32_weights]
    return pl.pallas_call(
        functools.partial(_attn_prompt_kernel, n_pairs=n_pairs),
        grid=(bsz, nt),
        in_specs=[
            _SINK_SPEC,
            pl.BlockSpec((PROJ_Q, tq), cur_c),
            pl.BlockSpec((WINDOW, PROJ_KV), lambda b, i: (prev(b, i), 0)),
            pl.BlockSpec((tq, PROJ_KV), cur_r),
            pl.BlockSpec((PROJ_KV, WINDOW), lambda b, i: (0, prev(b, i))),
            pl.BlockSpec((PROJ_KV, tq), cur_c),
            *cast_specs,
        ],
        out_specs=[pl.BlockSpec((PROJ_Q, tq), cur_c), *cast_specs],
        out_shape=[jax.ShapeDtypeStruct((PROJ_Q, bsz * seq), BF16),
                   *[jax.ShapeDtypeStruct(w.shape, BF16) for w in f32_weights]],
        compiler_params=pltpu.CompilerParams(
            dimension_semantics=("arbitrary", "arbitrary"), vmem_limit_bytes=V7X_VMEM_LIMIT_BYTES),
        name="attn_prompt",
    )(sinks, qt, k, k, vt, vt, *f32_weights)


def _attention_sample(sinks, qt, cache_k, k, cache_vt, vt, first_token, n_pairs):
    n = cache_k.shape[0] * CHUNK
    tq = n_pairs * PAIR
    assert first_token % tq == 0
    off = first_token // tq
    return pl.pallas_call(
        functools.partial(_attn_sample_kernel, n_pairs=n_pairs),
        grid=(n // tq,),
        in_specs=[
            _SINK_SPEC,
            pl.BlockSpec((PROJ_Q, tq), lambda i: (0, i + off)),
            pl.BlockSpec((2 * n_pairs, WINDOW, PROJ_KV), lambda i: (i, 0, 0)),
            pl.BlockSpec((tq, PROJ_KV), lambda i: (i + off, 0)),
            pl.BlockSpec((2 * n_pairs, PROJ_KV, WINDOW), lambda i: (i, 0, 0)),
            pl.BlockSpec((PROJ_KV, tq), lambda i: (0, i + off)),
        ],
        out_specs=pl.BlockSpec((PROJ_Q, tq), lambda i: (0, i)),
        out_shape=jax.ShapeDtypeStruct((PROJ_Q, n), BF16),
        compiler_params=pltpu.CompilerParams(dimension_semantics=("arbitrary",)),
        name="attn_sample",
    )(sinks, qt, cache_k, k, cache_vt, vt)


def _windows_kernel(kp_ref, vp_ref, kn_ref, vn_ref, ckt_ref, cvt_ref, wkp_ref, wvp_ref, wks_ref, wvs_ref):
    first_half = lax.broadcasted_iota(jnp.int32, (PROJ_KV, WINDOW), 1) < CHUNK
    for new_p, new_s, cache_t, win_p, win_s in ((kp_ref, kn_ref, ckt_ref, wkp_ref, wks_ref),
                                                (vp_ref, vn_ref, cvt_ref, wvp_ref, wvs_ref)):
        win_p[0] = new_p[...].T
        for pair in range(cache_t.shape[0] // 2):
            new_t = new_s[pair * PAIR:(pair + 1) * PAIR, :].T
            swapped = pltpu.roll(new_t, CHUNK, 1)
            for j, new in enumerate((swapped, new_t)):
                old = pltpu.roll(cache_t[2 * pair + j], CHUNK, 1)
                win_s[2 * pair + j] = jnp.where(first_half, old, new)


def _windows(k, v, cache_kt, cache_vt, bp, seq):
    bs = cache_kt.shape[0]
    assert bs % (2 * bp) == 0 and seq % WINDOW == 0
    share = bs // bp
    assert (bp * seq) % (share * CHUNK) == 0
    first_share = bp * seq // (share * CHUNK)
    last_window = pl.BlockSpec((WINDOW, PROJ_KV), lambda b: ((b + 1) * (seq // WINDOW) - 1, 0))
    new_rows = pl.BlockSpec((share * CHUNK, PROJ_KV), lambda b: (first_share + b, 0))
    one = pl.BlockSpec((1, PROJ_KV, WINDOW), lambda b: (b, 0, 0))
    many = pl.BlockSpec((share, PROJ_KV, WINDOW), lambda b: (b, 0, 0))
    return pl.pallas_call(
        _windows_kernel,
        grid=(bp,),
        in_specs=[last_window, last_window, new_rows, new_rows, many, many],
        out_specs=[one, one, many, many],
        out_shape=[jax.ShapeDtypeStruct((bp, PROJ_KV, WINDOW), F32)] * 2
        + [jax.ShapeDtypeStruct((bs, PROJ_KV, WINDOW), F32)] * 2,
        compiler_params=pltpu.CompilerParams(dimension_semantics=("arbitrary",)),
        name="windows",
    )(k, v, k, v, cache_kt, cache_vt)


def _power_table(tau, nbits, a_re, a_im):
    rows = tau.shape[0]
    w_re = jnp.ones((rows, STATE_LANES), F32)
    w_im = jnp.zeros((rows, STATE_LANES), F32)
    p_re, p_im = a_re, a_im
    for k in range(nbits):
        bit = ((tau >> k) & 1) == 1
        f_re = jnp.where(bit, p_re, 1.0)
        f_im = jnp.where(bit, p_im, 0.0)
        w_re, w_im = w_re * f_re - w_im * f_im, w_re * f_im + w_im * f_re
        p_re, p_im = p_re * p_re - p_im * p_im, 2.0 * p_re * p_im
    return w_re, w_im


def _power_rows(a_re, a_im, descending):
    t = lax.broadcasted_iota(jnp.int32, (ROW_GROUP, 1), 0)
    w_re, w_im = _power_table(ROW_GROUP - 1 - t if descending else t, 3, a_re, a_im)
    p_re, p_im = a_re, a_im
    for _ in range(3):
        p_re, p_im = p_re * p_re - p_im * p_im, 2.0 * p_re * p_im
    while w_re.shape[0] < SSM_T:
        n_re, n_im = w_re * p_re - w_im * p_im, w_re * p_im + w_im * p_re
        w_re = jnp.concatenate([n_re, w_re] if descending else [w_re, n_re], axis=0)
        w_im = jnp.concatenate([n_im, w_im] if descending else [w_im, n_im], axis=0)
        p_re, p_im = p_re * p_re - p_im * p_im, 2.0 * p_re * p_im
    return w_re, w_im


def _build_tables(par_ref, p_ref, qt_ref, coef_ref):
    lo = lax.broadcasted_iota(jnp.int32, (1, STATE_LANES), 1) < SSM_STATE
    mat = lambda i: par_ref[PAR_MATS + i * SSM_GROUP:PAR_MATS + (i + 1) * SSM_GROUP, :]
    lr, li = par_ref[0:1, :], par_ref[1:2, :]
    dt = jnp.exp(par_ref[2:3, :])
    mag = jnp.exp(lr * dt)
    a_re, a_im = mag * jnp.cos(li * dt), mag * jnp.sin(li * dt)
    nr, ni = a_re - 1.0, a_im
    den = lr * lr + li * li
    f_re, f_im = (nr * lr + ni * li) / den, (ni * lr - nr * li) / den
    b_re, b_im = mat(0), mat(1)
    bb_re = f_re * b_re - f_im * b_im
    bb_im = f_re * b_im + f_im * b_re
    c_re, c_im = mat(2), mat(3)

    w_re, w_im = _power_rows(a_re, a_im, descending=False)
    w1_re, w1_im = w_re * a_re - w_im * a_im, w_re * a_im + w_im * a_re
    wr_re, wr_im = _power_rows(a_re, a_im, descending=True)

    def outer(c, w):
        return (c[:, None, :] * w[None, :, :]).reshape(SSM_CW, STATE_LANES)

    cw_mix = (outer(c_re, jnp.where(lo, w_re, w_im)) + outer(c_im, jnp.where(lo, -w_im, w_re)))
    bb_mix = jnp.where(lo, bb_re, -bb_im)
    strip = lax.dot_general(bb_mix, cw_mix, (((1,), (1,)), ((), ())),
                            precision=lax.Precision.HIGHEST,
                            preferred_element_type=F32)

    x_a, y_a = jnp.where(lo, bb_re, bb_im), jnp.where(lo, -bb_im, bb_re)
    p_ref[...] = (outer(x_a, wr_re) + outer(y_a, wr_im)).astype(BF16)

    qt_ref[...] = (outer(c_re, jnp.where(lo, w1_re, -w1_im))
                   + outer(c_im, jnp.where(lo, -w1_im, -w1_re))).astype(BF16)

    t_re, t_im = a_re, a_im
    for _ in range(6):
        t_re, t_im = t_re * t_re - t_im * t_im, 2.0 * t_re * t_im
    for k in range(COEF_ROWS // 2):
        coef_ref[2 * k:2 * k + 1, :] = t_re
        coef_ref[2 * k + 1:2 * k + 2, :] = jnp.where(lo, -t_im, t_im)
        t_re, t_im = t_re * t_re - t_im * t_im, 2.0 * t_re * t_im
    return strip


def _toeplitz_columns(strip, m_ref, cols):
    width = cols.stop - cols.start
    s_idx = lax.broadcasted_iota(jnp.int32, (SSM_T, width), 0)
    t_idx = lax.broadcasted_iota(jnp.int32, (SSM_T, width), 1) & (SSM_T - 1)
    causal = t_idx >= s_idx
    for c in range(SSM_GROUP):
        rows = jnp.broadcast_to(strip[c:c + 1, cols], (SSM_T, width))
        shifted = pltpu.roll(rows, 0, 1, stride=1, stride_axis=0)
        m_ref[c * SSM_T:(c + 1) * SSM_T, cols] = jnp.where(causal, shifted, 0.0).astype(BF16)


PAR_MATS = 8
PAR_ROWS = PAR_MATS + 4 * SSM_GROUP


def _table_params(lam_re, lam_im, log_step, b_re, b_im, c_re, c_im):
    step = jnp.broadcast_to(log_step[:, None, None], (SSM_GROUPS, 1, SSM_STATE))
    filler = jnp.zeros((SSM_GROUPS, PAR_MATS - 3, SSM_STATE), F32)
    pack = jnp.concatenate([lam_re[:, None, :], lam_im[:, None, :], step, filler,
                            jnp.swapaxes(b_re, 1, 2), jnp.swapaxes(b_im, 1, 2), c_re, c_im], axis=1)
    return jnp.concatenate([pack, pack], axis=-1)


def _cmul(a1, a2, h, hs):
    return a1 * h + a2 * hs, a1 * hs - a2 * h


def _ssm_kernel(par_ref, u_ref, h0_ref, y_ref, hfin_ref, m_ref, p_ref, qt_ref, coef_ref,
                *, rows_per_seq, prompt_rows):
    for gi in range(SSM_GROUPS_PER_STEP):
        _ssm_group(par_ref.at[gi], u_ref, gi * SSM_GROUP, h0_ref.at[gi], y_ref, hfin_ref.at[gi],
                   m_ref.at[gi], p_ref.at[gi], qt_ref.at[gi], coef_ref.at[gi],
                   rows_per_seq=rows_per_seq, prompt_rows=prompt_rows)


def _ssm_group(par_ref, u_ref, c0, h0_ref, y_ref, hfin_ref, m_ref, p_ref, qt_ref, coef_ref,
               *, rows_per_seq, prompt_rows):
    strip = _build_tables(par_ref, p_ref, qt_ref, coef_ref)
    lo = lax.broadcasted_iota(jnp.int32, (1, ROW_TOKENS), 1) < SSM_T

    def chunk_rows(ref):
        even, odd = [], []
        rows = ref.shape[0] * ROW_GROUP
        for k in range(SSM_GROUP // 2):
            a = ref[:, c0 + 2 * k].reshape(rows, ROW_TOKENS)
            b = ref[:, c0 + 2 * k + 1].reshape(rows, ROW_TOKENS)
            even.append(jnp.where(lo, a, pltpu.roll(b, SSM_T, 1)))
            odd.append(jnp.where(lo, pltpu.roll(a, SSM_T, 1), b))
        return jnp.concatenate(even, axis=1), jnp.concatenate(odd, axis=1)

    def store_rows(y_even, y_odd, ref, first_channel):
        for k in range(y_even.shape[1] // ROW_TOKENS):
            te = y_even[:, k * ROW_TOKENS:(k + 1) * ROW_TOKENS]
            to = y_odd[:, k * ROW_TOKENS:(k + 1) * ROW_TOKENS]
            tiles = (ref.shape[0], ROW_GROUP, ROW_TOKENS)
            c = c0 + first_channel + 2 * k
            ref[:, c] = jnp.where(lo, te, pltpu.roll(to, SSM_T, 1)).reshape(tiles)
            ref[:, c + 1] = jnp.where(lo, pltpu.roll(te, SSM_T, 1), to).reshape(tiles)

    even, odd = chunk_rows(u_ref)
    rp, rs = prompt_rows, even.shape[0] - prompt_rows
    u = jnp.concatenate([even[:rp], odd[:rp], even[rp:], odd[rp:]], axis=0).astype(BF16)
    s1 = jnp.dot(u, p_ref[...], preferred_element_type=F32)
    s2 = pltpu.roll(s1, SSM_STATE, 1)
    a1, a2 = coef_ref[0:1, :], coef_ref[1:2, :]

    e1, e2, o1, o2 = s1[:rp], s2[:rp], s1[rp:2 * rp], s2[rp:2 * rp]
    x1, x2 = _cmul(a1, a2, e1, e2)
    x1, x2 = x1 + o1, x2 + o2
    pos = lax.broadcasted_iota(jnp.int32, (rp, STATE_LANES), 0) & (rows_per_seq - 1)
    for k in range(rows_per_seq.bit_length() - 1):
        d = 1 << k
        b1, b2 = coef_ref[2 + 2 * k:3 + 2 * k, :], coef_ref[3 + 2 * k:4 + 2 * k, :]
        sh1 = jnp.where(pos >= d, pltpu.roll(x1, d, 0), 0.0)
        sh2 = jnp.where(pos >= d, pltpu.roll(x2, d, 0), 0.0)
        y1, y2 = _cmul(b1, b2, sh1, sh2)
        x1, x2 = x1 + y1, x2 + y2
    g1 = jnp.where(pos >= 1, pltpu.roll(x1, 1, 0), 0.0)
    g2 = jnp.where(pos >= 1, pltpu.roll(x2, 1, 0), 0.0)
    ho1 = _cmul(a1, a2, g1, g2)[0] + e1
    hfin_ref[0:STATE_OUT_ROWS, :] = jnp.zeros((STATE_OUT_ROWS, STATE_LANES), F32)
    for b in range(rp // rows_per_seq):
        last = (b + 1) * rows_per_seq - 1
        hfin_ref[b:b + 1, :] = x1[last:last + 1, :]

    h0e, h0o = h0_ref[0], h0_ref[1]
    swap = lambda h: pltpu.roll(h, SSM_STATE, 1)
    hfin_ref[STATE_OUT_ROWS:STATE_OUT_ROWS + rs, :] = a1 * h0e + a2 * swap(h0e) + s1[2 * rp:2 * rp + rs]
    hfin_ref[STATE_OUT_ROWS + rs:, :] = a1 * h0o + a2 * swap(h0o) + s1[2 * rp + rs:]

    hprev = jnp.concatenate([g1, ho1, h0e, h0o], axis=0).astype(BF16)
    blocks = [slice(c * SSM_T, (c + M_BLOCK_CHANNELS) * SSM_T) for c in range(0, SSM_GROUP, M_BLOCK_CHANNELS)]
    _toeplitz_columns(strip, m_ref, blocks[0])
    for i, cols in enumerate(blocks):
        y = lax.dot_general(hprev, qt_ref[cols, :], (((1,), (1,)), ((), ())), preferred_element_type=F32)
        y += jnp.dot(u, m_ref[:, cols], preferred_element_type=F32)
        if i + 1 < len(blocks):
            _toeplitz_columns(strip, m_ref, blocks[i + 1])
        store_rows(jnp.concatenate([y[:rp], y[2 * rp:2 * rp + rs]], axis=0),
                   jnp.concatenate([y[rp:2 * rp], y[2 * rp + rs:]], axis=0), y_ref, i * M_BLOCK_CHANNELS)


def _ssm(table_params, u3, h0_pack, *, rows_per_seq, prompt_rows):
    groups = u3.shape[0]
    rs = groups * ROW_GROUP - prompt_rows
    u4 = u3.reshape(groups, SSM_WIDTH, ROW_GROUP, ROW_TOKENS)
    assert rows_per_seq & (rows_per_seq - 1) == 0 and 2 * rows_per_seq.bit_length() <= COEF_ROWS
    gps = SSM_GROUPS_PER_STEP
    g3 = lambda g: (g, 0, 0)
    u_spec = pl.BlockSpec((groups, gps * SSM_GROUP, ROW_GROUP, ROW_TOKENS), lambda g: (0, g, 0, 0))
    fin_rows = STATE_OUT_ROWS + 2 * rs
    return pl.pallas_call(
        functools.partial(_ssm_kernel, rows_per_seq=rows_per_seq, prompt_rows=prompt_rows),
        grid=(SSM_GROUPS // gps,),
        in_specs=[
            pl.BlockSpec((gps, PAR_ROWS, STATE_LANES), g3),
            u_spec,
            pl.BlockSpec((gps, 2, rs, STATE_LANES), lambda g: (g, 0, 0, 0)),
        ],
        out_specs=[
            u_spec,
            pl.BlockSpec((gps, fin_rows, STATE_LANES), g3),
        ],
        out_shape=[
            jax.ShapeDtypeStruct(u4.shape, F32),
            jax.ShapeDtypeStruct((SSM_GROUPS, fin_rows, STATE_LANES), F32),
        ],
        scratch_shapes=[
            pltpu.VMEM((gps, SSM_CW, SSM_CW), BF16),
            pltpu.VMEM((gps, SSM_CW, STATE_LANES), BF16),
            pltpu.VMEM((gps, SSM_CW, STATE_LANES), BF16),
            pltpu.VMEM((gps, COEF_ROWS, STATE_LANES), F32),
        ],
        compiler_params=pltpu.CompilerParams(
            dimension_semantics=("arbitrary",), vmem_limit_bytes=V7X_VMEM_LIMIT_BYTES),
        name="ssm",
    )(table_params, u4, h0_pack)


def _post_kernel(xp_ref, xs_ref, atp_ref, ats_ref, ut_ref, yt_ref, *refs, alpha, prompt_tiles):
    weights, (op_ref, os_ref) = refs[:-2], refs[-2:]

    @pl.when(pl.program_id(0) < prompt_tiles)
    def _():
        _post_tile(xp_ref, atp_ref, ut_ref, yt_ref, *weights, op_ref, alpha=alpha)

    @pl.when(pl.program_id(0) >= prompt_tiles)
    def _():
        _post_tile(xs_ref, ats_ref, ut_ref, yt_ref, *weights, os_ref, alpha=alpha)


def _post_tile(x_ref, at_ref, ut_ref, yt_ref, lng_ref, lnb_ref, cols_ref, wglut_ref,
               wout_ref, ln1g_ref, ln1b_ref, wgu_ref, wdown_ref, ln2g_ref, ln2b_ref, o_ref,
               *, alpha):
    xn = _layer_norm(x_ref[...], lng_ref[...], lnb_ref[...])
    n_rows = x_ref.shape[0] // ROW_TOKENS
    first_row = (pl.program_id(0) % (ROW_GROUP // n_rows)) * n_rows

    def feature_major(ref):
        return jnp.concatenate(
            [ref[pl.ds(first_row + j, SSM_WIDTH, stride=ROW_GROUP), :] for j in range(n_rows)], axis=1)

    widen = lambda col: jnp.tile(col, (1, n_rows))
    ys = feature_major(yt_ref) + widen(cols_ref[0]) * feature_major(ut_ref)
    gl = 0.5 * ys * (1.0 + lax.erf(ys * math.sqrt(0.5)))
    z = jnp.dot(wglut_ref[...], gl.astype(BF16), preferred_element_type=F32) + widen(cols_ref[1])
    s = gl * jax.nn.sigmoid(z)
    tn = (((0,), (0,)), ((), ()))
    mix = lax.dot_general(at_ref[...], wout_ref[:PROJ_Q, :], tn, preferred_element_type=F32)
    mix += lax.dot_general(s.astype(BF16), wout_ref[PROJ_Q:, :], tn, preferred_element_type=F32)
    h = _layer_norm(alpha * xn + mix, ln1g_ref[...], ln1b_ref[...])
    hb = h.astype(BF16)
    f = None
    for start in range(0, D_FF, FF_BLOCK):
        stop = min(start + FF_BLOCK, D_FF)
        cols = slice(start, stop)
        up_cols = slice(D_FF + start, D_FF + stop)
        g = jnp.dot(hb, wgu_ref[:, cols], preferred_element_type=F32)
        up = jnp.dot(hb, wgu_ref[:, up_cols], preferred_element_type=F32)
        act = (g * jax.nn.sigmoid(g)) * up
        down = jnp.dot(act.astype(BF16), wdown_ref[cols, :], preferred_element_type=F32)
        f = down if f is None else f + down
    o_ref[...] = _layer_norm(alpha * h + f, ln2g_ref[...], ln2b_ref[...])


def _post(xp2d, xs2d, atp, ats, ut3, yt3, ln_g, ln_b, d_bglu_cols, w_glu_t, w_out, ln1_g, ln1_b,
          w_gate_up, w_down, ln2_g, ln2_b, *, alpha, tm):
    n_p, n_s = xp2d.shape[0], xs2d.shape[0]
    tp, ts = n_p // tm, n_s // tm
    assert PROJ_TILE % tm == 0 and n_p % PROJ_TILE == 0
    group = lambda i: (i // (PROJ_TILE // tm), 0, 0)
    group_spec = pl.BlockSpec((None, SSM_WIDTH * ROW_GROUP, ROW_TOKENS), group)
    prompt = lambda i: jnp.minimum(i, tp - 1)
    sample = lambda i: jnp.maximum(i - tp, 0)
    const = lambda i: (0, 0)
    resident = lambda shape: pl.BlockSpec(shape, const, pipeline_mode=pl.Buffered(1))
    vec = lambda width: pl.BlockSpec((1, width), const)
    columns = pl.BlockSpec((2, SSM_WIDTH, ROW_TOKENS), lambda i: (0, 0, 0))
    return pl.pallas_call(
        functools.partial(_post_kernel, alpha=alpha, prompt_tiles=tp),
        grid=(tp + ts,),
        in_specs=[
            pl.BlockSpec((tm, D_MODEL), lambda i: (prompt(i), 0)),
            pl.BlockSpec((tm, D_MODEL), lambda i: (sample(i), 0)),
            pl.BlockSpec((PROJ_Q, tm), lambda i: (0, prompt(i))),
            pl.BlockSpec((PROJ_Q, tm), lambda i: (0, sample(i))),
            group_spec, group_spec,
            vec(D_MODEL), vec(D_MODEL), columns,
            resident((SSM_WIDTH, SSM_WIDTH)),
            resident((D_MODEL, D_MODEL)), vec(D_MODEL), vec(D_MODEL),
            resident((D_MODEL, 2 * D_FF)), resident((D_FF, D_MODEL)),
            vec(D_MODEL), vec(D_MODEL),
        ],
        out_specs=[
            pl.BlockSpec((tm, D_MODEL), lambda i: (prompt(i), 0)),
            pl.BlockSpec((tm, D_MODEL), lambda i: (sample(i), 0)),
        ],
        out_shape=[
            jax.ShapeDtypeStruct((n_p, D_MODEL), F32),
            jax.ShapeDtypeStruct((n_s, D_MODEL), F32),
        ],
        compiler_params=pltpu.CompilerParams(
            dimension_semantics=("arbitrary",), vmem_limit_bytes=V7X_VMEM_LIMIT_BYTES),
        name="post",
    )(xp2d, xs2d, atp, ats, ut3, yt3, ln_g, ln_b, d_bglu_cols, w_glu_t, w_out, ln1_g, ln1_b,
      w_gate_up, w_down, ln2_g, ln2_b)


def kernel(x_prompt, x_sample, cache_win_k, cache_win_v, state_ssm_re, state_ssm_im,
           ln_in_g, ln_in_b, w_in, attn_sinks, ssm_lambda_re, ssm_lambda_im, ssm_log_step,
           ssm_b_re, ssm_b_im, ssm_c_re, ssm_c_im, ssm_d, w_glu, b_glu, w_out,
           ln1_g, ln1_b, w_gate_up, w_down, ln2_g, ln2_b):
    depth = w_in.shape[0]
    assert depth == 1, "single-layer step"
    bp, lp, _ = x_prompt.shape
    bs, ls, _ = x_sample.shape
    assert ls == SSM_T and bs % 2 == 0 and lp % 512 == 0 and bp <= STATE_OUT_ROWS
    win_rows = cache_win_k.shape[2]
    assert win_rows == WINDOW
    alpha = (2.0 * depth) ** 0.25
    l = 0
    row = lambda a: a.reshape(1, -1)

    lng, lnb = row(ln_in_g), row(ln_in_b)
    xp2 = x_prompt.reshape(bp * lp, D_MODEL)
    xs2 = x_sample.reshape(bs * ls, D_MODEL)
    n_p = bp * lp
    qt, vt, u3, k, v = _proj(xp2, xs2, lng, lnb, w_in[l][:, PROJ_Q:PROJ_Q + KV].astype(BF16),
                             w_in[l].T.astype(BF16))

    atp, w_gu_b, w_down_b, w_out_b = _attention_prompt(
        attn_sinks, qt, k, vt, bp, lp, n_pairs=ATTN_PAIRS, f32_weights=(w_gate_up[l], w_down[l], w_out[l]))
    ck = cache_win_k[l].reshape(bs, win_rows, PROJ_KV)
    ckt = jnp.swapaxes(ck, 1, 2)
    cvt = jnp.swapaxes(cache_win_v[l].reshape(bs, win_rows, PROJ_KV), 1, 2)
    ats = _attention_sample(attn_sinks, qt, ck, k, cvt, vt, first_token=n_p, n_pairs=ATTN_PAIRS)
    windows = _windows(k, v, ckt, cvt, bp, lp)

    table_params = _table_params(
        ssm_lambda_re[l], ssm_lambda_im[l], ssm_log_step[l],
        ssm_b_re[l], ssm_b_im[l], ssm_c_re[l], ssm_c_im[l])
    h0 = jnp.concatenate([state_ssm_re[l], state_ssm_im[l]], axis=-1)
    h0_pack = jnp.transpose(h0.reshape(bs // 2, 2, SSM_GROUPS, STATE_LANES), (2, 1, 0, 3))
    y4, hfin = _ssm(table_params, u3, h0_pack,
                    rows_per_seq=lp // ROW_TOKENS, prompt_rows=n_p // ROW_TOKENS)

    post_args = (lng, lnb, jnp.broadcast_to(jnp.stack([ssm_d[l], b_glu[l]])[:, :, None], (2, SSM_WIDTH, ROW_TOKENS)),
                 w_glu[l].T.astype(BF16),
                 w_out_b, row(ln1_g[l]), row(ln1_b[l]), w_gu_b, w_down_b, row(ln2_g[l]), row(ln2_b[l]))
    out_p, out_s = _post(xp2, xs2, atp, ats, u3, y4.reshape(u3.shape), *post_args, alpha=alpha, tm=512)

    win_k_p, win_v_p, win_k_s, win_v_s = (
        jnp.swapaxes(w, 1, 2).reshape(1, -1, win_rows, N_KV_HEADS, HEAD_DIM) for w in windows)
    sp = jnp.swapaxes(hfin[:, :bp], 0, 1)
    ss = jnp.transpose(hfin[:, STATE_OUT_ROWS:].reshape(SSM_GROUPS, 2, bs // 2, STATE_LANES),
                       (2, 1, 0, 3)).reshape(bs, SSM_GROUPS, STATE_LANES)
    return (out_p.reshape(bp, lp, D_MODEL), out_s.reshape(bs, ls, D_MODEL),
            win_k_p, win_v_p, sp[None, ..., :SSM_STATE], sp[None, ..., SSM_STATE:],
            win_k_s, win_v_s, ss[None, ..., :SSM_STATE], ss[None, ..., SSM_STATE:])
```
